```python
import math
import jax, jax.numpy as jnp
from jax import lax
import numpy as np

D_MODEL = 1024
BATCH = 16
SEQ = 2048
DEPTH = 2

N_META = 16
N_EVEN = (DEPTH + 1) // 2
N_ODD = DEPTH // 2

LRU_WIDTH = D_MODEL // 2
LRU_HEADS = 4
LRU_HEAD_DIM = LRU_WIDTH // LRU_HEADS
CONV_WIDTH = 4
LRU_C = 8.0

MLA_HEADS = 8
MLA_NOPE = 64
MLA_ROPE = 32
MLA_V = 64
MLA_Q_RANK = D_MODEL // 4
MLA_KV_RANK = D_MODEL // 8
ATTN_BLOCK = 128

EVEN_IN = 2 * LRU_WIDTH + MLA_Q_RANK + MLA_KV_RANK + MLA_ROPE
EVEN_MIX = LRU_WIDTH + MLA_HEADS * MLA_V

RET_HEADS = 4
RET_QK_DIM = D_MODEL // RET_HEADS
RET_V_DIM = 2 * RET_QK_DIM
RET_CHUNK = 128
RET_IN = 2 * RET_HEADS * RET_QK_DIM + 2 * RET_HEADS * RET_V_DIM
RET_MIX = RET_HEADS * RET_V_DIM

D_FF = 4 * D_MODEL
ROPE_BASE = 10000.0
DN_ALPHA = (2 * DEPTH) ** 0.25
DN_BETA = (8 * DEPTH) ** -0.25
EPS = 1e-5
NEG_INF = -1e30

kernel_name = 'hybrid_rglru_mla_retention_deepnorm'


def _layernorm(x, g, b):
    xf = x.astype(jnp.float32)
    mu = jnp.mean(xf, axis=-1, keepdims=True)
    xc = xf - mu
    var = jnp.mean(jnp.square(xc), axis=-1, keepdims=True)
    return (xc * lax.rsqrt(var + EPS) * g + b).astype(x.dtype)


def _rmsnorm(x, g):
    xf = x.astype(jnp.float32)
    y = xf * lax.rsqrt(jnp.mean(jnp.square(xf), axis=-1, keepdims=True) + EPS)
    return (y * g).astype(x.dtype)


def _rope(x, pos):
    half = x.shape[-1] // 2
    inv = ROPE_BASE ** (-jnp.arange(half, dtype=jnp.float32) / half)
    ang = pos.astype(jnp.float32)[:, None] * inv[None, :]
    cos = jnp.cos(ang)[None, :, None, :].astype(x.dtype)
    sin = jnp.sin(ang)[None, :, None, :].astype(x.dtype)
    x1, x2 = x[..., :half], x[..., half:]
    return jnp.concatenate([x1 * cos - x2 * sin, x1 * sin + x2 * cos], axis=-1)


def _lru_combine(c1, c2):
    a1, b1 = c1
    a2, b2 = c2
    return a1 * a2, a2 * b1 + b2


def _rglru_group(p_gate, p_rec, conv_w, conv_b, w_rg_a, b_rg_a, w_rg_x, b_rg_x, lru_lambda):
    B, T, _ = p_rec.shape
    xc = lax.conv_general_dilated(
        p_rec, conv_w[:, None, :], window_strides=(1,), padding=[(CONV_WIDTH - 1, 0)],
        dimension_numbers=('NWC', 'WIO', 'NWC'), feature_group_count=LRU_WIDTH) + conv_b
    xh = xc.reshape(B, T, LRU_HEADS, LRU_HEAD_DIM)
    r = jax.nn.sigmoid(jnp.einsum('bthi,hij->bthj', xh, w_rg_a).reshape(B, T, LRU_WIDTH) + b_rg_a)
    i = jax.nn.sigmoid(jnp.einsum('bthi,hij->bthj', xh, w_rg_x).reshape(B, T, LRU_WIDTH) + b_rg_x)
    log_a = (-LRU_C * r * jax.nn.softplus(-lru_lambda)).astype(jnp.float32)
    a = jnp.exp(log_a)
    mult = jnp.sqrt(-jnp.expm1(2.0 * log_a))
    b = mult * (i * xc).astype(jnp.float32)
    _, h = lax.associative_scan(_lru_combine, (a, b), axis=1)
    return h.astype(p_rec.dtype) * jax.nn.gelu(p_gate)


def _attend(qb, qpos, k, v, kpos):
    scale = qb.shape[-1] ** -0.5
    s = jnp.einsum('bqhd,bkhd->bhqk', qb, k).astype(jnp.float32) * scale
    mask = kpos[None, :] <= qpos[:, None]
    s = jnp.where(mask[None, None], s, NEG_INF)
    p = jax.nn.softmax(s, axis=-1).astype(v.dtype)
    return jnp.einsum('bhqk,bkhd->bqhd', p, v)


def _causal_attention(q, k, v, pos):
    B, T, H, d = q.shape
    dv = v.shape[-1]
    out_meta = _attend(q[:, :N_META], pos[:N_META], k[:, :N_META], v[:, :N_META], pos[:N_META])
    nb = (T - N_META) // ATTN_BLOCK
    qr = q[:, N_META:].reshape(B, nb, ATTN_BLOCK, H, d).swapaxes(0, 1)
    pr = pos[N_META:].reshape(nb, ATTN_BLOCK)
    out_r = lax.map(lambda a: _attend(a[0], a[1], k, v, pos), (qr, pr))
    out_r = out_r.swapaxes(0, 1).reshape(B, T - N_META, H, dv)
    return jnp.concatenate([out_meta, out_r], axis=1)


def _mla_group(p_q, p_kv, p_kpe, pos, q_norm_g, w_uq, kv_norm_g, w_ukv):
    B, T, _ = p_q.shape
    q = (_rmsnorm(p_q, q_norm_g) @ w_uq).reshape(B, T, MLA_HEADS, MLA_NOPE + MLA_ROPE)
    q_nope, q_pe = q[..., :MLA_NOPE], _rope(q[..., MLA_NOPE:], pos)
    kv = (_rmsnorm(p_kv, kv_norm_g) @ w_ukv).reshape(B, T, MLA_HEADS, MLA_NOPE + MLA_V)
    k_nope, v = kv[..., :MLA_NOPE], kv[..., MLA_NOPE:]
    k_pe = _rope(p_kpe[:, :, None, :], pos)
    q = jnp.concatenate([q_nope, q_pe], axis=-1)
    k = jnp.concatenate([k_nope, jnp.broadcast_to(k_pe, (B, T, MLA_HEADS, MLA_ROPE))], axis=-1)
    o = _causal_attention(q, k, v, pos)
    return o.reshape(B, T, MLA_HEADS * MLA_V)


def _even_mixer(x, pos, w_in, conv_w, conv_b, w_rg_a, b_rg_a, w_rg_x, b_rg_x, lru_lambda,
                q_norm_g, w_uq, kv_norm_g, w_ukv, w_out):
    p = x @ w_in
    cuts = [LRU_WIDTH, 2 * LRU_WIDTH, 2 * LRU_WIDTH + MLA_Q_RANK,
            2 * LRU_WIDTH + MLA_Q_RANK + MLA_KV_RANK]
    p_gate, p_rec, p_q, p_kv, p_kpe = jnp.split(p, cuts, axis=-1)
    y_rec = _rglru_group(p_gate, p_rec, conv_w, conv_b, w_rg_a, b_rg_a, w_rg_x, b_rg_x, lru_lambda)
    y_att = _mla_group(p_q, p_kv, p_kpe, pos, q_norm_g, w_uq, kv_norm_g, w_ukv)
    return jnp.concatenate([y_rec, y_att], axis=-1) @ w_out


def _retention_chunk(q, k, v, s_prev, log_gamma):
    dt = q.dtype
    c = q.shape[2]
    idx = jnp.arange(c, dtype=jnp.float32)
    diff = idx[:, None] - idx[None, :]
    decay = jnp.where(diff >= 0, jnp.exp(log_gamma[:, None, None] * jnp.maximum(diff, 0.0)), 0.0).astype(dt)
    q_decay = jnp.exp(log_gamma[:, None] * (idx + 1.0))[None, :, :, None].astype(dt)
    k_decay = jnp.exp(log_gamma[:, None] * (c - 1.0 - idx))[None, :, :, None].astype(dt)
    chunk_decay = jnp.exp(log_gamma * c)[None, :, None, None].astype(dt)
    scores = jnp.einsum('bhid,bhjd->bhij', q, k) * decay
    o = jnp.einsum('bhij,bhjv->bhiv', scores, v) + q_decay * jnp.einsum('bhid,bhdv->bhiv', q, s_prev)
    s_new = chunk_decay * s_prev + jnp.einsum('bhjd,bhjv->bhdv', k * k_decay, v)
    return o, s_new


def _odd_mixer(x, pos, w_in, w_out):
    B, T, _ = x.shape
    qk = RET_HEADS * RET_QK_DIM
    p = x @ w_in
    q, k, v, g = jnp.split(p, [qk, 2 * qk, 2 * qk + RET_MIX], axis=-1)
    q = _rope(q.reshape(B, T, RET_HEADS, RET_QK_DIM), pos)
    k = _rope(k.reshape(B, T, RET_HEADS, RET_QK_DIM), pos) * (RET_QK_DIM ** -0.5)
    v = v.reshape(B, T, RET_HEADS, RET_V_DIM)
    q, k, v = (t.transpose(0, 2, 1, 3) for t in (q, k, v))
    log_gamma = jnp.log(1.0 - 2.0 ** (-5.0 - jnp.arange(RET_HEADS, dtype=jnp.float32)))
    s0 = jnp.zeros((B, RET_HEADS, RET_QK_DIM, RET_V_DIM), dtype=q.dtype)
    o_meta, s = _retention_chunk(q[:, :, :N_META], k[:, :, :N_META], v[:, :, :N_META], s0, log_gamma)
    nc = (T - N_META) // RET_CHUNK

    def to_chunks(t):
        return t[:, :, N_META:].reshape(B, RET_HEADS, nc, RET_CHUNK, t.shape[-1]).transpose(2, 0, 1, 3, 4)

    def body(state, qkv):
        qc, kc, vc = qkv
        o, state = _retention_chunk(qc, kc, vc, state, log_gamma)
        return state, o

    _, o_r = lax.scan(body, s, (to_chunks(q), to_chunks(k), to_chunks(v)))
    o_r = o_r.transpose(1, 2, 0, 3, 4).reshape(B, RET_HEADS, T - N_META, RET_V_DIM)
    o = jnp.concatenate([o_meta, o_r], axis=2)
    of = o.astype(jnp.float32)
    o = (of * lax.rsqrt(jnp.mean(jnp.square(of), axis=-1, keepdims=True) + EPS)).astype(x.dtype)
    y = o.transpose(0, 2, 1, 3).reshape(B, T, RET_MIX)
    return (jax.nn.silu(g) * y) @ w_out


def _fwd_setup_inputs(seed: int = 0) -> dict:
    key = jax.random.key(seed)
    ks = iter(jax.random.split(key, 40))
    f32 = jnp.float32

    def nrm(shape, scale):
        return jax.random.normal(next(ks), shape, f32) * scale

    u = jax.random.uniform(next(ks), (N_EVEN, LRU_WIDTH), f32, minval=0.9, maxval=0.999)
    a_base = u ** (1.0 / LRU_C)
    lru_lambda = jnp.log(a_base) - jnp.log1p(-a_base)
    return {
        'x': nrm((BATCH, SEQ, D_MODEL), 1.0),
        'meta_tokens': nrm((N_META, D_MODEL), 1.0),
        'ev_w_in': nrm((N_EVEN, D_MODEL, EVEN_IN), D_MODEL ** -0.5),
        'ev_conv_w': nrm((N_EVEN, CONV_WIDTH, LRU_WIDTH), CONV_WIDTH ** -0.5),
        'ev_conv_b': nrm((N_EVEN, LRU_WIDTH), 0.02),
        'ev_w_rg_a': nrm((N_EVEN, LRU_HEADS, LRU_HEAD_DIM, LRU_HEAD_DIM), LRU_HEAD_DIM ** -0.5),
        'ev_b_rg_a': nrm((N_EVEN, LRU_WIDTH), 0.02),
        'ev_w_rg_x': nrm((N_EVEN, LRU_HEADS, LRU_HEAD_DIM, LRU_HEAD_DIM), LRU_HEAD_DIM ** -0.5),
        'ev_b_rg_x': nrm((N_EVEN, LRU_WIDTH), 0.02),
        'ev_lru_lambda': lru_lambda,
        'ev_q_norm_g': 1.0 + nrm((N_EVEN, MLA_Q_RANK), 0.02),
        'ev_w_uq': nrm((N_EVEN, MLA_Q_RANK, MLA_HEADS * (MLA_NOPE + MLA_ROPE)), MLA_Q_RANK ** -0.5),
        'ev_kv_norm_g': 1.0 + nrm((N_EVEN, MLA_KV_RANK), 0.02),
        'ev_w_ukv': nrm((N_EVEN, MLA_KV_RANK, MLA_HEADS * (MLA_NOPE + MLA_V)), MLA_KV_RANK ** -0.5),
        'ev_w_out': nrm((N_EVEN, EVEN_MIX, D_MODEL), DN_BETA * EVEN_MIX ** -0.5),
        'od_w_in': nrm((N_ODD, D_MODEL, RET_IN), D_MODEL ** -0.5),
        'od_w_out': nrm((N_ODD, RET_MIX, D_MODEL), DN_BETA * RET_MIX ** -0.5),
        'ln_mix_g': 1.0 + nrm((DEPTH, D_MODEL), 0.02),
        'ln_mix_b': nrm((DEPTH, D_MODEL), 0.02),
        'mlp_w1': nrm((DEPTH, D_MODEL, D_FF), D_MODEL ** -0.5),
        'mlp_w2': nrm((DEPTH, D_FF, D_MODEL), DN_BETA * D_FF ** -0.5),
        'ln_mlp_g': 1.0 + nrm((DEPTH, D_MODEL), 0.02),
        'ln_mlp_b': nrm((DEPTH, D_MODEL), 0.02),
    }


def _fwd_reference(x, meta_tokens, ev_w_in, ev_conv_w, ev_conv_b, ev_w_rg_a, ev_b_rg_a, ev_w_rg_x,
              ev_b_rg_x, ev_lru_lambda, ev_q_norm_g, ev_w_uq, ev_kv_norm_g, ev_w_ukv, ev_w_out,
              od_w_in, od_w_out, ln_mix_g, ln_mix_b, mlp_w1, mlp_w2, ln_mlp_g, ln_mlp_b):
    B = x.shape[0]
    meta = jnp.broadcast_to(meta_tokens[None].astype(x.dtype), (B, N_META, D_MODEL))
    h = jnp.concatenate([meta, x], axis=1)
    pos = jnp.arange(h.shape[1], dtype=jnp.int32)
    for l in range(DEPTH):
        if l % 2 == 0:
            e = l // 2
            mix = _even_mixer(h, pos, ev_w_in[e], ev_conv_w[e], ev_conv_b[e], ev_w_rg_a[e], ev_b_rg_a[e],
                              ev_w_rg_x[e], ev_b_rg_x[e], ev_lru_lambda[e], ev_q_norm_g[e], ev_w_uq[e],
                              ev_kv_norm_g[e], ev_w_ukv[e], ev_w_out[e])
        else:
            o = l // 2
            mix = _odd_mixer(h, pos, od_w_in[o], od_w_out[o])
        h = _layernorm(DN_ALPHA * h + mix, ln_mix_g[l], ln_mix_b[l])
        f = jnp.square(jax.nn.relu(h @ mlp_w1[l])) @ mlp_w2[l]
        h = _layernorm(DN_ALPHA * h + f, ln_mlp_g[l], ln_mlp_b[l])
    return h[:, N_META:]


import jax as _jax
import jax.numpy as _jnp

TWIN_FORMAT = 'train_step'
FWD_PARAMS = ['x', 'meta_tokens', 'ev_w_in', 'ev_conv_w', 'ev_conv_b', 'ev_w_rg_a', 'ev_b_rg_a', 'ev_w_rg_x', 'ev_b_rg_x', 'ev_lru_lambda', 'ev_q_norm_g', 'ev_w_uq', 'ev_kv_norm_g', 'ev_w_ukv', 'ev_w_out', 'od_w_in', 'od_w_out', 'ln_mix_g', 'ln_mix_b', 'mlp_w1', 'mlp_w2', 'ln_mlp_g', 'ln_mlp_b']
TWIN_WEIGHTS = ['meta_tokens', 'ev_w_in', 'ev_conv_w', 'ev_conv_b', 'ev_w_rg_a', 'ev_b_rg_a', 'ev_w_rg_x', 'ev_b_rg_x', 'ev_lru_lambda', 'ev_q_norm_g', 'ev_w_uq', 'ev_kv_norm_g', 'ev_w_ukv', 'ev_w_out', 'od_w_in', 'od_w_out', 'ln_mix_g', 'ln_mix_b', 'mlp_w1', 'mlp_w2', 'ln_mlp_g', 'ln_mlp_b']
TWIN_DIFF_INPUT = 'x'
TWIN_INPUTS = ['x', 'meta_tokens', 'ev_w_in', 'ev_conv_w', 'ev_conv_b', 'ev_w_rg_a', 'ev_b_rg_a', 'ev_w_rg_x', 'ev_b_rg_x', 'ev_lru_lambda', 'ev_q_norm_g', 'ev_w_uq', 'ev_kv_norm_g', 'ev_w_ukv', 'ev_w_out', 'od_w_in', 'od_w_out', 'ln_mix_g', 'ln_mix_b', 'mlp_w1', 'mlp_w2', 'ln_mlp_g', 'ln_mlp_b', 'loss_target', 'm_meta_tokens', 'm_ev_w_in', 'm_ev_conv_w', 'm_ev_conv_b', 'm_ev_w_rg_a', 'm_ev_b_rg_a', 'm_ev_w_rg_x', 'm_ev_b_rg_x', 'm_ev_lru_lambda', 'm_ev_q_norm_g', 'm_ev_w_uq', 'm_ev_kv_norm_g', 'm_ev_w_ukv', 'm_ev_w_out', 'm_od_w_in', 'm_od_w_out', 'm_ln_mix_g', 'm_ln_mix_b', 'm_mlp_w1', 'm_mlp_w2', 'm_ln_mlp_g', 'm_ln_mlp_b', 'v_meta_tokens', 'v_ev_w_in', 'v_ev_conv_w', 'v_ev_conv_b', 'v_ev_w_rg_a', 'v_ev_b_rg_a', 'v_ev_w_rg_x', 'v_ev_b_rg_x', 'v_ev_lru_lambda', 'v_ev_q_norm_g', 'v_ev_w_uq', 'v_ev_kv_norm_g', 'v_ev_w_ukv', 'v_ev_w_out', 'v_od_w_in', 'v_od_w_out', 'v_ln_mix_g', 'v_ln_mix_b', 'v_mlp_w1', 'v_mlp_w2', 'v_ln_mlp_g', 'v_ln_mlp_b']
TWIN_OUTPUTS = ['loss', 'grad_x', 'grad_meta_tokens', 'grad_ev_w_in', 'grad_ev_conv_w', 'grad_ev_conv_b', 'grad_ev_w_rg_a', 'grad_ev_b_rg_a', 'grad_ev_w_rg_x', 'grad_ev_b_rg_x', 'grad_ev_lru_lambda', 'grad_ev_q_norm_g', 'grad_ev_w_uq', 'grad_ev_kv_norm_g', 'grad_ev_w_ukv', 'grad_ev_w_out', 'grad_od_w_in', 'grad_od_w_out', 'grad_ln_mix_g', 'grad_ln_mix_b', 'grad_mlp_w1', 'grad_mlp_w2', 'grad_ln_mlp_g', 'grad_ln_mlp_b', 'delta_meta_tokens', 'delta_ev_w_in', 'delta_ev_conv_w', 'delta_ev_conv_b', 'delta_ev_w_rg_a', 'delta_ev_b_rg_a', 'delta_ev_w_rg_x', 'delta_ev_b_rg_x', 'delta_ev_lru_lambda', 'delta_ev_q_norm_g', 'delta_ev_w_uq', 'delta_ev_kv_norm_g', 'delta_ev_w_ukv', 'delta_ev_w_out', 'delta_od_w_in', 'delta_od_w_out', 'delta_ln_mix_g', 'delta_ln_mix_b', 'delta_mlp_w1', 'delta_mlp_w2', 'delta_ln_mlp_g', 'delta_ln_mlp_b', 'new_m_meta_tokens', 'new_m_ev_w_in', 'new_m_ev_conv_w', 'new_m_ev_conv_b', 'new_m_ev_w_rg_a', 'new_m_ev_b_rg_a', 'new_m_ev_w_rg_x', 'new_m_ev_b_rg_x', 'new_m_ev_lru_lambda', 'new_m_ev_q_norm_g', 'new_m_ev_w_uq', 'new_m_ev_kv_norm_g', 'new_m_ev_w_ukv', 'new_m_ev_w_out', 'new_m_od_w_in', 'new_m_od_w_out', 'new_m_ln_mix_g', 'new_m_ln_mix_b', 'new_m_mlp_w1', 'new_m_mlp_w2', 'new_m_ln_mlp_g', 'new_m_ln_mlp_b', 'new_v_meta_tokens', 'new_v_ev_w_in', 'new_v_ev_conv_w', 'new_v_ev_conv_b', 'new_v_ev_w_rg_a', 'new_v_ev_b_rg_a', 'new_v_ev_w_rg_x', 'new_v_ev_b_rg_x', 'new_v_ev_lru_lambda', 'new_v_ev_q_norm_g', 'new_v_ev_w_uq', 'new_v_ev_kv_norm_g', 'new_v_ev_w_ukv', 'new_v_ev_w_out', 'new_v_od_w_in', 'new_v_od_w_out', 'new_v_ln_mix_g', 'new_v_ln_mix_b', 'new_v_mlp_w1', 'new_v_mlp_w2', 'new_v_ln_mlp_g', 'new_v_ln_mlp_b']
TWIN_LEAF_KINDS = {'loss': 'loss', 'grad_x': 'grad_x', 'grad_meta_tokens': 'grad_w', 'grad_ev_w_in': 'grad_w', 'grad_ev_conv_w': 'grad_w', 'grad_ev_conv_b': 'grad_w', 'grad_ev_w_rg_a': 'grad_w', 'grad_ev_b_rg_a': 'grad_w', 'grad_ev_w_rg_x': 'grad_w', 'grad_ev_b_rg_x': 'grad_w', 'grad_ev_lru_lambda': 'grad_w', 'grad_ev_q_norm_g': 'grad_w', 'grad_ev_w_uq': 'grad_w', 'grad_ev_kv_norm_g': 'grad_w', 'grad_ev_w_ukv': 'grad_w', 'grad_ev_w_out': 'grad_w', 'grad_od_w_in': 'grad_w', 'grad_od_w_out': 'grad_w', 'grad_ln_mix_g': 'grad_w', 'grad_ln_mix_b': 'grad_w', 'grad_mlp_w1': 'grad_w', 'grad_mlp_w2': 'grad_w', 'grad_ln_mlp_g': 'grad_w', 'grad_ln_mlp_b': 'grad_w', 'delta_meta_tokens': 'delta_w', 'delta_ev_w_in': 'delta_w', 'delta_ev_conv_w': 'delta_w', 'delta_ev_conv_b': 'delta_w', 'delta_ev_w_rg_a': 'delta_w', 'delta_ev_b_rg_a': 'delta_w', 'delta_ev_w_rg_x': 'delta_w', 'delta_ev_b_rg_x': 'delta_w', 'delta_ev_lru_lambda': 'delta_w', 'delta_ev_q_norm_g': 'delta_w', 'delta_ev_w_uq': 'delta_w', 'delta_ev_kv_norm_g': 'delta_w', 'delta_ev_w_ukv': 'delta_w', 'delta_ev_w_out': 'delta_w', 'delta_od_w_in': 'delta_w', 'delta_od_w_out': 'delta_w', 'delta_ln_mix_g': 'delta_w', 'delta_ln_mix_b': 'delta_w', 'delta_mlp_w1': 'delta_w', 'delta_mlp_w2': 'delta_w', 'delta_ln_mlp_g': 'delta_w', 'delta_ln_mlp_b': 'delta_w', 'new_m_meta_tokens': 'new_m', 'new_m_ev_w_in': 'new_m', 'new_m_ev_conv_w': 'new_m', 'new_m_ev_conv_b': 'new_m', 'new_m_ev_w_rg_a': 'new_m', 'new_m_ev_b_rg_a': 'new_m', 'new_m_ev_w_rg_x': 'new_m', 'new_m_ev_b_rg_x': 'new_m', 'new_m_ev_lru_lambda': 'new_m', 'new_m_ev_q_norm_g': 'new_m', 'new_m_ev_w_uq': 'new_m', 'new_m_ev_kv_norm_g': 'new_m', 'new_m_ev_w_ukv': 'new_m', 'new_m_ev_w_out': 'new_m', 'new_m_od_w_in': 'new_m', 'new_m_od_w_out': 'new_m', 'new_m_ln_mix_g': 'new_m', 'new_m_ln_mix_b': 'new_m', 'new_m_mlp_w1': 'new_m', 'new_m_mlp_w2': 'new_m', 'new_m_ln_mlp_g': 'new_m', 'new_m_ln_mlp_b': 'new_m', 'new_v_meta_tokens': 'new_v', 'new_v_ev_w_in': 'new_v', 'new_v_ev_conv_w': 'new_v', 'new_v_ev_conv_b': 'new_v', 'new_v_ev_w_rg_a': 'new_v', 'new_v_ev_b_rg_a': 'new_v', 'new_v_ev_w_rg_x': 'new_v', 'new_v_ev_b_rg_x': 'new_v', 'new_v_ev_lru_lambda': 'new_v', 'new_v_ev_q_norm_g': 'new_v', 'new_v_ev_w_uq': 'new_v', 'new_v_ev_kv_norm_g': 'new_v', 'new_v_ev_w_ukv': 'new_v', 'new_v_ev_w_out': 'new_v', 'new_v_od_w_in': 'new_v', 'new_v_od_w_out': 'new_v', 'new_v_ln_mix_g': 'new_v', 'new_v_ln_mix_b': 'new_v', 'new_v_mlp_w1': 'new_v', 'new_v_mlp_w2': 'new_v', 'new_v_ln_mlp_g': 'new_v', 'new_v_ln_mlp_b': 'new_v'}


def _forward(args):
    return _fwd_reference(*[args[k] for k in FWD_PARAMS])


def _output_shape():
    out = _jax.eval_shape(lambda: _forward(_fwd_setup_inputs(0)))
    return out.shape, out.dtype

N_MICROBATCH = 1
ADAM_LR = 0.001
ADAM_B1 = 0.9
ADAM_B2 = 0.999
ADAM_EPS = 1e-08
ADAM_WD = 0.01
ADAM_STEP = 10
PER_EXAMPLE_BATCH_AXIS = {'x': 0, 'loss_target': 0}
SHARED_INPUTS = []
_WEIGHT_DTYPES = {'meta_tokens': _jnp.float32, 'ev_w_in': _jnp.float32, 'ev_conv_w': _jnp.float32, 'ev_conv_b': _jnp.float32, 'ev_w_rg_a': _jnp.float32, 'ev_b_rg_a': _jnp.float32, 'ev_w_rg_x': _jnp.float32, 'ev_b_rg_x': _jnp.float32, 'ev_lru_lambda': _jnp.float32, 'ev_q_norm_g': _jnp.float32, 'ev_w_uq': _jnp.float32, 'ev_kv_norm_g': _jnp.float32, 'ev_w_ukv': _jnp.float32, 'ev_w_out': _jnp.float32, 'od_w_in': _jnp.float32, 'od_w_out': _jnp.float32, 'ln_mix_g': _jnp.float32, 'ln_mix_b': _jnp.float32, 'mlp_w1': _jnp.float32, 'mlp_w2': _jnp.float32, 'ln_mlp_g': _jnp.float32, 'ln_mlp_b': _jnp.float32}
MOMENT_SCALE = {'meta_tokens': 4.448480e-03, 'ev_w_in': 2.552745e-02, 'ev_conv_w': 3.303474e-02, 'ev_conv_b': 4.452818e-01, 'ev_w_rg_a': 1.193671e-02, 'ev_b_rg_a': 8.327818e-03, 'ev_w_rg_x': 2.163514e-02, 'ev_b_rg_x': 1.294437e-02, 'ev_lru_lambda': 1.645621e-02, 'ev_q_norm_g': 1.751859e-02, 'ev_w_uq': 9.722169e-03, 'ev_kv_norm_g': 3.670177e-02, 'ev_w_ukv': 1.207509e-02, 'ev_w_out': 4.550460e-02, 'od_w_in': 3.134049e-02, 'od_w_out': 7.489749e-02, 'ln_mix_g': 8.123301e-01, 'ln_mix_b': 4.842308e-01, 'mlp_w1': 4.288900e-02, 'mlp_w2': 1.955862e-01, 'ln_mlp_g': 2.270732e+01, 'ln_mlp_b': 4.999300e+00}


def _to_microbatches(a, axis):
    t = _jnp.moveaxis(a, axis, 0)
    t = t.reshape((N_MICROBATCH, t.shape[0] // N_MICROBATCH) + t.shape[1:])
    return _jnp.moveaxis(t, 1, axis + 1)


def setup_inputs(seed: int = 0) -> dict:
    inp = _fwd_setup_inputs(seed)
    key = _jax.random.fold_in(_jax.random.key(seed), 7919)
    shape, _ = _output_shape()
    out = dict(inp)
    out["loss_target"] = _jax.random.normal(_jax.random.fold_in(key, 0), shape, _jnp.float32)
    for i, name in enumerate(TWIN_WEIGHTS):
        w = inp[name].astype(_jnp.float32)
        if MOMENT_SCALE is None:
            s = _jnp.sqrt(_jnp.mean(_jnp.square(w)) + 1e-30)
        else:
            s = MOMENT_SCALE[name]
        km, kv = _jax.random.split(_jax.random.fold_in(key, i + 1))
        out[name] = w
        out["m_" + name] = s * _jax.random.normal(km, w.shape, _jnp.float32)
        out["v_" + name] = (s * s) * _jax.random.uniform(kv, w.shape, _jnp.float32, 0.5, 1.5)
    if N_MICROBATCH > 1:
        for name, axis in PER_EXAMPLE_BATCH_AXIS.items():
            out[name] = _to_microbatches(out[name], axis)
    return {'x': out['x'], 'meta_tokens': out['meta_tokens'], 'ev_w_in': out['ev_w_in'], 'ev_conv_w': out['ev_conv_w'], 'ev_conv_b': out['ev_conv_b'], 'ev_w_rg_a': out['ev_w_rg_a'], 'ev_b_rg_a': out['ev_b_rg_a'], 'ev_w_rg_x': out['ev_w_rg_x'], 'ev_b_rg_x': out['ev_b_rg_x'], 'ev_lru_lambda': out['ev_lru_lambda'], 'ev_q_norm_g': out['ev_q_norm_g'], 'ev_w_uq': out['ev_w_uq'], 'ev_kv_norm_g': out['ev_kv_norm_g'], 'ev_w_ukv': out['ev_w_ukv'], 'ev_w_out': out['ev_w_out'], 'od_w_in': out['od_w_in'], 'od_w_out': out['od_w_out'], 'ln_mix_g': out['ln_mix_g'], 'ln_mix_b': out['ln_mix_b'], 'mlp_w1': out['mlp_w1'], 'mlp_w2': out['mlp_w2'], 'ln_mlp_g': out['ln_mlp_g'], 'ln_mlp_b': out['ln_mlp_b'], 'loss_target': out['loss_target'], 'm_meta_tokens': out['m_meta_tokens'], 'm_ev_w_in': out['m_ev_w_in'], 'm_ev_conv_w': out['m_ev_conv_w'], 'm_ev_conv_b': out['m_ev_conv_b'], 'm_ev_w_rg_a': out['m_ev_w_rg_a'], 'm_ev_b_rg_a': out['m_ev_b_rg_a'], 'm_ev_w_rg_x': out['m_ev_w_rg_x'], 'm_ev_b_rg_x': out['m_ev_b_rg_x'], 'm_ev_lru_lambda': out['m_ev_lru_lambda'], 'm_ev_q_norm_g': out['m_ev_q_norm_g'], 'm_ev_w_uq': out['m_ev_w_uq'], 'm_ev_kv_norm_g': out['m_ev_kv_norm_g'], 'm_ev_w_ukv': out['m_ev_w_ukv'], 'm_ev_w_out': out['m_ev_w_out'], 'm_od_w_in': out['m_od_w_in'], 'm_od_w_out': out['m_od_w_out'], 'm_ln_mix_g': out['m_ln_mix_g'], 'm_ln_mix_b': out['m_ln_mix_b'], 'm_mlp_w1': out['m_mlp_w1'], 'm_mlp_w2': out['m_mlp_w2'], 'm_ln_mlp_g': out['m_ln_mlp_g'], 'm_ln_mlp_b': out['m_ln_mlp_b'], 'v_meta_tokens': out['v_meta_tokens'], 'v_ev_w_in': out['v_ev_w_in'], 'v_ev_conv_w': out['v_ev_conv_w'], 'v_ev_conv_b': out['v_ev_conv_b'], 'v_ev_w_rg_a': out['v_ev_w_rg_a'], 'v_ev_b_rg_a': out['v_ev_b_rg_a'], 'v_ev_w_rg_x': out['v_ev_w_rg_x'], 'v_ev_b_rg_x': out['v_ev_b_rg_x'], 'v_ev_lru_lambda': out['v_ev_lru_lambda'], 'v_ev_q_norm_g': out['v_ev_q_norm_g'], 'v_ev_w_uq': out['v_ev_w_uq'], 'v_ev_kv_norm_g': out['v_ev_kv_norm_g'], 'v_ev_w_ukv': out['v_ev_w_ukv'], 'v_ev_w_out': out['v_ev_w_out'], 'v_od_w_in': out['v_od_w_in'], 'v_od_w_out': out['v_od_w_out'], 'v_ln_mix_g': out['v_ln_mix_g'], 'v_ln_mix_b': out['v_ln_mix_b'], 'v_mlp_w1': out['v_mlp_w1'], 'v_mlp_w2': out['v_mlp_w2'], 'v_ln_mlp_g': out['v_ln_mlp_g'], 'v_ln_mlp_b': out['v_ln_mlp_b']}


def _loss(weights, diff, rest, loss_target):
    with _jax.named_scope("forward"):
        args = {**rest, TWIN_DIFF_INPUT: diff, **{k: w.astype(_WEIGHT_DTYPES[k]) for k, w in weights.items()}}
        y = _forward(args)
    with _jax.named_scope("loss_head"):
        err = _jnp.square(y.astype(_jnp.float32) - loss_target)
        return 0.5 * _jnp.sum(_jnp.mean(err, axis=-1)) if err.ndim else 0.5 * err


def _adamw(w, g, m, v):
    m = ADAM_B1 * m + (1.0 - ADAM_B1) * g
    v = ADAM_B2 * v + (1.0 - ADAM_B2) * _jnp.square(g)
    m_hat = m / (1.0 - ADAM_B1 ** ADAM_STEP)
    v_hat = v / (1.0 - ADAM_B2 ** ADAM_STEP)
    delta = -ADAM_LR * (m_hat / (_jnp.sqrt(v_hat) + ADAM_EPS) + ADAM_WD * w)
    return delta, m, v


def reference(x, meta_tokens, ev_w_in, ev_conv_w, ev_conv_b, ev_w_rg_a, ev_b_rg_a, ev_w_rg_x, ev_b_rg_x, ev_lru_lambda, ev_q_norm_g, ev_w_uq, ev_kv_norm_g, ev_w_ukv, ev_w_out, od_w_in, od_w_out, ln_mix_g, ln_mix_b, mlp_w1, mlp_w2, ln_mlp_g, ln_mlp_b, loss_target, m_meta_tokens, m_ev_w_in, m_ev_conv_w, m_ev_conv_b, m_ev_w_rg_a, m_ev_b_rg_a, m_ev_w_rg_x, m_ev_b_rg_x, m_ev_lru_lambda, m_ev_q_norm_g, m_ev_w_uq, m_ev_kv_norm_g, m_ev_w_ukv, m_ev_w_out, m_od_w_in, m_od_w_out, m_ln_mix_g, m_ln_mix_b, m_mlp_w1, m_mlp_w2, m_ln_mlp_g, m_ln_mlp_b, v_meta_tokens, v_ev_w_in, v_ev_conv_w, v_ev_conv_b, v_ev_w_rg_a, v_ev_b_rg_a, v_ev_w_rg_x, v_ev_b_rg_x, v_ev_lru_lambda, v_ev_q_norm_g, v_ev_w_uq, v_ev_kv_norm_g, v_ev_w_ukv, v_ev_w_out, v_od_w_in, v_od_w_out, v_ln_mix_g, v_ln_mix_b, v_mlp_w1, v_mlp_w2, v_ln_mlp_g, v_ln_mlp_b):
    given = dict(x=x, meta_tokens=meta_tokens, ev_w_in=ev_w_in, ev_conv_w=ev_conv_w, ev_conv_b=ev_conv_b, ev_w_rg_a=ev_w_rg_a, ev_b_rg_a=ev_b_rg_a, ev_w_rg_x=ev_w_rg_x, ev_b_rg_x=ev_b_rg_x, ev_lru_lambda=ev_lru_lambda, ev_q_norm_g=ev_q_norm_g, ev_w_uq=ev_w_uq, ev_kv_norm_g=ev_kv_norm_g, ev_w_ukv=ev_w_ukv, ev_w_out=ev_w_out, od_w_in=od_w_in, od_w_out=od_w_out, ln_mix_g=ln_mix_g, ln_mix_b=ln_mix_b, mlp_w1=mlp_w1, mlp_w2=mlp_w2, ln_mlp_g=ln_mlp_g, ln_mlp_b=ln_mlp_b, loss_target=loss_target, m_meta_tokens=m_meta_tokens, m_ev_w_in=m_ev_w_in, m_ev_conv_w=m_ev_conv_w, m_ev_conv_b=m_ev_conv_b, m_ev_w_rg_a=m_ev_w_rg_a, m_ev_b_rg_a=m_ev_b_rg_a, m_ev_w_rg_x=m_ev_w_rg_x, m_ev_b_rg_x=m_ev_b_rg_x, m_ev_lru_lambda=m_ev_lru_lambda, m_ev_q_norm_g=m_ev_q_norm_g, m_ev_w_uq=m_ev_w_uq, m_ev_kv_norm_g=m_ev_kv_norm_g, m_ev_w_ukv=m_ev_w_ukv, m_ev_w_out=m_ev_w_out, m_od_w_in=m_od_w_in, m_od_w_out=m_od_w_out, m_ln_mix_g=m_ln_mix_g, m_ln_mix_b=m_ln_mix_b, m_mlp_w1=m_mlp_w1, m_mlp_w2=m_mlp_w2, m_ln_mlp_g=m_ln_mlp_g, m_ln_mlp_b=m_ln_mlp_b, v_meta_tokens=v_meta_tokens, v_ev_w_in=v_ev_w_in, v_ev_conv_w=v_ev_conv_w, v_ev_conv_b=v_ev_conv_b, v_ev_w_rg_a=v_ev_w_rg_a, v_ev_b_rg_a=v_ev_b_rg_a, v_ev_w_rg_x=v_ev_w_rg_x, v_ev_b_rg_x=v_ev_b_rg_x, v_ev_lru_lambda=v_ev_lru_lambda, v_ev_q_norm_g=v_ev_q_norm_g, v_ev_w_uq=v_ev_w_uq, v_ev_kv_norm_g=v_ev_kv_norm_g, v_ev_w_ukv=v_ev_w_ukv, v_ev_w_out=v_ev_w_out, v_od_w_in=v_od_w_in, v_od_w_out=v_od_w_out, v_ln_mix_g=v_ln_mix_g, v_ln_mix_b=v_ln_mix_b, v_mlp_w1=v_mlp_w1, v_mlp_w2=v_mlp_w2, v_ln_mlp_g=v_ln_mlp_g, v_ln_mlp_b=v_ln_mlp_b)
    weights = {n: given[n] for n in TWIN_WEIGHTS}
    shared = {n: given[n] for n in SHARED_INPUTS}
    per_example = {n: given[n] for n in ['x']}
    grad_fn = _jax.value_and_grad(_loss, argnums=(0, 1))

    def one_microbatch(ex, loss_target):
        ex = dict(ex)
        diff = ex.pop(TWIN_DIFF_INPUT)
        return grad_fn(weights, diff, {**shared, **ex}, loss_target)

    if N_MICROBATCH == 1:
        loss, (grad_w, grad_x) = one_microbatch(per_example, given["loss_target"])
    else:
        def body(carry, xs):
            loss_sum, grad_sum = carry
            l_k, (gw_k, gx_k) = one_microbatch(xs[0], xs[1])
            with _jax.named_scope("update"):
                return (loss_sum + l_k, _jax.tree.map(_jnp.add, grad_sum, gw_k)), gx_k

        init = (_jnp.zeros((), _jnp.float32), _jax.tree.map(_jnp.zeros_like, weights))
        (loss, grad_w), grad_x = _jax.lax.scan(body, init, (per_example, given["loss_target"]))
    with _jax.named_scope("update"):
        delta_w, new_m, new_v = {}, {}, {}
        for n in TWIN_WEIGHTS:
            delta_w[n], new_m[n], new_v[n] = _adamw(weights[n], grad_w[n], given["m_" + n], given["v_" + n])
    return (loss, grad_x, *[grad_w[n] for n in TWIN_WEIGHTS], *[delta_w[n] for n in TWIN_WEIGHTS],
            *[new_m[n] for n in TWIN_WEIGHTS], *[new_v[n] for n in TWIN_WEIGHTS])
```

```python
import functools
import math

import jax
import jax.numpy as jnp
from jax import lax
from jax.experimental import pallas as pl
from jax.experimental.pallas import tpu as pltpu

F32 = jnp.float32
BF16 = jnp.bfloat16

N_DEV = 8
MESH_AXES = ("x", "y", "c")
LANES = 128
SEQ_BLOCK = 128

N_META = 16
LRU_C = 8.0
MLA_HEADS = 8
MLA_NOPE = 64
MLA_ROPE = 32
MLA_V = 64
RET_HEADS = 4
ROPE_BASE = 10000.0
DEPTH = 2
DN_ALPHA = (2 * DEPTH) ** 0.25
EPS = 1e-5
NEG_INF = -1e30

ADAM_LR = 0.001
ADAM_B1 = 0.9
ADAM_B2 = 0.999
ADAM_EPS = 1e-08
ADAM_WD = 0.01
ADAM_STEP = 10

VMEM_LIMIT = 56 * 1024 * 1024


def _params(*sem):
    return pltpu.CompilerParams(dimension_semantics=sem, vmem_limit_bytes=VMEM_LIMIT)


def _pick(n, cands):
    for c in cands:
        if n % c == 0:
            return c
    return n


def _row_tile(r, width):
    cands = (256, 128, 64, 32, 16, 8) if width <= 1024 else (128, 64, 32, 16, 8)
    return _pick(r, cands)


_DIMS = {"nn": (((1,), (0,)), ((), ())), "nt": (((1,), (1,)), ((), ())), "tn": (((0,), (0,)), ((), ()))}


def _dot(a, b, mode):
    return lax.dot_general(a.astype(BF16), b.astype(BF16), _DIMS[mode], preferred_element_type=F32)


def _matmul(a, b, mode, name):
    if mode == "nn":
        (m, k), n = a.shape, b.shape[1]
    elif mode == "nt":
        (m, k), n = a.shape, b.shape[0]
    else:
        (k, m), n = a.shape, b.shape[1]
    tm = _pick(m, (1088, 1024, 544, 512, 272, 256, 128, 64, 32, 16, 8))
    tn = _pick(n, (512, 256, 128))
    tk = _pick(k, (1088, 1024, 544, 512, 272, 256, 128))
    nk = k // tk

    if mode == "nn":
        a_spec = pl.BlockSpec((tm, tk), lambda i, j, kk: (i, kk))
        b_spec = pl.BlockSpec((tk, tn), lambda i, j, kk: (kk, j))
    elif mode == "nt":
        a_spec = pl.BlockSpec((tm, tk), lambda i, j, kk: (i, kk))
        b_spec = pl.BlockSpec((tn, tk), lambda i, j, kk: (j, kk))
    else:
        a_spec = pl.BlockSpec((tk, tm), lambda i, j, kk: (kk, i))
        b_spec = pl.BlockSpec((tk, tn), lambda i, j, kk: (kk, j))

    def body(a_ref, b_ref, o_ref):
        kk = pl.program_id(2)
        part = _dot(a_ref[...], b_ref[...], mode)

        @pl.when(kk == 0)
        def _():
            o_ref[...] = part

        @pl.when(kk != 0)
        def _():
            o_ref[...] += part

    return pl.pallas_call(
        body,
        name=name,
        grid=(m // tm, n // tn, nk),
        in_specs=[a_spec, b_spec],
        out_specs=pl.BlockSpec((tm, tn), lambda i, j, kk: (i, j)),
        out_shape=jax.ShapeDtypeStruct((m, n), F32),
        compiler_params=_params("parallel", "parallel", "arbitrary"),
    )(a, b)


def _group_matmul(a, w, mode, name):
    if mode in ("nn", "nt"):
        g, dk, dn = w.shape
        m = a.shape[0]
        d_in, d_out = (dk, dn) if mode == "nn" else (dn, dk)
        tm = _pick(m, (1088, 1024, 544, 512, 272, 256, 128, 64, 32, 16, 8))

        def body(a_ref, w_ref, o_ref):
            o_ref[...] = _dot(a_ref[...], w_ref[0], mode)

        return pl.pallas_call(
            body,
            name=name,
            grid=(g, m // tm),
            in_specs=[pl.BlockSpec((tm, d_in), lambda h, i: (i, h)), pl.BlockSpec((1, dk, dn), lambda h, i: (h, 0, 0))],
            out_specs=pl.BlockSpec((tm, d_out), lambda h, i: (i, h)),
            out_shape=jax.ShapeDtypeStruct((m, g * d_out), F32),
            compiler_params=_params("parallel", "parallel"),
        )(a, w)
    b = w
    m = a.shape[0]
    dk = dn = LANES
    g = a.shape[1] // dk
    tm = _pick(m, (1088, 1024, 544, 512, 272, 256, 128, 64, 32, 16, 8))

    def body(a_ref, b_ref, o_ref):
        part = _dot(a_ref[...], b_ref[...], "tn")

        @pl.when(pl.program_id(1) == 0)
        def _():
            o_ref[0] = part

        @pl.when(pl.program_id(1) != 0)
        def _():
            o_ref[0] += part

    return pl.pallas_call(
        body,
        name=name,
        grid=(g, m // tm),
        in_specs=[pl.BlockSpec((tm, dk), lambda h, i: (i, h)), pl.BlockSpec((tm, dn), lambda h, i: (i, h))],
        out_specs=pl.BlockSpec((1, dk, dn), lambda h, i: (h, 0, 0)),
        out_shape=jax.ShapeDtypeStruct((g, dk, dn), F32),
        compiler_params=_params("parallel", "arbitrary"),
    )(a, b)


def _make_group_linear(name):
    @jax.custom_vjp
    def op(x, w):
        return _group_matmul(x, w, "nn", name + "_fwd")

    def fwd(x, w):
        return op(x, w), (x, w)

    def bwd(res, dy):
        x, w = res
        return _group_matmul(dy, w, "nt", name + "_dx"), _group_matmul(x, dy, "tn", name + "_dw")

    op.defvjp(fwd, bwd)
    return op


def _my_place():
    return lax.axis_index("x"), lax.axis_index("y"), lax.axis_index("c")


def _all_gather(shard, name):
    shape, dtype = shard.shape, shard.dtype

    def body(x_ref, out_ref, send_sems, recv_sems, local_sem):
        x, y, c = _my_place()
        me, sibling = (x, y, c), (x, y, 1 - c)
        chips = [(1 - x, y), (x, 1 - y), (1 - x, 1 - y)]

        def slot(px, py, pc):
            return out_ref.at[4 * px + 2 * py + pc]

        def copy(k, block, to, src=None):
            return pltpu.make_async_remote_copy(
                src_ref=slot(*block) if src is None else src,
                dst_ref=slot(*block),
                send_sem=send_sems.at[k],
                recv_sem=recv_sems.at[k],
                device_id=to,
                device_id_type=pl.DeviceIdType.MESH,
            )

        mine = pltpu.make_async_copy(x_ref, slot(*me), local_sem)
        mine.start()
        first = [copy(0, me, sibling, src=x_ref)]
        first += [copy(1 + j, me, (*chip, c), src=x_ref) for j, chip in enumerate(chips)]
        for cp in first:
            cp.start()
        passed = [copy(4 + j, (*chip, c), sibling) for j, chip in enumerate(chips)]
        for j, chip in enumerate(chips):
            copy(1 + j, (*chip, c), me).wait_recv()
            passed[j].start()
        copy(0, sibling, me).wait_recv()
        for j, chip in enumerate(chips):
            copy(4 + j, (*chip, 1 - c), me).wait_recv()
        for cp in first + passed:
            cp.wait_send()
        mine.wait()

    return pl.pallas_call(
        body,
        name=name,
        out_shape=jax.ShapeDtypeStruct((N_DEV,) + shape, dtype),
        in_specs=[pl.BlockSpec(memory_space=pl.ANY)],
        out_specs=pl.BlockSpec(memory_space=pl.ANY),
        scratch_shapes=[pltpu.SemaphoreType.DMA((7,)), pltpu.SemaphoreType.DMA((7,)), pltpu.SemaphoreType.DMA],
    )(shard)


def _all_to_all(stacked, name):
    def body(x_ref, out_ref, send_sems, recv_sems, local_sem):
        x, y, c = _my_place()
        me = 4 * x + 2 * y + c
        mine = pltpu.make_async_copy(x_ref.at[me], out_ref.at[me], local_sem)
        mine.start()
        copies = []
        for k in range(1, N_DEV):
            px, py, pc = x ^ ((k >> 2) & 1), y ^ ((k >> 1) & 1), c ^ (k & 1)
            peer = 4 * px + 2 * py + pc
            copies.append(
                pltpu.make_async_remote_copy(
                    src_ref=x_ref.at[peer],
                    dst_ref=out_ref.at[me],
                    send_sem=send_sems.at[k - 1],
                    recv_sem=recv_sems.at[k - 1],
                    device_id=(px, py, pc),
                    device_id_type=pl.DeviceIdType.MESH,
                )
            )
        for cp in copies:
            cp.start()
        for cp in copies:
            cp.wait_recv()
        for cp in copies:
            cp.wait_send()
        mine.wait()

    return pl.pallas_call(
        body,
        name=name,
        out_shape=jax.ShapeDtypeStruct(stacked.shape, stacked.dtype),
        in_specs=[pl.BlockSpec(memory_space=pl.ANY)],
        out_specs=pl.BlockSpec(memory_space=pl.ANY),
        scratch_shapes=[pltpu.SemaphoreType.DMA((7,)), pltpu.SemaphoreType.DMA((7,)), pltpu.SemaphoreType.DMA],
    )(stacked)


def _sum_blocks(stacked, name):
    _, r, c = stacked.shape
    tr = _pick(r, (256, 128, 64, 32, 16, 8))

    def body(x_ref, o_ref):
        s = [x_ref[j] for j in range(N_DEV)]
        o_ref[...] = ((s[0] + s[1]) + (s[2] + s[3])) + ((s[4] + s[5]) + (s[6] + s[7]))

    return pl.pallas_call(
        body,
        name=name,
        grid=(r // tr,),
        in_specs=[pl.BlockSpec((N_DEV, tr, c), lambda i: (0, i, 0))],
        out_specs=pl.BlockSpec((tr, c), lambda i: (i, 0)),
        out_shape=jax.ShapeDtypeStruct((r, c), stacked.dtype),
        compiler_params=_params("parallel"),
    )(stacked)


def _stack_cols(full):
    k, n8 = full.shape
    return full.reshape(k, N_DEV, n8 // N_DEV).transpose(1, 0, 2)


def _unstack_cols(stacked):
    j, k, n = stacked.shape
    return stacked.transpose(1, 0, 2).reshape(k, j * n)


def _gather_weight(shard, cols, name):
    g = _all_gather(shard.astype(BF16), name)
    return _unstack_cols(g) if cols else g.reshape(-1, shard.shape[1])


def _scatter_grad(full, cols, name):
    if cols:
        st = _stack_cols(full)
    else:
        st = full.reshape(N_DEV, full.shape[0] // N_DEV, full.shape[1])
    return _sum_blocks(_all_to_all(st, name + "_a2a"), name + "_sum")


def _make_fsdp_linear(cols, name):
    @jax.custom_vjp
    def op(x, w_shard):
        return _matmul(x, _gather_weight(w_shard, cols, name + "_ag"), "nn", name + "_fwd")

    def fwd(x, w_shard):
        w = _gather_weight(w_shard, cols, name + "_ag")
        return _matmul(x, w, "nn", name + "_fwd"), (x, w)

    def bwd(res, dy):
        x, w = res
        dx = _matmul(dy, w, "nt", name + "_dx")
        dw = _matmul(x, dy, "tn", name + "_dw")
        return dx, _scatter_grad(dw, cols, name + "_rs")

    op.defvjp(fwd, bwd)
    return op


def _make_fsdp_param(name):
    @jax.custom_vjp
    def op(shard):
        return _unstack_cols(_all_gather(shard, name + "_ag"))

    def fwd(shard):
        return op(shard), None

    def bwd(_, g):
        return (_scatter_grad(g, True, name + "_rs"),)

    op.defvjp(fwd, bwd)
    return op


def _make_replicated_params(name):
    @jax.custom_vjp
    def op(ps):
        return ps

    def fwd(ps):
        return ps, None

    def bwd(_, gs):
        flat = jnp.concatenate([g.reshape(-1) for g in gs])
        n = flat.shape[0]
        rows = -(-n // (256 * LANES)) * 256
        packed = jnp.pad(flat, (0, rows * LANES - n)).reshape(rows, LANES)
        total = _sum_blocks(_all_gather(packed, name + "_ag"), name + "_sum").reshape(-1)
        out, off = [], 0
        for g in gs:
            out.append(total[off:off + g.size].reshape(g.shape))
            off += g.size
        return (tuple(out),)

    op.defvjp(fwd, bwd)
    return op


def _make_rowwise(f, name, n_rows, n_tabs, n_pars):
    n_in = n_rows + n_tabs + n_pars

    def specs(args, tm):
        blocked = [pl.BlockSpec((tm, a.shape[1]), lambda i: (i, 0)) for a in args[: n_rows + n_tabs]]
        whole = [pl.BlockSpec(a.shape, lambda i: (0, 0)) for a in args[n_rows + n_tabs:]]
        return blocked + whole

    def out_struct(args, tm):
        blk = [jax.ShapeDtypeStruct((tm, a.shape[1]), a.dtype) for a in args[: n_rows + n_tabs]]
        blk += [jax.ShapeDtypeStruct(a.shape, a.dtype) for a in args[n_rows + n_tabs:]]
        return jax.eval_shape(f, *blk)

    def fwd_call(*args):
        r = args[0].shape[0]
        tm = _row_tile(r, max(a.shape[1] for a in args[:n_rows]))
        ro, so = out_struct(args, tm)

        def body(*refs):
            vals = [x[...] for x in refs[:n_in]]
            outs = refs[n_in:]
            rv, sv = f(*vals)
            for o, v in zip(outs[: len(ro)], rv):
                o[...] = v
            for o, v in zip(outs[len(ro):], sv):
                @pl.when(pl.program_id(0) == 0)
                def _(o=o, v=v):
                    o[...] = v

                @pl.when(pl.program_id(0) != 0)
                def _(o=o, v=v):
                    o[...] += v

        out_shape = [jax.ShapeDtypeStruct((r, s.shape[1]), s.dtype) for s in ro]
        out_shape += [jax.ShapeDtypeStruct(s.shape, s.dtype) for s in so]
        out_specs = [pl.BlockSpec((tm, s.shape[1]), lambda i: (i, 0)) for s in ro]
        out_specs += [pl.BlockSpec(s.shape, lambda i: (0, 0)) for s in so]
        res = pl.pallas_call(
            body,
            name=name + "_fwd",
            grid=(r // tm,),
            in_specs=specs(args, tm),
            out_specs=out_specs,
            out_shape=out_shape,
            compiler_params=_params("arbitrary" if so else "parallel"),
        )(*args)
        return tuple(res[: len(ro)]), tuple(res[len(ro):])

    def bwd_call(args, cots):
        r = args[0].shape[0]
        tm = _row_tile(r, max(a.shape[1] for a in args[:n_rows]))
        ro, so = out_struct(args, tm)
        crow, csum = cots
        rows, tabs, pars = args[:n_rows], args[n_rows:n_rows + n_tabs], args[n_rows + n_tabs:]
        n_c = len(crow) + len(csum)

        def body(*refs):
            vals = [x[...] for x in refs[:n_in]]
            cv = [x[...] for x in refs[n_in:n_in + n_c]]
            outs = refs[n_in + n_c:]
            tv = vals[n_rows:n_rows + n_tabs]

            def g(*dargs):
                return f(*dargs[:n_rows], *tv, *dargs[n_rows:])

            _, vjp = jax.vjp(g, *vals[:n_rows], *vals[n_rows + n_tabs:])
            d = vjp((tuple(cv[: len(crow)]), tuple(cv[len(crow):])))
            for o, v in zip(outs[:n_rows], d[:n_rows]):
                o[...] = v
            for o, v in zip(outs[n_rows:], d[n_rows:]):
                @pl.when(pl.program_id(0) == 0)
                def _(o=o, v=v):
                    o[...] = v

                @pl.when(pl.program_id(0) != 0)
                def _(o=o, v=v):
                    o[...] += v

        in_specs = specs(args, tm)
        in_specs += [pl.BlockSpec((tm, c.shape[1]), lambda i: (i, 0)) for c in crow]
        in_specs += [pl.BlockSpec(c.shape, lambda i: (0, 0)) for c in csum]
        out_shape = [jax.ShapeDtypeStruct(a.shape, a.dtype) for a in rows + pars]
        out_specs = [pl.BlockSpec((tm, a.shape[1]), lambda i: (i, 0)) for a in rows]
        out_specs += [pl.BlockSpec(a.shape, lambda i: (0, 0)) for a in pars]
        res = pl.pallas_call(
            body,
            name=name + "_bwd",
            grid=(r // tm,),
            in_specs=in_specs,
            out_specs=out_specs,
            out_shape=out_shape,
            compiler_params=_params("arbitrary" if pars else "parallel"),
        )(*args, *crow, *csum)
        return tuple(res[:n_rows]), tuple(res[n_rows:])

    @jax.custom_vjp
    def op(rows, tabs, pars):
        return fwd_call(*rows, *tabs, *pars)

    def fwd(rows, tabs, pars):
        return fwd_call(*rows, *tabs, *pars), (rows, tabs, pars)

    def bwd(res, cots):
        rows, tabs, pars = res
        drows, dpars = bwd_call(tuple(rows) + tuple(tabs) + tuple(pars), cots)
        return drows, tuple(jnp.zeros_like(t) for t in tabs), dpars

    op.defvjp(fwd, bwd)
    return op


def _sigmoid(x):
    return 0.5 * (jnp.tanh(0.5 * x) + 1.0)


@jax.custom_jvp
def _softplus(x):
    e = jnp.exp(-jnp.abs(x))
    u = 1.0 + e
    log1p_e = jnp.where(u == 1.0, e, e * jnp.log(u) / jnp.where(u == 1.0, 1.0, u - 1.0))
    return jnp.maximum(x, 0.0) + log1p_e


@_softplus.defjvp
def _softplus_jvp(primals, tangents):
    (x,), (t,) = primals, tangents
    return _softplus(x), t * _sigmoid(x)


def _gelu(x):
    return 0.5 * x * (1.0 + jnp.tanh(math.sqrt(2.0 / math.pi) * (x + 0.044715 * (x * x * x))))


def _ln_res_f(h, mix, g, b):
    z = DN_ALPHA * h + mix
    mu = jnp.mean(z, axis=-1, keepdims=True)
    zc = z - mu
    var = jnp.mean(zc * zc, axis=-1, keepdims=True)
    return (zc * lax.rsqrt(var + EPS) * g + b,), ()


def _rmsnorm_f(x, g):
    return (x * lax.rsqrt(jnp.mean(x * x, axis=-1, keepdims=True) + EPS) * g,), ()


def _lru_gates_f(ga, gx, xc, b_a, b_x, lam):
    r = _sigmoid(ga + b_a)
    i = _sigmoid(gx + b_x)
    log_a = -LRU_C * r * _softplus(-lam)
    a = jnp.exp(log_a)
    one_minus_a2 = jnp.tanh(-log_a) * (jnp.exp(2.0 * log_a) + 1.0)
    return (a, jnp.sqrt(one_minus_a2) * (i * xc)), ()


def _lru_out_f(hh, p_gate):
    return (hh * _gelu(p_gate),), ()


def _rope_pair_f(x1, x2, cos, sin):
    return (x1 * cos - x2 * sin, x1 * sin + x2 * cos), ()


def _rope_ret_f(q, k, cos2, sin2):
    d = cos2.shape[1]
    half = d // 2
    k_scale = d ** -0.5

    def rope(x):
        outs = []
        for h in range(x.shape[1] // d):
            xh = x[:, h * d:(h + 1) * d]
            rot = jnp.concatenate([xh[:, half:], xh[:, :half]], axis=1)
            outs.append(xh * cos2 + rot * sin2)
        return jnp.concatenate(outs, axis=1)

    return (rope(q), rope(k) * k_scale), ()


def _ret_out_f(o, g):
    d = o.shape[1] // RET_HEADS
    outs = []
    for h in range(RET_HEADS):
        oh = o[:, h * d:(h + 1) * d]
        outs.append(oh * lax.rsqrt(jnp.mean(oh * oh, axis=-1, keepdims=True) + EPS))
    y = jnp.concatenate(outs, axis=1)
    return (g * _sigmoid(g) * y,), ()


def _relu2_f(u):
    r = jnp.maximum(u, 0.0)
    return (r * r,), ()


def _loss_f(y, t, mask):
    e = (y - t) * mask
    per_row = jnp.sum(e * e, axis=-1, keepdims=True) * (0.5 / y.shape[1])
    total = jnp.sum(per_row, axis=0, keepdims=True)
    return (), (jnp.broadcast_to(total, (1, LANES)),)


def _shift_down(x, s):
    if s == 0:
        return x
    t = x.shape[0]
    row = lax.broadcasted_iota(jnp.int32, x.shape, 0)
    return jnp.where(row >= s, pltpu.roll(x, s, 0), 0.0)


def _shift_up(x, s):
    if s == 0:
        return x
    t = x.shape[0]
    row = lax.broadcasted_iota(jnp.int32, x.shape, 0)
    return jnp.where(row < t - s, pltpu.roll(x, t - s, 0), 0.0)


def _conv_fwd(x, w, b, name):
    bsz, t, c = x.shape
    width = w.shape[0]

    def body(x_ref, w_ref, b_ref, y_ref):
        xv = x_ref[0]
        acc = jnp.broadcast_to(b_ref[...], xv.shape)
        for k in range(width):
            acc = acc + w_ref[k:k + 1, :] * _shift_down(xv, width - 1 - k)
        y_ref[0] = acc

    return pl.pallas_call(
        body,
        name=name,
        grid=(bsz, c // LANES),
        in_specs=[
            pl.BlockSpec((1, t, LANES), lambda i, j: (i, 0, j)),
            pl.BlockSpec((width, LANES), lambda i, j: (0, j)),
            pl.BlockSpec((1, LANES), lambda i, j: (0, j)),
        ],
        out_specs=pl.BlockSpec((1, t, LANES), lambda i, j: (i, 0, j)),
        out_shape=jax.ShapeDtypeStruct(x.shape, F32),
        compiler_params=_params("parallel", "parallel"),
    )(x, w, b)


def _conv_bwd(x, w, dy, name):
    bsz, t, c = x.shape
    width = w.shape[0]

    def body(x_ref, w_ref, dy_ref, dx_ref, dw_ref, db_ref):
        xv, g = x_ref[0], dy_ref[0]
        dx = jnp.zeros_like(xv)
        dws = []
        for k in range(width):
            s = width - 1 - k
            dx = dx + w_ref[k:k + 1, :] * _shift_up(g, s)
            dws.append(jnp.sum(g * _shift_down(xv, s), axis=0, keepdims=True))
        dx_ref[0] = dx
        dw = jnp.concatenate(dws, axis=0)
        db = jnp.sum(g, axis=0, keepdims=True)

        @pl.when(pl.program_id(1) == 0)
        def _():
            dw_ref[...] = dw
            db_ref[...] = db

        @pl.when(pl.program_id(1) != 0)
        def _():
            dw_ref[...] += dw
            db_ref[...] += db

    return pl.pallas_call(
        body,
        name=name,
        grid=(c // LANES, bsz),
        in_specs=[
            pl.BlockSpec((1, t, LANES), lambda j, i: (i, 0, j)),
            pl.BlockSpec((width, LANES), lambda j, i: (0, j)),
            pl.BlockSpec((1, t, LANES), lambda j, i: (i, 0, j)),
        ],
        out_specs=[
            pl.BlockSpec((1, t, LANES), lambda j, i: (i, 0, j)),
            pl.BlockSpec((width, LANES), lambda j, i: (0, j)),
            pl.BlockSpec((1, LANES), lambda j, i: (0, j)),
        ],
        out_shape=[
            jax.ShapeDtypeStruct(x.shape, F32),
            jax.ShapeDtypeStruct(w.shape, F32),
            jax.ShapeDtypeStruct((1, c), F32),
        ],
        compiler_params=_params("parallel", "arbitrary"),
    )(x, w, dy)


def _make_conv(name):
    @jax.custom_vjp
    def op(x, w, b):
        return _conv_fwd(x, w, b, name + "_fwd")

    def fwd(x, w, b):
        return op(x, w, b), (x, w)

    def bwd(res, dy):
        x, w = res
        return tuple(_conv_bwd(x, w, dy, name + "_bwd"))

    op.defvjp(fwd, bwd)
    return op


def _scan_fwd(a, b, name):
    bsz, t, c = a.shape

    def body(a_ref, b_ref, h_ref):
        def step(i, h):
            h = a_ref[0, pl.ds(i, 1), :] * h + b_ref[0, pl.ds(i, 1), :]
            h_ref[0, pl.ds(i, 1), :] = h
            return h

        lax.fori_loop(0, t, step, jnp.zeros((1, LANES), F32), unroll=8)

    spec = pl.BlockSpec((1, t, LANES), lambda i, j: (i, 0, j))
    return pl.pallas_call(
        body,
        name=name,
        grid=(bsz, c // LANES),
        in_specs=[spec, spec],
        out_specs=spec,
        out_shape=jax.ShapeDtypeStruct(a.shape, F32),
        compiler_params=_params("parallel", "parallel"),
    )(a, b)


def _scan_bwd(a, h, g, name):
    bsz, t, c = a.shape

    def body(a_ref, h_ref, g_ref, da_ref, db_ref):
        def step(n, carry):
            i = t - 1 - n
            lam = g_ref[0, pl.ds(i, 1), :] + carry
            db_ref[0, pl.ds(i, 1), :] = lam
            prev = jnp.where(i > 0, h_ref[0, pl.ds(jnp.maximum(i - 1, 0), 1), :], 0.0)
            da_ref[0, pl.ds(i, 1), :] = lam * prev
            return a_ref[0, pl.ds(i, 1), :] * lam

        lax.fori_loop(0, t, step, jnp.zeros((1, LANES), F32), unroll=8)

    spec = pl.BlockSpec((1, t, LANES), lambda i, j: (i, 0, j))
    return pl.pallas_call(
        body,
        name=name,
        grid=(bsz, c // LANES),
        in_specs=[spec, spec, spec],
        out_specs=[spec, spec],
        out_shape=[jax.ShapeDtypeStruct(a.shape, F32)] * 2,
        compiler_params=_params("parallel", "parallel"),
    )(a, h, g)


def _make_scan(name):
    @jax.custom_vjp
    def op(a, b):
        return _scan_fwd(a, b, name + "_fwd")

    def fwd(a, b):
        h = op(a, b)
        return h, (a, h)

    def bwd(res, g):
        a, h = res
        da, db = _scan_bwd(a, h, g, name + "_bwd")
        return da, db

    op.defvjp(fwd, bwd)
    return op


def _causal(i, tq, tk):
    qpos = i * tq + lax.broadcasted_iota(jnp.int32, (tq, tk), 0)
    kpos = lax.broadcasted_iota(jnp.int32, (tq, tk), 1)
    return kpos <= qpos


def _attn_probs(q, k, i, scale):
    s = _dot(q, k, "nt") * scale
    s = jnp.where(_causal(i, q.shape[0], k.shape[0]), s, NEG_INF)
    e = jnp.exp(s - jnp.max(s, axis=-1, keepdims=True))
    return e / jnp.sum(e, axis=-1, keepdims=True)


def _attn_fwd(q, k, v, name):
    g, t, dq = q.shape
    dv = v.shape[2]
    scale = dq ** -0.5
    tq = SEQ_BLOCK

    def body(q_ref, k_ref, v_ref, o_ref):
        p = _attn_probs(q_ref[0], k_ref[0], pl.program_id(1), scale)
        o_ref[0] = _dot(p, v_ref[0], "nn")

    return pl.pallas_call(
        body,
        name=name,
        grid=(g, t // tq),
        in_specs=[
            pl.BlockSpec((1, tq, dq), lambda b, i: (b, i, 0)),
            pl.BlockSpec((1, t, dq), lambda b, i: (b, 0, 0)),
            pl.BlockSpec((1, t, dv), lambda b, i: (b, 0, 0)),
        ],
        out_specs=pl.BlockSpec((1, tq, dv), lambda b, i: (b, i, 0)),
        out_shape=jax.ShapeDtypeStruct((g, t, dv), F32),
        compiler_params=_params("parallel", "parallel"),
    )(q, k, v)


def _attn_bwd(q, k, v, do, name):
    g, t, dq = q.shape
    dv = v.shape[2]
    scale = dq ** -0.5
    tq = SEQ_BLOCK

    def body(q_ref, k_ref, v_ref, do_ref, dq_ref, dk_ref, dv_ref):
        qb, kk, vv, dob = q_ref[0], k_ref[0], v_ref[0], do_ref[0]
        p = _attn_probs(qb, kk, pl.program_id(1), scale)
        dp = _dot(dob, vv, "nt")
        ds = p * (dp - jnp.sum(dp * p, axis=-1, keepdims=True)) * scale
        dq_ref[0] = _dot(ds, kk, "nn")
        dk = _dot(ds, qb, "tn")
        dvv = _dot(p, dob, "tn")

        @pl.when(pl.program_id(1) == 0)
        def _():
            dk_ref[0] = dk
            dv_ref[0] = dvv

        @pl.when(pl.program_id(1) != 0)
        def _():
            dk_ref[0] += dk
            dv_ref[0] += dvv

    return pl.pallas_call(
        body,
        name=name,
        grid=(g, t // tq),
        in_specs=[
            pl.BlockSpec((1, tq, dq), lambda b, i: (b, i, 0)),
            pl.BlockSpec((1, t, dq), lambda b, i: (b, 0, 0)),
            pl.BlockSpec((1, t, dv), lambda b, i: (b, 0, 0)),
            pl.BlockSpec((1, tq, dv), lambda b, i: (b, i, 0)),
        ],
        out_specs=[
            pl.BlockSpec((1, tq, dq), lambda b, i: (b, i, 0)),
            pl.BlockSpec((1, t, dq), lambda b, i: (b, 0, 0)),
            pl.BlockSpec((1, t, dv), lambda b, i: (b, 0, 0)),
        ],
        out_shape=[
            jax.ShapeDtypeStruct(q.shape, F32),
            jax.ShapeDtypeStruct(k.shape, F32),
            jax.ShapeDtypeStruct(v.shape, F32),
        ],
        compiler_params=_params("parallel", "arbitrary"),
    )(q, k, v, do)


def _make_attention(name):
    @jax.custom_vjp
    def op(q, k, v):
        return _attn_fwd(q, k, v, name + "_fwd")

    def fwd(q, k, v):
        return op(q, k, v), (q, k, v)

    def bwd(res, do):
        return tuple(_attn_bwd(*res, do, name + "_bwd"))

    op.defvjp(fwd, bwd)
    return op


def _decay(i, tq, tk, log_gamma):
    qpos = i * tq + lax.broadcasted_iota(jnp.int32, (tq, tk), 0)
    kpos = lax.broadcasted_iota(jnp.int32, (tq, tk), 1)
    diff = qpos - kpos
    return jnp.where(diff >= 0, jnp.exp(log_gamma * jnp.maximum(diff, 0).astype(F32)), 0.0)


def _ret_specs(t, dk, dv, heads):
    tq = SEQ_BLOCK
    return (
        pl.BlockSpec(memory_space=pltpu.SMEM),
        pl.BlockSpec((1, tq, dk), lambda b, h, i: (b, i, h)),
        pl.BlockSpec((1, t, dk), lambda b, h, i: (b, 0, h)),
        pl.BlockSpec((1, t, dv), lambda b, h, i: (b, 0, h)),
        pl.BlockSpec((1, tq, dv), lambda b, h, i: (b, i, h)),
    )


def _ret_fwd(lg, q, k, v, name):
    bsz, t, hdk = q.shape
    heads = lg.shape[0]
    dk, dv = hdk // heads, v.shape[2] // heads
    lg_spec, q_spec, k_spec, v_spec, o_spec = _ret_specs(t, dk, dv, heads)

    def body(lg_ref, q_ref, k_ref, v_ref, o_ref):
        d = _decay(pl.program_id(2), SEQ_BLOCK, t, lg_ref[pl.program_id(1)])
        o_ref[0] = _dot(_dot(q_ref[0], k_ref[0], "nt") * d, v_ref[0], "nn")

    return pl.pallas_call(
        body,
        name=name,
        grid=(bsz, heads, t // SEQ_BLOCK),
        in_specs=[lg_spec, q_spec, k_spec, v_spec],
        out_specs=o_spec,
        out_shape=jax.ShapeDtypeStruct(v.shape, F32),
        compiler_params=_params("parallel", "parallel", "parallel"),
    )(lg, q, k, v)


def _ret_bwd(lg, q, k, v, do, name):
    bsz, t, hdk = q.shape
    heads = lg.shape[0]
    dk, dv = hdk // heads, v.shape[2] // heads
    lg_spec, q_spec, k_spec, v_spec, o_spec = _ret_specs(t, dk, dv, heads)

    def body(lg_ref, q_ref, k_ref, v_ref, do_ref, dq_ref, dk_ref, dv_ref):
        qb, kk, vv, dob = q_ref[0], k_ref[0], v_ref[0], do_ref[0]
        d = _decay(pl.program_id(2), SEQ_BLOCK, t, lg_ref[pl.program_id(1)])
        a = _dot(qb, kk, "nt") * d
        ds = _dot(dob, vv, "nt") * d
        dq_ref[0] = _dot(ds, kk, "nn")
        dkk = _dot(ds, qb, "tn")
        dvv = _dot(a, dob, "tn")

        @pl.when(pl.program_id(2) == 0)
        def _():
            dk_ref[0] = dkk
            dv_ref[0] = dvv

        @pl.when(pl.program_id(2) != 0)
        def _():
            dk_ref[0] += dkk
            dv_ref[0] += dvv

    return pl.pallas_call(
        body,
        name=name,
        grid=(bsz, heads, t // SEQ_BLOCK),
        in_specs=[lg_spec, q_spec, k_spec, v_spec, o_spec],
        out_specs=[q_spec, k_spec, v_spec],
        out_shape=[
            jax.ShapeDtypeStruct(q.shape, F32),
            jax.ShapeDtypeStruct(k.shape, F32),
            jax.ShapeDtypeStruct(v.shape, F32),
        ],
        compiler_params=_params("parallel", "parallel", "arbitrary"),
    )(lg, q, k, v, do)


def _make_retention(name):
    @jax.custom_vjp
    def op(lg, q, k, v):
        return _ret_fwd(lg, q, k, v, name + "_fwd")

    def fwd(lg, q, k, v):
        return op(lg, q, k, v), (lg, q, k, v)

    def bwd(res, do):
        lg = res[0]
        return (jnp.zeros_like(lg),) + tuple(_ret_bwd(*res, do, name + "_bwd"))

    op.defvjp(fwd, bwd)
    return op


def _adamw(w, g, m, v, name):
    r, c = w.shape
    tr = _pick(r, (256, 128, 64, 32, 16, 8))

    def body(w_ref, g_ref, m_ref, v_ref, d_ref, nm_ref, nv_ref):
        gv = g_ref[...]
        nm = ADAM_B1 * m_ref[...] + (1.0 - ADAM_B1) * gv
        nv = ADAM_B2 * v_ref[...] + (1.0 - ADAM_B2) * (gv * gv)
        m_hat = nm / (1.0 - ADAM_B1 ** ADAM_STEP)
        v_hat = nv / (1.0 - ADAM_B2 ** ADAM_STEP)
        d_ref[...] = -ADAM_LR * (m_hat / (jnp.sqrt(v_hat) + ADAM_EPS) + ADAM_WD * w_ref[...])
        nm_ref[...] = nm
        nv_ref[...] = nv

    spec = pl.BlockSpec((tr, c), lambda i: (i, 0))
    return pl.pallas_call(
        body,
        name=name,
        grid=(r // tr,),
        in_specs=[spec] * 4,
        out_specs=[spec] * 3,
        out_shape=[jax.ShapeDtypeStruct((r, c), F32)] * 3,
        compiler_params=_params("parallel"),
    )(w, g, m, v)


def _rope_tables(t, half, reps):
    inv = ROPE_BASE ** (-jnp.arange(half, dtype=F32) / half)
    ang = jnp.arange(t, dtype=jnp.int32).astype(F32)[:, None] * inv[None, :]
    return jnp.tile(jnp.cos(ang), (1, reps)), jnp.tile(jnp.sin(ang), (1, reps))


def _forward_loss(weights, x, target):
    bsz, seq, d = x.shape
    t_real = N_META + seq
    t = -(-t_real // SEQ_BLOCK) * SEQ_BLOCK
    r = bsz * t
    rep = _make_replicated_params("rep")((
        weights["ev_conv_b"], weights["ev_w_rg_a"], weights["ev_b_rg_a"], weights["ev_w_rg_x"], weights["ev_b_rg_x"],
        weights["ev_lru_lambda"], weights["ev_q_norm_g"], weights["ev_kv_norm_g"],
        weights["ln_mix_g"], weights["ln_mix_b"], weights["ln_mlp_g"], weights["ln_mlp_b"],
    ))
    conv_b, w_rg_a, b_rg_a, w_rg_x, b_rg_x, lru_lambda, q_norm_g, kv_norm_g, ln_mix_g, ln_mix_b, ln_mlp_g, ln_mlp_b = rep

    meta = _make_fsdp_param("meta")(weights["meta_tokens"])
    h = jnp.concatenate(
        [jnp.broadcast_to(meta[None], (bsz, N_META, d)), x, jnp.zeros((bsz, t - t_real, d), F32)], axis=1
    ).reshape(r, d)

    def tile_rows(tab):
        return jnp.tile(tab, (bsz, 1))

    ln_res = lambda nm: _make_rowwise(_ln_res_f, nm, 2, 0, 2)

    def mlp(hh, l):
        u = _make_fsdp_linear(True, f"mlp{l}_w1")(hh, weights["mlp_w1"][l])
        (a,), _ = _make_rowwise(_relu2_f, f"mlp{l}_act", 1, 0, 0)((u,), (), ())
        f = _make_fsdp_linear(False, f"mlp{l}_w2")(a, weights["mlp_w2"][l])
        (out,), _ = ln_res(f"mlp{l}_ln")((hh, f), (), (ln_mlp_g[l:l + 1], ln_mlp_b[l:l + 1]))
        return out

    lru_w = w_rg_a.shape[1] * w_rg_a.shape[2]
    q_rank, kv_rank = q_norm_g.shape[1], kv_norm_g.shape[1]
    p = _make_fsdp_linear(True, "ev_in")(h, weights["ev_w_in"][0])
    c0, c1, c2, c3 = lru_w, 2 * lru_w, 2 * lru_w + q_rank, 2 * lru_w + q_rank + kv_rank
    p_gate, p_rec, p_q, p_kv, p_kpe = p[:, :c0], p[:, c0:c1], p[:, c1:c2], p[:, c2:c3], p[:, c3:]

    conv_w = _make_fsdp_param("conv_w")(weights["ev_conv_w"][0])
    xc = _make_conv("conv")(p_rec.reshape(bsz, t, lru_w), conv_w, conv_b).reshape(r, lru_w)
    ga = _make_group_linear("rg_a")(xc, w_rg_a[0])
    gx = _make_group_linear("rg_x")(xc, w_rg_x[0])
    (a, bb), _ = _make_rowwise(_lru_gates_f, "lru_gates", 3, 0, 3)((ga, gx, xc), (), (b_rg_a, b_rg_x, lru_lambda))
    hh = _make_scan("lru_scan")(a.reshape(bsz, t, lru_w), bb.reshape(bsz, t, lru_w)).reshape(r, lru_w)
    (y_rec,), _ = _make_rowwise(_lru_out_f, "lru_out", 2, 0, 0)((hh, p_gate), (), ())

    (qn,), _ = _make_rowwise(_rmsnorm_f, "q_norm", 1, 0, 1)((p_q,), (), (q_norm_g,))
    (kvn,), _ = _make_rowwise(_rmsnorm_f, "kv_norm", 1, 0, 1)((p_kv,), (), (kv_norm_g,))
    q = _make_fsdp_linear(True, "ev_uq")(qn, weights["ev_w_uq"][0]).reshape(r, MLA_HEADS, MLA_NOPE + MLA_ROPE)
    kv = _make_fsdp_linear(True, "ev_ukv")(kvn, weights["ev_w_ukv"][0]).reshape(r, MLA_HEADS, MLA_NOPE + MLA_V)
    half = MLA_ROPE // 2
    cos_h, sin_h = _rope_tables(t, half, MLA_HEADS)
    q1 = q[:, :, MLA_NOPE:MLA_NOPE + half].reshape(r, MLA_HEADS * half)
    q2 = q[:, :, MLA_NOPE + half:].reshape(r, MLA_HEADS * half)
    (q1, q2), _ = _make_rowwise(_rope_pair_f, "rope_q", 2, 2, 0)((q1, q2), (tile_rows(cos_h), tile_rows(sin_h)), ())
    (k1, k2), _ = _make_rowwise(_rope_pair_f, "rope_k", 2, 2, 0)(
        (p_kpe[:, :half], p_kpe[:, half:]), (tile_rows(cos_h[:, :half]), tile_rows(sin_h[:, :half])), ())
    q_full = jnp.concatenate(
        [q[:, :, :MLA_NOPE], q1.reshape(r, MLA_HEADS, half), q2.reshape(r, MLA_HEADS, half)], axis=2)
    k_pe = jnp.broadcast_to(jnp.concatenate([k1, k2], axis=1)[:, None, :], (r, MLA_HEADS, MLA_ROPE))
    k_full = jnp.concatenate([kv[:, :, :MLA_NOPE], k_pe], axis=2)

    def heads_major(z):
        return z.reshape(bsz, t, MLA_HEADS, z.shape[-1]).transpose(0, 2, 1, 3).reshape(bsz * MLA_HEADS, t, z.shape[-1])

    o = _make_attention("mla")(heads_major(q_full), heads_major(k_full), heads_major(kv[:, :, MLA_NOPE:]))
    y_att = o.reshape(bsz, MLA_HEADS, t, MLA_V).transpose(0, 2, 1, 3).reshape(r, MLA_HEADS * MLA_V)
    mix = _make_fsdp_linear(False, "ev_out")(jnp.concatenate([y_rec, y_att], axis=1), weights["ev_w_out"][0])
    (h,), _ = ln_res("mix0_ln")((h, mix), (), (ln_mix_g[0:1], ln_mix_b[0:1]))
    h = mlp(h, 0)

    p = _make_fsdp_linear(True, "od_in")(h, weights["od_w_in"][0])
    qk = p.shape[1] // 6
    dk = qk // RET_HEADS
    cos2, sin2 = _rope_tables(t, dk // 2, 2)
    sin2 = jnp.concatenate([-sin2[:, :dk // 2], sin2[:, dk // 2:]], axis=1)
    (rq, rk), _ = _make_rowwise(_rope_ret_f, "rope_ret", 2, 2, 0)(
        (p[:, :qk], p[:, qk:2 * qk]), (tile_rows(cos2), tile_rows(sin2)), ())
    lg = jnp.log(1.0 - 2.0 ** (-5.0 - jnp.arange(RET_HEADS, dtype=F32)))
    o = _make_retention("ret")(lg, rq.reshape(bsz, t, qk), rk.reshape(bsz, t, qk), p[:, 2 * qk:4 * qk].reshape(bsz, t, 2 * qk))
    (y,), _ = _make_rowwise(_ret_out_f, "ret_out", 2, 0, 0)((o.reshape(r, 2 * qk), p[:, 4 * qk:]), (), ())
    mix = _make_fsdp_linear(False, "od_out")(y, weights["od_w_out"][0])
    (h,), _ = ln_res("mix1_ln")((h, mix), (), (ln_mix_g[1:2], ln_mix_b[1:2]))
    h = mlp(h, 1)

    pos = jnp.arange(t, dtype=jnp.int32)
    mask = tile_rows(((pos >= N_META) & (pos < t_real)).astype(F32)[:, None])
    tgt = jnp.concatenate(
        [jnp.zeros((bsz, N_META, d), F32), target, jnp.zeros((bsz, t - t_real, d), F32)], axis=1).reshape(r, d)
    _, (total,) = _make_rowwise(_loss_f, "loss", 1, 2, 0)((h,), (tgt, mask), ())
    return total[0, 0]


_WEIGHTS = ("meta_tokens", "ev_w_in", "ev_conv_w", "ev_conv_b", "ev_w_rg_a", "ev_b_rg_a", "ev_w_rg_x", "ev_b_rg_x",
            "ev_lru_lambda", "ev_q_norm_g", "ev_w_uq", "ev_kv_norm_g", "ev_w_ukv", "ev_w_out", "od_w_in", "od_w_out",
            "ln_mix_g", "ln_mix_b", "mlp_w1", "mlp_w2", "ln_mlp_g", "ln_mlp_b")


def kernel(x, meta_tokens, ev_w_in, ev_conv_w, ev_conv_b, ev_w_rg_a, ev_b_rg_a, ev_w_rg_x, ev_b_rg_x, ev_lru_lambda, ev_q_norm_g, ev_w_uq, ev_kv_norm_g, ev_w_ukv, ev_w_out, od_w_in, od_w_out, ln_mix_g, ln_mix_b, mlp_w1, mlp_w2, ln_mlp_g, ln_mlp_b, loss_target, m_meta_tokens, m_ev_w_in, m_ev_conv_w, m_ev_conv_b, m_ev_w_rg_a, m_ev_b_rg_a, m_ev_w_rg_x, m_ev_b_rg_x, m_ev_lru_lambda, m_ev_q_norm_g, m_ev_w_uq, m_ev_kv_norm_g, m_ev_w_ukv, m_ev_w_out, m_od_w_in, m_od_w_out, m_ln_mix_g, m_ln_mix_b, m_mlp_w1, m_mlp_w2, m_ln_mlp_g, m_ln_mlp_b, v_meta_tokens, v_ev_w_in, v_ev_conv_w, v_ev_conv_b, v_ev_w_rg_a, v_ev_b_rg_a, v_ev_w_rg_x, v_ev_b_rg_x, v_ev_lru_lambda, v_ev_q_norm_g, v_ev_w_uq, v_ev_kv_norm_g, v_ev_w_ukv, v_ev_w_out, v_od_w_in, v_od_w_out, v_ln_mix_g, v_ln_mix_b, v_mlp_w1, v_mlp_w2, v_ln_mlp_g, v_ln_mlp_b):
    args = locals()
    weights = {n: args[n] for n in _WEIGHTS}
    loss_local, (grad_w, grad_x) = jax.value_and_grad(_forward_loss, argnums=(0, 1))(weights, x, loss_target)
    loss = lax.psum(loss_local, MESH_AXES)
    delta, new_m, new_v = {}, {}, {}
    for n in _WEIGHTS:
        w, g, m, v = weights[n], grad_w[n], args["m_" + n], args["v_" + n]
        two_d = (-1, w.shape[-1])
        d2, m2, v2 = _adamw(w.reshape(two_d), g.reshape(two_d), m.reshape(two_d), v.reshape(two_d), "adamw_" + n)
        delta[n], new_m[n], new_v[n] = d2.reshape(w.shape), m2.reshape(w.shape), v2.reshape(w.shape)
    return (loss, grad_x, *[grad_w[n] for n in _WEIGHTS], *[delta[n] for n in _WEIGHTS],
            *[new_m[n] for n in _WEIGHTS], *[new_v[n] for n in _WEIGHTS])
```

```python
import functools
import math

import jax
import jax.numpy as jnp
from jax import lax
from jax.experimental import pallas as pl
from jax.experimental.pallas import tpu as pltpu

F32 = jnp.float32
BF16 = jnp.bfloat16

N_DEV = 8
MESH_AXES = ("x", "y", "c")
LANES = 128
SEQ_BLOCK = 128

N_META = 16
LRU_C = 8.0
MLA_HEADS = 8
MLA_NOPE = 64
MLA_ROPE = 32
MLA_V = 64
RET_HEADS = 4
ROPE_BASE = 10000.0
DEPTH = 2
DN_ALPHA = (2 * DEPTH) ** 0.25
EPS = 1e-5
NEG_INF = -1e30

ADAM_LR = 0.001
ADAM_B1 = 0.9
ADAM_B2 = 0.999
ADAM_EPS = 1e-08
ADAM_WD = 0.01
ADAM_STEP = 10

VMEM_LIMIT = 56 * 1024 * 1024


def _params(*sem):
    return pltpu.CompilerParams(dimension_semantics=sem, vmem_limit_bytes=VMEM_LIMIT)


def _pick(n, cands):
    for c in cands:
        if n % c == 0:
            return c
    return n


def _row_tile(r, width):
    cands = (256, 128, 64, 32, 16, 8) if width <= 1024 else (128, 64, 32, 16, 8)
    return _pick(r, cands)


_DIMS = {"nn": (((1,), (0,)), ((), ())), "nt": (((1,), (1,)), ((), ())), "tn": (((0,), (0,)), ((), ()))}


def _dot(a, b, mode):
    return lax.dot_general(a.astype(BF16), b.astype(BF16), _DIMS[mode], preferred_element_type=F32)


def _matmul(a, b, mode, name):
    if mode == "nn":
        (m, k), n = a.shape, b.shape[1]
    elif mode == "nt":
        (m, k), n = a.shape, b.shape[0]
    else:
        (k, m), n = a.shape, b.shape[1]
    tm = _pick(m, (1088, 1024, 544, 512, 272, 256, 128, 64, 32, 16, 8))
    tn = _pick(n, (512, 256, 128))
    tk = _pick(k, (1088, 1024, 544, 512, 272, 256, 128))
    nk = k // tk

    if mode == "nn":
        a_spec = pl.BlockSpec((tm, tk), lambda i, j, kk: (i, kk))
        b_spec = pl.BlockSpec((tk, tn), lambda i, j, kk: (kk, j))
    elif mode == "nt":
        a_spec = pl.BlockSpec((tm, tk), lambda i, j, kk: (i, kk))
        b_spec = pl.BlockSpec((tn, tk), lambda i, j, kk: (j, kk))
    else:
        a_spec = pl.BlockSpec((tk, tm), lambda i, j, kk: (kk, i))
        b_spec = pl.BlockSpec((tk, tn), lambda i, j, kk: (kk, j))

    def body(a_ref, b_ref, o_ref):
        kk = pl.program_id(2)
        part = _dot(a_ref[...], b_ref[...], mode)

        @pl.when(kk == 0)
        def _():
            o_ref[...] = part

        @pl.when(kk != 0)
        def _():
            o_ref[...] += part

    return pl.pallas_call(
        body,
        name=name,
        grid=(m // tm, n // tn, nk),
        in_specs=[a_spec, b_spec],
        out_specs=pl.BlockSpec((tm, tn), lambda i, j, kk: (i, j)),
        out_shape=jax.ShapeDtypeStruct((m, n), F32),
        compiler_params=_params("parallel", "parallel", "arbitrary"),
    )(a, b)


def _group_matmul(a, w, mode, name):
    if mode in ("nn", "nt"):
        g, dk, dn = w.shape
        m = a.shape[0]
        d_in, d_out = (dk, dn) if mode == "nn" else (dn, dk)
        tm = _pick(m, (1088, 1024, 544, 512, 272, 256, 128, 64, 32, 16, 8))

        def body(a_ref, w_ref, o_ref):
            o_ref[...] = _dot(a_ref[...], w_ref[0], mode)

        return pl.pallas_call(
            body,
            name=name,
            grid=(g, m // tm),
            in_specs=[pl.BlockSpec((tm, d_in), lambda h, i: (i, h)), pl.BlockSpec((1, dk, dn), lambda h, i: (h, 0, 0))],
            out_specs=pl.BlockSpec((tm, d_out), lambda h, i: (i, h)),
            out_shape=jax.ShapeDtypeStruct((m, g * d_out), F32),
            compiler_params=_params("parallel", "parallel"),
        )(a, w)
    b = w
    m = a.shape[0]
    dk = dn = LANES
    g = a.shape[1] // dk
    tm = _pick(m, (1088, 1024, 544, 512, 272, 256, 128, 64, 32, 16, 8))

    def body(a_ref, b_ref, o_ref):
        part = _dot(a_ref[...], b_ref[...], "tn")

        @pl.when(pl.program_id(1) == 0)
        def _():
            o_ref[0] = part

        @pl.when(pl.program_id(1) != 0)
        def _():
            o_ref[0] += part

    return pl.pallas_call(
        body,
        name=name,
        grid=(g, m // tm),
        in_specs=[pl.BlockSpec((tm, dk), lambda h, i: (i, h)), pl.BlockSpec((tm, dn), lambda h, i: (i, h))],
        out_specs=pl.BlockSpec((1, dk, dn), lambda h, i: (h, 0, 0)),
        out_shape=jax.ShapeDtypeStruct((g, dk, dn), F32),
        compiler_params=_params("parallel", "arbitrary"),
    )(a, b)


def _make_group_linear(name):
    @jax.custom_vjp
    def op(x, w):
        return _group_matmul(x, w, "nn", name + "_fwd")

    def fwd(x, w):
        return op(x, w), (x, w)

    def bwd(res, dy):
        x, w = res
        return _group_matmul(dy, w, "nt", name + "_dx"), _group_matmul(x, dy, "tn", name + "_dw")

    op.defvjp(fwd, bwd)
    return op


def _my_place():
    return lax.axis_index("x"), lax.axis_index("y"), lax.axis_index("c")


def _all_gather(shard, name):
    shape, dtype = shard.shape, shard.dtype

    def body(x_ref, out_ref, send_sems, recv_sems, local_sem):
        x, y, c = _my_place()
        me, sibling = (x, y, c), (x, y, 1 - c)
        chips = [(1 - x, y), (x, 1 - y), (1 - x, 1 - y)]

        def slot(px, py, pc):
            return out_ref.at[4 * px + 2 * py + pc]

        def copy(k, block, to, src=None):
            return pltpu.make_async_remote_copy(
                src_ref=slot(*block) if src is None else src,
                dst_ref=slot(*block),
                send_sem=send_sems.at[k],
                recv_sem=recv_sems.at[k],
                device_id=to,
                device_id_type=pl.DeviceIdType.MESH,
            )

        mine = pltpu.make_async_copy(x_ref, slot(*me), local_sem)
        mine.start()
        first = [copy(0, me, sibling, src=x_ref)]
        first += [copy(1 + j, me, (*chip, c), src=x_ref) for j, chip in enumerate(chips)]
        for cp in first:
            cp.start()
        passed = [copy(4 + j, (*chip, c), sibling) for j, chip in enumerate(chips)]
        for j, chip in enumerate(chips):
            copy(1 + j, (*chip, c), me).wait_recv()
            passed[j].start()
        copy(0, sibling, me).wait_recv()
        for j, chip in enumerate(chips):
            copy(4 + j, (*chip, 1 - c), me).wait_recv()
        for cp in first + passed:
            cp.wait_send()
        mine.wait()

    return pl.pallas_call(
        body,
        name=name,
        out_shape=jax.ShapeDtypeStruct((N_DEV,) + shape, dtype),
        in_specs=[pl.BlockSpec(memory_space=pl.ANY)],
        out_specs=pl.BlockSpec(memory_space=pl.ANY),
        scratch_shapes=[pltpu.SemaphoreType.DMA((7,)), pltpu.SemaphoreType.DMA((7,)), pltpu.SemaphoreType.DMA],
    )(shard)


def _all_to_all(stacked, name):
    def body(x_ref, out_ref, send_sems, recv_sems, local_sem):
        x, y, c = _my_place()
        me = 4 * x + 2 * y + c
        mine = pltpu.make_async_copy(x_ref.at[me], out_ref.at[me], local_sem)
        mine.start()
        copies = []
        for k in range(1, N_DEV):
            px, py, pc = x ^ ((k >> 2) & 1), y ^ ((k >> 1) & 1), c ^ (k & 1)
            peer = 4 * px + 2 * py + pc
            copies.append(
                pltpu.make_async_remote_copy(
                    src_ref=x_ref.at[peer],
                    dst_ref=out_ref.at[me],
                    send_sem=send_sems.at[k - 1],
                    recv_sem=recv_sems.at[k - 1],
                    device_id=(px, py, pc),
                    device_id_type=pl.DeviceIdType.MESH,
                )
            )
        for cp in copies:
            cp.start()
        for cp in copies:
            cp.wait_recv()
        for cp in copies:
            cp.wait_send()
        mine.wait()

    return pl.pallas_call(
        body,
        name=name,
        out_shape=jax.ShapeDtypeStruct(stacked.shape, stacked.dtype),
        in_specs=[pl.BlockSpec(memory_space=pl.ANY)],
        out_specs=pl.BlockSpec(memory_space=pl.ANY),
        scratch_shapes=[pltpu.SemaphoreType.DMA((7,)), pltpu.SemaphoreType.DMA((7,)), pltpu.SemaphoreType.DMA],
    )(stacked)


def _sum_blocks(stacked, name):
    _, r, c = stacked.shape
    tr = _pick(r, (256, 128, 64, 32, 16, 8))

    def body(x_ref, o_ref):
        s = [x_ref[j] for j in range(N_DEV)]
        o_ref[...] = ((s[0] + s[1]) + (s[2] + s[3])) + ((s[4] + s[5]) + (s[6] + s[7]))

    return pl.pallas_call(
        body,
        name=name,
        grid=(r // tr,),
        in_specs=[pl.BlockSpec((N_DEV, tr, c), lambda i: (0, i, 0))],
        out_specs=pl.BlockSpec((tr, c), lambda i: (i, 0)),
        out_shape=jax.ShapeDtypeStruct((r, c), stacked.dtype),
        compiler_params=_params("parallel"),
    )(stacked)


def _stack_cols(full):
    k, n8 = full.shape
    return full.reshape(k, N_DEV, n8 // N_DEV).transpose(1, 0, 2)


def _unstack_cols(stacked):
    j, k, n = stacked.shape
    return stacked.transpose(1, 0, 2).reshape(k, j * n)


def _gather_weight(shard, cols, name):
    g = _all_gather(shard.astype(BF16), name)
    return _unstack_cols(g) if cols else g.reshape(-1, shard.shape[1])


def _scatter_grad(full, cols, name):
    if cols:
        st = _stack_cols(full)
    else:
        st = full.reshape(N_DEV, full.shape[0] // N_DEV, full.shape[1])
    return _sum_blocks(_all_to_all(st, name + "_a2a"), name + "_sum")


def _make_fsdp_linear(cols, name):
    @jax.custom_vjp
    def op(x, w_shard):
        return _matmul(x, _gather_weight(w_shard, cols, name + "_ag"), "nn", name + "_fwd")

    def fwd(x, w_shard):
        w = _gather_weight(w_shard, cols, name + "_ag")
        return _matmul(x, w, "nn", name + "_fwd"), (x, w)

    def bwd(res, dy):
        x, w = res
        dx = _matmul(dy, w, "nt", name + "_dx")
        dw = _matmul(x, dy, "tn", name + "_dw")
        return dx, _scatter_grad(dw, cols, name + "_rs")

    op.defvjp(fwd, bwd)
    return op


def _make_fsdp_param(name):
    @jax.custom_vjp
    def op(shard):
        return _unstack_cols(_all_gather(shard, name + "_ag"))

    def fwd(shard):
        return op(shard), None

    def bwd(_, g):
        return (_scatter_grad(g, True, name + "_rs"),)

    op.defvjp(fwd, bwd)
    return op


def _make_replicated_params(name):
    @jax.custom_vjp
    def op(ps):
        return ps

    def fwd(ps):
        return ps, None

    def bwd(_, gs):
        flat = jnp.concatenate([g.reshape(-1) for g in gs])
        n = flat.shape[0]
        rows = -(-n // (256 * LANES)) * 256
        packed = jnp.pad(flat, (0, rows * LANES - n)).reshape(rows, LANES)
        total = _sum_blocks(_all_gather(packed, name + "_ag"), name + "_sum").reshape(-1)
        out, off = [], 0
        for g in gs:
            out.append(total[off:off + g.size].reshape(g.shape))
            off += g.size
        return (tuple(out),)

    op.defvjp(fwd, bwd)
    return op


def _make_rowwise(f, name, n_rows, n_tabs, n_pars):
    n_in = n_rows + n_tabs + n_pars

    def specs(args, tm):
        blocked = [pl.BlockSpec((tm, a.shape[1]), lambda i: (i, 0)) for a in args[: n_rows + n_tabs]]
        whole = [pl.BlockSpec(a.shape, lambda i: (0, 0)) for a in args[n_rows + n_tabs:]]
        return blocked + whole

    def out_struct(args, tm):
        blk = [jax.ShapeDtypeStruct((tm, a.shape[1]), a.dtype) for a in args[: n_rows + n_tabs]]
        blk += [jax.ShapeDtypeStruct(a.shape, a.dtype) for a in args[n_rows + n_tabs:]]
        return jax.eval_shape(f, *blk)

    def fwd_call(*args):
        r = args[0].shape[0]
        tm = _row_tile(r, max(a.shape[1] for a in args[:n_rows]))
        ro, so = out_struct(args, tm)

        def body(*refs):
            vals = [x[...] for x in refs[:n_in]]
            outs = refs[n_in:]
            rv, sv = f(*vals)
            for o, v in zip(outs[: len(ro)], rv):
                o[...] = v
            for o, v in zip(outs[len(ro):], sv):
                @pl.when(pl.program_id(0) == 0)
                def _(o=o, v=v):
                    o[...] = v

                @pl.when(pl.program_id(0) != 0)
                def _(o=o, v=v):
                    o[...] += v

        out_shape = [jax.ShapeDtypeStruct((r, s.shape[1]), s.dtype) for s in ro]
        out_shape += [jax.ShapeDtypeStruct(s.shape, s.dtype) for s in so]
        out_specs = [pl.BlockSpec((tm, s.shape[1]), lambda i: (i, 0)) for s in ro]
        out_specs += [pl.BlockSpec(s.shape, lambda i: (0, 0)) for s in so]
        res = pl.pallas_call(
            body,
            name=name + "_fwd",
            grid=(r // tm,),
            in_specs=specs(args, tm),
            out_specs=out_specs,
            out_shape=out_shape,
            compiler_params=_params("arbitrary" if so else "parallel"),
        )(*args)
        return tuple(res[: len(ro)]), tuple(res[len(ro):])

    def bwd_call(args, cots):
        r = args[0].shape[0]
        tm = _row_tile(r, max(a.shape[1] for a in args[:n_rows]))
        ro, so = out_struct(args, tm)
        crow, csum = cots
        rows, tabs, pars = args[:n_rows], args[n_rows:n_rows + n_tabs], args[n_rows + n_tabs:]
        n_c = len(crow) + len(csum)

        def body(*refs):
            vals = [x[...] for x in refs[:n_in]]
            cv = [x[...] for x in refs[n_in:n_in + n_c]]
            outs = refs[n_in + n_c:]
            tv = vals[n_rows:n_rows + n_tabs]

            def g(*dargs):
                return f(*dargs[:n_rows], *tv, *dargs[n_rows:])

            _, vjp = jax.vjp(g, *vals[:n_rows], *vals[n_rows + n_tabs:])
            d = vjp((tuple(cv[: len(crow)]), tuple(cv[len(crow):])))
            for o, v in zip(outs[:n_rows], d[:n_rows]):
                o[...] = v
            for o, v in zip(outs[n_rows:], d[n_rows:]):
                @pl.when(pl.program_id(0) == 0)
                def _(o=o, v=v):
                    o[...] = v

                @pl.when(pl.program_id(0) != 0)
                def _(o=o, v=v):
                    o[...] += v

        in_specs = specs(args, tm)
        in_specs += [pl.BlockSpec((tm, c.shape[1]), lambda i: (i, 0)) for c in crow]
        in_specs += [pl.BlockSpec(c.shape, lambda i: (0, 0)) for c in csum]
        out_shape = [jax.ShapeDtypeStruct(a.shape, a.dtype) for a in rows + pars]
        out_specs = [pl.BlockSpec((tm, a.shape[1]), lambda i: (i, 0)) for a in rows]
        out_specs += [pl.BlockSpec(a.shape, lambda i: (0, 0)) for a in pars]
        res = pl.pallas_call(
            body,
            name=name + "_bwd",
            grid=(r // tm,),
            in_specs=in_specs,
            out_specs=out_specs,
            out_shape=out_shape,
            compiler_params=_params("arbitrary" if pars else "parallel"),
        )(*args, *crow, *csum)
        return tuple(res[:n_rows]), tuple(res[n_rows:])

    @jax.custom_vjp
    def op(rows, tabs, pars):
        return fwd_call(*rows, *tabs, *pars)

    def fwd(rows, tabs, pars):
        return fwd_call(*rows, *tabs, *pars), (rows, tabs, pars)

    def bwd(res, cots):
        rows, tabs, pars = res
        drows, dpars = bwd_call(tuple(rows) + tuple(tabs) + tuple(pars), cots)
        return drows, tuple(jnp.zeros_like(t) for t in tabs), dpars

    op.defvjp(fwd, bwd)
    return op


def _sigmoid(x):
    return 0.5 * (jnp.tanh(0.5 * x) + 1.0)


@jax.custom_jvp
def _softplus(x):
    e = jnp.exp(-jnp.abs(x))
    u = 1.0 + e
    log1p_e = jnp.where(u == 1.0, e, e * jnp.log(u) / jnp.where(u == 1.0, 1.0, u - 1.0))
    return jnp.maximum(x, 0.0) + log1p_e


@_softplus.defjvp
def _softplus_jvp(primals, tangents):
    (x,), (t,) = primals, tangents
    return _softplus(x), t * _sigmoid(x)


def _gelu(x):
    return 0.5 * x * (1.0 + jnp.tanh(math.sqrt(2.0 / math.pi) * (x + 0.044715 * (x * x * x))))


def _ln_res_f(h, mix, g, b):
    z = DN_ALPHA * h + mix
    mu = jnp.mean(z, axis=-1, keepdims=True)
    zc = z - mu
    var = jnp.mean(zc * zc, axis=-1, keepdims=True)
    return (zc * lax.rsqrt(var + EPS) * g + b,), ()


def _rmsnorm_f(x, g):
    return (x * lax.rsqrt(jnp.mean(x * x, axis=-1, keepdims=True) + EPS) * g,), ()


def _lru_gates_f(ga, gx, xc, b_a, b_x, lam):
    r = _sigmoid(ga + b_a)
    i = _sigmoid(gx + b_x)
    log_a = -LRU_C * r * _softplus(-lam)
    a = jnp.exp(log_a)
    one_minus_a2 = jnp.tanh(-log_a) * (jnp.exp(2.0 * log_a) + 1.0)
    return (a, jnp.sqrt(one_minus_a2) * (i * xc)), ()


def _lru_out_f(hh, p_gate):
    return (hh * _gelu(p_gate),), ()


def _rope_pair_f(x1, x2, cos, sin):
    return (x1 * cos - x2 * sin, x1 * sin + x2 * cos), ()


def _rope_ret_f(q, k, cos2, sin2):
    d = cos2.shape[1]
    half = d // 2
    k_scale = d ** -0.5

    def rope(x):
        outs = []
        for h in range(x.shape[1] // d):
            xh = x[:, h * d:(h + 1) * d]
            rot = jnp.concatenate([xh[:, half:], xh[:, :half]], axis=1)
            outs.append(xh * cos2 + rot * sin2)
        return jnp.concatenate(outs, axis=1)

    return (rope(q), rope(k) * k_scale), ()


def _ret_out_f(o, g):
    d = o.shape[1] // RET_HEADS
    outs = []
    for h in range(RET_HEADS):
        oh = o[:, h * d:(h + 1) * d]
        outs.append(oh * lax.rsqrt(jnp.mean(oh * oh, axis=-1, keepdims=True) + EPS))
    y = jnp.concatenate(outs, axis=1)
    return (g * _sigmoid(g) * y,), ()


def _relu2_f(u):
    r = jnp.maximum(u, 0.0)
    return (r * r,), ()


def _loss_f(y, t, mask):
    e = (y - t) * mask
    per_row = jnp.sum(e * e, axis=-1, keepdims=True) * (0.5 / y.shape[1])
    total = jnp.sum(per_row, axis=0, keepdims=True)
    return (), (jnp.broadcast_to(total, (1, LANES)),)


def _shift_down(x, s):
    if s == 0:
        return x
    t = x.shape[0]
    row = lax.broadcasted_iota(jnp.int32, x.shape, 0)
    return jnp.where(row >= s, pltpu.roll(x, s, 0), 0.0)


def _shift_up(x, s):
    if s == 0:
        return x
    t = x.shape[0]
    row = lax.broadcasted_iota(jnp.int32, x.shape, 0)
    return jnp.where(row < t - s, pltpu.roll(x, t - s, 0), 0.0)


def _conv_fwd(x, w, b, name):
    bsz, t, c = x.shape
    width = w.shape[0]

    def body(x_ref, w_ref, b_ref, y_ref):
        xv = x_ref[0]
        acc = jnp.broadcast_to(b_ref[...], xv.shape)
        for k in range(width):
            acc = acc + w_ref[k:k + 1, :] * _shift_down(xv, width - 1 - k)
        y_ref[0] = acc

    return pl.pallas_call(
        body,
        name=name,
        grid=(bsz, c // LANES),
        in_specs=[
            pl.BlockSpec((1, t, LANES), lambda i, j: (i, 0, j)),
            pl.BlockSpec((width, LANES), lambda i, j: (0, j)),
            pl.BlockSpec((1, LANES), lambda i, j: (0, j)),
        ],
        out_specs=pl.BlockSpec((1, t, LANES), lambda i, j: (i, 0, j)),
        out_shape=jax.ShapeDtypeStruct(x.shape, F32),
        compiler_params=_params("parallel", "parallel"),
    )(x, w, b)


def _conv_bwd(x, w, dy, name):
    bsz, t, c = x.shape
    width = w.shape[0]

    def body(x_ref, w_ref, dy_ref, dx_ref, dw_ref, db_ref):
        xv, g = x_ref[0], dy_ref[0]
        dx = jnp.zeros_like(xv)
        dws = []
        for k in range(width):
            s = width - 1 - k
            dx = dx + w_ref[k:k + 1, :] * _shift_up(g, s)
            dws.append(jnp.sum(g * _shift_down(xv, s), axis=0, keepdims=True))
        dx_ref[0] = dx
        dw = jnp.concatenate(dws, axis=0)
        db = jnp.sum(g, axis=0, keepdims=True)

        @pl.when(pl.program_id(1) == 0)
        def _():
            dw_ref[...] = dw
            db_ref[...] = db

        @pl.when(pl.program_id(1) != 0)
        def _():
            dw_ref[...] += dw
            db_ref[...] += db

    return pl.pallas_call(
        body,
        name=name,
        grid=(c // LANES, bsz),
        in_specs=[
            pl.BlockSpec((1, t, LANES), lambda j, i: (i, 0, j)),
            pl.BlockSpec((width, LANES), lambda j, i: (0, j)),
            pl.BlockSpec((1, t, LANES), lambda j, i: (i, 0, j)),
        ],
        out_specs=[
            pl.BlockSpec((1, t, LANES), lambda j, i: (i, 0, j)),
            pl.BlockSpec((width, LANES), lambda j, i: (0, j)),
            pl.BlockSpec((1, LANES), lambda j, i: (0, j)),
        ],
        out_shape=[
            jax.ShapeDtypeStruct(x.shape, F32),
            jax.ShapeDtypeStruct(w.shape, F32),
            jax.ShapeDtypeStruct((1, c), F32),
        ],
        compiler_params=_params("parallel", "arbitrary"),
    )(x, w, dy)


def _make_conv(name):
    @jax.custom_vjp
    def op(x, w, b):
        return _conv_fwd(x, w, b, name + "_fwd")

    def fwd(x, w, b):
        return op(x, w, b), (x, w)

    def bwd(res, dy):
        x, w = res
        return tuple(_conv_bwd(x, w, dy, name + "_bwd"))

    op.defvjp(fwd, bwd)
    return op


def _scan_fwd(a, b, name):
    bsz, t, c = a.shape
    cw = _pick(c, (4 * LANES, 2 * LANES, LANES))

    def body(a_ref, b_ref, h_ref):
        def step(i, h):
            h = a_ref[0, pl.ds(i, 1), :] * h + b_ref[0, pl.ds(i, 1), :]
            h_ref[0, pl.ds(i, 1), :] = h
            return h

        lax.fori_loop(0, t, step, jnp.zeros((1, cw), F32), unroll=8)

    spec = pl.BlockSpec((1, t, cw), lambda i, j: (i, 0, j))
    return pl.pallas_call(
        body,
        name=name,
        grid=(bsz, c // cw),
        in_specs=[spec, spec],
        out_specs=spec,
        out_shape=jax.ShapeDtypeStruct(a.shape, F32),
        compiler_params=_params("parallel", "parallel"),
    )(a, b)


def _scan_bwd(a, h, g, name):
    bsz, t, c = a.shape
    cw = _pick(c, (2 * LANES, LANES))

    def body(a_ref, h_ref, g_ref, da_ref, db_ref):
        def step(n, carry):
            i = t - 1 - n
            lam = g_ref[0, pl.ds(i, 1), :] + carry
            db_ref[0, pl.ds(i, 1), :] = lam
            prev = jnp.where(i > 0, h_ref[0, pl.ds(jnp.maximum(i - 1, 0), 1), :], 0.0)
            da_ref[0, pl.ds(i, 1), :] = lam * prev
            return a_ref[0, pl.ds(i, 1), :] * lam

        lax.fori_loop(0, t, step, jnp.zeros((1, cw), F32), unroll=8)

    spec = pl.BlockSpec((1, t, cw), lambda i, j: (i, 0, j))
    return pl.pallas_call(
        body,
        name=name,
        grid=(bsz, c // cw),
        in_specs=[spec, spec, spec],
        out_specs=[spec, spec],
        out_shape=[jax.ShapeDtypeStruct(a.shape, F32)] * 2,
        compiler_params=_params("parallel", "parallel"),
    )(a, h, g)


def _make_scan(name):
    @jax.custom_vjp
    def op(a, b):
        return _scan_fwd(a, b, name + "_fwd")

    def fwd(a, b):
        h = op(a, b)
        return h, (a, h)

    def bwd(res, g):
        a, h = res
        da, db = _scan_bwd(a, h, g, name + "_bwd")
        return da, db

    op.defvjp(fwd, bwd)
    return op


def _query_blocks(t):
    blocks, start = [], 0
    while start < t:
        rows = 2 * SEQ_BLOCK if start + 2 * SEQ_BLOCK <= t else SEQ_BLOCK
        blocks.append((start, rows))
        start += rows
    return blocks


def _attn_probs(q, k, start, scale):
    tq, tk = q.shape[0], k.shape[0]
    s = _dot(q, k, "nt") * scale
    qpos = start + lax.broadcasted_iota(jnp.int32, (tq, tk), 0)
    kpos = lax.broadcasted_iota(jnp.int32, (tq, tk), 1)
    s = jnp.where(kpos <= qpos, s, NEG_INF)
    e = jnp.exp(s - jnp.max(s, axis=-1, keepdims=True))
    return e / jnp.sum(e, axis=-1, keepdims=True)


def _attn_fwd(q, k, v, name):
    g, t, dq = q.shape
    dv = v.shape[2]
    scale = dq ** -0.5

    def body(q_ref, k_ref, v_ref, o_ref):
        for start, rows in _query_blocks(t):
            n = start + rows
            p = _attn_probs(q_ref[0, start:n, :], k_ref[0, :n, :], start, scale)
            o_ref[0, start:n, :] = _dot(p, v_ref[0, :n, :], "nn")

    return pl.pallas_call(
        body,
        name=name,
        grid=(g,),
        in_specs=[
            pl.BlockSpec((1, t, dq), lambda b: (b, 0, 0)),
            pl.BlockSpec((1, t, dq), lambda b: (b, 0, 0)),
            pl.BlockSpec((1, t, dv), lambda b: (b, 0, 0)),
        ],
        out_specs=pl.BlockSpec((1, t, dv), lambda b: (b, 0, 0)),
        out_shape=jax.ShapeDtypeStruct((g, t, dv), F32),
        compiler_params=_params("parallel"),
    )(q, k, v)


def _attn_bwd(q, k, v, do, name):
    g, t, dq = q.shape
    dv = v.shape[2]
    scale = dq ** -0.5

    def body(q_ref, k_ref, v_ref, do_ref, dq_ref, dk_ref, dv_ref):
        for start, rows in reversed(_query_blocks(t)):
            n = start + rows
            qb, dob = q_ref[0, start:n, :], do_ref[0, start:n, :]
            kk, vv = k_ref[0, :n, :], v_ref[0, :n, :]
            p = _attn_probs(qb, kk, start, scale)
            dp = _dot(dob, vv, "nt")
            ds = p * (dp - jnp.sum(dp * p, axis=-1, keepdims=True)) * scale
            dq_ref[0, start:n, :] = _dot(ds, kk, "nn")
            if n == t:
                dk_ref[0] = _dot(ds, qb, "tn")
                dv_ref[0] = _dot(p, dob, "tn")
            else:
                dk_ref[0, :n, :] += _dot(ds, qb, "tn")
                dv_ref[0, :n, :] += _dot(p, dob, "tn")

    qk_spec = pl.BlockSpec((1, t, dq), lambda b: (b, 0, 0))
    v_spec = pl.BlockSpec((1, t, dv), lambda b: (b, 0, 0))
    return pl.pallas_call(
        body,
        name=name,
        grid=(g,),
        in_specs=[qk_spec, qk_spec, v_spec, v_spec],
        out_specs=[qk_spec, qk_spec, v_spec],
        out_shape=[
            jax.ShapeDtypeStruct(q.shape, F32),
            jax.ShapeDtypeStruct(k.shape, F32),
            jax.ShapeDtypeStruct(v.shape, F32),
        ],
        compiler_params=_params("parallel"),
    )(q, k, v, do)


def _make_attention(name):
    @jax.custom_vjp
    def op(q, k, v):
        return _attn_fwd(q, k, v, name + "_fwd")

    def fwd(q, k, v):
        return op(q, k, v), (q, k, v)

    def bwd(res, do):
        return tuple(_attn_bwd(*res, do, name + "_bwd"))

    op.defvjp(fwd, bwd)
    return op


RET_KEY_CHUNK = 512


def _key_chunks(t):
    return [(c, min(RET_KEY_CHUNK, t - c)) for c in range(0, t, RET_KEY_CHUNK)]


def _decay(q0, k0, tq, tk, log_gamma):
    qpos = q0 + lax.broadcasted_iota(jnp.int32, (tq, tk), 0)
    kpos = k0 + lax.broadcasted_iota(jnp.int32, (tq, tk), 1)
    diff = qpos - kpos
    return jnp.where(diff >= 0, jnp.exp(log_gamma * jnp.maximum(diff, 0).astype(F32)), 0.0)


def _ret_specs(t, dk, dv, heads):
    tq = SEQ_BLOCK
    return (
        pl.BlockSpec(memory_space=pltpu.SMEM),
        pl.BlockSpec((1, tq, dk), lambda b, h, i: (b, i, h)),
        pl.BlockSpec((1, t, dk), lambda b, h, i: (b, 0, h)),
        pl.BlockSpec((1, t, dv), lambda b, h, i: (b, 0, h)),
        pl.BlockSpec((1, tq, dv), lambda b, h, i: (b, i, h)),
    )


def _ret_fwd(lg, q, k, v, name):
    bsz, t, hdk = q.shape
    heads = lg.shape[0]
    dk, dv = hdk // heads, v.shape[2] // heads
    lg_spec, q_spec, k_spec, v_spec, o_spec = _ret_specs(t, dk, dv, heads)

    def body(lg_ref, q_ref, k_ref, v_ref, o_ref):
        q0 = pl.program_id(2) * SEQ_BLOCK
        lgh = lg_ref[pl.program_id(1)]
        o_ref[0] = jnp.zeros((SEQ_BLOCK, dv), F32)
        for c0, cw in _key_chunks(t):
            @pl.when(c0 < q0 + SEQ_BLOCK)
            def _(c0=c0, cw=cw):
                d = _decay(q0, c0, SEQ_BLOCK, cw, lgh)
                a = _dot(q_ref[0], k_ref[0, c0:c0 + cw, :], "nt") * d
                o_ref[0] += _dot(a, v_ref[0, c0:c0 + cw, :], "nn")

    return pl.pallas_call(
        body,
        name=name,
        grid=(bsz, heads, t // SEQ_BLOCK),
        in_specs=[lg_spec, q_spec, k_spec, v_spec],
        out_specs=o_spec,
        out_shape=jax.ShapeDtypeStruct(v.shape, F32),
        compiler_params=_params("parallel", "parallel", "parallel"),
    )(lg, q, k, v)


def _ret_bwd(lg, q, k, v, do, name):
    bsz, t, hdk = q.shape
    heads = lg.shape[0]
    dk, dv = hdk // heads, v.shape[2] // heads
    lg_spec, q_spec, k_spec, v_spec, o_spec = _ret_specs(t, dk, dv, heads)

    def body(lg_ref, q_ref, k_ref, v_ref, do_ref, dq_ref, dk_ref, dv_ref):
        q0 = pl.program_id(2) * SEQ_BLOCK
        lgh = lg_ref[pl.program_id(1)]

        @pl.when(pl.program_id(2) == 0)
        def _():
            dk_ref[0] = jnp.zeros((t, dk), F32)
            dv_ref[0] = jnp.zeros((t, dv), F32)

        dq_ref[0] = jnp.zeros((SEQ_BLOCK, dk), F32)
        for c0, cw in _key_chunks(t):
            @pl.when(c0 < q0 + SEQ_BLOCK)
            def _(c0=c0, cw=cw):
                qb, dob = q_ref[0], do_ref[0]
                kk, vv = k_ref[0, c0:c0 + cw, :], v_ref[0, c0:c0 + cw, :]
                d = _decay(q0, c0, SEQ_BLOCK, cw, lgh)
                a = _dot(qb, kk, "nt") * d
                ds = _dot(dob, vv, "nt") * d
                dq_ref[0] += _dot(ds, kk, "nn")
                dk_ref[0, c0:c0 + cw, :] += _dot(ds, qb, "tn")
                dv_ref[0, c0:c0 + cw, :] += _dot(a, dob, "tn")

    return pl.pallas_call(
        body,
        name=name,
        grid=(bsz, heads, t // SEQ_BLOCK),
        in_specs=[lg_spec, q_spec, k_spec, v_spec, o_spec],
        out_specs=[q_spec, k_spec, v_spec],
        out_shape=[
            jax.ShapeDtypeStruct(q.shape, F32),
            jax.ShapeDtypeStruct(k.shape, F32),
            jax.ShapeDtypeStruct(v.shape, F32),
        ],
        compiler_params=_params("parallel", "parallel", "arbitrary"),
    )(lg, q, k, v, do)


def _make_retention(name):
    @jax.custom_vjp
    def op(lg, q, k, v):
        return _ret_fwd(lg, q, k, v, name + "_fwd")

    def fwd(lg, q, k, v):
        return op(lg, q, k, v), (lg, q, k, v)

    def bwd(res, do):
        lg = res[0]
        return (jnp.zeros_like(lg),) + tuple(_ret_bwd(*res, do, name + "_bwd"))

    op.defvjp(fwd, bwd)
    return op


def _adamw(w, g, m, v, name):
    r, c = w.shape
    tr = _pick(r, (256, 128, 64, 32, 16, 8))

    def body(w_ref, g_ref, m_ref, v_ref, d_ref, nm_ref, nv_ref):
        gv = g_ref[...]
        nm = ADAM_B1 * m_ref[...] + (1.0 - ADAM_B1) * gv
        nv = ADAM_B2 * v_ref[...] + (1.0 - ADAM_B2) * (gv * gv)
        m_hat = nm / (1.0 - ADAM_B1 ** ADAM_STEP)
        v_hat = nv / (1.0 - ADAM_B2 ** ADAM_STEP)
        d_ref[...] = -ADAM_LR * (m_hat / (jnp.sqrt(v_hat) + ADAM_EPS) + ADAM_WD * w_ref[...])
        nm_ref[...] = nm
        nv_ref[...] = nv

    spec = pl.BlockSpec((tr, c), lambda i: (i, 0))
    return pl.pallas_call(
        body,
        name=name,
        grid=(r // tr,),
        in_specs=[spec] * 4,
        out_specs=[spec] * 3,
        out_shape=[jax.ShapeDtypeStruct((r, c), F32)] * 3,
        compiler_params=_params("parallel"),
    )(w, g, m, v)


def _rope_tables(t, half, reps):
    inv = ROPE_BASE ** (-jnp.arange(half, dtype=F32) / half)
    ang = jnp.arange(t, dtype=jnp.int32).astype(F32)[:, None] * inv[None, :]
    return jnp.tile(jnp.cos(ang), (1, reps)), jnp.tile(jnp.sin(ang), (1, reps))


def _forward_loss(weights, x, target):
    bsz, seq, d = x.shape
    t_real = N_META + seq
    t = -(-t_real // SEQ_BLOCK) * SEQ_BLOCK
    r = bsz * t
    rep = _make_replicated_params("rep")((
        weights["ev_conv_b"], weights["ev_w_rg_a"], weights["ev_b_rg_a"], weights["ev_w_rg_x"], weights["ev_b_rg_x"],
        weights["ev_lru_lambda"], weights["ev_q_norm_g"], weights["ev_kv_norm_g"],
        weights["ln_mix_g"], weights["ln_mix_b"], weights["ln_mlp_g"], weights["ln_mlp_b"],
    ))
    conv_b, w_rg_a, b_rg_a, w_rg_x, b_rg_x, lru_lambda, q_norm_g, kv_norm_g, ln_mix_g, ln_mix_b, ln_mlp_g, ln_mlp_b = rep

    meta = _make_fsdp_param("meta")(weights["meta_tokens"])
    h = jnp.concatenate(
        [jnp.broadcast_to(meta[None], (bsz, N_META, d)), x, jnp.zeros((bsz, t - t_real, d), F32)], axis=1
    ).reshape(r, d)

    def tile_rows(tab):
        return jnp.tile(tab, (bsz, 1))

    ln_res = lambda nm: _make_rowwise(_ln_res_f, nm, 2, 0, 2)

    def mlp(hh, l):
        u = _make_fsdp_linear(True, f"mlp{l}_w1")(hh, weights["mlp_w1"][l])
        (a,), _ = _make_rowwise(_relu2_f, f"mlp{l}_act", 1, 0, 0)((u,), (), ())
        f = _make_fsdp_linear(False, f"mlp{l}_w2")(a, weights["mlp_w2"][l])
        (out,), _ = ln_res(f"mlp{l}_ln")((hh, f), (), (ln_mlp_g[l:l + 1], ln_mlp_b[l:l + 1]))
        return out

    lru_w = w_rg_a.shape[1] * w_rg_a.shape[2]
    q_rank, kv_rank = q_norm_g.shape[1], kv_norm_g.shape[1]
    p = _make_fsdp_linear(True, "ev_in")(h, weights["ev_w_in"][0])
    c0, c1, c2, c3 = lru_w, 2 * lru_w, 2 * lru_w + q_rank, 2 * lru_w + q_rank + kv_rank
    p_gate, p_rec, p_q, p_kv, p_kpe = p[:, :c0], p[:, c0:c1], p[:, c1:c2], p[:, c2:c3], p[:, c3:]

    conv_w = _make_fsdp_param("conv_w")(weights["ev_conv_w"][0])
    xc = _make_conv("conv")(p_rec.reshape(bsz, t, lru_w), conv_w, conv_b).reshape(r, lru_w)
    ga = _make_group_linear("rg_a")(xc, w_rg_a[0])
    gx = _make_group_linear("rg_x")(xc, w_rg_x[0])
    (a, bb), _ = _make_rowwise(_lru_gates_f, "lru_gates", 3, 0, 3)((ga, gx, xc), (), (b_rg_a, b_rg_x, lru_lambda))
    hh = _make_scan("lru_scan")(a.reshape(bsz, t, lru_w), bb.reshape(bsz, t, lru_w)).reshape(r, lru_w)
    (y_rec,), _ = _make_rowwise(_lru_out_f, "lru_out", 2, 0, 0)((hh, p_gate), (), ())

    (qn,), _ = _make_rowwise(_rmsnorm_f, "q_norm", 1, 0, 1)((p_q,), (), (q_norm_g,))
    (kvn,), _ = _make_rowwise(_rmsnorm_f, "kv_norm", 1, 0, 1)((p_kv,), (), (kv_norm_g,))
    q = _make_fsdp_linear(True, "ev_uq")(qn, weights["ev_w_uq"][0]).reshape(r, MLA_HEADS, MLA_NOPE + MLA_ROPE)
    kv = _make_fsdp_linear(True, "ev_ukv")(kvn, weights["ev_w_ukv"][0]).reshape(r, MLA_HEADS, MLA_NOPE + MLA_V)
    half = MLA_ROPE // 2
    cos_h, sin_h = _rope_tables(t, half, MLA_HEADS)
    q1 = q[:, :, MLA_NOPE:MLA_NOPE + half].reshape(r, MLA_HEADS * half)
    q2 = q[:, :, MLA_NOPE + half:].reshape(r, MLA_HEADS * half)
    (q1, q2), _ = _make_rowwise(_rope_pair_f, "rope_q", 2, 2, 0)((q1, q2), (tile_rows(cos_h), tile_rows(sin_h)), ())
    (k1, k2), _ = _make_rowwise(_rope_pair_f, "rope_k", 2, 2, 0)(
        (p_kpe[:, :half], p_kpe[:, half:]), (tile_rows(cos_h[:, :half]), tile_rows(sin_h[:, :half])), ())
    q_full = jnp.concatenate(
        [q[:, :, :MLA_NOPE], q1.reshape(r, MLA_HEADS, half), q2.reshape(r, MLA_HEADS, half)], axis=2)
    k_pe = jnp.broadcast_to(jnp.concatenate([k1, k2], axis=1)[:, None, :], (r, MLA_HEADS, MLA_ROPE))
    k_full = jnp.concatenate([kv[:, :, :MLA_NOPE], k_pe], axis=2)

    def heads_major(z):
        return z.reshape(bsz, t, MLA_HEADS, z.shape[-1]).transpose(0, 2, 1, 3).reshape(bsz * MLA_HEADS, t, z.shape[-1])

    o = _make_attention("mla")(heads_major(q_full), heads_major(k_full), heads_major(kv[:, :, MLA_NOPE:]))
    y_att = o.reshape(bsz, MLA_HEADS, t, MLA_V).transpose(0, 2, 1, 3).reshape(r, MLA_HEADS * MLA_V)
    mix = _make_fsdp_linear(False, "ev_out")(jnp.concatenate([y_rec, y_att], axis=1), weights["ev_w_out"][0])
    (h,), _ = ln_res("mix0_ln")((h, mix), (), (ln_mix_g[0:1], ln_mix_b[0:1]))
    h = mlp(h, 0)

    p = _make_fsdp_linear(True, "od_in")(h, weights["od_w_in"][0])
    qk = p.shape[1] // 6
    dk = qk // RET_HEADS
    cos2, sin2 = _rope_tables(t, dk // 2, 2)
    sin2 = jnp.concatenate([-sin2[:, :dk // 2], sin2[:, dk // 2:]], axis=1)
    (rq, rk), _ = _make_rowwise(_rope_ret_f, "rope_ret", 2, 2, 0)(
        (p[:, :qk], p[:, qk:2 * qk]), (tile_rows(cos2), tile_rows(sin2)), ())
    lg = jnp.log(1.0 - 2.0 ** (-5.0 - jnp.arange(RET_HEADS, dtype=F32)))
    o = _make_retention("ret")(lg, rq.reshape(bsz, t, qk), rk.reshape(bsz, t, qk), p[:, 2 * qk:4 * qk].reshape(bsz, t, 2 * qk))
    (y,), _ = _make_rowwise(_ret_out_f, "ret_out", 2, 0, 0)((o.reshape(r, 2 * qk), p[:, 4 * qk:]), (), ())
    mix = _make_fsdp_linear(False, "od_out")(y, weights["od_w_out"][0])
    (h,), _ = ln_res("mix1_ln")((h, mix), (), (ln_mix_g[1:2], ln_mix_b[1:2]))
    h = mlp(h, 1)

    pos = jnp.arange(t, dtype=jnp.int32)
    mask = tile_rows(((pos >= N_META) & (pos < t_real)).astype(F32)[:, None])
    tgt = jnp.concatenate(
        [jnp.zeros((bsz, N_META, d), F32), target, jnp.zeros((bsz, t - t_real, d), F32)], axis=1).reshape(r, d)
    _, (total,) = _make_rowwise(_loss_f, "loss", 1, 2, 0)((h,), (tgt, mask), ())
    return total[0, 0]


_WEIGHTS = ("meta_tokens", "ev_w_in", "ev_conv_w", "ev_conv_b", "ev_w_rg_a", "ev_b_rg_a", "ev_w_rg_x", "ev_b_rg_x",
            "ev_lru_lambda", "ev_q_norm_g", "ev_w_uq", "ev_kv_norm_g", "ev_w_ukv", "ev_w_out", "od_w_in", "od_w_out",
            "ln_mix_g", "ln_mix_b", "mlp_w1", "mlp_w2", "ln_mlp_g", "ln_mlp_b")


def kernel(x, meta_tokens, ev_w_in, ev_conv_w, ev_conv_b, ev_w_rg_a, ev_b_rg_a, ev_w_rg_x, ev_b_rg_x, ev_lru_lambda, ev_q_norm_g, ev_w_uq, ev_kv_norm_g, ev_w_ukv, ev_w_out, od_w_in, od_w_out, ln_mix_g, ln_mix_b, mlp_w1, mlp_w2, ln_mlp_g, ln_mlp_b, loss_target, m_meta_tokens, m_ev_w_in, m_ev_conv_w, m_ev_conv_b, m_ev_w_rg_a, m_ev_b_rg_a, m_ev_w_rg_x, m_ev_b_rg_x, m_ev_lru_lambda, m_ev_q_norm_g, m_ev_w_uq, m_ev_kv_norm_g, m_ev_w_ukv, m_ev_w_out, m_od_w_in, m_od_w_out, m_ln_mix_g, m_ln_mix_b, m_mlp_w1, m_mlp_w2, m_ln_mlp_g, m_ln_mlp_b, v_meta_tokens, v_ev_w_in, v_ev_conv_w, v_ev_conv_b, v_ev_w_rg_a, v_ev_b_rg_a, v_ev_w_rg_x, v_ev_b_rg_x, v_ev_lru_lambda, v_ev_q_norm_g, v_ev_w_uq, v_ev_kv_norm_g, v_ev_w_ukv, v_ev_w_out, v_od_w_in, v_od_w_out, v_ln_mix_g, v_ln_mix_b, v_mlp_w1, v_mlp_w2, v_ln_mlp_g, v_ln_mlp_b):
    args = locals()
    weights = {n: args[n] for n in _WEIGHTS}
    loss_local, (grad_w, grad_x) = jax.value_and_grad(_forward_loss, argnums=(0, 1))(weights, x, loss_target)
    loss = lax.psum(loss_local, MESH_AXES)
    delta, new_m, new_v = {}, {}, {}
    for n in _WEIGHTS:
        w, g, m, v = weights[n], grad_w[n], args["m_" + n], args["v_" + n]
        two_d = (-1, w.shape[-1])
        d2, m2, v2 = _adamw(w.reshape(two_d), g.reshape(two_d), m.reshape(two_d), v.reshape(two_d), "adamw_" + n)
        delta[n], new_m[n], new_v[n] = d2.reshape(w.shape), m2.reshape(w.shape), v2.reshape(w.shape)
    return (loss, grad_x, *[grad_w[n] for n in _WEIGHTS], *[delta[n] for n in _WEIGHTS],
            *[new_m[n] for n in _WEIGHTS], *[new_v[n] for n in _WEIGHTS])
```

```python
import functools
import math

import jax
import jax.numpy as jnp
from jax import lax
from jax.experimental import pallas as pl
from jax.experimental.pallas import tpu as pltpu

F32 = jnp.float32
BF16 = jnp.bfloat16

N_DEV = 8
MESH_AXES = ("x", "y", "c")
LANES = 128
SEQ_BLOCK = 128

N_META = 16
LRU_C = 8.0
MLA_HEADS = 8
MLA_NOPE = 64
MLA_ROPE = 32
MLA_V = 64
RET_HEADS = 4
ROPE_BASE = 10000.0
DEPTH = 2
DN_ALPHA = (2 * DEPTH) ** 0.25
EPS = 1e-5
NEG_INF = -1e30

ADAM_LR = 0.001
ADAM_B1 = 0.9
ADAM_B2 = 0.999
ADAM_EPS = 1e-08
ADAM_WD = 0.01
ADAM_STEP = 10

VMEM_LIMIT = 56 * 1024 * 1024


def _params(*sem):
    return pltpu.CompilerParams(dimension_semantics=sem, vmem_limit_bytes=VMEM_LIMIT)


def _pick(n, cands):
    for c in cands:
        if n % c == 0:
            return c
    return n


def _row_tile(r, width):
    cands = (256, 128, 64, 32, 16, 8) if width <= 1024 else (128, 64, 32, 16, 8)
    return _pick(r, cands)


_DIMS = {"nn": (((1,), (0,)), ((), ())), "nt": (((1,), (1,)), ((), ())), "tn": (((0,), (0,)), ((), ()))}


def _dot(a, b, mode):
    return lax.dot_general(a.astype(BF16), b.astype(BF16), _DIMS[mode], preferred_element_type=F32)


def _matmul(a, b, mode, name, after=()):
    if mode == "nn":
        (m, k), n = a.shape, b.shape[1]
    elif mode == "nt":
        (m, k), n = a.shape, b.shape[0]
    else:
        (k, m), n = a.shape, b.shape[1]
    tm = _pick(m, (1088, 1024, 544, 512, 272, 256, 128, 64, 32, 16, 8))
    tn = _pick(n, (512, 256, 128))
    tk = _pick(k, (1088, 1024, 544, 512, 272, 256, 128))
    nk = k // tk

    if mode == "nn":
        a_spec = pl.BlockSpec((tm, tk), lambda i, j, kk: (i, kk))
        b_spec = pl.BlockSpec((tk, tn), lambda i, j, kk: (kk, j))
    elif mode == "nt":
        a_spec = pl.BlockSpec((tm, tk), lambda i, j, kk: (i, kk))
        b_spec = pl.BlockSpec((tn, tk), lambda i, j, kk: (j, kk))
    else:
        a_spec = pl.BlockSpec((tk, tm), lambda i, j, kk: (kk, i))
        b_spec = pl.BlockSpec((tk, tn), lambda i, j, kk: (kk, j))

    def body(a_ref, b_ref, *rest):
        o_ref = rest[-1]
        kk = pl.program_id(2)
        part = _dot(a_ref[...], b_ref[...], mode)

        @pl.when(kk == 0)
        def _():
            o_ref[...] = part

        @pl.when(kk != 0)
        def _():
            o_ref[...] += part

    return pl.pallas_call(
        body,
        name=name,
        grid=(m // tm, n // tn, nk),
        in_specs=[a_spec, b_spec] + [pl.BlockSpec(memory_space=pl.ANY)] * len(after),
        out_specs=pl.BlockSpec((tm, tn), lambda i, j, kk: (i, j)),
        out_shape=jax.ShapeDtypeStruct((m, n), F32),
        compiler_params=_params("parallel", "parallel", "arbitrary"),
    )(a, b, *after)


def _group_matmul(a, w, mode, name):
    if mode in ("nn", "nt"):
        g, dk, dn = w.shape
        m = a.shape[0]
        d_in, d_out = (dk, dn) if mode == "nn" else (dn, dk)
        tm = _pick(m, (1088, 1024, 544, 512, 272, 256, 128, 64, 32, 16, 8))

        def body(a_ref, w_ref, o_ref):
            o_ref[...] = _dot(a_ref[...], w_ref[0], mode)

        return pl.pallas_call(
            body,
            name=name,
            grid=(g, m // tm),
            in_specs=[pl.BlockSpec((tm, d_in), lambda h, i: (i, h)), pl.BlockSpec((1, dk, dn), lambda h, i: (h, 0, 0))],
            out_specs=pl.BlockSpec((tm, d_out), lambda h, i: (i, h)),
            out_shape=jax.ShapeDtypeStruct((m, g * d_out), F32),
            compiler_params=_params("parallel", "parallel"),
        )(a, w)
    b = w
    m = a.shape[0]
    dk = dn = LANES
    g = a.shape[1] // dk
    tm = _pick(m, (1088, 1024, 544, 512, 272, 256, 128, 64, 32, 16, 8))

    def body(a_ref, b_ref, o_ref):
        part = _dot(a_ref[...], b_ref[...], "tn")

        @pl.when(pl.program_id(1) == 0)
        def _():
            o_ref[0] = part

        @pl.when(pl.program_id(1) != 0)
        def _():
            o_ref[0] += part

    return pl.pallas_call(
        body,
        name=name,
        grid=(g, m // tm),
        in_specs=[pl.BlockSpec((tm, dk), lambda h, i: (i, h)), pl.BlockSpec((tm, dn), lambda h, i: (i, h))],
        out_specs=pl.BlockSpec((1, dk, dn), lambda h, i: (h, 0, 0)),
        out_shape=jax.ShapeDtypeStruct((g, dk, dn), F32),
        compiler_params=_params("parallel", "arbitrary"),
    )(a, b)


def _make_group_linear(name):
    @jax.custom_vjp
    def op(x, w):
        return _group_matmul(x, w, "nn", name + "_fwd")

    def fwd(x, w):
        return op(x, w), (x, w)

    def bwd(res, dy):
        x, w = res
        return _group_matmul(dy, w, "nt", name + "_dx"), _group_matmul(x, dy, "tn", name + "_dw")

    op.defvjp(fwd, bwd)
    return op


def _my_place():
    return lax.axis_index("x"), lax.axis_index("y"), lax.axis_index("c")


def _all_gather(shard, name):
    shape, dtype = shard.shape, shard.dtype

    def body(x_ref, out_ref, send_sems, recv_sems, local_sem):
        x, y, c = _my_place()
        me, sibling = (x, y, c), (x, y, 1 - c)
        chips = [(1 - x, y), (x, 1 - y), (1 - x, 1 - y)]

        def slot(px, py, pc):
            return out_ref.at[4 * px + 2 * py + pc]

        def copy(k, block, to, src=None):
            return pltpu.make_async_remote_copy(
                src_ref=slot(*block) if src is None else src,
                dst_ref=slot(*block),
                send_sem=send_sems.at[k],
                recv_sem=recv_sems.at[k],
                device_id=to,
                device_id_type=pl.DeviceIdType.MESH,
            )

        mine = pltpu.make_async_copy(x_ref, slot(*me), local_sem)
        mine.start()
        first = [copy(0, me, sibling, src=x_ref)]
        first += [copy(1 + j, me, (*chip, c), src=x_ref) for j, chip in enumerate(chips)]
        for cp in first:
            cp.start()
        passed = [copy(4 + j, (*chip, c), sibling) for j, chip in enumerate(chips)]
        for j, chip in enumerate(chips):
            copy(1 + j, (*chip, c), me).wait_recv()
            passed[j].start()
        copy(0, sibling, me).wait_recv()
        for j, chip in enumerate(chips):
            copy(4 + j, (*chip, 1 - c), me).wait_recv()
        for cp in first + passed:
            cp.wait_send()
        mine.wait()

    return pl.pallas_call(
        body,
        name=name,
        out_shape=jax.ShapeDtypeStruct((N_DEV,) + shape, dtype),
        in_specs=[pl.BlockSpec(memory_space=pl.ANY)],
        out_specs=pl.BlockSpec(memory_space=pl.ANY),
        scratch_shapes=[pltpu.SemaphoreType.DMA((7,)), pltpu.SemaphoreType.DMA((7,)), pltpu.SemaphoreType.DMA],
    )(shard)


def _all_to_all(stacked, name):
    def body(x_ref, out_ref, send_sems, recv_sems, local_sem):
        x, y, c = _my_place()
        me = 4 * x + 2 * y + c
        mine = pltpu.make_async_copy(x_ref.at[me], out_ref.at[me], local_sem)
        mine.start()
        copies = []
        for k in range(1, N_DEV):
            px, py, pc = x ^ ((k >> 2) & 1), y ^ ((k >> 1) & 1), c ^ (k & 1)
            peer = 4 * px + 2 * py + pc
            copies.append(
                pltpu.make_async_remote_copy(
                    src_ref=x_ref.at[peer],
                    dst_ref=out_ref.at[me],
                    send_sem=send_sems.at[k - 1],
                    recv_sem=recv_sems.at[k - 1],
                    device_id=(px, py, pc),
                    device_id_type=pl.DeviceIdType.MESH,
                )
            )
        for cp in copies:
            cp.start()
        for cp in copies:
            cp.wait_recv()
        for cp in copies:
            cp.wait_send()
        mine.wait()

    return pl.pallas_call(
        body,
        name=name,
        out_shape=jax.ShapeDtypeStruct(stacked.shape, stacked.dtype),
        in_specs=[pl.BlockSpec(memory_space=pl.ANY)],
        out_specs=pl.BlockSpec(memory_space=pl.ANY),
        scratch_shapes=[pltpu.SemaphoreType.DMA((7,)), pltpu.SemaphoreType.DMA((7,)), pltpu.SemaphoreType.DMA],
    )(stacked)


def _sum_blocks(stacked, name):
    _, r, c = stacked.shape
    tr = _pick(r, (256, 128, 64, 32, 16, 8))

    def body(x_ref, o_ref):
        s = [x_ref[j] for j in range(N_DEV)]
        o_ref[...] = ((s[0] + s[1]) + (s[2] + s[3])) + ((s[4] + s[5]) + (s[6] + s[7]))

    return pl.pallas_call(
        body,
        name=name,
        grid=(r // tr,),
        in_specs=[pl.BlockSpec((N_DEV, tr, c), lambda i: (0, i, 0))],
        out_specs=pl.BlockSpec((tr, c), lambda i: (i, 0)),
        out_shape=jax.ShapeDtypeStruct((r, c), stacked.dtype),
        compiler_params=_params("parallel"),
    )(stacked)


def _stack_cols(full):
    k, n8 = full.shape
    return full.reshape(k, N_DEV, n8 // N_DEV).transpose(1, 0, 2)


def _unstack_cols(stacked):
    j, k, n = stacked.shape
    return stacked.transpose(1, 0, 2).reshape(k, j * n)


def _gather_weight(shard, cols, name):
    g = _all_gather(shard.astype(BF16), name)
    return _unstack_cols(g) if cols else g.reshape(-1, shard.shape[1])


def _scatter_grad(full, cols, name):
    if cols:
        st = _stack_cols(full)
    else:
        st = full.reshape(N_DEV, full.shape[0] // N_DEV, full.shape[1])
    return _sum_blocks(_all_to_all(st, name + "_a2a"), name + "_sum")


def _make_fsdp_linear(cols, name):
    @jax.custom_vjp
    def op(x, w_shard):
        return _matmul(x, _gather_weight(w_shard, cols, name + "_ag"), "nn", name + "_fwd")

    def fwd(x, w_shard):
        w = _gather_weight(w_shard, cols, name + "_ag")
        return _matmul(x, w, "nn", name + "_fwd"), (x, w)

    def bwd(res, dy):
        x, w = res
        dx = _matmul(dy, w, "nt", name + "_dx")
        dw = _matmul(x, dy, "tn", name + "_dw")
        return dx, _scatter_grad(dw, cols, name + "_rs")

    op.defvjp(fwd, bwd)
    return op


def _make_fsdp_param(name):
    @jax.custom_vjp
    def op(shard):
        return _unstack_cols(_all_gather(shard, name + "_ag"))

    def fwd(shard):
        return op(shard), None

    def bwd(_, g):
        return (_scatter_grad(g, True, name + "_rs"),)

    op.defvjp(fwd, bwd)
    return op


def _allreduce_replicated(gs, name):
    flat = jnp.concatenate([g.reshape(-1) for g in gs])
    n = flat.shape[0]
    rows = -(-n // (256 * LANES)) * 256
    packed = jnp.pad(flat, (0, rows * LANES - n)).reshape(rows, LANES)
    total = _sum_blocks(_all_gather(packed, name + "_ag"), name + "_sum").reshape(-1)
    out, off = [], 0
    for g in gs:
        out.append(total[off:off + g.size].reshape(g.shape))
        off += g.size
    return out


_HBM = pl.BlockSpec(memory_space=pltpu.HBM)
_SEM = pl.BlockSpec(memory_space=pltpu.SEMAPHORE)
_SIDE_EFFECT = pltpu.SideEffectType.DATAFLOW_SIDE_EFFECTING
_N_PEERS = N_DEV - 1


def _peer(k):
    x, y, c = _my_place()
    return x ^ ((k >> 2) & 1), y ^ ((k >> 1) & 1), c ^ (k & 1)


def _exchange_start(src, land_shape, gather, name):
    def body(src_ref, land_ref, send_sems, recv_sems, src_thru, land_thru, token):
        x, y, c = _my_place()
        me = 4 * x + 2 * y + c
        for k in range(1, N_DEV):
            px, py, pc = _peer(k)
            pltpu.make_async_remote_copy(
                src_ref=src_ref if gather else src_ref.at[4 * px + 2 * py + pc],
                dst_ref=land_ref.at[me] if gather else land_ref.at[k - 1],
                send_sem=send_sems.at[k - 1],
                recv_sem=recv_sems.at[k - 1],
                device_id=(px, py, pc),
                device_id_type=pl.DeviceIdType.MESH,
            ).start()
        token[...] = jnp.zeros_like(token)

    return pl.pallas_call(
        body,
        name=name,
        out_shape=(
            pltpu.SemaphoreType.DMA((_N_PEERS,)),
            pltpu.SemaphoreType.DMA((_N_PEERS,)),
            pltpu.HBM(src.shape, src.dtype),
            pltpu.HBM(land_shape, src.dtype),
            jax.ShapeDtypeStruct((8, LANES), F32),
        ),
        in_specs=(_HBM, _HBM),
        out_specs=(_SEM, _SEM, _HBM, _HBM, pl.BlockSpec(memory_space=pltpu.VMEM)),
        input_output_aliases={0: 2, 1: 3},
        compiler_params=pltpu.CompilerParams(has_side_effects=_SIDE_EFFECT),
    )(pltpu.with_memory_space_constraint(src, pltpu.HBM),
      pltpu.with_memory_space_constraint(lax.empty(land_shape, src.dtype), pltpu.HBM))


def _exchange_wait(handle, gather, after, name):
    send_sems, recv_sems, src_thru, land_thru, _ = handle

    def body(src_ref, land_ref, send_sems, recv_sems, after_ref, src_dead, got_ref):
        for k in range(1, N_DEV):
            cp = pltpu.make_async_remote_copy(
                src_ref=src_ref if gather else src_ref.at[k],
                dst_ref=land_ref.at[k - 1],
                send_sem=send_sems.at[k - 1],
                recv_sem=recv_sems.at[k - 1],
                device_id=_peer(k),
                device_id_type=pl.DeviceIdType.MESH,
            )
            cp.wait_send()
            cp.wait_recv()

    return pl.pallas_call(
        body,
        name=name,
        out_shape=(pltpu.HBM(src_thru.shape, src_thru.dtype), pltpu.HBM(land_thru.shape, land_thru.dtype)),
        in_specs=(_HBM, _HBM, _SEM, _SEM, pl.BlockSpec(memory_space=pl.ANY)),
        out_specs=(_HBM, _HBM),
        input_output_aliases={0: 0, 1: 1},
        compiler_params=pltpu.CompilerParams(has_side_effects=_SIDE_EFFECT),
    )(src_thru, land_thru, send_sems, recv_sems, after)[1]


def _sum_own_and_peers(own, land, name):
    r, c = own.shape
    tr = _pick(r, (256, 128, 64, 32, 16, 8))

    def body(o_ref, l_ref, out_ref):
        s = [l_ref[j] for j in range(_N_PEERS)]
        out_ref[...] = ((o_ref[...] + s[0]) + (s[1] + s[2])) + ((s[3] + s[4]) + (s[5] + s[6]))

    return pl.pallas_call(
        body,
        name=name,
        grid=(r // tr,),
        in_specs=[pl.BlockSpec((tr, c), lambda i: (i, 0)), pl.BlockSpec((_N_PEERS, tr, c), lambda i: (0, i, 0))],
        out_specs=pl.BlockSpec((tr, c), lambda i: (i, 0)),
        out_shape=jax.ShapeDtypeStruct((r, c), own.dtype),
        compiler_params=_params("parallel"),
    )(own, land)


def _make_rowwise(f, name, n_rows, n_tabs, n_pars):
    n_in = n_rows + n_tabs + n_pars

    def specs(args, tm):
        blocked = [pl.BlockSpec((tm, a.shape[1]), lambda i: (i, 0)) for a in args[: n_rows + n_tabs]]
        whole = [pl.BlockSpec(a.shape, lambda i: (0, 0)) for a in args[n_rows + n_tabs:]]
        return blocked + whole

    def out_struct(args, tm):
        blk = [jax.ShapeDtypeStruct((tm, a.shape[1]), a.dtype) for a in args[: n_rows + n_tabs]]
        blk += [jax.ShapeDtypeStruct(a.shape, a.dtype) for a in args[n_rows + n_tabs:]]
        return jax.eval_shape(f, *blk)

    def fwd_call(*args):
        r = args[0].shape[0]
        tm = _row_tile(r, max(a.shape[1] for a in args[:n_rows]))
        ro, so = out_struct(args, tm)

        def body(*refs):
            vals = [x[...] for x in refs[:n_in]]
            outs = refs[n_in:]
            rv, sv = f(*vals)
            for o, v in zip(outs[: len(ro)], rv):
                o[...] = v
            for o, v in zip(outs[len(ro):], sv):
                @pl.when(pl.program_id(0) == 0)
                def _(o=o, v=v):
                    o[...] = v

                @pl.when(pl.program_id(0) != 0)
                def _(o=o, v=v):
                    o[...] += v

        out_shape = [jax.ShapeDtypeStruct((r, s.shape[1]), s.dtype) for s in ro]
        out_shape += [jax.ShapeDtypeStruct(s.shape, s.dtype) for s in so]
        out_specs = [pl.BlockSpec((tm, s.shape[1]), lambda i: (i, 0)) for s in ro]
        out_specs += [pl.BlockSpec(s.shape, lambda i: (0, 0)) for s in so]
        res = pl.pallas_call(
            body,
            name=name + "_fwd",
            grid=(r // tm,),
            in_specs=specs(args, tm),
            out_specs=out_specs,
            out_shape=out_shape,
            compiler_params=_params("arbitrary" if so else "parallel"),
        )(*args)
        return tuple(res[: len(ro)]), tuple(res[len(ro):])

    def bwd_call(args, cots):
        r = args[0].shape[0]
        tm = _row_tile(r, max(a.shape[1] for a in args[:n_rows]))
        ro, so = out_struct(args, tm)
        crow, csum = cots
        rows, tabs, pars = args[:n_rows], args[n_rows:n_rows + n_tabs], args[n_rows + n_tabs:]
        n_c = len(crow) + len(csum)

        def body(*refs):
            vals = [x[...] for x in refs[:n_in]]
            cv = [x[...] for x in refs[n_in:n_in + n_c]]
            outs = refs[n_in + n_c:]
            tv = vals[n_rows:n_rows + n_tabs]

            def g(*dargs):
                return f(*dargs[:n_rows], *tv, *dargs[n_rows:])

            _, vjp = jax.vjp(g, *vals[:n_rows], *vals[n_rows + n_tabs:])
            d = vjp((tuple(cv[: len(crow)]), tuple(cv[len(crow):])))
            for o, v in zip(outs[:n_rows], d[:n_rows]):
                o[...] = v
            for o, v in zip(outs[n_rows:], d[n_rows:]):
                @pl.when(pl.program_id(0) == 0)
                def _(o=o, v=v):
                    o[...] = v

                @pl.when(pl.program_id(0) != 0)
                def _(o=o, v=v):
                    o[...] += v

        in_specs = specs(args, tm)
        in_specs += [pl.BlockSpec((tm, c.shape[1]), lambda i: (i, 0)) for c in crow]
        in_specs += [pl.BlockSpec(c.shape, lambda i: (0, 0)) for c in csum]
        out_shape = [jax.ShapeDtypeStruct(a.shape, a.dtype) for a in rows + pars]
        out_specs = [pl.BlockSpec((tm, a.shape[1]), lambda i: (i, 0)) for a in rows]
        out_specs += [pl.BlockSpec(a.shape, lambda i: (0, 0)) for a in pars]
        res = pl.pallas_call(
            body,
            name=name + "_bwd",
            grid=(r // tm,),
            in_specs=in_specs,
            out_specs=out_specs,
            out_shape=out_shape,
            compiler_params=_params("arbitrary" if pars else "parallel"),
        )(*args, *crow, *csum)
        return tuple(res[:n_rows]), tuple(res[n_rows:])

    @jax.custom_vjp
    def op(rows, tabs, pars):
        return fwd_call(*rows, *tabs, *pars)

    def fwd(rows, tabs, pars):
        return fwd_call(*rows, *tabs, *pars), (rows, tabs, pars)

    def bwd(res, cots):
        rows, tabs, pars = res
        drows, dpars = bwd_call(tuple(rows) + tuple(tabs) + tuple(pars), cots)
        return drows, tuple(jnp.zeros_like(t) for t in tabs), dpars

    op.defvjp(fwd, bwd)
    return op


def _sigmoid(x):
    return 0.5 * (jnp.tanh(0.5 * x) + 1.0)


@jax.custom_jvp
def _softplus(x):
    e = jnp.exp(-jnp.abs(x))
    u = 1.0 + e
    log1p_e = jnp.where(u == 1.0, e, e * jnp.log(u) / jnp.where(u == 1.0, 1.0, u - 1.0))
    return jnp.maximum(x, 0.0) + log1p_e


@_softplus.defjvp
def _softplus_jvp(primals, tangents):
    (x,), (t,) = primals, tangents
    return _softplus(x), t * _sigmoid(x)


def _gelu(x):
    return 0.5 * x * (1.0 + jnp.tanh(math.sqrt(2.0 / math.pi) * (x + 0.044715 * (x * x * x))))


def _ln_res_f(h, mix, g, b):
    z = DN_ALPHA * h + mix
    mu = jnp.mean(z, axis=-1, keepdims=True)
    zc = z - mu
    var = jnp.mean(zc * zc, axis=-1, keepdims=True)
    return (zc * lax.rsqrt(var + EPS) * g + b,), ()


def _rmsnorm_f(x, g):
    return (x * lax.rsqrt(jnp.mean(x * x, axis=-1, keepdims=True) + EPS) * g,), ()


def _lru_gates_f(ga, gx, xc, b_a, b_x, lam):
    r = _sigmoid(ga + b_a)
    i = _sigmoid(gx + b_x)
    log_a = -LRU_C * r * _softplus(-lam)
    a = jnp.exp(log_a)
    one_minus_a2 = jnp.tanh(-log_a) * (jnp.exp(2.0 * log_a) + 1.0)
    return (a, jnp.sqrt(one_minus_a2) * (i * xc)), ()


def _lru_out_f(hh, p_gate):
    return (hh * _gelu(p_gate),), ()


def _rope_pair_f(x1, x2, cos, sin):
    return (x1 * cos - x2 * sin, x1 * sin + x2 * cos), ()


def _rope_ret_f(q, k, cos2, sin2):
    d = cos2.shape[1]
    half = d // 2
    k_scale = d ** -0.5

    def rope(x):
        outs = []
        for h in range(x.shape[1] // d):
            xh = x[:, h * d:(h + 1) * d]
            rot = jnp.concatenate([xh[:, half:], xh[:, :half]], axis=1)
            outs.append(xh * cos2 + rot * sin2)
        return jnp.concatenate(outs, axis=1)

    return (rope(q), rope(k) * k_scale), ()


def _ret_out_f(o, g):
    d = o.shape[1] // RET_HEADS
    outs = []
    for h in range(RET_HEADS):
        oh = o[:, h * d:(h + 1) * d]
        outs.append(oh * lax.rsqrt(jnp.mean(oh * oh, axis=-1, keepdims=True) + EPS))
    y = jnp.concatenate(outs, axis=1)
    return (g * _sigmoid(g) * y,), ()


def _relu2_f(u):
    r = jnp.maximum(u, 0.0)
    return (r * r,), ()


def _loss_f(y, t, mask):
    e = (y - t) * mask
    per_row = jnp.sum(e * e, axis=-1, keepdims=True) * (0.5 / y.shape[1])
    total = jnp.sum(per_row, axis=0, keepdims=True)
    return (), (jnp.broadcast_to(total, (1, LANES)),)


def _shift_down(x, s):
    if s == 0:
        return x
    t = x.shape[0]
    row = lax.broadcasted_iota(jnp.int32, x.shape, 0)
    return jnp.where(row >= s, pltpu.roll(x, s, 0), 0.0)


def _shift_up(x, s):
    if s == 0:
        return x
    t = x.shape[0]
    row = lax.broadcasted_iota(jnp.int32, x.shape, 0)
    return jnp.where(row < t - s, pltpu.roll(x, t - s, 0), 0.0)


def _conv_fwd(x, w, b, name):
    bsz, t, c = x.shape
    width = w.shape[0]

    def body(x_ref, w_ref, b_ref, y_ref):
        xv = x_ref[0]
        acc = jnp.broadcast_to(b_ref[...], xv.shape)
        for k in range(width):
            acc = acc + w_ref[k:k + 1, :] * _shift_down(xv, width - 1 - k)
        y_ref[0] = acc

    return pl.pallas_call(
        body,
        name=name,
        grid=(bsz, c // LANES),
        in_specs=[
            pl.BlockSpec((1, t, LANES), lambda i, j: (i, 0, j)),
            pl.BlockSpec((width, LANES), lambda i, j: (0, j)),
            pl.BlockSpec((1, LANES), lambda i, j: (0, j)),
        ],
        out_specs=pl.BlockSpec((1, t, LANES), lambda i, j: (i, 0, j)),
        out_shape=jax.ShapeDtypeStruct(x.shape, F32),
        compiler_params=_params("parallel", "parallel"),
    )(x, w, b)


def _conv_bwd(x, w, dy, name):
    bsz, t, c = x.shape
    width = w.shape[0]

    def body(x_ref, w_ref, dy_ref, dx_ref, dw_ref, db_ref):
        xv, g = x_ref[0], dy_ref[0]
        dx = jnp.zeros_like(xv)
        dws = []
        for k in range(width):
            s = width - 1 - k
            dx = dx + w_ref[k:k + 1, :] * _shift_up(g, s)
            dws.append(jnp.sum(g * _shift_down(xv, s), axis=0, keepdims=True))
        dx_ref[0] = dx
        dw = jnp.concatenate(dws, axis=0)
        db = jnp.sum(g, axis=0, keepdims=True)

        @pl.when(pl.program_id(1) == 0)
        def _():
            dw_ref[...] = dw
            db_ref[...] = db

        @pl.when(pl.program_id(1) != 0)
        def _():
            dw_ref[...] += dw
            db_ref[...] += db

    return pl.pallas_call(
        body,
        name=name,
        grid=(c // LANES, bsz),
        in_specs=[
            pl.BlockSpec((1, t, LANES), lambda j, i: (i, 0, j)),
            pl.BlockSpec((width, LANES), lambda j, i: (0, j)),
            pl.BlockSpec((1, t, LANES), lambda j, i: (i, 0, j)),
        ],
        out_specs=[
            pl.BlockSpec((1, t, LANES), lambda j, i: (i, 0, j)),
            pl.BlockSpec((width, LANES), lambda j, i: (0, j)),
            pl.BlockSpec((1, LANES), lambda j, i: (0, j)),
        ],
        out_shape=[
            jax.ShapeDtypeStruct(x.shape, F32),
            jax.ShapeDtypeStruct(w.shape, F32),
            jax.ShapeDtypeStruct((1, c), F32),
        ],
        compiler_params=_params("parallel", "arbitrary"),
    )(x, w, dy)


def _make_conv(name):
    @jax.custom_vjp
    def op(x, w, b):
        return _conv_fwd(x, w, b, name + "_fwd")

    def fwd(x, w, b):
        return op(x, w, b), (x, w)

    def bwd(res, dy):
        x, w = res
        return tuple(_conv_bwd(x, w, dy, name + "_bwd"))

    op.defvjp(fwd, bwd)
    return op


def _scan_fwd(a, b, name):
    bsz, t, c = a.shape
    cw = _pick(c, (4 * LANES, 2 * LANES, LANES))

    def body(a_ref, b_ref, h_ref):
        def step(i, h):
            h = a_ref[0, pl.ds(i, 1), :] * h + b_ref[0, pl.ds(i, 1), :]
            h_ref[0, pl.ds(i, 1), :] = h
            return h

        lax.fori_loop(0, t, step, jnp.zeros((1, cw), F32), unroll=8)

    spec = pl.BlockSpec((1, t, cw), lambda i, j: (i, 0, j))
    return pl.pallas_call(
        body,
        name=name,
        grid=(bsz, c // cw),
        in_specs=[spec, spec],
        out_specs=spec,
        out_shape=jax.ShapeDtypeStruct(a.shape, F32),
        compiler_params=_params("parallel", "parallel"),
    )(a, b)


def _scan_bwd(a, h, g, name):
    bsz, t, c = a.shape
    cw = _pick(c, (2 * LANES, LANES))

    def body(a_ref, h_ref, g_ref, da_ref, db_ref):
        def step(n, carry):
            i = t - 1 - n
            lam = g_ref[0, pl.ds(i, 1), :] + carry
            db_ref[0, pl.ds(i, 1), :] = lam
            prev = jnp.where(i > 0, h_ref[0, pl.ds(jnp.maximum(i - 1, 0), 1), :], 0.0)
            da_ref[0, pl.ds(i, 1), :] = lam * prev
            return a_ref[0, pl.ds(i, 1), :] * lam

        lax.fori_loop(0, t, step, jnp.zeros((1, cw), F32), unroll=8)

    spec = pl.BlockSpec((1, t, cw), lambda i, j: (i, 0, j))
    return pl.pallas_call(
        body,
        name=name,
        grid=(bsz, c // cw),
        in_specs=[spec, spec, spec],
        out_specs=[spec, spec],
        out_shape=[jax.ShapeDtypeStruct(a.shape, F32)] * 2,
        compiler_params=_params("parallel", "parallel"),
    )(a, h, g)


def _make_scan(name):
    @jax.custom_vjp
    def op(a, b):
        return _scan_fwd(a, b, name + "_fwd")

    def fwd(a, b):
        h = op(a, b)
        return h, (a, h)

    def bwd(res, g):
        a, h = res
        da, db = _scan_bwd(a, h, g, name + "_bwd")
        return da, db

    op.defvjp(fwd, bwd)
    return op


def _query_blocks(t):
    blocks, start = [], 0
    while start < t:
        rows = 2 * SEQ_BLOCK if start + 2 * SEQ_BLOCK <= t else SEQ_BLOCK
        blocks.append((start, rows))
        start += rows
    return blocks


def _attn_probs(q, k, start, scale):
    tq, tk = q.shape[0], k.shape[0]
    s = _dot(q, k, "nt") * scale
    qpos = start + lax.broadcasted_iota(jnp.int32, (tq, tk), 0)
    kpos = lax.broadcasted_iota(jnp.int32, (tq, tk), 1)
    s = jnp.where(kpos <= qpos, s, NEG_INF)
    e = jnp.exp(s - jnp.max(s, axis=-1, keepdims=True))
    return e / jnp.sum(e, axis=-1, keepdims=True)


def _attn_fwd(q, k, v, name):
    g, t, dq = q.shape
    dv = v.shape[2]
    scale = dq ** -0.5

    def body(q_ref, k_ref, v_ref, o_ref):
        for start, rows in _query_blocks(t):
            n = start + rows
            p = _attn_probs(q_ref[0, start:n, :], k_ref[0, :n, :], start, scale)
            o_ref[0, start:n, :] = _dot(p, v_ref[0, :n, :], "nn")

    return pl.pallas_call(
        body,
        name=name,
        grid=(g,),
        in_specs=[
            pl.BlockSpec((1, t, dq), lambda b: (b, 0, 0)),
            pl.BlockSpec((1, t, dq), lambda b: (b, 0, 0)),
            pl.BlockSpec((1, t, dv), lambda b: (b, 0, 0)),
        ],
        out_specs=pl.BlockSpec((1, t, dv), lambda b: (b, 0, 0)),
        out_shape=jax.ShapeDtypeStruct((g, t, dv), F32),
        compiler_params=_params("parallel"),
    )(q, k, v)


def _attn_bwd(q, k, v, do, name):
    g, t, dq = q.shape
    dv = v.shape[2]
    scale = dq ** -0.5

    def body(q_ref, k_ref, v_ref, do_ref, dq_ref, dk_ref, dv_ref):
        for start, rows in reversed(_query_blocks(t)):
            n = start + rows
            qb, dob = q_ref[0, start:n, :], do_ref[0, start:n, :]
            kk, vv = k_ref[0, :n, :], v_ref[0, :n, :]
            p = _attn_probs(qb, kk, start, scale)
            dp = _dot(dob, vv, "nt")
            ds = p * (dp - jnp.sum(dp * p, axis=-1, keepdims=True)) * scale
            dq_ref[0, start:n, :] = _dot(ds, kk, "nn")
            if n == t:
                dk_ref[0] = _dot(ds, qb, "tn")
                dv_ref[0] = _dot(p, dob, "tn")
            else:
                dk_ref[0, :n, :] += _dot(ds, qb, "tn")
                dv_ref[0, :n, :] += _dot(p, dob, "tn")

    qk_spec = pl.BlockSpec((1, t, dq), lambda b: (b, 0, 0))
    v_spec = pl.BlockSpec((1, t, dv), lambda b: (b, 0, 0))
    return pl.pallas_call(
        body,
        name=name,
        grid=(g,),
        in_specs=[qk_spec, qk_spec, v_spec, v_spec],
        out_specs=[qk_spec, qk_spec, v_spec],
        out_shape=[
            jax.ShapeDtypeStruct(q.shape, F32),
            jax.ShapeDtypeStruct(k.shape, F32),
            jax.ShapeDtypeStruct(v.shape, F32),
        ],
        compiler_params=_params("parallel"),
    )(q, k, v, do)


def _make_attention(name):
    @jax.custom_vjp
    def op(q, k, v):
        return _attn_fwd(q, k, v, name + "_fwd")

    def fwd(q, k, v):
        return op(q, k, v), (q, k, v)

    def bwd(res, do):
        return tuple(_attn_bwd(*res, do, name + "_bwd"))

    op.defvjp(fwd, bwd)
    return op


RET_KEY_CHUNK = 512


def _key_chunks(t):
    return [(c, min(RET_KEY_CHUNK, t - c)) for c in range(0, t, RET_KEY_CHUNK)]


def _decay(q0, k0, tq, tk, log_gamma):
    qpos = q0 + lax.broadcasted_iota(jnp.int32, (tq, tk), 0)
    kpos = k0 + lax.broadcasted_iota(jnp.int32, (tq, tk), 1)
    diff = qpos - kpos
    return jnp.where(diff >= 0, jnp.exp(log_gamma * jnp.maximum(diff, 0).astype(F32)), 0.0)


def _ret_specs(t, dk, dv, heads):
    tq = SEQ_BLOCK
    return (
        pl.BlockSpec(memory_space=pltpu.SMEM),
        pl.BlockSpec((1, tq, dk), lambda b, h, i: (b, i, h)),
        pl.BlockSpec((1, t, dk), lambda b, h, i: (b, 0, h)),
        pl.BlockSpec((1, t, dv), lambda b, h, i: (b, 0, h)),
        pl.BlockSpec((1, tq, dv), lambda b, h, i: (b, i, h)),
    )


def _ret_fwd(lg, q, k, v, name):
    bsz, t, hdk = q.shape
    heads = lg.shape[0]
    dk, dv = hdk // heads, v.shape[2] // heads
    lg_spec, q_spec, k_spec, v_spec, o_spec = _ret_specs(t, dk, dv, heads)

    def body(lg_ref, q_ref, k_ref, v_ref, o_ref):
        q0 = pl.program_id(2) * SEQ_BLOCK
        lgh = lg_ref[pl.program_id(1)]
        o_ref[0] = jnp.zeros((SEQ_BLOCK, dv), F32)
        for c0, cw in _key_chunks(t):
            @pl.when(c0 < q0 + SEQ_BLOCK)
            def _(c0=c0, cw=cw):
                d = _decay(q0, c0, SEQ_BLOCK, cw, lgh)
                a = _dot(q_ref[0], k_ref[0, c0:c0 + cw, :], "nt") * d
                o_ref[0] += _dot(a, v_ref[0, c0:c0 + cw, :], "nn")

    return pl.pallas_call(
        body,
        name=name,
        grid=(bsz, heads, t // SEQ_BLOCK),
        in_specs=[lg_spec, q_spec, k_spec, v_spec],
        out_specs=o_spec,
        out_shape=jax.ShapeDtypeStruct(v.shape, F32),
        compiler_params=_params("parallel", "parallel", "parallel"),
    )(lg, q, k, v)


def _ret_bwd(lg, q, k, v, do, name):
    bsz, t, hdk = q.shape
    heads = lg.shape[0]
    dk, dv = hdk // heads, v.shape[2] // heads
    lg_spec, q_spec, k_spec, v_spec, o_spec = _ret_specs(t, dk, dv, heads)

    def body(lg_ref, q_ref, k_ref, v_ref, do_ref, dq_ref, dk_ref, dv_ref):
        q0 = pl.program_id(2) * SEQ_BLOCK
        lgh = lg_ref[pl.program_id(1)]

        @pl.when(pl.program_id(2) == 0)
        def _():
            dk_ref[0] = jnp.zeros((t, dk), F32)
            dv_ref[0] = jnp.zeros((t, dv), F32)

        dq_ref[0] = jnp.zeros((SEQ_BLOCK, dk), F32)
        for c0, cw in _key_chunks(t):
            @pl.when(c0 < q0 + SEQ_BLOCK)
            def _(c0=c0, cw=cw):
                qb, dob = q_ref[0], do_ref[0]
                kk, vv = k_ref[0, c0:c0 + cw, :], v_ref[0, c0:c0 + cw, :]
                d = _decay(q0, c0, SEQ_BLOCK, cw, lgh)
                a = _dot(qb, kk, "nt") * d
                ds = _dot(dob, vv, "nt") * d
                dq_ref[0] += _dot(ds, kk, "nn")
                dk_ref[0, c0:c0 + cw, :] += _dot(ds, qb, "tn")
                dv_ref[0, c0:c0 + cw, :] += _dot(a, dob, "tn")

    return pl.pallas_call(
        body,
        name=name,
        grid=(bsz, heads, t // SEQ_BLOCK),
        in_specs=[lg_spec, q_spec, k_spec, v_spec, o_spec],
        out_specs=[q_spec, k_spec, v_spec],
        out_shape=[
            jax.ShapeDtypeStruct(q.shape, F32),
            jax.ShapeDtypeStruct(k.shape, F32),
            jax.ShapeDtypeStruct(v.shape, F32),
        ],
        compiler_params=_params("parallel", "parallel", "arbitrary"),
    )(lg, q, k, v, do)


def _make_retention(name):
    @jax.custom_vjp
    def op(lg, q, k, v):
        return _ret_fwd(lg, q, k, v, name + "_fwd")

    def fwd(lg, q, k, v):
        return op(lg, q, k, v), (lg, q, k, v)

    def bwd(res, do):
        lg = res[0]
        return (jnp.zeros_like(lg),) + tuple(_ret_bwd(*res, do, name + "_bwd"))

    op.defvjp(fwd, bwd)
    return op


def _adamw(w, g, m, v, name):
    r, c = w.shape
    tr = _pick(r, (256, 128, 64, 32, 16, 8))

    def body(w_ref, g_ref, m_ref, v_ref, d_ref, nm_ref, nv_ref):
        gv = g_ref[...]
        nm = ADAM_B1 * m_ref[...] + (1.0 - ADAM_B1) * gv
        nv = ADAM_B2 * v_ref[...] + (1.0 - ADAM_B2) * (gv * gv)
        m_hat = nm / (1.0 - ADAM_B1 ** ADAM_STEP)
        v_hat = nv / (1.0 - ADAM_B2 ** ADAM_STEP)
        d_ref[...] = -ADAM_LR * (m_hat / (jnp.sqrt(v_hat) + ADAM_EPS) + ADAM_WD * w_ref[...])
        nm_ref[...] = nm
        nv_ref[...] = nv

    spec = pl.BlockSpec((tr, c), lambda i: (i, 0))
    return pl.pallas_call(
        body,
        name=name,
        grid=(r // tr,),
        in_specs=[spec] * 4,
        out_specs=[spec] * 3,
        out_shape=[jax.ShapeDtypeStruct((r, c), F32)] * 3,
        compiler_params=_params("parallel"),
    )(w, g, m, v)


def _rope_tables(t, half, reps):
    inv = ROPE_BASE ** (-jnp.arange(half, dtype=F32) / half)
    ang = jnp.arange(t, dtype=jnp.int32).astype(F32)[:, None] * inv[None, :]
    return jnp.tile(jnp.cos(ang), (1, reps)), jnp.tile(jnp.sin(ang), (1, reps))


def _padded_len(seq):
    return -(-(N_META + seq) // SEQ_BLOCK) * SEQ_BLOCK


def _embed(meta_shard, x):
    bsz, seq, d = x.shape
    t = _padded_len(seq)
    meta = _make_fsdp_param("meta")(meta_shard)
    return jnp.concatenate(
        [jnp.broadcast_to(meta[None], (bsz, N_META, d)), x, jnp.zeros((bsz, t - N_META - seq, d), F32)], axis=1
    ).reshape(bsz * t, d)


def _ln_res(name, h, mix, g, b):
    return _make_rowwise(_ln_res_f, name, 2, 0, 2)((h, mix), (), (g, b))[0][0]


def _relu2(name, u):
    return _make_rowwise(_relu2_f, name, 1, 0, 0)((u,), (), ())[0][0]


def _even_mixer(p, conv_w_shard, conv_b, w_rg_a, b_rg_a, w_rg_x, b_rg_x, lru_lambda, q_norm_g, w_uq_shard,
                kv_norm_g, w_ukv_shard, bsz):
    r = p.shape[0]
    t = r // bsz

    def tile_rows(tab):
        return jnp.tile(tab, (bsz, 1))

    lru_w = w_rg_a.shape[2] * w_rg_a.shape[1]
    q_rank, kv_rank = q_norm_g.shape[1], kv_norm_g.shape[1]
    c0, c1, c2, c3 = lru_w, 2 * lru_w, 2 * lru_w + q_rank, 2 * lru_w + q_rank + kv_rank
    p_gate, p_rec, p_q, p_kv, p_kpe = p[:, :c0], p[:, c0:c1], p[:, c1:c2], p[:, c2:c3], p[:, c3:]

    conv_w = _make_fsdp_param("conv_w")(conv_w_shard[0])
    xc = _make_conv("conv")(p_rec.reshape(bsz, t, lru_w), conv_w, conv_b).reshape(r, lru_w)
    ga = _make_group_linear("rg_a")(xc, w_rg_a[0])
    gx = _make_group_linear("rg_x")(xc, w_rg_x[0])
    (a, bb), _ = _make_rowwise(_lru_gates_f, "lru_gates", 3, 0, 3)((ga, gx, xc), (), (b_rg_a, b_rg_x, lru_lambda))
    hh = _make_scan("lru_scan")(a.reshape(bsz, t, lru_w), bb.reshape(bsz, t, lru_w)).reshape(r, lru_w)
    (y_rec,), _ = _make_rowwise(_lru_out_f, "lru_out", 2, 0, 0)((hh, p_gate), (), ())

    (qn,), _ = _make_rowwise(_rmsnorm_f, "q_norm", 1, 0, 1)((p_q,), (), (q_norm_g,))
    (kvn,), _ = _make_rowwise(_rmsnorm_f, "kv_norm", 1, 0, 1)((p_kv,), (), (kv_norm_g,))
    q = _make_fsdp_linear(True, "ev_uq")(qn, w_uq_shard[0]).reshape(r, MLA_HEADS, MLA_NOPE + MLA_ROPE)
    kv = _make_fsdp_linear(True, "ev_ukv")(kvn, w_ukv_shard[0]).reshape(r, MLA_HEADS, MLA_NOPE + MLA_V)
    half = MLA_ROPE // 2
    cos_h, sin_h = _rope_tables(t, half, MLA_HEADS)
    q1 = q[:, :, MLA_NOPE:MLA_NOPE + half].reshape(r, MLA_HEADS * half)
    q2 = q[:, :, MLA_NOPE + half:].reshape(r, MLA_HEADS * half)
    (q1, q2), _ = _make_rowwise(_rope_pair_f, "rope_q", 2, 2, 0)((q1, q2), (tile_rows(cos_h), tile_rows(sin_h)), ())
    (k1, k2), _ = _make_rowwise(_rope_pair_f, "rope_k", 2, 2, 0)(
        (p_kpe[:, :half], p_kpe[:, half:]), (tile_rows(cos_h[:, :half]), tile_rows(sin_h[:, :half])), ())
    q_full = jnp.concatenate(
        [q[:, :, :MLA_NOPE], q1.reshape(r, MLA_HEADS, half), q2.reshape(r, MLA_HEADS, half)], axis=2)
    k_pe = jnp.broadcast_to(jnp.concatenate([k1, k2], axis=1)[:, None, :], (r, MLA_HEADS, MLA_ROPE))
    k_full = jnp.concatenate([kv[:, :, :MLA_NOPE], k_pe], axis=2)

    def heads_major(z):
        return z.reshape(bsz, t, MLA_HEADS, z.shape[-1]).transpose(0, 2, 1, 3).reshape(bsz * MLA_HEADS, t, z.shape[-1])

    o = _make_attention("mla")(heads_major(q_full), heads_major(k_full), heads_major(kv[:, :, MLA_NOPE:]))
    y_att = o.reshape(bsz, MLA_HEADS, t, MLA_V).transpose(0, 2, 1, 3).reshape(r, MLA_HEADS * MLA_V)
    return jnp.concatenate([y_rec, y_att], axis=1)


def _odd_mixer(p, bsz):
    r = p.shape[0]
    t = r // bsz

    def tile_rows(tab):
        return jnp.tile(tab, (bsz, 1))

    qk = p.shape[1] // 6
    dk = qk // RET_HEADS
    cos2, sin2 = _rope_tables(t, dk // 2, 2)
    sin2 = jnp.concatenate([-sin2[:, :dk // 2], sin2[:, dk // 2:]], axis=1)
    (rq, rk), _ = _make_rowwise(_rope_ret_f, "rope_ret", 2, 2, 0)(
        (p[:, :qk], p[:, qk:2 * qk]), (tile_rows(cos2), tile_rows(sin2)), ())
    lg = jnp.log(1.0 - 2.0 ** (-5.0 - jnp.arange(RET_HEADS, dtype=F32)))
    o = _make_retention("ret")(lg, rq.reshape(bsz, t, qk), rk.reshape(bsz, t, qk), p[:, 2 * qk:4 * qk].reshape(bsz, t, 2 * qk))
    (y,), _ = _make_rowwise(_ret_out_f, "ret_out", 2, 0, 0)((o.reshape(r, 2 * qk), p[:, 4 * qk:]), (), ())
    return y


def _local_loss(h, target):
    bsz, seq, d = target.shape
    t = _padded_len(seq)
    t_real = N_META + seq
    pos = jnp.arange(t, dtype=jnp.int32)
    mask = jnp.tile(((pos >= N_META) & (pos < t_real)).astype(F32)[:, None], (bsz, 1))
    tgt = jnp.concatenate(
        [jnp.zeros((bsz, N_META, d), F32), target, jnp.zeros((bsz, t - t_real, d), F32)], axis=1).reshape(bsz * t, d)
    _, (total,) = _make_rowwise(_loss_f, "loss", 1, 2, 0)((h,), (tgt, mask), ())
    return total[0, 0]


_WEIGHTS = ("meta_tokens", "ev_w_in", "ev_conv_w", "ev_conv_b", "ev_w_rg_a", "ev_b_rg_a", "ev_w_rg_x", "ev_b_rg_x",
            "ev_lru_lambda", "ev_q_norm_g", "ev_w_uq", "ev_kv_norm_g", "ev_w_ukv", "ev_w_out", "od_w_in", "od_w_out",
            "ln_mix_g", "ln_mix_b", "mlp_w1", "mlp_w2", "ln_mlp_g", "ln_mlp_b")


def kernel(x, meta_tokens, ev_w_in, ev_conv_w, ev_conv_b, ev_w_rg_a, ev_b_rg_a, ev_w_rg_x, ev_b_rg_x, ev_lru_lambda, ev_q_norm_g, ev_w_uq, ev_kv_norm_g, ev_w_ukv, ev_w_out, od_w_in, od_w_out, ln_mix_g, ln_mix_b, mlp_w1, mlp_w2, ln_mlp_g, ln_mlp_b, loss_target, m_meta_tokens, m_ev_w_in, m_ev_conv_w, m_ev_conv_b, m_ev_w_rg_a, m_ev_b_rg_a, m_ev_w_rg_x, m_ev_b_rg_x, m_ev_lru_lambda, m_ev_q_norm_g, m_ev_w_uq, m_ev_kv_norm_g, m_ev_w_ukv, m_ev_w_out, m_od_w_in, m_od_w_out, m_ln_mix_g, m_ln_mix_b, m_mlp_w1, m_mlp_w2, m_ln_mlp_g, m_ln_mlp_b, v_meta_tokens, v_ev_w_in, v_ev_conv_w, v_ev_conv_b, v_ev_w_rg_a, v_ev_b_rg_a, v_ev_w_rg_x, v_ev_b_rg_x, v_ev_lru_lambda, v_ev_q_norm_g, v_ev_w_uq, v_ev_kv_norm_g, v_ev_w_ukv, v_ev_w_out, v_od_w_in, v_od_w_out, v_ln_mix_g, v_ln_mix_b, v_mlp_w1, v_mlp_w2, v_ln_mlp_g, v_ln_mlp_b):
    args = locals()
    weights = {n: args[n] for n in _WEIGHTS}
    bsz = x.shape[0]
    my_x, my_y, my_c = _my_place()
    me = 4 * my_x + 2 * my_y + my_c

    big = (("ev_in", ev_w_in[0], True), ("ev_out", ev_w_out[0], False), ("mlp0_w1", mlp_w1[0], True),
           ("mlp0_w2", mlp_w2[0], False), ("od_in", od_w_in[0], True), ("od_out", od_w_out[0], False),
           ("mlp1_w1", mlp_w1[1], True), ("mlp1_w2", mlp_w2[1], False))
    gathers = {}
    for nm, shard, cols in big:
        shard16 = shard.astype(BF16)
        gathers[nm] = (shard16, cols, _exchange_start(shard16, (N_DEV,) + shard16.shape, True, "ag_start_" + nm))
    gather_tokens = tuple(g[2][4] for g in gathers.values())

    def full_weight(nm, after):
        shard16, cols, handle = gathers[nm]
        land = _exchange_wait(handle, True, after, "ag_wait_" + nm)
        land = lax.dynamic_update_index_in_dim(land, shard16, me, 0)
        return _unstack_cols(land) if cols else land.reshape(-1, shard16.shape[1])

    pending = []

    def linear_bwd(nm, x_in, w_full, dy, cols):
        dw = _matmul(x_in, dy, "tn", nm + "_dw")
        n = dw.shape[1] // N_DEV
        if cols:
            stacked = _stack_cols(dw)
            own = lax.dynamic_slice_in_dim(dw, me * n, n, axis=1)
        else:
            stacked = dw.reshape(N_DEV, dw.shape[0] // N_DEV, dw.shape[1])
            own = lax.dynamic_index_in_dim(stacked, me, 0, keepdims=False)
        handle = _exchange_start(stacked, (_N_PEERS,) + stacked.shape[1:], False, "rs_start_" + nm)
        pending.append((nm, own, handle))
        return _matmul(dy, w_full, "nt", nm + "_dx", after=(handle[4],))

    def mlp_fwd(h, l):
        w1 = full_weight(f"mlp{l}_w1", h)
        u = _matmul(h, w1, "nn", f"mlp{l}_w1_fwd")
        a, vjp_act = jax.vjp(functools.partial(_relu2, f"mlp{l}_act"), u)
        w2 = full_weight(f"mlp{l}_w2", a)
        f = _matmul(a, w2, "nn", f"mlp{l}_w2_fwd")
        out, vjp_ln = jax.vjp(functools.partial(_ln_res, f"mlp{l}_ln"), h, f, ln_mlp_g[l:l + 1], ln_mlp_b[l:l + 1])
        return out, (h, w1, a, w2, vjp_act, vjp_ln)

    def mlp_bwd(dout, res, l):
        h, w1, a, w2, vjp_act, vjp_ln = res
        dh, df, dg, db = vjp_ln(dout)
        da = linear_bwd(f"mlp{l}_w2", a, w2, df, False)
        (du,) = vjp_act(da)
        return dh + linear_bwd(f"mlp{l}_w1", h, w1, du, True), dg, db

    h0, vjp_embed = jax.vjp(_embed, meta_tokens, x)
    w_ev_in = full_weight("ev_in", h0)
    p0 = _matmul(h0, w_ev_in, "nn", "ev_in_fwd", after=gather_tokens)
    small = (ev_conv_w, ev_conv_b, ev_w_rg_a, ev_b_rg_a, ev_w_rg_x, ev_b_rg_x, ev_lru_lambda, ev_q_norm_g, ev_w_uq,
             ev_kv_norm_g, ev_w_ukv)
    y0, vjp_even = jax.vjp(lambda p, *s: _even_mixer(p, *s, bsz), p0, *small)
    w_ev_out = full_weight("ev_out", y0)
    mix0 = _matmul(y0, w_ev_out, "nn", "ev_out_fwd")
    h1, vjp_ln0 = jax.vjp(functools.partial(_ln_res, "mix0_ln"), h0, mix0, ln_mix_g[0:1], ln_mix_b[0:1])
    h2, res_mlp0 = mlp_fwd(h1, 0)
    w_od_in = full_weight("od_in", h2)
    p1 = _matmul(h2, w_od_in, "nn", "od_in_fwd")
    y1, vjp_odd = jax.vjp(lambda p: _odd_mixer(p, bsz), p1)
    w_od_out = full_weight("od_out", y1)
    mix1 = _matmul(y1, w_od_out, "nn", "od_out_fwd")
    h3, vjp_ln1 = jax.vjp(functools.partial(_ln_res, "mix1_ln"), h2, mix1, ln_mix_g[1:2], ln_mix_b[1:2])
    h4, res_mlp1 = mlp_fwd(h3, 1)
    loss_local, vjp_loss = jax.vjp(lambda h: _local_loss(h, loss_target), h4)

    (dh4,) = vjp_loss(jnp.ones((), F32))
    dh3, dg_mlp1, db_mlp1 = mlp_bwd(dh4, res_mlp1, 1)
    dh2, dmix1, dg_mix1, db_mix1 = vjp_ln1(dh3)
    (dp1,) = vjp_odd(linear_bwd("od_out", y1, w_od_out, dmix1, False))
    dh2 = dh2 + linear_bwd("od_in", h2, w_od_in, dp1, True)
    dh1, dg_mlp0, db_mlp0 = mlp_bwd(dh2, res_mlp0, 0)
    dh0, dmix0, dg_mix0, db_mix0 = vjp_ln0(dh1)
    dp0, *dsmall = vjp_even(linear_bwd("ev_out", y0, w_ev_out, dmix0, False))
    dh0 = dh0 + linear_bwd("ev_in", h0, w_ev_in, dp0, True)
    g_meta, grad_x = vjp_embed(dh0)
    (g_conv_w, g_conv_b, g_w_rg_a, g_b_rg_a, g_w_rg_x, g_b_rg_x, g_lambda, g_q_norm, g_w_uq, g_kv_norm, g_w_ukv) = dsmall

    rep_names = ("ev_conv_b", "ev_w_rg_a", "ev_b_rg_a", "ev_w_rg_x", "ev_b_rg_x", "ev_lru_lambda", "ev_q_norm_g",
                 "ev_kv_norm_g", "ln_mix_g", "ln_mix_b", "ln_mlp_g", "ln_mlp_b")
    rep_local = (g_conv_b, g_w_rg_a, g_b_rg_a, g_w_rg_x, g_b_rg_x, g_lambda, g_q_norm, g_kv_norm,
                 jnp.concatenate([dg_mix0, dg_mix1]), jnp.concatenate([db_mix0, db_mix1]),
                 jnp.concatenate([dg_mlp0, dg_mlp1]), jnp.concatenate([db_mlp0, db_mlp1]))
    grad_w = dict(zip(rep_names, _allreduce_replicated(rep_local, "rep")))
    grad_w.update(meta_tokens=g_meta, ev_conv_w=g_conv_w, ev_w_uq=g_w_uq, ev_w_ukv=g_w_ukv)

    after, summed = grad_x, {}
    for nm, own, handle in pending:
        land = _exchange_wait(handle, False, after, "rs_wait_" + nm)
        summed[nm] = after = _sum_own_and_peers(own, land, "rs_sum_" + nm)
    grad_w.update(ev_w_in=summed["ev_in"][None], ev_w_out=summed["ev_out"][None], od_w_in=summed["od_in"][None],
                  od_w_out=summed["od_out"][None], mlp_w1=jnp.stack([summed["mlp0_w1"], summed["mlp1_w1"]]),
                  mlp_w2=jnp.stack([summed["mlp0_w2"], summed["mlp1_w2"]]))

    loss = lax.psum(loss_local, MESH_AXES)
    delta, new_m, new_v = {}, {}, {}
    for n in _WEIGHTS:
        w, g, m, v = weights[n], grad_w[n], args["m_" + n], args["v_" + n]
        two_d = (-1, w.shape[-1])
        d2, m2, v2 = _adamw(w.reshape(two_d), g.reshape(two_d), m.reshape(two_d), v.reshape(two_d), "adamw_" + n)
        delta[n], new_m[n], new_v[n] = d2.reshape(w.shape), m2.reshape(w.shape), v2.reshape(w.shape)
    return (loss, grad_x, *[grad_w[n] for n in _WEIGHTS], *[delta[n] for n in _WEIGHTS],
            *[new_m[n] for n in _WEIGHTS], *[new_v[n] for n in _WEIGHTS])
```

```python
import functools
import math

import jax
import jax.numpy as jnp
from jax import lax
from jax.experimental import pallas as pl
from jax.experimental.pallas import tpu as pltpu

F32 = jnp.float32
BF16 = jnp.bfloat16

N_DEV = 8
MESH_AXES = ("x", "y", "c")
LANES = 128
SEQ_BLOCK = 128

N_META = 16
LRU_C = 8.0
MLA_HEADS = 8
MLA_NOPE = 64
MLA_ROPE = 32
MLA_V = 64
RET_HEADS = 4
ROPE_BASE = 10000.0
DEPTH = 2
DN_ALPHA = (2 * DEPTH) ** 0.25
EPS = 1e-5
NEG_INF = -1e30

ADAM_LR = 0.001
ADAM_B1 = 0.9
ADAM_B2 = 0.999
ADAM_EPS = 1e-08
ADAM_WD = 0.01
ADAM_STEP = 10

VMEM_LIMIT = 56 * 1024 * 1024


def _params(*sem):
    return pltpu.CompilerParams(dimension_semantics=sem, vmem_limit_bytes=VMEM_LIMIT)


def _pick(n, cands):
    for c in cands:
        if n % c == 0:
            return c
    return n


def _row_tile(r, width):
    cands = (256, 128, 64, 32, 16, 8) if width <= 1024 else (128, 64, 32, 16, 8)
    return _pick(r, cands)


_DIMS = {"nn": (((1,), (0,)), ((), ())), "nt": (((1,), (1,)), ((), ())), "tn": (((0,), (0,)), ((), ()))}


def _dot(a, b, mode):
    return lax.dot_general(a.astype(BF16), b.astype(BF16), _DIMS[mode], preferred_element_type=F32)


def _matmul(a, b, mode, name, after=(), stacked=False, a_relu2=False, relu2_bwd_of=None):
    if stacked:
        n_blk = b.shape[2] if mode != "tn" else b.shape[1] // N_DEV
    if mode == "nn":
        (m, k), n = a.shape, (N_DEV * n_blk if stacked else b.shape[1])
    elif mode == "nt":
        (m, k), n = a.shape, (b.shape[1] if stacked else b.shape[0])
    else:
        (k, m), n = a.shape, b.shape[1]
    tm = _pick(m, (1088, 1024, 544, 512, 272, 256, 128, 64, 32, 16, 8))
    tn = _pick(n, (512, 256, 128))
    tk = _pick(k, (1088, 1024, 544, 512, 272, 256, 128))
    if stacked and mode in ("nn", "tn"):
        tn = n_blk
    if stacked and mode == "nt":
        tk = n_blk
    nk = k // tk

    out_spec = pl.BlockSpec((tm, tn), lambda i, j, kk: (i, j))
    out_shape = jax.ShapeDtypeStruct((m, n), F32)
    if mode == "nn":
        a_spec = pl.BlockSpec((tm, tk), lambda i, j, kk: (i, kk))
        b_spec = pl.BlockSpec((tk, tn), lambda i, j, kk: (kk, j))
        if stacked:
            b_spec = pl.BlockSpec((None, tk, tn), lambda i, j, kk: (j, kk, 0))
    elif mode == "nt":
        a_spec = pl.BlockSpec((tm, tk), lambda i, j, kk: (i, kk))
        b_spec = pl.BlockSpec((tn, tk), lambda i, j, kk: (j, kk))
        if stacked:
            b_spec = pl.BlockSpec((None, tn, tk), lambda i, j, kk: (kk, j, 0))
    else:
        a_spec = pl.BlockSpec((tk, tm), lambda i, j, kk: (kk, i))
        b_spec = pl.BlockSpec((tk, tn), lambda i, j, kk: (kk, j))
        if stacked:
            out_spec = pl.BlockSpec((None, tm, tn), lambda i, j, kk: (j, i, 0))
            out_shape = jax.ShapeDtypeStruct((N_DEV, m, tn), F32)
    extra = [] if relu2_bwd_of is None else [relu2_bwd_of]
    extra_specs = [pl.BlockSpec((tm, tn), lambda i, j, kk: (i, j))] * len(extra)

    def body(a_ref, b_ref, *rest):
        o_ref = rest[-1]
        kk = pl.program_id(2)
        av = a_ref[...]
        if a_relu2:
            av = jnp.maximum(av, 0.0)
            av = av * av
        part = _dot(av, b_ref[...], mode)

        @pl.when(kk == 0)
        def _():
            o_ref[...] = part

        @pl.when(kk != 0)
        def _():
            o_ref[...] += part

        if relu2_bwd_of is not None:
            @pl.when(kk == nk - 1)
            def _():
                o_ref[...] *= 2.0 * jnp.maximum(rest[0][...], 0.0)

    return pl.pallas_call(
        body,
        name=name,
        grid=(m // tm, n // tn, nk),
        in_specs=[a_spec, b_spec] + extra_specs + [pl.BlockSpec(memory_space=pl.ANY)] * len(after),
        out_specs=out_spec,
        out_shape=out_shape,
        compiler_params=_params("parallel", "parallel", "arbitrary"),
    )(a, b, *extra, *after)


def _group_matmul(a, w, mode, name):
    if mode in ("nn", "nt"):
        g, dk, dn = w.shape
        m = a.shape[0]
        d_in, d_out = (dk, dn) if mode == "nn" else (dn, dk)
        tm = _pick(m, (1088, 1024, 544, 512, 272, 256, 128, 64, 32, 16, 8))

        def body(a_ref, w_ref, o_ref):
            o_ref[...] = _dot(a_ref[...], w_ref[0], mode)

        return pl.pallas_call(
            body,
            name=name,
            grid=(g, m // tm),
            in_specs=[pl.BlockSpec((tm, d_in), lambda h, i: (i, h)), pl.BlockSpec((1, dk, dn), lambda h, i: (h, 0, 0))],
            out_specs=pl.BlockSpec((tm, d_out), lambda h, i: (i, h)),
            out_shape=jax.ShapeDtypeStruct((m, g * d_out), F32),
            compiler_params=_params("parallel", "parallel"),
        )(a, w)
    b = w
    m = a.shape[0]
    dk = dn = LANES
    g = a.shape[1] // dk
    tm = _pick(m, (1088, 1024, 544, 512, 272, 256, 128, 64, 32, 16, 8))

    def body(a_ref, b_ref, o_ref):
        part = _dot(a_ref[...], b_ref[...], "tn")

        @pl.when(pl.program_id(1) == 0)
        def _():
            o_ref[0] = part

        @pl.when(pl.program_id(1) != 0)
        def _():
            o_ref[0] += part

    return pl.pallas_call(
        body,
        name=name,
        grid=(g, m // tm),
        in_specs=[pl.BlockSpec((tm, dk), lambda h, i: (i, h)), pl.BlockSpec((tm, dn), lambda h, i: (i, h))],
        out_specs=pl.BlockSpec((1, dk, dn), lambda h, i: (h, 0, 0)),
        out_shape=jax.ShapeDtypeStruct((g, dk, dn), F32),
        compiler_params=_params("parallel", "arbitrary"),
    )(a, b)


def _make_group_linear(name):
    @jax.custom_vjp
    def op(x, w):
        return _group_matmul(x, w, "nn", name + "_fwd")

    def fwd(x, w):
        return op(x, w), (x, w)

    def bwd(res, dy):
        x, w = res
        return _group_matmul(dy, w, "nt", name + "_dx"), _group_matmul(x, dy, "tn", name + "_dw")

    op.defvjp(fwd, bwd)
    return op


def _my_place():
    return lax.axis_index("x"), lax.axis_index("y"), lax.axis_index("c")


def _all_gather(shard, name):
    shape, dtype = shard.shape, shard.dtype

    def body(x_ref, out_ref, send_sems, recv_sems, local_sem):
        x, y, c = _my_place()
        me, sibling = (x, y, c), (x, y, 1 - c)
        chips = [(1 - x, y), (x, 1 - y), (1 - x, 1 - y)]

        def slot(px, py, pc):
            return out_ref.at[4 * px + 2 * py + pc]

        def copy(k, block, to, src=None):
            return pltpu.make_async_remote_copy(
                src_ref=slot(*block) if src is None else src,
                dst_ref=slot(*block),
                send_sem=send_sems.at[k],
                recv_sem=recv_sems.at[k],
                device_id=to,
                device_id_type=pl.DeviceIdType.MESH,
            )

        mine = pltpu.make_async_copy(x_ref, slot(*me), local_sem)
        mine.start()
        first = [copy(0, me, sibling, src=x_ref)]
        first += [copy(1 + j, me, (*chip, c), src=x_ref) for j, chip in enumerate(chips)]
        for cp in first:
            cp.start()
        passed = [copy(4 + j, (*chip, c), sibling) for j, chip in enumerate(chips)]
        for j, chip in enumerate(chips):
            copy(1 + j, (*chip, c), me).wait_recv()
            passed[j].start()
        copy(0, sibling, me).wait_recv()
        for j, chip in enumerate(chips):
            copy(4 + j, (*chip, 1 - c), me).wait_recv()
        for cp in first + passed:
            cp.wait_send()
        mine.wait()

    return pl.pallas_call(
        body,
        name=name,
        out_shape=jax.ShapeDtypeStruct((N_DEV,) + shape, dtype),
        in_specs=[pl.BlockSpec(memory_space=pl.ANY)],
        out_specs=pl.BlockSpec(memory_space=pl.ANY),
        scratch_shapes=[pltpu.SemaphoreType.DMA((7,)), pltpu.SemaphoreType.DMA((7,)), pltpu.SemaphoreType.DMA],
    )(shard)


def _all_to_all(stacked, name):
    def body(x_ref, out_ref, send_sems, recv_sems, local_sem):
        x, y, c = _my_place()
        me = 4 * x + 2 * y + c
        mine = pltpu.make_async_copy(x_ref.at[me], out_ref.at[me], local_sem)
        mine.start()
        copies = []
        for k in range(1, N_DEV):
            px, py, pc = x ^ ((k >> 2) & 1), y ^ ((k >> 1) & 1), c ^ (k & 1)
            peer = 4 * px + 2 * py + pc
            copies.append(
                pltpu.make_async_remote_copy(
                    src_ref=x_ref.at[peer],
                    dst_ref=out_ref.at[me],
                    send_sem=send_sems.at[k - 1],
                    recv_sem=recv_sems.at[k - 1],
                    device_id=(px, py, pc),
                    device_id_type=pl.DeviceIdType.MESH,
                )
            )
        for cp in copies:
            cp.start()
        for cp in copies:
            cp.wait_recv()
        for cp in copies:
            cp.wait_send()
        mine.wait()

    return pl.pallas_call(
        body,
        name=name,
        out_shape=jax.ShapeDtypeStruct(stacked.shape, stacked.dtype),
        in_specs=[pl.BlockSpec(memory_space=pl.ANY)],
        out_specs=pl.BlockSpec(memory_space=pl.ANY),
        scratch_shapes=[pltpu.SemaphoreType.DMA((7,)), pltpu.SemaphoreType.DMA((7,)), pltpu.SemaphoreType.DMA],
    )(stacked)


def _sum_blocks(stacked, name):
    _, r, c = stacked.shape
    tr = _pick(r, (256, 128, 64, 32, 16, 8))

    def body(x_ref, o_ref):
        s = [x_ref[j] for j in range(N_DEV)]
        o_ref[...] = ((s[0] + s[1]) + (s[2] + s[3])) + ((s[4] + s[5]) + (s[6] + s[7]))

    return pl.pallas_call(
        body,
        name=name,
        grid=(r // tr,),
        in_specs=[pl.BlockSpec((N_DEV, tr, c), lambda i: (0, i, 0))],
        out_specs=pl.BlockSpec((tr, c), lambda i: (i, 0)),
        out_shape=jax.ShapeDtypeStruct((r, c), stacked.dtype),
        compiler_params=_params("parallel"),
    )(stacked)


def _stack_cols(full):
    k, n8 = full.shape
    return full.reshape(k, N_DEV, n8 // N_DEV).transpose(1, 0, 2)


def _unstack_cols(stacked):
    j, k, n = stacked.shape
    return stacked.transpose(1, 0, 2).reshape(k, j * n)


def _split_cols(p, cuts):
    bounds = (0,) + tuple(cuts) + (p.shape[1],)

    @jax.custom_vjp
    def op(z):
        return tuple(z[:, lo:hi] for lo, hi in zip(bounds[:-1], bounds[1:]))

    op.defvjp(lambda z: (op(z), None), lambda _, cots: (jnp.concatenate(cots, axis=1),))
    return op(p)


def _gather_weight(shard, cols, name):
    g = _all_gather(shard.astype(BF16), name)
    return _unstack_cols(g) if cols else g.reshape(-1, shard.shape[1])


def _scatter_grad(full, cols, name):
    if cols:
        st = _stack_cols(full)
    else:
        st = full.reshape(N_DEV, full.shape[0] // N_DEV, full.shape[1])
    return _sum_blocks(_all_to_all(st, name + "_a2a"), name + "_sum")


def _make_fsdp_linear(cols, name):
    @jax.custom_vjp
    def op(x, w_shard, w_full):
        return _matmul(x, w_full, "nn", name + "_fwd")

    def fwd(x, w_shard, w_full):
        return op(x, w_shard, w_full), (x, w_full)

    def bwd(res, dy):
        x, w = res
        dx = _matmul(dy, w, "nt", name + "_dx")
        dw = _matmul(x, dy, "tn", name + "_dw")
        return dx, _scatter_grad(dw, cols, name + "_rs"), jnp.zeros_like(w)

    op.defvjp(fwd, bwd)
    return op


def _make_fsdp_param(name):
    @jax.custom_vjp
    def op(shard, full):
        return full

    def fwd(shard, full):
        return full, None

    def bwd(_, g):
        return _scatter_grad(g, True, name + "_rs"), jnp.zeros_like(g)

    op.defvjp(fwd, bwd)
    return op


def _allreduce_replicated(gs, name):
    flat = jnp.concatenate([g.reshape(-1) for g in gs])
    n = flat.shape[0]
    rows = -(-n // (256 * LANES)) * 256
    packed = jnp.pad(flat, (0, rows * LANES - n)).reshape(rows, LANES)
    total = _sum_blocks(_all_gather(packed, name + "_ag"), name + "_sum").reshape(-1)
    out, off = [], 0
    for g in gs:
        out.append(total[off:off + g.size].reshape(g.shape))
        off += g.size
    return out


_HBM = pl.BlockSpec(memory_space=pltpu.HBM)
_SEM = pl.BlockSpec(memory_space=pltpu.SEMAPHORE)
_SIDE_EFFECT = pltpu.SideEffectType.DATAFLOW_SIDE_EFFECTING
_N_PEERS = N_DEV - 1


def _peer(k):
    x, y, c = _my_place()
    return x ^ ((k >> 2) & 1), y ^ ((k >> 1) & 1), c ^ (k & 1)


def _exchange_start(src, land_shape, gather, name):
    def body(src_ref, land_ref, send_sems, recv_sems, src_thru, land_thru, token):
        x, y, c = _my_place()
        me = 4 * x + 2 * y + c
        for k in range(1, N_DEV):
            px, py, pc = _peer(k)
            pltpu.make_async_remote_copy(
                src_ref=src_ref if gather else src_ref.at[4 * px + 2 * py + pc],
                dst_ref=land_ref.at[me] if gather else land_ref.at[k - 1],
                send_sem=send_sems.at[k - 1],
                recv_sem=recv_sems.at[k - 1],
                device_id=(px, py, pc),
                device_id_type=pl.DeviceIdType.MESH,
            ).start()
        token[...] = jnp.zeros_like(token)

    return pl.pallas_call(
        body,
        name=name,
        out_shape=(
            pltpu.SemaphoreType.DMA((_N_PEERS,)),
            pltpu.SemaphoreType.DMA((_N_PEERS,)),
            pltpu.HBM(src.shape, src.dtype),
            pltpu.HBM(land_shape, src.dtype),
            jax.ShapeDtypeStruct((8, LANES), F32),
        ),
        in_specs=(_HBM, _HBM),
        out_specs=(_SEM, _SEM, _HBM, _HBM, pl.BlockSpec(memory_space=pltpu.VMEM)),
        input_output_aliases={0: 2, 1: 3},
        compiler_params=pltpu.CompilerParams(has_side_effects=_SIDE_EFFECT),
    )(pltpu.with_memory_space_constraint(src, pltpu.HBM),
      pltpu.with_memory_space_constraint(lax.empty(land_shape, src.dtype), pltpu.HBM))


def _exchange_wait(handle, gather, after, name):
    send_sems, recv_sems, src_thru, land_thru, _ = handle

    def body(src_ref, land_ref, send_sems, recv_sems, after_ref, src_dead, got_ref):
        for k in range(1, N_DEV):
            cp = pltpu.make_async_remote_copy(
                src_ref=src_ref if gather else src_ref.at[k],
                dst_ref=land_ref.at[k - 1],
                send_sem=send_sems.at[k - 1],
                recv_sem=recv_sems.at[k - 1],
                device_id=_peer(k),
                device_id_type=pl.DeviceIdType.MESH,
            )
            cp.wait_send()
            cp.wait_recv()

    return pl.pallas_call(
        body,
        name=name,
        out_shape=(pltpu.HBM(src_thru.shape, src_thru.dtype), pltpu.HBM(land_thru.shape, land_thru.dtype)),
        in_specs=(_HBM, _HBM, _SEM, _SEM, pl.BlockSpec(memory_space=pl.ANY)),
        out_specs=(_HBM, _HBM),
        input_output_aliases={0: 0, 1: 1},
        compiler_params=pltpu.CompilerParams(has_side_effects=_SIDE_EFFECT),
    )(src_thru, land_thru, send_sems, recv_sems, after)[1]


def _sum_own_and_peers(own, land, name):
    r, c = own.shape
    tr = _pick(r, (256, 128, 64, 32, 16, 8))

    def body(o_ref, l_ref, out_ref):
        s = [l_ref[j] for j in range(_N_PEERS)]
        out_ref[...] = ((o_ref[...] + s[0]) + (s[1] + s[2])) + ((s[3] + s[4]) + (s[5] + s[6]))

    return pl.pallas_call(
        body,
        name=name,
        grid=(r // tr,),
        in_specs=[pl.BlockSpec((tr, c), lambda i: (i, 0)), pl.BlockSpec((_N_PEERS, tr, c), lambda i: (0, i, 0))],
        out_specs=pl.BlockSpec((tr, c), lambda i: (i, 0)),
        out_shape=jax.ShapeDtypeStruct((r, c), own.dtype),
        compiler_params=_params("parallel"),
    )(own, land)


def _make_rowwise(f, name, n_rows, n_tabs, n_pars):
    n_in = n_rows + n_tabs + n_pars

    def specs(args, tm):
        blocked = [pl.BlockSpec((tm, a.shape[1]), lambda i: (i, 0)) for a in args[: n_rows + n_tabs]]
        whole = [pl.BlockSpec(a.shape, lambda i: (0, 0)) for a in args[n_rows + n_tabs:]]
        return blocked + whole

    def out_struct(args, tm):
        blk = [jax.ShapeDtypeStruct((tm, a.shape[1]), a.dtype) for a in args[: n_rows + n_tabs]]
        blk += [jax.ShapeDtypeStruct(a.shape, a.dtype) for a in args[n_rows + n_tabs:]]
        return jax.eval_shape(f, *blk)

    def fwd_call(*args):
        r = args[0].shape[0]
        tm = _row_tile(r, max(a.shape[1] for a in args[:n_rows]))
        ro, so = out_struct(args, tm)

        def body(*refs):
            vals = [x[...] for x in refs[:n_in]]
            outs = refs[n_in:]
            rv, sv = f(*vals)
            for o, v in zip(outs[: len(ro)], rv):
                o[...] = v
            for o, v in zip(outs[len(ro):], sv):
                @pl.when(pl.program_id(0) == 0)
                def _(o=o, v=v):
                    o[...] = v

                @pl.when(pl.program_id(0) != 0)
                def _(o=o, v=v):
                    o[...] += v

        out_shape = [jax.ShapeDtypeStruct((r, s.shape[1]), s.dtype) for s in ro]
        out_shape += [jax.ShapeDtypeStruct(s.shape, s.dtype) for s in so]
        out_specs = [pl.BlockSpec((tm, s.shape[1]), lambda i: (i, 0)) for s in ro]
        out_specs += [pl.BlockSpec(s.shape, lambda i: (0, 0)) for s in so]
        res = pl.pallas_call(
            body,
            name=name + "_fwd",
            grid=(r // tm,),
            in_specs=specs(args, tm),
            out_specs=out_specs,
            out_shape=out_shape,
            compiler_params=_params("arbitrary" if so else "parallel"),
        )(*args)
        return tuple(res[: len(ro)]), tuple(res[len(ro):])

    def bwd_call(args, cots):
        r = args[0].shape[0]
        tm = _row_tile(r, max(a.shape[1] for a in args[:n_rows]))
        ro, so = out_struct(args, tm)
        crow, csum = cots
        rows, tabs, pars = args[:n_rows], args[n_rows:n_rows + n_tabs], args[n_rows + n_tabs:]
        n_c = len(crow) + len(csum)

        def body(*refs):
            vals = [x[...] for x in refs[:n_in]]
            cv = [x[...] for x in refs[n_in:n_in + n_c]]
            outs = refs[n_in + n_c:]
            tv = vals[n_rows:n_rows + n_tabs]

            def g(*dargs):
                return f(*dargs[:n_rows], *tv, *dargs[n_rows:])

            _, vjp = jax.vjp(g, *vals[:n_rows], *vals[n_rows + n_tabs:])
            d = vjp((tuple(cv[: len(crow)]), tuple(cv[len(crow):])))
            for o, v in zip(outs[:n_rows], d[:n_rows]):
                o[...] = v
            for o, v in zip(outs[n_rows:], d[n_rows:]):
                @pl.when(pl.program_id(0) == 0)
                def _(o=o, v=v):
                    o[...] = v

                @pl.when(pl.program_id(0) != 0)
                def _(o=o, v=v):
                    o[...] += v

        in_specs = specs(args, tm)
        in_specs += [pl.BlockSpec((tm, c.shape[1]), lambda i: (i, 0)) for c in crow]
        in_specs += [pl.BlockSpec(c.shape, lambda i: (0, 0)) for c in csum]
        out_shape = [jax.ShapeDtypeStruct(a.shape, a.dtype) for a in rows + pars]
        out_specs = [pl.BlockSpec((tm, a.shape[1]), lambda i: (i, 0)) for a in rows]
        out_specs += [pl.BlockSpec(a.shape, lambda i: (0, 0)) for a in pars]
        res = pl.pallas_call(
            body,
            name=name + "_bwd",
            grid=(r // tm,),
            in_specs=in_specs,
            out_specs=out_specs,
            out_shape=out_shape,
            compiler_params=_params("arbitrary" if pars else "parallel"),
        )(*args, *crow, *csum)
        return tuple(res[:n_rows]), tuple(res[n_rows:])

    @jax.custom_vjp
    def op(rows, tabs, pars):
        return fwd_call(*rows, *tabs, *pars)

    def fwd(rows, tabs, pars):
        return fwd_call(*rows, *tabs, *pars), (rows, tabs, pars)

    def bwd(res, cots):
        rows, tabs, pars = res
        drows, dpars = bwd_call(tuple(rows) + tuple(tabs) + tuple(pars), cots)
        return drows, tuple(jnp.zeros_like(t) for t in tabs), dpars

    op.defvjp(fwd, bwd)
    return op


def _sigmoid(x):
    return 0.5 * (jnp.tanh(0.5 * x) + 1.0)


@jax.custom_jvp
def _softplus(x):
    e = jnp.exp(-jnp.abs(x))
    u = 1.0 + e
    log1p_e = jnp.where(u == 1.0, e, e * jnp.log(u) / jnp.where(u == 1.0, 1.0, u - 1.0))
    return jnp.maximum(x, 0.0) + log1p_e


@_softplus.defjvp
def _softplus_jvp(primals, tangents):
    (x,), (t,) = primals, tangents
    return _softplus(x), t * _sigmoid(x)


def _gelu(x):
    return 0.5 * x * (1.0 + jnp.tanh(math.sqrt(2.0 / math.pi) * (x + 0.044715 * (x * x * x))))


def _ln_res_f(h, mix, g, b):
    z = DN_ALPHA * h + mix
    mu = jnp.mean(z, axis=-1, keepdims=True)
    zc = z - mu
    var = jnp.mean(zc * zc, axis=-1, keepdims=True)
    return (zc * lax.rsqrt(var + EPS) * g + b,), ()


def _rmsnorm_f(x, g):
    return (x * lax.rsqrt(jnp.mean(x * x, axis=-1, keepdims=True) + EPS) * g,), ()


def _lru_gates_f(ga, gx, xc, b_a, b_x, lam):
    r = _sigmoid(ga + b_a)
    i = _sigmoid(gx + b_x)
    log_a = -LRU_C * r * _softplus(-lam)
    a = jnp.exp(log_a)
    one_minus_a2 = jnp.tanh(-log_a) * (jnp.exp(2.0 * log_a) + 1.0)
    return (a, jnp.sqrt(one_minus_a2) * (i * xc)), ()


def _lru_out_f(hh, p_gate):
    return (hh * _gelu(p_gate),), ()


def _rope_pair_f(x1, x2, cos, sin):
    return (x1 * cos - x2 * sin, x1 * sin + x2 * cos), ()


def _rope_ret_f(q, k, cos2, sin2):
    d = cos2.shape[1]
    half = d // 2
    k_scale = d ** -0.5

    def rope(x):
        outs = []
        for h in range(x.shape[1] // d):
            xh = x[:, h * d:(h + 1) * d]
            rot = jnp.concatenate([xh[:, half:], xh[:, :half]], axis=1)
            outs.append(xh * cos2 + rot * sin2)
        return jnp.concatenate(outs, axis=1)

    return (rope(q), rope(k) * k_scale), ()


def _ret_out_f(o, g):
    d = o.shape[1] // RET_HEADS
    outs = []
    for h in range(RET_HEADS):
        oh = o[:, h * d:(h + 1) * d]
        outs.append(oh * lax.rsqrt(jnp.mean(oh * oh, axis=-1, keepdims=True) + EPS))
    y = jnp.concatenate(outs, axis=1)
    return (g * _sigmoid(g) * y,), ()


def _loss_f(y, t, mask):
    e = (y - t) * mask
    per_row = jnp.sum(e * e, axis=-1, keepdims=True) * (0.5 / y.shape[1])
    total = jnp.sum(per_row, axis=0, keepdims=True)
    return (), (jnp.broadcast_to(total, (1, LANES)),)


def _shift_down(x, s):
    if s == 0:
        return x
    t = x.shape[0]
    row = lax.broadcasted_iota(jnp.int32, x.shape, 0)
    return jnp.where(row >= s, pltpu.roll(x, s, 0), 0.0)


def _shift_up(x, s):
    if s == 0:
        return x
    t = x.shape[0]
    row = lax.broadcasted_iota(jnp.int32, x.shape, 0)
    return jnp.where(row < t - s, pltpu.roll(x, t - s, 0), 0.0)


def _conv_fwd(x, w, b, name):
    bsz, t, c = x.shape
    width = w.shape[0]

    def body(x_ref, w_ref, b_ref, y_ref):
        xv = x_ref[0]
        acc = jnp.broadcast_to(b_ref[...], xv.shape)
        for k in range(width):
            acc = acc + w_ref[k:k + 1, :] * _shift_down(xv, width - 1 - k)
        y_ref[0] = acc

    return pl.pallas_call(
        body,
        name=name,
        grid=(bsz, c // LANES),
        in_specs=[
            pl.BlockSpec((1, t, LANES), lambda i, j: (i, 0, j)),
            pl.BlockSpec((width, LANES), lambda i, j: (0, j)),
            pl.BlockSpec((1, LANES), lambda i, j: (0, j)),
        ],
        out_specs=pl.BlockSpec((1, t, LANES), lambda i, j: (i, 0, j)),
        out_shape=jax.ShapeDtypeStruct(x.shape, F32),
        compiler_params=_params("parallel", "parallel"),
    )(x, w, b)


def _conv_bwd(x, w, dy, name):
    bsz, t, c = x.shape
    width = w.shape[0]

    def body(x_ref, w_ref, dy_ref, dx_ref, dw_ref, db_ref):
        xv, g = x_ref[0], dy_ref[0]
        dx = jnp.zeros_like(xv)
        dws = []
        for k in range(width):
            s = width - 1 - k
            dx = dx + w_ref[k:k + 1, :] * _shift_up(g, s)
            dws.append(jnp.sum(g * _shift_down(xv, s), axis=0, keepdims=True))
        dx_ref[0] = dx
        dw = jnp.concatenate(dws, axis=0)
        db = jnp.sum(g, axis=0, keepdims=True)

        @pl.when(pl.program_id(1) == 0)
        def _():
            dw_ref[...] = dw
            db_ref[...] = db

        @pl.when(pl.program_id(1) != 0)
        def _():
            dw_ref[...] += dw
            db_ref[...] += db

    return pl.pallas_call(
        body,
        name=name,
        grid=(c // LANES, bsz),
        in_specs=[
            pl.BlockSpec((1, t, LANES), lambda j, i: (i, 0, j)),
            pl.BlockSpec((width, LANES), lambda j, i: (0, j)),
            pl.BlockSpec((1, t, LANES), lambda j, i: (i, 0, j)),
        ],
        out_specs=[
            pl.BlockSpec((1, t, LANES), lambda j, i: (i, 0, j)),
            pl.BlockSpec((width, LANES), lambda j, i: (0, j)),
            pl.BlockSpec((1, LANES), lambda j, i: (0, j)),
        ],
        out_shape=[
            jax.ShapeDtypeStruct(x.shape, F32),
            jax.ShapeDtypeStruct(w.shape, F32),
            jax.ShapeDtypeStruct((1, c), F32),
        ],
        compiler_params=_params("parallel", "arbitrary"),
    )(x, w, dy)


def _make_conv(name):
    @jax.custom_vjp
    def op(x, w, b):
        return _conv_fwd(x, w, b, name + "_fwd")

    def fwd(x, w, b):
        return op(x, w, b), (x, w)

    def bwd(res, dy):
        x, w = res
        return tuple(_conv_bwd(x, w, dy, name + "_bwd"))

    op.defvjp(fwd, bwd)
    return op


def _scan_fwd(a, b, name):
    bsz, t, c = a.shape
    cw = _pick(c, (4 * LANES, 2 * LANES, LANES))

    def body(a_ref, b_ref, h_ref):
        def step(i, h):
            h = a_ref[0, pl.ds(i, 1), :] * h + b_ref[0, pl.ds(i, 1), :]
            h_ref[0, pl.ds(i, 1), :] = h
            return h

        lax.fori_loop(0, t, step, jnp.zeros((1, cw), F32), unroll=8)

    spec = pl.BlockSpec((1, t, cw), lambda i, j: (i, 0, j))
    return pl.pallas_call(
        body,
        name=name,
        grid=(bsz, c // cw),
        in_specs=[spec, spec],
        out_specs=spec,
        out_shape=jax.ShapeDtypeStruct(a.shape, F32),
        compiler_params=_params("parallel", "parallel"),
    )(a, b)


def _scan_bwd(a, h, g, name):
    bsz, t, c = a.shape
    cw = _pick(c, (2 * LANES, LANES))

    def body(a_ref, h_ref, g_ref, da_ref, db_ref):
        def step(n, carry):
            i = t - 1 - n
            lam = g_ref[0, pl.ds(i, 1), :] + carry
            db_ref[0, pl.ds(i, 1), :] = lam
            prev = jnp.where(i > 0, h_ref[0, pl.ds(jnp.maximum(i - 1, 0), 1), :], 0.0)
            da_ref[0, pl.ds(i, 1), :] = lam * prev
            return a_ref[0, pl.ds(i, 1), :] * lam

        lax.fori_loop(0, t, step, jnp.zeros((1, cw), F32), unroll=8)

    spec = pl.BlockSpec((1, t, cw), lambda i, j: (i, 0, j))
    return pl.pallas_call(
        body,
        name=name,
        grid=(bsz, c // cw),
        in_specs=[spec, spec, spec],
        out_specs=[spec, spec],
        out_shape=[jax.ShapeDtypeStruct(a.shape, F32)] * 2,
        compiler_params=_params("parallel", "parallel"),
    )(a, h, g)


def _make_scan(name):
    @jax.custom_vjp
    def op(a, b):
        return _scan_fwd(a, b, name + "_fwd")

    def fwd(a, b):
        h = op(a, b)
        return h, (a, h)

    def bwd(res, g):
        a, h = res
        da, db = _scan_bwd(a, h, g, name + "_bwd")
        return da, db

    op.defvjp(fwd, bwd)
    return op


def _query_blocks(t):
    blocks, start = [], 0
    while start < t:
        rows = 2 * SEQ_BLOCK if start + 2 * SEQ_BLOCK <= t else SEQ_BLOCK
        blocks.append((start, rows))
        start += rows
    return blocks


def _attn_probs(q, k, start, scale):
    tq, tk = q.shape[0], k.shape[0]
    s = _dot(q, k, "nt") * scale
    qpos = start + lax.broadcasted_iota(jnp.int32, (tq, tk), 0)
    kpos = lax.broadcasted_iota(jnp.int32, (tq, tk), 1)
    s = jnp.where(kpos <= qpos, s, NEG_INF)
    e = jnp.exp(s - jnp.max(s, axis=-1, keepdims=True))
    return e / jnp.sum(e, axis=-1, keepdims=True)


def _attn_fwd(q, k, v, name):
    g, t, dq = q.shape
    dv = v.shape[2]
    scale = dq ** -0.5

    def body(q_ref, k_ref, v_ref, o_ref):
        for start, rows in _query_blocks(t):
            n = start + rows
            p = _attn_probs(q_ref[0, start:n, :], k_ref[0, :n, :], start, scale)
            o_ref[0, start:n, :] = _dot(p, v_ref[0, :n, :], "nn")

    return pl.pallas_call(
        body,
        name=name,
        grid=(g,),
        in_specs=[
            pl.BlockSpec((1, t, dq), lambda b: (b, 0, 0)),
            pl.BlockSpec((1, t, dq), lambda b: (b, 0, 0)),
            pl.BlockSpec((1, t, dv), lambda b: (b, 0, 0)),
        ],
        out_specs=pl.BlockSpec((1, t, dv), lambda b: (b, 0, 0)),
        out_shape=jax.ShapeDtypeStruct((g, t, dv), F32),
        compiler_params=_params("parallel"),
    )(q, k, v)


def _attn_bwd(q, k, v, do, name):
    g, t, dq = q.shape
    dv = v.shape[2]
    scale = dq ** -0.5

    def body(q_ref, k_ref, v_ref, do_ref, dq_ref, dk_ref, dv_ref):
        for start, rows in reversed(_query_blocks(t)):
            n = start + rows
            qb, dob = q_ref[0, start:n, :], do_ref[0, start:n, :]
            kk, vv = k_ref[0, :n, :], v_ref[0, :n, :]
            p = _attn_probs(qb, kk, start, scale)
            dp = _dot(dob, vv, "nt")
            ds = p * (dp - jnp.sum(dp * p, axis=-1, keepdims=True)) * scale
            dq_ref[0, start:n, :] = _dot(ds, kk, "nn")
            if n == t:
                dk_ref[0] = _dot(ds, qb, "tn")
                dv_ref[0] = _dot(p, dob, "tn")
            else:
                dk_ref[0, :n, :] += _dot(ds, qb, "tn")
                dv_ref[0, :n, :] += _dot(p, dob, "tn")

    qk_spec = pl.BlockSpec((1, t, dq), lambda b: (b, 0, 0))
    v_spec = pl.BlockSpec((1, t, dv), lambda b: (b, 0, 0))
    return pl.pallas_call(
        body,
        name=name,
        grid=(g,),
        in_specs=[qk_spec, qk_spec, v_spec, v_spec],
        out_specs=[qk_spec, qk_spec, v_spec],
        out_shape=[
            jax.ShapeDtypeStruct(q.shape, F32),
            jax.ShapeDtypeStruct(k.shape, F32),
            jax.ShapeDtypeStruct(v.shape, F32),
        ],
        compiler_params=_params("parallel"),
    )(q, k, v, do)


def _make_attention(name):
    @jax.custom_vjp
    def op(q, k, v):
        return _attn_fwd(q, k, v, name + "_fwd")

    def fwd(q, k, v):
        return op(q, k, v), (q, k, v)

    def bwd(res, do):
        return tuple(_attn_bwd(*res, do, name + "_bwd"))

    op.defvjp(fwd, bwd)
    return op


RET_KEY_CHUNK = 512


def _key_chunks(t):
    return [(c, min(RET_KEY_CHUNK, t - c)) for c in range(0, t, RET_KEY_CHUNK)]


def _decay(q0, k0, tq, tk, log_gamma):
    row = lax.broadcasted_iota(jnp.int32, (tq, 1), 0)
    col = (q0 - k0) - lax.broadcasted_iota(jnp.int32, (1, tk), 1)
    outer = jnp.exp(log_gamma * row.astype(F32)) * jnp.exp(log_gamma * col.astype(F32))
    return jnp.where(row + col >= 0, outer, 0.0)


def _ret_specs(t, dk, dv, heads):
    tq = SEQ_BLOCK
    return (
        pl.BlockSpec(memory_space=pltpu.SMEM),
        pl.BlockSpec((1, tq, dk), lambda b, h, i: (b, i, h)),
        pl.BlockSpec((1, t, dk), lambda b, h, i: (b, 0, h)),
        pl.BlockSpec((1, t, dv), lambda b, h, i: (b, 0, h)),
        pl.BlockSpec((1, tq, dv), lambda b, h, i: (b, i, h)),
    )


def _ret_fwd(lg, q, k, v, name):
    bsz, t, hdk = q.shape
    heads = lg.shape[0]
    dk, dv = hdk // heads, v.shape[2] // heads
    lg_spec, q_spec, k_spec, v_spec, o_spec = _ret_specs(t, dk, dv, heads)

    def body(lg_ref, q_ref, k_ref, v_ref, o_ref):
        q0 = pl.program_id(2) * SEQ_BLOCK
        lgh = lg_ref[pl.program_id(1)]
        o_ref[0] = jnp.zeros((SEQ_BLOCK, dv), F32)
        for c0, cw in _key_chunks(t):
            @pl.when(c0 < q0 + SEQ_BLOCK)
            def _(c0=c0, cw=cw):
                d = _decay(q0, c0, SEQ_BLOCK, cw, lgh)
                a = _dot(q_ref[0], k_ref[0, c0:c0 + cw, :], "nt") * d
                o_ref[0] += _dot(a, v_ref[0, c0:c0 + cw, :], "nn")

    return pl.pallas_call(
        body,
        name=name,
        grid=(bsz, heads, t // SEQ_BLOCK),
        in_specs=[lg_spec, q_spec, k_spec, v_spec],
        out_specs=o_spec,
        out_shape=jax.ShapeDtypeStruct(v.shape, F32),
        compiler_params=_params("parallel", "parallel", "parallel"),
    )(lg, q, k, v)


def _ret_bwd(lg, q, k, v, do, name):
    bsz, t, hdk = q.shape
    heads = lg.shape[0]
    dk, dv = hdk // heads, v.shape[2] // heads
    lg_spec, q_spec, k_spec, v_spec, o_spec = _ret_specs(t, dk, dv, heads)

    def body(lg_ref, q_ref, k_ref, v_ref, do_ref, dq_ref, dk_ref, dv_ref):
        q0 = pl.program_id(2) * SEQ_BLOCK
        lgh = lg_ref[pl.program_id(1)]

        @pl.when(pl.program_id(2) == 0)
        def _():
            dk_ref[0] = jnp.zeros((t, dk), F32)
            dv_ref[0] = jnp.zeros((t, dv), F32)

        dq_ref[0] = jnp.zeros((SEQ_BLOCK, dk), F32)
        for c0, cw in _key_chunks(t):
            @pl.when(c0 < q0 + SEQ_BLOCK)
            def _(c0=c0, cw=cw):
                qb, dob = q_ref[0], do_ref[0]
                kk, vv = k_ref[0, c0:c0 + cw, :], v_ref[0, c0:c0 + cw, :]
                d = _decay(q0, c0, SEQ_BLOCK, cw, lgh)
                a = _dot(qb, kk, "nt") * d
                ds = _dot(dob, vv, "nt") * d
                dq_ref[0] += _dot(ds, kk, "nn")
                dk_ref[0, c0:c0 + cw, :] += _dot(ds, qb, "tn")
                dv_ref[0, c0:c0 + cw, :] += _dot(a, dob, "tn")

    return pl.pallas_call(
        body,
        name=name,
        grid=(bsz, heads, t // SEQ_BLOCK),
        in_specs=[lg_spec, q_spec, k_spec, v_spec, o_spec],
        out_specs=[q_spec, k_spec, v_spec],
        out_shape=[
            jax.ShapeDtypeStruct(q.shape, F32),
            jax.ShapeDtypeStruct(k.shape, F32),
            jax.ShapeDtypeStruct(v.shape, F32),
        ],
        compiler_params=_params("parallel", "parallel", "arbitrary"),
    )(lg, q, k, v, do)


def _make_retention(name):
    @jax.custom_vjp
    def op(lg, q, k, v):
        return _ret_fwd(lg, q, k, v, name + "_fwd")

    def fwd(lg, q, k, v):
        return op(lg, q, k, v), (lg, q, k, v)

    def bwd(res, do):
        lg = res[0]
        return (jnp.zeros_like(lg),) + tuple(_ret_bwd(*res, do, name + "_bwd"))

    op.defvjp(fwd, bwd)
    return op


def _adamw(w, g, m, v, name):
    r, c = w.shape
    tr = _pick(r, (256, 128, 64, 32, 16, 8))

    def body(w_ref, g_ref, m_ref, v_ref, d_ref, nm_ref, nv_ref):
        gv = g_ref[...]
        nm = ADAM_B1 * m_ref[...] + (1.0 - ADAM_B1) * gv
        nv = ADAM_B2 * v_ref[...] + (1.0 - ADAM_B2) * (gv * gv)
        m_hat = nm / (1.0 - ADAM_B1 ** ADAM_STEP)
        v_hat = nv / (1.0 - ADAM_B2 ** ADAM_STEP)
        d_ref[...] = -ADAM_LR * (m_hat / (jnp.sqrt(v_hat) + ADAM_EPS) + ADAM_WD * w_ref[...])
        nm_ref[...] = nm
        nv_ref[...] = nv

    spec = pl.BlockSpec((tr, c), lambda i: (i, 0))
    return pl.pallas_call(
        body,
        name=name,
        grid=(r // tr,),
        in_specs=[spec] * 4,
        out_specs=[spec] * 3,
        out_shape=[jax.ShapeDtypeStruct((r, c), F32)] * 3,
        compiler_params=_params("parallel"),
    )(w, g, m, v)


def _rope_tables(t, half, reps):
    inv = ROPE_BASE ** (-jnp.arange(half, dtype=F32) / half)
    ang = jnp.arange(t, dtype=jnp.int32).astype(F32)[:, None] * inv[None, :]
    return jnp.tile(jnp.cos(ang), (1, reps)), jnp.tile(jnp.sin(ang), (1, reps))


def _padded_len(seq):
    return -(-(N_META + seq) // SEQ_BLOCK) * SEQ_BLOCK


def _embed(meta_shard, x, meta_full):
    bsz, seq, d = x.shape
    t = _padded_len(seq)
    meta = _make_fsdp_param("meta")(meta_shard, meta_full)
    return jnp.concatenate(
        [jnp.broadcast_to(meta[None], (bsz, N_META, d)), x, jnp.zeros((bsz, t - N_META - seq, d), F32)], axis=1
    ).reshape(bsz * t, d)


def _ln_res(name, h, mix, g, b):
    return _make_rowwise(_ln_res_f, name, 2, 0, 2)((h, mix), (), (g, b))[0][0]


def _even_mixer(p, conv_w_shard, conv_b, w_rg_a, b_rg_a, w_rg_x, b_rg_x, lru_lambda, q_norm_g, w_uq_shard,
                kv_norm_g, w_ukv_shard, gathered, bsz):
    conv_w_full, w_uq_full, w_ukv_full = gathered
    r = p.shape[0]
    t = r // bsz

    def tile_rows(tab):
        return jnp.tile(tab, (bsz, 1))

    lru_w = w_rg_a.shape[2] * w_rg_a.shape[1]
    q_rank, kv_rank = q_norm_g.shape[1], kv_norm_g.shape[1]
    p_gate, p_rec, p_q, p_kv, p_kpe = _split_cols(
        p, (lru_w, 2 * lru_w, 2 * lru_w + q_rank, 2 * lru_w + q_rank + kv_rank))

    conv_w = _make_fsdp_param("conv_w")(conv_w_shard[0], conv_w_full)
    xc = _make_conv("conv")(p_rec.reshape(bsz, t, lru_w), conv_w, conv_b).reshape(r, lru_w)
    ga = _make_group_linear("rg_a")(xc, w_rg_a[0])
    gx = _make_group_linear("rg_x")(xc, w_rg_x[0])
    (a, bb), _ = _make_rowwise(_lru_gates_f, "lru_gates", 3, 0, 3)((ga, gx, xc), (), (b_rg_a, b_rg_x, lru_lambda))
    hh = _make_scan("lru_scan")(a.reshape(bsz, t, lru_w), bb.reshape(bsz, t, lru_w)).reshape(r, lru_w)
    (y_rec,), _ = _make_rowwise(_lru_out_f, "lru_out", 2, 0, 0)((hh, p_gate), (), ())

    (qn,), _ = _make_rowwise(_rmsnorm_f, "q_norm", 1, 0, 1)((p_q,), (), (q_norm_g,))
    (kvn,), _ = _make_rowwise(_rmsnorm_f, "kv_norm", 1, 0, 1)((p_kv,), (), (kv_norm_g,))
    q = _make_fsdp_linear(True, "ev_uq")(qn, w_uq_shard[0], w_uq_full).reshape(r, MLA_HEADS, MLA_NOPE + MLA_ROPE)
    kv = _make_fsdp_linear(True, "ev_ukv")(kvn, w_ukv_shard[0], w_ukv_full).reshape(r, MLA_HEADS, MLA_NOPE + MLA_V)
    half = MLA_ROPE // 2
    cos_h, sin_h = _rope_tables(t, half, MLA_HEADS)
    q1 = q[:, :, MLA_NOPE:MLA_NOPE + half].reshape(r, MLA_HEADS * half)
    q2 = q[:, :, MLA_NOPE + half:].reshape(r, MLA_HEADS * half)
    (q1, q2), _ = _make_rowwise(_rope_pair_f, "rope_q", 2, 2, 0)((q1, q2), (tile_rows(cos_h), tile_rows(sin_h)), ())
    (k1, k2), _ = _make_rowwise(_rope_pair_f, "rope_k", 2, 2, 0)(
        (p_kpe[:, :half], p_kpe[:, half:]), (tile_rows(cos_h[:, :half]), tile_rows(sin_h[:, :half])), ())
    q_full = jnp.concatenate(
        [q[:, :, :MLA_NOPE], q1.reshape(r, MLA_HEADS, half), q2.reshape(r, MLA_HEADS, half)], axis=2)
    k_pe = jnp.broadcast_to(jnp.concatenate([k1, k2], axis=1)[:, None, :], (r, MLA_HEADS, MLA_ROPE))
    k_full = jnp.concatenate([kv[:, :, :MLA_NOPE], k_pe], axis=2)

    def heads_major(z):
        return z.reshape(bsz, t, MLA_HEADS, z.shape[-1]).transpose(0, 2, 1, 3).reshape(bsz * MLA_HEADS, t, z.shape[-1])

    o = _make_attention("mla")(heads_major(q_full), heads_major(k_full), heads_major(kv[:, :, MLA_NOPE:]))
    y_att = o.reshape(bsz, MLA_HEADS, t, MLA_V).transpose(0, 2, 1, 3).reshape(r, MLA_HEADS * MLA_V)
    return jnp.concatenate([y_rec, y_att], axis=1)


def _odd_mixer(p, bsz):
    r = p.shape[0]
    t = r // bsz

    def tile_rows(tab):
        return jnp.tile(tab, (bsz, 1))

    qk = p.shape[1] // 6
    dk = qk // RET_HEADS
    cos2, sin2 = _rope_tables(t, dk // 2, 2)
    sin2 = jnp.concatenate([-sin2[:, :dk // 2], sin2[:, dk // 2:]], axis=1)
    p_q, p_k, p_v, p_g = _split_cols(p, (qk, 2 * qk, 4 * qk))
    (rq, rk), _ = _make_rowwise(_rope_ret_f, "rope_ret", 2, 2, 0)((p_q, p_k), (tile_rows(cos2), tile_rows(sin2)), ())
    lg = jnp.log(1.0 - 2.0 ** (-5.0 - jnp.arange(RET_HEADS, dtype=F32)))
    o = _make_retention("ret")(lg, rq.reshape(bsz, t, qk), rk.reshape(bsz, t, qk), p_v.reshape(bsz, t, 2 * qk))
    (y,), _ = _make_rowwise(_ret_out_f, "ret_out", 2, 0, 0)((o.reshape(r, 2 * qk), p_g), (), ())
    return y


def _local_loss(h, target):
    bsz, seq, d = target.shape
    t = _padded_len(seq)
    t_real = N_META + seq
    pos = jnp.arange(t, dtype=jnp.int32)
    mask = jnp.tile(((pos >= N_META) & (pos < t_real)).astype(F32)[:, None], (bsz, 1))
    tgt = jnp.concatenate(
        [jnp.zeros((bsz, N_META, d), F32), target, jnp.zeros((bsz, t - t_real, d), F32)], axis=1).reshape(bsz * t, d)
    _, (total,) = _make_rowwise(_loss_f, "loss", 1, 2, 0)((h,), (tgt, mask), ())
    return total[0, 0]


_WEIGHTS = ("meta_tokens", "ev_w_in", "ev_conv_w", "ev_conv_b", "ev_w_rg_a", "ev_b_rg_a", "ev_w_rg_x", "ev_b_rg_x",
            "ev_lru_lambda", "ev_q_norm_g", "ev_w_uq", "ev_kv_norm_g", "ev_w_ukv", "ev_w_out", "od_w_in", "od_w_out",
            "ln_mix_g", "ln_mix_b", "mlp_w1", "mlp_w2", "ln_mlp_g", "ln_mlp_b")


def kernel(x, meta_tokens, ev_w_in, ev_conv_w, ev_conv_b, ev_w_rg_a, ev_b_rg_a, ev_w_rg_x, ev_b_rg_x, ev_lru_lambda, ev_q_norm_g, ev_w_uq, ev_kv_norm_g, ev_w_ukv, ev_w_out, od_w_in, od_w_out, ln_mix_g, ln_mix_b, mlp_w1, mlp_w2, ln_mlp_g, ln_mlp_b, loss_target, m_meta_tokens, m_ev_w_in, m_ev_conv_w, m_ev_conv_b, m_ev_w_rg_a, m_ev_b_rg_a, m_ev_w_rg_x, m_ev_b_rg_x, m_ev_lru_lambda, m_ev_q_norm_g, m_ev_w_uq, m_ev_kv_norm_g, m_ev_w_ukv, m_ev_w_out, m_od_w_in, m_od_w_out, m_ln_mix_g, m_ln_mix_b, m_mlp_w1, m_mlp_w2, m_ln_mlp_g, m_ln_mlp_b, v_meta_tokens, v_ev_w_in, v_ev_conv_w, v_ev_conv_b, v_ev_w_rg_a, v_ev_b_rg_a, v_ev_w_rg_x, v_ev_b_rg_x, v_ev_lru_lambda, v_ev_q_norm_g, v_ev_w_uq, v_ev_kv_norm_g, v_ev_w_ukv, v_ev_w_out, v_od_w_in, v_od_w_out, v_ln_mix_g, v_ln_mix_b, v_mlp_w1, v_mlp_w2, v_ln_mlp_g, v_ln_mlp_b):
    args = locals()
    weights = {n: args[n] for n in _WEIGHTS}
    bsz = x.shape[0]
    my_x, my_y, my_c = _my_place()
    me = 4 * my_x + 2 * my_y + my_c

    big = (("ev_in", ev_w_in[0], True), ("ev_out", ev_w_out[0], False), ("mlp0_w1", mlp_w1[0], True),
           ("mlp0_w2", mlp_w2[0], False), ("od_in", od_w_in[0], True), ("od_out", od_w_out[0], False),
           ("mlp1_w1", mlp_w1[1], True), ("mlp1_w2", mlp_w2[1], False))
    small_sharded = (("meta", meta_tokens, F32), ("conv_w", ev_conv_w[0], F32), ("ev_uq", ev_w_uq[0], BF16),
                     ("ev_ukv", ev_w_ukv[0], BF16))
    gathers = {}
    for nm, shard16, cols in (tuple((nm, s.astype(dt), True) for nm, s, dt in small_sharded)
                              + tuple((nm, s.astype(BF16), cols) for nm, s, cols in big)):
        gathers[nm] = (shard16, cols, _exchange_start(shard16, (N_DEV,) + shard16.shape, True, "ag_start_" + nm))
    gather_tokens = tuple(g[2][4] for g in gathers.values())

    def full_weight(nm, after):
        shard16, cols, handle = gathers[nm]
        land = _exchange_wait(handle, True, after, "ag_wait_" + nm)
        land = lax.dynamic_update_index_in_dim(land, shard16, me, 0)
        if cols and shard16.shape[1] % LANES == 0:
            return land, True
        return (_unstack_cols(land) if cols else land.reshape(-1, shard16.shape[1])), False

    meta_full, conv_w_full, w_uq_full, w_ukv_full = (
        _unstack_cols(lax.dynamic_update_index_in_dim(
            _exchange_wait(gathers[nm][2], True, gather_tokens[-1], "ag_wait_" + nm), gathers[nm][0], me, 0))
        for nm, _, _ in small_sharded)

    pending = []

    def linear_bwd(nm, x_in, w, dy, cols, **fused):
        w_full, w_stacked = w
        a_relu2 = fused.pop("a_relu2", False)
        if w_stacked:
            stacked = _matmul(x_in, dy, "tn", nm + "_dw", stacked=True, a_relu2=a_relu2)
            own = lax.dynamic_index_in_dim(stacked, me, 0, keepdims=False)
        else:
            dw = _matmul(x_in, dy, "tn", nm + "_dw", a_relu2=a_relu2)
            n = dw.shape[1] // N_DEV
            if cols:
                stacked = _stack_cols(dw)
                own = lax.dynamic_slice_in_dim(dw, me * n, n, axis=1)
            else:
                stacked = dw.reshape(N_DEV, dw.shape[0] // N_DEV, dw.shape[1])
                own = lax.dynamic_index_in_dim(stacked, me, 0, keepdims=False)
        handle = _exchange_start(stacked, (_N_PEERS,) + stacked.shape[1:], False, "rs_start_" + nm)
        pending.append((nm, own, handle))
        return _matmul(dy, w_full, "nt", nm + "_dx", after=(handle[4],), stacked=w_stacked, **fused)

    def linear_fwd(nm, x_in, w, **fused):
        return _matmul(x_in, w[0], "nn", nm + "_fwd", stacked=w[1], **fused)

    def mlp_fwd(h, l):
        w1 = full_weight(f"mlp{l}_w1", h)
        u = linear_fwd(f"mlp{l}_w1", h, w1)
        w2 = full_weight(f"mlp{l}_w2", u)
        f = linear_fwd(f"mlp{l}_w2", u, w2, a_relu2=True)
        out, vjp_ln = jax.vjp(functools.partial(_ln_res, f"mlp{l}_ln"), h, f, ln_mlp_g[l:l + 1], ln_mlp_b[l:l + 1])
        return out, (h, w1, u, w2, vjp_ln)

    def mlp_bwd(dout, res, l):
        h, w1, u, w2, vjp_ln = res
        dh, df, dg, db = vjp_ln(dout)
        du = linear_bwd(f"mlp{l}_w2", u, w2, df, False, a_relu2=True, relu2_bwd_of=u)
        return dh + linear_bwd(f"mlp{l}_w1", h, w1, du, True), dg, db

    h0, vjp_embed = jax.vjp(lambda m, xx: _embed(m, xx, meta_full), meta_tokens, x)
    w_ev_in = full_weight("ev_in", h0)
    p0 = _matmul(h0, w_ev_in[0], "nn", "ev_in_fwd", after=gather_tokens, stacked=w_ev_in[1])
    small = (ev_conv_w, ev_conv_b, ev_w_rg_a, ev_b_rg_a, ev_w_rg_x, ev_b_rg_x, ev_lru_lambda, ev_q_norm_g, ev_w_uq,
             ev_kv_norm_g, ev_w_ukv)
    y0, vjp_even = jax.vjp(lambda p, *s: _even_mixer(p, *s, (conv_w_full, w_uq_full, w_ukv_full), bsz), p0, *small)
    w_ev_out = full_weight("ev_out", y0)
    mix0 = linear_fwd("ev_out", y0, w_ev_out)
    h1, vjp_ln0 = jax.vjp(functools.partial(_ln_res, "mix0_ln"), h0, mix0, ln_mix_g[0:1], ln_mix_b[0:1])
    h2, res_mlp0 = mlp_fwd(h1, 0)
    w_od_in = full_weight("od_in", h2)
    p1 = linear_fwd("od_in", h2, w_od_in)
    y1, vjp_odd = jax.vjp(lambda p: _odd_mixer(p, bsz), p1)
    w_od_out = full_weight("od_out", y1)
    mix1 = linear_fwd("od_out", y1, w_od_out)
    h3, vjp_ln1 = jax.vjp(functools.partial(_ln_res, "mix1_ln"), h2, mix1, ln_mix_g[1:2], ln_mix_b[1:2])
    h4, res_mlp1 = mlp_fwd(h3, 1)
    loss_local, vjp_loss = jax.vjp(lambda h: _local_loss(h, loss_target), h4)

    (dh4,) = vjp_loss(jnp.ones((), F32))
    dh3, dg_mlp1, db_mlp1 = mlp_bwd(dh4, res_mlp1, 1)
    dh2, dmix1, dg_mix1, db_mix1 = vjp_ln1(dh3)
    (dp1,) = vjp_odd(linear_bwd("od_out", y1, w_od_out, dmix1, False))
    dh2 = dh2 + linear_bwd("od_in", h2, w_od_in, dp1, True)
    dh1, dg_mlp0, db_mlp0 = mlp_bwd(dh2, res_mlp0, 0)
    dh0, dmix0, dg_mix0, db_mix0 = vjp_ln0(dh1)
    dp0, *dsmall = vjp_even(linear_bwd("ev_out", y0, w_ev_out, dmix0, False))
    dh0 = dh0 + linear_bwd("ev_in", h0, w_ev_in, dp0, True)
    g_meta, grad_x = vjp_embed(dh0)
    (g_conv_w, g_conv_b, g_w_rg_a, g_b_rg_a, g_w_rg_x, g_b_rg_x, g_lambda, g_q_norm, g_w_uq, g_kv_norm, g_w_ukv) = dsmall

    rep_names = ("ev_conv_b", "ev_w_rg_a", "ev_b_rg_a", "ev_w_rg_x", "ev_b_rg_x", "ev_lru_lambda", "ev_q_norm_g",
                 "ev_kv_norm_g", "ln_mix_g", "ln_mix_b", "ln_mlp_g", "ln_mlp_b")
    rep_local = (g_conv_b, g_w_rg_a, g_b_rg_a, g_w_rg_x, g_b_rg_x, g_lambda, g_q_norm, g_kv_norm,
                 jnp.concatenate([dg_mix0, dg_mix1]), jnp.concatenate([db_mix0, db_mix1]),
                 jnp.concatenate([dg_mlp0, dg_mlp1]), jnp.concatenate([db_mlp0, db_mlp1]))
    grad_w = dict(zip(rep_names, _allreduce_replicated(rep_local, "rep")))
    grad_w.update(meta_tokens=g_meta, ev_conv_w=g_conv_w, ev_w_uq=g_w_uq, ev_w_ukv=g_w_ukv)

    after, summed = grad_x, {}
    for nm, own, handle in pending:
        land = _exchange_wait(handle, False, after, "rs_wait_" + nm)
        summed[nm] = after = _sum_own_and_peers(own, land, "rs_sum_" + nm)
    grad_w.update(ev_w_in=summed["ev_in"][None], ev_w_out=summed["ev_out"][None], od_w_in=summed["od_in"][None],
                  od_w_out=summed["od_out"][None], mlp_w1=jnp.stack([summed["mlp0_w1"], summed["mlp1_w1"]]),
                  mlp_w2=jnp.stack([summed["mlp0_w2"], summed["mlp1_w2"]]))

    loss = lax.psum(loss_local, MESH_AXES)
    delta, new_m, new_v = {}, {}, {}
    for n in _WEIGHTS:
        w, g, m, v = weights[n], grad_w[n], args["m_" + n], args["v_" + n]
        two_d = (-1, w.shape[-1])
        d2, m2, v2 = _adamw(w.reshape(two_d), g.reshape(two_d), m.reshape(two_d), v.reshape(two_d), "adamw_" + n)
        delta[n], new_m[n], new_v[n] = d2.reshape(w.shape), m2.reshape(w.shape), v2.reshape(w.shape)
    return (loss, grad_x, *[grad_w[n] for n in _WEIGHTS], *[delta[n] for n in _WEIGHTS],
            *[new_m[n] for n in _WEIGHTS], *[new_v[n] for n in _WEIGHTS])
```

```python
import functools
import math

import jax
import jax.numpy as jnp
from jax import lax
from jax.experimental import pallas as pl
from jax.experimental.pallas import tpu as pltpu

F32 = jnp.float32
BF16 = jnp.bfloat16

N_DEV = 8
MESH_AXES = ("x", "y", "c")
LANES = 128
SEQ_BLOCK = 128

N_META = 16
LRU_C = 8.0
MLA_HEADS = 8
MLA_NOPE = 64
MLA_ROPE = 32
MLA_V = 64
RET_HEADS = 4
ROPE_BASE = 10000.0
DEPTH = 2
DN_ALPHA = (2 * DEPTH) ** 0.25
EPS = 1e-5
NEG_INF = -1e30

ADAM_LR = 0.001
ADAM_B1 = 0.9
ADAM_B2 = 0.999
ADAM_EPS = 1e-08
ADAM_WD = 0.01
ADAM_STEP = 10

VMEM_LIMIT = 56 * 1024 * 1024


def _params(*sem):
    return pltpu.CompilerParams(dimension_semantics=sem, vmem_limit_bytes=VMEM_LIMIT)


def _pick(n, cands):
    for c in cands:
        if n % c == 0:
            return c
    return n


def _row_tile(r, width):
    cands = (256, 128, 64, 32, 16, 8) if width <= 1024 else (128, 64, 32, 16, 8)
    return _pick(r, cands)


_DIMS = {"nn": (((1,), (0,)), ((), ())), "nt": (((1,), (1,)), ((), ())), "tn": (((0,), (0,)), ((), ()))}


def _dot(a, b, mode):
    return lax.dot_general(a.astype(BF16), b.astype(BF16), _DIMS[mode], preferred_element_type=F32)


def _matmul(a, b, mode, name, after=(), stacked=False, a_relu2=False, relu2_bwd_of=None):
    if stacked:
        n_blk = b.shape[2] if mode != "tn" else b.shape[1] // N_DEV
    if mode == "nn":
        (m, k), n = a.shape, (N_DEV * n_blk if stacked else b.shape[1])
    elif mode == "nt":
        (m, k), n = a.shape, (b.shape[1] if stacked else b.shape[0])
    else:
        (k, m), n = a.shape, b.shape[1]
    tm = _pick(m, (1088, 1024, 544, 512, 272, 256, 128, 64, 32, 16, 8))
    tn = _pick(n, (512, 256, 128))
    tk = _pick(k, (1088, 1024, 544, 512, 272, 256, 128))
    if stacked and mode in ("nn", "tn"):
        tn = n_blk
    kb = 2
    if stacked and mode == "nt":
        tk = kb * n_blk
    nk = k // tk

    out_spec = pl.BlockSpec((tm, tn), lambda i, j, kk: (i, j))
    out_shape = jax.ShapeDtypeStruct((m, n), F32)
    if mode == "nn":
        a_spec = pl.BlockSpec((tm, tk), lambda i, j, kk: (i, kk))
        b_spec = pl.BlockSpec((tk, tn), lambda i, j, kk: (kk, j))
        if stacked:
            b_spec = pl.BlockSpec((None, tk, tn), lambda i, j, kk: (j, kk, 0))
    elif mode == "nt":
        a_spec = pl.BlockSpec((tm, tk), lambda i, j, kk: (i, kk))
        b_spec = pl.BlockSpec((tn, tk), lambda i, j, kk: (j, kk))
        if stacked:
            b_spec = pl.BlockSpec((kb, tn, n_blk), lambda i, j, kk: (kk, j, 0))
    else:
        a_spec = pl.BlockSpec((tk, tm), lambda i, j, kk: (kk, i))
        b_spec = pl.BlockSpec((tk, tn), lambda i, j, kk: (kk, j))
        if stacked:
            out_spec = pl.BlockSpec((None, tm, tn), lambda i, j, kk: (j, i, 0))
            out_shape = jax.ShapeDtypeStruct((N_DEV, m, tn), F32)
    extra = [] if relu2_bwd_of is None else [relu2_bwd_of]
    extra_specs = [pl.BlockSpec((tm, tn), lambda i, j, kk: (i, j))] * len(extra)

    def body(a_ref, b_ref, *rest):
        o_ref = rest[-1]
        kk = pl.program_id(2)
        av = a_ref[...]
        if a_relu2:
            av = jnp.maximum(av, 0.0)
            av = av * av
        if stacked and mode == "nt":
            part = _dot(av[:, :n_blk], b_ref[0], mode)
            for q in range(1, kb):
                part = part + _dot(av[:, q * n_blk:(q + 1) * n_blk], b_ref[q], mode)
        else:
            part = _dot(av, b_ref[...], mode)

        @pl.when(kk == 0)
        def _():
            o_ref[...] = part

        @pl.when(kk != 0)
        def _():
            o_ref[...] += part

        if relu2_bwd_of is not None:
            @pl.when(kk == nk - 1)
            def _():
                o_ref[...] *= 2.0 * jnp.maximum(rest[0][...], 0.0)

    return pl.pallas_call(
        body,
        name=name,
        grid=(m // tm, n // tn, nk),
        in_specs=[a_spec, b_spec] + extra_specs + [pl.BlockSpec(memory_space=pl.ANY)] * len(after),
        out_specs=out_spec,
        out_shape=out_shape,
        compiler_params=_params("parallel", "parallel", "arbitrary"),
    )(a, b, *extra, *after)


def _group_matmul(a, w, mode, name):
    if mode in ("nn", "nt"):
        g, dk, dn = w.shape
        m = a.shape[0]
        d_in, d_out = (dk, dn) if mode == "nn" else (dn, dk)
        tm = _pick(m, (1088, 1024, 544, 512, 272, 256, 128, 64, 32, 16, 8))

        def body(a_ref, w_ref, o_ref):
            o_ref[...] = _dot(a_ref[...], w_ref[0], mode)

        return pl.pallas_call(
            body,
            name=name,
            grid=(g, m // tm),
            in_specs=[pl.BlockSpec((tm, d_in), lambda h, i: (i, h)), pl.BlockSpec((1, dk, dn), lambda h, i: (h, 0, 0))],
            out_specs=pl.BlockSpec((tm, d_out), lambda h, i: (i, h)),
            out_shape=jax.ShapeDtypeStruct((m, g * d_out), F32),
            compiler_params=_params("parallel", "parallel"),
        )(a, w)
    b = w
    m = a.shape[0]
    dk = dn = LANES
    g = a.shape[1] // dk
    tm = _pick(m, (1088, 1024, 544, 512, 272, 256, 128, 64, 32, 16, 8))

    def body(a_ref, b_ref, o_ref):
        part = _dot(a_ref[...], b_ref[...], "tn")

        @pl.when(pl.program_id(1) == 0)
        def _():
            o_ref[0] = part

        @pl.when(pl.program_id(1) != 0)
        def _():
            o_ref[0] += part

    return pl.pallas_call(
        body,
        name=name,
        grid=(g, m // tm),
        in_specs=[pl.BlockSpec((tm, dk), lambda h, i: (i, h)), pl.BlockSpec((tm, dn), lambda h, i: (i, h))],
        out_specs=pl.BlockSpec((1, dk, dn), lambda h, i: (h, 0, 0)),
        out_shape=jax.ShapeDtypeStruct((g, dk, dn), F32),
        compiler_params=_params("parallel", "arbitrary"),
    )(a, b)


def _make_group_linear(name):
    @jax.custom_vjp
    def op(x, w):
        return _group_matmul(x, w, "nn", name + "_fwd")

    def fwd(x, w):
        return op(x, w), (x, w)

    def bwd(res, dy):
        x, w = res
        return _group_matmul(dy, w, "nt", name + "_dx"), _group_matmul(x, dy, "tn", name + "_dw")

    op.defvjp(fwd, bwd)
    return op


def _my_place():
    return lax.axis_index("x"), lax.axis_index("y"), lax.axis_index("c")


def _all_gather(shard, name):
    shape, dtype = shard.shape, shard.dtype

    def body(x_ref, out_ref, send_sems, recv_sems, local_sem):
        x, y, c = _my_place()
        me, sibling = (x, y, c), (x, y, 1 - c)
        chips = [(1 - x, y), (x, 1 - y), (1 - x, 1 - y)]

        def slot(px, py, pc):
            return out_ref.at[4 * px + 2 * py + pc]

        def copy(k, block, to, src=None):
            return pltpu.make_async_remote_copy(
                src_ref=slot(*block) if src is None else src,
                dst_ref=slot(*block),
                send_sem=send_sems.at[k],
                recv_sem=recv_sems.at[k],
                device_id=to,
                device_id_type=pl.DeviceIdType.MESH,
            )

        mine = pltpu.make_async_copy(x_ref, slot(*me), local_sem)
        mine.start()
        first = [copy(0, me, sibling, src=x_ref)]
        first += [copy(1 + j, me, (*chip, c), src=x_ref) for j, chip in enumerate(chips)]
        for cp in first:
            cp.start()
        passed = [copy(4 + j, (*chip, c), sibling) for j, chip in enumerate(chips)]
        for j, chip in enumerate(chips):
            copy(1 + j, (*chip, c), me).wait_recv()
            passed[j].start()
        copy(0, sibling, me).wait_recv()
        for j, chip in enumerate(chips):
            copy(4 + j, (*chip, 1 - c), me).wait_recv()
        for cp in first + passed:
            cp.wait_send()
        mine.wait()

    return pl.pallas_call(
        body,
        name=name,
        out_shape=jax.ShapeDtypeStruct((N_DEV,) + shape, dtype),
        in_specs=[pl.BlockSpec(memory_space=pl.ANY)],
        out_specs=pl.BlockSpec(memory_space=pl.ANY),
        scratch_shapes=[pltpu.SemaphoreType.DMA((7,)), pltpu.SemaphoreType.DMA((7,)), pltpu.SemaphoreType.DMA],
    )(shard)


def _all_to_all(stacked, name):
    def body(x_ref, out_ref, send_sems, recv_sems, local_sem):
        x, y, c = _my_place()
        me = 4 * x + 2 * y + c
        mine = pltpu.make_async_copy(x_ref.at[me], out_ref.at[me], local_sem)
        mine.start()
        copies = []
        for k in range(1, N_DEV):
            px, py, pc = x ^ ((k >> 2) & 1), y ^ ((k >> 1) & 1), c ^ (k & 1)
            peer = 4 * px + 2 * py + pc
            copies.append(
                pltpu.make_async_remote_copy(
                    src_ref=x_ref.at[peer],
                    dst_ref=out_ref.at[me],
                    send_sem=send_sems.at[k - 1],
                    recv_sem=recv_sems.at[k - 1],
                    device_id=(px, py, pc),
                    device_id_type=pl.DeviceIdType.MESH,
                )
            )
        for cp in copies:
            cp.start()
        for cp in copies:
            cp.wait_recv()
        for cp in copies:
            cp.wait_send()
        mine.wait()

    return pl.pallas_call(
        body,
        name=name,
        out_shape=jax.ShapeDtypeStruct(stacked.shape, stacked.dtype),
        in_specs=[pl.BlockSpec(memory_space=pl.ANY)],
        out_specs=pl.BlockSpec(memory_space=pl.ANY),
        scratch_shapes=[pltpu.SemaphoreType.DMA((7,)), pltpu.SemaphoreType.DMA((7,)), pltpu.SemaphoreType.DMA],
    )(stacked)


def _sum_blocks(stacked, name):
    _, r, c = stacked.shape
    tr = _pick(r, (256, 128, 64, 32, 16, 8))

    def body(x_ref, o_ref):
        s = [x_ref[j] for j in range(N_DEV)]
        o_ref[...] = ((s[0] + s[1]) + (s[2] + s[3])) + ((s[4] + s[5]) + (s[6] + s[7]))

    return pl.pallas_call(
        body,
        name=name,
        grid=(r // tr,),
        in_specs=[pl.BlockSpec((N_DEV, tr, c), lambda i: (0, i, 0))],
        out_specs=pl.BlockSpec((tr, c), lambda i: (i, 0)),
        out_shape=jax.ShapeDtypeStruct((r, c), stacked.dtype),
        compiler_params=_params("parallel"),
    )(stacked)


def _stack_cols(full):
    k, n8 = full.shape
    return full.reshape(k, N_DEV, n8 // N_DEV).transpose(1, 0, 2)


def _unstack_cols(stacked):
    j, k, n = stacked.shape
    return stacked.transpose(1, 0, 2).reshape(k, j * n)


def _split_cols(p, cuts):
    bounds = (0,) + tuple(cuts) + (p.shape[1],)

    @jax.custom_vjp
    def op(z):
        return tuple(z[:, lo:hi] for lo, hi in zip(bounds[:-1], bounds[1:]))

    op.defvjp(lambda z: (op(z), None), lambda _, cots: (jnp.concatenate(cots, axis=1),))
    return op(p)


def _gather_weight(shard, cols, name):
    g = _all_gather(shard.astype(BF16), name)
    return _unstack_cols(g) if cols else g.reshape(-1, shard.shape[1])


def _scatter_grad(full, cols, name):
    if cols:
        st = _stack_cols(full)
    else:
        st = full.reshape(N_DEV, full.shape[0] // N_DEV, full.shape[1])
    return _sum_blocks(_all_to_all(st, name + "_a2a"), name + "_sum")


def _make_fsdp_linear(cols, name):
    @jax.custom_vjp
    def op(x, w_shard, w_full):
        return _matmul(x, w_full, "nn", name + "_fwd")

    def fwd(x, w_shard, w_full):
        return op(x, w_shard, w_full), (x, w_full)

    def bwd(res, dy):
        x, w = res
        dx = _matmul(dy, w, "nt", name + "_dx")
        dw = _matmul(x, dy, "tn", name + "_dw")
        return dx, _scatter_grad(dw, cols, name + "_rs"), jnp.zeros_like(w)

    op.defvjp(fwd, bwd)
    return op


def _make_fsdp_param(name):
    @jax.custom_vjp
    def op(shard, full):
        return full

    def fwd(shard, full):
        return full, None

    def bwd(_, g):
        return _scatter_grad(g, True, name + "_rs"), jnp.zeros_like(g)

    op.defvjp(fwd, bwd)
    return op


def _allreduce_replicated(gs, name):
    flat = jnp.concatenate([g.reshape(-1) for g in gs])
    n = flat.shape[0]
    rows = -(-n // (256 * LANES)) * 256
    packed = jnp.pad(flat, (0, rows * LANES - n)).reshape(rows, LANES)
    total = _sum_blocks(_all_gather(packed, name + "_ag"), name + "_sum").reshape(-1)
    out, off = [], 0
    for g in gs:
        out.append(total[off:off + g.size].reshape(g.shape))
        off += g.size
    return out


_HBM = pl.BlockSpec(memory_space=pltpu.HBM)
_SEM = pl.BlockSpec(memory_space=pltpu.SEMAPHORE)
_SIDE_EFFECT = pltpu.SideEffectType.DATAFLOW_SIDE_EFFECTING
_N_PEERS = N_DEV - 1


def _peer(k):
    x, y, c = _my_place()
    return x ^ ((k >> 2) & 1), y ^ ((k >> 1) & 1), c ^ (k & 1)


def _exchange_start(src, land_shape, gather, name, after=()):
    def body(src_ref, land_ref, *rest):
        send_sems, recv_sems, src_thru, land_thru, token = rest[len(after):]
        x, y, c = _my_place()
        me = 4 * x + 2 * y + c
        for k in range(1, N_DEV):
            px, py, pc = _peer(k)
            pltpu.make_async_remote_copy(
                src_ref=src_ref if gather else src_ref.at[4 * px + 2 * py + pc],
                dst_ref=land_ref.at[me] if gather else land_ref.at[k - 1],
                send_sem=send_sems.at[k - 1],
                recv_sem=recv_sems.at[k - 1],
                device_id=(px, py, pc),
                device_id_type=pl.DeviceIdType.MESH,
            ).start()
        token[...] = jnp.zeros_like(token)

    return pl.pallas_call(
        body,
        name=name,
        out_shape=(
            pltpu.SemaphoreType.DMA((_N_PEERS,)),
            pltpu.SemaphoreType.DMA((_N_PEERS,)),
            pltpu.HBM(src.shape, src.dtype),
            pltpu.HBM(land_shape, src.dtype),
            jax.ShapeDtypeStruct((8, LANES), F32),
        ),
        in_specs=(_HBM, _HBM) + (pl.BlockSpec(memory_space=pl.ANY),) * len(after),
        out_specs=(_SEM, _SEM, _HBM, _HBM, pl.BlockSpec(memory_space=pltpu.VMEM)),
        input_output_aliases={0: 2, 1: 3},
        compiler_params=pltpu.CompilerParams(has_side_effects=_SIDE_EFFECT),
    )(pltpu.with_memory_space_constraint(src, pltpu.HBM),
      pltpu.with_memory_space_constraint(lax.empty(land_shape, src.dtype), pltpu.HBM), *after)


def _exchange_wait(handle, gather, after, name):
    send_sems, recv_sems, src_thru, land_thru, _ = handle

    def body(src_ref, land_ref, send_sems, recv_sems, after_ref, src_dead, got_ref):
        for k in range(1, N_DEV):
            cp = pltpu.make_async_remote_copy(
                src_ref=src_ref if gather else src_ref.at[k],
                dst_ref=land_ref.at[k - 1],
                send_sem=send_sems.at[k - 1],
                recv_sem=recv_sems.at[k - 1],
                device_id=_peer(k),
                device_id_type=pl.DeviceIdType.MESH,
            )
            cp.wait_send()
            cp.wait_recv()

    return pl.pallas_call(
        body,
        name=name,
        out_shape=(pltpu.HBM(src_thru.shape, src_thru.dtype), pltpu.HBM(land_thru.shape, land_thru.dtype)),
        in_specs=(_HBM, _HBM, _SEM, _SEM, pl.BlockSpec(memory_space=pl.ANY)),
        out_specs=(_HBM, _HBM),
        input_output_aliases={0: 0, 1: 1},
        compiler_params=pltpu.CompilerParams(has_side_effects=_SIDE_EFFECT),
    )(src_thru, land_thru, send_sems, recv_sems, after)[1]


def _sum_own_and_peers(own, land, name):
    r, c = own.shape
    tr = _pick(r, (256, 128, 64, 32, 16, 8))

    def body(o_ref, l_ref, out_ref):
        s = [l_ref[j] for j in range(_N_PEERS)]
        out_ref[...] = ((o_ref[...] + s[0]) + (s[1] + s[2])) + ((s[3] + s[4]) + (s[5] + s[6]))

    return pl.pallas_call(
        body,
        name=name,
        grid=(r // tr,),
        in_specs=[pl.BlockSpec((tr, c), lambda i: (i, 0)), pl.BlockSpec((_N_PEERS, tr, c), lambda i: (0, i, 0))],
        out_specs=pl.BlockSpec((tr, c), lambda i: (i, 0)),
        out_shape=jax.ShapeDtypeStruct((r, c), own.dtype),
        compiler_params=_params("parallel"),
    )(own, land)


def _make_rowwise(f, name, n_rows, n_tabs, n_pars):
    n_in = n_rows + n_tabs + n_pars

    def specs(args, tm):
        blocked = [pl.BlockSpec((tm, a.shape[1]), lambda i: (i, 0)) for a in args[: n_rows + n_tabs]]
        whole = [pl.BlockSpec(a.shape, lambda i: (0, 0)) for a in args[n_rows + n_tabs:]]
        return blocked + whole

    def out_struct(args, tm):
        blk = [jax.ShapeDtypeStruct((tm, a.shape[1]), a.dtype) for a in args[: n_rows + n_tabs]]
        blk += [jax.ShapeDtypeStruct(a.shape, a.dtype) for a in args[n_rows + n_tabs:]]
        return jax.eval_shape(f, *blk)

    def fwd_call(*args):
        r = args[0].shape[0]
        tm = _row_tile(r, max(a.shape[1] for a in args[:n_rows]))
        ro, so = out_struct(args, tm)

        def body(*refs):
            vals = [x[...] for x in refs[:n_in]]
            outs = refs[n_in:]
            rv, sv = f(*vals)
            for o, v in zip(outs[: len(ro)], rv):
                o[...] = v
            for o, v in zip(outs[len(ro):], sv):
                @pl.when(pl.program_id(0) == 0)
                def _(o=o, v=v):
                    o[...] = v

                @pl.when(pl.program_id(0) != 0)
                def _(o=o, v=v):
                    o[...] += v

        out_shape = [jax.ShapeDtypeStruct((r, s.shape[1]), s.dtype) for s in ro]
        out_shape += [jax.ShapeDtypeStruct(s.shape, s.dtype) for s in so]
        out_specs = [pl.BlockSpec((tm, s.shape[1]), lambda i: (i, 0)) for s in ro]
        out_specs += [pl.BlockSpec(s.shape, lambda i: (0, 0)) for s in so]
        res = pl.pallas_call(
            body,
            name=name + "_fwd",
            grid=(r // tm,),
            in_specs=specs(args, tm),
            out_specs=out_specs,
            out_shape=out_shape,
            compiler_params=_params("arbitrary" if so else "parallel"),
        )(*args)
        return tuple(res[: len(ro)]), tuple(res[len(ro):])

    def bwd_call(args, cots, more=()):
        r = args[0].shape[0]
        tm = _row_tile(r, max(a.shape[1] for a in args[:n_rows]))
        ro, so = out_struct(args, tm)
        crow, csum = cots
        rows, tabs, pars = args[:n_rows], args[n_rows:n_rows + n_tabs], args[n_rows + n_tabs:]
        n_c = len(crow) + len(csum)

        def body(*refs):
            vals = [x[...] for x in refs[:n_in]]
            cv = [x[...] for x in refs[n_in:n_in + n_c]]
            for x in refs[n_in + n_c:n_in + n_c + len(more)]:
                cv[0] = cv[0] + x[...]
            outs = refs[n_in + n_c + len(more):]
            tv = vals[n_rows:n_rows + n_tabs]

            def g(*dargs):
                return f(*dargs[:n_rows], *tv, *dargs[n_rows:])

            _, vjp = jax.vjp(g, *vals[:n_rows], *vals[n_rows + n_tabs:])
            d = vjp((tuple(cv[: len(crow)]), tuple(cv[len(crow):])))
            for o, v in zip(outs[:n_rows], d[:n_rows]):
                o[...] = v
            for o, v in zip(outs[n_rows:], d[n_rows:]):
                @pl.when(pl.program_id(0) == 0)
                def _(o=o, v=v):
                    o[...] = v

                @pl.when(pl.program_id(0) != 0)
                def _(o=o, v=v):
                    o[...] += v

        in_specs = specs(args, tm)
        in_specs += [pl.BlockSpec((tm, c.shape[1]), lambda i: (i, 0)) for c in crow]
        in_specs += [pl.BlockSpec(c.shape, lambda i: (0, 0)) for c in csum]
        in_specs += [pl.BlockSpec((tm, c.shape[1]), lambda i: (i, 0)) for c in more]
        out_shape = [jax.ShapeDtypeStruct(a.shape, a.dtype) for a in rows + pars]
        out_specs = [pl.BlockSpec((tm, a.shape[1]), lambda i: (i, 0)) for a in rows]
        out_specs += [pl.BlockSpec(a.shape, lambda i: (0, 0)) for a in pars]
        res = pl.pallas_call(
            body,
            name=name + "_bwd",
            grid=(r // tm,),
            in_specs=in_specs,
            out_specs=out_specs,
            out_shape=out_shape,
            compiler_params=_params("arbitrary" if pars else "parallel"),
        )(*args, *crow, *csum, *more)
        return tuple(res[:n_rows]), tuple(res[n_rows:])

    @jax.custom_vjp
    def op(rows, tabs, pars):
        return fwd_call(*rows, *tabs, *pars)

    op.fwd_call, op.bwd_call = fwd_call, bwd_call

    def fwd(rows, tabs, pars):
        return fwd_call(*rows, *tabs, *pars), (rows, tabs, pars)

    def bwd(res, cots):
        rows, tabs, pars = res
        drows, dpars = bwd_call(tuple(rows) + tuple(tabs) + tuple(pars), cots)
        return drows, tuple(jnp.zeros_like(t) for t in tabs), dpars

    op.defvjp(fwd, bwd)
    return op


def _sigmoid(x):
    return 0.5 * (jnp.tanh(0.5 * x) + 1.0)


@jax.custom_jvp
def _softplus(x):
    e = jnp.exp(-jnp.abs(x))
    u = 1.0 + e
    log1p_e = jnp.where(u == 1.0, e, e * jnp.log(u) / jnp.where(u == 1.0, 1.0, u - 1.0))
    return jnp.maximum(x, 0.0) + log1p_e


@_softplus.defjvp
def _softplus_jvp(primals, tangents):
    (x,), (t,) = primals, tangents
    return _softplus(x), t * _sigmoid(x)


def _gelu(x):
    return 0.5 * x * (1.0 + jnp.tanh(math.sqrt(2.0 / math.pi) * (x + 0.044715 * (x * x * x))))


def _ln_res_f(h, mix, g, b):
    z = DN_ALPHA * h + mix
    mu = jnp.mean(z, axis=-1, keepdims=True)
    zc = z - mu
    var = jnp.mean(zc * zc, axis=-1, keepdims=True)
    return (zc * lax.rsqrt(var + EPS) * g + b,), ()


def _rmsnorm_f(x, g):
    return (x * lax.rsqrt(jnp.mean(x * x, axis=-1, keepdims=True) + EPS) * g,), ()


def _lru_gates_f(ga, gx, xc, b_a, b_x, lam):
    r = _sigmoid(ga + b_a)
    i = _sigmoid(gx + b_x)
    log_a = -LRU_C * r * _softplus(-lam)
    a = jnp.exp(log_a)
    one_minus_a2 = jnp.tanh(-log_a) * (jnp.exp(2.0 * log_a) + 1.0)
    return (a, jnp.sqrt(one_minus_a2) * (i * xc)), ()


def _lru_out_f(hh, p_gate):
    return (hh * _gelu(p_gate),), ()


def _rope_pair_f(x1, x2, cos, sin):
    return (x1 * cos - x2 * sin, x1 * sin + x2 * cos), ()


def _rope_ret_f(q, k, cos2, sin2):
    d = cos2.shape[1]
    half = d // 2
    k_scale = d ** -0.5

    def rope(x):
        outs = []
        for h in range(x.shape[1] // d):
            xh = x[:, h * d:(h + 1) * d]
            rot = jnp.concatenate([xh[:, half:], xh[:, :half]], axis=1)
            outs.append(xh * cos2 + rot * sin2)
        return jnp.concatenate(outs, axis=1)

    return (rope(q), rope(k) * k_scale), ()


def _ret_out_f(o, g):
    d = o.shape[1] // RET_HEADS
    outs = []
    for h in range(RET_HEADS):
        oh = o[:, h * d:(h + 1) * d]
        outs.append(oh * lax.rsqrt(jnp.mean(oh * oh, axis=-1, keepdims=True) + EPS))
    y = jnp.concatenate(outs, axis=1)
    return (g * _sigmoid(g) * y,), ()


def _loss_f(y, t, mask):
    e = (y - t) * mask
    per_row = jnp.sum(e * e, axis=-1, keepdims=True) * (0.5 / y.shape[1])
    total = jnp.sum(per_row, axis=0, keepdims=True)
    return (), (jnp.broadcast_to(total, (1, LANES)),)


def _shift_down(x, s):
    if s == 0:
        return x
    t = x.shape[0]
    row = lax.broadcasted_iota(jnp.int32, x.shape, 0)
    return jnp.where(row >= s, pltpu.roll(x, s, 0), 0.0)


def _shift_up(x, s):
    if s == 0:
        return x
    t = x.shape[0]
    row = lax.broadcasted_iota(jnp.int32, x.shape, 0)
    return jnp.where(row < t - s, pltpu.roll(x, t - s, 0), 0.0)


def _conv_fwd(x, w, b, name):
    bsz, t, c = x.shape
    width = w.shape[0]

    def body(x_ref, w_ref, b_ref, y_ref):
        xv = x_ref[0]
        acc = jnp.broadcast_to(b_ref[...], xv.shape)
        for k in range(width):
            acc = acc + w_ref[k:k + 1, :] * _shift_down(xv, width - 1 - k)
        y_ref[0] = acc

    return pl.pallas_call(
        body,
        name=name,
        grid=(bsz, c // LANES),
        in_specs=[
            pl.BlockSpec((1, t, LANES), lambda i, j: (i, 0, j)),
            pl.BlockSpec((width, LANES), lambda i, j: (0, j)),
            pl.BlockSpec((1, LANES), lambda i, j: (0, j)),
        ],
        out_specs=pl.BlockSpec((1, t, LANES), lambda i, j: (i, 0, j)),
        out_shape=jax.ShapeDtypeStruct(x.shape, F32),
        compiler_params=_params("parallel", "parallel"),
    )(x, w, b)


def _conv_bwd(x, w, dy, name):
    bsz, t, c = x.shape
    width = w.shape[0]

    def body(x_ref, w_ref, dy_ref, dx_ref, dw_ref, db_ref):
        xv, g = x_ref[0], dy_ref[0]
        dx = jnp.zeros_like(xv)
        dws = []
        for k in range(width):
            s = width - 1 - k
            dx = dx + w_ref[k:k + 1, :] * _shift_up(g, s)
            dws.append(jnp.sum(g * _shift_down(xv, s), axis=0, keepdims=True))
        dx_ref[0] = dx
        dw = jnp.concatenate(dws, axis=0)
        db = jnp.sum(g, axis=0, keepdims=True)

        @pl.when(pl.program_id(1) == 0)
        def _():
            dw_ref[...] = dw
            db_ref[...] = db

        @pl.when(pl.program_id(1) != 0)
        def _():
            dw_ref[...] += dw
            db_ref[...] += db

    return pl.pallas_call(
        body,
        name=name,
        grid=(c // LANES, bsz),
        in_specs=[
            pl.BlockSpec((1, t, LANES), lambda j, i: (i, 0, j)),
            pl.BlockSpec((width, LANES), lambda j, i: (0, j)),
            pl.BlockSpec((1, t, LANES), lambda j, i: (i, 0, j)),
        ],
        out_specs=[
            pl.BlockSpec((1, t, LANES), lambda j, i: (i, 0, j)),
            pl.BlockSpec((width, LANES), lambda j, i: (0, j)),
            pl.BlockSpec((1, LANES), lambda j, i: (0, j)),
        ],
        out_shape=[
            jax.ShapeDtypeStruct(x.shape, F32),
            jax.ShapeDtypeStruct(w.shape, F32),
            jax.ShapeDtypeStruct((1, c), F32),
        ],
        compiler_params=_params("parallel", "arbitrary"),
    )(x, w, dy)


def _make_conv(name):
    @jax.custom_vjp
    def op(x, w, b):
        return _conv_fwd(x, w, b, name + "_fwd")

    def fwd(x, w, b):
        return op(x, w, b), (x, w)

    def bwd(res, dy):
        x, w = res
        return tuple(_conv_bwd(x, w, dy, name + "_bwd"))

    op.defvjp(fwd, bwd)
    return op


def _scan_fwd(a, b, name):
    bsz, t, c = a.shape
    cw = _pick(c, (4 * LANES, 2 * LANES, LANES))

    def body(a_ref, b_ref, h_ref):
        def step(i, h):
            h = a_ref[0, pl.ds(i, 1), :] * h + b_ref[0, pl.ds(i, 1), :]
            h_ref[0, pl.ds(i, 1), :] = h
            return h

        lax.fori_loop(0, t, step, jnp.zeros((1, cw), F32), unroll=8)

    spec = pl.BlockSpec((1, t, cw), lambda i, j: (i, 0, j))
    return pl.pallas_call(
        body,
        name=name,
        grid=(bsz, c // cw),
        in_specs=[spec, spec],
        out_specs=spec,
        out_shape=jax.ShapeDtypeStruct(a.shape, F32),
        compiler_params=_params("parallel", "parallel"),
    )(a, b)


def _scan_bwd(a, h, g, name):
    bsz, t, c = a.shape
    cw = _pick(c, (2 * LANES, LANES))

    def body(a_ref, h_ref, g_ref, da_ref, db_ref):
        def step(n, carry):
            i = t - 1 - n
            lam = g_ref[0, pl.ds(i, 1), :] + carry
            db_ref[0, pl.ds(i, 1), :] = lam
            prev = jnp.where(i > 0, h_ref[0, pl.ds(jnp.maximum(i - 1, 0), 1), :], 0.0)
            da_ref[0, pl.ds(i, 1), :] = lam * prev
            return a_ref[0, pl.ds(i, 1), :] * lam

        lax.fori_loop(0, t, step, jnp.zeros((1, cw), F32), unroll=8)

    spec = pl.BlockSpec((1, t, cw), lambda i, j: (i, 0, j))
    return pl.pallas_call(
        body,
        name=name,
        grid=(bsz, c // cw),
        in_specs=[spec, spec, spec],
        out_specs=[spec, spec],
        out_shape=[jax.ShapeDtypeStruct(a.shape, F32)] * 2,
        compiler_params=_params("parallel", "parallel"),
    )(a, h, g)


def _make_scan(name):
    @jax.custom_vjp
    def op(a, b):
        return _scan_fwd(a, b, name + "_fwd")

    def fwd(a, b):
        h = op(a, b)
        return h, (a, h)

    def bwd(res, g):
        a, h = res
        da, db = _scan_bwd(a, h, g, name + "_bwd")
        return da, db

    op.defvjp(fwd, bwd)
    return op


def _query_blocks(t):
    blocks, start = [], 0
    while start < t:
        rows = 2 * SEQ_BLOCK if start + 2 * SEQ_BLOCK <= t else SEQ_BLOCK
        blocks.append((start, rows))
        start += rows
    return blocks


def _attn_probs(q, k, start, scale):
    tq, tk = q.shape[0], k.shape[0]
    s = _dot(q, k, "nt") * scale
    qpos = start + lax.broadcasted_iota(jnp.int32, (tq, tk), 0)
    kpos = lax.broadcasted_iota(jnp.int32, (tq, tk), 1)
    s = jnp.where(kpos <= qpos, s, NEG_INF)
    e = jnp.exp(s - jnp.max(s, axis=-1, keepdims=True))
    return e / jnp.sum(e, axis=-1, keepdims=True)


def _attn_fwd(q, k, v, name):
    g, t, dq = q.shape
    dv = v.shape[2]
    scale = dq ** -0.5

    def body(q_ref, k_ref, v_ref, o_ref):
        for start, rows in _query_blocks(t):
            n = start + rows
            p = _attn_probs(q_ref[0, start:n, :], k_ref[0, :n, :], start, scale)
            o_ref[0, start:n, :] = _dot(p, v_ref[0, :n, :], "nn")

    return pl.pallas_call(
        body,
        name=name,
        grid=(g,),
        in_specs=[
            pl.BlockSpec((1, t, dq), lambda b: (b, 0, 0)),
            pl.BlockSpec((1, t, dq), lambda b: (b, 0, 0)),
            pl.BlockSpec((1, t, dv), lambda b: (b, 0, 0)),
        ],
        out_specs=pl.BlockSpec((1, t, dv), lambda b: (b, 0, 0)),
        out_shape=jax.ShapeDtypeStruct((g, t, dv), F32),
        compiler_params=_params("parallel"),
    )(q, k, v)


def _attn_bwd(q, k, v, do, name):
    g, t, dq = q.shape
    dv = v.shape[2]
    scale = dq ** -0.5

    def body(q_ref, k_ref, v_ref, do_ref, dq_ref, dk_ref, dv_ref):
        for start, rows in reversed(_query_blocks(t)):
            n = start + rows
            qb, dob = q_ref[0, start:n, :], do_ref[0, start:n, :]
            kk, vv = k_ref[0, :n, :], v_ref[0, :n, :]
            p = _attn_probs(qb, kk, start, scale)
            dp = _dot(dob, vv, "nt")
            ds = p * (dp - jnp.sum(dp * p, axis=-1, keepdims=True)) * scale
            dq_ref[0, start:n, :] = _dot(ds, kk, "nn")
            if n == t:
                dk_ref[0] = _dot(ds, qb, "tn")
                dv_ref[0] = _dot(p, dob, "tn")
            else:
                dk_ref[0, :n, :] += _dot(ds, qb, "tn")
                dv_ref[0, :n, :] += _dot(p, dob, "tn")

    qk_spec = pl.BlockSpec((1, t, dq), lambda b: (b, 0, 0))
    v_spec = pl.BlockSpec((1, t, dv), lambda b: (b, 0, 0))
    return pl.pallas_call(
        body,
        name=name,
        grid=(g,),
        in_specs=[qk_spec, qk_spec, v_spec, v_spec],
        out_specs=[qk_spec, qk_spec, v_spec],
        out_shape=[
            jax.ShapeDtypeStruct(q.shape, F32),
            jax.ShapeDtypeStruct(k.shape, F32),
            jax.ShapeDtypeStruct(v.shape, F32),
        ],
        compiler_params=_params("parallel"),
    )(q, k, v, do)


def _make_attention(name):
    @jax.custom_vjp
    def op(q, k, v):
        return _attn_fwd(q, k, v, name + "_fwd")

    def fwd(q, k, v):
        return op(q, k, v), (q, k, v)

    def bwd(res, do):
        return tuple(_attn_bwd(*res, do, name + "_bwd"))

    op.defvjp(fwd, bwd)
    return op


RET_KEY_CHUNK = 512


def _key_chunks(t):
    return [(c, min(RET_KEY_CHUNK, t - c)) for c in range(0, t, RET_KEY_CHUNK)]


def _decay(q0, k0, tq, tk, log_gamma):
    row = lax.broadcasted_iota(jnp.int32, (tq, 1), 0)
    col = (q0 - k0) - lax.broadcasted_iota(jnp.int32, (1, tk), 1)
    outer = jnp.exp(log_gamma * row.astype(F32)) * jnp.exp(log_gamma * col.astype(F32))
    return jnp.where(row + col >= 0, outer, 0.0)


def _ret_query_rows(t):
    return t // 4 if t % 32 == 0 else SEQ_BLOCK


def _ret_specs(t, dk, dv, heads):
    tq = _ret_query_rows(t)
    return (
        pl.BlockSpec(memory_space=pltpu.SMEM),
        pl.BlockSpec((1, tq, dk), lambda b, h, i: (b, i, h)),
        pl.BlockSpec((1, t, dk), lambda b, h, i: (b, 0, h)),
        pl.BlockSpec((1, t, dv), lambda b, h, i: (b, 0, h)),
        pl.BlockSpec((1, tq, dv), lambda b, h, i: (b, i, h)),
    )


def _ret_fwd(lg, q, k, v, name):
    bsz, t, hdk = q.shape
    heads = lg.shape[0]
    dk, dv = hdk // heads, v.shape[2] // heads
    lg_spec, q_spec, k_spec, v_spec, o_spec = _ret_specs(t, dk, dv, heads)
    tq = _ret_query_rows(t)

    def body(lg_ref, q_ref, k_ref, v_ref, o_ref):
        q0 = pl.program_id(2) * tq
        lgh = lg_ref[pl.program_id(1)]
        o_ref[0] = jnp.zeros((tq, dv), F32)
        for c0, cw in _key_chunks(t):
            @pl.when(c0 < q0 + tq)
            def _(c0=c0, cw=cw):
                d = _decay(q0, c0, tq, cw, lgh)
                a = _dot(q_ref[0], k_ref[0, c0:c0 + cw, :], "nt") * d
                o_ref[0] += _dot(a, v_ref[0, c0:c0 + cw, :], "nn")

    return pl.pallas_call(
        body,
        name=name,
        grid=(bsz, heads, t // tq),
        in_specs=[lg_spec, q_spec, k_spec, v_spec],
        out_specs=o_spec,
        out_shape=jax.ShapeDtypeStruct(v.shape, F32),
        compiler_params=_params("parallel", "parallel", "parallel"),
    )(lg, q, k, v)


def _ret_bwd(lg, q, k, v, do, name):
    bsz, t, hdk = q.shape
    heads = lg.shape[0]
    dk, dv = hdk // heads, v.shape[2] // heads
    lg_spec, q_spec, k_spec, v_spec, o_spec = _ret_specs(t, dk, dv, heads)
    tq = _ret_query_rows(t)

    def body(lg_ref, q_ref, k_ref, v_ref, do_ref, dq_ref, dk_ref, dv_ref):
        q0 = pl.program_id(2) * tq
        lgh = lg_ref[pl.program_id(1)]

        @pl.when(pl.program_id(2) == 0)
        def _():
            dk_ref[0] = jnp.zeros((t, dk), F32)
            dv_ref[0] = jnp.zeros((t, dv), F32)

        dq_ref[0] = jnp.zeros((tq, dk), F32)
        for c0, cw in _key_chunks(t):
            @pl.when(c0 < q0 + tq)
            def _(c0=c0, cw=cw):
                qb, dob = q_ref[0], do_ref[0]
                kk, vv = k_ref[0, c0:c0 + cw, :], v_ref[0, c0:c0 + cw, :]
                d = _decay(q0, c0, tq, cw, lgh)
                a = _dot(qb, kk, "nt") * d
                ds = _dot(dob, vv, "nt") * d
                dq_ref[0] += _dot(ds, kk, "nn")
                dk_ref[0, c0:c0 + cw, :] += _dot(ds, qb, "tn")
                dv_ref[0, c0:c0 + cw, :] += _dot(a, dob, "tn")

    return pl.pallas_call(
        body,
        name=name,
        grid=(bsz, heads, t // tq),
        in_specs=[lg_spec, q_spec, k_spec, v_spec, o_spec],
        out_specs=[q_spec, k_spec, v_spec],
        out_shape=[
            jax.ShapeDtypeStruct(q.shape, F32),
            jax.ShapeDtypeStruct(k.shape, F32),
            jax.ShapeDtypeStruct(v.shape, F32),
        ],
        compiler_params=_params("parallel", "parallel", "arbitrary"),
    )(lg, q, k, v, do)


def _make_retention(name):
    @jax.custom_vjp
    def op(lg, q, k, v):
        return _ret_fwd(lg, q, k, v, name + "_fwd")

    def fwd(lg, q, k, v):
        return op(lg, q, k, v), (lg, q, k, v)

    def bwd(res, do):
        lg = res[0]
        return (jnp.zeros_like(lg),) + tuple(_ret_bwd(*res, do, name + "_bwd"))

    op.defvjp(fwd, bwd)
    return op


def _adamw(w, g, m, v, name):
    r, c = w.shape
    tr = _pick(r, (256, 128, 64, 32, 16, 8))

    def body(w_ref, g_ref, m_ref, v_ref, d_ref, nm_ref, nv_ref):
        gv = g_ref[...]
        nm = ADAM_B1 * m_ref[...] + (1.0 - ADAM_B1) * gv
        nv = ADAM_B2 * v_ref[...] + (1.0 - ADAM_B2) * (gv * gv)
        m_hat = nm / (1.0 - ADAM_B1 ** ADAM_STEP)
        v_hat = nv / (1.0 - ADAM_B2 ** ADAM_STEP)
        d_ref[...] = -ADAM_LR * (m_hat / (jnp.sqrt(v_hat) + ADAM_EPS) + ADAM_WD * w_ref[...])
        nm_ref[...] = nm
        nv_ref[...] = nv

    spec = pl.BlockSpec((tr, c), lambda i: (i, 0))
    return pl.pallas_call(
        body,
        name=name,
        grid=(r // tr,),
        in_specs=[spec] * 4,
        out_specs=[spec] * 3,
        out_shape=[jax.ShapeDtypeStruct((r, c), F32)] * 3,
        compiler_params=_params("parallel"),
    )(w, g, m, v)


def _rope_tables(t, half, reps):
    inv = ROPE_BASE ** (-jnp.arange(half, dtype=F32) / half)
    ang = jnp.arange(t, dtype=jnp.int32).astype(F32)[:, None] * inv[None, :]
    return jnp.tile(jnp.cos(ang), (1, reps)), jnp.tile(jnp.sin(ang), (1, reps))


def _padded_len(seq):
    return -(-(N_META + seq) // SEQ_BLOCK) * SEQ_BLOCK


def _embed(meta_shard, x, meta_full):
    bsz, seq, d = x.shape
    t = _padded_len(seq)
    meta = _make_fsdp_param("meta")(meta_shard, meta_full)
    return jnp.concatenate(
        [jnp.broadcast_to(meta[None], (bsz, N_META, d)), x, jnp.zeros((bsz, t - N_META - seq, d), F32)], axis=1
    ).reshape(bsz * t, d)


def _even_mixer(p, conv_w_shard, conv_b, w_rg_a, b_rg_a, w_rg_x, b_rg_x, lru_lambda, q_norm_g, w_uq_shard,
                kv_norm_g, w_ukv_shard, gathered, bsz):
    conv_w_full, w_uq_full, w_ukv_full = gathered
    r = p.shape[0]
    t = r // bsz

    def tile_rows(tab):
        return jnp.tile(tab, (bsz, 1))

    lru_w = w_rg_a.shape[2] * w_rg_a.shape[1]
    q_rank, kv_rank = q_norm_g.shape[1], kv_norm_g.shape[1]
    p_gate, p_rec, p_q, p_kv, p_kpe = _split_cols(
        p, (lru_w, 2 * lru_w, 2 * lru_w + q_rank, 2 * lru_w + q_rank + kv_rank))

    conv_w = _make_fsdp_param("conv_w")(conv_w_shard[0], conv_w_full)
    xc = _make_conv("conv")(p_rec.reshape(bsz, t, lru_w), conv_w, conv_b).reshape(r, lru_w)
    ga = _make_group_linear("rg_a")(xc, w_rg_a[0])
    gx = _make_group_linear("rg_x")(xc, w_rg_x[0])
    (a, bb), _ = _make_rowwise(_lru_gates_f, "lru_gates", 3, 0, 3)((ga, gx, xc), (), (b_rg_a, b_rg_x, lru_lambda))
    hh = _make_scan("lru_scan")(a.reshape(bsz, t, lru_w), bb.reshape(bsz, t, lru_w)).reshape(r, lru_w)
    (y_rec,), _ = _make_rowwise(_lru_out_f, "lru_out", 2, 0, 0)((hh, p_gate), (), ())

    (qn,), _ = _make_rowwise(_rmsnorm_f, "q_norm", 1, 0, 1)((p_q,), (), (q_norm_g,))
    (kvn,), _ = _make_rowwise(_rmsnorm_f, "kv_norm", 1, 0, 1)((p_kv,), (), (kv_norm_g,))
    q = _make_fsdp_linear(True, "ev_uq")(qn, w_uq_shard[0], w_uq_full).reshape(r, MLA_HEADS, MLA_NOPE + MLA_ROPE)
    kv = _make_fsdp_linear(True, "ev_ukv")(kvn, w_ukv_shard[0], w_ukv_full).reshape(r, MLA_HEADS, MLA_NOPE + MLA_V)
    half = MLA_ROPE // 2
    cos_h, sin_h = _rope_tables(t, half, MLA_HEADS)
    q1 = q[:, :, MLA_NOPE:MLA_NOPE + half].reshape(r, MLA_HEADS * half)
    q2 = q[:, :, MLA_NOPE + half:].reshape(r, MLA_HEADS * half)
    (q1, q2), _ = _make_rowwise(_rope_pair_f, "rope_q", 2, 2, 0)((q1, q2), (tile_rows(cos_h), tile_rows(sin_h)), ())
    (k1, k2), _ = _make_rowwise(_rope_pair_f, "rope_k", 2, 2, 0)(
        (p_kpe[:, :half], p_kpe[:, half:]), (tile_rows(cos_h[:, :half]), tile_rows(sin_h[:, :half])), ())
    q_full = jnp.concatenate(
        [q[:, :, :MLA_NOPE], q1.reshape(r, MLA_HEADS, half), q2.reshape(r, MLA_HEADS, half)], axis=2)
    k_pe = jnp.broadcast_to(jnp.concatenate([k1, k2], axis=1)[:, None, :], (r, MLA_HEADS, MLA_ROPE))
    k_full = jnp.concatenate([kv[:, :, :MLA_NOPE], k_pe], axis=2)

    def heads_major(z):
        return z.reshape(bsz, t, MLA_HEADS, z.shape[-1]).transpose(0, 2, 1, 3).reshape(bsz * MLA_HEADS, t, z.shape[-1])

    o = _make_attention("mla")(heads_major(q_full), heads_major(k_full), heads_major(kv[:, :, MLA_NOPE:]))
    y_att = o.reshape(bsz, MLA_HEADS, t, MLA_V).transpose(0, 2, 1, 3).reshape(r, MLA_HEADS * MLA_V)
    return jnp.concatenate([y_rec, y_att], axis=1)


def _odd_mixer(p, bsz):
    r = p.shape[0]
    t = r // bsz

    def tile_rows(tab):
        return jnp.tile(tab, (bsz, 1))

    qk = p.shape[1] // 6
    dk = qk // RET_HEADS
    cos2, sin2 = _rope_tables(t, dk // 2, 2)
    sin2 = jnp.concatenate([-sin2[:, :dk // 2], sin2[:, dk // 2:]], axis=1)
    p_q, p_k, p_v, p_g = _split_cols(p, (qk, 2 * qk, 4 * qk))
    (rq, rk), _ = _make_rowwise(_rope_ret_f, "rope_ret", 2, 2, 0)((p_q, p_k), (tile_rows(cos2), tile_rows(sin2)), ())
    lg = jnp.log(1.0 - 2.0 ** (-5.0 - jnp.arange(RET_HEADS, dtype=F32)))
    o = _make_retention("ret")(lg, rq.reshape(bsz, t, qk), rk.reshape(bsz, t, qk), p_v.reshape(bsz, t, 2 * qk))
    (y,), _ = _make_rowwise(_ret_out_f, "ret_out", 2, 0, 0)((o.reshape(r, 2 * qk), p_g), (), ())
    return y


def _local_loss(h, target):
    bsz, seq, d = target.shape
    t = _padded_len(seq)
    t_real = N_META + seq
    pos = jnp.arange(t, dtype=jnp.int32)
    mask = jnp.tile(((pos >= N_META) & (pos < t_real)).astype(F32)[:, None], (bsz, 1))
    tgt = jnp.concatenate(
        [jnp.zeros((bsz, N_META, d), F32), target, jnp.zeros((bsz, t - t_real, d), F32)], axis=1).reshape(bsz * t, d)
    _, (total,) = _make_rowwise(_loss_f, "loss", 1, 2, 0)((h,), (tgt, mask), ())
    return total[0, 0]


_WEIGHTS = ("meta_tokens", "ev_w_in", "ev_conv_w", "ev_conv_b", "ev_w_rg_a", "ev_b_rg_a", "ev_w_rg_x", "ev_b_rg_x",
            "ev_lru_lambda", "ev_q_norm_g", "ev_w_uq", "ev_kv_norm_g", "ev_w_ukv", "ev_w_out", "od_w_in", "od_w_out",
            "ln_mix_g", "ln_mix_b", "mlp_w1", "mlp_w2", "ln_mlp_g", "ln_mlp_b")


def kernel(x, meta_tokens, ev_w_in, ev_conv_w, ev_conv_b, ev_w_rg_a, ev_b_rg_a, ev_w_rg_x, ev_b_rg_x, ev_lru_lambda, ev_q_norm_g, ev_w_uq, ev_kv_norm_g, ev_w_ukv, ev_w_out, od_w_in, od_w_out, ln_mix_g, ln_mix_b, mlp_w1, mlp_w2, ln_mlp_g, ln_mlp_b, loss_target, m_meta_tokens, m_ev_w_in, m_ev_conv_w, m_ev_conv_b, m_ev_w_rg_a, m_ev_b_rg_a, m_ev_w_rg_x, m_ev_b_rg_x, m_ev_lru_lambda, m_ev_q_norm_g, m_ev_w_uq, m_ev_kv_norm_g, m_ev_w_ukv, m_ev_w_out, m_od_w_in, m_od_w_out, m_ln_mix_g, m_ln_mix_b, m_mlp_w1, m_mlp_w2, m_ln_mlp_g, m_ln_mlp_b, v_meta_tokens, v_ev_w_in, v_ev_conv_w, v_ev_conv_b, v_ev_w_rg_a, v_ev_b_rg_a, v_ev_w_rg_x, v_ev_b_rg_x, v_ev_lru_lambda, v_ev_q_norm_g, v_ev_w_uq, v_ev_kv_norm_g, v_ev_w_ukv, v_ev_w_out, v_od_w_in, v_od_w_out, v_ln_mix_g, v_ln_mix_b, v_mlp_w1, v_mlp_w2, v_ln_mlp_g, v_ln_mlp_b):
    args = locals()
    weights = {n: args[n] for n in _WEIGHTS}
    bsz = x.shape[0]
    my_x, my_y, my_c = _my_place()
    me = 4 * my_x + 2 * my_y + my_c

    big = (("ev_in", ev_w_in[0], True), ("ev_out", ev_w_out[0], False), ("mlp0_w1", mlp_w1[0], True),
           ("mlp0_w2", mlp_w2[0], False), ("od_in", od_w_in[0], True), ("od_out", od_w_out[0], False),
           ("mlp1_w1", mlp_w1[1], True), ("mlp1_w2", mlp_w2[1], False))
    small_sharded = (("meta", meta_tokens, F32), ("conv_w", ev_conv_w[0], F32), ("ev_uq", ev_w_uq[0], BF16),
                     ("ev_ukv", ev_w_ukv[0], BF16))
    gathers, prev = {}, ()
    for nm, shard16, cols in (tuple((nm, s.astype(dt), True) for nm, s, dt in small_sharded)
                              + tuple((nm, s.astype(BF16), cols) for nm, s, cols in big)):
        handle = _exchange_start(shard16, (N_DEV,) + shard16.shape, True, "ag_start_" + nm, after=prev)
        gathers[nm], prev = (shard16, cols, handle), (handle[4],)
    gather_tokens = tuple(g[2][4] for g in gathers.values())

    def full_weight(nm, after):
        shard16, cols, handle = gathers[nm]
        land = _exchange_wait(handle, True, after, "ag_wait_" + nm)
        land = lax.dynamic_update_index_in_dim(land, shard16, me, 0)
        if cols and shard16.shape[1] % LANES == 0:
            return land, True
        return (_unstack_cols(land) if cols else land.reshape(-1, shard16.shape[1])), False

    meta_full, conv_w_full, w_uq_full, w_ukv_full = (
        _unstack_cols(lax.dynamic_update_index_in_dim(
            _exchange_wait(gathers[nm][2], True, gather_tokens[-1], "ag_wait_" + nm), gathers[nm][0], me, 0))
        for nm, _, _ in small_sharded)

    pending = []

    def linear_bwd(nm, x_in, w, dy, cols, **fused):
        w_full, w_stacked = w
        a_relu2 = fused.pop("a_relu2", False)
        if w_stacked:
            stacked = _matmul(x_in, dy, "tn", nm + "_dw", stacked=True, a_relu2=a_relu2)
            own = lax.dynamic_index_in_dim(stacked, me, 0, keepdims=False)
        else:
            dw = _matmul(x_in, dy, "tn", nm + "_dw", a_relu2=a_relu2)
            n = dw.shape[1] // N_DEV
            if cols:
                stacked = _stack_cols(dw)
                own = lax.dynamic_slice_in_dim(dw, me * n, n, axis=1)
            else:
                stacked = dw.reshape(N_DEV, dw.shape[0] // N_DEV, dw.shape[1])
                own = lax.dynamic_index_in_dim(stacked, me, 0, keepdims=False)
        handle = _exchange_start(stacked, (_N_PEERS,) + stacked.shape[1:], False, "rs_start_" + nm)
        pending.append((nm, own, handle))
        return _matmul(dy, w_full, "nt", nm + "_dx", after=(handle[4],), stacked=w_stacked, **fused)

    def linear_fwd(nm, x_in, w, **fused):
        return _matmul(x_in, w[0], "nn", nm + "_fwd", stacked=w[1], **fused)

    def mlp_fwd(h, l):
        w1 = full_weight(f"mlp{l}_w1", h)
        u = linear_fwd(f"mlp{l}_w1", h, w1)
        w2 = full_weight(f"mlp{l}_w2", u)
        f = linear_fwd(f"mlp{l}_w2", u, w2, a_relu2=True)
        ln_args = (h, f, ln_mlp_g[l:l + 1], ln_mlp_b[l:l + 1])
        return ln_fwd(f"mlp{l}_ln", *ln_args), (h, w1, u, w2, ln_args)

    def mlp_bwd(dout, res, l):
        h, w1, u, w2, ln_args = res
        dh, df, dg, db = ln_bwd(f"mlp{l}_ln", ln_args, dout)
        du = linear_bwd(f"mlp{l}_w2", u, w2, df, False, a_relu2=True, relu2_bwd_of=u)
        return (dh, linear_bwd(f"mlp{l}_w1", h, w1, du, True)), dg, db

    def ln_fwd(nm, h, mix, g, b):
        return _make_rowwise(_ln_res_f, nm, 2, 0, 2).fwd_call(h, mix, g, b)[0][0]

    def ln_bwd(nm, ln_args, pieces):
        (dh, dmix), (dg, db) = _make_rowwise(_ln_res_f, nm, 2, 0, 2).bwd_call(
            ln_args, ((pieces[0],), ()), more=tuple(pieces[1:]))
        return dh, dmix, dg, db

    h0, vjp_embed = jax.vjp(lambda m, xx: _embed(m, xx, meta_full), meta_tokens, x)
    w_ev_in = full_weight("ev_in", h0)
    p0 = _matmul(h0, w_ev_in[0], "nn", "ev_in_fwd", after=gather_tokens, stacked=w_ev_in[1])
    small = (ev_conv_w, ev_conv_b, ev_w_rg_a, ev_b_rg_a, ev_w_rg_x, ev_b_rg_x, ev_lru_lambda, ev_q_norm_g, ev_w_uq,
             ev_kv_norm_g, ev_w_ukv)
    y0, vjp_even = jax.vjp(lambda p, *s: _even_mixer(p, *s, (conv_w_full, w_uq_full, w_ukv_full), bsz), p0, *small)
    w_ev_out = full_weight("ev_out", y0)
    mix0 = linear_fwd("ev_out", y0, w_ev_out)
    ln0_args = (h0, mix0, ln_mix_g[0:1], ln_mix_b[0:1])
    h1 = ln_fwd("mix0_ln", *ln0_args)
    h2, res_mlp0 = mlp_fwd(h1, 0)
    w_od_in = full_weight("od_in", h2)
    p1 = linear_fwd("od_in", h2, w_od_in)
    y1, vjp_odd = jax.vjp(lambda p: _odd_mixer(p, bsz), p1)
    w_od_out = full_weight("od_out", y1)
    mix1 = linear_fwd("od_out", y1, w_od_out)
    ln1_args = (h2, mix1, ln_mix_g[1:2], ln_mix_b[1:2])
    h3 = ln_fwd("mix1_ln", *ln1_args)
    h4, res_mlp1 = mlp_fwd(h3, 1)
    loss_local, vjp_loss = jax.vjp(lambda h: _local_loss(h, loss_target), h4)

    dh4 = vjp_loss(jnp.ones((), F32))
    dh3, dg_mlp1, db_mlp1 = mlp_bwd(dh4, res_mlp1, 1)
    dh2, dmix1, dg_mix1, db_mix1 = ln_bwd("mix1_ln", ln1_args, dh3)
    (dp1,) = vjp_odd(linear_bwd("od_out", y1, w_od_out, dmix1, False))
    dh2 = (dh2, linear_bwd("od_in", h2, w_od_in, dp1, True))
    dh1, dg_mlp0, db_mlp0 = mlp_bwd(dh2, res_mlp0, 0)
    dh0, dmix0, dg_mix0, db_mix0 = ln_bwd("mix0_ln", ln0_args, dh1)
    dp0, *dsmall = vjp_even(linear_bwd("ev_out", y0, w_ev_out, dmix0, False))
    dh0 = dh0 + linear_bwd("ev_in", h0, w_ev_in, dp0, True)
    g_meta, grad_x = vjp_embed(dh0)
    (g_conv_w, g_conv_b, g_w_rg_a, g_b_rg_a, g_w_rg_x, g_b_rg_x, g_lambda, g_q_norm, g_w_uq, g_kv_norm, g_w_ukv) = dsmall

    rep_names = ("ev_conv_b", "ev_w_rg_a", "ev_b_rg_a", "ev_w_rg_x", "ev_b_rg_x", "ev_lru_lambda", "ev_q_norm_g",
                 "ev_kv_norm_g", "ln_mix_g", "ln_mix_b", "ln_mlp_g", "ln_mlp_b")
    rep_local = (g_conv_b, g_w_rg_a, g_b_rg_a, g_w_rg_x, g_b_rg_x, g_lambda, g_q_norm, g_kv_norm,
                 jnp.concatenate([dg_mix0, dg_mix1]), jnp.concatenate([db_mix0, db_mix1]),
                 jnp.concatenate([dg_mlp0, dg_mlp1]), jnp.concatenate([db_mlp0, db_mlp1]))
    grad_w = dict(zip(rep_names, _allreduce_replicated(rep_local, "rep")))
    grad_w.update(meta_tokens=g_meta, ev_conv_w=g_conv_w, ev_w_uq=g_w_uq, ev_w_ukv=g_w_ukv)

    after, summed = grad_x, {}
    for nm, own, handle in pending:
        land = _exchange_wait(handle, False, after, "rs_wait_" + nm)
        summed[nm] = after = _sum_own_and_peers(own, land, "rs_sum_" + nm)
    grad_w.update(ev_w_in=summed["ev_in"][None], ev_w_out=summed["ev_out"][None], od_w_in=summed["od_in"][None],
                  od_w_out=summed["od_out"][None], mlp_w1=jnp.stack([summed["mlp0_w1"], summed["mlp1_w1"]]),
                  mlp_w2=jnp.stack([summed["mlp0_w2"], summed["mlp1_w2"]]))

    loss = lax.psum(loss_local, MESH_AXES)
    delta, new_m, new_v = {}, {}, {}
    for n in _WEIGHTS:
        w, g, m, v = weights[n], grad_w[n], args["m_" + n], args["v_" + n]
        two_d = (-1, w.shape[-1])
        d2, m2, v2 = _adamw(w.reshape(two_d), g.reshape(two_d), m.reshape(two_d), v.reshape(two_d), "adamw_" + n)
        delta[n], new_m[n], new_v[n] = d2.reshape(w.shape), m2.reshape(w.shape), v2.reshape(w.shape)
    return (loss, grad_x, *[grad_w[n] for n in _WEIGHTS], *[delta[n] for n in _WEIGHTS],
            *[new_m[n] for n in _WEIGHTS], *[new_v[n] for n in _WEIGHTS])
```

```python
import functools
import math

import jax
import jax.numpy as jnp
from jax import lax
from jax.experimental import pallas as pl
from jax.experimental.pallas import tpu as pltpu

F32 = jnp.float32
BF16 = jnp.bfloat16

N_DEV = 8
MESH_AXES = ("x", "y", "c")
LANES = 128
SEQ_BLOCK = 128

N_META = 16
LRU_C = 8.0
MLA_HEADS = 8
MLA_NOPE = 64
MLA_ROPE = 32
MLA_V = 64
RET_HEADS = 4
ROPE_BASE = 10000.0
DEPTH = 2
DN_ALPHA = (2 * DEPTH) ** 0.25
EPS = 1e-5
NEG_INF = -1e30

ADAM_LR = 0.001
ADAM_B1 = 0.9
ADAM_B2 = 0.999
ADAM_EPS = 1e-08
ADAM_WD = 0.01
ADAM_STEP = 10

VMEM_LIMIT = 56 * 1024 * 1024


def _params(*sem):
    return pltpu.CompilerParams(dimension_semantics=sem, vmem_limit_bytes=VMEM_LIMIT)


def _pick(n, cands):
    for c in cands:
        if n % c == 0:
            return c
    return n


def _row_tile(r, width):
    cands = (256, 128, 64, 32, 16, 8) if width <= 1024 else (128, 64, 32, 16, 8)
    return _pick(r, cands)


_DIMS = {"nn": (((1,), (0,)), ((), ())), "nt": (((1,), (1,)), ((), ())), "tn": (((0,), (0,)), ((), ()))}


def _dot(a, b, mode):
    return lax.dot_general(a.astype(BF16), b.astype(BF16), _DIMS[mode], preferred_element_type=F32)


def _matmul(a, b, mode, name, after=(), stacked=False, a_relu2=False, relu2_bwd_of=None):
    if stacked:
        n_blk = b.shape[2] if mode != "tn" else b.shape[1] // N_DEV
    if mode == "nn":
        (m, k), n = a.shape, (N_DEV * n_blk if stacked else b.shape[1])
    elif mode == "nt":
        (m, k), n = a.shape, (b.shape[1] if stacked else b.shape[0])
    else:
        (k, m), n = a.shape, b.shape[1]
    tm = _pick(m, (1088, 1024, 544, 512, 272, 256, 128, 64, 32, 16, 8))
    tn = _pick(n, (512, 256, 128))
    tk = _pick(k, (1088, 1024, 544, 512, 272, 256, 128))
    if stacked and mode in ("nn", "tn"):
        tn = n_blk
    kb = 2
    if stacked and mode == "nt":
        tk = kb * n_blk
    nk = k // tk

    out_spec = pl.BlockSpec((tm, tn), lambda i, j, kk: (i, j))
    out_shape = jax.ShapeDtypeStruct((m, n), F32)
    if mode == "nn":
        a_spec = pl.BlockSpec((tm, tk), lambda i, j, kk: (i, kk))
        b_spec = pl.BlockSpec((tk, tn), lambda i, j, kk: (kk, j))
        if stacked:
            b_spec = pl.BlockSpec((None, tk, tn), lambda i, j, kk: (j, kk, 0))
    elif mode == "nt":
        a_spec = pl.BlockSpec((tm, tk), lambda i, j, kk: (i, kk))
        b_spec = pl.BlockSpec((tn, tk), lambda i, j, kk: (j, kk))
        if stacked:
            b_spec = pl.BlockSpec((kb, tn, n_blk), lambda i, j, kk: (kk, j, 0))
    else:
        a_spec = pl.BlockSpec((tk, tm), lambda i, j, kk: (kk, i))
        b_spec = pl.BlockSpec((tk, tn), lambda i, j, kk: (kk, j))
        if stacked:
            out_spec = pl.BlockSpec((None, tm, tn), lambda i, j, kk: (j, i, 0))
            out_shape = jax.ShapeDtypeStruct((N_DEV, m, tn), F32)
    extra = [] if relu2_bwd_of is None else [relu2_bwd_of]
    extra_specs = [pl.BlockSpec((tm, tn), lambda i, j, kk: (i, j))] * len(extra)

    def body(a_ref, b_ref, *rest):
        o_ref = rest[-1]
        kk = pl.program_id(2)
        av = a_ref[...]
        if a_relu2:
            av = jnp.maximum(av, 0.0)
            av = av * av
        if stacked and mode == "nt":
            part = _dot(av[:, :n_blk], b_ref[0], mode)
            for q in range(1, kb):
                part = part + _dot(av[:, q * n_blk:(q + 1) * n_blk], b_ref[q], mode)
        else:
            part = _dot(av, b_ref[...], mode)

        @pl.when(kk == 0)
        def _():
            o_ref[...] = part

        @pl.when(kk != 0)
        def _():
            o_ref[...] += part

        if relu2_bwd_of is not None:
            @pl.when(kk == nk - 1)
            def _():
                o_ref[...] *= 2.0 * jnp.maximum(rest[0][...], 0.0)

    return pl.pallas_call(
        body,
        name=name,
        grid=(m // tm, n // tn, nk),
        in_specs=[a_spec, b_spec] + extra_specs + [pl.BlockSpec(memory_space=pl.ANY)] * len(after),
        out_specs=out_spec,
        out_shape=out_shape,
        compiler_params=_params("parallel", "parallel", "arbitrary"),
    )(a, b, *extra, *after)


def _group_matmul(a, w, mode, name):
    if mode in ("nn", "nt"):
        g, dk, dn = w.shape
        m = a.shape[0]
        d_in, d_out = (dk, dn) if mode == "nn" else (dn, dk)
        tm = _pick(m, (1088, 1024, 544, 512, 272, 256, 128, 64, 32, 16, 8))

        def body(a_ref, w_ref, o_ref):
            o_ref[...] = _dot(a_ref[...], w_ref[0], mode)

        return pl.pallas_call(
            body,
            name=name,
            grid=(g, m // tm),
            in_specs=[pl.BlockSpec((tm, d_in), lambda h, i: (i, h)), pl.BlockSpec((1, dk, dn), lambda h, i: (h, 0, 0))],
            out_specs=pl.BlockSpec((tm, d_out), lambda h, i: (i, h)),
            out_shape=jax.ShapeDtypeStruct((m, g * d_out), F32),
            compiler_params=_params("parallel", "parallel"),
        )(a, w)
    b = w
    m = a.shape[0]
    dk = dn = LANES
    g = a.shape[1] // dk
    tm = _pick(m, (1088, 1024, 544, 512, 272, 256, 128, 64, 32, 16, 8))

    def body(a_ref, b_ref, o_ref):
        part = _dot(a_ref[...], b_ref[...], "tn")

        @pl.when(pl.program_id(1) == 0)
        def _():
            o_ref[0] = part

        @pl.when(pl.program_id(1) != 0)
        def _():
            o_ref[0] += part

    return pl.pallas_call(
        body,
        name=name,
        grid=(g, m // tm),
        in_specs=[pl.BlockSpec((tm, dk), lambda h, i: (i, h)), pl.BlockSpec((tm, dn), lambda h, i: (i, h))],
        out_specs=pl.BlockSpec((1, dk, dn), lambda h, i: (h, 0, 0)),
        out_shape=jax.ShapeDtypeStruct((g, dk, dn), F32),
        compiler_params=_params("parallel", "arbitrary"),
    )(a, b)


def _make_group_linear(name):
    @jax.custom_vjp
    def op(x, w):
        return _group_matmul(x, w, "nn", name + "_fwd")

    def fwd(x, w):
        return op(x, w), (x, w)

    def bwd(res, dy):
        x, w = res
        return _group_matmul(dy, w, "nt", name + "_dx"), _group_matmul(x, dy, "tn", name + "_dw")

    op.defvjp(fwd, bwd)
    return op


def _my_place():
    return lax.axis_index("x"), lax.axis_index("y"), lax.axis_index("c")


def _all_gather(shard, name, after=()):
    shape, dtype = shard.shape, shard.dtype

    def body(x_ref, *rest):
        out_ref, send_sems, recv_sems, local_sem = rest[len(after):]
        x, y, c = _my_place()
        me, sibling = (x, y, c), (x, y, 1 - c)
        chips = [(1 - x, y), (x, 1 - y), (1 - x, 1 - y)]

        def slot(px, py, pc):
            return out_ref.at[4 * px + 2 * py + pc]

        def copy(k, block, to, src=None):
            return pltpu.make_async_remote_copy(
                src_ref=slot(*block) if src is None else src,
                dst_ref=slot(*block),
                send_sem=send_sems.at[k],
                recv_sem=recv_sems.at[k],
                device_id=to,
                device_id_type=pl.DeviceIdType.MESH,
            )

        mine = pltpu.make_async_copy(x_ref, slot(*me), local_sem)
        mine.start()
        first = [copy(0, me, sibling, src=x_ref)]
        first += [copy(1 + j, me, (*chip, c), src=x_ref) for j, chip in enumerate(chips)]
        for cp in first:
            cp.start()
        passed = [copy(4 + j, (*chip, c), sibling) for j, chip in enumerate(chips)]
        for j, chip in enumerate(chips):
            copy(1 + j, (*chip, c), me).wait_recv()
            passed[j].start()
        copy(0, sibling, me).wait_recv()
        for j, chip in enumerate(chips):
            copy(4 + j, (*chip, 1 - c), me).wait_recv()
        for cp in first + passed:
            cp.wait_send()
        mine.wait()

    return pl.pallas_call(
        body,
        name=name,
        out_shape=jax.ShapeDtypeStruct((N_DEV,) + shape, dtype),
        in_specs=[pl.BlockSpec(memory_space=pl.ANY)] * (1 + len(after)),
        out_specs=pl.BlockSpec(memory_space=pl.ANY),
        scratch_shapes=[pltpu.SemaphoreType.DMA((7,)), pltpu.SemaphoreType.DMA((7,)), pltpu.SemaphoreType.DMA],
    )(shard, *after)


def _all_to_all(stacked, name, after=()):
    def body(x_ref, *rest):
        out_ref, send_sems, recv_sems, local_sem = rest[len(after):]
        x, y, c = _my_place()
        me = 4 * x + 2 * y + c
        mine = pltpu.make_async_copy(x_ref.at[me], out_ref.at[me], local_sem)
        mine.start()
        copies = []
        for k in range(1, N_DEV):
            px, py, pc = x ^ ((k >> 2) & 1), y ^ ((k >> 1) & 1), c ^ (k & 1)
            peer = 4 * px + 2 * py + pc
            copies.append(
                pltpu.make_async_remote_copy(
                    src_ref=x_ref.at[peer],
                    dst_ref=out_ref.at[me],
                    send_sem=send_sems.at[k - 1],
                    recv_sem=recv_sems.at[k - 1],
                    device_id=(px, py, pc),
                    device_id_type=pl.DeviceIdType.MESH,
                )
            )
        for cp in copies:
            cp.start()
        for cp in copies:
            cp.wait_recv()
        for cp in copies:
            cp.wait_send()
        mine.wait()

    return pl.pallas_call(
        body,
        name=name,
        out_shape=jax.ShapeDtypeStruct(stacked.shape, stacked.dtype),
        in_specs=[pl.BlockSpec(memory_space=pl.ANY)] * (1 + len(after)),
        out_specs=pl.BlockSpec(memory_space=pl.ANY),
        scratch_shapes=[pltpu.SemaphoreType.DMA((7,)), pltpu.SemaphoreType.DMA((7,)), pltpu.SemaphoreType.DMA],
    )(stacked, *after)


def _sum_blocks(stacked, name):
    _, r, c = stacked.shape
    tr = _pick(r, (256, 128, 64, 32, 16, 8))

    def body(x_ref, o_ref):
        s = [x_ref[j] for j in range(N_DEV)]
        o_ref[...] = ((s[0] + s[1]) + (s[2] + s[3])) + ((s[4] + s[5]) + (s[6] + s[7]))

    return pl.pallas_call(
        body,
        name=name,
        grid=(r // tr,),
        in_specs=[pl.BlockSpec((N_DEV, tr, c), lambda i: (0, i, 0))],
        out_specs=pl.BlockSpec((tr, c), lambda i: (i, 0)),
        out_shape=jax.ShapeDtypeStruct((r, c), stacked.dtype),
        compiler_params=_params("parallel"),
    )(stacked)


def _stack_cols(full):
    k, n8 = full.shape
    return full.reshape(k, N_DEV, n8 // N_DEV).transpose(1, 0, 2)


def _unstack_cols(stacked):
    j, k, n = stacked.shape
    return stacked.transpose(1, 0, 2).reshape(k, j * n)


def _split_cols(p, cuts):
    bounds = (0,) + tuple(cuts) + (p.shape[1],)

    @jax.custom_vjp
    def op(z):
        return tuple(z[:, lo:hi] for lo, hi in zip(bounds[:-1], bounds[1:]))

    op.defvjp(lambda z: (op(z), None), lambda _, cots: (jnp.concatenate(cots, axis=1),))
    return op(p)


def _gather_weight(shard, cols, name):
    g = _all_gather(shard.astype(BF16), name)
    return _unstack_cols(g) if cols else g.reshape(-1, shard.shape[1])


def _scatter_grad(full, cols, name, after=()):
    if cols:
        st = _stack_cols(full)
    else:
        st = full.reshape(N_DEV, full.shape[0] // N_DEV, full.shape[1])
    return _sum_blocks(_all_to_all(st, name + "_a2a", after), name + "_sum")


def _make_fsdp_linear(cols, name, unpad=None):
    @jax.custom_vjp
    def op(x, w_shard, w_full):
        return _matmul(x, w_full, "nn", name + "_fwd")

    def fwd(x, w_shard, w_full):
        return op(x, w_shard, w_full), (x, w_full)

    def bwd(res, dy):
        x, w = res
        dx = _matmul(dy, w, "nt", name + "_dx")
        dw = _matmul(x, dy, "tn", name + "_dw")
        dw = dw if unpad is None else unpad(dw)
        return dx, _scatter_grad(dw, cols, name + "_rs"), jnp.zeros_like(w)

    op.defvjp(fwd, bwd)
    return op


def _make_fsdp_param(name):
    @jax.custom_vjp
    def op(shard, full):
        return full

    def fwd(shard, full):
        return full, None

    def bwd(_, g):
        return _scatter_grad(g, True, name + "_rs"), jnp.zeros_like(g)

    op.defvjp(fwd, bwd)
    return op


def _allreduce_replicated(gs, name, after=()):
    flat = jnp.concatenate([g.reshape(-1) for g in gs])
    n = flat.shape[0]
    rows = -(-n // (256 * LANES)) * 256
    packed = jnp.pad(flat, (0, rows * LANES - n)).reshape(rows, LANES)
    total = _sum_blocks(_all_gather(packed, name + "_ag", after), name + "_sum").reshape(-1)
    out, off = [], 0
    for g in gs:
        out.append(total[off:off + g.size].reshape(g.shape))
        off += g.size
    return out


_HBM = pl.BlockSpec(memory_space=pltpu.HBM)
_SEM = pl.BlockSpec(memory_space=pltpu.SEMAPHORE)
_SIDE_EFFECT = pltpu.SideEffectType.DATAFLOW_SIDE_EFFECTING
_N_PEERS = N_DEV - 1


def _peer(k):
    x, y, c = _my_place()
    return x ^ ((k >> 2) & 1), y ^ ((k >> 1) & 1), c ^ (k & 1)


def _exchange_start(src, land_shape, gather, name, after=()):
    def body(src_ref, land_ref, *rest):
        send_sems, recv_sems, src_thru, land_thru, token = rest[len(after):]
        x, y, c = _my_place()
        me = 4 * x + 2 * y + c
        for k in range(1, N_DEV):
            px, py, pc = _peer(k)
            pltpu.make_async_remote_copy(
                src_ref=src_ref if gather else src_ref.at[4 * px + 2 * py + pc],
                dst_ref=land_ref.at[me] if gather else land_ref.at[k - 1],
                send_sem=send_sems.at[k - 1],
                recv_sem=recv_sems.at[k - 1],
                device_id=(px, py, pc),
                device_id_type=pl.DeviceIdType.MESH,
            ).start()
        token[...] = jnp.zeros_like(token)

    return pl.pallas_call(
        body,
        name=name,
        out_shape=(
            pltpu.SemaphoreType.DMA((_N_PEERS,)),
            pltpu.SemaphoreType.DMA((_N_PEERS,)),
            pltpu.HBM(src.shape, src.dtype),
            pltpu.HBM(land_shape, src.dtype),
            jax.ShapeDtypeStruct((8, LANES), F32),
        ),
        in_specs=(_HBM, _HBM) + (pl.BlockSpec(memory_space=pl.ANY),) * len(after),
        out_specs=(_SEM, _SEM, _HBM, _HBM, pl.BlockSpec(memory_space=pltpu.VMEM)),
        input_output_aliases={0: 2, 1: 3},
        compiler_params=pltpu.CompilerParams(has_side_effects=_SIDE_EFFECT),
    )(pltpu.with_memory_space_constraint(src, pltpu.HBM),
      pltpu.with_memory_space_constraint(lax.empty(land_shape, src.dtype), pltpu.HBM), *after)


def _exchange_wait(handle, gather, after, name):
    send_sems, recv_sems, src_thru, land_thru, _ = handle

    def body(src_ref, land_ref, send_sems, recv_sems, after_ref, src_dead, got_ref):
        for k in range(1, N_DEV):
            cp = pltpu.make_async_remote_copy(
                src_ref=src_ref if gather else src_ref.at[k],
                dst_ref=land_ref.at[k - 1],
                send_sem=send_sems.at[k - 1],
                recv_sem=recv_sems.at[k - 1],
                device_id=_peer(k),
                device_id_type=pl.DeviceIdType.MESH,
            )
            cp.wait_send()
            cp.wait_recv()

    return pl.pallas_call(
        body,
        name=name,
        out_shape=(pltpu.HBM(src_thru.shape, src_thru.dtype), pltpu.HBM(land_thru.shape, land_thru.dtype)),
        in_specs=(_HBM, _HBM, _SEM, _SEM, pl.BlockSpec(memory_space=pl.ANY)),
        out_specs=(_HBM, _HBM),
        input_output_aliases={0: 0, 1: 1},
        compiler_params=pltpu.CompilerParams(has_side_effects=_SIDE_EFFECT),
    )(src_thru, land_thru, send_sems, recv_sems, after)[1]


def _sum_own_and_peers(own, land, name):
    r, c = own.shape
    tr = _pick(r, (256, 128, 64, 32, 16, 8))

    def body(o_ref, l_ref, out_ref):
        s = [l_ref[j] for j in range(_N_PEERS)]
        out_ref[...] = ((o_ref[...] + s[0]) + (s[1] + s[2])) + ((s[3] + s[4]) + (s[5] + s[6]))

    return pl.pallas_call(
        body,
        name=name,
        grid=(r // tr,),
        in_specs=[pl.BlockSpec((tr, c), lambda i: (i, 0)), pl.BlockSpec((_N_PEERS, tr, c), lambda i: (0, i, 0))],
        out_specs=pl.BlockSpec((tr, c), lambda i: (i, 0)),
        out_shape=jax.ShapeDtypeStruct((r, c), own.dtype),
        compiler_params=_params("parallel"),
    )(own, land)


def _make_rowwise(f, name, n_rows, n_tabs, n_pars):
    n_in = n_rows + n_tabs + n_pars

    def specs(args, tm):
        blocked = [pl.BlockSpec((tm, a.shape[1]), lambda i: (i, 0)) for a in args[: n_rows + n_tabs]]
        whole = [pl.BlockSpec(a.shape, lambda i: (0, 0)) for a in args[n_rows + n_tabs:]]
        return blocked + whole

    def out_struct(args, tm):
        blk = [jax.ShapeDtypeStruct((tm, a.shape[1]), a.dtype) for a in args[: n_rows + n_tabs]]
        blk += [jax.ShapeDtypeStruct(a.shape, a.dtype) for a in args[n_rows + n_tabs:]]
        return jax.eval_shape(f, *blk)

    def fwd_call(*args):
        r = args[0].shape[0]
        tm = _row_tile(r, max(a.shape[1] for a in args[:n_rows]))
        ro, so = out_struct(args, tm)

        def body(*refs):
            vals = [x[...] for x in refs[:n_in]]
            outs = refs[n_in:]
            rv, sv = f(*vals)
            for o, v in zip(outs[: len(ro)], rv):
                o[...] = v
            for o, v in zip(outs[len(ro):], sv):
                @pl.when(pl.program_id(0) == 0)
                def _(o=o, v=v):
                    o[...] = v

                @pl.when(pl.program_id(0) != 0)
                def _(o=o, v=v):
                    o[...] += v

        out_shape = [jax.ShapeDtypeStruct((r, s.shape[1]), s.dtype) for s in ro]
        out_shape += [jax.ShapeDtypeStruct(s.shape, s.dtype) for s in so]
        out_specs = [pl.BlockSpec((tm, s.shape[1]), lambda i: (i, 0)) for s in ro]
        out_specs += [pl.BlockSpec(s.shape, lambda i: (0, 0)) for s in so]
        res = pl.pallas_call(
            body,
            name=name + "_fwd",
            grid=(r // tm,),
            in_specs=specs(args, tm),
            out_specs=out_specs,
            out_shape=out_shape,
            compiler_params=_params("arbitrary" if so else "parallel"),
        )(*args)
        return tuple(res[: len(ro)]), tuple(res[len(ro):])

    def bwd_call(args, cots, more=()):
        r = args[0].shape[0]
        tm = _row_tile(r, max(a.shape[1] for a in args[:n_rows]))
        ro, so = out_struct(args, tm)
        crow, csum = cots
        rows, tabs, pars = args[:n_rows], args[n_rows:n_rows + n_tabs], args[n_rows + n_tabs:]
        n_c = len(crow) + len(csum)

        def body(*refs):
            vals = [x[...] for x in refs[:n_in]]
            cv = [x[...] for x in refs[n_in:n_in + n_c]]
            for x in refs[n_in + n_c:n_in + n_c + len(more)]:
                cv[0] = cv[0] + x[...]
            outs = refs[n_in + n_c + len(more):]
            tv = vals[n_rows:n_rows + n_tabs]

            def g(*dargs):
                return f(*dargs[:n_rows], *tv, *dargs[n_rows:])

            _, vjp = jax.vjp(g, *vals[:n_rows], *vals[n_rows + n_tabs:])
            d = vjp((tuple(cv[: len(crow)]), tuple(cv[len(crow):])))
            for o, v in zip(outs[:n_rows], d[:n_rows]):
                o[...] = v
            for o, v in zip(outs[n_rows:], d[n_rows:]):
                @pl.when(pl.program_id(0) == 0)
                def _(o=o, v=v):
                    o[...] = v

                @pl.when(pl.program_id(0) != 0)
                def _(o=o, v=v):
                    o[...] += v

        in_specs = specs(args, tm)
        in_specs += [pl.BlockSpec((tm, c.shape[1]), lambda i: (i, 0)) for c in crow]
        in_specs += [pl.BlockSpec(c.shape, lambda i: (0, 0)) for c in csum]
        in_specs += [pl.BlockSpec((tm, c.shape[1]), lambda i: (i, 0)) for c in more]
        out_shape = [jax.ShapeDtypeStruct(a.shape, a.dtype) for a in rows + pars]
        out_specs = [pl.BlockSpec((tm, a.shape[1]), lambda i: (i, 0)) for a in rows]
        out_specs += [pl.BlockSpec(a.shape, lambda i: (0, 0)) for a in pars]
        res = pl.pallas_call(
            body,
            name=name + "_bwd",
            grid=(r // tm,),
            in_specs=in_specs,
            out_specs=out_specs,
            out_shape=out_shape,
            compiler_params=_params("arbitrary" if pars else "parallel"),
        )(*args, *crow, *csum, *more)
        return tuple(res[:n_rows]), tuple(res[n_rows:])

    @jax.custom_vjp
    def op(rows, tabs, pars):
        return fwd_call(*rows, *tabs, *pars)

    op.fwd_call, op.bwd_call = fwd_call, bwd_call

    def fwd(rows, tabs, pars):
        return fwd_call(*rows, *tabs, *pars), (rows, tabs, pars)

    def bwd(res, cots):
        rows, tabs, pars = res
        drows, dpars = bwd_call(tuple(rows) + tuple(tabs) + tuple(pars), cots)
        return drows, tuple(jnp.zeros_like(t) for t in tabs), dpars

    op.defvjp(fwd, bwd)
    return op


def _sigmoid(x):
    return 0.5 * (jnp.tanh(0.5 * x) + 1.0)


@jax.custom_jvp
def _softplus(x):
    e = jnp.exp(-jnp.abs(x))
    u = 1.0 + e
    log1p_e = jnp.where(u == 1.0, e, e * jnp.log(u) / jnp.where(u == 1.0, 1.0, u - 1.0))
    return jnp.maximum(x, 0.0) + log1p_e


@_softplus.defjvp
def _softplus_jvp(primals, tangents):
    (x,), (t,) = primals, tangents
    return _softplus(x), t * _sigmoid(x)


def _gelu(x):
    return 0.5 * x * (1.0 + jnp.tanh(math.sqrt(2.0 / math.pi) * (x + 0.044715 * (x * x * x))))


def _ln_res_f(h, mix, g, b):
    z = DN_ALPHA * h + mix
    mu = jnp.mean(z, axis=-1, keepdims=True)
    zc = z - mu
    var = jnp.mean(zc * zc, axis=-1, keepdims=True)
    return (zc * lax.rsqrt(var + EPS) * g + b,), ()


def _rmsnorm_f(x, g):
    return (x * lax.rsqrt(jnp.mean(x * x, axis=-1, keepdims=True) + EPS) * g,), ()


def _lru_gates_f(ga, gx, xc, b_a, b_x, lam):
    r = _sigmoid(ga + b_a)
    i = _sigmoid(gx + b_x)
    log_a = -LRU_C * r * _softplus(-lam)
    a = jnp.exp(log_a)
    one_minus_a2 = jnp.tanh(-log_a) * (jnp.exp(2.0 * log_a) + 1.0)
    return (a, jnp.sqrt(one_minus_a2) * (i * xc)), ()


def _lru_out_f(hh, p_gate):
    return (hh * _gelu(p_gate),), ()


def _rope_ret_f(q, k, cos2, sin2):
    d = cos2.shape[1]
    half = d // 2
    k_scale = d ** -0.5

    def rope(x):
        outs = []
        for h in range(x.shape[1] // d):
            xh = x[:, h * d:(h + 1) * d]
            rot = jnp.concatenate([xh[:, half:], xh[:, :half]], axis=1)
            outs.append(xh * cos2 + rot * sin2)
        return jnp.concatenate(outs, axis=1)

    return (rope(q), rope(k) * k_scale), ()


def _ret_out_f(o, g):
    d = o.shape[1] // RET_HEADS
    outs = []
    for h in range(RET_HEADS):
        oh = o[:, h * d:(h + 1) * d]
        outs.append(oh * lax.rsqrt(jnp.mean(oh * oh, axis=-1, keepdims=True) + EPS))
    y = jnp.concatenate(outs, axis=1)
    return (g * _sigmoid(g) * y,), ()


def _loss_f(y, t, mask):
    e = (y - t) * mask
    per_row = jnp.sum(e * e, axis=-1, keepdims=True) * (0.5 / y.shape[1])
    total = jnp.sum(per_row, axis=0, keepdims=True)
    return (), (jnp.broadcast_to(total, (1, LANES)),)


def _shift_down(x, s):
    if s == 0:
        return x
    t = x.shape[0]
    row = lax.broadcasted_iota(jnp.int32, x.shape, 0)
    return jnp.where(row >= s, pltpu.roll(x, s, 0), 0.0)


def _shift_up(x, s):
    if s == 0:
        return x
    t = x.shape[0]
    row = lax.broadcasted_iota(jnp.int32, x.shape, 0)
    return jnp.where(row < t - s, pltpu.roll(x, t - s, 0), 0.0)


def _conv_fwd(x, w, b, name):
    bsz, t, c = x.shape
    width = w.shape[0]

    def body(x_ref, w_ref, b_ref, y_ref):
        xv = x_ref[0]
        acc = jnp.broadcast_to(b_ref[...], xv.shape)
        for k in range(width):
            acc = acc + w_ref[k:k + 1, :] * _shift_down(xv, width - 1 - k)
        y_ref[0] = acc

    return pl.pallas_call(
        body,
        name=name,
        grid=(bsz, c // LANES),
        in_specs=[
            pl.BlockSpec((1, t, LANES), lambda i, j: (i, 0, j)),
            pl.BlockSpec((width, LANES), lambda i, j: (0, j)),
            pl.BlockSpec((1, LANES), lambda i, j: (0, j)),
        ],
        out_specs=pl.BlockSpec((1, t, LANES), lambda i, j: (i, 0, j)),
        out_shape=jax.ShapeDtypeStruct(x.shape, F32),
        compiler_params=_params("parallel", "parallel"),
    )(x, w, b)


def _conv_bwd(x, w, dy, name):
    bsz, t, c = x.shape
    width = w.shape[0]

    def body(x_ref, w_ref, dy_ref, dx_ref, dw_ref, db_ref):
        xv, g = x_ref[0], dy_ref[0]
        dx = jnp.zeros_like(xv)
        dws = []
        for k in range(width):
            s = width - 1 - k
            dx = dx + w_ref[k:k + 1, :] * _shift_up(g, s)
            dws.append(jnp.sum(g * _shift_down(xv, s), axis=0, keepdims=True))
        dx_ref[0] = dx
        dw = jnp.concatenate(dws, axis=0)
        db = jnp.sum(g, axis=0, keepdims=True)

        @pl.when(pl.program_id(1) == 0)
        def _():
            dw_ref[...] = dw
            db_ref[...] = db

        @pl.when(pl.program_id(1) != 0)
        def _():
            dw_ref[...] += dw
            db_ref[...] += db

    return pl.pallas_call(
        body,
        name=name,
        grid=(c // LANES, bsz),
        in_specs=[
            pl.BlockSpec((1, t, LANES), lambda j, i: (i, 0, j)),
            pl.BlockSpec((width, LANES), lambda j, i: (0, j)),
            pl.BlockSpec((1, t, LANES), lambda j, i: (i, 0, j)),
        ],
        out_specs=[
            pl.BlockSpec((1, t, LANES), lambda j, i: (i, 0, j)),
            pl.BlockSpec((width, LANES), lambda j, i: (0, j)),
            pl.BlockSpec((1, LANES), lambda j, i: (0, j)),
        ],
        out_shape=[
            jax.ShapeDtypeStruct(x.shape, F32),
            jax.ShapeDtypeStruct(w.shape, F32),
            jax.ShapeDtypeStruct((1, c), F32),
        ],
        compiler_params=_params("parallel", "arbitrary"),
    )(x, w, dy)


def _make_conv(name):
    @jax.custom_vjp
    def op(x, w, b):
        return _conv_fwd(x, w, b, name + "_fwd")

    def fwd(x, w, b):
        return op(x, w, b), (x, w)

    def bwd(res, dy):
        x, w = res
        return tuple(_conv_bwd(x, w, dy, name + "_bwd"))

    op.defvjp(fwd, bwd)
    return op


def _scan_fwd(a, b, name):
    bsz, t, c = a.shape
    cw = _pick(c, (4 * LANES, 2 * LANES, LANES))

    def body(a_ref, b_ref, h_ref):
        def step(i, h):
            h = a_ref[0, pl.ds(i, 1), :] * h + b_ref[0, pl.ds(i, 1), :]
            h_ref[0, pl.ds(i, 1), :] = h
            return h

        lax.fori_loop(0, t, step, jnp.zeros((1, cw), F32), unroll=8)

    spec = pl.BlockSpec((1, t, cw), lambda i, j: (i, 0, j))
    return pl.pallas_call(
        body,
        name=name,
        grid=(bsz, c // cw),
        in_specs=[spec, spec],
        out_specs=spec,
        out_shape=jax.ShapeDtypeStruct(a.shape, F32),
        compiler_params=_params("parallel", "parallel"),
    )(a, b)


def _scan_bwd(a, h, g, name):
    bsz, t, c = a.shape
    cw = _pick(c, (2 * LANES, LANES))

    def body(a_ref, h_ref, g_ref, da_ref, db_ref):
        def step(n, carry):
            i = t - 1 - n
            lam = g_ref[0, pl.ds(i, 1), :] + carry
            db_ref[0, pl.ds(i, 1), :] = lam
            prev = jnp.where(i > 0, h_ref[0, pl.ds(jnp.maximum(i - 1, 0), 1), :], 0.0)
            da_ref[0, pl.ds(i, 1), :] = lam * prev
            return a_ref[0, pl.ds(i, 1), :] * lam

        lax.fori_loop(0, t, step, jnp.zeros((1, cw), F32), unroll=8)

    spec = pl.BlockSpec((1, t, cw), lambda i, j: (i, 0, j))
    return pl.pallas_call(
        body,
        name=name,
        grid=(bsz, c // cw),
        in_specs=[spec, spec, spec],
        out_specs=[spec, spec],
        out_shape=[jax.ShapeDtypeStruct(a.shape, F32)] * 2,
        compiler_params=_params("parallel", "parallel"),
    )(a, h, g)


def _make_scan(name):
    @jax.custom_vjp
    def op(a, b):
        return _scan_fwd(a, b, name + "_fwd")

    def fwd(a, b):
        h = op(a, b)
        return h, (a, h)

    def bwd(res, g):
        a, h = res
        da, db = _scan_bwd(a, h, g, name + "_bwd")
        return da, db

    op.defvjp(fwd, bwd)
    return op


def _query_blocks(t):
    blocks, start = [], 0
    while start < t:
        rows = 2 * SEQ_BLOCK if start + 2 * SEQ_BLOCK <= t else SEQ_BLOCK
        blocks.append((start, rows))
        start += rows
    return blocks


def _attn_probs(q, k, start, scale):
    tq, tk = q.shape[0], k.shape[0]
    s = _dot(q, k, "nt") * scale
    qpos = start + lax.broadcasted_iota(jnp.int32, (tq, tk), 0)
    kpos = lax.broadcasted_iota(jnp.int32, (tq, tk), 1)
    s = jnp.where(kpos <= qpos, s, NEG_INF)
    e = jnp.exp(s - jnp.max(s, axis=-1, keepdims=True))
    return e / jnp.sum(e, axis=-1, keepdims=True)


_MLA_SCALE = (MLA_NOPE + MLA_ROPE) ** -0.5


def _attn_specs(t):
    head = pl.BlockSpec((1, t, LANES), lambda b, h: (b, 0, h))
    shared = pl.BlockSpec((1, t, LANES), lambda b, h: (b, 0, 0))
    return head, shared


def _attn_fwd(q, kv, kpe, name):
    bsz, t, hl = q.shape
    head, shared = _attn_specs(t)

    def body(q_ref, kv_ref, kpe_ref, o_ref, k_s, v_s):
        lane = lax.broadcasted_iota(jnp.int32, (t, LANES), 1)
        kvh = kv_ref[0]
        k_s[...] = jnp.where(lane < MLA_NOPE, kvh, kpe_ref[0]).astype(BF16)
        v_s[...] = kvh.astype(BF16)
        for start, rows in _query_blocks(t):
            n = start + rows
            p = _attn_probs(q_ref[0, start:n, :], k_s[:n, :], start, _MLA_SCALE)
            o_ref[0, start:n, :] = _dot(p, v_s[:n, :], "nn")

    return pl.pallas_call(
        body,
        name=name,
        grid=(bsz, hl // LANES),
        in_specs=[head, head, shared],
        out_specs=head,
        out_shape=jax.ShapeDtypeStruct(q.shape, F32),
        scratch_shapes=[pltpu.VMEM((t, LANES), BF16), pltpu.VMEM((t, LANES), BF16)],
        compiler_params=_params("parallel", "parallel"),
    )(q, kv, kpe)


def _attn_bwd(q, kv, kpe, do, name):
    bsz, t, hl = q.shape
    head, shared = _attn_specs(t)

    def body(q_ref, kv_ref, kpe_ref, do_ref, dq_ref, dkv_ref, dkpe_ref, k_s, v_s, dk_s, dv_s):
        lane = lax.broadcasted_iota(jnp.int32, (t, LANES), 1)
        kvh = kv_ref[0]
        k_s[...] = jnp.where(lane < MLA_NOPE, kvh, kpe_ref[0]).astype(BF16)
        v_s[...] = kvh.astype(BF16)
        for start, rows in reversed(_query_blocks(t)):
            n = start + rows
            qb = q_ref[0, start:n, :]
            dob = jnp.where(lane[:rows] >= MLA_NOPE, do_ref[0, start:n, :], 0.0)
            kk, vv = k_s[:n, :], v_s[:n, :]
            p = _attn_probs(qb, kk, start, _MLA_SCALE)
            dp = _dot(dob, vv, "nt")
            ds = p * (dp - jnp.sum(dp * p, axis=-1, keepdims=True)) * _MLA_SCALE
            dq_ref[0, start:n, :] = _dot(ds, kk, "nn")
            if n == t:
                dk_s[...] = _dot(ds, qb, "tn")
                dv_s[...] = _dot(p, dob, "tn")
            else:
                dk_s[:n, :] += _dot(ds, qb, "tn")
                dv_s[:n, :] += _dot(p, dob, "tn")
        dk = dk_s[...]
        dkv_ref[0] = jnp.where(lane < MLA_NOPE, dk, dv_s[...])
        dkpe = jnp.where(lane >= MLA_NOPE, dk, 0.0)

        @pl.when(pl.program_id(1) == 0)
        def _():
            dkpe_ref[0] = dkpe

        @pl.when(pl.program_id(1) != 0)
        def _():
            dkpe_ref[0] += dkpe

    return pl.pallas_call(
        body,
        name=name,
        grid=(bsz, hl // LANES),
        in_specs=[head, head, shared, head],
        out_specs=[head, head, shared],
        out_shape=[
            jax.ShapeDtypeStruct(q.shape, F32),
            jax.ShapeDtypeStruct(kv.shape, F32),
            jax.ShapeDtypeStruct(kpe.shape, F32),
        ],
        scratch_shapes=[pltpu.VMEM((t, LANES), BF16), pltpu.VMEM((t, LANES), BF16),
                        pltpu.VMEM((t, LANES), F32), pltpu.VMEM((t, LANES), F32)],
        compiler_params=_params("parallel", "arbitrary"),
    )(q, kv, kpe, do)


def _make_attention(name):
    @jax.custom_vjp
    def op(q, kv, kpe):
        return _attn_fwd(q, kv, kpe, name + "_fwd")

    def fwd(q, kv, kpe):
        return op(q, kv, kpe), (q, kv, kpe)

    def bwd(res, do):
        return tuple(_attn_bwd(*res, do, name + "_bwd"))

    op.defvjp(fwd, bwd)
    return op


_ROPE_SHIFT = MLA_ROPE // 2


def _rope_lanes_call(x, c, sm, sp, transpose, name):
    r, width = x.shape
    tm = _row_tile(r, width)

    def body(x_ref, c_ref, sm_ref, sp_ref, y_ref):
        cv, smv, spv = c_ref[...], sm_ref[...], sp_ref[...]
        for b in range(width // LANES):
            xb = x_ref[:, b * LANES:(b + 1) * LANES]
            if transpose:
                yb = xb * cv + pltpu.roll(xb * smv, _ROPE_SHIFT, 1) + pltpu.roll(xb * spv, LANES - _ROPE_SHIFT, 1)
            else:
                yb = xb * cv + pltpu.roll(xb, LANES - _ROPE_SHIFT, 1) * smv + pltpu.roll(xb, _ROPE_SHIFT, 1) * spv
            y_ref[:, b * LANES:(b + 1) * LANES] = yb

    tab = pl.BlockSpec((tm, LANES), lambda i: (i, 0))
    blk = pl.BlockSpec((tm, width), lambda i: (i, 0))
    return pl.pallas_call(
        body,
        name=name,
        grid=(r // tm,),
        in_specs=[blk, tab, tab, tab],
        out_specs=blk,
        out_shape=jax.ShapeDtypeStruct(x.shape, F32),
        compiler_params=_params("parallel"),
    )(x, c, sm, sp)


def _make_rope_lanes(name):
    @jax.custom_vjp
    def op(x, c, sm, sp):
        return _rope_lanes_call(x, c, sm, sp, False, name + "_fwd")

    def fwd(x, c, sm, sp):
        return op(x, c, sm, sp), (c, sm, sp)

    def bwd(res, dy):
        c, sm, sp = res
        return _rope_lanes_call(dy, c, sm, sp, True, name + "_bwd"), jnp.zeros_like(c), jnp.zeros_like(sm), jnp.zeros_like(sp)

    op.defvjp(fwd, bwd)
    return op


RET_KEY_CHUNK = 512


def _key_chunks(t):
    return [(c, min(RET_KEY_CHUNK, t - c)) for c in range(0, t, RET_KEY_CHUNK)]


def _decay(q0, k0, tq, tk, log_gamma):
    row = lax.broadcasted_iota(jnp.int32, (tq, 1), 0)
    col = (q0 - k0) - lax.broadcasted_iota(jnp.int32, (1, tk), 1)
    outer = jnp.exp(log_gamma * row.astype(F32)) * jnp.exp(log_gamma * col.astype(F32))
    return jnp.where(row + col >= 0, outer, 0.0)


def _ret_query_rows(t):
    return t // 4 if t % 32 == 0 else SEQ_BLOCK


def _ret_specs(t, dk, dv, heads):
    tq = _ret_query_rows(t)
    return (
        pl.BlockSpec(memory_space=pltpu.SMEM),
        pl.BlockSpec((1, tq, dk), lambda b, h, i: (b, i, h)),
        pl.BlockSpec((1, t, dk), lambda b, h, i: (b, 0, h)),
        pl.BlockSpec((1, t, dv), lambda b, h, i: (b, 0, h)),
        pl.BlockSpec((1, tq, dv), lambda b, h, i: (b, i, h)),
    )


def _ret_fwd(lg, q, k, v, name):
    bsz, t, hdk = q.shape
    heads = lg.shape[0]
    dk, dv = hdk // heads, v.shape[2] // heads
    lg_spec, q_spec, k_spec, v_spec, o_spec = _ret_specs(t, dk, dv, heads)
    tq = _ret_query_rows(t)

    def body(lg_ref, q_ref, k_ref, v_ref, o_ref):
        q0 = pl.program_id(2) * tq
        lgh = lg_ref[pl.program_id(1)]
        o_ref[0] = jnp.zeros((tq, dv), F32)
        for c0, cw in _key_chunks(t):
            @pl.when(c0 < q0 + tq)
            def _(c0=c0, cw=cw):
                d = _decay(q0, c0, tq, cw, lgh)
                a = _dot(q_ref[0], k_ref[0, c0:c0 + cw, :], "nt") * d
                o_ref[0] += _dot(a, v_ref[0, c0:c0 + cw, :], "nn")

    return pl.pallas_call(
        body,
        name=name,
        grid=(bsz, heads, t // tq),
        in_specs=[lg_spec, q_spec, k_spec, v_spec],
        out_specs=o_spec,
        out_shape=jax.ShapeDtypeStruct(v.shape, F32),
        compiler_params=_params("parallel", "parallel", "parallel"),
    )(lg, q, k, v)


def _ret_bwd(lg, q, k, v, do, name):
    bsz, t, hdk = q.shape
    heads = lg.shape[0]
    dk, dv = hdk // heads, v.shape[2] // heads
    lg_spec, q_spec, k_spec, v_spec, o_spec = _ret_specs(t, dk, dv, heads)
    tq = _ret_query_rows(t)

    def body(lg_ref, q_ref, k_ref, v_ref, do_ref, dq_ref, dk_ref, dv_ref):
        q0 = pl.program_id(2) * tq
        lgh = lg_ref[pl.program_id(1)]

        @pl.when(pl.program_id(2) == 0)
        def _():
            dk_ref[0] = jnp.zeros((t, dk), F32)
            dv_ref[0] = jnp.zeros((t, dv), F32)

        dq_ref[0] = jnp.zeros((tq, dk), F32)
        for c0, cw in _key_chunks(t):
            @pl.when(c0 < q0 + tq)
            def _(c0=c0, cw=cw):
                qb, dob = q_ref[0], do_ref[0]
                kk, vv = k_ref[0, c0:c0 + cw, :], v_ref[0, c0:c0 + cw, :]
                d = _decay(q0, c0, tq, cw, lgh)
                a = _dot(qb, kk, "nt") * d
                ds = _dot(dob, vv, "nt") * d
                dq_ref[0] += _dot(ds, kk, "nn")
                dk_ref[0, c0:c0 + cw, :] += _dot(ds, qb, "tn")
                dv_ref[0, c0:c0 + cw, :] += _dot(a, dob, "tn")

    return pl.pallas_call(
        body,
        name=name,
        grid=(bsz, heads, t // tq),
        in_specs=[lg_spec, q_spec, k_spec, v_spec, o_spec],
        out_specs=[q_spec, k_spec, v_spec],
        out_shape=[
            jax.ShapeDtypeStruct(q.shape, F32),
            jax.ShapeDtypeStruct(k.shape, F32),
            jax.ShapeDtypeStruct(v.shape, F32),
        ],
        compiler_params=_params("parallel", "parallel", "arbitrary"),
    )(lg, q, k, v, do)


def _make_retention(name):
    @jax.custom_vjp
    def op(lg, q, k, v):
        return _ret_fwd(lg, q, k, v, name + "_fwd")

    def fwd(lg, q, k, v):
        return op(lg, q, k, v), (lg, q, k, v)

    def bwd(res, do):
        lg = res[0]
        return (jnp.zeros_like(lg),) + tuple(_ret_bwd(*res, do, name + "_bwd"))

    op.defvjp(fwd, bwd)
    return op


def _adamw(w, g, m, v, name):
    r, c = w.shape
    tr = _pick(r, (256, 128, 64, 32, 16, 8))

    def body(w_ref, g_ref, m_ref, v_ref, d_ref, nm_ref, nv_ref):
        gv = g_ref[...]
        nm = ADAM_B1 * m_ref[...] + (1.0 - ADAM_B1) * gv
        nv = ADAM_B2 * v_ref[...] + (1.0 - ADAM_B2) * (gv * gv)
        m_hat = nm / (1.0 - ADAM_B1 ** ADAM_STEP)
        v_hat = nv / (1.0 - ADAM_B2 ** ADAM_STEP)
        d_ref[...] = -ADAM_LR * (m_hat / (jnp.sqrt(v_hat) + ADAM_EPS) + ADAM_WD * w_ref[...])
        nm_ref[...] = nm
        nv_ref[...] = nv

    spec = pl.BlockSpec((tr, c), lambda i: (i, 0))
    return pl.pallas_call(
        body,
        name=name,
        grid=(r // tr,),
        in_specs=[spec] * 4,
        out_specs=[spec] * 3,
        out_shape=[jax.ShapeDtypeStruct((r, c), F32)] * 3,
        compiler_params=_params("parallel"),
    )(w, g, m, v)


def _rope_tables(t, half, reps):
    inv = ROPE_BASE ** (-jnp.arange(half, dtype=F32) / half)
    ang = jnp.arange(t, dtype=jnp.int32).astype(F32)[:, None] * inv[None, :]
    return jnp.tile(jnp.cos(ang), (1, reps)), jnp.tile(jnp.sin(ang), (1, reps))


def _padded_len(seq):
    return -(-(N_META + seq) // SEQ_BLOCK) * SEQ_BLOCK


def _embed(meta, x):
    bsz, seq, d = x.shape
    t = _padded_len(seq)
    return jnp.concatenate(
        [jnp.broadcast_to(meta[None], (bsz, N_META, d)), x, jnp.zeros((bsz, t - N_META - seq, d), F32)], axis=1
    ).reshape(bsz * t, d)


def _even_mixer(p, conv_w_shard, conv_b, w_rg_a, b_rg_a, w_rg_x, b_rg_x, lru_lambda, q_norm_g, w_uq_shard,
                kv_norm_g, w_ukv_shard, gathered, bsz):
    conv_w_full, w_uq_full, w_ukv_full = gathered
    r = p.shape[0]
    t = r // bsz

    def tile_rows(tab):
        return jnp.tile(tab, (bsz, 1))

    lru_w = w_rg_a.shape[2] * w_rg_a.shape[1]
    q_rank, kv_rank = q_norm_g.shape[1], kv_norm_g.shape[1]
    p_gate, p_rec, p_q, p_kv, p_kpe = _split_cols(
        p, (lru_w, 2 * lru_w, 2 * lru_w + q_rank, 2 * lru_w + q_rank + kv_rank))

    conv_w = _make_fsdp_param("conv_w")(conv_w_shard[0], conv_w_full)
    xc = _make_conv("conv")(p_rec.reshape(bsz, t, lru_w), conv_w, conv_b).reshape(r, lru_w)
    ga = _make_group_linear("rg_a")(xc, w_rg_a[0])
    gx = _make_group_linear("rg_x")(xc, w_rg_x[0])
    (a, bb), _ = _make_rowwise(_lru_gates_f, "lru_gates", 3, 0, 3)((ga, gx, xc), (), (b_rg_a, b_rg_x, lru_lambda))
    hh = _make_scan("lru_scan")(a.reshape(bsz, t, lru_w), bb.reshape(bsz, t, lru_w)).reshape(r, lru_w)
    (y_rec,), _ = _make_rowwise(_lru_out_f, "lru_out", 2, 0, 0)((hh, p_gate), (), ())

    (qn,), _ = _make_rowwise(_rmsnorm_f, "q_norm", 1, 0, 1)((p_q,), (), (q_norm_g,))
    (kvn,), _ = _make_rowwise(_rmsnorm_f, "kv_norm", 1, 0, 1)((p_kv,), (), (kv_norm_g,))
    d_head = MLA_NOPE + MLA_ROPE
    w_uq_pad = jnp.pad(w_uq_full.reshape(q_rank, MLA_HEADS, d_head), ((0, 0), (0, 0), (0, LANES - d_head)))

    def unpad_uq(dw):
        return dw.reshape(q_rank, MLA_HEADS, LANES)[:, :, :d_head].reshape(q_rank, MLA_HEADS * d_head)

    q = _make_fsdp_linear(True, "ev_uq", unpad_uq)(qn, w_uq_shard[0], w_uq_pad.reshape(q_rank, MLA_HEADS * LANES))
    kv = _make_fsdp_linear(True, "ev_ukv")(kvn, w_ukv_shard[0], w_ukv_full)
    half = MLA_ROPE // 2
    cos, sin = _rope_tables(t, half, 1)
    one, zero = jnp.ones((t, MLA_NOPE), F32), jnp.zeros((t, MLA_NOPE), F32)
    tail = LANES - MLA_NOPE - MLA_ROPE
    c_tab = tile_rows(jnp.concatenate([one, cos, cos, one[:, :tail]], axis=1))
    sm_tab = tile_rows(jnp.concatenate([zero, -sin, zero[:, :half + tail]], axis=1))
    sp_tab = tile_rows(jnp.concatenate([zero, zero[:, :half], sin, zero[:, :tail]], axis=1))
    q = _make_rope_lanes("rope_q")(q, c_tab, sm_tab, sp_tab)
    kpe = _make_rope_lanes("rope_k")(p_kpe, c_tab, sm_tab, sp_tab)
    o = _make_attention("mla")(q.reshape(bsz, t, -1), kv.reshape(bsz, t, -1), kpe.reshape(bsz, t, LANES))
    return jnp.concatenate([y_rec, o.reshape(r, -1)], axis=1)


def _odd_mixer(p, bsz):
    r = p.shape[0]
    t = r // bsz

    def tile_rows(tab):
        return jnp.tile(tab, (bsz, 1))

    qk = p.shape[1] // 6
    dk = qk // RET_HEADS
    cos2, sin2 = _rope_tables(t, dk // 2, 2)
    sin2 = jnp.concatenate([-sin2[:, :dk // 2], sin2[:, dk // 2:]], axis=1)
    p_q, p_k, p_v, p_g = _split_cols(p, (qk, 2 * qk, 4 * qk))
    (rq, rk), _ = _make_rowwise(_rope_ret_f, "rope_ret", 2, 2, 0)((p_q, p_k), (tile_rows(cos2), tile_rows(sin2)), ())
    lg = jnp.log(1.0 - 2.0 ** (-5.0 - jnp.arange(RET_HEADS, dtype=F32)))
    o = _make_retention("ret")(lg, rq.reshape(bsz, t, qk), rk.reshape(bsz, t, qk), p_v.reshape(bsz, t, 2 * qk))
    (y,), _ = _make_rowwise(_ret_out_f, "ret_out", 2, 0, 0)((o.reshape(r, 2 * qk), p_g), (), ())
    return y


def _local_loss(h, target):
    bsz, seq, d = target.shape
    t = _padded_len(seq)
    t_real = N_META + seq
    pos = jnp.arange(t, dtype=jnp.int32)
    mask = jnp.tile(((pos >= N_META) & (pos < t_real)).astype(F32)[:, None], (bsz, 1))
    tgt = jnp.concatenate(
        [jnp.zeros((bsz, N_META, d), F32), target, jnp.zeros((bsz, t - t_real, d), F32)], axis=1).reshape(bsz * t, d)
    _, (total,) = _make_rowwise(_loss_f, "loss", 1, 2, 0)((h,), (tgt, mask), ())
    return total[0, 0]


_WEIGHTS = ("meta_tokens", "ev_w_in", "ev_conv_w", "ev_conv_b", "ev_w_rg_a", "ev_b_rg_a", "ev_w_rg_x", "ev_b_rg_x",
            "ev_lru_lambda", "ev_q_norm_g", "ev_w_uq", "ev_kv_norm_g", "ev_w_ukv", "ev_w_out", "od_w_in", "od_w_out",
            "ln_mix_g", "ln_mix_b", "mlp_w1", "mlp_w2", "ln_mlp_g", "ln_mlp_b")


def kernel(x, meta_tokens, ev_w_in, ev_conv_w, ev_conv_b, ev_w_rg_a, ev_b_rg_a, ev_w_rg_x, ev_b_rg_x, ev_lru_lambda, ev_q_norm_g, ev_w_uq, ev_kv_norm_g, ev_w_ukv, ev_w_out, od_w_in, od_w_out, ln_mix_g, ln_mix_b, mlp_w1, mlp_w2, ln_mlp_g, ln_mlp_b, loss_target, m_meta_tokens, m_ev_w_in, m_ev_conv_w, m_ev_conv_b, m_ev_w_rg_a, m_ev_b_rg_a, m_ev_w_rg_x, m_ev_b_rg_x, m_ev_lru_lambda, m_ev_q_norm_g, m_ev_w_uq, m_ev_kv_norm_g, m_ev_w_ukv, m_ev_w_out, m_od_w_in, m_od_w_out, m_ln_mix_g, m_ln_mix_b, m_mlp_w1, m_mlp_w2, m_ln_mlp_g, m_ln_mlp_b, v_meta_tokens, v_ev_w_in, v_ev_conv_w, v_ev_conv_b, v_ev_w_rg_a, v_ev_b_rg_a, v_ev_w_rg_x, v_ev_b_rg_x, v_ev_lru_lambda, v_ev_q_norm_g, v_ev_w_uq, v_ev_kv_norm_g, v_ev_w_ukv, v_ev_w_out, v_od_w_in, v_od_w_out, v_ln_mix_g, v_ln_mix_b, v_mlp_w1, v_mlp_w2, v_ln_mlp_g, v_ln_mlp_b):
    args = locals()
    weights = {n: args[n] for n in _WEIGHTS}
    bsz = x.shape[0]
    my_x, my_y, my_c = _my_place()
    me = 4 * my_x + 2 * my_y + my_c

    big = (("ev_in", ev_w_in[0], True), ("ev_out", ev_w_out[0], False), ("mlp0_w1", mlp_w1[0], True),
           ("mlp0_w2", mlp_w2[0], False), ("od_in", od_w_in[0], True), ("od_out", od_w_out[0], False),
           ("mlp1_w1", mlp_w1[1], True), ("mlp1_w2", mlp_w2[1], False))
    small_sharded = (("meta", meta_tokens, F32), ("conv_w", ev_conv_w[0], F32), ("ev_uq", ev_w_uq[0], BF16),
                     ("ev_ukv", ev_w_ukv[0], BF16))
    gathers, prev = {}, ()
    for nm, shard16, cols in (tuple((nm, s.astype(dt), True) for nm, s, dt in small_sharded)
                              + tuple((nm, s.astype(BF16), cols) for nm, s, cols in big)):
        handle = _exchange_start(shard16, (N_DEV,) + shard16.shape, True, "ag_start_" + nm, after=prev)
        gathers[nm], prev = (shard16, cols, handle), (handle[4],)
    gather_tokens = tuple(g[2][4] for g in gathers.values())

    def full_weight(nm, after):
        shard16, cols, handle = gathers[nm]
        land = _exchange_wait(handle, True, after, "ag_wait_" + nm)
        land = lax.dynamic_update_index_in_dim(land, shard16, me, 0)
        if cols and shard16.shape[1] % LANES == 0:
            return land, True
        return (_unstack_cols(land) if cols else land.reshape(-1, shard16.shape[1])), False

    meta_full, conv_w_full, w_uq_full, w_ukv_full = (
        _unstack_cols(lax.dynamic_update_index_in_dim(
            _exchange_wait(gathers[nm][2], True, gather_tokens[-1], "ag_wait_" + nm), gathers[nm][0], me, 0))
        for nm, _, _ in small_sharded)

    pending = []

    def linear_bwd(nm, x_in, w, dy, cols, unpad=None, **fused):
        w_full, w_stacked = w
        a_relu2 = fused.pop("a_relu2", False)
        if w_stacked:
            stacked = _matmul(x_in, dy, "tn", nm + "_dw", stacked=True, a_relu2=a_relu2)
            own = lax.dynamic_index_in_dim(stacked, me, 0, keepdims=False)
        else:
            dw = _matmul(x_in, dy, "tn", nm + "_dw", a_relu2=a_relu2)
            dw = dw if unpad is None else unpad(dw)
            n = dw.shape[1] // N_DEV
            if cols:
                stacked = _stack_cols(dw)
                own = lax.dynamic_slice_in_dim(dw, me * n, n, axis=1)
            else:
                stacked = dw.reshape(N_DEV, dw.shape[0] // N_DEV, dw.shape[1])
                own = lax.dynamic_index_in_dim(stacked, me, 0, keepdims=False)
        handle = _exchange_start(stacked, (_N_PEERS,) + stacked.shape[1:], False, "rs_start_" + nm)
        pending.append((nm, own, handle))
        return _matmul(dy, w_full, "nt", nm + "_dx", after=(handle[4],), stacked=w_stacked, **fused)

    def linear_fwd(nm, x_in, w, **fused):
        return _matmul(x_in, w[0], "nn", nm + "_fwd", stacked=w[1], **fused)

    def mlp_fwd(h, l):
        w1 = full_weight(f"mlp{l}_w1", h)
        u = linear_fwd(f"mlp{l}_w1", h, w1)
        w2 = full_weight(f"mlp{l}_w2", u)
        f = linear_fwd(f"mlp{l}_w2", u, w2, a_relu2=True)
        ln_args = (h, f, ln_mlp_g[l:l + 1], ln_mlp_b[l:l + 1])
        return ln_fwd(f"mlp{l}_ln", *ln_args), (h, w1, u, w2, ln_args)

    def mlp_bwd(dout, res, l):
        h, w1, u, w2, ln_args = res
        dh, df, dg, db = ln_bwd(f"mlp{l}_ln", ln_args, dout)
        du = linear_bwd(f"mlp{l}_w2", u, w2, df, False, a_relu2=True, relu2_bwd_of=u)
        return (dh, linear_bwd(f"mlp{l}_w1", h, w1, du, True)), dg, db

    def ln_fwd(nm, h, mix, g, b):
        return _make_rowwise(_ln_res_f, nm, 2, 0, 2).fwd_call(h, mix, g, b)[0][0]

    def ln_bwd(nm, ln_args, pieces):
        (dh, dmix), (dg, db) = _make_rowwise(_ln_res_f, nm, 2, 0, 2).bwd_call(
            ln_args, ((pieces[0],), ()), more=tuple(pieces[1:]))
        return dh, dmix, dg, db

    h0, vjp_embed = jax.vjp(_embed, meta_full, x)
    n_in = ev_w_in.shape[2] * N_DEV
    kpe0, pad_lo, pad_hi = n_in - MLA_ROPE, MLA_NOPE, LANES - MLA_NOPE - MLA_ROPE
    w_in = full_weight("ev_in", h0)[0]
    zeros_in = jnp.zeros((w_in.shape[0], pad_lo), BF16)
    w_ev_in = (jnp.concatenate([w_in[:, :kpe0], zeros_in, w_in[:, kpe0:], zeros_in[:, :pad_hi]], axis=1), False)

    def unpad_in(dw):
        return jnp.concatenate([dw[:, :kpe0], dw[:, kpe0 + pad_lo:kpe0 + pad_lo + MLA_ROPE]], axis=1)

    p0 = _matmul(h0, w_ev_in[0], "nn", "ev_in_fwd", after=gather_tokens)
    small = (ev_conv_w, ev_conv_b, ev_w_rg_a, ev_b_rg_a, ev_w_rg_x, ev_b_rg_x, ev_lru_lambda, ev_q_norm_g, ev_w_uq,
             ev_kv_norm_g, ev_w_ukv)
    y0, vjp_even = jax.vjp(lambda p, *s: _even_mixer(p, *s, (conv_w_full, w_uq_full, w_ukv_full), bsz), p0, *small)
    w_out = full_weight("ev_out", y0)[0]
    lru_w, d_model = y0.shape[1] - MLA_HEADS * LANES, w_out.shape[1]
    w_att = w_out[lru_w:].reshape(MLA_HEADS, MLA_V, d_model)
    w_att = jnp.concatenate([jnp.zeros((MLA_HEADS, LANES - MLA_V, d_model), BF16), w_att], axis=1)
    w_ev_out = (jnp.concatenate([w_out[:lru_w], w_att.reshape(MLA_HEADS * LANES, d_model)], axis=0), False)

    def unpad_out(dw):
        d_att = dw[lru_w:].reshape(MLA_HEADS, LANES, d_model)[:, LANES - MLA_V:].reshape(MLA_HEADS * MLA_V, d_model)
        return jnp.concatenate([dw[:lru_w], d_att], axis=0)

    mix0 = linear_fwd("ev_out", y0, w_ev_out)
    ln0_args = (h0, mix0, ln_mix_g[0:1], ln_mix_b[0:1])
    h1 = ln_fwd("mix0_ln", *ln0_args)
    h2, res_mlp0 = mlp_fwd(h1, 0)
    w_od_in = full_weight("od_in", h2)
    p1 = linear_fwd("od_in", h2, w_od_in)
    y1, vjp_odd = jax.vjp(lambda p: _odd_mixer(p, bsz), p1)
    w_od_out = full_weight("od_out", y1)
    mix1 = linear_fwd("od_out", y1, w_od_out)
    ln1_args = (h2, mix1, ln_mix_g[1:2], ln_mix_b[1:2])
    h3 = ln_fwd("mix1_ln", *ln1_args)
    h4, res_mlp1 = mlp_fwd(h3, 1)
    loss_local, vjp_loss = jax.vjp(lambda h: _local_loss(h, loss_target), h4)

    dh4 = vjp_loss(jnp.ones((), F32))
    dh3, dg_mlp1, db_mlp1 = mlp_bwd(dh4, res_mlp1, 1)
    dh2, dmix1, dg_mix1, db_mix1 = ln_bwd("mix1_ln", ln1_args, dh3)
    (dp1,) = vjp_odd(linear_bwd("od_out", y1, w_od_out, dmix1, False))
    dh2 = (dh2, linear_bwd("od_in", h2, w_od_in, dp1, True))
    dh1, dg_mlp0, db_mlp0 = mlp_bwd(dh2, res_mlp0, 0)
    dh0, dmix0, dg_mix0, db_mix0 = ln_bwd("mix0_ln", ln0_args, dh1)
    dp0, *dsmall = vjp_even(linear_bwd("ev_out", y0, w_ev_out, dmix0, False, unpad=unpad_out))
    dh0 = dh0 + linear_bwd("ev_in", h0, w_ev_in, dp0, True, unpad=unpad_in)
    g_meta_full, grad_x = vjp_embed(dh0)
    (g_conv_w, g_conv_b, g_w_rg_a, g_b_rg_a, g_w_rg_x, g_b_rg_x, g_lambda, g_q_norm, g_w_uq, g_kv_norm, g_w_ukv) = dsmall

    after, summed = grad_x, {}
    for nm, own, handle in pending:
        land = _exchange_wait(handle, False, after, "rs_wait_" + nm)
        summed[nm] = after = _sum_own_and_peers(own, land, "rs_sum_" + nm)

    g_meta = _scatter_grad(g_meta_full, True, "meta_rs", after=(after,))
    rep_names = ("ev_conv_b", "ev_w_rg_a", "ev_b_rg_a", "ev_w_rg_x", "ev_b_rg_x", "ev_lru_lambda", "ev_q_norm_g",
                 "ev_kv_norm_g", "ln_mix_g", "ln_mix_b", "ln_mlp_g", "ln_mlp_b")
    rep_local = (g_conv_b, g_w_rg_a, g_b_rg_a, g_w_rg_x, g_b_rg_x, g_lambda, g_q_norm, g_kv_norm,
                 jnp.concatenate([dg_mix0, dg_mix1]), jnp.concatenate([db_mix0, db_mix1]),
                 jnp.concatenate([dg_mlp0, dg_mlp1]), jnp.concatenate([db_mlp0, db_mlp1]))
    grad_w = dict(zip(rep_names, _allreduce_replicated(rep_local, "rep", after=(g_meta,))))
    grad_w.update(meta_tokens=g_meta, ev_conv_w=g_conv_w, ev_w_uq=g_w_uq, ev_w_ukv=g_w_ukv)
    grad_w.update(ev_w_in=summed["ev_in"][None], ev_w_out=summed["ev_out"][None], od_w_in=summed["od_in"][None],
                  od_w_out=summed["od_out"][None], mlp_w1=jnp.stack([summed["mlp0_w1"], summed["mlp1_w1"]]),
                  mlp_w2=jnp.stack([summed["mlp0_w2"], summed["mlp1_w2"]]))

    loss = lax.psum(loss_local, MESH_AXES)
    delta, new_m, new_v = {}, {}, {}
    for n in _WEIGHTS:
        w, g, m, v = weights[n], grad_w[n], args["m_" + n], args["v_" + n]
        two_d = (-1, w.shape[-1])
        d2, m2, v2 = _adamw(w.reshape(two_d), g.reshape(two_d), m.reshape(two_d), v.reshape(two_d), "adamw_" + n)
        delta[n], new_m[n], new_v[n] = d2.reshape(w.shape), m2.reshape(w.shape), v2.reshape(w.shape)
    return (loss, grad_x, *[grad_w[n] for n in _WEIGHTS], *[delta[n] for n in _WEIGHTS],
            *[new_m[n] for n in _WEIGHTS], *[new_v[n] for n in _WEIGHTS])
```

```python
import functools
import math

import jax
import jax.numpy as jnp
from jax import lax
from jax.experimental import pallas as pl
from jax.experimental.pallas import tpu as pltpu

F32 = jnp.float32
BF16 = jnp.bfloat16

N_DEV = 8
MESH_AXES = ("x", "y", "c")
LANES = 128
SEQ_BLOCK = 128

N_META = 16
LRU_C = 8.0
MLA_HEADS = 8
MLA_NOPE = 64
MLA_ROPE = 32
MLA_V = 64
RET_HEADS = 4
ROPE_BASE = 10000.0
DEPTH = 2
DN_ALPHA = (2 * DEPTH) ** 0.25
EPS = 1e-5
NEG_INF = -1e30

ADAM_LR = 0.001
ADAM_B1 = 0.9
ADAM_B2 = 0.999
ADAM_EPS = 1e-08
ADAM_WD = 0.01
ADAM_STEP = 10

VMEM_LIMIT = 56 * 1024 * 1024


def _params(*sem):
    return pltpu.CompilerParams(dimension_semantics=sem, vmem_limit_bytes=VMEM_LIMIT)


def _pick(n, cands):
    for c in cands:
        if n % c == 0:
            return c
    return n


def _row_tile(r, width):
    cands = (256, 128, 64, 32, 16, 8) if width <= 1024 else (128, 64, 32, 16, 8)
    return _pick(r, cands)


_DIMS = {"nn": (((1,), (0,)), ((), ())), "nt": (((1,), (1,)), ((), ())), "tn": (((0,), (0,)), ((), ()))}


def _dot(a, b, mode):
    return lax.dot_general(a.astype(BF16), b.astype(BF16), _DIMS[mode], preferred_element_type=F32)


def _matmul(a, b, mode, name, after=(), stacked=False, a_relu2=False, relu2_bwd_of=None):
    if stacked:
        n_blk = b.shape[2] if mode != "tn" else b.shape[1] // N_DEV
    if mode == "nn":
        (m, k), n = a.shape, (N_DEV * n_blk if stacked else b.shape[1])
    elif mode == "nt":
        (m, k), n = a.shape, (b.shape[1] if stacked else b.shape[0])
    else:
        (k, m), n = a.shape, b.shape[1]
    tm = _pick(m, (1088, 1024, 544, 512, 272, 256, 128, 64, 32, 16, 8))
    tn = _pick(n, (1024, 512, 256, 128))
    tk = _pick(k, (1088, 1024, 544, 512, 272, 256, 128))
    kb = 2
    if stacked and mode == "nn":
        tn = n_blk
    if stacked and mode == "tn":
        tn = kb * n_blk
    if stacked and mode == "nt":
        tk = kb * n_blk
    nk = k // tk

    out_spec = pl.BlockSpec((tm, tn), lambda i, j, kk: (i, j))
    out_shape = jax.ShapeDtypeStruct((m, n), F32)
    if mode == "nn":
        a_spec = pl.BlockSpec((tm, tk), lambda i, j, kk: (i, kk))
        b_spec = pl.BlockSpec((tk, tn), lambda i, j, kk: (kk, j))
        if stacked:
            b_spec = pl.BlockSpec((None, tk, tn), lambda i, j, kk: (j, kk, 0))
    elif mode == "nt":
        a_spec = pl.BlockSpec((tm, tk), lambda i, j, kk: (i, kk))
        b_spec = pl.BlockSpec((tn, tk), lambda i, j, kk: (j, kk))
        if stacked:
            b_spec = pl.BlockSpec((kb, tn, n_blk), lambda i, j, kk: (kk, j, 0))
    else:
        a_spec = pl.BlockSpec((tk, tm), lambda i, j, kk: (kk, i))
        b_spec = pl.BlockSpec((tk, tn), lambda i, j, kk: (kk, j))
        if stacked:
            out_spec = pl.BlockSpec((kb, tm, n_blk), lambda i, j, kk: (j, i, 0))
            out_shape = jax.ShapeDtypeStruct((N_DEV, m, n_blk), F32)
    extra = [] if relu2_bwd_of is None else [relu2_bwd_of]
    extra_specs = [pl.BlockSpec((tm, tn), lambda i, j, kk: (i, j))] * len(extra)

    def body(a_ref, b_ref, *rest):
        o_ref = rest[-1]
        kk = pl.program_id(2)
        av = a_ref[...]
        if a_relu2:
            av = jnp.maximum(av, 0.0)
            av = av * av
        if stacked and mode == "nt":
            part = _dot(av[:, :n_blk], b_ref[0], mode)
            for q in range(1, kb):
                part = part + _dot(av[:, q * n_blk:(q + 1) * n_blk], b_ref[q], mode)
        else:
            part = _dot(av, b_ref[...], mode)
        if stacked and mode == "tn":
            part = jnp.stack([part[:, q * n_blk:(q + 1) * n_blk] for q in range(kb)])

        @pl.when(kk == 0)
        def _():
            o_ref[...] = part

        @pl.when(kk != 0)
        def _():
            o_ref[...] += part

        if relu2_bwd_of is not None:
            @pl.when(kk == nk - 1)
            def _():
                o_ref[...] *= 2.0 * jnp.maximum(rest[0][...], 0.0)

    return pl.pallas_call(
        body,
        name=name,
        grid=(m // tm, n // tn, nk),
        in_specs=[a_spec, b_spec] + extra_specs + [pl.BlockSpec(memory_space=pl.ANY)] * len(after),
        out_specs=out_spec,
        out_shape=out_shape,
        compiler_params=_params("parallel", "parallel", "arbitrary"),
    )(a, b, *extra, *after)


def _group_matmul(a, w, mode, name):
    if mode in ("nn", "nt"):
        g, dk, dn = w.shape
        m = a.shape[0]
        d_in, d_out = (dk, dn) if mode == "nn" else (dn, dk)
        tm = _pick(m, (1088, 1024, 544, 512, 272, 256, 128, 64, 32, 16, 8))

        def body(a_ref, w_ref, o_ref):
            o_ref[...] = _dot(a_ref[...], w_ref[0], mode)

        return pl.pallas_call(
            body,
            name=name,
            grid=(g, m // tm),
            in_specs=[pl.BlockSpec((tm, d_in), lambda h, i: (i, h)), pl.BlockSpec((1, dk, dn), lambda h, i: (h, 0, 0))],
            out_specs=pl.BlockSpec((tm, d_out), lambda h, i: (i, h)),
            out_shape=jax.ShapeDtypeStruct((m, g * d_out), F32),
            compiler_params=_params("parallel", "parallel"),
        )(a, w)
    b = w
    m = a.shape[0]
    dk = dn = LANES
    g = a.shape[1] // dk
    tm = _pick(m, (1088, 1024, 544, 512, 272, 256, 128, 64, 32, 16, 8))

    def body(a_ref, b_ref, o_ref):
        part = _dot(a_ref[...], b_ref[...], "tn")

        @pl.when(pl.program_id(1) == 0)
        def _():
            o_ref[0] = part

        @pl.when(pl.program_id(1) != 0)
        def _():
            o_ref[0] += part

    return pl.pallas_call(
        body,
        name=name,
        grid=(g, m // tm),
        in_specs=[pl.BlockSpec((tm, dk), lambda h, i: (i, h)), pl.BlockSpec((tm, dn), lambda h, i: (i, h))],
        out_specs=pl.BlockSpec((1, dk, dn), lambda h, i: (h, 0, 0)),
        out_shape=jax.ShapeDtypeStruct((g, dk, dn), F32),
        compiler_params=_params("parallel", "arbitrary"),
    )(a, b)


def _make_group_linear(name):
    @jax.custom_vjp
    def op(x, w):
        return _group_matmul(x, w, "nn", name + "_fwd")

    def fwd(x, w):
        return op(x, w), (x, w)

    def bwd(res, dy):
        x, w = res
        return _group_matmul(dy, w, "nt", name + "_dx"), _group_matmul(x, dy, "tn", name + "_dw")

    op.defvjp(fwd, bwd)
    return op


def _my_place():
    return lax.axis_index("x"), lax.axis_index("y"), lax.axis_index("c")


def _all_gather(shard, name, after=()):
    shape, dtype = shard.shape, shard.dtype

    def body(x_ref, *rest):
        out_ref, send_sems, recv_sems, local_sem = rest[len(after):]
        x, y, c = _my_place()
        me, sibling = (x, y, c), (x, y, 1 - c)
        chips = [(1 - x, y), (x, 1 - y), (1 - x, 1 - y)]

        def slot(px, py, pc):
            return out_ref.at[4 * px + 2 * py + pc]

        def copy(k, block, to, src=None):
            return pltpu.make_async_remote_copy(
                src_ref=slot(*block) if src is None else src,
                dst_ref=slot(*block),
                send_sem=send_sems.at[k],
                recv_sem=recv_sems.at[k],
                device_id=to,
                device_id_type=pl.DeviceIdType.MESH,
            )

        mine = pltpu.make_async_copy(x_ref, slot(*me), local_sem)
        mine.start()
        first = [copy(0, me, sibling, src=x_ref)]
        first += [copy(1 + j, me, (*chip, c), src=x_ref) for j, chip in enumerate(chips)]
        for cp in first:
            cp.start()
        passed = [copy(4 + j, (*chip, c), sibling) for j, chip in enumerate(chips)]
        for j, chip in enumerate(chips):
            copy(1 + j, (*chip, c), me).wait_recv()
            passed[j].start()
        copy(0, sibling, me).wait_recv()
        for j, chip in enumerate(chips):
            copy(4 + j, (*chip, 1 - c), me).wait_recv()
        for cp in first + passed:
            cp.wait_send()
        mine.wait()

    return pl.pallas_call(
        body,
        name=name,
        out_shape=jax.ShapeDtypeStruct((N_DEV,) + shape, dtype),
        in_specs=[pl.BlockSpec(memory_space=pl.ANY)] * (1 + len(after)),
        out_specs=pl.BlockSpec(memory_space=pl.ANY),
        scratch_shapes=[pltpu.SemaphoreType.DMA((7,)), pltpu.SemaphoreType.DMA((7,)), pltpu.SemaphoreType.DMA],
    )(shard, *after)


def _all_to_all(stacked, name, after=()):
    def body(x_ref, *rest):
        out_ref, send_sems, recv_sems, local_sem = rest[len(after):]
        x, y, c = _my_place()
        me = 4 * x + 2 * y + c
        mine = pltpu.make_async_copy(x_ref.at[me], out_ref.at[me], local_sem)
        mine.start()
        copies = []
        for k in range(1, N_DEV):
            px, py, pc = x ^ ((k >> 2) & 1), y ^ ((k >> 1) & 1), c ^ (k & 1)
            peer = 4 * px + 2 * py + pc
            copies.append(
                pltpu.make_async_remote_copy(
                    src_ref=x_ref.at[peer],
                    dst_ref=out_ref.at[me],
                    send_sem=send_sems.at[k - 1],
                    recv_sem=recv_sems.at[k - 1],
                    device_id=(px, py, pc),
                    device_id_type=pl.DeviceIdType.MESH,
                )
            )
        for cp in copies:
            cp.start()
        for cp in copies:
            cp.wait_recv()
        for cp in copies:
            cp.wait_send()
        mine.wait()

    return pl.pallas_call(
        body,
        name=name,
        out_shape=jax.ShapeDtypeStruct(stacked.shape, stacked.dtype),
        in_specs=[pl.BlockSpec(memory_space=pl.ANY)] * (1 + len(after)),
        out_specs=pl.BlockSpec(memory_space=pl.ANY),
        scratch_shapes=[pltpu.SemaphoreType.DMA((7,)), pltpu.SemaphoreType.DMA((7,)), pltpu.SemaphoreType.DMA],
    )(stacked, *after)


def _sum_blocks(stacked, name):
    _, r, c = stacked.shape
    tr = _pick(r, (256, 128, 64, 32, 16, 8))

    def body(x_ref, o_ref):
        s = [x_ref[j] for j in range(N_DEV)]
        o_ref[...] = ((s[0] + s[1]) + (s[2] + s[3])) + ((s[4] + s[5]) + (s[6] + s[7]))

    return pl.pallas_call(
        body,
        name=name,
        grid=(r // tr,),
        in_specs=[pl.BlockSpec((N_DEV, tr, c), lambda i: (0, i, 0))],
        out_specs=pl.BlockSpec((tr, c), lambda i: (i, 0)),
        out_shape=jax.ShapeDtypeStruct((r, c), stacked.dtype),
        compiler_params=_params("parallel"),
    )(stacked)


def _stack_cols(full):
    k, n8 = full.shape
    return full.reshape(k, N_DEV, n8 // N_DEV).transpose(1, 0, 2)


def _unstack_cols(stacked):
    j, k, n = stacked.shape
    return stacked.transpose(1, 0, 2).reshape(k, j * n)


def _split_cols(p, cuts):
    bounds = (0,) + tuple(cuts) + (p.shape[1],)

    @jax.custom_vjp
    def op(z):
        return tuple(z[:, lo:hi] for lo, hi in zip(bounds[:-1], bounds[1:]))

    op.defvjp(lambda z: (op(z), None), lambda _, cots: (jnp.concatenate(cots, axis=1),))
    return op(p)


def _gather_weight(shard, cols, name):
    g = _all_gather(shard.astype(BF16), name)
    return _unstack_cols(g) if cols else g.reshape(-1, shard.shape[1])


def _scatter_grad(full, cols, name, after=()):
    if cols:
        st = _stack_cols(full)
    else:
        st = full.reshape(N_DEV, full.shape[0] // N_DEV, full.shape[1])
    return _sum_blocks(_all_to_all(st, name + "_a2a", after), name + "_sum")


def _make_fsdp_linear(cols, name, unpad=None):
    @jax.custom_vjp
    def op(x, w_shard, w_full):
        return _matmul(x, w_full, "nn", name + "_fwd")

    def fwd(x, w_shard, w_full):
        return op(x, w_shard, w_full), (x, w_full)

    def bwd(res, dy):
        x, w = res
        dx = _matmul(dy, w, "nt", name + "_dx")
        dw = _matmul(x, dy, "tn", name + "_dw")
        dw = dw if unpad is None else unpad(dw)
        return dx, _scatter_grad(dw, cols, name + "_rs"), jnp.zeros_like(w)

    op.defvjp(fwd, bwd)
    return op


def _make_fsdp_param(name):
    @jax.custom_vjp
    def op(shard, full):
        return full

    def fwd(shard, full):
        return full, None

    def bwd(_, g):
        return _scatter_grad(g, True, name + "_rs"), jnp.zeros_like(g)

    op.defvjp(fwd, bwd)
    return op


def _allreduce_replicated(gs, name, after=()):
    flat = jnp.concatenate([g.reshape(-1) for g in gs])
    n = flat.shape[0]
    rows = -(-n // (256 * LANES)) * 256
    packed = jnp.pad(flat, (0, rows * LANES - n)).reshape(rows, LANES)
    total = _sum_blocks(_all_gather(packed, name + "_ag", after), name + "_sum").reshape(-1)
    out, off = [], 0
    for g in gs:
        out.append(total[off:off + g.size].reshape(g.shape))
        off += g.size
    return out


_HBM = pl.BlockSpec(memory_space=pltpu.HBM)
_SEM = pl.BlockSpec(memory_space=pltpu.SEMAPHORE)
_SIDE_EFFECT = pltpu.SideEffectType.DATAFLOW_SIDE_EFFECTING
_N_PEERS = N_DEV - 1


def _peer(k):
    x, y, c = _my_place()
    return x ^ ((k >> 2) & 1), y ^ ((k >> 1) & 1), c ^ (k & 1)


def _exchange_start(src, land_shape, gather, name, after=()):
    def body(src_ref, land_ref, *rest):
        send_sems, recv_sems, src_thru, land_thru, token = rest[len(after):]
        x, y, c = _my_place()
        me = 4 * x + 2 * y + c
        for k in range(1, N_DEV):
            px, py, pc = _peer(k)
            pltpu.make_async_remote_copy(
                src_ref=src_ref if gather else src_ref.at[4 * px + 2 * py + pc],
                dst_ref=land_ref.at[me] if gather else land_ref.at[k - 1],
                send_sem=send_sems.at[k - 1],
                recv_sem=recv_sems.at[k - 1],
                device_id=(px, py, pc),
                device_id_type=pl.DeviceIdType.MESH,
            ).start()
        token[...] = jnp.zeros_like(token)

    return pl.pallas_call(
        body,
        name=name,
        out_shape=(
            pltpu.SemaphoreType.DMA((_N_PEERS,)),
            pltpu.SemaphoreType.DMA((_N_PEERS,)),
            pltpu.HBM(src.shape, src.dtype),
            pltpu.HBM(land_shape, src.dtype),
            jax.ShapeDtypeStruct((8, LANES), F32),
        ),
        in_specs=(_HBM, _HBM) + (pl.BlockSpec(memory_space=pl.ANY),) * len(after),
        out_specs=(_SEM, _SEM, _HBM, _HBM, pl.BlockSpec(memory_space=pltpu.VMEM)),
        input_output_aliases={0: 2, 1: 3},
        compiler_params=pltpu.CompilerParams(has_side_effects=_SIDE_EFFECT),
    )(pltpu.with_memory_space_constraint(src, pltpu.HBM),
      pltpu.with_memory_space_constraint(lax.empty(land_shape, src.dtype), pltpu.HBM), *after)


def _gather_start_all(shards, name):
    n = len(shards)

    def body(*refs):
        srcs, lands = refs[:n], refs[n:2 * n]
        outs = refs[2 * n:]
        send_sems, recv_sems, token = outs[:n], outs[n:2 * n], outs[-1]
        x, y, c = _my_place()
        me = 4 * x + 2 * y + c
        for i in range(n):
            for k in range(1, N_DEV):
                pltpu.make_async_remote_copy(
                    src_ref=srcs[i],
                    dst_ref=lands[i].at[me],
                    send_sem=send_sems[i].at[k - 1],
                    recv_sem=recv_sems[i].at[k - 1],
                    device_id=_peer(k),
                    device_id_type=pl.DeviceIdType.MESH,
                ).start()
        token[...] = jnp.zeros_like(token)

    lands = [(N_DEV,) + s.shape for s in shards]
    sems = tuple(pltpu.SemaphoreType.DMA((_N_PEERS,)) for _ in range(2 * n))
    res = pl.pallas_call(
        body,
        name=name,
        out_shape=sems + tuple(pltpu.HBM(s.shape, s.dtype) for s in shards)
        + tuple(pltpu.HBM(ls, s.dtype) for ls, s in zip(lands, shards)) + (jax.ShapeDtypeStruct((8, LANES), F32),),
        in_specs=(_HBM,) * (2 * n),
        out_specs=(_SEM,) * (2 * n) + (_HBM,) * (2 * n) + (pl.BlockSpec(memory_space=pltpu.VMEM),),
        input_output_aliases={i: 2 * n + i for i in range(2 * n)},
        compiler_params=pltpu.CompilerParams(has_side_effects=_SIDE_EFFECT),
    )(*[pltpu.with_memory_space_constraint(s, pltpu.HBM) for s in shards],
      *[pltpu.with_memory_space_constraint(lax.empty(ls, s.dtype), pltpu.HBM) for ls, s in zip(lands, shards)])
    return [(res[i], res[n + i], res[2 * n + i], res[3 * n + i], res[-1]) for i in range(n)]


def _exchange_wait(handle, gather, after, name):
    send_sems, recv_sems, src_thru, land_thru, _ = handle

    def body(src_ref, land_ref, send_sems, recv_sems, after_ref, src_dead, got_ref):
        for k in range(1, N_DEV):
            cp = pltpu.make_async_remote_copy(
                src_ref=src_ref if gather else src_ref.at[k],
                dst_ref=land_ref.at[k - 1],
                send_sem=send_sems.at[k - 1],
                recv_sem=recv_sems.at[k - 1],
                device_id=_peer(k),
                device_id_type=pl.DeviceIdType.MESH,
            )
            cp.wait_send()
            cp.wait_recv()

    return pl.pallas_call(
        body,
        name=name,
        out_shape=(pltpu.HBM(src_thru.shape, src_thru.dtype), pltpu.HBM(land_thru.shape, land_thru.dtype)),
        in_specs=(_HBM, _HBM, _SEM, _SEM, pl.BlockSpec(memory_space=pl.ANY)),
        out_specs=(_HBM, _HBM),
        input_output_aliases={0: 0, 1: 1},
        compiler_params=pltpu.CompilerParams(has_side_effects=_SIDE_EFFECT),
    )(src_thru, land_thru, send_sems, recv_sems, after)[1]


def _sum_own_and_peers(own, land, name):
    r, c = own.shape
    tr = _pick(r, (256, 128, 64, 32, 16, 8))

    def body(o_ref, l_ref, out_ref):
        s = [l_ref[j] for j in range(_N_PEERS)]
        out_ref[...] = ((o_ref[...] + s[0]) + (s[1] + s[2])) + ((s[3] + s[4]) + (s[5] + s[6]))

    return pl.pallas_call(
        body,
        name=name,
        grid=(r // tr,),
        in_specs=[pl.BlockSpec((tr, c), lambda i: (i, 0)), pl.BlockSpec((_N_PEERS, tr, c), lambda i: (0, i, 0))],
        out_specs=pl.BlockSpec((tr, c), lambda i: (i, 0)),
        out_shape=jax.ShapeDtypeStruct((r, c), own.dtype),
        compiler_params=_params("parallel"),
    )(own, land)


def _make_rowwise(f, name, n_rows, n_tabs, n_pars):
    n_in = n_rows + n_tabs + n_pars

    def specs(args, tm):
        blocked = [pl.BlockSpec((tm, a.shape[1]), lambda i: (i, 0)) for a in args[: n_rows + n_tabs]]
        whole = [pl.BlockSpec(a.shape, lambda i: (0, 0)) for a in args[n_rows + n_tabs:]]
        return blocked + whole

    def out_struct(args, tm):
        blk = [jax.ShapeDtypeStruct((tm, a.shape[1]), a.dtype) for a in args[: n_rows + n_tabs]]
        blk += [jax.ShapeDtypeStruct(a.shape, a.dtype) for a in args[n_rows + n_tabs:]]
        return jax.eval_shape(f, *blk)

    def fwd_call(*args):
        r = args[0].shape[0]
        tm = _row_tile(r, max(a.shape[1] for a in args[:n_rows]))
        ro, so = out_struct(args, tm)

        def body(*refs):
            vals = [x[...] for x in refs[:n_in]]
            outs = refs[n_in:]
            rv, sv = f(*vals)
            for o, v in zip(outs[: len(ro)], rv):
                o[...] = v
            for o, v in zip(outs[len(ro):], sv):
                @pl.when(pl.program_id(0) == 0)
                def _(o=o, v=v):
                    o[...] = v

                @pl.when(pl.program_id(0) != 0)
                def _(o=o, v=v):
                    o[...] += v

        out_shape = [jax.ShapeDtypeStruct((r, s.shape[1]), s.dtype) for s in ro]
        out_shape += [jax.ShapeDtypeStruct(s.shape, s.dtype) for s in so]
        out_specs = [pl.BlockSpec((tm, s.shape[1]), lambda i: (i, 0)) for s in ro]
        out_specs += [pl.BlockSpec(s.shape, lambda i: (0, 0)) for s in so]
        res = pl.pallas_call(
            body,
            name=name + "_fwd",
            grid=(r // tm,),
            in_specs=specs(args, tm),
            out_specs=out_specs,
            out_shape=out_shape,
            compiler_params=_params("arbitrary" if so else "parallel"),
        )(*args)
        return tuple(res[: len(ro)]), tuple(res[len(ro):])

    def bwd_call(args, cots, more=()):
        r = args[0].shape[0]
        tm = _row_tile(r, max(a.shape[1] for a in args[:n_rows]))
        ro, so = out_struct(args, tm)
        crow, csum = cots
        rows, tabs, pars = args[:n_rows], args[n_rows:n_rows + n_tabs], args[n_rows + n_tabs:]
        n_c = len(crow) + len(csum)

        def body(*refs):
            vals = [x[...] for x in refs[:n_in]]
            cv = [x[...] for x in refs[n_in:n_in + n_c]]
            for x in refs[n_in + n_c:n_in + n_c + len(more)]:
                cv[0] = cv[0] + x[...]
            outs = refs[n_in + n_c + len(more):]
            tv = vals[n_rows:n_rows + n_tabs]

            def g(*dargs):
                return f(*dargs[:n_rows], *tv, *dargs[n_rows:])

            _, vjp = jax.vjp(g, *vals[:n_rows], *vals[n_rows + n_tabs:])
            d = vjp((tuple(cv[: len(crow)]), tuple(cv[len(crow):])))
            for o, v in zip(outs[:n_rows], d[:n_rows]):
                o[...] = v
            for o, v in zip(outs[n_rows:], d[n_rows:]):
                @pl.when(pl.program_id(0) == 0)
                def _(o=o, v=v):
                    o[...] = v

                @pl.when(pl.program_id(0) != 0)
                def _(o=o, v=v):
                    o[...] += v

        in_specs = specs(args, tm)
        in_specs += [pl.BlockSpec((tm, c.shape[1]), lambda i: (i, 0)) for c in crow]
        in_specs += [pl.BlockSpec(c.shape, lambda i: (0, 0)) for c in csum]
        in_specs += [pl.BlockSpec((tm, c.shape[1]), lambda i: (i, 0)) for c in more]
        out_shape = [jax.ShapeDtypeStruct(a.shape, a.dtype) for a in rows + pars]
        out_specs = [pl.BlockSpec((tm, a.shape[1]), lambda i: (i, 0)) for a in rows]
        out_specs += [pl.BlockSpec(a.shape, lambda i: (0, 0)) for a in pars]
        res = pl.pallas_call(
            body,
            name=name + "_bwd",
            grid=(r // tm,),
            in_specs=in_specs,
            out_specs=out_specs,
            out_shape=out_shape,
            compiler_params=_params("arbitrary" if pars else "parallel"),
        )(*args, *crow, *csum, *more)
        return tuple(res[:n_rows]), tuple(res[n_rows:])

    @jax.custom_vjp
    def op(rows, tabs, pars):
        return fwd_call(*rows, *tabs, *pars)

    op.fwd_call, op.bwd_call = fwd_call, bwd_call

    def fwd(rows, tabs, pars):
        return fwd_call(*rows, *tabs, *pars), (rows, tabs, pars)

    def bwd(res, cots):
        rows, tabs, pars = res
        drows, dpars = bwd_call(tuple(rows) + tuple(tabs) + tuple(pars), cots)
        return drows, tuple(jnp.zeros_like(t) for t in tabs), dpars

    op.defvjp(fwd, bwd)
    return op


def _sigmoid(x):
    return 0.5 * (jnp.tanh(0.5 * x) + 1.0)


@jax.custom_jvp
def _softplus(x):
    e = jnp.exp(-jnp.abs(x))
    u = 1.0 + e
    log1p_e = jnp.where(u == 1.0, e, e * jnp.log(u) / jnp.where(u == 1.0, 1.0, u - 1.0))
    return jnp.maximum(x, 0.0) + log1p_e


@_softplus.defjvp
def _softplus_jvp(primals, tangents):
    (x,), (t,) = primals, tangents
    return _softplus(x), t * _sigmoid(x)


def _gelu(x):
    return 0.5 * x * (1.0 + jnp.tanh(math.sqrt(2.0 / math.pi) * (x + 0.044715 * (x * x * x))))


def _ln_res_f(h, mix, g, b):
    z = DN_ALPHA * h + mix
    mu = jnp.mean(z, axis=-1, keepdims=True)
    zc = z - mu
    var = jnp.mean(zc * zc, axis=-1, keepdims=True)
    return (zc * lax.rsqrt(var + EPS) * g + b,), ()


def _rmsnorm_f(x, g):
    return (x * lax.rsqrt(jnp.mean(x * x, axis=-1, keepdims=True) + EPS) * g,), ()


def _lru_gates_f(ga, gx, xc, b_a, b_x, lam):
    r = _sigmoid(ga + b_a)
    i = _sigmoid(gx + b_x)
    log_a = -LRU_C * r * _softplus(-lam)
    a = jnp.exp(log_a)
    one_minus_a2 = jnp.tanh(-log_a) * (jnp.exp(2.0 * log_a) + 1.0)
    return (a, jnp.sqrt(one_minus_a2) * (i * xc)), ()


def _lru_out_f(hh, p_gate):
    return (hh * _gelu(p_gate),), ()


def _rope_ret_f(q, k, cos2, sin2):
    d = cos2.shape[1]
    half = d // 2
    k_scale = d ** -0.5

    def rope(x):
        outs = []
        for h in range(x.shape[1] // d):
            xh = x[:, h * d:(h + 1) * d]
            rot = jnp.concatenate([xh[:, half:], xh[:, :half]], axis=1)
            outs.append(xh * cos2 + rot * sin2)
        return jnp.concatenate(outs, axis=1)

    return (rope(q), rope(k) * k_scale), ()


def _ret_out_f(o, g):
    d = o.shape[1] // RET_HEADS
    outs = []
    for h in range(RET_HEADS):
        oh = o[:, h * d:(h + 1) * d]
        outs.append(oh * lax.rsqrt(jnp.mean(oh * oh, axis=-1, keepdims=True) + EPS))
    y = jnp.concatenate(outs, axis=1)
    return (g * _sigmoid(g) * y,), ()


def _loss_f(y, t, mask):
    e = (y - t) * mask
    per_row = jnp.sum(e * e, axis=-1, keepdims=True) * (0.5 / y.shape[1])
    total = jnp.sum(per_row, axis=0, keepdims=True)
    return (), (jnp.broadcast_to(total, (1, LANES)),)


def _shift_down(x, s):
    if s == 0:
        return x
    t = x.shape[0]
    row = lax.broadcasted_iota(jnp.int32, x.shape, 0)
    return jnp.where(row >= s, pltpu.roll(x, s, 0), 0.0)


def _shift_up(x, s):
    if s == 0:
        return x
    t = x.shape[0]
    row = lax.broadcasted_iota(jnp.int32, x.shape, 0)
    return jnp.where(row < t - s, pltpu.roll(x, t - s, 0), 0.0)


def _conv_fwd(x, w, b, name):
    bsz, t, c = x.shape
    width = w.shape[0]

    def body(x_ref, w_ref, b_ref, y_ref):
        xv = x_ref[0]
        acc = jnp.broadcast_to(b_ref[...], xv.shape)
        for k in range(width):
            acc = acc + w_ref[k:k + 1, :] * _shift_down(xv, width - 1 - k)
        y_ref[0] = acc

    return pl.pallas_call(
        body,
        name=name,
        grid=(bsz, c // LANES),
        in_specs=[
            pl.BlockSpec((1, t, LANES), lambda i, j: (i, 0, j)),
            pl.BlockSpec((width, LANES), lambda i, j: (0, j)),
            pl.BlockSpec((1, LANES), lambda i, j: (0, j)),
        ],
        out_specs=pl.BlockSpec((1, t, LANES), lambda i, j: (i, 0, j)),
        out_shape=jax.ShapeDtypeStruct(x.shape, F32),
        compiler_params=_params("parallel", "parallel"),
    )(x, w, b)


def _conv_bwd(x, w, dy, name):
    bsz, t, c = x.shape
    width = w.shape[0]

    def body(x_ref, w_ref, dy_ref, dx_ref, dw_ref, db_ref):
        xv, g = x_ref[0], dy_ref[0]
        dx = jnp.zeros_like(xv)
        dws = []
        for k in range(width):
            s = width - 1 - k
            dx = dx + w_ref[k:k + 1, :] * _shift_up(g, s)
            dws.append(jnp.sum(g * _shift_down(xv, s), axis=0, keepdims=True))
        dx_ref[0] = dx
        dw = jnp.concatenate(dws, axis=0)
        db = jnp.sum(g, axis=0, keepdims=True)

        @pl.when(pl.program_id(1) == 0)
        def _():
            dw_ref[...] = dw
            db_ref[...] = db

        @pl.when(pl.program_id(1) != 0)
        def _():
            dw_ref[...] += dw
            db_ref[...] += db

    return pl.pallas_call(
        body,
        name=name,
        grid=(c // LANES, bsz),
        in_specs=[
            pl.BlockSpec((1, t, LANES), lambda j, i: (i, 0, j)),
            pl.BlockSpec((width, LANES), lambda j, i: (0, j)),
            pl.BlockSpec((1, t, LANES), lambda j, i: (i, 0, j)),
        ],
        out_specs=[
            pl.BlockSpec((1, t, LANES), lambda j, i: (i, 0, j)),
            pl.BlockSpec((width, LANES), lambda j, i: (0, j)),
            pl.BlockSpec((1, LANES), lambda j, i: (0, j)),
        ],
        out_shape=[
            jax.ShapeDtypeStruct(x.shape, F32),
            jax.ShapeDtypeStruct(w.shape, F32),
            jax.ShapeDtypeStruct((1, c), F32),
        ],
        compiler_params=_params("parallel", "arbitrary"),
    )(x, w, dy)


def _make_conv(name):
    @jax.custom_vjp
    def op(x, w, b):
        return _conv_fwd(x, w, b, name + "_fwd")

    def fwd(x, w, b):
        return op(x, w, b), (x, w)

    def bwd(res, dy):
        x, w = res
        return tuple(_conv_bwd(x, w, dy, name + "_bwd"))

    op.defvjp(fwd, bwd)
    return op


def _scan_fwd(a, b, name):
    bsz, t, c = a.shape
    cw = _pick(c, (4 * LANES, 2 * LANES, LANES))

    def body(a_ref, b_ref, h_ref):
        def step(i, h):
            h = a_ref[0, pl.ds(i, 1), :] * h + b_ref[0, pl.ds(i, 1), :]
            h_ref[0, pl.ds(i, 1), :] = h
            return h

        lax.fori_loop(0, t, step, jnp.zeros((1, cw), F32), unroll=8)

    spec = pl.BlockSpec((1, t, cw), lambda i, j: (i, 0, j))
    return pl.pallas_call(
        body,
        name=name,
        grid=(bsz, c // cw),
        in_specs=[spec, spec],
        out_specs=spec,
        out_shape=jax.ShapeDtypeStruct(a.shape, F32),
        compiler_params=_params("parallel", "parallel"),
    )(a, b)


def _scan_bwd(a, h, g, name):
    bsz, t, c = a.shape
    cw = _pick(c, (2 * LANES, LANES))

    def body(a_ref, h_ref, g_ref, da_ref, db_ref):
        def step(n, carry):
            i = t - 1 - n
            lam = g_ref[0, pl.ds(i, 1), :] + carry
            db_ref[0, pl.ds(i, 1), :] = lam
            prev = jnp.where(i > 0, h_ref[0, pl.ds(jnp.maximum(i - 1, 0), 1), :], 0.0)
            da_ref[0, pl.ds(i, 1), :] = lam * prev
            return a_ref[0, pl.ds(i, 1), :] * lam

        lax.fori_loop(0, t, step, jnp.zeros((1, cw), F32), unroll=8)

    spec = pl.BlockSpec((1, t, cw), lambda i, j: (i, 0, j))
    return pl.pallas_call(
        body,
        name=name,
        grid=(bsz, c // cw),
        in_specs=[spec, spec, spec],
        out_specs=[spec, spec],
        out_shape=[jax.ShapeDtypeStruct(a.shape, F32)] * 2,
        compiler_params=_params("parallel", "parallel"),
    )(a, h, g)


def _make_scan(name):
    @jax.custom_vjp
    def op(a, b):
        return _scan_fwd(a, b, name + "_fwd")

    def fwd(a, b):
        h = op(a, b)
        return h, (a, h)

    def bwd(res, g):
        a, h = res
        da, db = _scan_bwd(a, h, g, name + "_bwd")
        return da, db

    op.defvjp(fwd, bwd)
    return op


def _query_blocks(t):
    blocks, start = [], 0
    while start < t:
        rows = 2 * SEQ_BLOCK if start + 2 * SEQ_BLOCK <= t else SEQ_BLOCK
        blocks.append((start, rows))
        start += rows
    return blocks


def _attn_probs(q, k, start, scale):
    tq, tk = q.shape[0], k.shape[0]
    s = _dot(q, k, "nt") * scale
    qpos = start + lax.broadcasted_iota(jnp.int32, (tq, tk), 0)
    kpos = lax.broadcasted_iota(jnp.int32, (tq, tk), 1)
    s = jnp.where(kpos <= qpos, s, NEG_INF)
    e = jnp.exp(s - jnp.max(s, axis=-1, keepdims=True))
    return e / jnp.sum(e, axis=-1, keepdims=True)


_MLA_SCALE = (MLA_NOPE + MLA_ROPE) ** -0.5


def _attn_specs(t):
    head = pl.BlockSpec((1, t, LANES), lambda b, h: (b, 0, h))
    shared = pl.BlockSpec((1, t, LANES), lambda b, h: (b, 0, 0))
    return head, shared


def _attn_fwd(q, kv, kpe, name):
    bsz, t, hl = q.shape
    head, shared = _attn_specs(t)

    def body(q_ref, kv_ref, kpe_ref, o_ref, k_s, v_s):
        lane = lax.broadcasted_iota(jnp.int32, (t, LANES), 1)
        kvh = kv_ref[0]
        k_s[...] = jnp.where(lane < MLA_NOPE, kvh, kpe_ref[0]).astype(BF16)
        v_s[...] = kvh.astype(BF16)
        for start, rows in _query_blocks(t):
            n = start + rows
            p = _attn_probs(q_ref[0, start:n, :], k_s[:n, :], start, _MLA_SCALE)
            o_ref[0, start:n, :] = _dot(p, v_s[:n, :], "nn")

    return pl.pallas_call(
        body,
        name=name,
        grid=(bsz, hl // LANES),
        in_specs=[head, head, shared],
        out_specs=head,
        out_shape=jax.ShapeDtypeStruct(q.shape, F32),
        scratch_shapes=[pltpu.VMEM((t, LANES), BF16), pltpu.VMEM((t, LANES), BF16)],
        compiler_params=_params("parallel", "parallel"),
    )(q, kv, kpe)


def _attn_bwd(q, kv, kpe, do, name):
    bsz, t, hl = q.shape
    head, shared = _attn_specs(t)

    def body(q_ref, kv_ref, kpe_ref, do_ref, dq_ref, dkv_ref, dkpe_ref, k_s, v_s, dk_s, dv_s):
        lane = lax.broadcasted_iota(jnp.int32, (t, LANES), 1)
        kvh = kv_ref[0]
        k_s[...] = jnp.where(lane < MLA_NOPE, kvh, kpe_ref[0]).astype(BF16)
        v_s[...] = kvh.astype(BF16)
        for start, rows in reversed(_query_blocks(t)):
            n = start + rows
            qb = q_ref[0, start:n, :]
            dob = jnp.where(lane[:rows] >= MLA_NOPE, do_ref[0, start:n, :], 0.0)
            kk, vv = k_s[:n, :], v_s[:n, :]
            p = _attn_probs(qb, kk, start, _MLA_SCALE)
            dp = _dot(dob, vv, "nt")
            ds = p * (dp - jnp.sum(dp * p, axis=-1, keepdims=True)) * _MLA_SCALE
            dq_ref[0, start:n, :] = _dot(ds, kk, "nn")
            if n == t:
                dk_s[...] = _dot(ds, qb, "tn")
                dv_s[...] = _dot(p, dob, "tn")
            else:
                dk_s[:n, :] += _dot(ds, qb, "tn")
                dv_s[:n, :] += _dot(p, dob, "tn")
        dk = dk_s[...]
        dkv_ref[0] = jnp.where(lane < MLA_NOPE, dk, dv_s[...])
        dkpe = jnp.where(lane >= MLA_NOPE, dk, 0.0)

        @pl.when(pl.program_id(1) == 0)
        def _():
            dkpe_ref[0] = dkpe

        @pl.when(pl.program_id(1) != 0)
        def _():
            dkpe_ref[0] += dkpe

    return pl.pallas_call(
        body,
        name=name,
        grid=(bsz, hl // LANES),
        in_specs=[head, head, shared, head],
        out_specs=[head, head, shared],
        out_shape=[
            jax.ShapeDtypeStruct(q.shape, F32),
            jax.ShapeDtypeStruct(kv.shape, F32),
            jax.ShapeDtypeStruct(kpe.shape, F32),
        ],
        scratch_shapes=[pltpu.VMEM((t, LANES), BF16), pltpu.VMEM((t, LANES), BF16),
                        pltpu.VMEM((t, LANES), F32), pltpu.VMEM((t, LANES), F32)],
        compiler_params=_params("parallel", "arbitrary"),
    )(q, kv, kpe, do)


def _make_attention(name):
    @jax.custom_vjp
    def op(q, kv, kpe):
        return _attn_fwd(q, kv, kpe, name + "_fwd")

    def fwd(q, kv, kpe):
        return op(q, kv, kpe), (q, kv, kpe)

    def bwd(res, do):
        return tuple(_attn_bwd(*res, do, name + "_bwd"))

    op.defvjp(fwd, bwd)
    return op


_ROPE_SHIFT = MLA_ROPE // 2


def _rope_lanes_call(x, c, sm, sp, transpose, name):
    r, width = x.shape
    tm = _row_tile(r, width)

    def body(x_ref, c_ref, sm_ref, sp_ref, y_ref):
        cv, smv, spv = c_ref[...], sm_ref[...], sp_ref[...]
        for b in range(width // LANES):
            xb = x_ref[:, b * LANES:(b + 1) * LANES]
            if transpose:
                yb = xb * cv + pltpu.roll(xb * smv, _ROPE_SHIFT, 1) + pltpu.roll(xb * spv, LANES - _ROPE_SHIFT, 1)
            else:
                yb = xb * cv + pltpu.roll(xb, LANES - _ROPE_SHIFT, 1) * smv + pltpu.roll(xb, _ROPE_SHIFT, 1) * spv
            y_ref[:, b * LANES:(b + 1) * LANES] = yb

    tab = pl.BlockSpec((tm, LANES), lambda i: (i, 0))
    blk = pl.BlockSpec((tm, width), lambda i: (i, 0))
    return pl.pallas_call(
        body,
        name=name,
        grid=(r // tm,),
        in_specs=[blk, tab, tab, tab],
        out_specs=blk,
        out_shape=jax.ShapeDtypeStruct(x.shape, F32),
        compiler_params=_params("parallel"),
    )(x, c, sm, sp)


def _make_rope_lanes(name):
    @jax.custom_vjp
    def op(x, c, sm, sp):
        return _rope_lanes_call(x, c, sm, sp, False, name + "_fwd")

    def fwd(x, c, sm, sp):
        return op(x, c, sm, sp), (c, sm, sp)

    def bwd(res, dy):
        c, sm, sp = res
        return _rope_lanes_call(dy, c, sm, sp, True, name + "_bwd"), jnp.zeros_like(c), jnp.zeros_like(sm), jnp.zeros_like(sp)

    op.defvjp(fwd, bwd)
    return op


RET_KEY_CHUNK = 512


def _key_chunks(t):
    return [(c, min(RET_KEY_CHUNK, t - c)) for c in range(0, t, RET_KEY_CHUNK)]


def _decay(q0, k0, tq, tk, log_gamma):
    row = lax.broadcasted_iota(jnp.int32, (tq, 1), 0)
    col = (q0 - k0) - lax.broadcasted_iota(jnp.int32, (1, tk), 1)
    outer = jnp.exp(log_gamma * row.astype(F32)) * jnp.exp(log_gamma * col.astype(F32))
    return jnp.where(row + col >= 0, outer, 0.0)


def _ret_query_rows(t):
    return t // 4 if t % 32 == 0 else SEQ_BLOCK


def _ret_specs(t, dk, dv, heads):
    tq = _ret_query_rows(t)
    return (
        pl.BlockSpec(memory_space=pltpu.SMEM),
        pl.BlockSpec((1, tq, dk), lambda b, h, i: (b, i, h)),
        pl.BlockSpec((1, t, dk), lambda b, h, i: (b, 0, h)),
        pl.BlockSpec((1, t, dv), lambda b, h, i: (b, 0, h)),
        pl.BlockSpec((1, tq, dv), lambda b, h, i: (b, i, h)),
    )


def _ret_fwd(lg, q, k, v, name):
    bsz, t, hdk = q.shape
    heads = lg.shape[0]
    dk, dv = hdk // heads, v.shape[2] // heads
    lg_spec, q_spec, k_spec, v_spec, o_spec = _ret_specs(t, dk, dv, heads)
    tq = _ret_query_rows(t)

    def body(lg_ref, q_ref, k_ref, v_ref, o_ref):
        q0 = pl.program_id(2) * tq
        lgh = lg_ref[pl.program_id(1)]
        o_ref[0] = jnp.zeros((tq, dv), F32)
        for c0, cw in _key_chunks(t):
            @pl.when(c0 < q0 + tq)
            def _(c0=c0, cw=cw):
                d = _decay(q0, c0, tq, cw, lgh)
                a = _dot(q_ref[0], k_ref[0, c0:c0 + cw, :], "nt") * d
                o_ref[0] += _dot(a, v_ref[0, c0:c0 + cw, :], "nn")

    return pl.pallas_call(
        body,
        name=name,
        grid=(bsz, heads, t // tq),
        in_specs=[lg_spec, q_spec, k_spec, v_spec],
        out_specs=o_spec,
        out_shape=jax.ShapeDtypeStruct(v.shape, F32),
        compiler_params=_params("parallel", "parallel", "parallel"),
    )(lg, q, k, v)


def _ret_bwd(lg, q, k, v, do, name):
    bsz, t, hdk = q.shape
    heads = lg.shape[0]
    dk, dv = hdk // heads, v.shape[2] // heads
    lg_spec, q_spec, k_spec, v_spec, o_spec = _ret_specs(t, dk, dv, heads)
    tq = _ret_query_rows(t)

    def body(lg_ref, q_ref, k_ref, v_ref, do_ref, dq_ref, dk_ref, dv_ref):
        q0 = pl.program_id(2) * tq
        lgh = lg_ref[pl.program_id(1)]

        @pl.when(pl.program_id(2) == 0)
        def _():
            dk_ref[0] = jnp.zeros((t, dk), F32)
            dv_ref[0] = jnp.zeros((t, dv), F32)

        dq_ref[0] = jnp.zeros((tq, dk), F32)
        for c0, cw in _key_chunks(t):
            @pl.when(c0 < q0 + tq)
            def _(c0=c0, cw=cw):
                qb, dob = q_ref[0], do_ref[0]
                kk, vv = k_ref[0, c0:c0 + cw, :], v_ref[0, c0:c0 + cw, :]
                d = _decay(q0, c0, tq, cw, lgh)
                a = _dot(qb, kk, "nt") * d
                ds = _dot(dob, vv, "nt") * d
                dq_ref[0] += _dot(ds, kk, "nn")
                dk_ref[0, c0:c0 + cw, :] += _dot(ds, qb, "tn")
                dv_ref[0, c0:c0 + cw, :] += _dot(a, dob, "tn")

    return pl.pallas_call(
        body,
        name=name,
        grid=(bsz, heads, t // tq),
        in_specs=[lg_spec, q_spec, k_spec, v_spec, o_spec],
        out_specs=[q_spec, k_spec, v_spec],
        out_shape=[
            jax.ShapeDtypeStruct(q.shape, F32),
            jax.ShapeDtypeStruct(k.shape, F32),
            jax.ShapeDtypeStruct(v.shape, F32),
        ],
        compiler_params=_params("parallel", "parallel", "arbitrary"),
    )(lg, q, k, v, do)


def _make_retention(name):
    @jax.custom_vjp
    def op(lg, q, k, v):
        return _ret_fwd(lg, q, k, v, name + "_fwd")

    def fwd(lg, q, k, v):
        return op(lg, q, k, v), (lg, q, k, v)

    def bwd(res, do):
        lg = res[0]
        return (jnp.zeros_like(lg),) + tuple(_ret_bwd(*res, do, name + "_bwd"))

    op.defvjp(fwd, bwd)
    return op


def _adamw(w, g, m, v, name):
    r, c = w.shape
    tr = _pick(r, (256, 128, 64, 32, 16, 8))

    def body(w_ref, g_ref, m_ref, v_ref, d_ref, nm_ref, nv_ref):
        gv = g_ref[...]
        nm = ADAM_B1 * m_ref[...] + (1.0 - ADAM_B1) * gv
        nv = ADAM_B2 * v_ref[...] + (1.0 - ADAM_B2) * (gv * gv)
        m_hat = nm / (1.0 - ADAM_B1 ** ADAM_STEP)
        v_hat = nv / (1.0 - ADAM_B2 ** ADAM_STEP)
        d_ref[...] = -ADAM_LR * (m_hat / (jnp.sqrt(v_hat) + ADAM_EPS) + ADAM_WD * w_ref[...])
        nm_ref[...] = nm
        nv_ref[...] = nv

    spec = pl.BlockSpec((tr, c), lambda i: (i, 0))
    return pl.pallas_call(
        body,
        name=name,
        grid=(r // tr,),
        in_specs=[spec] * 4,
        out_specs=[spec] * 3,
        out_shape=[jax.ShapeDtypeStruct((r, c), F32)] * 3,
        compiler_params=_params("parallel"),
    )(w, g, m, v)


def _rope_tables(t, half, reps):
    inv = ROPE_BASE ** (-jnp.arange(half, dtype=F32) / half)
    ang = jnp.arange(t, dtype=jnp.int32).astype(F32)[:, None] * inv[None, :]
    return jnp.tile(jnp.cos(ang), (1, reps)), jnp.tile(jnp.sin(ang), (1, reps))


def _padded_len(seq):
    return -(-(N_META + seq) // SEQ_BLOCK) * SEQ_BLOCK


def _embed(meta, x):
    bsz, seq, d = x.shape
    t = _padded_len(seq)
    return jnp.concatenate(
        [jnp.broadcast_to(meta[None], (bsz, N_META, d)), x, jnp.zeros((bsz, t - N_META - seq, d), F32)], axis=1
    ).reshape(bsz * t, d)


def _even_mixer(p, conv_w_shard, conv_b, w_rg_a, b_rg_a, w_rg_x, b_rg_x, lru_lambda, q_norm_g, w_uq_shard,
                kv_norm_g, w_ukv_shard, gathered, bsz):
    conv_w_full, w_uq_full, w_ukv_full = gathered
    r = p.shape[0]
    t = r // bsz

    def tile_rows(tab):
        return jnp.tile(tab, (bsz, 1))

    lru_w = w_rg_a.shape[2] * w_rg_a.shape[1]
    q_rank, kv_rank = q_norm_g.shape[1], kv_norm_g.shape[1]
    p_gate, p_rec, p_q, p_kv, p_kpe = _split_cols(
        p, (lru_w, 2 * lru_w, 2 * lru_w + q_rank, 2 * lru_w + q_rank + kv_rank))

    conv_w = _make_fsdp_param("conv_w")(conv_w_shard[0], conv_w_full)
    xc = _make_conv("conv")(p_rec.reshape(bsz, t, lru_w), conv_w, conv_b).reshape(r, lru_w)
    ga = _make_group_linear("rg_a")(xc, w_rg_a[0])
    gx = _make_group_linear("rg_x")(xc, w_rg_x[0])
    (a, bb), _ = _make_rowwise(_lru_gates_f, "lru_gates", 3, 0, 3)((ga, gx, xc), (), (b_rg_a, b_rg_x, lru_lambda))
    hh = _make_scan("lru_scan")(a.reshape(bsz, t, lru_w), bb.reshape(bsz, t, lru_w)).reshape(r, lru_w)
    (y_rec,), _ = _make_rowwise(_lru_out_f, "lru_out", 2, 0, 0)((hh, p_gate), (), ())

    (qn,), _ = _make_rowwise(_rmsnorm_f, "q_norm", 1, 0, 1)((p_q,), (), (q_norm_g,))
    (kvn,), _ = _make_rowwise(_rmsnorm_f, "kv_norm", 1, 0, 1)((p_kv,), (), (kv_norm_g,))
    d_head = MLA_NOPE + MLA_ROPE
    w_uq_pad = jnp.pad(w_uq_full.reshape(q_rank, MLA_HEADS, d_head), ((0, 0), (0, 0), (0, LANES - d_head)))

    def unpad_uq(dw):
        return dw.reshape(q_rank, MLA_HEADS, LANES)[:, :, :d_head].reshape(q_rank, MLA_HEADS * d_head)

    q = _make_fsdp_linear(True, "ev_uq", unpad_uq)(qn, w_uq_shard[0], w_uq_pad.reshape(q_rank, MLA_HEADS * LANES))
    kv = _make_fsdp_linear(True, "ev_ukv")(kvn, w_ukv_shard[0], w_ukv_full)
    half = MLA_ROPE // 2
    cos, sin = _rope_tables(t, half, 1)
    one, zero = jnp.ones((t, MLA_NOPE), F32), jnp.zeros((t, MLA_NOPE), F32)
    tail = LANES - MLA_NOPE - MLA_ROPE
    c_tab = tile_rows(jnp.concatenate([one, cos, cos, one[:, :tail]], axis=1))
    sm_tab = tile_rows(jnp.concatenate([zero, -sin, zero[:, :half + tail]], axis=1))
    sp_tab = tile_rows(jnp.concatenate([zero, zero[:, :half], sin, zero[:, :tail]], axis=1))
    q = _make_rope_lanes("rope_q")(q, c_tab, sm_tab, sp_tab)
    kpe = _make_rope_lanes("rope_k")(p_kpe, c_tab, sm_tab, sp_tab)
    o = _make_attention("mla")(q.reshape(bsz, t, -1), kv.reshape(bsz, t, -1), kpe.reshape(bsz, t, LANES))
    return jnp.concatenate([y_rec, o.reshape(r, -1)], axis=1)


def _odd_mixer(p, bsz):
    r = p.shape[0]
    t = r // bsz

    def tile_rows(tab):
        return jnp.tile(tab, (bsz, 1))

    qk = p.shape[1] // 6
    dk = qk // RET_HEADS
    cos2, sin2 = _rope_tables(t, dk // 2, 2)
    sin2 = jnp.concatenate([-sin2[:, :dk // 2], sin2[:, dk // 2:]], axis=1)
    p_q, p_k, p_v, p_g = _split_cols(p, (qk, 2 * qk, 4 * qk))
    (rq, rk), _ = _make_rowwise(_rope_ret_f, "rope_ret", 2, 2, 0)((p_q, p_k), (tile_rows(cos2), tile_rows(sin2)), ())
    lg = jnp.log(1.0 - 2.0 ** (-5.0 - jnp.arange(RET_HEADS, dtype=F32)))
    o = _make_retention("ret")(lg, rq.reshape(bsz, t, qk), rk.reshape(bsz, t, qk), p_v.reshape(bsz, t, 2 * qk))
    (y,), _ = _make_rowwise(_ret_out_f, "ret_out", 2, 0, 0)((o.reshape(r, 2 * qk), p_g), (), ())
    return y


def _local_loss(h, target):
    bsz, seq, d = target.shape
    t = _padded_len(seq)
    t_real = N_META + seq
    pos = jnp.arange(t, dtype=jnp.int32)
    mask = jnp.tile(((pos >= N_META) & (pos < t_real)).astype(F32)[:, None], (bsz, 1))
    tgt = jnp.concatenate(
        [jnp.zeros((bsz, N_META, d), F32), target, jnp.zeros((bsz, t - t_real, d), F32)], axis=1).reshape(bsz * t, d)
    _, (total,) = _make_rowwise(_loss_f, "loss", 1, 2, 0)((h,), (tgt, mask), ())
    return total[0, 0]


_WEIGHTS = ("meta_tokens", "ev_w_in", "ev_conv_w", "ev_conv_b", "ev_w_rg_a", "ev_b_rg_a", "ev_w_rg_x", "ev_b_rg_x",
            "ev_lru_lambda", "ev_q_norm_g", "ev_w_uq", "ev_kv_norm_g", "ev_w_ukv", "ev_w_out", "od_w_in", "od_w_out",
            "ln_mix_g", "ln_mix_b", "mlp_w1", "mlp_w2", "ln_mlp_g", "ln_mlp_b")


def kernel(x, meta_tokens, ev_w_in, ev_conv_w, ev_conv_b, ev_w_rg_a, ev_b_rg_a, ev_w_rg_x, ev_b_rg_x, ev_lru_lambda, ev_q_norm_g, ev_w_uq, ev_kv_norm_g, ev_w_ukv, ev_w_out, od_w_in, od_w_out, ln_mix_g, ln_mix_b, mlp_w1, mlp_w2, ln_mlp_g, ln_mlp_b, loss_target, m_meta_tokens, m_ev_w_in, m_ev_conv_w, m_ev_conv_b, m_ev_w_rg_a, m_ev_b_rg_a, m_ev_w_rg_x, m_ev_b_rg_x, m_ev_lru_lambda, m_ev_q_norm_g, m_ev_w_uq, m_ev_kv_norm_g, m_ev_w_ukv, m_ev_w_out, m_od_w_in, m_od_w_out, m_ln_mix_g, m_ln_mix_b, m_mlp_w1, m_mlp_w2, m_ln_mlp_g, m_ln_mlp_b, v_meta_tokens, v_ev_w_in, v_ev_conv_w, v_ev_conv_b, v_ev_w_rg_a, v_ev_b_rg_a, v_ev_w_rg_x, v_ev_b_rg_x, v_ev_lru_lambda, v_ev_q_norm_g, v_ev_w_uq, v_ev_kv_norm_g, v_ev_w_ukv, v_ev_w_out, v_od_w_in, v_od_w_out, v_ln_mix_g, v_ln_mix_b, v_mlp_w1, v_mlp_w2, v_ln_mlp_g, v_ln_mlp_b):
    args = locals()
    weights = {n: args[n] for n in _WEIGHTS}
    bsz = x.shape[0]
    my_x, my_y, my_c = _my_place()
    me = 4 * my_x + 2 * my_y + my_c

    big = (("ev_in", ev_w_in[0], True), ("ev_out", ev_w_out[0], False), ("mlp0_w1", mlp_w1[0], True),
           ("mlp0_w2", mlp_w2[0], False), ("od_in", od_w_in[0], True), ("od_out", od_w_out[0], False),
           ("mlp1_w1", mlp_w1[1], True), ("mlp1_w2", mlp_w2[1], False))
    small_sharded = (("meta", meta_tokens, F32), ("conv_w", ev_conv_w[0], F32), ("ev_uq", ev_w_uq[0], BF16),
                     ("ev_ukv", ev_w_ukv[0], BF16))
    to_gather = (tuple((nm, s.astype(dt), True) for nm, s, dt in small_sharded)
                 + tuple((nm, s.astype(BF16), cols) for nm, s, cols in big))
    handles = _gather_start_all([s for _, s, _ in to_gather], "ag_start")
    gathers = {nm: (s, cols, h) for (nm, s, cols), h in zip(to_gather, handles)}
    gather_tokens = (handles[0][4],)

    def full_weight(nm, after):
        shard16, cols, handle = gathers[nm]
        land = _exchange_wait(handle, True, after, "ag_wait_" + nm)
        land = lax.dynamic_update_index_in_dim(land, shard16, me, 0)
        if cols and shard16.shape[1] % LANES == 0:
            return land, True
        return (_unstack_cols(land) if cols else land.reshape(-1, shard16.shape[1])), False

    meta_full, conv_w_full, w_uq_full, w_ukv_full = (
        _unstack_cols(lax.dynamic_update_index_in_dim(
            _exchange_wait(gathers[nm][2], True, gather_tokens[-1], "ag_wait_" + nm), gathers[nm][0], me, 0))
        for nm, _, _ in small_sharded)

    pending = []

    def linear_bwd(nm, x_in, w, dy, cols, unpad=None, **fused):
        w_full, w_stacked = w
        a_relu2 = fused.pop("a_relu2", False)
        if w_stacked:
            stacked = _matmul(x_in, dy, "tn", nm + "_dw", stacked=True, a_relu2=a_relu2)
            own = lax.dynamic_index_in_dim(stacked, me, 0, keepdims=False)
        else:
            dw = _matmul(x_in, dy, "tn", nm + "_dw", a_relu2=a_relu2)
            dw = dw if unpad is None else unpad(dw)
            n = dw.shape[1] // N_DEV
            if cols:
                stacked = _stack_cols(dw)
                own = lax.dynamic_slice_in_dim(dw, me * n, n, axis=1)
            else:
                stacked = dw.reshape(N_DEV, dw.shape[0] // N_DEV, dw.shape[1])
                own = lax.dynamic_index_in_dim(stacked, me, 0, keepdims=False)
        handle = _exchange_start(stacked, (_N_PEERS,) + stacked.shape[1:], False, "rs_start_" + nm)
        pending.append((nm, own, handle))
        return _matmul(dy, w_full, "nt", nm + "_dx", after=(handle[4],), stacked=w_stacked, **fused)

    def linear_fwd(nm, x_in, w, **fused):
        return _matmul(x_in, w[0], "nn", nm + "_fwd", stacked=w[1], **fused)

    def mlp_fwd(h, l):
        w1 = full_weight(f"mlp{l}_w1", h)
        u = linear_fwd(f"mlp{l}_w1", h, w1)
        w2 = full_weight(f"mlp{l}_w2", u)
        f = linear_fwd(f"mlp{l}_w2", u, w2, a_relu2=True)
        ln_args = (h, f, ln_mlp_g[l:l + 1], ln_mlp_b[l:l + 1])
        return ln_fwd(f"mlp{l}_ln", *ln_args), (h, w1, u, w2, ln_args)

    def mlp_bwd(dout, res, l):
        h, w1, u, w2, ln_args = res
        dh, df, dg, db = ln_bwd(f"mlp{l}_ln", ln_args, dout)
        du = linear_bwd(f"mlp{l}_w2", u, w2, df, False, a_relu2=True, relu2_bwd_of=u)
        return (dh, linear_bwd(f"mlp{l}_w1", h, w1, du, True)), dg, db

    def ln_fwd(nm, h, mix, g, b):
        return _make_rowwise(_ln_res_f, nm, 2, 0, 2).fwd_call(h, mix, g, b)[0][0]

    def ln_bwd(nm, ln_args, pieces):
        (dh, dmix), (dg, db) = _make_rowwise(_ln_res_f, nm, 2, 0, 2).bwd_call(
            ln_args, ((pieces[0],), ()), more=tuple(pieces[1:]))
        return dh, dmix, dg, db

    h0, vjp_embed = jax.vjp(_embed, meta_full, x)
    n_in = ev_w_in.shape[2] * N_DEV
    kpe0, pad_lo, pad_hi = n_in - MLA_ROPE, MLA_NOPE, LANES - MLA_NOPE - MLA_ROPE
    w_in = full_weight("ev_in", h0)[0]
    zeros_in = jnp.zeros((w_in.shape[0], pad_lo), BF16)
    w_ev_in = (jnp.concatenate([w_in[:, :kpe0], zeros_in, w_in[:, kpe0:], zeros_in[:, :pad_hi]], axis=1), False)

    def unpad_in(dw):
        return jnp.concatenate([dw[:, :kpe0], dw[:, kpe0 + pad_lo:kpe0 + pad_lo + MLA_ROPE]], axis=1)

    p0 = _matmul(h0, w_ev_in[0], "nn", "ev_in_fwd", after=gather_tokens)
    small = (ev_conv_w, ev_conv_b, ev_w_rg_a, ev_b_rg_a, ev_w_rg_x, ev_b_rg_x, ev_lru_lambda, ev_q_norm_g, ev_w_uq,
             ev_kv_norm_g, ev_w_ukv)
    y0, vjp_even = jax.vjp(lambda p, *s: _even_mixer(p, *s, (conv_w_full, w_uq_full, w_ukv_full), bsz), p0, *small)
    w_out = full_weight("ev_out", y0)[0]
    lru_w, d_model = y0.shape[1] - MLA_HEADS * LANES, w_out.shape[1]
    w_att = w_out[lru_w:].reshape(MLA_HEADS, MLA_V, d_model)
    w_att = jnp.concatenate([jnp.zeros((MLA_HEADS, LANES - MLA_V, d_model), BF16), w_att], axis=1)
    w_ev_out = (jnp.concatenate([w_out[:lru_w], w_att.reshape(MLA_HEADS * LANES, d_model)], axis=0), False)

    def unpad_out(dw):
        d_att = dw[lru_w:].reshape(MLA_HEADS, LANES, d_model)[:, LANES - MLA_V:].reshape(MLA_HEADS * MLA_V, d_model)
        return jnp.concatenate([dw[:lru_w], d_att], axis=0)

    mix0 = linear_fwd("ev_out", y0, w_ev_out)
    ln0_args = (h0, mix0, ln_mix_g[0:1], ln_mix_b[0:1])
    h1 = ln_fwd("mix0_ln", *ln0_args)
    h2, res_mlp0 = mlp_fwd(h1, 0)
    w_od_in = full_weight("od_in", h2)
    p1 = linear_fwd("od_in", h2, w_od_in)
    y1, vjp_odd = jax.vjp(lambda p: _odd_mixer(p, bsz), p1)
    w_od_out = full_weight("od_out", y1)
    mix1 = linear_fwd("od_out", y1, w_od_out)
    ln1_args = (h2, mix1, ln_mix_g[1:2], ln_mix_b[1:2])
    h3 = ln_fwd("mix1_ln", *ln1_args)
    h4, res_mlp1 = mlp_fwd(h3, 1)
    loss_local, vjp_loss = jax.vjp(lambda h: _local_loss(h, loss_target), h4)

    dh4 = vjp_loss(jnp.ones((), F32))
    dh3, dg_mlp1, db_mlp1 = mlp_bwd(dh4, res_mlp1, 1)
    dh2, dmix1, dg_mix1, db_mix1 = ln_bwd("mix1_ln", ln1_args, dh3)
    (dp1,) = vjp_odd(linear_bwd("od_out", y1, w_od_out, dmix1, False))
    dh2 = (dh2, linear_bwd("od_in", h2, w_od_in, dp1, True))
    dh1, dg_mlp0, db_mlp0 = mlp_bwd(dh2, res_mlp0, 0)
    dh0, dmix0, dg_mix0, db_mix0 = ln_bwd("mix0_ln", ln0_args, dh1)
    dp0, *dsmall = vjp_even(linear_bwd("ev_out", y0, w_ev_out, dmix0, False, unpad=unpad_out))
    dh0 = dh0 + linear_bwd("ev_in", h0, w_ev_in, dp0, True, unpad=unpad_in)
    g_meta_full, grad_x = vjp_embed(dh0)
    (g_conv_w, g_conv_b, g_w_rg_a, g_b_rg_a, g_w_rg_x, g_b_rg_x, g_lambda, g_q_norm, g_w_uq, g_kv_norm, g_w_ukv) = dsmall

    after, summed = grad_x, {}
    for nm, own, handle in pending:
        land = _exchange_wait(handle, False, after, "rs_wait_" + nm)
        summed[nm] = after = _sum_own_and_peers(own, land, "rs_sum_" + nm)

    g_meta = _scatter_grad(g_meta_full, True, "meta_rs", after=(after,))
    rep_names = ("ev_conv_b", "ev_w_rg_a", "ev_b_rg_a", "ev_w_rg_x", "ev_b_rg_x", "ev_lru_lambda", "ev_q_norm_g",
                 "ev_kv_norm_g", "ln_mix_g", "ln_mix_b", "ln_mlp_g", "ln_mlp_b")
    rep_local = (g_conv_b, g_w_rg_a, g_b_rg_a, g_w_rg_x, g_b_rg_x, g_lambda, g_q_norm, g_kv_norm,
                 jnp.concatenate([dg_mix0, dg_mix1]), jnp.concatenate([db_mix0, db_mix1]),
                 jnp.concatenate([dg_mlp0, dg_mlp1]), jnp.concatenate([db_mlp0, db_mlp1]))
    grad_w = dict(zip(rep_names, _allreduce_replicated(rep_local, "rep", after=(g_meta,))))
    grad_w.update(meta_tokens=g_meta, ev_conv_w=g_conv_w, ev_w_uq=g_w_uq, ev_w_ukv=g_w_ukv)
    grad_w.update(ev_w_in=summed["ev_in"][None], ev_w_out=summed["ev_out"][None], od_w_in=summed["od_in"][None],
                  od_w_out=summed["od_out"][None], mlp_w1=jnp.stack([summed["mlp0_w1"], summed["mlp1_w1"]]),
                  mlp_w2=jnp.stack([summed["mlp0_w2"], summed["mlp1_w2"]]))

    loss = lax.psum(loss_local, MESH_AXES)
    delta, new_m, new_v = {}, {}, {}
    for n in _WEIGHTS:
        w, g, m, v = weights[n], grad_w[n], args["m_" + n], args["v_" + n]
        two_d = (-1, w.shape[-1])
        d2, m2, v2 = _adamw(w.reshape(two_d), g.reshape(two_d), m.reshape(two_d), v.reshape(two_d), "adamw_" + n)
        delta[n], new_m[n], new_v[n] = d2.reshape(w.shape), m2.reshape(w.shape), v2.reshape(w.shape)
    return (loss, grad_x, *[grad_w[n] for n in _WEIGHTS], *[delta[n] for n in _WEIGHTS],
            *[new_m[n] for n in _WEIGHTS], *[new_v[n] for n in _WEIGHTS])
```

```python
import functools
import math

import jax
import jax.numpy as jnp
from jax import lax
from jax.experimental import pallas as pl
from jax.experimental.pallas import tpu as pltpu

F32 = jnp.float32
BF16 = jnp.bfloat16

N_DEV = 8
MESH_AXES = ("x", "y", "c")
LANES = 128
SEQ_BLOCK = 128

N_META = 16
LRU_C = 8.0
MLA_HEADS = 8
MLA_NOPE = 64
MLA_ROPE = 32
MLA_V = 64
RET_HEADS = 4
ROPE_BASE = 10000.0
DEPTH = 2
DN_ALPHA = (2 * DEPTH) ** 0.25
EPS = 1e-5
NEG_INF = -1e30

ADAM_LR = 0.001
ADAM_B1 = 0.9
ADAM_B2 = 0.999
ADAM_EPS = 1e-08
ADAM_WD = 0.01
ADAM_STEP = 10

VMEM_LIMIT = 56 * 1024 * 1024


def _params(*sem):
    return pltpu.CompilerParams(dimension_semantics=sem, vmem_limit_bytes=VMEM_LIMIT)


def _pick(n, cands):
    for c in cands:
        if n % c == 0:
            return c
    return n


def _row_tile(r, width):
    cands = (256, 128, 64, 32, 16, 8) if width <= 1024 else (128, 64, 32, 16, 8)
    return _pick(r, cands)


_DIMS = {"nn": (((1,), (0,)), ((), ())), "nt": (((1,), (1,)), ((), ())), "tn": (((0,), (0,)), ((), ()))}


def _dot(a, b, mode):
    return lax.dot_general(a.astype(BF16), b.astype(BF16), _DIMS[mode], preferred_element_type=F32)


def _matmul(a, b, mode, name, after=(), stacked=False, a_relu2=False, relu2_bwd_of=None, out_dtype=F32):
    if stacked:
        n_blk = b.shape[2] if mode != "tn" else b.shape[1] // N_DEV
    if mode == "nn":
        (m, k), n = a.shape, (N_DEV * n_blk if stacked else b.shape[1])
    elif mode == "nt":
        (m, k), n = a.shape, (b.shape[1] if stacked else b.shape[0])
    else:
        (k, m), n = a.shape, b.shape[1]
    tm = _pick(m, (1088, 1024, 544, 512, 272, 256, 128, 64, 32, 16, 8))
    tn = _pick(n, (1024, 512, 256, 128))
    tk = _pick(k, (1088, 1024, 544, 512, 272, 256, 128))
    kb = 2
    if stacked and mode == "nn":
        tn = n_blk
    if stacked and mode == "tn":
        tn = kb * n_blk
    if stacked and mode == "nt":
        tk = kb * n_blk
    nk = k // tk
    assert out_dtype == F32 or (nk == 1 and not (stacked and mode == "tn")), "narrow results need a single k step"

    out_spec = pl.BlockSpec((tm, tn), lambda i, j, kk: (i, j))
    out_shape = jax.ShapeDtypeStruct((m, n), out_dtype)
    if mode == "nn":
        a_spec = pl.BlockSpec((tm, tk), lambda i, j, kk: (i, kk))
        b_spec = pl.BlockSpec((tk, tn), lambda i, j, kk: (kk, j))
        if stacked:
            b_spec = pl.BlockSpec((None, tk, tn), lambda i, j, kk: (j, kk, 0))
    elif mode == "nt":
        a_spec = pl.BlockSpec((tm, tk), lambda i, j, kk: (i, kk))
        b_spec = pl.BlockSpec((tn, tk), lambda i, j, kk: (j, kk))
        if stacked:
            b_spec = pl.BlockSpec((kb, tn, n_blk), lambda i, j, kk: (kk, j, 0))
    else:
        a_spec = pl.BlockSpec((tk, tm), lambda i, j, kk: (kk, i))
        b_spec = pl.BlockSpec((tk, tn), lambda i, j, kk: (kk, j))
        if stacked:
            out_spec = pl.BlockSpec((kb, tm, n_blk), lambda i, j, kk: (j, i, 0))
            out_shape = jax.ShapeDtypeStruct((N_DEV, m, n_blk), F32)
    extra = [] if relu2_bwd_of is None else [relu2_bwd_of]
    extra_specs = [pl.BlockSpec((tm, tn), lambda i, j, kk: (i, j))] * len(extra)

    def body(a_ref, b_ref, *rest):
        o_ref = rest[-1]
        kk = pl.program_id(2)
        av = a_ref[...]
        if a_relu2:
            av = jnp.maximum(av, 0.0)
            av = av * av
        if stacked and mode == "nt":
            part = _dot(av[:, :n_blk], b_ref[0], mode)
            for q in range(1, kb):
                part = part + _dot(av[:, q * n_blk:(q + 1) * n_blk], b_ref[q], mode)
        else:
            part = _dot(av, b_ref[...], mode)
        if stacked and mode == "tn":
            part = jnp.stack([part[:, q * n_blk:(q + 1) * n_blk] for q in range(kb)])
        if out_dtype != F32:
            if relu2_bwd_of is not None:
                part = part * (2.0 * jnp.maximum(rest[0][...], 0.0))
            o_ref[...] = part.astype(out_dtype)
            return

        @pl.when(kk == 0)
        def _():
            o_ref[...] = part

        @pl.when(kk != 0)
        def _():
            o_ref[...] += part

        if relu2_bwd_of is not None:
            @pl.when(kk == nk - 1)
            def _():
                o_ref[...] *= 2.0 * jnp.maximum(rest[0][...], 0.0)

    return pl.pallas_call(
        body,
        name=name,
        grid=(m // tm, n // tn, nk),
        in_specs=[a_spec, b_spec] + extra_specs + [pl.BlockSpec(memory_space=pl.ANY)] * len(after),
        out_specs=out_spec,
        out_shape=out_shape,
        compiler_params=_params("parallel", "parallel", "arbitrary"),
    )(a, b, *extra, *after)


def _group_matmul(a, w, mode, name):
    if mode in ("nn", "nt"):
        g, dk, dn = w.shape
        m = a.shape[0]
        d_in, d_out = (dk, dn) if mode == "nn" else (dn, dk)
        tm = _pick(m, (1088, 1024, 544, 512, 272, 256, 128, 64, 32, 16, 8))

        def body(a_ref, w_ref, o_ref):
            o_ref[...] = _dot(a_ref[...], w_ref[0], mode)

        return pl.pallas_call(
            body,
            name=name,
            grid=(g, m // tm),
            in_specs=[pl.BlockSpec((tm, d_in), lambda h, i: (i, h)), pl.BlockSpec((1, dk, dn), lambda h, i: (h, 0, 0))],
            out_specs=pl.BlockSpec((tm, d_out), lambda h, i: (i, h)),
            out_shape=jax.ShapeDtypeStruct((m, g * d_out), F32),
            compiler_params=_params("parallel", "parallel"),
        )(a, w)
    b = w
    m = a.shape[0]
    dk = dn = LANES
    g = a.shape[1] // dk
    tm = _pick(m, (1088, 1024, 544, 512, 272, 256, 128, 64, 32, 16, 8))

    def body(a_ref, b_ref, o_ref):
        part = _dot(a_ref[...], b_ref[...], "tn")

        @pl.when(pl.program_id(1) == 0)
        def _():
            o_ref[0] = part

        @pl.when(pl.program_id(1) != 0)
        def _():
            o_ref[0] += part

    return pl.pallas_call(
        body,
        name=name,
        grid=(g, m // tm),
        in_specs=[pl.BlockSpec((tm, dk), lambda h, i: (i, h)), pl.BlockSpec((tm, dn), lambda h, i: (i, h))],
        out_specs=pl.BlockSpec((1, dk, dn), lambda h, i: (h, 0, 0)),
        out_shape=jax.ShapeDtypeStruct((g, dk, dn), F32),
        compiler_params=_params("parallel", "arbitrary"),
    )(a, b)


def _make_group_linear(name):
    @jax.custom_vjp
    def op(x, w):
        return _group_matmul(x, w, "nn", name + "_fwd")

    def fwd(x, w):
        return op(x, w), (x, w)

    def bwd(res, dy):
        x, w = res
        return _group_matmul(dy, w, "nt", name + "_dx"), _group_matmul(x, dy, "tn", name + "_dw")

    op.defvjp(fwd, bwd)
    return op


def _my_place():
    return lax.axis_index("x"), lax.axis_index("y"), lax.axis_index("c")


def _all_gather(shard, name, after=()):
    shape, dtype = shard.shape, shard.dtype

    def body(x_ref, *rest):
        out_ref, send_sems, recv_sems, local_sem = rest[len(after):]
        x, y, c = _my_place()
        me, sibling = (x, y, c), (x, y, 1 - c)
        chips = [(1 - x, y), (x, 1 - y), (1 - x, 1 - y)]

        def slot(px, py, pc):
            return out_ref.at[4 * px + 2 * py + pc]

        def copy(k, block, to, src=None):
            return pltpu.make_async_remote_copy(
                src_ref=slot(*block) if src is None else src,
                dst_ref=slot(*block),
                send_sem=send_sems.at[k],
                recv_sem=recv_sems.at[k],
                device_id=to,
                device_id_type=pl.DeviceIdType.MESH,
            )

        mine = pltpu.make_async_copy(x_ref, slot(*me), local_sem)
        mine.start()
        first = [copy(0, me, sibling, src=x_ref)]
        first += [copy(1 + j, me, (*chip, c), src=x_ref) for j, chip in enumerate(chips)]
        for cp in first:
            cp.start()
        passed = [copy(4 + j, (*chip, c), sibling) for j, chip in enumerate(chips)]
        for j, chip in enumerate(chips):
            copy(1 + j, (*chip, c), me).wait_recv()
            passed[j].start()
        copy(0, sibling, me).wait_recv()
        for j, chip in enumerate(chips):
            copy(4 + j, (*chip, 1 - c), me).wait_recv()
        for cp in first + passed:
            cp.wait_send()
        mine.wait()

    return pl.pallas_call(
        body,
        name=name,
        out_shape=jax.ShapeDtypeStruct((N_DEV,) + shape, dtype),
        in_specs=[pl.BlockSpec(memory_space=pl.ANY)] * (1 + len(after)),
        out_specs=pl.BlockSpec(memory_space=pl.ANY),
        scratch_shapes=[pltpu.SemaphoreType.DMA((7,)), pltpu.SemaphoreType.DMA((7,)), pltpu.SemaphoreType.DMA],
    )(shard, *after)


def _all_to_all(stacked, name, after=()):
    def body(x_ref, *rest):
        out_ref, send_sems, recv_sems, local_sem = rest[len(after):]
        x, y, c = _my_place()
        me = 4 * x + 2 * y + c
        mine = pltpu.make_async_copy(x_ref.at[me], out_ref.at[me], local_sem)
        mine.start()
        copies = []
        for k in range(1, N_DEV):
            px, py, pc = x ^ ((k >> 2) & 1), y ^ ((k >> 1) & 1), c ^ (k & 1)
            peer = 4 * px + 2 * py + pc
            copies.append(
                pltpu.make_async_remote_copy(
                    src_ref=x_ref.at[peer],
                    dst_ref=out_ref.at[me],
                    send_sem=send_sems.at[k - 1],
                    recv_sem=recv_sems.at[k - 1],
                    device_id=(px, py, pc),
                    device_id_type=pl.DeviceIdType.MESH,
                )
            )
        for cp in copies:
            cp.start()
        for cp in copies:
            cp.wait_recv()
        for cp in copies:
            cp.wait_send()
        mine.wait()

    return pl.pallas_call(
        body,
        name=name,
        out_shape=jax.ShapeDtypeStruct(stacked.shape, stacked.dtype),
        in_specs=[pl.BlockSpec(memory_space=pl.ANY)] * (1 + len(after)),
        out_specs=pl.BlockSpec(memory_space=pl.ANY),
        scratch_shapes=[pltpu.SemaphoreType.DMA((7,)), pltpu.SemaphoreType.DMA((7,)), pltpu.SemaphoreType.DMA],
    )(stacked, *after)


def _sum_blocks(stacked, name):
    _, r, c = stacked.shape
    tr = _pick(r, (256, 128, 64, 32, 16, 8))

    def body(x_ref, o_ref):
        s = [x_ref[j] for j in range(N_DEV)]
        o_ref[...] = ((s[0] + s[1]) + (s[2] + s[3])) + ((s[4] + s[5]) + (s[6] + s[7]))

    return pl.pallas_call(
        body,
        name=name,
        grid=(r // tr,),
        in_specs=[pl.BlockSpec((N_DEV, tr, c), lambda i: (0, i, 0))],
        out_specs=pl.BlockSpec((tr, c), lambda i: (i, 0)),
        out_shape=jax.ShapeDtypeStruct((r, c), stacked.dtype),
        compiler_params=_params("parallel"),
    )(stacked)


def _stack_cols(full):
    k, n8 = full.shape
    return full.reshape(k, N_DEV, n8 // N_DEV).transpose(1, 0, 2)


def _unstack_cols(stacked):
    j, k, n = stacked.shape
    return stacked.transpose(1, 0, 2).reshape(k, j * n)


def _split_cols(p, cuts):
    bounds = (0,) + tuple(cuts) + (p.shape[1],)

    @jax.custom_vjp
    def op(z):
        return tuple(z[:, lo:hi] for lo, hi in zip(bounds[:-1], bounds[1:]))

    op.defvjp(lambda z: (op(z), None), lambda _, cots: (jnp.concatenate(cots, axis=1),))
    return op(p)


def _gather_weight(shard, cols, name):
    g = _all_gather(shard.astype(BF16), name)
    return _unstack_cols(g) if cols else g.reshape(-1, shard.shape[1])


def _scatter_grad(full, cols, name, after=()):
    if cols:
        st = _stack_cols(full)
    else:
        st = full.reshape(N_DEV, full.shape[0] // N_DEV, full.shape[1])
    return _sum_blocks(_all_to_all(st, name + "_a2a", after), name + "_sum")


def _make_fsdp_linear(cols, name, unpad=None):
    @jax.custom_vjp
    def op(x, w_shard, w_full):
        return _matmul(x, w_full, "nn", name + "_fwd")

    def fwd(x, w_shard, w_full):
        return op(x, w_shard, w_full), (x, w_full)

    def bwd(res, dy):
        x, w = res
        dx = _matmul(dy, w, "nt", name + "_dx")
        dw = _matmul(x, dy, "tn", name + "_dw")
        dw = dw if unpad is None else unpad(dw)
        return dx, _scatter_grad(dw, cols, name + "_rs"), jnp.zeros_like(w)

    op.defvjp(fwd, bwd)
    return op


def _make_fsdp_param(name):
    @jax.custom_vjp
    def op(shard, full):
        return full

    def fwd(shard, full):
        return full, None

    def bwd(_, g):
        return _scatter_grad(g, True, name + "_rs"), jnp.zeros_like(g)

    op.defvjp(fwd, bwd)
    return op


def _allreduce_replicated(gs, name, after=()):
    flat = jnp.concatenate([g.reshape(-1) for g in gs])
    n = flat.shape[0]
    rows = -(-n // (256 * LANES)) * 256
    packed = jnp.pad(flat, (0, rows * LANES - n)).reshape(rows, LANES)
    total = _sum_blocks(_all_gather(packed, name + "_ag", after), name + "_sum").reshape(-1)
    out, off = [], 0
    for g in gs:
        out.append(total[off:off + g.size].reshape(g.shape))
        off += g.size
    return out


_HBM = pl.BlockSpec(memory_space=pltpu.HBM)
_SEM = pl.BlockSpec(memory_space=pltpu.SEMAPHORE)
_SIDE_EFFECT = pltpu.SideEffectType.DATAFLOW_SIDE_EFFECTING
_N_PEERS = N_DEV - 1


def _peer(k):
    x, y, c = _my_place()
    return x ^ ((k >> 2) & 1), y ^ ((k >> 1) & 1), c ^ (k & 1)


def _exchange_start(src, land_shape, gather, name, after=()):
    def body(src_ref, land_ref, *rest):
        send_sems, recv_sems, src_thru, land_thru, token = rest[len(after):]
        x, y, c = _my_place()
        me = 4 * x + 2 * y + c
        for k in range(1, N_DEV):
            px, py, pc = _peer(k)
            pltpu.make_async_remote_copy(
                src_ref=src_ref if gather else src_ref.at[4 * px + 2 * py + pc],
                dst_ref=land_ref.at[me] if gather else land_ref.at[k - 1],
                send_sem=send_sems.at[k - 1],
                recv_sem=recv_sems.at[k - 1],
                device_id=(px, py, pc),
                device_id_type=pl.DeviceIdType.MESH,
            ).start()
        token[...] = jnp.zeros_like(token)

    return pl.pallas_call(
        body,
        name=name,
        out_shape=(
            pltpu.SemaphoreType.DMA((_N_PEERS,)),
            pltpu.SemaphoreType.DMA((_N_PEERS,)),
            pltpu.HBM(src.shape, src.dtype),
            pltpu.HBM(land_shape, src.dtype),
            jax.ShapeDtypeStruct((8, LANES), F32),
        ),
        in_specs=(_HBM, _HBM) + (pl.BlockSpec(memory_space=pl.ANY),) * len(after),
        out_specs=(_SEM, _SEM, _HBM, _HBM, pl.BlockSpec(memory_space=pltpu.VMEM)),
        input_output_aliases={0: 2, 1: 3},
        compiler_params=pltpu.CompilerParams(has_side_effects=_SIDE_EFFECT),
    )(pltpu.with_memory_space_constraint(src, pltpu.HBM),
      pltpu.with_memory_space_constraint(lax.empty(land_shape, src.dtype), pltpu.HBM), *after)


def _gather_start_all(shards, name):
    n = len(shards)

    def body(*refs):
        srcs, lands = refs[:n], refs[n:2 * n]
        outs = refs[2 * n:]
        send_sems, recv_sems, token = outs[:n], outs[n:2 * n], outs[-1]
        x, y, c = _my_place()
        me = 4 * x + 2 * y + c
        for i in range(n):
            for k in range(1, N_DEV):
                pltpu.make_async_remote_copy(
                    src_ref=srcs[i],
                    dst_ref=lands[i].at[me],
                    send_sem=send_sems[i].at[k - 1],
                    recv_sem=recv_sems[i].at[k - 1],
                    device_id=_peer(k),
                    device_id_type=pl.DeviceIdType.MESH,
                ).start()
        token[...] = jnp.zeros_like(token)

    lands = [(N_DEV,) + s.shape for s in shards]
    sems = tuple(pltpu.SemaphoreType.DMA((_N_PEERS,)) for _ in range(2 * n))
    res = pl.pallas_call(
        body,
        name=name,
        out_shape=sems + tuple(pltpu.HBM(s.shape, s.dtype) for s in shards)
        + tuple(pltpu.HBM(ls, s.dtype) for ls, s in zip(lands, shards)) + (jax.ShapeDtypeStruct((8, LANES), F32),),
        in_specs=(_HBM,) * (2 * n),
        out_specs=(_SEM,) * (2 * n) + (_HBM,) * (2 * n) + (pl.BlockSpec(memory_space=pltpu.VMEM),),
        input_output_aliases={i: 2 * n + i for i in range(2 * n)},
        compiler_params=pltpu.CompilerParams(has_side_effects=_SIDE_EFFECT),
    )(*[pltpu.with_memory_space_constraint(s, pltpu.HBM) for s in shards],
      *[pltpu.with_memory_space_constraint(lax.empty(ls, s.dtype), pltpu.HBM) for ls, s in zip(lands, shards)])
    return [(res[i], res[n + i], res[2 * n + i], res[3 * n + i], res[-1]) for i in range(n)]


def _exchange_wait(handle, gather, after, name):
    send_sems, recv_sems, src_thru, land_thru, _ = handle

    def body(src_ref, land_ref, send_sems, recv_sems, after_ref, src_dead, got_ref):
        for k in range(1, N_DEV):
            cp = pltpu.make_async_remote_copy(
                src_ref=src_ref if gather else src_ref.at[k],
                dst_ref=land_ref.at[k - 1],
                send_sem=send_sems.at[k - 1],
                recv_sem=recv_sems.at[k - 1],
                device_id=_peer(k),
                device_id_type=pl.DeviceIdType.MESH,
            )
            cp.wait_send()
            cp.wait_recv()

    return pl.pallas_call(
        body,
        name=name,
        out_shape=(pltpu.HBM(src_thru.shape, src_thru.dtype), pltpu.HBM(land_thru.shape, land_thru.dtype)),
        in_specs=(_HBM, _HBM, _SEM, _SEM, pl.BlockSpec(memory_space=pl.ANY)),
        out_specs=(_HBM, _HBM),
        input_output_aliases={0: 0, 1: 1},
        compiler_params=pltpu.CompilerParams(has_side_effects=_SIDE_EFFECT),
    )(src_thru, land_thru, send_sems, recv_sems, after)[1]


def _sum_own_and_peers(own, land, name):
    r, c = own.shape
    tr = _pick(r, (256, 128, 64, 32, 16, 8))

    def body(o_ref, l_ref, out_ref):
        s = [l_ref[j] for j in range(_N_PEERS)]
        out_ref[...] = ((o_ref[...] + s[0]) + (s[1] + s[2])) + ((s[3] + s[4]) + (s[5] + s[6]))

    return pl.pallas_call(
        body,
        name=name,
        grid=(r // tr,),
        in_specs=[pl.BlockSpec((tr, c), lambda i: (i, 0)), pl.BlockSpec((_N_PEERS, tr, c), lambda i: (0, i, 0))],
        out_specs=pl.BlockSpec((tr, c), lambda i: (i, 0)),
        out_shape=jax.ShapeDtypeStruct((r, c), own.dtype),
        compiler_params=_params("parallel"),
    )(own, land)


def _make_rowwise(f, name, n_rows, n_tabs, n_pars):
    n_in = n_rows + n_tabs + n_pars

    def specs(args, tm):
        blocked = [pl.BlockSpec((tm, a.shape[1]), lambda i: (i, 0)) for a in args[: n_rows + n_tabs]]
        whole = [pl.BlockSpec(a.shape, lambda i: (0, 0)) for a in args[n_rows + n_tabs:]]
        return blocked + whole

    def out_struct(args, tm):
        blk = [jax.ShapeDtypeStruct((tm, a.shape[1]), a.dtype) for a in args[: n_rows + n_tabs]]
        blk += [jax.ShapeDtypeStruct(a.shape, a.dtype) for a in args[n_rows + n_tabs:]]
        return jax.eval_shape(f, *blk)

    def fwd_call(*args):
        r = args[0].shape[0]
        tm = _row_tile(r, max(a.shape[1] for a in args[:n_rows]))
        ro, so = out_struct(args, tm)

        def body(*refs):
            vals = [x[...] for x in refs[:n_in]]
            outs = refs[n_in:]
            rv, sv = f(*vals)
            for o, v in zip(outs[: len(ro)], rv):
                o[...] = v
            for o, v in zip(outs[len(ro):], sv):
                @pl.when(pl.program_id(0) == 0)
                def _(o=o, v=v):
                    o[...] = v

                @pl.when(pl.program_id(0) != 0)
                def _(o=o, v=v):
                    o[...] += v

        out_shape = [jax.ShapeDtypeStruct((r, s.shape[1]), s.dtype) for s in ro]
        out_shape += [jax.ShapeDtypeStruct(s.shape, s.dtype) for s in so]
        out_specs = [pl.BlockSpec((tm, s.shape[1]), lambda i: (i, 0)) for s in ro]
        out_specs += [pl.BlockSpec(s.shape, lambda i: (0, 0)) for s in so]
        res = pl.pallas_call(
            body,
            name=name + "_fwd",
            grid=(r // tm,),
            in_specs=specs(args, tm),
            out_specs=out_specs,
            out_shape=out_shape,
            compiler_params=_params("arbitrary" if so else "parallel"),
        )(*args)
        return tuple(res[: len(ro)]), tuple(res[len(ro):])

    def bwd_call(args, cots, more=(), row_dtypes=None):
        r = args[0].shape[0]
        tm = _row_tile(r, max(a.shape[1] for a in args[:n_rows]))
        ro, so = out_struct(args, tm)
        crow, csum = cots
        rows, tabs, pars = args[:n_rows], args[n_rows:n_rows + n_tabs], args[n_rows + n_tabs:]
        n_c = len(crow) + len(csum)

        def body(*refs):
            vals = [x[...] for x in refs[:n_in]]
            cv = [x[...] for x in refs[n_in:n_in + n_c]]
            for x in refs[n_in + n_c:n_in + n_c + len(more)]:
                cv[0] = cv[0] + x[...]
            outs = refs[n_in + n_c + len(more):]
            tv = vals[n_rows:n_rows + n_tabs]

            def g(*dargs):
                return f(*dargs[:n_rows], *tv, *dargs[n_rows:])

            _, vjp = jax.vjp(g, *vals[:n_rows], *vals[n_rows + n_tabs:])
            d = vjp((tuple(cv[: len(crow)]), tuple(cv[len(crow):])))
            for o, v in zip(outs[:n_rows], d[:n_rows]):
                o[...] = v.astype(o.dtype)
            for o, v in zip(outs[n_rows:], d[n_rows:]):
                @pl.when(pl.program_id(0) == 0)
                def _(o=o, v=v):
                    o[...] = v

                @pl.when(pl.program_id(0) != 0)
                def _(o=o, v=v):
                    o[...] += v

        in_specs = specs(args, tm)
        in_specs += [pl.BlockSpec((tm, c.shape[1]), lambda i: (i, 0)) for c in crow]
        in_specs += [pl.BlockSpec(c.shape, lambda i: (0, 0)) for c in csum]
        in_specs += [pl.BlockSpec((tm, c.shape[1]), lambda i: (i, 0)) for c in more]
        out_shape = [jax.ShapeDtypeStruct(a.shape, dt) for a, dt in zip(rows, row_dtypes or [a.dtype for a in rows])]
        out_shape += [jax.ShapeDtypeStruct(a.shape, a.dtype) for a in pars]
        out_specs = [pl.BlockSpec((tm, a.shape[1]), lambda i: (i, 0)) for a in rows]
        out_specs += [pl.BlockSpec(a.shape, lambda i: (0, 0)) for a in pars]
        res = pl.pallas_call(
            body,
            name=name + "_bwd",
            grid=(r // tm,),
            in_specs=in_specs,
            out_specs=out_specs,
            out_shape=out_shape,
            compiler_params=_params("arbitrary" if pars else "parallel"),
        )(*args, *crow, *csum, *more)
        return tuple(res[:n_rows]), tuple(res[n_rows:])

    @jax.custom_vjp
    def op(rows, tabs, pars):
        return fwd_call(*rows, *tabs, *pars)

    op.fwd_call, op.bwd_call = fwd_call, bwd_call

    def fwd(rows, tabs, pars):
        return fwd_call(*rows, *tabs, *pars), (rows, tabs, pars)

    def bwd(res, cots):
        rows, tabs, pars = res
        drows, dpars = bwd_call(tuple(rows) + tuple(tabs) + tuple(pars), cots)
        return drows, tuple(jnp.zeros_like(t) for t in tabs), dpars

    op.defvjp(fwd, bwd)
    return op


def _sigmoid(x):
    return 0.5 * (jnp.tanh(0.5 * x) + 1.0)


@jax.custom_jvp
def _softplus(x):
    e = jnp.exp(-jnp.abs(x))
    u = 1.0 + e
    log1p_e = jnp.where(u == 1.0, e, e * jnp.log(u) / jnp.where(u == 1.0, 1.0, u - 1.0))
    return jnp.maximum(x, 0.0) + log1p_e


@_softplus.defjvp
def _softplus_jvp(primals, tangents):
    (x,), (t,) = primals, tangents
    return _softplus(x), t * _sigmoid(x)


def _gelu(x):
    return 0.5 * x * (1.0 + jnp.tanh(math.sqrt(2.0 / math.pi) * (x + 0.044715 * (x * x * x))))


def _ln_res_f(h, mix, g, b):
    z = DN_ALPHA * h + mix
    mu = jnp.mean(z, axis=-1, keepdims=True)
    zc = z - mu
    var = jnp.mean(zc * zc, axis=-1, keepdims=True)
    return (zc * lax.rsqrt(var + EPS) * g + b,), ()


def _rmsnorm_f(x, g):
    return (x * lax.rsqrt(jnp.mean(x * x, axis=-1, keepdims=True) + EPS) * g,), ()


def _lru_gates_f(ga, gx, xc, b_a, b_x, lam):
    r = _sigmoid(ga + b_a)
    i = _sigmoid(gx + b_x)
    log_a = -LRU_C * r * _softplus(-lam)
    a = jnp.exp(log_a)
    one_minus_a2 = jnp.tanh(-log_a) * (jnp.exp(2.0 * log_a) + 1.0)
    return (a, jnp.sqrt(one_minus_a2) * (i * xc)), ()


def _lru_out_f(hh, p_gate):
    return (hh * _gelu(p_gate),), ()


def _rope_ret_f(q, k, cos2, sin2):
    d = cos2.shape[1]
    half = d // 2
    k_scale = d ** -0.5

    def rope(x):
        outs = []
        for h in range(x.shape[1] // d):
            xh = x[:, h * d:(h + 1) * d]
            rot = jnp.concatenate([xh[:, half:], xh[:, :half]], axis=1)
            outs.append(xh * cos2 + rot * sin2)
        return jnp.concatenate(outs, axis=1)

    return (rope(q), rope(k) * k_scale), ()


def _ret_out_f(o, g):
    d = o.shape[1] // RET_HEADS
    outs = []
    for h in range(RET_HEADS):
        oh = o[:, h * d:(h + 1) * d]
        outs.append(oh * lax.rsqrt(jnp.mean(oh * oh, axis=-1, keepdims=True) + EPS))
    y = jnp.concatenate(outs, axis=1)
    return (g * _sigmoid(g) * y,), ()


def _loss_f(y, t, mask):
    e = (y - t) * mask
    per_row = jnp.sum(e * e, axis=-1, keepdims=True) * (0.5 / y.shape[1])
    total = jnp.sum(per_row, axis=0, keepdims=True)
    return (), (jnp.broadcast_to(total, (1, LANES)),)


def _shift_down(x, s):
    if s == 0:
        return x
    t = x.shape[0]
    row = lax.broadcasted_iota(jnp.int32, x.shape, 0)
    return jnp.where(row >= s, pltpu.roll(x, s, 0), 0.0)


def _shift_up(x, s):
    if s == 0:
        return x
    t = x.shape[0]
    row = lax.broadcasted_iota(jnp.int32, x.shape, 0)
    return jnp.where(row < t - s, pltpu.roll(x, t - s, 0), 0.0)


def _conv_fwd(x, w, b, name):
    bsz, t, c = x.shape
    width = w.shape[0]

    def body(x_ref, w_ref, b_ref, y_ref):
        xv = x_ref[0]
        acc = jnp.broadcast_to(b_ref[...], xv.shape)
        for k in range(width):
            acc = acc + w_ref[k:k + 1, :] * _shift_down(xv, width - 1 - k)
        y_ref[0] = acc

    return pl.pallas_call(
        body,
        name=name,
        grid=(bsz, c // LANES),
        in_specs=[
            pl.BlockSpec((1, t, LANES), lambda i, j: (i, 0, j)),
            pl.BlockSpec((width, LANES), lambda i, j: (0, j)),
            pl.BlockSpec((1, LANES), lambda i, j: (0, j)),
        ],
        out_specs=pl.BlockSpec((1, t, LANES), lambda i, j: (i, 0, j)),
        out_shape=jax.ShapeDtypeStruct(x.shape, F32),
        compiler_params=_params("parallel", "parallel"),
    )(x, w, b)


def _conv_bwd(x, w, dy, name):
    bsz, t, c = x.shape
    width = w.shape[0]

    def body(x_ref, w_ref, dy_ref, dx_ref, dw_ref, db_ref):
        xv, g = x_ref[0], dy_ref[0]
        dx = jnp.zeros_like(xv)
        dws = []
        for k in range(width):
            s = width - 1 - k
            dx = dx + w_ref[k:k + 1, :] * _shift_up(g, s)
            dws.append(jnp.sum(g * _shift_down(xv, s), axis=0, keepdims=True))
        dx_ref[0] = dx
        dw = jnp.concatenate(dws, axis=0)
        db = jnp.sum(g, axis=0, keepdims=True)

        @pl.when(pl.program_id(1) == 0)
        def _():
            dw_ref[...] = dw
            db_ref[...] = db

        @pl.when(pl.program_id(1) != 0)
        def _():
            dw_ref[...] += dw
            db_ref[...] += db

    return pl.pallas_call(
        body,
        name=name,
        grid=(c // LANES, bsz),
        in_specs=[
            pl.BlockSpec((1, t, LANES), lambda j, i: (i, 0, j)),
            pl.BlockSpec((width, LANES), lambda j, i: (0, j)),
            pl.BlockSpec((1, t, LANES), lambda j, i: (i, 0, j)),
        ],
        out_specs=[
            pl.BlockSpec((1, t, LANES), lambda j, i: (i, 0, j)),
            pl.BlockSpec((width, LANES), lambda j, i: (0, j)),
            pl.BlockSpec((1, LANES), lambda j, i: (0, j)),
        ],
        out_shape=[
            jax.ShapeDtypeStruct(x.shape, F32),
            jax.ShapeDtypeStruct(w.shape, F32),
            jax.ShapeDtypeStruct((1, c), F32),
        ],
        compiler_params=_params("parallel", "arbitrary"),
    )(x, w, dy)


def _make_conv(name):
    @jax.custom_vjp
    def op(x, w, b):
        return _conv_fwd(x, w, b, name + "_fwd")

    def fwd(x, w, b):
        return op(x, w, b), (x, w)

    def bwd(res, dy):
        x, w = res
        return tuple(_conv_bwd(x, w, dy, name + "_bwd"))

    op.defvjp(fwd, bwd)
    return op


_SCAN_ROWS = 8


def _scan_fwd(a, b, name):
    bsz, t, c = a.shape
    cw = _pick(c, (4 * LANES, 2 * LANES, LANES))

    def body(a_ref, b_ref, h_ref):
        row = lax.broadcasted_iota(jnp.int32, (_SCAN_ROWS, cw), 0)

        def step(i, carry):
            r0 = pl.multiple_of(i * _SCAN_ROWS, _SCAN_ROWS)
            av, bv = a_ref[0, pl.ds(r0, _SCAN_ROWS), :], b_ref[0, pl.ds(r0, _SCAN_ROWS), :]
            for s in (1, 2, 4):
                a_sh = jnp.where(row >= s, pltpu.roll(av, s, 0), 1.0)
                b_sh = jnp.where(row >= s, pltpu.roll(bv, s, 0), 0.0)
                bv = av * b_sh + bv
                av = av * a_sh
            hv = bv + av * carry
            h_ref[0, pl.ds(r0, _SCAN_ROWS), :] = hv
            return hv[_SCAN_ROWS - 1:, :]

        lax.fori_loop(0, t // _SCAN_ROWS, step, jnp.zeros((1, cw), F32), unroll=2)

    spec = pl.BlockSpec((1, t, cw), lambda i, j: (i, 0, j))
    return pl.pallas_call(
        body,
        name=name,
        grid=(bsz, c // cw),
        in_specs=[spec, spec],
        out_specs=spec,
        out_shape=jax.ShapeDtypeStruct(a.shape, F32),
        compiler_params=_params("parallel", "parallel"),
    )(a, b)


def _scan_bwd(a, h, g, name):
    bsz, t, c = a.shape
    cw = _pick(c, (2 * LANES, LANES))

    def body(a_ref, h_ref, g_ref, da_ref, db_ref):
        rows = _SCAN_ROWS
        row = lax.broadcasted_iota(jnp.int32, (rows, cw), 0)
        n_tiles = t // rows

        def step(n, carry):
            lam_next, a_next = carry
            i = n_tiles - 1 - n
            r0 = pl.multiple_of(i * rows, rows)
            rp = pl.multiple_of(jnp.maximum(i - 1, 0) * rows, rows)
            av, gv, hv = a_ref[0, pl.ds(r0, rows), :], g_ref[0, pl.ds(r0, rows), :], h_ref[0, pl.ds(r0, rows), :]
            h_before = jnp.where(i > 0, h_ref[0, pl.ds(rp, rows), :][rows - 1:, :], 0.0)
            cv = jnp.where(row < rows - 1, pltpu.roll(av, rows - 1, 0), a_next)
            for s in (1, 2, 4):
                c_sh = jnp.where(row < rows - s, pltpu.roll(cv, rows - s, 0), 1.0)
                g_sh = jnp.where(row < rows - s, pltpu.roll(gv, rows - s, 0), 0.0)
                gv = cv * g_sh + gv
                cv = cv * c_sh
            lam = gv + cv * lam_next
            db_ref[0, pl.ds(r0, rows), :] = lam
            da_ref[0, pl.ds(r0, rows), :] = lam * jnp.where(row >= 1, pltpu.roll(hv, 1, 0), h_before)
            return lam[:1, :], av[:1, :]

        zero = jnp.zeros((1, cw), F32)
        lax.fori_loop(0, n_tiles, step, (zero, zero), unroll=2)

    spec = pl.BlockSpec((1, t, cw), lambda i, j: (i, 0, j))
    return pl.pallas_call(
        body,
        name=name,
        grid=(bsz, c // cw),
        in_specs=[spec, spec, spec],
        out_specs=[spec, spec],
        out_shape=[jax.ShapeDtypeStruct(a.shape, F32)] * 2,
        compiler_params=_params("parallel", "parallel"),
    )(a, h, g)


def _make_scan(name):
    @jax.custom_vjp
    def op(a, b):
        return _scan_fwd(a, b, name + "_fwd")

    def fwd(a, b):
        h = op(a, b)
        return h, (a, h)

    def bwd(res, g):
        a, h = res
        da, db = _scan_bwd(a, h, g, name + "_bwd")
        return da, db

    op.defvjp(fwd, bwd)
    return op


def _query_blocks(t):
    blocks, start = [], 0
    while start < t:
        rows = 2 * SEQ_BLOCK if start + 2 * SEQ_BLOCK <= t else SEQ_BLOCK
        blocks.append((start, rows))
        start += rows
    return blocks


def _attn_probs(q, k, start, scale):
    tq, tk = q.shape[0], k.shape[0]
    s = _dot(q, k, "nt") * scale
    qpos = start + lax.broadcasted_iota(jnp.int32, (tq, tk), 0)
    kpos = lax.broadcasted_iota(jnp.int32, (tq, tk), 1)
    s = jnp.where(kpos <= qpos, s, NEG_INF)
    e = jnp.exp(s - jnp.max(s, axis=-1, keepdims=True))
    return e / jnp.sum(e, axis=-1, keepdims=True)


_MLA_SCALE = (MLA_NOPE + MLA_ROPE) ** -0.5


def _attn_specs(t):
    head = pl.BlockSpec((1, t, LANES), lambda b, h: (b, 0, h))
    shared = pl.BlockSpec((1, t, LANES), lambda b, h: (b, 0, 0))
    return head, shared


def _attn_fwd(q, kv, kpe, name):
    bsz, t, hl = q.shape
    head, shared = _attn_specs(t)

    def body(q_ref, kv_ref, kpe_ref, o_ref, k_s, v_s):
        lane = lax.broadcasted_iota(jnp.int32, (t, LANES), 1)
        kvh = kv_ref[0]
        k_s[...] = jnp.where(lane < MLA_NOPE, kvh, kpe_ref[0]).astype(BF16)
        v_s[...] = kvh.astype(BF16)
        for start, rows in _query_blocks(t):
            n = start + rows
            p = _attn_probs(q_ref[0, start:n, :], k_s[:n, :], start, _MLA_SCALE)
            o_ref[0, start:n, :] = _dot(p, v_s[:n, :], "nn")

    return pl.pallas_call(
        body,
        name=name,
        grid=(bsz, hl // LANES),
        in_specs=[head, head, shared],
        out_specs=head,
        out_shape=jax.ShapeDtypeStruct(q.shape, F32),
        scratch_shapes=[pltpu.VMEM((t, LANES), BF16), pltpu.VMEM((t, LANES), BF16)],
        compiler_params=_params("parallel", "parallel"),
    )(q, kv, kpe)


def _attn_bwd(q, kv, kpe, do, name):
    bsz, t, hl = q.shape
    head, shared = _attn_specs(t)

    def body(q_ref, kv_ref, kpe_ref, do_ref, dq_ref, dkv_ref, dkpe_ref, k_s, v_s, dk_s, dv_s):
        lane = lax.broadcasted_iota(jnp.int32, (t, LANES), 1)
        kvh = kv_ref[0]
        k_s[...] = jnp.where(lane < MLA_NOPE, kvh, kpe_ref[0]).astype(BF16)
        v_s[...] = kvh.astype(BF16)
        for start, rows in reversed(_query_blocks(t)):
            n = start + rows
            qb = q_ref[0, start:n, :]
            dob = jnp.where(lane[:rows] >= MLA_NOPE, do_ref[0, start:n, :], 0.0)
            kk, vv = k_s[:n, :], v_s[:n, :]
            p = _attn_probs(qb, kk, start, _MLA_SCALE)
            dp = _dot(dob, vv, "nt")
            ds = p * (dp - jnp.sum(dp * p, axis=-1, keepdims=True)) * _MLA_SCALE
            dq_ref[0, start:n, :] = _dot(ds, kk, "nn")
            if n == t:
                dk_s[...] = _dot(ds, qb, "tn")
                dv_s[...] = _dot(p, dob, "tn")
            else:
                dk_s[:n, :] += _dot(ds, qb, "tn")
                dv_s[:n, :] += _dot(p, dob, "tn")
        dk = dk_s[...]
        dkv_ref[0] = jnp.where(lane < MLA_NOPE, dk, dv_s[...])
        dkpe = jnp.where(lane >= MLA_NOPE, dk, 0.0)

        @pl.when(pl.program_id(1) == 0)
        def _():
            dkpe_ref[0] = dkpe

        @pl.when(pl.program_id(1) != 0)
        def _():
            dkpe_ref[0] += dkpe

    return pl.pallas_call(
        body,
        name=name,
        grid=(bsz, hl // LANES),
        in_specs=[head, head, shared, head],
        out_specs=[head, head, shared],
        out_shape=[
            jax.ShapeDtypeStruct(q.shape, F32),
            jax.ShapeDtypeStruct(kv.shape, F32),
            jax.ShapeDtypeStruct(kpe.shape, F32),
        ],
        scratch_shapes=[pltpu.VMEM((t, LANES), BF16), pltpu.VMEM((t, LANES), BF16),
                        pltpu.VMEM((t, LANES), F32), pltpu.VMEM((t, LANES), F32)],
        compiler_params=_params("parallel", "arbitrary"),
    )(q, kv, kpe, do)


def _make_attention(name):
    @jax.custom_vjp
    def op(q, kv, kpe):
        return _attn_fwd(q, kv, kpe, name + "_fwd")

    def fwd(q, kv, kpe):
        return op(q, kv, kpe), (q, kv, kpe)

    def bwd(res, do):
        return tuple(_attn_bwd(*res, do, name + "_bwd"))

    op.defvjp(fwd, bwd)
    return op


_ROPE_SHIFT = MLA_ROPE // 2


def _rope_lanes_call(x, c, sm, sp, transpose, name):
    r, width = x.shape
    tm = _row_tile(r, width)

    def body(x_ref, c_ref, sm_ref, sp_ref, y_ref):
        cv, smv, spv = c_ref[...], sm_ref[...], sp_ref[...]
        for b in range(width // LANES):
            xb = x_ref[:, b * LANES:(b + 1) * LANES]
            if transpose:
                yb = xb * cv + pltpu.roll(xb * smv, _ROPE_SHIFT, 1) + pltpu.roll(xb * spv, LANES - _ROPE_SHIFT, 1)
            else:
                yb = xb * cv + pltpu.roll(xb, LANES - _ROPE_SHIFT, 1) * smv + pltpu.roll(xb, _ROPE_SHIFT, 1) * spv
            y_ref[:, b * LANES:(b + 1) * LANES] = yb

    tab = pl.BlockSpec((tm, LANES), lambda i: (i, 0))
    blk = pl.BlockSpec((tm, width), lambda i: (i, 0))
    return pl.pallas_call(
        body,
        name=name,
        grid=(r // tm,),
        in_specs=[blk, tab, tab, tab],
        out_specs=blk,
        out_shape=jax.ShapeDtypeStruct(x.shape, F32),
        compiler_params=_params("parallel"),
    )(x, c, sm, sp)


def _make_rope_lanes(name):
    @jax.custom_vjp
    def op(x, c, sm, sp):
        return _rope_lanes_call(x, c, sm, sp, False, name + "_fwd")

    def fwd(x, c, sm, sp):
        return op(x, c, sm, sp), (c, sm, sp)

    def bwd(res, dy):
        c, sm, sp = res
        return _rope_lanes_call(dy, c, sm, sp, True, name + "_bwd"), jnp.zeros_like(c), jnp.zeros_like(sm), jnp.zeros_like(sp)

    op.defvjp(fwd, bwd)
    return op


RET_KEY_CHUNK = 512


def _key_chunks(t):
    return [(c, min(RET_KEY_CHUNK, t - c)) for c in range(0, t, RET_KEY_CHUNK)]


def _decay(q0, k0, tq, tk, log_gamma):
    row = lax.broadcasted_iota(jnp.int32, (tq, 1), 0)
    col = (q0 - k0) - lax.broadcasted_iota(jnp.int32, (1, tk), 1)
    outer = jnp.exp(log_gamma * row.astype(F32)) * jnp.exp(log_gamma * col.astype(F32))
    return jnp.where(row + col >= 0, outer, 0.0)


def _ret_query_rows(t):
    return t // 4 if t % 32 == 0 else SEQ_BLOCK


def _ret_specs(t, dk, dv, heads):
    tq = _ret_query_rows(t)
    return (
        pl.BlockSpec(memory_space=pltpu.SMEM),
        pl.BlockSpec((1, tq, dk), lambda b, h, i: (b, i, h)),
        pl.BlockSpec((1, t, dk), lambda b, h, i: (b, 0, h)),
        pl.BlockSpec((1, t, dv), lambda b, h, i: (b, 0, h)),
        pl.BlockSpec((1, tq, dv), lambda b, h, i: (b, i, h)),
    )


def _ret_fwd(lg, q, k, v, name):
    bsz, t, hdk = q.shape
    heads = lg.shape[0]
    dk, dv = hdk // heads, v.shape[2] // heads
    lg_spec, q_spec, k_spec, v_spec, o_spec = _ret_specs(t, dk, dv, heads)
    tq = _ret_query_rows(t)

    def body(lg_ref, q_ref, k_ref, v_ref, o_ref):
        q0 = pl.program_id(2) * tq
        lgh = lg_ref[pl.program_id(1)]
        o_ref[0] = jnp.zeros((tq, dv), F32)
        for c0, cw in _key_chunks(t):
            @pl.when(c0 < q0 + tq)
            def _(c0=c0, cw=cw):
                d = _decay(q0, c0, tq, cw, lgh)
                a = _dot(q_ref[0], k_ref[0, c0:c0 + cw, :], "nt") * d
                o_ref[0] += _dot(a, v_ref[0, c0:c0 + cw, :], "nn")

    return pl.pallas_call(
        body,
        name=name,
        grid=(bsz, heads, t // tq),
        in_specs=[lg_spec, q_spec, k_spec, v_spec],
        out_specs=o_spec,
        out_shape=jax.ShapeDtypeStruct(v.shape, F32),
        compiler_params=_params("parallel", "parallel", "parallel"),
    )(lg, q, k, v)


def _ret_bwd(lg, q, k, v, do, name):
    bsz, t, hdk = q.shape
    heads = lg.shape[0]
    dk, dv = hdk // heads, v.shape[2] // heads
    lg_spec, q_spec, k_spec, v_spec, o_spec = _ret_specs(t, dk, dv, heads)
    tq = _ret_query_rows(t)

    def body(lg_ref, q_ref, k_ref, v_ref, do_ref, dq_ref, dk_ref, dv_ref):
        q0 = pl.program_id(2) * tq
        lgh = lg_ref[pl.program_id(1)]

        @pl.when(pl.program_id(2) == 0)
        def _():
            dk_ref[0] = jnp.zeros((t, dk), F32)
            dv_ref[0] = jnp.zeros((t, dv), F32)

        dq_ref[0] = jnp.zeros((tq, dk), F32)
        for c0, cw in _key_chunks(t):
            @pl.when(c0 < q0 + tq)
            def _(c0=c0, cw=cw):
                qb, dob = q_ref[0], do_ref[0]
                kk, vv = k_ref[0, c0:c0 + cw, :], v_ref[0, c0:c0 + cw, :]
                d = _decay(q0, c0, tq, cw, lgh)
                a = _dot(qb, kk, "nt") * d
                ds = _dot(dob, vv, "nt") * d
                dq_ref[0] += _dot(ds, kk, "nn")
                dk_ref[0, c0:c0 + cw, :] += _dot(ds, qb, "tn")
                dv_ref[0, c0:c0 + cw, :] += _dot(a, dob, "tn")

    return pl.pallas_call(
        body,
        name=name,
        grid=(bsz, heads, t // tq),
        in_specs=[lg_spec, q_spec, k_spec, v_spec, o_spec],
        out_specs=[q_spec, k_spec, v_spec],
        out_shape=[
            jax.ShapeDtypeStruct(q.shape, F32),
            jax.ShapeDtypeStruct(k.shape, F32),
            jax.ShapeDtypeStruct(v.shape, F32),
        ],
        compiler_params=_params("parallel", "parallel", "arbitrary"),
    )(lg, q, k, v, do)


def _make_retention(name):
    @jax.custom_vjp
    def op(lg, q, k, v):
        return _ret_fwd(lg, q, k, v, name + "_fwd")

    def fwd(lg, q, k, v):
        return op(lg, q, k, v), (lg, q, k, v)

    def bwd(res, do):
        lg = res[0]
        return (jnp.zeros_like(lg),) + tuple(_ret_bwd(*res, do, name + "_bwd"))

    op.defvjp(fwd, bwd)
    return op


def _adamw(w, g, m, v, name):
    r, c = w.shape
    tr = _pick(r, (256, 128, 64, 32, 16, 8))

    def body(w_ref, g_ref, m_ref, v_ref, d_ref, nm_ref, nv_ref):
        gv = g_ref[...]
        nm = ADAM_B1 * m_ref[...] + (1.0 - ADAM_B1) * gv
        nv = ADAM_B2 * v_ref[...] + (1.0 - ADAM_B2) * (gv * gv)
        m_hat = nm / (1.0 - ADAM_B1 ** ADAM_STEP)
        v_hat = nv / (1.0 - ADAM_B2 ** ADAM_STEP)
        d_ref[...] = -ADAM_LR * (m_hat / (jnp.sqrt(v_hat) + ADAM_EPS) + ADAM_WD * w_ref[...])
        nm_ref[...] = nm
        nv_ref[...] = nv

    spec = pl.BlockSpec((tr, c), lambda i: (i, 0))
    return pl.pallas_call(
        body,
        name=name,
        grid=(r // tr,),
        in_specs=[spec] * 4,
        out_specs=[spec] * 3,
        out_shape=[jax.ShapeDtypeStruct((r, c), F32)] * 3,
        compiler_params=_params("parallel"),
    )(w, g, m, v)


def _rope_tables(t, half, reps):
    inv = ROPE_BASE ** (-jnp.arange(half, dtype=F32) / half)
    ang = jnp.arange(t, dtype=jnp.int32).astype(F32)[:, None] * inv[None, :]
    return jnp.tile(jnp.cos(ang), (1, reps)), jnp.tile(jnp.sin(ang), (1, reps))


def _padded_len(seq):
    return -(-(N_META + seq) // SEQ_BLOCK) * SEQ_BLOCK


def _embed(meta, x):
    bsz, seq, d = x.shape
    t = _padded_len(seq)
    return jnp.concatenate(
        [jnp.broadcast_to(meta[None], (bsz, N_META, d)), x, jnp.zeros((bsz, t - N_META - seq, d), F32)], axis=1
    ).reshape(bsz * t, d)


def _even_mixer(p, conv_w_shard, conv_b, w_rg_a, b_rg_a, w_rg_x, b_rg_x, lru_lambda, q_norm_g, w_uq_shard,
                kv_norm_g, w_ukv_shard, gathered, bsz):
    conv_w_full, w_uq_full, w_ukv_full = gathered
    r = p.shape[0]
    t = r // bsz

    def tile_rows(tab):
        return jnp.tile(tab, (bsz, 1))

    lru_w = w_rg_a.shape[2] * w_rg_a.shape[1]
    q_rank, kv_rank = q_norm_g.shape[1], kv_norm_g.shape[1]
    p_gate, p_rec, p_q, p_kv, p_kpe = _split_cols(
        p, (lru_w, 2 * lru_w, 2 * lru_w + q_rank, 2 * lru_w + q_rank + kv_rank))

    conv_w = _make_fsdp_param("conv_w")(conv_w_shard[0], conv_w_full)
    xc = _make_conv("conv")(p_rec.reshape(bsz, t, lru_w), conv_w, conv_b).reshape(r, lru_w)
    ga = _make_group_linear("rg_a")(xc, w_rg_a[0])
    gx = _make_group_linear("rg_x")(xc, w_rg_x[0])
    (a, bb), _ = _make_rowwise(_lru_gates_f, "lru_gates", 3, 0, 3)((ga, gx, xc), (), (b_rg_a, b_rg_x, lru_lambda))
    hh = _make_scan("lru_scan")(a.reshape(bsz, t, lru_w), bb.reshape(bsz, t, lru_w)).reshape(r, lru_w)
    (y_rec,), _ = _make_rowwise(_lru_out_f, "lru_out", 2, 0, 0)((hh, p_gate), (), ())

    (qn,), _ = _make_rowwise(_rmsnorm_f, "q_norm", 1, 0, 1)((p_q,), (), (q_norm_g,))
    (kvn,), _ = _make_rowwise(_rmsnorm_f, "kv_norm", 1, 0, 1)((p_kv,), (), (kv_norm_g,))
    d_head = MLA_NOPE + MLA_ROPE
    w_uq_pad = jnp.pad(w_uq_full.reshape(q_rank, MLA_HEADS, d_head), ((0, 0), (0, 0), (0, LANES - d_head)))

    def unpad_uq(dw):
        return dw.reshape(q_rank, MLA_HEADS, LANES)[:, :, :d_head].reshape(q_rank, MLA_HEADS * d_head)

    q = _make_fsdp_linear(True, "ev_uq", unpad_uq)(qn, w_uq_shard[0], w_uq_pad.reshape(q_rank, MLA_HEADS * LANES))
    kv = _make_fsdp_linear(True, "ev_ukv")(kvn, w_ukv_shard[0], w_ukv_full)
    half = MLA_ROPE // 2
    cos, sin = _rope_tables(t, half, 1)
    one, zero = jnp.ones((t, MLA_NOPE), F32), jnp.zeros((t, MLA_NOPE), F32)
    tail = LANES - MLA_NOPE - MLA_ROPE
    c_tab = tile_rows(jnp.concatenate([one, cos, cos, one[:, :tail]], axis=1))
    sm_tab = tile_rows(jnp.concatenate([zero, -sin, zero[:, :half + tail]], axis=1))
    sp_tab = tile_rows(jnp.concatenate([zero, zero[:, :half], sin, zero[:, :tail]], axis=1))
    q = _make_rope_lanes("rope_q")(q, c_tab, sm_tab, sp_tab)
    kpe = _make_rope_lanes("rope_k")(p_kpe, c_tab, sm_tab, sp_tab)
    o = _make_attention("mla")(q.reshape(bsz, t, -1), kv.reshape(bsz, t, -1), kpe.reshape(bsz, t, LANES))
    return jnp.concatenate([y_rec, o.reshape(r, -1)], axis=1)


def _odd_mixer(p, bsz):
    r = p.shape[0]
    t = r // bsz

    def tile_rows(tab):
        return jnp.tile(tab, (bsz, 1))

    qk = p.shape[1] // 6
    dk = qk // RET_HEADS
    cos2, sin2 = _rope_tables(t, dk // 2, 2)
    sin2 = jnp.concatenate([-sin2[:, :dk // 2], sin2[:, dk // 2:]], axis=1)
    p_q, p_k, p_v, p_g = _split_cols(p, (qk, 2 * qk, 4 * qk))
    (rq, rk), _ = _make_rowwise(_rope_ret_f, "rope_ret", 2, 2, 0)((p_q, p_k), (tile_rows(cos2), tile_rows(sin2)), ())
    lg = jnp.log(1.0 - 2.0 ** (-5.0 - jnp.arange(RET_HEADS, dtype=F32)))
    o = _make_retention("ret")(lg, rq.reshape(bsz, t, qk), rk.reshape(bsz, t, qk), p_v.reshape(bsz, t, 2 * qk))
    (y,), _ = _make_rowwise(_ret_out_f, "ret_out", 2, 0, 0)((o.reshape(r, 2 * qk), p_g), (), ())
    return y


def _local_loss(h, target):
    bsz, seq, d = target.shape
    t = _padded_len(seq)
    t_real = N_META + seq
    pos = jnp.arange(t, dtype=jnp.int32)
    mask = jnp.tile(((pos >= N_META) & (pos < t_real)).astype(F32)[:, None], (bsz, 1))
    tgt = jnp.concatenate(
        [jnp.zeros((bsz, N_META, d), F32), target, jnp.zeros((bsz, t - t_real, d), F32)], axis=1).reshape(bsz * t, d)
    _, (total,) = _make_rowwise(_loss_f, "loss", 1, 2, 0)((h,), (tgt, mask), ())
    return total[0, 0]


_WEIGHTS = ("meta_tokens", "ev_w_in", "ev_conv_w", "ev_conv_b", "ev_w_rg_a", "ev_b_rg_a", "ev_w_rg_x", "ev_b_rg_x",
            "ev_lru_lambda", "ev_q_norm_g", "ev_w_uq", "ev_kv_norm_g", "ev_w_ukv", "ev_w_out", "od_w_in", "od_w_out",
            "ln_mix_g", "ln_mix_b", "mlp_w1", "mlp_w2", "ln_mlp_g", "ln_mlp_b")


def kernel(x, meta_tokens, ev_w_in, ev_conv_w, ev_conv_b, ev_w_rg_a, ev_b_rg_a, ev_w_rg_x, ev_b_rg_x, ev_lru_lambda, ev_q_norm_g, ev_w_uq, ev_kv_norm_g, ev_w_ukv, ev_w_out, od_w_in, od_w_out, ln_mix_g, ln_mix_b, mlp_w1, mlp_w2, ln_mlp_g, ln_mlp_b, loss_target, m_meta_tokens, m_ev_w_in, m_ev_conv_w, m_ev_conv_b, m_ev_w_rg_a, m_ev_b_rg_a, m_ev_w_rg_x, m_ev_b_rg_x, m_ev_lru_lambda, m_ev_q_norm_g, m_ev_w_uq, m_ev_kv_norm_g, m_ev_w_ukv, m_ev_w_out, m_od_w_in, m_od_w_out, m_ln_mix_g, m_ln_mix_b, m_mlp_w1, m_mlp_w2, m_ln_mlp_g, m_ln_mlp_b, v_meta_tokens, v_ev_w_in, v_ev_conv_w, v_ev_conv_b, v_ev_w_rg_a, v_ev_b_rg_a, v_ev_w_rg_x, v_ev_b_rg_x, v_ev_lru_lambda, v_ev_q_norm_g, v_ev_w_uq, v_ev_kv_norm_g, v_ev_w_ukv, v_ev_w_out, v_od_w_in, v_od_w_out, v_ln_mix_g, v_ln_mix_b, v_mlp_w1, v_mlp_w2, v_ln_mlp_g, v_ln_mlp_b):
    args = locals()
    weights = {n: args[n] for n in _WEIGHTS}
    bsz = x.shape[0]
    my_x, my_y, my_c = _my_place()
    me = 4 * my_x + 2 * my_y + my_c

    big = (("ev_in", ev_w_in[0], True), ("ev_out", ev_w_out[0], False), ("mlp0_w1", mlp_w1[0], True),
           ("mlp0_w2", mlp_w2[0], False), ("od_in", od_w_in[0], True), ("od_out", od_w_out[0], False),
           ("mlp1_w1", mlp_w1[1], True), ("mlp1_w2", mlp_w2[1], False))
    small_sharded = (("meta", meta_tokens, F32), ("conv_w", ev_conv_w[0], F32), ("ev_uq", ev_w_uq[0], BF16),
                     ("ev_ukv", ev_w_ukv[0], BF16))
    to_gather = (tuple((nm, s.astype(dt), True) for nm, s, dt in small_sharded)
                 + tuple((nm, s.astype(BF16), cols) for nm, s, cols in big))
    handles = _gather_start_all([s for _, s, _ in to_gather], "ag_start")
    gathers = {nm: (s, cols, h) for (nm, s, cols), h in zip(to_gather, handles)}
    gather_tokens = (handles[0][4],)

    def full_weight(nm, after):
        shard16, cols, handle = gathers[nm]
        land = _exchange_wait(handle, True, after, "ag_wait_" + nm)
        land = lax.dynamic_update_index_in_dim(land, shard16, me, 0)
        if cols and shard16.shape[1] % LANES == 0:
            return land, True
        return (_unstack_cols(land) if cols else land.reshape(-1, shard16.shape[1])), False

    meta_full, conv_w_full, w_uq_full, w_ukv_full = (
        _unstack_cols(lax.dynamic_update_index_in_dim(
            _exchange_wait(gathers[nm][2], True, gather_tokens[-1], "ag_wait_" + nm), gathers[nm][0], me, 0))
        for nm, _, _ in small_sharded)

    pending = []

    def linear_bwd(nm, x_in, w, dy, cols, unpad=None, **fused):
        w_full, w_stacked = w
        a_relu2 = fused.pop("a_relu2", False)
        if w_stacked:
            stacked = _matmul(x_in, dy, "tn", nm + "_dw", stacked=True, a_relu2=a_relu2)
            own = lax.dynamic_index_in_dim(stacked, me, 0, keepdims=False)
        else:
            dw = _matmul(x_in, dy, "tn", nm + "_dw", a_relu2=a_relu2)
            dw = dw if unpad is None else unpad(dw)
            n = dw.shape[1] // N_DEV
            if cols:
                stacked = _stack_cols(dw)
                own = lax.dynamic_slice_in_dim(dw, me * n, n, axis=1)
            else:
                stacked = dw.reshape(N_DEV, dw.shape[0] // N_DEV, dw.shape[1])
                own = lax.dynamic_index_in_dim(stacked, me, 0, keepdims=False)
        handle = _exchange_start(stacked, (_N_PEERS,) + stacked.shape[1:], False, "rs_start_" + nm)
        pending.append((nm, own, handle))
        return _matmul(dy, w_full, "nt", nm + "_dx", after=(handle[4],), stacked=w_stacked, **fused)

    def linear_fwd(nm, x_in, w, **fused):
        return _matmul(x_in, w[0], "nn", nm + "_fwd", stacked=w[1], **fused)

    def mlp_fwd(h, l):
        w1 = full_weight(f"mlp{l}_w1", h)
        u = linear_fwd(f"mlp{l}_w1", h, w1)
        w2 = full_weight(f"mlp{l}_w2", u)
        f = linear_fwd(f"mlp{l}_w2", u, w2, a_relu2=True)
        ln_args = (h, f, ln_mlp_g[l:l + 1], ln_mlp_b[l:l + 1])
        return ln_fwd(f"mlp{l}_ln", *ln_args), (h, w1, u, w2, ln_args)

    def mlp_bwd(dout, res, l):
        h, w1, u, w2, ln_args = res
        dh, df, dg, db = ln_bwd(f"mlp{l}_ln", ln_args, dout)
        du = linear_bwd(f"mlp{l}_w2", u, w2, df, False, a_relu2=True, relu2_bwd_of=u, out_dtype=BF16)
        return (dh, linear_bwd(f"mlp{l}_w1", h, w1, du, True)), dg, db

    def ln_fwd(nm, h, mix, g, b):
        return _make_rowwise(_ln_res_f, nm, 2, 0, 2).fwd_call(h, mix, g, b)[0][0]

    def ln_bwd(nm, ln_args, pieces):
        (dh, dmix), (dg, db) = _make_rowwise(_ln_res_f, nm, 2, 0, 2).bwd_call(
            ln_args, ((pieces[0],), ()), more=tuple(pieces[1:]), row_dtypes=(F32, BF16))
        return dh, dmix, dg, db

    h0, vjp_embed = jax.vjp(_embed, meta_full, x)
    n_in = ev_w_in.shape[2] * N_DEV
    kpe0, pad_lo, pad_hi = n_in - MLA_ROPE, MLA_NOPE, LANES - MLA_NOPE - MLA_ROPE
    w_in = full_weight("ev_in", h0)[0]
    zeros_in = jnp.zeros((w_in.shape[0], pad_lo), BF16)
    w_ev_in = (jnp.concatenate([w_in[:, :kpe0], zeros_in, w_in[:, kpe0:], zeros_in[:, :pad_hi]], axis=1), False)

    def unpad_in(dw):
        return jnp.concatenate([dw[:, :kpe0], dw[:, kpe0 + pad_lo:kpe0 + pad_lo + MLA_ROPE]], axis=1)

    p0 = _matmul(h0, w_ev_in[0], "nn", "ev_in_fwd", after=gather_tokens)
    small = (ev_conv_w, ev_conv_b, ev_w_rg_a, ev_b_rg_a, ev_w_rg_x, ev_b_rg_x, ev_lru_lambda, ev_q_norm_g, ev_w_uq,
             ev_kv_norm_g, ev_w_ukv)
    y0, vjp_even = jax.vjp(lambda p, *s: _even_mixer(p, *s, (conv_w_full, w_uq_full, w_ukv_full), bsz), p0, *small)
    w_out = full_weight("ev_out", y0)[0]
    lru_w, d_model = y0.shape[1] - MLA_HEADS * LANES, w_out.shape[1]
    w_att = w_out[lru_w:].reshape(MLA_HEADS, MLA_V, d_model)
    w_att = jnp.concatenate([jnp.zeros((MLA_HEADS, LANES - MLA_V, d_model), BF16), w_att], axis=1)
    w_ev_out = (jnp.concatenate([w_out[:lru_w], w_att.reshape(MLA_HEADS * LANES, d_model)], axis=0), False)

    def unpad_out(dw):
        d_att = dw[lru_w:].reshape(MLA_HEADS, LANES, d_model)[:, LANES - MLA_V:].reshape(MLA_HEADS * MLA_V, d_model)
        return jnp.concatenate([dw[:lru_w], d_att], axis=0)

    mix0 = linear_fwd("ev_out", y0, w_ev_out)
    ln0_args = (h0, mix0, ln_mix_g[0:1], ln_mix_b[0:1])
    h1 = ln_fwd("mix0_ln", *ln0_args)
    h2, res_mlp0 = mlp_fwd(h1, 0)
    w_od_in = full_weight("od_in", h2)
    p1 = linear_fwd("od_in", h2, w_od_in)
    y1, vjp_odd = jax.vjp(lambda p: _odd_mixer(p, bsz), p1)
    w_od_out = full_weight("od_out", y1)
    mix1 = linear_fwd("od_out", y1, w_od_out)
    ln1_args = (h2, mix1, ln_mix_g[1:2], ln_mix_b[1:2])
    h3 = ln_fwd("mix1_ln", *ln1_args)
    h4, res_mlp1 = mlp_fwd(h3, 1)
    loss_local, vjp_loss = jax.vjp(lambda h: _local_loss(h, loss_target), h4)

    dh4 = vjp_loss(jnp.ones((), F32))
    dh3, dg_mlp1, db_mlp1 = mlp_bwd(dh4, res_mlp1, 1)
    dh2, dmix1, dg_mix1, db_mix1 = ln_bwd("mix1_ln", ln1_args, dh3)
    (dp1,) = vjp_odd(linear_bwd("od_out", y1, w_od_out, dmix1, False))
    dh2 = (dh2, linear_bwd("od_in", h2, w_od_in, dp1.astype(BF16), True))
    dh1, dg_mlp0, db_mlp0 = mlp_bwd(dh2, res_mlp0, 0)
    dh0, dmix0, dg_mix0, db_mix0 = ln_bwd("mix0_ln", ln0_args, dh1)
    dp0, *dsmall = vjp_even(linear_bwd("ev_out", y0, w_ev_out, dmix0, False, unpad=unpad_out))
    dh0 = dh0 + linear_bwd("ev_in", h0, w_ev_in, dp0.astype(BF16), True, unpad=unpad_in)
    g_meta_full, grad_x = vjp_embed(dh0)
    (g_conv_w, g_conv_b, g_w_rg_a, g_b_rg_a, g_w_rg_x, g_b_rg_x, g_lambda, g_q_norm, g_w_uq, g_kv_norm, g_w_ukv) = dsmall

    after, summed = grad_x, {}
    for nm, own, handle in pending:
        land = _exchange_wait(handle, False, after, "rs_wait_" + nm)
        summed[nm] = after = _sum_own_and_peers(own, land, "rs_sum_" + nm)

    g_meta = _scatter_grad(g_meta_full, True, "meta_rs", after=(after,))
    rep_names = ("ev_conv_b", "ev_w_rg_a", "ev_b_rg_a", "ev_w_rg_x", "ev_b_rg_x", "ev_lru_lambda", "ev_q_norm_g",
                 "ev_kv_norm_g", "ln_mix_g", "ln_mix_b", "ln_mlp_g", "ln_mlp_b")
    rep_local = (g_conv_b, g_w_rg_a, g_b_rg_a, g_w_rg_x, g_b_rg_x, g_lambda, g_q_norm, g_kv_norm,
                 jnp.concatenate([dg_mix0, dg_mix1]), jnp.concatenate([db_mix0, db_mix1]),
                 jnp.concatenate([dg_mlp0, dg_mlp1]), jnp.concatenate([db_mlp0, db_mlp1]))
    grad_w = dict(zip(rep_names, _allreduce_replicated(rep_local, "rep", after=(g_meta,))))
    grad_w.update(meta_tokens=g_meta, ev_conv_w=g_conv_w, ev_w_uq=g_w_uq, ev_w_ukv=g_w_ukv)
    grad_w.update(ev_w_in=summed["ev_in"][None], ev_w_out=summed["ev_out"][None], od_w_in=summed["od_in"][None],
                  od_w_out=summed["od_out"][None], mlp_w1=jnp.stack([summed["mlp0_w1"], summed["mlp1_w1"]]),
                  mlp_w2=jnp.stack([summed["mlp0_w2"], summed["mlp1_w2"]]))

    loss = lax.psum(loss_local, MESH_AXES)
    delta, new_m, new_v = {}, {}, {}
    for n in _WEIGHTS:
        w, g, m, v = weights[n], grad_w[n], args["m_" + n], args["v_" + n]
        two_d = (-1, w.shape[-1])
        d2, m2, v2 = _adamw(w.reshape(two_d), g.reshape(two_d), m.reshape(two_d), v.reshape(two_d), "adamw_" + n)
        delta[n], new_m[n], new_v[n] = d2.reshape(w.shape), m2.reshape(w.shape), v2.reshape(w.shape)
    return (loss, grad_x, *[grad_w[n] for n in _WEIGHTS], *[delta[n] for n in _WEIGHTS],
            *[new_m[n] for n in _WEIGHTS], *[new_v[n] for n in _WEIGHTS])
```

```python
import functools
import math

import jax
import jax.numpy as jnp
from jax import lax
from jax.experimental import pallas as pl
from jax.experimental.pallas import tpu as pltpu

F32 = jnp.float32
BF16 = jnp.bfloat16

N_DEV = 8
MESH_AXES = ("x", "y", "c")
LANES = 128
SEQ_BLOCK = 128

N_META = 16
LRU_C = 8.0
MLA_HEADS = 8
MLA_NOPE = 64
MLA_ROPE = 32
MLA_V = 64
RET_HEADS = 4
ROPE_BASE = 10000.0
DEPTH = 2
DN_ALPHA = (2 * DEPTH) ** 0.25
EPS = 1e-5
NEG_INF = -1e30

ADAM_LR = 0.001
ADAM_B1 = 0.9
ADAM_B2 = 0.999
ADAM_EPS = 1e-08
ADAM_WD = 0.01
ADAM_STEP = 10

VMEM_LIMIT = 56 * 1024 * 1024


def _params(*sem):
    return pltpu.CompilerParams(dimension_semantics=sem, vmem_limit_bytes=VMEM_LIMIT)


def _pick(n, cands):
    for c in cands:
        if n % c == 0:
            return c
    return n


def _row_tile(r, width):
    cands = (256, 128, 64, 32, 16, 8) if width <= 1024 else (128, 64, 32, 16, 8)
    return _pick(r, cands)


_DIMS = {"nn": (((1,), (0,)), ((), ())), "nt": (((1,), (1,)), ((), ())), "tn": (((0,), (0,)), ((), ()))}


def _dot(a, b, mode):
    return lax.dot_general(a.astype(BF16), b.astype(BF16), _DIMS[mode], preferred_element_type=F32)


def _matmul(a, b, mode, name, after=(), stacked=False, a_relu2=False, relu2_bwd_of=None, out_dtype=F32):
    if stacked:
        n_blk = b.shape[2] if mode != "tn" else b.shape[1] // N_DEV
    if mode == "nn":
        (m, k), n = a.shape, (N_DEV * n_blk if stacked else b.shape[1])
    elif mode == "nt":
        (m, k), n = a.shape, (b.shape[1] if stacked else b.shape[0])
    else:
        (k, m), n = a.shape, b.shape[1]
    tm = _pick(m, (1088, 1024, 544, 512, 272, 256, 128, 64, 32, 16, 8))
    tn = _pick(n, (1024, 512, 256, 128))
    tk = _pick(k, (1088, 1024, 544, 512, 272, 256, 128))
    kb = 2
    if stacked and mode == "nn":
        tn = n_blk
    if stacked and mode == "tn":
        tn = kb * n_blk
    if stacked and mode == "nt":
        tk = kb * n_blk
    nk = k // tk
    assert out_dtype == F32 or (nk == 1 and not (stacked and mode == "tn")), "narrow results need a single k step"

    out_spec = pl.BlockSpec((tm, tn), lambda i, j, kk: (i, j))
    out_shape = jax.ShapeDtypeStruct((m, n), out_dtype)
    if mode == "nn":
        a_spec = pl.BlockSpec((tm, tk), lambda i, j, kk: (i, kk))
        b_spec = pl.BlockSpec((tk, tn), lambda i, j, kk: (kk, j))
        if stacked:
            b_spec = pl.BlockSpec((None, tk, tn), lambda i, j, kk: (j, kk, 0))
    elif mode == "nt":
        a_spec = pl.BlockSpec((tm, tk), lambda i, j, kk: (i, kk))
        b_spec = pl.BlockSpec((tn, tk), lambda i, j, kk: (j, kk))
        if stacked:
            b_spec = pl.BlockSpec((kb, tn, n_blk), lambda i, j, kk: (kk, j, 0))
    else:
        a_spec = pl.BlockSpec((tk, tm), lambda i, j, kk: (kk, i))
        b_spec = pl.BlockSpec((tk, tn), lambda i, j, kk: (kk, j))
        if stacked:
            out_spec = pl.BlockSpec((kb, tm, n_blk), lambda i, j, kk: (j, i, 0))
            out_shape = jax.ShapeDtypeStruct((N_DEV, m, n_blk), F32)
    extra = [] if relu2_bwd_of is None else [relu2_bwd_of]
    extra_specs = [pl.BlockSpec((tm, tn), lambda i, j, kk: (i, j))] * len(extra)

    def body(a_ref, b_ref, *rest):
        o_ref = rest[-1]
        kk = pl.program_id(2)
        av = a_ref[...]
        if a_relu2:
            av = jnp.maximum(av, 0.0)
            av = av * av
        if stacked and mode == "nt":
            part = _dot(av[:, :n_blk], b_ref[0], mode)
            for q in range(1, kb):
                part = part + _dot(av[:, q * n_blk:(q + 1) * n_blk], b_ref[q], mode)
        else:
            part = _dot(av, b_ref[...], mode)
        if stacked and mode == "tn":
            part = jnp.stack([part[:, q * n_blk:(q + 1) * n_blk] for q in range(kb)])
        if out_dtype != F32:
            if relu2_bwd_of is not None:
                part = part * (2.0 * jnp.maximum(rest[0][...], 0.0))
            o_ref[...] = part.astype(out_dtype)
            return

        @pl.when(kk == 0)
        def _():
            o_ref[...] = part

        @pl.when(kk != 0)
        def _():
            o_ref[...] += part

        if relu2_bwd_of is not None:
            @pl.when(kk == nk - 1)
            def _():
                o_ref[...] *= 2.0 * jnp.maximum(rest[0][...], 0.0)

    return pl.pallas_call(
        body,
        name=name,
        grid=(m // tm, n // tn, nk),
        in_specs=[a_spec, b_spec] + extra_specs + [pl.BlockSpec(memory_space=pl.ANY)] * len(after),
        out_specs=out_spec,
        out_shape=out_shape,
        compiler_params=_params("parallel", "parallel", "arbitrary"),
    )(a, b, *extra, *after)


def _group_matmul(a, w, mode, name):
    if mode in ("nn", "nt"):
        g, dk, dn = w.shape
        m = a.shape[0]
        d_in, d_out = (dk, dn) if mode == "nn" else (dn, dk)
        tm = _pick(m, (1088, 1024, 544, 512, 272, 256, 128, 64, 32, 16, 8))

        def body(a_ref, w_ref, o_ref):
            o_ref[...] = _dot(a_ref[...], w_ref[0], mode)

        return pl.pallas_call(
            body,
            name=name,
            grid=(g, m // tm),
            in_specs=[pl.BlockSpec((tm, d_in), lambda h, i: (i, h)), pl.BlockSpec((1, dk, dn), lambda h, i: (h, 0, 0))],
            out_specs=pl.BlockSpec((tm, d_out), lambda h, i: (i, h)),
            out_shape=jax.ShapeDtypeStruct((m, g * d_out), F32),
            compiler_params=_params("parallel", "parallel"),
        )(a, w)
    b = w
    m = a.shape[0]
    dk = dn = LANES
    g = a.shape[1] // dk
    tm = _pick(m, (1088, 1024, 544, 512, 272, 256, 128, 64, 32, 16, 8))

    def body(a_ref, b_ref, o_ref):
        part = _dot(a_ref[...], b_ref[...], "tn")

        @pl.when(pl.program_id(1) == 0)
        def _():
            o_ref[0] = part

        @pl.when(pl.program_id(1) != 0)
        def _():
            o_ref[0] += part

    return pl.pallas_call(
        body,
        name=name,
        grid=(g, m // tm),
        in_specs=[pl.BlockSpec((tm, dk), lambda h, i: (i, h)), pl.BlockSpec((tm, dn), lambda h, i: (i, h))],
        out_specs=pl.BlockSpec((1, dk, dn), lambda h, i: (h, 0, 0)),
        out_shape=jax.ShapeDtypeStruct((g, dk, dn), F32),
        compiler_params=_params("parallel", "arbitrary"),
    )(a, b)


def _make_group_linear(name):
    @jax.custom_vjp
    def op(x, w):
        return _group_matmul(x, w, "nn", name + "_fwd")

    def fwd(x, w):
        return op(x, w), (x, w)

    def bwd(res, dy):
        x, w = res
        return _group_matmul(dy, w, "nt", name + "_dx"), _group_matmul(x, dy, "tn", name + "_dw")

    op.defvjp(fwd, bwd)
    return op


def _my_place():
    return lax.axis_index("x"), lax.axis_index("y"), lax.axis_index("c")


def _all_gather(shard, name, after=()):
    shape, dtype = shard.shape, shard.dtype

    def body(x_ref, *rest):
        out_ref, send_sems, recv_sems, local_sem = rest[len(after):]
        x, y, c = _my_place()
        me, sibling = (x, y, c), (x, y, 1 - c)
        chips = [(1 - x, y), (x, 1 - y), (1 - x, 1 - y)]

        def slot(px, py, pc):
            return out_ref.at[4 * px + 2 * py + pc]

        def copy(k, block, to, src=None):
            return pltpu.make_async_remote_copy(
                src_ref=slot(*block) if src is None else src,
                dst_ref=slot(*block),
                send_sem=send_sems.at[k],
                recv_sem=recv_sems.at[k],
                device_id=to,
                device_id_type=pl.DeviceIdType.MESH,
            )

        mine = pltpu.make_async_copy(x_ref, slot(*me), local_sem)
        mine.start()
        first = [copy(0, me, sibling, src=x_ref)]
        first += [copy(1 + j, me, (*chip, c), src=x_ref) for j, chip in enumerate(chips)]
        for cp in first:
            cp.start()
        passed = [copy(4 + j, (*chip, c), sibling) for j, chip in enumerate(chips)]
        for j, chip in enumerate(chips):
            copy(1 + j, (*chip, c), me).wait_recv()
            passed[j].start()
        copy(0, sibling, me).wait_recv()
        for j, chip in enumerate(chips):
            copy(4 + j, (*chip, 1 - c), me).wait_recv()
        for cp in first + passed:
            cp.wait_send()
        mine.wait()

    return pl.pallas_call(
        body,
        name=name,
        out_shape=jax.ShapeDtypeStruct((N_DEV,) + shape, dtype),
        in_specs=[pl.BlockSpec(memory_space=pl.ANY)] * (1 + len(after)),
        out_specs=pl.BlockSpec(memory_space=pl.ANY),
        scratch_shapes=[pltpu.SemaphoreType.DMA((7,)), pltpu.SemaphoreType.DMA((7,)), pltpu.SemaphoreType.DMA],
    )(shard, *after)


def _all_to_all(stacked, name, after=()):
    def body(x_ref, *rest):
        out_ref, send_sems, recv_sems, local_sem = rest[len(after):]
        x, y, c = _my_place()
        me = 4 * x + 2 * y + c
        mine = pltpu.make_async_copy(x_ref.at[me], out_ref.at[me], local_sem)
        mine.start()
        copies = []
        for k in range(1, N_DEV):
            px, py, pc = x ^ ((k >> 2) & 1), y ^ ((k >> 1) & 1), c ^ (k & 1)
            peer = 4 * px + 2 * py + pc
            copies.append(
                pltpu.make_async_remote_copy(
                    src_ref=x_ref.at[peer],
                    dst_ref=out_ref.at[me],
                    send_sem=send_sems.at[k - 1],
                    recv_sem=recv_sems.at[k - 1],
                    device_id=(px, py, pc),
                    device_id_type=pl.DeviceIdType.MESH,
                )
            )
        for cp in copies:
            cp.start()
        for cp in copies:
            cp.wait_recv()
        for cp in copies:
            cp.wait_send()
        mine.wait()

    return pl.pallas_call(
        body,
        name=name,
        out_shape=jax.ShapeDtypeStruct(stacked.shape, stacked.dtype),
        in_specs=[pl.BlockSpec(memory_space=pl.ANY)] * (1 + len(after)),
        out_specs=pl.BlockSpec(memory_space=pl.ANY),
        scratch_shapes=[pltpu.SemaphoreType.DMA((7,)), pltpu.SemaphoreType.DMA((7,)), pltpu.SemaphoreType.DMA],
    )(stacked, *after)


def _sum_blocks(stacked, name):
    _, r, c = stacked.shape
    tr = _pick(r, (256, 128, 64, 32, 16, 8))

    def body(x_ref, o_ref):
        s = [x_ref[j] for j in range(N_DEV)]
        o_ref[...] = ((s[0] + s[1]) + (s[2] + s[3])) + ((s[4] + s[5]) + (s[6] + s[7]))

    return pl.pallas_call(
        body,
        name=name,
        grid=(r // tr,),
        in_specs=[pl.BlockSpec((N_DEV, tr, c), lambda i: (0, i, 0))],
        out_specs=pl.BlockSpec((tr, c), lambda i: (i, 0)),
        out_shape=jax.ShapeDtypeStruct((r, c), stacked.dtype),
        compiler_params=_params("parallel"),
    )(stacked)


def _stack_cols(full):
    k, n8 = full.shape
    return full.reshape(k, N_DEV, n8 // N_DEV).transpose(1, 0, 2)


def _unstack_cols(stacked):
    j, k, n = stacked.shape
    return stacked.transpose(1, 0, 2).reshape(k, j * n)


def _split_cols(p, cuts):
    bounds = (0,) + tuple(cuts) + (p.shape[1],)

    @jax.custom_vjp
    def op(z):
        return tuple(z[:, lo:hi] for lo, hi in zip(bounds[:-1], bounds[1:]))

    op.defvjp(lambda z: (op(z), None), lambda _, cots: (jnp.concatenate(cots, axis=1),))
    return op(p)


def _gather_weight(shard, cols, name):
    g = _all_gather(shard.astype(BF16), name)
    return _unstack_cols(g) if cols else g.reshape(-1, shard.shape[1])


def _scatter_grad(full, cols, name, after=()):
    if cols:
        st = _stack_cols(full)
    else:
        st = full.reshape(N_DEV, full.shape[0] // N_DEV, full.shape[1])
    return _sum_blocks(_all_to_all(st, name + "_a2a", after), name + "_sum")


def _make_fsdp_linear(cols, name, unpad=None):
    @jax.custom_vjp
    def op(x, w_shard, w_full):
        return _matmul(x, w_full, "nn", name + "_fwd")

    def fwd(x, w_shard, w_full):
        return op(x, w_shard, w_full), (x, w_full)

    def bwd(res, dy):
        x, w = res
        dx = _matmul(dy, w, "nt", name + "_dx")
        dw = _matmul(x, dy, "tn", name + "_dw")
        dw = dw if unpad is None else unpad(dw)
        return dx, _scatter_grad(dw, cols, name + "_rs"), jnp.zeros_like(w)

    op.defvjp(fwd, bwd)
    return op


def _make_fsdp_param(name):
    @jax.custom_vjp
    def op(shard, full):
        return full

    def fwd(shard, full):
        return full, None

    def bwd(_, g):
        return _scatter_grad(g, True, name + "_rs"), jnp.zeros_like(g)

    op.defvjp(fwd, bwd)
    return op


def _allreduce_replicated(gs, name, after=()):
    flat = jnp.concatenate([g.reshape(-1) for g in gs])
    n = flat.shape[0]
    rows = -(-n // (256 * LANES)) * 256
    packed = jnp.pad(flat, (0, rows * LANES - n)).reshape(rows, LANES)
    total = _sum_blocks(_all_gather(packed, name + "_ag", after), name + "_sum").reshape(-1)
    out, off = [], 0
    for g in gs:
        out.append(total[off:off + g.size].reshape(g.shape))
        off += g.size
    return out


_HBM = pl.BlockSpec(memory_space=pltpu.HBM)
_SEM = pl.BlockSpec(memory_space=pltpu.SEMAPHORE)
_SIDE_EFFECT = pltpu.SideEffectType.DATAFLOW_SIDE_EFFECTING
_N_PEERS = N_DEV - 1


def _peer(k):
    x, y, c = _my_place()
    return x ^ ((k >> 2) & 1), y ^ ((k >> 1) & 1), c ^ (k & 1)


def _exchange_start(src, land_shape, gather, name, after=()):
    def body(src_ref, land_ref, *rest):
        send_sems, recv_sems, src_thru, land_thru, token = rest[len(after):]
        x, y, c = _my_place()
        me = 4 * x + 2 * y + c
        for k in range(1, N_DEV):
            px, py, pc = _peer(k)
            pltpu.make_async_remote_copy(
                src_ref=src_ref if gather else src_ref.at[4 * px + 2 * py + pc],
                dst_ref=land_ref.at[me] if gather else land_ref.at[k - 1],
                send_sem=send_sems.at[k - 1],
                recv_sem=recv_sems.at[k - 1],
                device_id=(px, py, pc),
                device_id_type=pl.DeviceIdType.MESH,
            ).start()
        token[...] = jnp.zeros_like(token)

    return pl.pallas_call(
        body,
        name=name,
        out_shape=(
            pltpu.SemaphoreType.DMA((_N_PEERS,)),
            pltpu.SemaphoreType.DMA((_N_PEERS,)),
            pltpu.HBM(src.shape, src.dtype),
            pltpu.HBM(land_shape, src.dtype),
            jax.ShapeDtypeStruct((8, LANES), F32),
        ),
        in_specs=(_HBM, _HBM) + (pl.BlockSpec(memory_space=pl.ANY),) * len(after),
        out_specs=(_SEM, _SEM, _HBM, _HBM, pl.BlockSpec(memory_space=pltpu.VMEM)),
        input_output_aliases={0: 2, 1: 3},
        compiler_params=pltpu.CompilerParams(has_side_effects=_SIDE_EFFECT),
    )(pltpu.with_memory_space_constraint(src, pltpu.HBM),
      pltpu.with_memory_space_constraint(lax.empty(land_shape, src.dtype), pltpu.HBM), *after)


def _gather_start_all(shards, name):
    n = len(shards)

    def body(*refs):
        srcs, lands = refs[:n], refs[n:2 * n]
        outs = refs[2 * n:]
        send_sems, recv_sems, token = outs[:n], outs[n:2 * n], outs[-1]
        x, y, c = _my_place()
        me = 4 * x + 2 * y + c
        for i in range(n):
            for k in range(1, N_DEV):
                pltpu.make_async_remote_copy(
                    src_ref=srcs[i],
                    dst_ref=lands[i].at[me],
                    send_sem=send_sems[i].at[k - 1],
                    recv_sem=recv_sems[i].at[k - 1],
                    device_id=_peer(k),
                    device_id_type=pl.DeviceIdType.MESH,
                ).start()
        token[...] = jnp.zeros_like(token)

    lands = [(N_DEV,) + s.shape for s in shards]
    sems = tuple(pltpu.SemaphoreType.DMA((_N_PEERS,)) for _ in range(2 * n))
    res = pl.pallas_call(
        body,
        name=name,
        out_shape=sems + tuple(pltpu.HBM(s.shape, s.dtype) for s in shards)
        + tuple(pltpu.HBM(ls, s.dtype) for ls, s in zip(lands, shards)) + (jax.ShapeDtypeStruct((8, LANES), F32),),
        in_specs=(_HBM,) * (2 * n),
        out_specs=(_SEM,) * (2 * n) + (_HBM,) * (2 * n) + (pl.BlockSpec(memory_space=pltpu.VMEM),),
        input_output_aliases={i: 2 * n + i for i in range(2 * n)},
        compiler_params=pltpu.CompilerParams(has_side_effects=_SIDE_EFFECT),
    )(*[pltpu.with_memory_space_constraint(s, pltpu.HBM) for s in shards],
      *[pltpu.with_memory_space_constraint(lax.empty(ls, s.dtype), pltpu.HBM) for ls, s in zip(lands, shards)])
    return [(res[i], res[n + i], res[2 * n + i], res[3 * n + i], res[-1]) for i in range(n)]


def _exchange_wait(handle, gather, after, name):
    send_sems, recv_sems, src_thru, land_thru, _ = handle

    def body(src_ref, land_ref, send_sems, recv_sems, after_ref, src_dead, got_ref):
        for k in range(1, N_DEV):
            cp = pltpu.make_async_remote_copy(
                src_ref=src_ref if gather else src_ref.at[k],
                dst_ref=land_ref.at[k - 1],
                send_sem=send_sems.at[k - 1],
                recv_sem=recv_sems.at[k - 1],
                device_id=_peer(k),
                device_id_type=pl.DeviceIdType.MESH,
            )
            cp.wait_send()
            cp.wait_recv()

    return pl.pallas_call(
        body,
        name=name,
        out_shape=(pltpu.HBM(src_thru.shape, src_thru.dtype), pltpu.HBM(land_thru.shape, land_thru.dtype)),
        in_specs=(_HBM, _HBM, _SEM, _SEM, pl.BlockSpec(memory_space=pl.ANY)),
        out_specs=(_HBM, _HBM),
        input_output_aliases={0: 0, 1: 1},
        compiler_params=pltpu.CompilerParams(has_side_effects=_SIDE_EFFECT),
    )(src_thru, land_thru, send_sems, recv_sems, after)[1]


def _sum_own_and_peers(own, land, name):
    r, c = own.shape
    tr = _pick(r, (256, 128, 64, 32, 16, 8))

    def body(o_ref, l_ref, out_ref):
        s = [l_ref[j] for j in range(_N_PEERS)]
        out_ref[...] = ((o_ref[...] + s[0]) + (s[1] + s[2])) + ((s[3] + s[4]) + (s[5] + s[6]))

    return pl.pallas_call(
        body,
        name=name,
        grid=(r // tr,),
        in_specs=[pl.BlockSpec((tr, c), lambda i: (i, 0)), pl.BlockSpec((_N_PEERS, tr, c), lambda i: (0, i, 0))],
        out_specs=pl.BlockSpec((tr, c), lambda i: (i, 0)),
        out_shape=jax.ShapeDtypeStruct((r, c), own.dtype),
        compiler_params=_params("parallel"),
    )(own, land)


class _Cols:
    def __init__(self, array, width, block):
        self.array, self.width, self.block = array, width, block
        self.shape, self.dtype = (array.shape[0], width), array.dtype


def _base(a):
    return a.array if isinstance(a, _Cols) else a


def _col_block(a):
    return a.block if isinstance(a, _Cols) else 0


def _make_rowwise(f, name, n_rows, n_tabs, n_pars):
    n_in = n_rows + n_tabs + n_pars

    def specs(args, tm):
        blocked = [pl.BlockSpec((tm, a.shape[1]), lambda i, blk=_col_block(a): (i, blk)) for a in args[: n_rows + n_tabs]]
        whole = [pl.BlockSpec(a.shape, lambda i: (0, 0)) for a in args[n_rows + n_tabs:]]
        return blocked + whole

    def out_struct(args, tm):
        blk = [jax.ShapeDtypeStruct((tm, a.shape[1]), a.dtype) for a in args[: n_rows + n_tabs]]
        blk += [jax.ShapeDtypeStruct(a.shape, a.dtype) for a in args[n_rows + n_tabs:]]
        return jax.eval_shape(f, *blk)

    def fwd_call(*args):
        r = args[0].shape[0]
        tm = _row_tile(r, max(a.shape[1] for a in args[:n_rows]))
        ro, so = out_struct(args, tm)

        def body(*refs):
            vals = [x[...] for x in refs[:n_in]]
            outs = refs[n_in:]
            rv, sv = f(*vals)
            for o, v in zip(outs[: len(ro)], rv):
                o[...] = v
            for o, v in zip(outs[len(ro):], sv):
                @pl.when(pl.program_id(0) == 0)
                def _(o=o, v=v):
                    o[...] = v

                @pl.when(pl.program_id(0) != 0)
                def _(o=o, v=v):
                    o[...] += v

        out_shape = [jax.ShapeDtypeStruct((r, s.shape[1]), s.dtype) for s in ro]
        out_shape += [jax.ShapeDtypeStruct(s.shape, s.dtype) for s in so]
        out_specs = [pl.BlockSpec((tm, s.shape[1]), lambda i: (i, 0)) for s in ro]
        out_specs += [pl.BlockSpec(s.shape, lambda i: (0, 0)) for s in so]
        res = pl.pallas_call(
            body,
            name=name + "_fwd",
            grid=(r // tm,),
            in_specs=specs(args, tm),
            out_specs=out_specs,
            out_shape=out_shape,
            compiler_params=_params("arbitrary" if so else "parallel"),
        )(*[_base(a) for a in args])
        return tuple(res[: len(ro)]), tuple(res[len(ro):])

    def bwd_call(args, cots, more=(), row_dtypes=None):
        r = args[0].shape[0]
        tm = _row_tile(r, max(a.shape[1] for a in args[:n_rows]))
        ro, so = out_struct(args, tm)
        crow, csum = cots
        rows, tabs, pars = args[:n_rows], args[n_rows:n_rows + n_tabs], args[n_rows + n_tabs:]
        n_c = len(crow) + len(csum)

        def body(*refs):
            vals = [x[...] for x in refs[:n_in]]
            cv = [x[...] for x in refs[n_in:n_in + n_c]]
            for x in refs[n_in + n_c:n_in + n_c + len(more)]:
                cv[0] = cv[0] + x[...]
            outs = refs[n_in + n_c + len(more):]
            tv = vals[n_rows:n_rows + n_tabs]

            def g(*dargs):
                return f(*dargs[:n_rows], *tv, *dargs[n_rows:])

            _, vjp = jax.vjp(g, *vals[:n_rows], *vals[n_rows + n_tabs:])
            d = vjp((tuple(cv[: len(crow)]), tuple(cv[len(crow):])))
            for o, v in zip(outs[:n_rows], d[:n_rows]):
                o[...] = v.astype(o.dtype)
            for o, v in zip(outs[n_rows:], d[n_rows:]):
                @pl.when(pl.program_id(0) == 0)
                def _(o=o, v=v):
                    o[...] = v

                @pl.when(pl.program_id(0) != 0)
                def _(o=o, v=v):
                    o[...] += v

        in_specs = specs(args, tm)
        in_specs += [pl.BlockSpec((tm, c.shape[1]), lambda i: (i, 0)) for c in crow]
        in_specs += [pl.BlockSpec(c.shape, lambda i: (0, 0)) for c in csum]
        in_specs += [pl.BlockSpec((tm, c.shape[1]), lambda i: (i, 0)) for c in more]
        out_shape = [jax.ShapeDtypeStruct(a.shape, dt) for a, dt in zip(rows, row_dtypes or [a.dtype for a in rows])]
        out_shape += [jax.ShapeDtypeStruct(a.shape, a.dtype) for a in pars]
        out_specs = [pl.BlockSpec((tm, a.shape[1]), lambda i: (i, 0)) for a in rows]
        out_specs += [pl.BlockSpec(a.shape, lambda i: (0, 0)) for a in pars]
        res = pl.pallas_call(
            body,
            name=name + "_bwd",
            grid=(r // tm,),
            in_specs=in_specs,
            out_specs=out_specs,
            out_shape=out_shape,
            compiler_params=_params("arbitrary" if pars else "parallel"),
        )(*[_base(a) for a in args], *crow, *csum, *more)
        return tuple(res[:n_rows]), tuple(res[n_rows:])

    @jax.custom_vjp
    def op(rows, tabs, pars):
        return fwd_call(*rows, *tabs, *pars)

    op.fwd_call, op.bwd_call = fwd_call, bwd_call

    def fwd(rows, tabs, pars):
        return fwd_call(*rows, *tabs, *pars), (rows, tabs, pars)

    def bwd(res, cots):
        rows, tabs, pars = res
        drows, dpars = bwd_call(tuple(rows) + tuple(tabs) + tuple(pars), cots)
        return drows, tuple(jnp.zeros_like(t) for t in tabs), dpars

    op.defvjp(fwd, bwd)
    return op


def _sigmoid(x):
    return 0.5 * (jnp.tanh(0.5 * x) + 1.0)


@jax.custom_jvp
def _softplus(x):
    e = jnp.exp(-jnp.abs(x))
    u = 1.0 + e
    log1p_e = jnp.where(u == 1.0, e, e * jnp.log(u) / jnp.where(u == 1.0, 1.0, u - 1.0))
    return jnp.maximum(x, 0.0) + log1p_e


@_softplus.defjvp
def _softplus_jvp(primals, tangents):
    (x,), (t,) = primals, tangents
    return _softplus(x), t * _sigmoid(x)


def _gelu(x):
    return 0.5 * x * (1.0 + jnp.tanh(math.sqrt(2.0 / math.pi) * (x + 0.044715 * (x * x * x))))


def _ln_res_f(h, mix, g, b):
    z = DN_ALPHA * h + mix
    mu = jnp.mean(z, axis=-1, keepdims=True)
    zc = z - mu
    var = jnp.mean(zc * zc, axis=-1, keepdims=True)
    return (zc * lax.rsqrt(var + EPS) * g + b,), ()


def _rmsnorm_f(x, g):
    return (x * lax.rsqrt(jnp.mean(x * x, axis=-1, keepdims=True) + EPS) * g,), ()


def _lru_gates_f(ga, gx, xc, b_a, b_x, lam):
    r = _sigmoid(ga + b_a)
    i = _sigmoid(gx + b_x)
    log_a = -LRU_C * r * _softplus(-lam)
    a = jnp.exp(log_a)
    one_minus_a2 = jnp.tanh(-log_a) * (jnp.exp(2.0 * log_a) + 1.0)
    return (a, jnp.sqrt(one_minus_a2) * (i * xc)), ()


def _lru_out_f(hh, p_gate):
    return (hh * _gelu(p_gate),), ()


def _rope_ret_f(q, k, cos2, sin2):
    d = cos2.shape[1]
    half = d // 2
    k_scale = d ** -0.5

    def rope(x):
        outs = []
        for h in range(x.shape[1] // d):
            xh = x[:, h * d:(h + 1) * d]
            rot = jnp.concatenate([xh[:, half:], xh[:, :half]], axis=1)
            outs.append(xh * cos2 + rot * sin2)
        return jnp.concatenate(outs, axis=1)

    return (rope(q), rope(k) * k_scale), ()


def _ret_out_f(o, g):
    d = o.shape[1] // RET_HEADS
    outs = []
    for h in range(RET_HEADS):
        oh = o[:, h * d:(h + 1) * d]
        outs.append(oh * lax.rsqrt(jnp.mean(oh * oh, axis=-1, keepdims=True) + EPS))
    y = jnp.concatenate(outs, axis=1)
    return (g * _sigmoid(g) * y,), ()


def _loss_f(y, t, mask):
    e = (y - t) * mask
    per_row = jnp.sum(e * e, axis=-1, keepdims=True) * (0.5 / y.shape[1])
    total = jnp.sum(per_row, axis=0, keepdims=True)
    return (), (jnp.broadcast_to(total, (1, LANES)),)


def _shift_down(x, s):
    if s == 0:
        return x
    t = x.shape[0]
    row = lax.broadcasted_iota(jnp.int32, x.shape, 0)
    return jnp.where(row >= s, pltpu.roll(x, s, 0), 0.0)


def _shift_up(x, s):
    if s == 0:
        return x
    t = x.shape[0]
    row = lax.broadcasted_iota(jnp.int32, x.shape, 0)
    return jnp.where(row < t - s, pltpu.roll(x, t - s, 0), 0.0)


def _conv_fwd(x, w, b, name):
    bsz, t, c = x.shape
    width = w.shape[0]

    def body(x_ref, w_ref, b_ref, y_ref):
        xv = x_ref[0]
        acc = jnp.broadcast_to(b_ref[...], xv.shape)
        for k in range(width):
            acc = acc + w_ref[k:k + 1, :] * _shift_down(xv, width - 1 - k)
        y_ref[0] = acc

    return pl.pallas_call(
        body,
        name=name,
        grid=(bsz, c // LANES),
        in_specs=[
            pl.BlockSpec((1, t, LANES), lambda i, j: (i, 0, j)),
            pl.BlockSpec((width, LANES), lambda i, j: (0, j)),
            pl.BlockSpec((1, LANES), lambda i, j: (0, j)),
        ],
        out_specs=pl.BlockSpec((1, t, LANES), lambda i, j: (i, 0, j)),
        out_shape=jax.ShapeDtypeStruct(x.shape, F32),
        compiler_params=_params("parallel", "parallel"),
    )(x, w, b)


def _conv_bwd(x, w, dy, name):
    bsz, t, c = x.shape
    width = w.shape[0]

    def body(x_ref, w_ref, dy_ref, dx_ref, dw_ref, db_ref):
        xv, g = x_ref[0], dy_ref[0]
        dx = jnp.zeros_like(xv)
        dws = []
        for k in range(width):
            s = width - 1 - k
            dx = dx + w_ref[k:k + 1, :] * _shift_up(g, s)
            dws.append(jnp.sum(g * _shift_down(xv, s), axis=0, keepdims=True))
        dx_ref[0] = dx
        dw = jnp.concatenate(dws, axis=0)
        db = jnp.sum(g, axis=0, keepdims=True)

        @pl.when(pl.program_id(1) == 0)
        def _():
            dw_ref[...] = dw
            db_ref[...] = db

        @pl.when(pl.program_id(1) != 0)
        def _():
            dw_ref[...] += dw
            db_ref[...] += db

    return pl.pallas_call(
        body,
        name=name,
        grid=(c // LANES, bsz),
        in_specs=[
            pl.BlockSpec((1, t, LANES), lambda j, i: (i, 0, j)),
            pl.BlockSpec((width, LANES), lambda j, i: (0, j)),
            pl.BlockSpec((1, t, LANES), lambda j, i: (i, 0, j)),
        ],
        out_specs=[
            pl.BlockSpec((1, t, LANES), lambda j, i: (i, 0, j)),
            pl.BlockSpec((width, LANES), lambda j, i: (0, j)),
            pl.BlockSpec((1, LANES), lambda j, i: (0, j)),
        ],
        out_shape=[
            jax.ShapeDtypeStruct(x.shape, F32),
            jax.ShapeDtypeStruct(w.shape, F32),
            jax.ShapeDtypeStruct((1, c), F32),
        ],
        compiler_params=_params("parallel", "arbitrary"),
    )(x, w, dy)


def _make_conv(name):
    @jax.custom_vjp
    def op(x, w, b):
        return _conv_fwd(x, w, b, name + "_fwd")

    def fwd(x, w, b):
        return op(x, w, b), (x, w)

    def bwd(res, dy):
        x, w = res
        return tuple(_conv_bwd(x, w, dy, name + "_bwd"))

    op.defvjp(fwd, bwd)
    return op


_SCAN_ROWS = 8


def _scan_fwd(a, b, name):
    bsz, t, c = a.shape
    cw = _pick(c, (4 * LANES, 2 * LANES, LANES))

    def body(a_ref, b_ref, h_ref):
        row = lax.broadcasted_iota(jnp.int32, (_SCAN_ROWS, cw), 0)

        def step(i, carry):
            r0 = pl.multiple_of(i * _SCAN_ROWS, _SCAN_ROWS)
            av, bv = a_ref[0, pl.ds(r0, _SCAN_ROWS), :], b_ref[0, pl.ds(r0, _SCAN_ROWS), :]
            for s in (1, 2, 4):
                a_sh = jnp.where(row >= s, pltpu.roll(av, s, 0), 1.0)
                b_sh = jnp.where(row >= s, pltpu.roll(bv, s, 0), 0.0)
                bv = av * b_sh + bv
                av = av * a_sh
            hv = bv + av * carry
            h_ref[0, pl.ds(r0, _SCAN_ROWS), :] = hv
            return hv[_SCAN_ROWS - 1:, :]

        lax.fori_loop(0, t // _SCAN_ROWS, step, jnp.zeros((1, cw), F32), unroll=2)

    spec = pl.BlockSpec((1, t, cw), lambda i, j: (i, 0, j))
    return pl.pallas_call(
        body,
        name=name,
        grid=(bsz, c // cw),
        in_specs=[spec, spec],
        out_specs=spec,
        out_shape=jax.ShapeDtypeStruct(a.shape, F32),
        compiler_params=_params("parallel", "parallel"),
    )(a, b)


def _scan_bwd(a, h, g, name):
    bsz, t, c = a.shape
    cw = _pick(c, (2 * LANES, LANES))

    def body(a_ref, h_ref, g_ref, da_ref, db_ref):
        rows = _SCAN_ROWS
        row = lax.broadcasted_iota(jnp.int32, (rows, cw), 0)
        n_tiles = t // rows

        def step(n, carry):
            lam_next, a_next = carry
            i = n_tiles - 1 - n
            r0 = pl.multiple_of(i * rows, rows)
            rp = pl.multiple_of(jnp.maximum(i - 1, 0) * rows, rows)
            av, gv, hv = a_ref[0, pl.ds(r0, rows), :], g_ref[0, pl.ds(r0, rows), :], h_ref[0, pl.ds(r0, rows), :]
            h_before = jnp.where(i > 0, h_ref[0, pl.ds(rp, rows), :][rows - 1:, :], 0.0)
            cv = jnp.where(row < rows - 1, pltpu.roll(av, rows - 1, 0), a_next)
            for s in (1, 2, 4):
                c_sh = jnp.where(row < rows - s, pltpu.roll(cv, rows - s, 0), 1.0)
                g_sh = jnp.where(row < rows - s, pltpu.roll(gv, rows - s, 0), 0.0)
                gv = cv * g_sh + gv
                cv = cv * c_sh
            lam = gv + cv * lam_next
            db_ref[0, pl.ds(r0, rows), :] = lam
            da_ref[0, pl.ds(r0, rows), :] = lam * jnp.where(row >= 1, pltpu.roll(hv, 1, 0), h_before)
            return lam[:1, :], av[:1, :]

        zero = jnp.zeros((1, cw), F32)
        lax.fori_loop(0, n_tiles, step, (zero, zero), unroll=2)

    spec = pl.BlockSpec((1, t, cw), lambda i, j: (i, 0, j))
    return pl.pallas_call(
        body,
        name=name,
        grid=(bsz, c // cw),
        in_specs=[spec, spec, spec],
        out_specs=[spec, spec],
        out_shape=[jax.ShapeDtypeStruct(a.shape, F32)] * 2,
        compiler_params=_params("parallel", "parallel"),
    )(a, h, g)


def _make_scan(name):
    @jax.custom_vjp
    def op(a, b):
        return _scan_fwd(a, b, name + "_fwd")

    def fwd(a, b):
        h = op(a, b)
        return h, (a, h)

    def bwd(res, g):
        a, h = res
        da, db = _scan_bwd(a, h, g, name + "_bwd")
        return da, db

    op.defvjp(fwd, bwd)
    return op


def _query_blocks(t):
    blocks, start = [], 0
    while start < t:
        rows = 2 * SEQ_BLOCK if start + 2 * SEQ_BLOCK <= t else SEQ_BLOCK
        blocks.append((start, rows))
        start += rows
    return blocks


def _attn_probs(q, k, start, scale):
    tq, tk = q.shape[0], k.shape[0]
    s = _dot(q, k, "nt") * scale
    qpos = start + lax.broadcasted_iota(jnp.int32, (tq, tk), 0)
    kpos = lax.broadcasted_iota(jnp.int32, (tq, tk), 1)
    s = jnp.where(kpos <= qpos, s, NEG_INF)
    e = jnp.exp(s - jnp.max(s, axis=-1, keepdims=True))
    return e / jnp.sum(e, axis=-1, keepdims=True)


_MLA_SCALE = (MLA_NOPE + MLA_ROPE) ** -0.5


def _attn_specs(t):
    head = pl.BlockSpec((1, t, LANES), lambda b, h: (b, 0, h))
    shared = pl.BlockSpec((1, t, LANES), lambda b, h: (b, 0, 0))
    return head, shared


def _attn_fwd(q, kv, kpe, name):
    bsz, t, hl = q.shape
    head, shared = _attn_specs(t)

    def body(q_ref, kv_ref, kpe_ref, o_ref, k_s, v_s):
        lane = lax.broadcasted_iota(jnp.int32, (t, LANES), 1)
        kvh = kv_ref[0]
        k_s[...] = jnp.where(lane < MLA_NOPE, kvh, kpe_ref[0]).astype(BF16)
        v_s[...] = kvh.astype(BF16)
        for start, rows in _query_blocks(t):
            n = start + rows
            p = _attn_probs(q_ref[0, start:n, :], k_s[:n, :], start, _MLA_SCALE)
            o_ref[0, start:n, :] = _dot(p, v_s[:n, :], "nn")

    return pl.pallas_call(
        body,
        name=name,
        grid=(bsz, hl // LANES),
        in_specs=[head, head, shared],
        out_specs=head,
        out_shape=jax.ShapeDtypeStruct(q.shape, F32),
        scratch_shapes=[pltpu.VMEM((t, LANES), BF16), pltpu.VMEM((t, LANES), BF16)],
        compiler_params=_params("parallel", "parallel"),
    )(q, kv, kpe)


def _attn_bwd(q, kv, kpe, do, name):
    bsz, t, hl = q.shape
    head, shared = _attn_specs(t)

    def body(q_ref, kv_ref, kpe_ref, do_ref, dq_ref, dkv_ref, dkpe_ref, k_s, v_s, dk_s, dv_s):
        lane = lax.broadcasted_iota(jnp.int32, (t, LANES), 1)
        kvh = kv_ref[0]
        k_s[...] = jnp.where(lane < MLA_NOPE, kvh, kpe_ref[0]).astype(BF16)
        v_s[...] = kvh.astype(BF16)
        for start, rows in reversed(_query_blocks(t)):
            n = start + rows
            qb = q_ref[0, start:n, :]
            dob = jnp.where(lane[:rows] >= MLA_NOPE, do_ref[0, start:n, :], 0.0)
            kk, vv = k_s[:n, :], v_s[:n, :]
            p = _attn_probs(qb, kk, start, _MLA_SCALE)
            dp = _dot(dob, vv, "nt")
            ds = p * (dp - jnp.sum(dp * p, axis=-1, keepdims=True)) * _MLA_SCALE
            dq_ref[0, start:n, :] = _dot(ds, kk, "nn")
            if n == t:
                dk_s[...] = _dot(ds, qb, "tn")
                dv_s[...] = _dot(p, dob, "tn")
            else:
                dk_s[:n, :] += _dot(ds, qb, "tn")
                dv_s[:n, :] += _dot(p, dob, "tn")
        dk = dk_s[...]
        dkv_ref[0] = jnp.where(lane < MLA_NOPE, dk, dv_s[...])
        dkpe = jnp.where(lane >= MLA_NOPE, dk, 0.0)

        @pl.when(pl.program_id(1) == 0)
        def _():
            dkpe_ref[0] = dkpe

        @pl.when(pl.program_id(1) != 0)
        def _():
            dkpe_ref[0] += dkpe

    return pl.pallas_call(
        body,
        name=name,
        grid=(bsz, hl // LANES),
        in_specs=[head, head, shared, head],
        out_specs=[head, head, shared],
        out_shape=[
            jax.ShapeDtypeStruct(q.shape, F32),
            jax.ShapeDtypeStruct(kv.shape, F32),
            jax.ShapeDtypeStruct(kpe.shape, F32),
        ],
        scratch_shapes=[pltpu.VMEM((t, LANES), BF16), pltpu.VMEM((t, LANES), BF16),
                        pltpu.VMEM((t, LANES), F32), pltpu.VMEM((t, LANES), F32)],
        compiler_params=_params("parallel", "arbitrary"),
    )(q, kv, kpe, do)


def _make_attention(name):
    @jax.custom_vjp
    def op(q, kv, kpe):
        return _attn_fwd(q, kv, kpe, name + "_fwd")

    def fwd(q, kv, kpe):
        return op(q, kv, kpe), (q, kv, kpe)

    def bwd(res, do):
        return tuple(_attn_bwd(*res, do, name + "_bwd"))

    op.defvjp(fwd, bwd)
    return op


_ROPE_SHIFT = MLA_ROPE // 2


def _rope_lanes_call(x, c, sm, sp, transpose, name):
    r, width = x.shape
    tm = _row_tile(r, width)

    def body(x_ref, c_ref, sm_ref, sp_ref, y_ref):
        cv, smv, spv = c_ref[...], sm_ref[...], sp_ref[...]
        for b in range(width // LANES):
            xb = x_ref[:, b * LANES:(b + 1) * LANES]
            if transpose:
                yb = xb * cv + pltpu.roll(xb * smv, _ROPE_SHIFT, 1) + pltpu.roll(xb * spv, LANES - _ROPE_SHIFT, 1)
            else:
                yb = xb * cv + pltpu.roll(xb, LANES - _ROPE_SHIFT, 1) * smv + pltpu.roll(xb, _ROPE_SHIFT, 1) * spv
            y_ref[:, b * LANES:(b + 1) * LANES] = yb

    tab = pl.BlockSpec((tm, LANES), lambda i: (i, 0))
    blk = pl.BlockSpec((tm, width), lambda i: (i, 0))
    return pl.pallas_call(
        body,
        name=name,
        grid=(r // tm,),
        in_specs=[blk, tab, tab, tab],
        out_specs=blk,
        out_shape=jax.ShapeDtypeStruct(x.shape, F32),
        compiler_params=_params("parallel"),
    )(x, c, sm, sp)


def _make_rope_lanes(name):
    @jax.custom_vjp
    def op(x, c, sm, sp):
        return _rope_lanes_call(x, c, sm, sp, False, name + "_fwd")

    def fwd(x, c, sm, sp):
        return op(x, c, sm, sp), (c, sm, sp)

    def bwd(res, dy):
        c, sm, sp = res
        return _rope_lanes_call(dy, c, sm, sp, True, name + "_bwd"), jnp.zeros_like(c), jnp.zeros_like(sm), jnp.zeros_like(sp)

    op.defvjp(fwd, bwd)
    return op


RET_KEY_CHUNK = 512


def _key_chunks(t):
    return [(c, min(RET_KEY_CHUNK, t - c)) for c in range(0, t, RET_KEY_CHUNK)]


def _decay(q0, k0, tq, tk, log_gamma):
    row = lax.broadcasted_iota(jnp.int32, (tq, 1), 0)
    col = (q0 - k0) - lax.broadcasted_iota(jnp.int32, (1, tk), 1)
    outer = jnp.exp(log_gamma * row.astype(F32)) * jnp.exp(log_gamma * col.astype(F32))
    return jnp.where(row + col >= 0, outer, 0.0)


def _ret_query_rows(t):
    return t // 4 if t % 32 == 0 else SEQ_BLOCK


def _ret_specs(t, dk, dv, heads):
    tq = _ret_query_rows(t)
    return (
        pl.BlockSpec(memory_space=pltpu.SMEM),
        pl.BlockSpec((1, tq, dk), lambda b, h, i: (b, i, h)),
        pl.BlockSpec((1, t, dk), lambda b, h, i: (b, 0, h)),
        pl.BlockSpec((1, t, dv), lambda b, h, i: (b, 0, h)),
        pl.BlockSpec((1, tq, dv), lambda b, h, i: (b, i, h)),
    )


def _ret_v_spec(t, dv, v_block0):
    return pl.BlockSpec((1, t, dv), lambda b, h, i: (b, 0, h + v_block0))


def _ret_fwd(lg, q, k, v, name, dv=None, v_block0=0):
    bsz, t, hdk = q.shape
    heads = lg.shape[0]
    dk, dv = hdk // heads, dv or v.shape[2] // heads
    lg_spec, q_spec, k_spec, v_spec, o_spec = _ret_specs(t, dk, dv, heads)
    tq = _ret_query_rows(t)

    def body(lg_ref, q_ref, k_ref, v_ref, o_ref):
        q0 = pl.program_id(2) * tq
        lgh = lg_ref[pl.program_id(1)]
        o_ref[0] = jnp.zeros((tq, dv), F32)
        for c0, cw in _key_chunks(t):
            @pl.when(c0 < q0 + tq)
            def _(c0=c0, cw=cw):
                d = _decay(q0, c0, tq, cw, lgh)
                a = _dot(q_ref[0], k_ref[0, c0:c0 + cw, :], "nt") * d
                o_ref[0] += _dot(a, v_ref[0, c0:c0 + cw, :], "nn")

    return pl.pallas_call(
        body,
        name=name,
        grid=(bsz, heads, t // tq),
        in_specs=[lg_spec, q_spec, k_spec, _ret_v_spec(t, dv, v_block0)],
        out_specs=o_spec,
        out_shape=jax.ShapeDtypeStruct((bsz, t, heads * dv), F32),
        compiler_params=_params("parallel", "parallel", "parallel"),
    )(lg, q, k, v)


def _ret_bwd(lg, q, k, v, do, name, v_block0=0):
    bsz, t, hdk = q.shape
    heads = lg.shape[0]
    dk, dv = hdk // heads, do.shape[2] // heads
    lg_spec, q_spec, k_spec, v_spec, o_spec = _ret_specs(t, dk, dv, heads)
    tq = _ret_query_rows(t)

    def body(lg_ref, q_ref, k_ref, v_ref, do_ref, dq_ref, dk_ref, dv_ref):
        q0 = pl.program_id(2) * tq
        lgh = lg_ref[pl.program_id(1)]

        @pl.when(pl.program_id(2) == 0)
        def _():
            dk_ref[0] = jnp.zeros((t, dk), F32)
            dv_ref[0] = jnp.zeros((t, dv), F32)

        dq_ref[0] = jnp.zeros((tq, dk), F32)
        for c0, cw in _key_chunks(t):
            @pl.when(c0 < q0 + tq)
            def _(c0=c0, cw=cw):
                qb, dob = q_ref[0], do_ref[0]
                kk, vv = k_ref[0, c0:c0 + cw, :], v_ref[0, c0:c0 + cw, :]
                d = _decay(q0, c0, tq, cw, lgh)
                a = _dot(qb, kk, "nt") * d
                ds = _dot(dob, vv, "nt") * d
                dq_ref[0] += _dot(ds, kk, "nn")
                dk_ref[0, c0:c0 + cw, :] += _dot(ds, qb, "tn")
                dv_ref[0, c0:c0 + cw, :] += _dot(a, dob, "tn")

    return pl.pallas_call(
        body,
        name=name,
        grid=(bsz, heads, t // tq),
        in_specs=[lg_spec, q_spec, k_spec, _ret_v_spec(t, dv, v_block0), o_spec],
        out_specs=[q_spec, k_spec, v_spec],
        out_shape=[
            jax.ShapeDtypeStruct(q.shape, F32),
            jax.ShapeDtypeStruct(k.shape, F32),
            jax.ShapeDtypeStruct(do.shape, F32),
        ],
        compiler_params=_params("parallel", "parallel", "arbitrary"),
    )(lg, q, k, v, do)


def _make_retention(name):
    @jax.custom_vjp
    def op(lg, q, k, v):
        return _ret_fwd(lg, q, k, v, name + "_fwd")

    def fwd(lg, q, k, v):
        return op(lg, q, k, v), (lg, q, k, v)

    def bwd(res, do):
        lg = res[0]
        return (jnp.zeros_like(lg),) + tuple(_ret_bwd(*res, do, name + "_bwd"))

    op.defvjp(fwd, bwd)
    return op


def _adamw(w, g, m, v, name):
    r, c = w.shape
    tr = _pick(r, (256, 128, 64, 32, 16, 8))

    def body(w_ref, g_ref, m_ref, v_ref, d_ref, nm_ref, nv_ref):
        gv = g_ref[...]
        nm = ADAM_B1 * m_ref[...] + (1.0 - ADAM_B1) * gv
        nv = ADAM_B2 * v_ref[...] + (1.0 - ADAM_B2) * (gv * gv)
        m_hat = nm / (1.0 - ADAM_B1 ** ADAM_STEP)
        v_hat = nv / (1.0 - ADAM_B2 ** ADAM_STEP)
        d_ref[...] = -ADAM_LR * (m_hat / (jnp.sqrt(v_hat) + ADAM_EPS) + ADAM_WD * w_ref[...])
        nm_ref[...] = nm
        nv_ref[...] = nv

    spec = pl.BlockSpec((tr, c), lambda i: (i, 0))
    return pl.pallas_call(
        body,
        name=name,
        grid=(r // tr,),
        in_specs=[spec] * 4,
        out_specs=[spec] * 3,
        out_shape=[jax.ShapeDtypeStruct((r, c), F32)] * 3,
        compiler_params=_params("parallel"),
    )(w, g, m, v)


def _rope_tables(t, half, reps):
    inv = ROPE_BASE ** (-jnp.arange(half, dtype=F32) / half)
    ang = jnp.arange(t, dtype=jnp.int32).astype(F32)[:, None] * inv[None, :]
    return jnp.tile(jnp.cos(ang), (1, reps)), jnp.tile(jnp.sin(ang), (1, reps))


def _padded_len(seq):
    return -(-(N_META + seq) // SEQ_BLOCK) * SEQ_BLOCK


def _embed(meta, x):
    bsz, seq, d = x.shape
    t = _padded_len(seq)
    return jnp.concatenate(
        [jnp.broadcast_to(meta[None], (bsz, N_META, d)), x, jnp.zeros((bsz, t - N_META - seq, d), F32)], axis=1
    ).reshape(bsz * t, d)


def _even_mixer(p, conv_w_shard, conv_b, w_rg_a, b_rg_a, w_rg_x, b_rg_x, lru_lambda, q_norm_g, w_uq_shard,
                kv_norm_g, w_ukv_shard, gathered, bsz):
    conv_w_full, w_uq_full, w_ukv_full = gathered
    r = p.shape[0]
    t = r // bsz

    def tile_rows(tab):
        return jnp.tile(tab, (bsz, 1))

    lru_w = w_rg_a.shape[2] * w_rg_a.shape[1]
    q_rank, kv_rank = q_norm_g.shape[1], kv_norm_g.shape[1]
    p_gate, p_rec, p_q, p_kv, p_kpe = _split_cols(
        p, (lru_w, 2 * lru_w, 2 * lru_w + q_rank, 2 * lru_w + q_rank + kv_rank))

    conv_w = _make_fsdp_param("conv_w")(conv_w_shard[0], conv_w_full)
    xc = _make_conv("conv")(p_rec.reshape(bsz, t, lru_w), conv_w, conv_b).reshape(r, lru_w)
    ga = _make_group_linear("rg_a")(xc, w_rg_a[0])
    gx = _make_group_linear("rg_x")(xc, w_rg_x[0])
    (a, bb), _ = _make_rowwise(_lru_gates_f, "lru_gates", 3, 0, 3)((ga, gx, xc), (), (b_rg_a, b_rg_x, lru_lambda))
    hh = _make_scan("lru_scan")(a.reshape(bsz, t, lru_w), bb.reshape(bsz, t, lru_w)).reshape(r, lru_w)
    (y_rec,), _ = _make_rowwise(_lru_out_f, "lru_out", 2, 0, 0)((hh, p_gate), (), ())

    (qn,), _ = _make_rowwise(_rmsnorm_f, "q_norm", 1, 0, 1)((p_q,), (), (q_norm_g,))
    (kvn,), _ = _make_rowwise(_rmsnorm_f, "kv_norm", 1, 0, 1)((p_kv,), (), (kv_norm_g,))
    d_head = MLA_NOPE + MLA_ROPE
    w_uq_pad = jnp.pad(w_uq_full.reshape(q_rank, MLA_HEADS, d_head), ((0, 0), (0, 0), (0, LANES - d_head)))

    def unpad_uq(dw):
        return dw.reshape(q_rank, MLA_HEADS, LANES)[:, :, :d_head].reshape(q_rank, MLA_HEADS * d_head)

    q = _make_fsdp_linear(True, "ev_uq", unpad_uq)(qn, w_uq_shard[0], w_uq_pad.reshape(q_rank, MLA_HEADS * LANES))
    kv = _make_fsdp_linear(True, "ev_ukv")(kvn, w_ukv_shard[0], w_ukv_full)
    half = MLA_ROPE // 2
    cos, sin = _rope_tables(t, half, 1)
    one, zero = jnp.ones((t, MLA_NOPE), F32), jnp.zeros((t, MLA_NOPE), F32)
    tail = LANES - MLA_NOPE - MLA_ROPE
    c_tab = tile_rows(jnp.concatenate([one, cos, cos, one[:, :tail]], axis=1))
    sm_tab = tile_rows(jnp.concatenate([zero, -sin, zero[:, :half + tail]], axis=1))
    sp_tab = tile_rows(jnp.concatenate([zero, zero[:, :half], sin, zero[:, :tail]], axis=1))
    q = _make_rope_lanes("rope_q")(q, c_tab, sm_tab, sp_tab)
    kpe = _make_rope_lanes("rope_k")(p_kpe, c_tab, sm_tab, sp_tab)
    o = _make_attention("mla")(q.reshape(bsz, t, -1), kv.reshape(bsz, t, -1), kpe.reshape(bsz, t, LANES))
    return jnp.concatenate([y_rec, o.reshape(r, -1)], axis=1)


def _odd_mixer_fwd(p, bsz):
    r, width = p.shape
    t = r // bsz
    qk = width // 6
    dk = qk // RET_HEADS
    cos2, sin2 = _rope_tables(t, dk // 2, 2)
    sin2 = jnp.concatenate([-sin2[:, :dk // 2], sin2[:, dk // 2:]], axis=1)
    rope_args = (_Cols(p, qk, 0), _Cols(p, qk, 1), jnp.tile(cos2, (bsz, 1)), jnp.tile(sin2, (bsz, 1)))
    (rq, rk), _ = _make_rowwise(_rope_ret_f, "rope_ret", 2, 2, 0).fwd_call(*rope_args)
    lg = jnp.log(1.0 - 2.0 ** (-5.0 - jnp.arange(RET_HEADS, dtype=F32)))
    ret_args = (lg, rq.reshape(bsz, t, qk), rk.reshape(bsz, t, qk), p.reshape(bsz, t, width))
    o = _ret_fwd(*ret_args, "ret_fwd", dv=2 * dk, v_block0=qk // dk)
    gate_args = (o.reshape(r, 2 * qk), _Cols(p, 2 * qk, 2))
    (y,), _ = _make_rowwise(_ret_out_f, "ret_out", 2, 0, 0).fwd_call(*gate_args)
    return y, (rope_args, ret_args, gate_args)


def _odd_mixer_bwd(res, dy):
    rope_args, ret_args, gate_args = res
    bsz, t, qk = ret_args[1].shape
    dk = qk // RET_HEADS
    (do, dg), _ = _make_rowwise(_ret_out_f, "ret_out", 2, 0, 0).bwd_call(gate_args, ((dy,), ()))
    drq, drk, dv = _ret_bwd(*ret_args, do.reshape(bsz, t, 2 * qk), "ret_bwd", v_block0=qk // dk)
    (dq, dkk), _ = _make_rowwise(_rope_ret_f, "rope_ret", 2, 2, 0).bwd_call(
        rope_args, ((drq.reshape(bsz * t, qk), drk.reshape(bsz * t, qk)), ()))
    return jnp.concatenate([dq, dkk, dv.reshape(bsz * t, 2 * qk), dg], axis=1).astype(BF16)


def _local_loss(h, target):
    bsz, seq, d = target.shape
    t = _padded_len(seq)
    t_real = N_META + seq
    pos = jnp.arange(t, dtype=jnp.int32)
    mask = jnp.tile(((pos >= N_META) & (pos < t_real)).astype(F32)[:, None], (bsz, 1))
    tgt = jnp.concatenate(
        [jnp.zeros((bsz, N_META, d), F32), target, jnp.zeros((bsz, t - t_real, d), F32)], axis=1).reshape(bsz * t, d)
    _, (total,) = _make_rowwise(_loss_f, "loss", 1, 2, 0)((h,), (tgt, mask), ())
    return total[0, 0]


_WEIGHTS = ("meta_tokens", "ev_w_in", "ev_conv_w", "ev_conv_b", "ev_w_rg_a", "ev_b_rg_a", "ev_w_rg_x", "ev_b_rg_x",
            "ev_lru_lambda", "ev_q_norm_g", "ev_w_uq", "ev_kv_norm_g", "ev_w_ukv", "ev_w_out", "od_w_in", "od_w_out",
            "ln_mix_g", "ln_mix_b", "mlp_w1", "mlp_w2", "ln_mlp_g", "ln_mlp_b")


def kernel(x, meta_tokens, ev_w_in, ev_conv_w, ev_conv_b, ev_w_rg_a, ev_b_rg_a, ev_w_rg_x, ev_b_rg_x, ev_lru_lambda, ev_q_norm_g, ev_w_uq, ev_kv_norm_g, ev_w_ukv, ev_w_out, od_w_in, od_w_out, ln_mix_g, ln_mix_b, mlp_w1, mlp_w2, ln_mlp_g, ln_mlp_b, loss_target, m_meta_tokens, m_ev_w_in, m_ev_conv_w, m_ev_conv_b, m_ev_w_rg_a, m_ev_b_rg_a, m_ev_w_rg_x, m_ev_b_rg_x, m_ev_lru_lambda, m_ev_q_norm_g, m_ev_w_uq, m_ev_kv_norm_g, m_ev_w_ukv, m_ev_w_out, m_od_w_in, m_od_w_out, m_ln_mix_g, m_ln_mix_b, m_mlp_w1, m_mlp_w2, m_ln_mlp_g, m_ln_mlp_b, v_meta_tokens, v_ev_w_in, v_ev_conv_w, v_ev_conv_b, v_ev_w_rg_a, v_ev_b_rg_a, v_ev_w_rg_x, v_ev_b_rg_x, v_ev_lru_lambda, v_ev_q_norm_g, v_ev_w_uq, v_ev_kv_norm_g, v_ev_w_ukv, v_ev_w_out, v_od_w_in, v_od_w_out, v_ln_mix_g, v_ln_mix_b, v_mlp_w1, v_mlp_w2, v_ln_mlp_g, v_ln_mlp_b):
    args = locals()
    weights = {n: args[n] for n in _WEIGHTS}
    bsz = x.shape[0]
    my_x, my_y, my_c = _my_place()
    me = 4 * my_x + 2 * my_y + my_c

    big = (("ev_in", ev_w_in[0], True), ("ev_out", ev_w_out[0], False), ("mlp0_w1", mlp_w1[0], True),
           ("mlp0_w2", mlp_w2[0], False), ("od_in", od_w_in[0], True), ("od_out", od_w_out[0], False),
           ("mlp1_w1", mlp_w1[1], True), ("mlp1_w2", mlp_w2[1], False))
    small_sharded = (("meta", meta_tokens, F32), ("conv_w", ev_conv_w[0], F32), ("ev_uq", ev_w_uq[0], BF16),
                     ("ev_ukv", ev_w_ukv[0], BF16))
    to_gather = (tuple((nm, s.astype(dt), True) for nm, s, dt in small_sharded)
                 + tuple((nm, s.astype(BF16), cols) for nm, s, cols in big))
    handles = _gather_start_all([s for _, s, _ in to_gather], "ag_start")
    gathers = {nm: (s, cols, h) for (nm, s, cols), h in zip(to_gather, handles)}
    gather_tokens = (handles[0][4],)

    def full_weight(nm, after):
        shard16, cols, handle = gathers[nm]
        land = _exchange_wait(handle, True, after, "ag_wait_" + nm)
        land = lax.dynamic_update_index_in_dim(land, shard16, me, 0)
        if cols and shard16.shape[1] % LANES == 0:
            return land, True
        return (_unstack_cols(land) if cols else land.reshape(-1, shard16.shape[1])), False

    meta_full, conv_w_full, w_uq_full, w_ukv_full = (
        _unstack_cols(lax.dynamic_update_index_in_dim(
            _exchange_wait(gathers[nm][2], True, gather_tokens[-1], "ag_wait_" + nm), gathers[nm][0], me, 0))
        for nm, _, _ in small_sharded)

    pending = []

    def linear_bwd(nm, x_in, w, dy, cols, unpad=None, **fused):
        w_full, w_stacked = w
        a_relu2 = fused.pop("a_relu2", False)
        if w_stacked:
            stacked = _matmul(x_in, dy, "tn", nm + "_dw", stacked=True, a_relu2=a_relu2)
            own = lax.dynamic_index_in_dim(stacked, me, 0, keepdims=False)
        else:
            dw = _matmul(x_in, dy, "tn", nm + "_dw", a_relu2=a_relu2)
            dw = dw if unpad is None else unpad(dw)
            n = dw.shape[1] // N_DEV
            if cols:
                stacked = _stack_cols(dw)
                own = lax.dynamic_slice_in_dim(dw, me * n, n, axis=1)
            else:
                stacked = dw.reshape(N_DEV, dw.shape[0] // N_DEV, dw.shape[1])
                own = lax.dynamic_index_in_dim(stacked, me, 0, keepdims=False)
        handle = _exchange_start(stacked, (_N_PEERS,) + stacked.shape[1:], False, "rs_start_" + nm)
        pending.append((nm, own, handle))
        return _matmul(dy, w_full, "nt", nm + "_dx", after=(handle[4],), stacked=w_stacked, **fused)

    def linear_fwd(nm, x_in, w, **fused):
        return _matmul(x_in, w[0], "nn", nm + "_fwd", stacked=w[1], **fused)

    def mlp_fwd(h, l):
        w1 = full_weight(f"mlp{l}_w1", h)
        u = linear_fwd(f"mlp{l}_w1", h, w1)
        w2 = full_weight(f"mlp{l}_w2", u)
        f = linear_fwd(f"mlp{l}_w2", u, w2, a_relu2=True)
        ln_args = (h, f, ln_mlp_g[l:l + 1], ln_mlp_b[l:l + 1])
        return ln_fwd(f"mlp{l}_ln", *ln_args), (h, w1, u, w2, ln_args)

    def mlp_bwd(dout, res, l):
        h, w1, u, w2, ln_args = res
        dh, df, dg, db = ln_bwd(f"mlp{l}_ln", ln_args, dout)
        du = linear_bwd(f"mlp{l}_w2", u, w2, df, False, a_relu2=True, relu2_bwd_of=u, out_dtype=BF16)
        return (dh, linear_bwd(f"mlp{l}_w1", h, w1, du, True)), dg, db

    def ln_fwd(nm, h, mix, g, b):
        return _make_rowwise(_ln_res_f, nm, 2, 0, 2).fwd_call(h, mix, g, b)[0][0]

    def ln_bwd(nm, ln_args, pieces):
        (dh, dmix), (dg, db) = _make_rowwise(_ln_res_f, nm, 2, 0, 2).bwd_call(
            ln_args, ((pieces[0],), ()), more=tuple(pieces[1:]), row_dtypes=(F32, BF16))
        return dh, dmix, dg, db

    h0, vjp_embed = jax.vjp(_embed, meta_full, x)
    n_in = ev_w_in.shape[2] * N_DEV
    kpe0, pad_lo, pad_hi = n_in - MLA_ROPE, MLA_NOPE, LANES - MLA_NOPE - MLA_ROPE
    w_in = full_weight("ev_in", h0)[0]
    zeros_in = jnp.zeros((w_in.shape[0], pad_lo), BF16)
    w_ev_in = (jnp.concatenate([w_in[:, :kpe0], zeros_in, w_in[:, kpe0:], zeros_in[:, :pad_hi]], axis=1), False)

    def unpad_in(dw):
        return jnp.concatenate([dw[:, :kpe0], dw[:, kpe0 + pad_lo:kpe0 + pad_lo + MLA_ROPE]], axis=1)

    p0 = _matmul(h0, w_ev_in[0], "nn", "ev_in_fwd", after=gather_tokens)
    small = (ev_conv_w, ev_conv_b, ev_w_rg_a, ev_b_rg_a, ev_w_rg_x, ev_b_rg_x, ev_lru_lambda, ev_q_norm_g, ev_w_uq,
             ev_kv_norm_g, ev_w_ukv)
    y0, vjp_even = jax.vjp(lambda p, *s: _even_mixer(p, *s, (conv_w_full, w_uq_full, w_ukv_full), bsz), p0, *small)
    w_out = full_weight("ev_out", y0)[0]
    lru_w, d_model = y0.shape[1] - MLA_HEADS * LANES, w_out.shape[1]
    w_att = w_out[lru_w:].reshape(MLA_HEADS, MLA_V, d_model)
    w_att = jnp.concatenate([jnp.zeros((MLA_HEADS, LANES - MLA_V, d_model), BF16), w_att], axis=1)
    w_ev_out = (jnp.concatenate([w_out[:lru_w], w_att.reshape(MLA_HEADS * LANES, d_model)], axis=0), False)

    def unpad_out(dw):
        d_att = dw[lru_w:].reshape(MLA_HEADS, LANES, d_model)[:, LANES - MLA_V:].reshape(MLA_HEADS * MLA_V, d_model)
        return jnp.concatenate([dw[:lru_w], d_att], axis=0)

    mix0 = linear_fwd("ev_out", y0, w_ev_out)
    ln0_args = (h0, mix0, ln_mix_g[0:1], ln_mix_b[0:1])
    h1 = ln_fwd("mix0_ln", *ln0_args)
    h2, res_mlp0 = mlp_fwd(h1, 0)
    w_od_in = full_weight("od_in", h2)
    p1 = linear_fwd("od_in", h2, w_od_in)
    y1, res_odd = _odd_mixer_fwd(p1, bsz)
    w_od_out = full_weight("od_out", y1)
    mix1 = linear_fwd("od_out", y1, w_od_out)
    ln1_args = (h2, mix1, ln_mix_g[1:2], ln_mix_b[1:2])
    h3 = ln_fwd("mix1_ln", *ln1_args)
    h4, res_mlp1 = mlp_fwd(h3, 1)
    loss_local, vjp_loss = jax.vjp(lambda h: _local_loss(h, loss_target), h4)

    dh4 = vjp_loss(jnp.ones((), F32))
    dh3, dg_mlp1, db_mlp1 = mlp_bwd(dh4, res_mlp1, 1)
    dh2, dmix1, dg_mix1, db_mix1 = ln_bwd("mix1_ln", ln1_args, dh3)
    dp1 = _odd_mixer_bwd(res_odd, linear_bwd("od_out", y1, w_od_out, dmix1, False))
    dh2 = (dh2, linear_bwd("od_in", h2, w_od_in, dp1, True))
    dh1, dg_mlp0, db_mlp0 = mlp_bwd(dh2, res_mlp0, 0)
    dh0, dmix0, dg_mix0, db_mix0 = ln_bwd("mix0_ln", ln0_args, dh1)
    dp0, *dsmall = vjp_even(linear_bwd("ev_out", y0, w_ev_out, dmix0, False, unpad=unpad_out))
    dh0 = dh0 + linear_bwd("ev_in", h0, w_ev_in, dp0.astype(BF16), True, unpad=unpad_in)
    g_meta_full, grad_x = vjp_embed(dh0)
    (g_conv_w, g_conv_b, g_w_rg_a, g_b_rg_a, g_w_rg_x, g_b_rg_x, g_lambda, g_q_norm, g_w_uq, g_kv_norm, g_w_ukv) = dsmall

    after, summed = grad_x, {}
    for nm, own, handle in pending:
        land = _exchange_wait(handle, False, after, "rs_wait_" + nm)
        summed[nm] = after = _sum_own_and_peers(own, land, "rs_sum_" + nm)

    g_meta = _scatter_grad(g_meta_full, True, "meta_rs", after=(after,))
    rep_names = ("ev_conv_b", "ev_w_rg_a", "ev_b_rg_a", "ev_w_rg_x", "ev_b_rg_x", "ev_lru_lambda", "ev_q_norm_g",
                 "ev_kv_norm_g", "ln_mix_g", "ln_mix_b", "ln_mlp_g", "ln_mlp_b")
    rep_local = (g_conv_b, g_w_rg_a, g_b_rg_a, g_w_rg_x, g_b_rg_x, g_lambda, g_q_norm, g_kv_norm,
                 jnp.concatenate([dg_mix0, dg_mix1]), jnp.concatenate([db_mix0, db_mix1]),
                 jnp.concatenate([dg_mlp0, dg_mlp1]), jnp.concatenate([db_mlp0, db_mlp1]))
    grad_w = dict(zip(rep_names, _allreduce_replicated(rep_local, "rep", after=(g_meta,))))
    grad_w.update(meta_tokens=g_meta, ev_conv_w=g_conv_w, ev_w_uq=g_w_uq, ev_w_ukv=g_w_ukv)
    grad_w.update(ev_w_in=summed["ev_in"][None], ev_w_out=summed["ev_out"][None], od_w_in=summed["od_in"][None],
                  od_w_out=summed["od_out"][None], mlp_w1=jnp.stack([summed["mlp0_w1"], summed["mlp1_w1"]]),
                  mlp_w2=jnp.stack([summed["mlp0_w2"], summed["mlp1_w2"]]))

    loss = lax.psum(loss_local, MESH_AXES)
    delta, new_m, new_v = {}, {}, {}
    for n in _WEIGHTS:
        w, g, m, v = weights[n], grad_w[n], args["m_" + n], args["v_" + n]
        two_d = (-1, w.shape[-1])
        d2, m2, v2 = _adamw(w.reshape(two_d), g.reshape(two_d), m.reshape(two_d), v.reshape(two_d), "adamw_" + n)
        delta[n], new_m[n], new_v[n] = d2.reshape(w.shape), m2.reshape(w.shape), v2.reshape(w.shape)
    return (loss, grad_x, *[grad_w[n] for n in _WEIGHTS], *[delta[n] for n in _WEIGHTS],
            *[new_m[n] for n in _WEIGHTS], *[new_v[n] for n in _WEIGHTS])
```

```python
import functools
import math

import jax
import jax.numpy as jnp
from jax import lax
from jax.experimental import pallas as pl
from jax.experimental.pallas import tpu as pltpu

F32 = jnp.float32
BF16 = jnp.bfloat16

N_DEV = 8
MESH_AXES = ("x", "y", "c")
LANES = 128
SEQ_BLOCK = 128

N_META = 16
LRU_C = 8.0
MLA_HEADS = 8
MLA_NOPE = 64
MLA_ROPE = 32
MLA_V = 64
RET_HEADS = 4
ROPE_BASE = 10000.0
DEPTH = 2
DN_ALPHA = (2 * DEPTH) ** 0.25
EPS = 1e-5
NEG_INF = -1e30

ADAM_LR = 0.001
ADAM_B1 = 0.9
ADAM_B2 = 0.999
ADAM_EPS = 1e-08
ADAM_WD = 0.01
ADAM_STEP = 10

VMEM_LIMIT = 56 * 1024 * 1024


def _params(*sem):
    return pltpu.CompilerParams(dimension_semantics=sem, vmem_limit_bytes=VMEM_LIMIT)


def _pick(n, cands):
    for c in cands:
        if n % c == 0:
            return c
    return n


def _row_tile(r, width):
    cands = (256, 128, 64, 32, 16, 8) if width <= 1024 else (128, 64, 32, 16, 8)
    return _pick(r, cands)


_DIMS = {"nn": (((1,), (0,)), ((), ())), "nt": (((1,), (1,)), ((), ())), "tn": (((0,), (0,)), ((), ()))}


def _dot(a, b, mode):
    return lax.dot_general(a.astype(BF16), b.astype(BF16), _DIMS[mode], preferred_element_type=F32)


def _matmul(a, b, mode, name, after=(), stacked=False, a_relu2=False, relu2_bwd_of=None, out_dtype=F32,
            relu2_copy=False):
    if stacked:
        n_blk = b.shape[2] if mode != "tn" else b.shape[1] // N_DEV
    if mode == "nn":
        (m, k), n = a.shape, (N_DEV * n_blk if stacked else b.shape[1])
    elif mode == "nt":
        (m, k), n = a.shape, (b.shape[1] if stacked else b.shape[0])
    else:
        (k, m), n = a.shape, b.shape[1]
    tm = _pick(m, (1088, 1024, 544, 512, 272, 256, 128, 64, 32, 16, 8))
    tn = _pick(n, (1024, 512, 256, 128))
    tk = _pick(k, (1088, 1024, 544, 512, 272, 256, 128))
    kb = 2
    if stacked and mode == "nn":
        tn = n_blk
    if stacked and mode == "tn":
        tn = kb * n_blk
    if stacked and mode == "nt":
        tk = kb * n_blk
    nk = k // tk
    assert out_dtype == F32 or (nk == 1 and not (stacked and mode == "tn")), "narrow results need a single k step"
    assert not relu2_copy or (nk == 1 and mode == "nn"), "the relu^2 copy is taken from a finished tile"

    out_spec = pl.BlockSpec((tm, tn), lambda i, j, kk: (i, j))
    out_shape = jax.ShapeDtypeStruct((m, n), out_dtype)
    if mode == "nn":
        a_spec = pl.BlockSpec((tm, tk), lambda i, j, kk: (i, kk))
        b_spec = pl.BlockSpec((tk, tn), lambda i, j, kk: (kk, j))
        if stacked:
            b_spec = pl.BlockSpec((None, tk, tn), lambda i, j, kk: (j, kk, 0))
    elif mode == "nt":
        a_spec = pl.BlockSpec((tm, tk), lambda i, j, kk: (i, kk))
        b_spec = pl.BlockSpec((tn, tk), lambda i, j, kk: (j, kk))
        if stacked:
            b_spec = pl.BlockSpec((kb, tn, n_blk), lambda i, j, kk: (kk, j, 0))
    else:
        a_spec = pl.BlockSpec((tk, tm), lambda i, j, kk: (kk, i))
        b_spec = pl.BlockSpec((tk, tn), lambda i, j, kk: (kk, j))
        if stacked:
            out_spec = pl.BlockSpec((kb, tm, n_blk), lambda i, j, kk: (j, i, 0))
            out_shape = jax.ShapeDtypeStruct((N_DEV, m, n_blk), F32)
    extra = [] if relu2_bwd_of is None else [relu2_bwd_of]
    extra_specs = [pl.BlockSpec((tm, tn), lambda i, j, kk: (i, j))] * len(extra)

    def body(a_ref, b_ref, *rest):
        o_ref = rest[-2] if relu2_copy else rest[-1]
        kk = pl.program_id(2)
        av = a_ref[...]
        if a_relu2:
            av = jnp.maximum(av, 0.0)
            av = av * av
        if stacked and mode == "nt":
            part = _dot(av[:, :n_blk], b_ref[0], mode)
            for q in range(1, kb):
                part = part + _dot(av[:, q * n_blk:(q + 1) * n_blk], b_ref[q], mode)
        else:
            part = _dot(av, b_ref[...], mode)
        if stacked and mode == "tn":
            part = jnp.stack([part[:, q * n_blk:(q + 1) * n_blk] for q in range(kb)])
        if out_dtype != F32:
            if relu2_bwd_of is not None:
                part = part * (2.0 * jnp.maximum(rest[0][...], 0.0))
            o_ref[...] = part.astype(out_dtype)
            return
        if relu2_copy:
            r = jnp.maximum(part, 0.0)
            rest[-1][...] = (r * r).astype(BF16)

        @pl.when(kk == 0)
        def _():
            o_ref[...] = part

        @pl.when(kk != 0)
        def _():
            o_ref[...] += part

        if relu2_bwd_of is not None:
            @pl.when(kk == nk - 1)
            def _():
                o_ref[...] *= 2.0 * jnp.maximum(rest[0][...], 0.0)

    return pl.pallas_call(
        body,
        name=name,
        grid=(m // tm, n // tn, nk),
        in_specs=[a_spec, b_spec] + extra_specs + [pl.BlockSpec(memory_space=pl.ANY)] * len(after),
        out_specs=[out_spec, out_spec] if relu2_copy else out_spec,
        out_shape=[out_shape, jax.ShapeDtypeStruct((m, n), BF16)] if relu2_copy else out_shape,
        compiler_params=_params("parallel", "parallel", "arbitrary"),
    )(a, b, *extra, *after)


def _group_matmul(a, w, mode, name):
    if mode in ("nn", "nt"):
        g, dk, dn = w.shape
        m = a.shape[0]
        d_in, d_out = (dk, dn) if mode == "nn" else (dn, dk)
        tm = _pick(m, (1088, 1024, 544, 512, 272, 256, 128, 64, 32, 16, 8))

        def body(a_ref, w_ref, o_ref):
            o_ref[...] = _dot(a_ref[...], w_ref[0], mode)

        return pl.pallas_call(
            body,
            name=name,
            grid=(g, m // tm),
            in_specs=[pl.BlockSpec((tm, d_in), lambda h, i: (i, h)), pl.BlockSpec((1, dk, dn), lambda h, i: (h, 0, 0))],
            out_specs=pl.BlockSpec((tm, d_out), lambda h, i: (i, h)),
            out_shape=jax.ShapeDtypeStruct((m, g * d_out), F32),
            compiler_params=_params("parallel", "parallel"),
        )(a, w)
    b = w
    m = a.shape[0]
    dk = dn = LANES
    g = a.shape[1] // dk
    tm = _pick(m, (1088, 1024, 544, 512, 272, 256, 128, 64, 32, 16, 8))

    def body(a_ref, b_ref, o_ref):
        part = _dot(a_ref[...], b_ref[...], "tn")

        @pl.when(pl.program_id(1) == 0)
        def _():
            o_ref[0] = part

        @pl.when(pl.program_id(1) != 0)
        def _():
            o_ref[0] += part

    return pl.pallas_call(
        body,
        name=name,
        grid=(g, m // tm),
        in_specs=[pl.BlockSpec((tm, dk), lambda h, i: (i, h)), pl.BlockSpec((tm, dn), lambda h, i: (i, h))],
        out_specs=pl.BlockSpec((1, dk, dn), lambda h, i: (h, 0, 0)),
        out_shape=jax.ShapeDtypeStruct((g, dk, dn), F32),
        compiler_params=_params("parallel", "arbitrary"),
    )(a, b)


def _make_group_linear(name):
    @jax.custom_vjp
    def op(x, w):
        return _group_matmul(x, w, "nn", name + "_fwd")

    def fwd(x, w):
        return op(x, w), (x, w)

    def bwd(res, dy):
        x, w = res
        return _group_matmul(dy, w, "nt", name + "_dx"), _group_matmul(x, dy, "tn", name + "_dw")

    op.defvjp(fwd, bwd)
    return op


def _my_place():
    return lax.axis_index("x"), lax.axis_index("y"), lax.axis_index("c")


def _all_gather(shard, name, after=()):
    shape, dtype = shard.shape, shard.dtype

    def body(x_ref, *rest):
        out_ref, send_sems, recv_sems, local_sem = rest[len(after):]
        x, y, c = _my_place()
        me, sibling = (x, y, c), (x, y, 1 - c)
        chips = [(1 - x, y), (x, 1 - y), (1 - x, 1 - y)]

        def slot(px, py, pc):
            return out_ref.at[4 * px + 2 * py + pc]

        def copy(k, block, to, src=None):
            return pltpu.make_async_remote_copy(
                src_ref=slot(*block) if src is None else src,
                dst_ref=slot(*block),
                send_sem=send_sems.at[k],
                recv_sem=recv_sems.at[k],
                device_id=to,
                device_id_type=pl.DeviceIdType.MESH,
            )

        mine = pltpu.make_async_copy(x_ref, slot(*me), local_sem)
        mine.start()
        first = [copy(0, me, sibling, src=x_ref)]
        first += [copy(1 + j, me, (*chip, c), src=x_ref) for j, chip in enumerate(chips)]
        for cp in first:
            cp.start()
        passed = [copy(4 + j, (*chip, c), sibling) for j, chip in enumerate(chips)]
        for j, chip in enumerate(chips):
            copy(1 + j, (*chip, c), me).wait_recv()
            passed[j].start()
        copy(0, sibling, me).wait_recv()
        for j, chip in enumerate(chips):
            copy(4 + j, (*chip, 1 - c), me).wait_recv()
        for cp in first + passed:
            cp.wait_send()
        mine.wait()

    return pl.pallas_call(
        body,
        name=name,
        out_shape=jax.ShapeDtypeStruct((N_DEV,) + shape, dtype),
        in_specs=[pl.BlockSpec(memory_space=pl.ANY)] * (1 + len(after)),
        out_specs=pl.BlockSpec(memory_space=pl.ANY),
        scratch_shapes=[pltpu.SemaphoreType.DMA((7,)), pltpu.SemaphoreType.DMA((7,)), pltpu.SemaphoreType.DMA],
    )(shard, *after)


def _all_to_all(stacked, name, after=()):
    def body(x_ref, *rest):
        out_ref, send_sems, recv_sems, local_sem = rest[len(after):]
        x, y, c = _my_place()
        me = 4 * x + 2 * y + c
        mine = pltpu.make_async_copy(x_ref.at[me], out_ref.at[me], local_sem)
        mine.start()
        copies = []
        for k in range(1, N_DEV):
            px, py, pc = x ^ ((k >> 2) & 1), y ^ ((k >> 1) & 1), c ^ (k & 1)
            peer = 4 * px + 2 * py + pc
            copies.append(
                pltpu.make_async_remote_copy(
                    src_ref=x_ref.at[peer],
                    dst_ref=out_ref.at[me],
                    send_sem=send_sems.at[k - 1],
                    recv_sem=recv_sems.at[k - 1],
                    device_id=(px, py, pc),
                    device_id_type=pl.DeviceIdType.MESH,
                )
            )
        for cp in copies:
            cp.start()
        for cp in copies:
            cp.wait_recv()
        for cp in copies:
            cp.wait_send()
        mine.wait()

    return pl.pallas_call(
        body,
        name=name,
        out_shape=jax.ShapeDtypeStruct(stacked.shape, stacked.dtype),
        in_specs=[pl.BlockSpec(memory_space=pl.ANY)] * (1 + len(after)),
        out_specs=pl.BlockSpec(memory_space=pl.ANY),
        scratch_shapes=[pltpu.SemaphoreType.DMA((7,)), pltpu.SemaphoreType.DMA((7,)), pltpu.SemaphoreType.DMA],
    )(stacked, *after)


def _sum_blocks(stacked, name):
    _, r, c = stacked.shape
    tr = _pick(r, (256, 128, 64, 32, 16, 8))

    def body(x_ref, o_ref):
        s = [x_ref[j] for j in range(N_DEV)]
        o_ref[...] = ((s[0] + s[1]) + (s[2] + s[3])) + ((s[4] + s[5]) + (s[6] + s[7]))

    return pl.pallas_call(
        body,
        name=name,
        grid=(r // tr,),
        in_specs=[pl.BlockSpec((N_DEV, tr, c), lambda i: (0, i, 0))],
        out_specs=pl.BlockSpec((tr, c), lambda i: (i, 0)),
        out_shape=jax.ShapeDtypeStruct((r, c), stacked.dtype),
        compiler_params=_params("parallel"),
    )(stacked)


def _stack_cols(full):
    k, n8 = full.shape
    return full.reshape(k, N_DEV, n8 // N_DEV).transpose(1, 0, 2)


def _unstack_cols(stacked):
    j, k, n = stacked.shape
    return stacked.transpose(1, 0, 2).reshape(k, j * n)


def _split_cols(p, cuts):
    bounds = (0,) + tuple(cuts) + (p.shape[1],)

    @jax.custom_vjp
    def op(z):
        return tuple(z[:, lo:hi] for lo, hi in zip(bounds[:-1], bounds[1:]))

    op.defvjp(lambda z: (op(z), None), lambda _, cots: (jnp.concatenate(cots, axis=1),))
    return op(p)


def _gather_weight(shard, cols, name):
    g = _all_gather(shard.astype(BF16), name)
    return _unstack_cols(g) if cols else g.reshape(-1, shard.shape[1])


def _scatter_grad(full, cols, name, after=()):
    if cols:
        st = _stack_cols(full)
    else:
        st = full.reshape(N_DEV, full.shape[0] // N_DEV, full.shape[1])
    return _sum_blocks(_all_to_all(st, name + "_a2a", after), name + "_sum")


def _make_fsdp_linear(cols, name, unpad=None):
    @jax.custom_vjp
    def op(x, w_shard, w_full):
        return _matmul(x, w_full, "nn", name + "_fwd")

    def fwd(x, w_shard, w_full):
        return op(x, w_shard, w_full), (x, w_full)

    def bwd(res, dy):
        x, w = res
        dx = _matmul(dy, w, "nt", name + "_dx")
        dw = _matmul(x, dy, "tn", name + "_dw")
        dw = dw if unpad is None else unpad(dw)
        return dx, _scatter_grad(dw, cols, name + "_rs"), jnp.zeros_like(w)

    op.defvjp(fwd, bwd)
    return op


def _make_fsdp_param(name):
    @jax.custom_vjp
    def op(shard, full):
        return full

    def fwd(shard, full):
        return full, None

    def bwd(_, g):
        return _scatter_grad(g, True, name + "_rs"), jnp.zeros_like(g)

    op.defvjp(fwd, bwd)
    return op


def _allreduce_replicated(gs, name, after=()):
    flat = jnp.concatenate([g.reshape(-1) for g in gs])
    n = flat.shape[0]
    rows = -(-n // (256 * LANES)) * 256
    packed = jnp.pad(flat, (0, rows * LANES - n)).reshape(rows, LANES)
    total = _sum_blocks(_all_gather(packed, name + "_ag", after), name + "_sum").reshape(-1)
    out, off = [], 0
    for g in gs:
        out.append(total[off:off + g.size].reshape(g.shape))
        off += g.size
    return out


_HBM = pl.BlockSpec(memory_space=pltpu.HBM)
_SEM = pl.BlockSpec(memory_space=pltpu.SEMAPHORE)
_SIDE_EFFECT = pltpu.SideEffectType.DATAFLOW_SIDE_EFFECTING
_N_PEERS = N_DEV - 1


def _peer(k):
    x, y, c = _my_place()
    return x ^ ((k >> 2) & 1), y ^ ((k >> 1) & 1), c ^ (k & 1)


def _exchange_start(src, land_shape, gather, name, after=()):
    def body(src_ref, land_ref, *rest):
        send_sems, recv_sems, src_thru, land_thru, token = rest[len(after):]
        x, y, c = _my_place()
        me = 4 * x + 2 * y + c
        for k in range(1, N_DEV):
            px, py, pc = _peer(k)
            pltpu.make_async_remote_copy(
                src_ref=src_ref if gather else src_ref.at[4 * px + 2 * py + pc],
                dst_ref=land_ref.at[me] if gather else land_ref.at[k - 1],
                send_sem=send_sems.at[k - 1],
                recv_sem=recv_sems.at[k - 1],
                device_id=(px, py, pc),
                device_id_type=pl.DeviceIdType.MESH,
            ).start()
        token[...] = jnp.zeros_like(token)

    return pl.pallas_call(
        body,
        name=name,
        out_shape=(
            pltpu.SemaphoreType.DMA((_N_PEERS,)),
            pltpu.SemaphoreType.DMA((_N_PEERS,)),
            pltpu.HBM(src.shape, src.dtype),
            pltpu.HBM(land_shape, src.dtype),
            jax.ShapeDtypeStruct((8, LANES), F32),
        ),
        in_specs=(_HBM, _HBM) + (pl.BlockSpec(memory_space=pl.ANY),) * len(after),
        out_specs=(_SEM, _SEM, _HBM, _HBM, pl.BlockSpec(memory_space=pltpu.VMEM)),
        input_output_aliases={0: 2, 1: 3},
        compiler_params=pltpu.CompilerParams(has_side_effects=_SIDE_EFFECT),
    )(pltpu.with_memory_space_constraint(src, pltpu.HBM),
      pltpu.with_memory_space_constraint(lax.empty(land_shape, src.dtype), pltpu.HBM), *after)


def _gather_start_all(shards, name):
    n = len(shards)

    def body(*refs):
        srcs, lands = refs[:n], refs[n:2 * n]
        outs = refs[2 * n:]
        send_sems, recv_sems, token = outs[:n], outs[n:2 * n], outs[-1]
        x, y, c = _my_place()
        me = 4 * x + 2 * y + c
        for i in range(n):
            for k in range(1, N_DEV):
                pltpu.make_async_remote_copy(
                    src_ref=srcs[i],
                    dst_ref=lands[i].at[me],
                    send_sem=send_sems[i].at[k - 1],
                    recv_sem=recv_sems[i].at[k - 1],
                    device_id=_peer(k),
                    device_id_type=pl.DeviceIdType.MESH,
                ).start()
        token[...] = jnp.zeros_like(token)

    lands = [(N_DEV,) + s.shape for s in shards]
    sems = tuple(pltpu.SemaphoreType.DMA((_N_PEERS,)) for _ in range(2 * n))
    res = pl.pallas_call(
        body,
        name=name,
        out_shape=sems + tuple(pltpu.HBM(s.shape, s.dtype) for s in shards)
        + tuple(pltpu.HBM(ls, s.dtype) for ls, s in zip(lands, shards)) + (jax.ShapeDtypeStruct((8, LANES), F32),),
        in_specs=(_HBM,) * (2 * n),
        out_specs=(_SEM,) * (2 * n) + (_HBM,) * (2 * n) + (pl.BlockSpec(memory_space=pltpu.VMEM),),
        input_output_aliases={i: 2 * n + i for i in range(2 * n)},
        compiler_params=pltpu.CompilerParams(has_side_effects=_SIDE_EFFECT),
    )(*[pltpu.with_memory_space_constraint(s, pltpu.HBM) for s in shards],
      *[pltpu.with_memory_space_constraint(lax.empty(ls, s.dtype), pltpu.HBM) for ls, s in zip(lands, shards)])
    return [(res[i], res[n + i], res[2 * n + i], res[3 * n + i], res[-1]) for i in range(n)]


def _exchange_wait(handle, gather, after, name):
    send_sems, recv_sems, src_thru, land_thru, _ = handle

    def body(src_ref, land_ref, send_sems, recv_sems, after_ref, src_dead, got_ref):
        for k in range(1, N_DEV):
            cp = pltpu.make_async_remote_copy(
                src_ref=src_ref if gather else src_ref.at[k],
                dst_ref=land_ref.at[k - 1],
                send_sem=send_sems.at[k - 1],
                recv_sem=recv_sems.at[k - 1],
                device_id=_peer(k),
                device_id_type=pl.DeviceIdType.MESH,
            )
            cp.wait_send()
            cp.wait_recv()

    return pl.pallas_call(
        body,
        name=name,
        out_shape=(pltpu.HBM(src_thru.shape, src_thru.dtype), pltpu.HBM(land_thru.shape, land_thru.dtype)),
        in_specs=(_HBM, _HBM, _SEM, _SEM, pl.BlockSpec(memory_space=pl.ANY)),
        out_specs=(_HBM, _HBM),
        input_output_aliases={0: 0, 1: 1},
        compiler_params=pltpu.CompilerParams(has_side_effects=_SIDE_EFFECT),
    )(src_thru, land_thru, send_sems, recv_sems, after)[1]


def _sum_own_and_peers(own, land, name):
    r, c = own.shape
    tr = _pick(r, (256, 128, 64, 32, 16, 8))

    def body(o_ref, l_ref, out_ref):
        s = [l_ref[j] for j in range(_N_PEERS)]
        out_ref[...] = ((o_ref[...] + s[0]) + (s[1] + s[2])) + ((s[3] + s[4]) + (s[5] + s[6]))

    return pl.pallas_call(
        body,
        name=name,
        grid=(r // tr,),
        in_specs=[pl.BlockSpec((tr, c), lambda i: (i, 0)), pl.BlockSpec((_N_PEERS, tr, c), lambda i: (0, i, 0))],
        out_specs=pl.BlockSpec((tr, c), lambda i: (i, 0)),
        out_shape=jax.ShapeDtypeStruct((r, c), own.dtype),
        compiler_params=_params("parallel"),
    )(own, land)


class _Cols:
    def __init__(self, array, width, block):
        self.array, self.width, self.block = array, width, block
        self.shape, self.dtype = (array.shape[0], width), array.dtype


def _base(a):
    return a.array if isinstance(a, _Cols) else a


def _col_block(a):
    return a.block if isinstance(a, _Cols) else 0


def _make_rowwise(f, name, n_rows, n_tabs, n_pars):
    n_in = n_rows + n_tabs + n_pars

    def specs(args, tm):
        blocked = [pl.BlockSpec((tm, a.shape[1]), lambda i, blk=_col_block(a): (i, blk)) for a in args[: n_rows + n_tabs]]
        whole = [pl.BlockSpec(a.shape, lambda i: (0, 0)) for a in args[n_rows + n_tabs:]]
        return blocked + whole

    def out_struct(args, tm):
        blk = [jax.ShapeDtypeStruct((tm, a.shape[1]), a.dtype) for a in args[: n_rows + n_tabs]]
        blk += [jax.ShapeDtypeStruct(a.shape, a.dtype) for a in args[n_rows + n_tabs:]]
        return jax.eval_shape(f, *blk)

    def fwd_call(*args):
        r = args[0].shape[0]
        tm = _row_tile(r, max(a.shape[1] for a in args[:n_rows]))
        ro, so = out_struct(args, tm)

        def body(*refs):
            vals = [x[...] for x in refs[:n_in]]
            outs = refs[n_in:]
            rv, sv = f(*vals)
            for o, v in zip(outs[: len(ro)], rv):
                o[...] = v
            for o, v in zip(outs[len(ro):], sv):
                @pl.when(pl.program_id(0) == 0)
                def _(o=o, v=v):
                    o[...] = v

                @pl.when(pl.program_id(0) != 0)
                def _(o=o, v=v):
                    o[...] += v

        out_shape = [jax.ShapeDtypeStruct((r, s.shape[1]), s.dtype) for s in ro]
        out_shape += [jax.ShapeDtypeStruct(s.shape, s.dtype) for s in so]
        out_specs = [pl.BlockSpec((tm, s.shape[1]), lambda i: (i, 0)) for s in ro]
        out_specs += [pl.BlockSpec(s.shape, lambda i: (0, 0)) for s in so]
        res = pl.pallas_call(
            body,
            name=name + "_fwd",
            grid=(r // tm,),
            in_specs=specs(args, tm),
            out_specs=out_specs,
            out_shape=out_shape,
            compiler_params=_params("arbitrary" if so else "parallel"),
        )(*[_base(a) for a in args])
        return tuple(res[: len(ro)]), tuple(res[len(ro):])

    def bwd_call(args, cots, more=(), row_dtypes=None):
        r = args[0].shape[0]
        tm = _row_tile(r, max(a.shape[1] for a in args[:n_rows]))
        ro, so = out_struct(args, tm)
        crow, csum = cots
        rows, tabs, pars = args[:n_rows], args[n_rows:n_rows + n_tabs], args[n_rows + n_tabs:]
        n_c = len(crow) + len(csum)

        def body(*refs):
            vals = [x[...] for x in refs[:n_in]]
            cv = [x[...] for x in refs[n_in:n_in + n_c]]
            for x in refs[n_in + n_c:n_in + n_c + len(more)]:
                cv[0] = cv[0] + x[...]
            outs = refs[n_in + n_c + len(more):]
            tv = vals[n_rows:n_rows + n_tabs]

            def g(*dargs):
                return f(*dargs[:n_rows], *tv, *dargs[n_rows:])

            _, vjp = jax.vjp(g, *vals[:n_rows], *vals[n_rows + n_tabs:])
            d = vjp((tuple(cv[: len(crow)]), tuple(cv[len(crow):])))
            for o, v in zip(outs[:n_rows], d[:n_rows]):
                o[...] = v.astype(o.dtype)
            for o, v in zip(outs[n_rows:], d[n_rows:]):
                @pl.when(pl.program_id(0) == 0)
                def _(o=o, v=v):
                    o[...] = v

                @pl.when(pl.program_id(0) != 0)
                def _(o=o, v=v):
                    o[...] += v

        in_specs = specs(args, tm)
        in_specs += [pl.BlockSpec((tm, c.shape[1]), lambda i: (i, 0)) for c in crow]
        in_specs += [pl.BlockSpec(c.shape, lambda i: (0, 0)) for c in csum]
        in_specs += [pl.BlockSpec((tm, c.shape[1]), lambda i: (i, 0)) for c in more]
        out_shape = [jax.ShapeDtypeStruct(a.shape, dt) for a, dt in zip(rows, row_dtypes or [a.dtype for a in rows])]
        out_shape += [jax.ShapeDtypeStruct(a.shape, a.dtype) for a in pars]
        out_specs = [pl.BlockSpec((tm, a.shape[1]), lambda i: (i, 0)) for a in rows]
        out_specs += [pl.BlockSpec(a.shape, lambda i: (0, 0)) for a in pars]
        res = pl.pallas_call(
            body,
            name=name + "_bwd",
            grid=(r // tm,),
            in_specs=in_specs,
            out_specs=out_specs,
            out_shape=out_shape,
            compiler_params=_params("arbitrary" if pars else "parallel"),
        )(*[_base(a) for a in args], *crow, *csum, *more)
        return tuple(res[:n_rows]), tuple(res[n_rows:])

    @jax.custom_vjp
    def op(rows, tabs, pars):
        return fwd_call(*rows, *tabs, *pars)

    op.fwd_call, op.bwd_call = fwd_call, bwd_call

    def fwd(rows, tabs, pars):
        return fwd_call(*rows, *tabs, *pars), (rows, tabs, pars)

    def bwd(res, cots):
        rows, tabs, pars = res
        drows, dpars = bwd_call(tuple(rows) + tuple(tabs) + tuple(pars), cots)
        return drows, tuple(jnp.zeros_like(t) for t in tabs), dpars

    op.defvjp(fwd, bwd)
    return op


def _sigmoid(x):
    return 0.5 * (jnp.tanh(0.5 * x) + 1.0)


@jax.custom_jvp
def _softplus(x):
    e = jnp.exp(-jnp.abs(x))
    u = 1.0 + e
    log1p_e = jnp.where(u == 1.0, e, e * jnp.log(u) / jnp.where(u == 1.0, 1.0, u - 1.0))
    return jnp.maximum(x, 0.0) + log1p_e


@_softplus.defjvp
def _softplus_jvp(primals, tangents):
    (x,), (t,) = primals, tangents
    return _softplus(x), t * _sigmoid(x)


def _gelu(x):
    return 0.5 * x * (1.0 + jnp.tanh(math.sqrt(2.0 / math.pi) * (x + 0.044715 * (x * x * x))))


def _ln_res_f(h, mix, g, b):
    z = DN_ALPHA * h + mix
    mu = jnp.mean(z, axis=-1, keepdims=True)
    zc = z - mu
    var = jnp.mean(zc * zc, axis=-1, keepdims=True)
    return (zc * lax.rsqrt(var + EPS) * g + b,), ()


def _ln_res_copy_f(h, mix, g, b):
    (out,), _ = _ln_res_f(h, mix, g, b)
    return (out, out.astype(BF16)), ()


def _rmsnorm_f(x, g):
    return (x * lax.rsqrt(jnp.mean(x * x, axis=-1, keepdims=True) + EPS) * g,), ()


def _lru_gates_f(ga, gx, xc, b_a, b_x, lam):
    r = _sigmoid(ga + b_a)
    i = _sigmoid(gx + b_x)
    log_a = -LRU_C * r * _softplus(-lam)
    a = jnp.exp(log_a)
    one_minus_a2 = jnp.tanh(-log_a) * (jnp.exp(2.0 * log_a) + 1.0)
    return (a, jnp.sqrt(one_minus_a2) * (i * xc)), ()


def _lru_out_f(hh, p_gate):
    return (hh * _gelu(p_gate),), ()


def _rope_ret_f(q, k, cos2, sin2):
    d = cos2.shape[1]
    half = d // 2
    k_scale = d ** -0.5

    def rope(x):
        outs = []
        for h in range(x.shape[1] // d):
            xh = x[:, h * d:(h + 1) * d]
            rot = jnp.concatenate([xh[:, half:], xh[:, :half]], axis=1)
            outs.append(xh * cos2 + rot * sin2)
        return jnp.concatenate(outs, axis=1)

    return (rope(q), rope(k) * k_scale), ()


def _ret_out_f(o, g):
    d = o.shape[1] // RET_HEADS
    outs = []
    for h in range(RET_HEADS):
        oh = o[:, h * d:(h + 1) * d]
        outs.append(oh * lax.rsqrt(jnp.mean(oh * oh, axis=-1, keepdims=True) + EPS))
    y = jnp.concatenate(outs, axis=1)
    return (g * _sigmoid(g) * y,), ()


def _loss_f(y, t, mask):
    e = (y - t) * mask
    per_row = jnp.sum(e * e, axis=-1, keepdims=True) * (0.5 / y.shape[1])
    total = jnp.sum(per_row, axis=0, keepdims=True)
    return (), (jnp.broadcast_to(total, (1, LANES)),)


def _shift_down(x, s):
    if s == 0:
        return x
    t = x.shape[0]
    row = lax.broadcasted_iota(jnp.int32, x.shape, 0)
    return jnp.where(row >= s, pltpu.roll(x, s, 0), 0.0)


def _shift_up(x, s):
    if s == 0:
        return x
    t = x.shape[0]
    row = lax.broadcasted_iota(jnp.int32, x.shape, 0)
    return jnp.where(row < t - s, pltpu.roll(x, t - s, 0), 0.0)


def _conv_fwd(x, w, b, name):
    bsz, t, c = x.shape
    width = w.shape[0]

    def body(x_ref, w_ref, b_ref, y_ref):
        xv = x_ref[0]
        acc = jnp.broadcast_to(b_ref[...], xv.shape)
        for k in range(width):
            acc = acc + w_ref[k:k + 1, :] * _shift_down(xv, width - 1 - k)
        y_ref[0] = acc

    return pl.pallas_call(
        body,
        name=name,
        grid=(bsz, c // LANES),
        in_specs=[
            pl.BlockSpec((1, t, LANES), lambda i, j: (i, 0, j)),
            pl.BlockSpec((width, LANES), lambda i, j: (0, j)),
            pl.BlockSpec((1, LANES), lambda i, j: (0, j)),
        ],
        out_specs=pl.BlockSpec((1, t, LANES), lambda i, j: (i, 0, j)),
        out_shape=jax.ShapeDtypeStruct(x.shape, F32),
        compiler_params=_params("parallel", "parallel"),
    )(x, w, b)


def _conv_bwd(x, w, dy, name):
    bsz, t, c = x.shape
    width = w.shape[0]

    def body(x_ref, w_ref, dy_ref, dx_ref, dw_ref, db_ref):
        xv, g = x_ref[0], dy_ref[0]
        dx = jnp.zeros_like(xv)
        dws = []
        for k in range(width):
            s = width - 1 - k
            dx = dx + w_ref[k:k + 1, :] * _shift_up(g, s)
            dws.append(jnp.sum(g * _shift_down(xv, s), axis=0, keepdims=True))
        dx_ref[0] = dx
        dw = jnp.concatenate(dws, axis=0)
        db = jnp.sum(g, axis=0, keepdims=True)

        @pl.when(pl.program_id(1) == 0)
        def _():
            dw_ref[...] = dw
            db_ref[...] = db

        @pl.when(pl.program_id(1) != 0)
        def _():
            dw_ref[...] += dw
            db_ref[...] += db

    return pl.pallas_call(
        body,
        name=name,
        grid=(c // LANES, bsz),
        in_specs=[
            pl.BlockSpec((1, t, LANES), lambda j, i: (i, 0, j)),
            pl.BlockSpec((width, LANES), lambda j, i: (0, j)),
            pl.BlockSpec((1, t, LANES), lambda j, i: (i, 0, j)),
        ],
        out_specs=[
            pl.BlockSpec((1, t, LANES), lambda j, i: (i, 0, j)),
            pl.BlockSpec((width, LANES), lambda j, i: (0, j)),
            pl.BlockSpec((1, LANES), lambda j, i: (0, j)),
        ],
        out_shape=[
            jax.ShapeDtypeStruct(x.shape, F32),
            jax.ShapeDtypeStruct(w.shape, F32),
            jax.ShapeDtypeStruct((1, c), F32),
        ],
        compiler_params=_params("parallel", "arbitrary"),
    )(x, w, dy)


def _make_conv(name):
    @jax.custom_vjp
    def op(x, w, b):
        return _conv_fwd(x, w, b, name + "_fwd")

    def fwd(x, w, b):
        return op(x, w, b), (x, w)

    def bwd(res, dy):
        x, w = res
        return tuple(_conv_bwd(x, w, dy, name + "_bwd"))

    op.defvjp(fwd, bwd)
    return op


_SCAN_ROWS = 8


def _scan_fwd(a, b, name):
    bsz, t, c = a.shape
    cw = _pick(c, (4 * LANES, 2 * LANES, LANES))

    def body(a_ref, b_ref, h_ref):
        row = lax.broadcasted_iota(jnp.int32, (_SCAN_ROWS, cw), 0)

        def step(i, carry):
            r0 = pl.multiple_of(i * _SCAN_ROWS, _SCAN_ROWS)
            av, bv = a_ref[0, pl.ds(r0, _SCAN_ROWS), :], b_ref[0, pl.ds(r0, _SCAN_ROWS), :]
            for s in (1, 2, 4):
                a_sh = jnp.where(row >= s, pltpu.roll(av, s, 0), 1.0)
                b_sh = jnp.where(row >= s, pltpu.roll(bv, s, 0), 0.0)
                bv = av * b_sh + bv
                av = av * a_sh
            hv = bv + av * carry
            h_ref[0, pl.ds(r0, _SCAN_ROWS), :] = hv
            return hv[_SCAN_ROWS - 1:, :]

        lax.fori_loop(0, t // _SCAN_ROWS, step, jnp.zeros((1, cw), F32), unroll=2)

    spec = pl.BlockSpec((1, t, cw), lambda i, j: (i, 0, j))
    return pl.pallas_call(
        body,
        name=name,
        grid=(bsz, c // cw),
        in_specs=[spec, spec],
        out_specs=spec,
        out_shape=jax.ShapeDtypeStruct(a.shape, F32),
        compiler_params=_params("parallel", "parallel"),
    )(a, b)


def _scan_bwd(a, h, g, name):
    bsz, t, c = a.shape
    cw = _pick(c, (2 * LANES, LANES))

    def body(a_ref, h_ref, g_ref, da_ref, db_ref):
        rows = _SCAN_ROWS
        row = lax.broadcasted_iota(jnp.int32, (rows, cw), 0)
        n_tiles = t // rows

        def step(n, carry):
            lam_next, a_next = carry
            i = n_tiles - 1 - n
            r0 = pl.multiple_of(i * rows, rows)
            rp = pl.multiple_of(jnp.maximum(i - 1, 0) * rows, rows)
            av, gv, hv = a_ref[0, pl.ds(r0, rows), :], g_ref[0, pl.ds(r0, rows), :], h_ref[0, pl.ds(r0, rows), :]
            h_before = jnp.where(i > 0, h_ref[0, pl.ds(rp, rows), :][rows - 1:, :], 0.0)
            cv = jnp.where(row < rows - 1, pltpu.roll(av, rows - 1, 0), a_next)
            for s in (1, 2, 4):
                c_sh = jnp.where(row < rows - s, pltpu.roll(cv, rows - s, 0), 1.0)
                g_sh = jnp.where(row < rows - s, pltpu.roll(gv, rows - s, 0), 0.0)
                gv = cv * g_sh + gv
                cv = cv * c_sh
            lam = gv + cv * lam_next
            db_ref[0, pl.ds(r0, rows), :] = lam
            da_ref[0, pl.ds(r0, rows), :] = lam * jnp.where(row >= 1, pltpu.roll(hv, 1, 0), h_before)
            return lam[:1, :], av[:1, :]

        zero = jnp.zeros((1, cw), F32)
        lax.fori_loop(0, n_tiles, step, (zero, zero), unroll=2)

    spec = pl.BlockSpec((1, t, cw), lambda i, j: (i, 0, j))
    return pl.pallas_call(
        body,
        name=name,
        grid=(bsz, c // cw),
        in_specs=[spec, spec, spec],
        out_specs=[spec, spec],
        out_shape=[jax.ShapeDtypeStruct(a.shape, F32)] * 2,
        compiler_params=_params("parallel", "parallel"),
    )(a, h, g)


def _make_scan(name):
    @jax.custom_vjp
    def op(a, b):
        return _scan_fwd(a, b, name + "_fwd")

    def fwd(a, b):
        h = op(a, b)
        return h, (a, h)

    def bwd(res, g):
        a, h = res
        da, db = _scan_bwd(a, h, g, name + "_bwd")
        return da, db

    op.defvjp(fwd, bwd)
    return op


def _query_blocks(t):
    blocks, start = [], 0
    while start < t:
        rows = 2 * SEQ_BLOCK if start + 2 * SEQ_BLOCK <= t else SEQ_BLOCK
        blocks.append((start, rows))
        start += rows
    return blocks


def _attn_probs(q, k, start, scale):
    tq, tk = q.shape[0], k.shape[0]
    s = _dot(q, k, "nt") * scale
    qpos = start + lax.broadcasted_iota(jnp.int32, (tq, tk), 0)
    kpos = lax.broadcasted_iota(jnp.int32, (tq, tk), 1)
    s = jnp.where(kpos <= qpos, s, NEG_INF)
    e = jnp.exp(s - jnp.max(s, axis=-1, keepdims=True))
    return e / jnp.sum(e, axis=-1, keepdims=True)


_MLA_SCALE = (MLA_NOPE + MLA_ROPE) ** -0.5


def _attn_specs(t):
    head = pl.BlockSpec((1, t, LANES), lambda b, h: (b, 0, h))
    shared = pl.BlockSpec((1, t, LANES), lambda b, h: (b, 0, 0))
    return head, shared


def _attn_fwd(q, kv, kpe, name):
    bsz, t, hl = q.shape
    head, shared = _attn_specs(t)

    def body(q_ref, kv_ref, kpe_ref, o_ref, k_s, v_s):
        lane = lax.broadcasted_iota(jnp.int32, (t, LANES), 1)
        kvh = kv_ref[0]
        k_s[...] = jnp.where(lane < MLA_NOPE, kvh, kpe_ref[0]).astype(BF16)
        v_s[...] = kvh.astype(BF16)
        for start, rows in _query_blocks(t):
            n = start + rows
            p = _attn_probs(q_ref[0, start:n, :], k_s[:n, :], start, _MLA_SCALE)
            o_ref[0, start:n, :] = _dot(p, v_s[:n, :], "nn")

    return pl.pallas_call(
        body,
        name=name,
        grid=(bsz, hl // LANES),
        in_specs=[head, head, shared],
        out_specs=head,
        out_shape=jax.ShapeDtypeStruct(q.shape, F32),
        scratch_shapes=[pltpu.VMEM((t, LANES), BF16), pltpu.VMEM((t, LANES), BF16)],
        compiler_params=_params("parallel", "parallel"),
    )(q, kv, kpe)


def _attn_bwd(q, kv, kpe, do, name):
    bsz, t, hl = q.shape
    head, shared = _attn_specs(t)

    def body(q_ref, kv_ref, kpe_ref, do_ref, dq_ref, dkv_ref, dkpe_ref, k_s, v_s, dk_s, dv_s):
        lane = lax.broadcasted_iota(jnp.int32, (t, LANES), 1)
        kvh = kv_ref[0]
        k_s[...] = jnp.where(lane < MLA_NOPE, kvh, kpe_ref[0]).astype(BF16)
        v_s[...] = kvh.astype(BF16)
        for start, rows in reversed(_query_blocks(t)):
            n = start + rows
            qb = q_ref[0, start:n, :]
            dob = jnp.where(lane[:rows] >= MLA_NOPE, do_ref[0, start:n, :], 0.0)
            kk, vv = k_s[:n, :], v_s[:n, :]
            p = _attn_probs(qb, kk, start, _MLA_SCALE)
            dp = _dot(dob, vv, "nt")
            ds = p * (dp - jnp.sum(dp * p, axis=-1, keepdims=True)) * _MLA_SCALE
            dq_ref[0, start:n, :] = _dot(ds, kk, "nn")
            if n == t:
                dk_s[...] = _dot(ds, qb, "tn")
                dv_s[...] = _dot(p, dob, "tn")
            else:
                dk_s[:n, :] += _dot(ds, qb, "tn")
                dv_s[:n, :] += _dot(p, dob, "tn")
        dk = dk_s[...]
        dkv_ref[0] = jnp.where(lane < MLA_NOPE, dk, dv_s[...])
        dkpe = jnp.where(lane >= MLA_NOPE, dk, 0.0)

        @pl.when(pl.program_id(1) == 0)
        def _():
            dkpe_ref[0] = dkpe

        @pl.when(pl.program_id(1) != 0)
        def _():
            dkpe_ref[0] += dkpe

    return pl.pallas_call(
        body,
        name=name,
        grid=(bsz, hl // LANES),
        in_specs=[head, head, shared, head],
        out_specs=[head, head, shared],
        out_shape=[
            jax.ShapeDtypeStruct(q.shape, F32),
            jax.ShapeDtypeStruct(kv.shape, F32),
            jax.ShapeDtypeStruct(kpe.shape, F32),
        ],
        scratch_shapes=[pltpu.VMEM((t, LANES), BF16), pltpu.VMEM((t, LANES), BF16),
                        pltpu.VMEM((t, LANES), F32), pltpu.VMEM((t, LANES), F32)],
        compiler_params=_params("parallel", "arbitrary"),
    )(q, kv, kpe, do)


def _make_attention(name):
    @jax.custom_vjp
    def op(q, kv, kpe):
        return _attn_fwd(q, kv, kpe, name + "_fwd")

    def fwd(q, kv, kpe):
        return op(q, kv, kpe), (q, kv, kpe)

    def bwd(res, do):
        return tuple(_attn_bwd(*res, do, name + "_bwd"))

    op.defvjp(fwd, bwd)
    return op


_ROPE_SHIFT = MLA_ROPE // 2


def _rope_lanes_call(x, c, sm, sp, transpose, name):
    r, width = x.shape
    tm = _row_tile(r, width)

    def body(x_ref, c_ref, sm_ref, sp_ref, y_ref):
        cv, smv, spv = c_ref[...], sm_ref[...], sp_ref[...]
        for b in range(width // LANES):
            xb = x_ref[:, b * LANES:(b + 1) * LANES]
            if transpose:
                yb = xb * cv + pltpu.roll(xb * smv, _ROPE_SHIFT, 1) + pltpu.roll(xb * spv, LANES - _ROPE_SHIFT, 1)
            else:
                yb = xb * cv + pltpu.roll(xb, LANES - _ROPE_SHIFT, 1) * smv + pltpu.roll(xb, _ROPE_SHIFT, 1) * spv
            y_ref[:, b * LANES:(b + 1) * LANES] = yb

    tab = pl.BlockSpec((tm, LANES), lambda i: (i, 0))
    blk = pl.BlockSpec((tm, width), lambda i: (i, 0))
    return pl.pallas_call(
        body,
        name=name,
        grid=(r // tm,),
        in_specs=[blk, tab, tab, tab],
        out_specs=blk,
        out_shape=jax.ShapeDtypeStruct(x.shape, F32),
        compiler_params=_params("parallel"),
    )(x, c, sm, sp)


def _make_rope_lanes(name):
    @jax.custom_vjp
    def op(x, c, sm, sp):
        return _rope_lanes_call(x, c, sm, sp, False, name + "_fwd")

    def fwd(x, c, sm, sp):
        return op(x, c, sm, sp), (c, sm, sp)

    def bwd(res, dy):
        c, sm, sp = res
        return _rope_lanes_call(dy, c, sm, sp, True, name + "_bwd"), jnp.zeros_like(c), jnp.zeros_like(sm), jnp.zeros_like(sp)

    op.defvjp(fwd, bwd)
    return op


RET_KEY_CHUNK = 512


def _key_chunks(t):
    return [(c, min(RET_KEY_CHUNK, t - c)) for c in range(0, t, RET_KEY_CHUNK)]


def _decay(q0, k0, tq, tk, log_gamma):
    row = lax.broadcasted_iota(jnp.int32, (tq, 1), 0)
    col = (q0 - k0) - lax.broadcasted_iota(jnp.int32, (1, tk), 1)
    outer = jnp.exp(log_gamma * row.astype(F32)) * jnp.exp(log_gamma * col.astype(F32))
    return jnp.where(row + col >= 0, outer, 0.0)


def _ret_query_rows(t):
    return t // 4 if t % 32 == 0 else SEQ_BLOCK


def _ret_specs(t, dk, dv, heads):
    tq = _ret_query_rows(t)
    return (
        pl.BlockSpec(memory_space=pltpu.SMEM),
        pl.BlockSpec((1, tq, dk), lambda b, h, i: (b, i, h)),
        pl.BlockSpec((1, t, dk), lambda b, h, i: (b, 0, h)),
        pl.BlockSpec((1, t, dv), lambda b, h, i: (b, 0, h)),
        pl.BlockSpec((1, tq, dv), lambda b, h, i: (b, i, h)),
    )


def _ret_v_spec(t, dv, v_block0):
    return pl.BlockSpec((1, t, dv), lambda b, h, i: (b, 0, h + v_block0))


def _ret_fwd(lg, q, k, v, name, dv=None, v_block0=0):
    bsz, t, hdk = q.shape
    heads = lg.shape[0]
    dk, dv = hdk // heads, dv or v.shape[2] // heads
    lg_spec, q_spec, k_spec, v_spec, o_spec = _ret_specs(t, dk, dv, heads)
    tq = _ret_query_rows(t)

    def body(lg_ref, q_ref, k_ref, v_ref, o_ref):
        q0 = pl.program_id(2) * tq
        lgh = lg_ref[pl.program_id(1)]
        o_ref[0] = jnp.zeros((tq, dv), F32)
        for c0, cw in _key_chunks(t):
            @pl.when(c0 < q0 + tq)
            def _(c0=c0, cw=cw):
                d = _decay(q0, c0, tq, cw, lgh)
                a = _dot(q_ref[0], k_ref[0, c0:c0 + cw, :], "nt") * d
                o_ref[0] += _dot(a, v_ref[0, c0:c0 + cw, :], "nn")

    return pl.pallas_call(
        body,
        name=name,
        grid=(bsz, heads, t // tq),
        in_specs=[lg_spec, q_spec, k_spec, _ret_v_spec(t, dv, v_block0)],
        out_specs=o_spec,
        out_shape=jax.ShapeDtypeStruct((bsz, t, heads * dv), F32),
        compiler_params=_params("parallel", "parallel", "parallel"),
    )(lg, q, k, v)


def _ret_bwd(lg, q, k, v, do, name, v_block0=0, dv_dtype=F32):
    bsz, t, hdk = q.shape
    heads = lg.shape[0]
    dk, dv = hdk // heads, do.shape[2] // heads
    lg_spec, q_spec, k_spec, v_spec, o_spec = _ret_specs(t, dk, dv, heads)
    tq = _ret_query_rows(t)

    def body(lg_ref, q_ref, k_ref, v_ref, do_ref, dq_ref, dk_ref, dv_out_ref, dv_ref):
        q0 = pl.program_id(2) * tq
        lgh = lg_ref[pl.program_id(1)]

        @pl.when(pl.program_id(2) == 0)
        def _():
            dk_ref[0] = jnp.zeros((t, dk), F32)
            dv_ref[0] = jnp.zeros((t, dv), F32)

        dq_ref[0] = jnp.zeros((tq, dk), F32)
        for c0, cw in _key_chunks(t):
            @pl.when(c0 < q0 + tq)
            def _(c0=c0, cw=cw):
                qb, dob = q_ref[0], do_ref[0]
                kk, vv = k_ref[0, c0:c0 + cw, :], v_ref[0, c0:c0 + cw, :]
                d = _decay(q0, c0, tq, cw, lgh)
                a = _dot(qb, kk, "nt") * d
                ds = _dot(dob, vv, "nt") * d
                dq_ref[0] += _dot(ds, kk, "nn")
                dk_ref[0, c0:c0 + cw, :] += _dot(ds, qb, "tn")
                dv_ref[0, c0:c0 + cw, :] += _dot(a, dob, "tn")

        @pl.when(pl.program_id(2) == t // tq - 1)
        def _():
            dv_out_ref[...] = dv_ref[...].astype(dv_dtype)

    return pl.pallas_call(
        body,
        name=name,
        grid=(bsz, heads, t // tq),
        in_specs=[lg_spec, q_spec, k_spec, _ret_v_spec(t, dv, v_block0), o_spec],
        out_specs=[q_spec, k_spec, v_spec],
        out_shape=[
            jax.ShapeDtypeStruct(q.shape, F32),
            jax.ShapeDtypeStruct(k.shape, F32),
            jax.ShapeDtypeStruct(do.shape, dv_dtype),
        ],
        scratch_shapes=[pltpu.VMEM((1, t, dv), F32)],
        compiler_params=_params("parallel", "parallel", "arbitrary"),
    )(lg, q, k, v, do)


def _make_retention(name):
    @jax.custom_vjp
    def op(lg, q, k, v):
        return _ret_fwd(lg, q, k, v, name + "_fwd")

    def fwd(lg, q, k, v):
        return op(lg, q, k, v), (lg, q, k, v)

    def bwd(res, do):
        lg = res[0]
        return (jnp.zeros_like(lg),) + tuple(_ret_bwd(*res, do, name + "_bwd"))

    op.defvjp(fwd, bwd)
    return op


def _adamw(w, g, m, v, name):
    r, c = w.shape
    tr = _pick(r, (256, 128, 64, 32, 16, 8))

    def body(w_ref, g_ref, m_ref, v_ref, d_ref, nm_ref, nv_ref):
        gv = g_ref[...]
        nm = ADAM_B1 * m_ref[...] + (1.0 - ADAM_B1) * gv
        nv = ADAM_B2 * v_ref[...] + (1.0 - ADAM_B2) * (gv * gv)
        m_hat = nm / (1.0 - ADAM_B1 ** ADAM_STEP)
        v_hat = nv / (1.0 - ADAM_B2 ** ADAM_STEP)
        d_ref[...] = -ADAM_LR * (m_hat / (jnp.sqrt(v_hat) + ADAM_EPS) + ADAM_WD * w_ref[...])
        nm_ref[...] = nm
        nv_ref[...] = nv

    spec = pl.BlockSpec((tr, c), lambda i: (i, 0))
    return pl.pallas_call(
        body,
        name=name,
        grid=(r // tr,),
        in_specs=[spec] * 4,
        out_specs=[spec] * 3,
        out_shape=[jax.ShapeDtypeStruct((r, c), F32)] * 3,
        compiler_params=_params("parallel"),
    )(w, g, m, v)


def _rope_tables(t, half, reps):
    inv = ROPE_BASE ** (-jnp.arange(half, dtype=F32) / half)
    ang = jnp.arange(t, dtype=jnp.int32).astype(F32)[:, None] * inv[None, :]
    return jnp.tile(jnp.cos(ang), (1, reps)), jnp.tile(jnp.sin(ang), (1, reps))


def _padded_len(seq):
    return -(-(N_META + seq) // SEQ_BLOCK) * SEQ_BLOCK


def _embed(meta, x):
    bsz, seq, d = x.shape
    t = _padded_len(seq)
    return jnp.concatenate(
        [jnp.broadcast_to(meta[None], (bsz, N_META, d)), x, jnp.zeros((bsz, t - N_META - seq, d), F32)], axis=1
    ).reshape(bsz * t, d)


def _even_mixer(p, conv_w_shard, conv_b, w_rg_a, b_rg_a, w_rg_x, b_rg_x, lru_lambda, q_norm_g, w_uq_shard,
                kv_norm_g, w_ukv_shard, gathered, bsz):
    conv_w_full, w_uq_full, w_ukv_full = gathered
    r = p.shape[0]
    t = r // bsz

    def tile_rows(tab):
        return jnp.tile(tab, (bsz, 1))

    lru_w = w_rg_a.shape[2] * w_rg_a.shape[1]
    q_rank, kv_rank = q_norm_g.shape[1], kv_norm_g.shape[1]
    p_gate, p_rec, p_q, p_kv, p_kpe = _split_cols(
        p, (lru_w, 2 * lru_w, 2 * lru_w + q_rank, 2 * lru_w + q_rank + kv_rank))

    conv_w = _make_fsdp_param("conv_w")(conv_w_shard[0], conv_w_full)
    xc = _make_conv("conv")(p_rec.reshape(bsz, t, lru_w), conv_w, conv_b).reshape(r, lru_w)
    ga = _make_group_linear("rg_a")(xc, w_rg_a[0])
    gx = _make_group_linear("rg_x")(xc, w_rg_x[0])
    (a, bb), _ = _make_rowwise(_lru_gates_f, "lru_gates", 3, 0, 3)((ga, gx, xc), (), (b_rg_a, b_rg_x, lru_lambda))
    hh = _make_scan("lru_scan")(a.reshape(bsz, t, lru_w), bb.reshape(bsz, t, lru_w)).reshape(r, lru_w)
    (y_rec,), _ = _make_rowwise(_lru_out_f, "lru_out", 2, 0, 0)((hh, p_gate), (), ())

    (qn,), _ = _make_rowwise(_rmsnorm_f, "q_norm", 1, 0, 1)((p_q,), (), (q_norm_g,))
    (kvn,), _ = _make_rowwise(_rmsnorm_f, "kv_norm", 1, 0, 1)((p_kv,), (), (kv_norm_g,))
    d_head = MLA_NOPE + MLA_ROPE
    w_uq_pad = jnp.pad(w_uq_full.reshape(q_rank, MLA_HEADS, d_head), ((0, 0), (0, 0), (0, LANES - d_head)))

    def unpad_uq(dw):
        return dw.reshape(q_rank, MLA_HEADS, LANES)[:, :, :d_head].reshape(q_rank, MLA_HEADS * d_head)

    q = _make_fsdp_linear(True, "ev_uq", unpad_uq)(qn, w_uq_shard[0], w_uq_pad.reshape(q_rank, MLA_HEADS * LANES))
    kv = _make_fsdp_linear(True, "ev_ukv")(kvn, w_ukv_shard[0], w_ukv_full)
    half = MLA_ROPE // 2
    cos, sin = _rope_tables(t, half, 1)
    one, zero = jnp.ones((t, MLA_NOPE), F32), jnp.zeros((t, MLA_NOPE), F32)
    tail = LANES - MLA_NOPE - MLA_ROPE
    c_tab = tile_rows(jnp.concatenate([one, cos, cos, one[:, :tail]], axis=1))
    sm_tab = tile_rows(jnp.concatenate([zero, -sin, zero[:, :half + tail]], axis=1))
    sp_tab = tile_rows(jnp.concatenate([zero, zero[:, :half], sin, zero[:, :tail]], axis=1))
    q = _make_rope_lanes("rope_q")(q, c_tab, sm_tab, sp_tab)
    kpe = _make_rope_lanes("rope_k")(p_kpe, c_tab, sm_tab, sp_tab)
    o = _make_attention("mla")(q.reshape(bsz, t, -1), kv.reshape(bsz, t, -1), kpe.reshape(bsz, t, LANES))
    return jnp.concatenate([y_rec, o.reshape(r, -1)], axis=1)


def _odd_mixer_fwd(p, bsz):
    r, width = p.shape
    t = r // bsz
    qk = width // 6
    dk = qk // RET_HEADS
    cos2, sin2 = _rope_tables(t, dk // 2, 2)
    sin2 = jnp.concatenate([-sin2[:, :dk // 2], sin2[:, dk // 2:]], axis=1)
    rope_args = (_Cols(p, qk, 0), _Cols(p, qk, 1), jnp.tile(cos2, (bsz, 1)), jnp.tile(sin2, (bsz, 1)))
    (rq, rk), _ = _make_rowwise(_rope_ret_f, "rope_ret", 2, 2, 0).fwd_call(*rope_args)
    lg = jnp.log(1.0 - 2.0 ** (-5.0 - jnp.arange(RET_HEADS, dtype=F32)))
    ret_args = (lg, rq.reshape(bsz, t, qk), rk.reshape(bsz, t, qk), p.reshape(bsz, t, width))
    o = _ret_fwd(*ret_args, "ret_fwd", dv=2 * dk, v_block0=qk // dk)
    gate_args = (o.reshape(r, 2 * qk), _Cols(p, 2 * qk, 2))
    (y,), _ = _make_rowwise(_ret_out_f, "ret_out", 2, 0, 0).fwd_call(*gate_args)
    return y, (rope_args, ret_args, gate_args)


def _odd_mixer_bwd(res, dy):
    rope_args, ret_args, gate_args = res
    bsz, t, qk = ret_args[1].shape
    dk = qk // RET_HEADS
    (do, dg), _ = _make_rowwise(_ret_out_f, "ret_out", 2, 0, 0).bwd_call(
        gate_args, ((dy,), ()), row_dtypes=(F32, BF16))
    drq, drk, dv = _ret_bwd(*ret_args, do.reshape(bsz, t, 2 * qk), "ret_bwd", v_block0=qk // dk, dv_dtype=BF16)
    (dq, dkk), _ = _make_rowwise(_rope_ret_f, "rope_ret", 2, 2, 0).bwd_call(
        rope_args, ((drq.reshape(bsz * t, qk), drk.reshape(bsz * t, qk)), ()), row_dtypes=(BF16, BF16))
    return jnp.concatenate([dq, dkk, dv.reshape(bsz * t, 2 * qk), dg], axis=1)


def _local_loss(h, target):
    bsz, seq, d = target.shape
    t = _padded_len(seq)
    t_real = N_META + seq
    pos = jnp.arange(t, dtype=jnp.int32)
    mask = jnp.tile(((pos >= N_META) & (pos < t_real)).astype(F32)[:, None], (bsz, 1))
    tgt = jnp.concatenate(
        [jnp.zeros((bsz, N_META, d), F32), target, jnp.zeros((bsz, t - t_real, d), F32)], axis=1).reshape(bsz * t, d)
    _, (total,) = _make_rowwise(_loss_f, "loss", 1, 2, 0)((h,), (tgt, mask), ())
    return total[0, 0]


_WEIGHTS = ("meta_tokens", "ev_w_in", "ev_conv_w", "ev_conv_b", "ev_w_rg_a", "ev_b_rg_a", "ev_w_rg_x", "ev_b_rg_x",
            "ev_lru_lambda", "ev_q_norm_g", "ev_w_uq", "ev_kv_norm_g", "ev_w_ukv", "ev_w_out", "od_w_in", "od_w_out",
            "ln_mix_g", "ln_mix_b", "mlp_w1", "mlp_w2", "ln_mlp_g", "ln_mlp_b")


def kernel(x, meta_tokens, ev_w_in, ev_conv_w, ev_conv_b, ev_w_rg_a, ev_b_rg_a, ev_w_rg_x, ev_b_rg_x, ev_lru_lambda, ev_q_norm_g, ev_w_uq, ev_kv_norm_g, ev_w_ukv, ev_w_out, od_w_in, od_w_out, ln_mix_g, ln_mix_b, mlp_w1, mlp_w2, ln_mlp_g, ln_mlp_b, loss_target, m_meta_tokens, m_ev_w_in, m_ev_conv_w, m_ev_conv_b, m_ev_w_rg_a, m_ev_b_rg_a, m_ev_w_rg_x, m_ev_b_rg_x, m_ev_lru_lambda, m_ev_q_norm_g, m_ev_w_uq, m_ev_kv_norm_g, m_ev_w_ukv, m_ev_w_out, m_od_w_in, m_od_w_out, m_ln_mix_g, m_ln_mix_b, m_mlp_w1, m_mlp_w2, m_ln_mlp_g, m_ln_mlp_b, v_meta_tokens, v_ev_w_in, v_ev_conv_w, v_ev_conv_b, v_ev_w_rg_a, v_ev_b_rg_a, v_ev_w_rg_x, v_ev_b_rg_x, v_ev_lru_lambda, v_ev_q_norm_g, v_ev_w_uq, v_ev_kv_norm_g, v_ev_w_ukv, v_ev_w_out, v_od_w_in, v_od_w_out, v_ln_mix_g, v_ln_mix_b, v_mlp_w1, v_mlp_w2, v_ln_mlp_g, v_ln_mlp_b):
    args = locals()
    weights = {n: args[n] for n in _WEIGHTS}
    bsz = x.shape[0]
    my_x, my_y, my_c = _my_place()
    me = 4 * my_x + 2 * my_y + my_c

    big = (("ev_in", ev_w_in[0], True), ("ev_out", ev_w_out[0], False), ("mlp0_w1", mlp_w1[0], True),
           ("mlp0_w2", mlp_w2[0], False), ("od_in", od_w_in[0], True), ("od_out", od_w_out[0], False),
           ("mlp1_w1", mlp_w1[1], True), ("mlp1_w2", mlp_w2[1], False))
    small_sharded = (("meta", meta_tokens, F32), ("conv_w", ev_conv_w[0], F32), ("ev_uq", ev_w_uq[0], BF16),
                     ("ev_ukv", ev_w_ukv[0], BF16))
    to_gather = (tuple((nm, s.astype(dt), True) for nm, s, dt in small_sharded)
                 + tuple((nm, s.astype(BF16), cols) for nm, s, cols in big))
    handles = _gather_start_all([s for _, s, _ in to_gather], "ag_start")
    gathers = {nm: (s, cols, h) for (nm, s, cols), h in zip(to_gather, handles)}
    gather_tokens = (handles[0][4],)

    def full_weight(nm, after):
        shard16, cols, handle = gathers[nm]
        land = _exchange_wait(handle, True, after, "ag_wait_" + nm)
        land = lax.dynamic_update_index_in_dim(land, shard16, me, 0)
        if cols and shard16.shape[1] % LANES == 0:
            return land, True
        return (_unstack_cols(land) if cols else land.reshape(-1, shard16.shape[1])), False

    meta_full, conv_w_full, w_uq_full, w_ukv_full = (
        _unstack_cols(lax.dynamic_update_index_in_dim(
            _exchange_wait(gathers[nm][2], True, gather_tokens[-1], "ag_wait_" + nm), gathers[nm][0], me, 0))
        for nm, _, _ in small_sharded)

    pending = []

    def linear_bwd(nm, x_in, w, dy, cols, unpad=None, **fused):
        w_full, w_stacked = w
        a_relu2 = fused.pop("a_relu2", False)
        if w_stacked:
            stacked = _matmul(x_in, dy, "tn", nm + "_dw", stacked=True, a_relu2=a_relu2)
            own = lax.dynamic_index_in_dim(stacked, me, 0, keepdims=False)
        else:
            dw = _matmul(x_in, dy, "tn", nm + "_dw", a_relu2=a_relu2)
            dw = dw if unpad is None else unpad(dw)
            n = dw.shape[1] // N_DEV
            if cols:
                stacked = _stack_cols(dw)
                own = lax.dynamic_slice_in_dim(dw, me * n, n, axis=1)
            else:
                stacked = dw.reshape(N_DEV, dw.shape[0] // N_DEV, dw.shape[1])
                own = lax.dynamic_index_in_dim(stacked, me, 0, keepdims=False)
        handle = _exchange_start(stacked, (_N_PEERS,) + stacked.shape[1:], False, "rs_start_" + nm)
        pending.append((nm, own, handle))
        return _matmul(dy, w_full, "nt", nm + "_dx", after=(handle[4],), stacked=w_stacked, **fused)

    def linear_fwd(nm, x_in, w, **fused):
        return _matmul(x_in, w[0], "nn", nm + "_fwd", stacked=w[1], **fused)

    def mlp_fwd(h, h16, l):
        w1 = full_weight(f"mlp{l}_w1", h16)
        u, a16 = linear_fwd(f"mlp{l}_w1", h16, w1, relu2_copy=True)
        w2 = full_weight(f"mlp{l}_w2", a16)
        f = linear_fwd(f"mlp{l}_w2", a16, w2)
        ln_args = (h, f, ln_mlp_g[l:l + 1], ln_mlp_b[l:l + 1])
        return ln_fwd(f"mlp{l}_ln", *ln_args), (h16, w1, u, a16, w2, ln_args)

    def mlp_bwd(dout, res, l):
        h16, w1, u, a16, w2, ln_args = res
        dh, df, dg, db = ln_bwd(f"mlp{l}_ln", ln_args, dout)
        du = linear_bwd(f"mlp{l}_w2", a16, w2, df, False, relu2_bwd_of=u, out_dtype=BF16)
        return (dh, linear_bwd(f"mlp{l}_w1", h16, w1, du, True)), dg, db

    def ln_fwd(nm, h, mix, g, b):
        return _make_rowwise(_ln_res_copy_f, nm, 2, 0, 2).fwd_call(h, mix, g, b)[0]

    def ln_bwd(nm, ln_args, pieces):
        (dh, dmix), (dg, db) = _make_rowwise(_ln_res_f, nm, 2, 0, 2).bwd_call(
            ln_args, ((pieces[0],), ()), more=tuple(pieces[1:]), row_dtypes=(F32, BF16))
        return dh, dmix, dg, db

    h0, vjp_embed = jax.vjp(_embed, meta_full, x)
    n_in = ev_w_in.shape[2] * N_DEV
    kpe0, pad_lo, pad_hi = n_in - MLA_ROPE, MLA_NOPE, LANES - MLA_NOPE - MLA_ROPE
    w_in = full_weight("ev_in", h0)[0]
    zeros_in = jnp.zeros((w_in.shape[0], pad_lo), BF16)
    w_ev_in = (jnp.concatenate([w_in[:, :kpe0], zeros_in, w_in[:, kpe0:], zeros_in[:, :pad_hi]], axis=1), False)

    def unpad_in(dw):
        return jnp.concatenate([dw[:, :kpe0], dw[:, kpe0 + pad_lo:kpe0 + pad_lo + MLA_ROPE]], axis=1)

    p0 = _matmul(h0, w_ev_in[0], "nn", "ev_in_fwd", after=gather_tokens)
    small = (ev_conv_w, ev_conv_b, ev_w_rg_a, ev_b_rg_a, ev_w_rg_x, ev_b_rg_x, ev_lru_lambda, ev_q_norm_g, ev_w_uq,
             ev_kv_norm_g, ev_w_ukv)
    y0, vjp_even = jax.vjp(lambda p, *s: _even_mixer(p, *s, (conv_w_full, w_uq_full, w_ukv_full), bsz), p0, *small)
    w_out = full_weight("ev_out", y0)[0]
    lru_w, d_model = y0.shape[1] - MLA_HEADS * LANES, w_out.shape[1]
    w_att = w_out[lru_w:].reshape(MLA_HEADS, MLA_V, d_model)
    w_att = jnp.concatenate([jnp.zeros((MLA_HEADS, LANES - MLA_V, d_model), BF16), w_att], axis=1)
    w_ev_out = (jnp.concatenate([w_out[:lru_w], w_att.reshape(MLA_HEADS * LANES, d_model)], axis=0), False)

    def unpad_out(dw):
        d_att = dw[lru_w:].reshape(MLA_HEADS, LANES, d_model)[:, LANES - MLA_V:].reshape(MLA_HEADS * MLA_V, d_model)
        return jnp.concatenate([dw[:lru_w], d_att], axis=0)

    mix0 = linear_fwd("ev_out", y0, w_ev_out)
    ln0_args = (h0, mix0, ln_mix_g[0:1], ln_mix_b[0:1])
    h1, h1_16 = ln_fwd("mix0_ln", *ln0_args)
    (h2, h2_16), res_mlp0 = mlp_fwd(h1, h1_16, 0)
    w_od_in = full_weight("od_in", h2_16)
    p1 = linear_fwd("od_in", h2_16, w_od_in)
    y1, res_odd = _odd_mixer_fwd(p1, bsz)
    w_od_out = full_weight("od_out", y1)
    mix1 = linear_fwd("od_out", y1, w_od_out)
    ln1_args = (h2, mix1, ln_mix_g[1:2], ln_mix_b[1:2])
    h3, h3_16 = ln_fwd("mix1_ln", *ln1_args)
    (h4, _), res_mlp1 = mlp_fwd(h3, h3_16, 1)
    loss_local, vjp_loss = jax.vjp(lambda h: _local_loss(h, loss_target), h4)

    dh4 = vjp_loss(jnp.ones((), F32))
    dh3, dg_mlp1, db_mlp1 = mlp_bwd(dh4, res_mlp1, 1)
    dh2, dmix1, dg_mix1, db_mix1 = ln_bwd("mix1_ln", ln1_args, dh3)
    dp1 = _odd_mixer_bwd(res_odd, linear_bwd("od_out", y1, w_od_out, dmix1, False))
    dh2 = (dh2, linear_bwd("od_in", h2_16, w_od_in, dp1, True))
    dh1, dg_mlp0, db_mlp0 = mlp_bwd(dh2, res_mlp0, 0)
    dh0, dmix0, dg_mix0, db_mix0 = ln_bwd("mix0_ln", ln0_args, dh1)
    dp0, *dsmall = vjp_even(linear_bwd("ev_out", y0, w_ev_out, dmix0, False, unpad=unpad_out))
    dh0 = dh0 + linear_bwd("ev_in", h0, w_ev_in, dp0.astype(BF16), True, unpad=unpad_in)
    g_meta_full, grad_x = vjp_embed(dh0)
    (g_conv_w, g_conv_b, g_w_rg_a, g_b_rg_a, g_w_rg_x, g_b_rg_x, g_lambda, g_q_norm, g_w_uq, g_kv_norm, g_w_ukv) = dsmall

    after, summed = grad_x, {}
    for nm, own, handle in pending:
        land = _exchange_wait(handle, False, after, "rs_wait_" + nm)
        summed[nm] = after = _sum_own_and_peers(own, land, "rs_sum_" + nm)

    g_meta = _scatter_grad(g_meta_full, True, "meta_rs", after=(after,))
    rep_names = ("ev_conv_b", "ev_w_rg_a", "ev_b_rg_a", "ev_w_rg_x", "ev_b_rg_x", "ev_lru_lambda", "ev_q_norm_g",
                 "ev_kv_norm_g", "ln_mix_g", "ln_mix_b", "ln_mlp_g", "ln_mlp_b")
    rep_local = (g_conv_b, g_w_rg_a, g_b_rg_a, g_w_rg_x, g_b_rg_x, g_lambda, g_q_norm, g_kv_norm,
                 jnp.concatenate([dg_mix0, dg_mix1]), jnp.concatenate([db_mix0, db_mix1]),
                 jnp.concatenate([dg_mlp0, dg_mlp1]), jnp.concatenate([db_mlp0, db_mlp1]))
    grad_w = dict(zip(rep_names, _allreduce_replicated(rep_local, "rep", after=(g_meta,))))
    grad_w.update(meta_tokens=g_meta, ev_conv_w=g_conv_w, ev_w_uq=g_w_uq, ev_w_ukv=g_w_ukv)
    grad_w.update(ev_w_in=summed["ev_in"][None], ev_w_out=summed["ev_out"][None], od_w_in=summed["od_in"][None],
                  od_w_out=summed["od_out"][None], mlp_w1=jnp.stack([summed["mlp0_w1"], summed["mlp1_w1"]]),
                  mlp_w2=jnp.stack([summed["mlp0_w2"], summed["mlp1_w2"]]))

    loss = lax.psum(loss_local, MESH_AXES)
    delta, new_m, new_v = {}, {}, {}
    for n in _WEIGHTS:
        w, g, m, v = weights[n], grad_w[n], args["m_" + n], args["v_" + n]
        two_d = (-1, w.shape[-1])
        d2, m2, v2 = _adamw(w.reshape(two_d), g.reshape(two_d), m.reshape(two_d), v.reshape(two_d), "adamw_" + n)
        delta[n], new_m[n], new_v[n] = d2.reshape(w.shape), m2.reshape(w.shape), v2.reshape(w.shape)
    return (loss, grad_x, *[grad_w[n] for n in _WEIGHTS], *[delta[n] for n in _WEIGHTS],
            *[new_m[n] for n in _WEIGHTS], *[new_v[n] for n in _WEIGHTS])
```

```python
import functools
import math

import jax
import jax.numpy as jnp
from jax import lax
from jax.experimental import pallas as pl
from jax.experimental.pallas import tpu as pltpu

F32 = jnp.float32
BF16 = jnp.bfloat16

N_DEV = 8
MESH_AXES = ("x", "y", "c")
LANES = 128
SEQ_BLOCK = 128

N_META = 16
LRU_C = 8.0
MLA_HEADS = 8
MLA_NOPE = 64
MLA_ROPE = 32
MLA_V = 64
RET_HEADS = 4
ROPE_BASE = 10000.0
DEPTH = 2
DN_ALPHA = (2 * DEPTH) ** 0.25
EPS = 1e-5
NEG_INF = -1e30

ADAM_LR = 0.001
ADAM_B1 = 0.9
ADAM_B2 = 0.999
ADAM_EPS = 1e-08
ADAM_WD = 0.01
ADAM_STEP = 10

VMEM_LIMIT = 56 * 1024 * 1024


def _params(*sem):
    return pltpu.CompilerParams(dimension_semantics=sem, vmem_limit_bytes=VMEM_LIMIT)


def _pick(n, cands):
    for c in cands:
        if n % c == 0:
            return c
    return n


def _row_tile(r, width):
    cands = (256, 128, 64, 32, 16, 8) if width <= 1024 else (128, 64, 32, 16, 8)
    return _pick(r, cands)


_DIMS = {"nn": (((1,), (0,)), ((), ())), "nt": (((1,), (1,)), ((), ())), "tn": (((0,), (0,)), ((), ()))}


def _dot(a, b, mode):
    return lax.dot_general(a.astype(BF16), b.astype(BF16), _DIMS[mode], preferred_element_type=F32)


def _matmul(a, b, mode, name, after=(), stacked=False, a_relu2=False, relu2_bwd_of=None, out_dtype=F32,
            relu2_copy=False):
    if stacked:
        n_blk = b.shape[2] if mode != "tn" else b.shape[1] // N_DEV
    if mode == "nn":
        (m, k), n = a.shape, (N_DEV * n_blk if stacked else b.shape[1])
    elif mode == "nt":
        (m, k), n = a.shape, (b.shape[1] if stacked else b.shape[0])
    else:
        (k, m), n = a.shape, b.shape[1]
    tm = _pick(m, (1088, 1024, 544, 512, 272, 256, 128, 64, 32, 16, 8))
    tn = _pick(n, (1024, 512, 256, 128))
    tk = _pick(k, (1088, 1024, 544, 512, 272, 256, 128))
    kb = 2
    if stacked and mode == "nn":
        tn = n_blk
    if stacked and mode == "tn":
        tn = kb * n_blk
    if stacked and mode == "nt":
        tk = kb * n_blk
    nk = k // tk
    assert out_dtype == F32 or (nk == 1 and not (stacked and mode == "tn")), "narrow results need a single k step"
    assert not relu2_copy or (nk == 1 and mode == "nn"), "the relu^2 copy is taken from a finished tile"

    out_spec = pl.BlockSpec((tm, tn), lambda i, j, kk: (i, j))
    out_shape = jax.ShapeDtypeStruct((m, n), out_dtype)
    if mode == "nn":
        a_spec = pl.BlockSpec((tm, tk), lambda i, j, kk: (i, kk))
        b_spec = pl.BlockSpec((tk, tn), lambda i, j, kk: (kk, j))
        if stacked:
            b_spec = pl.BlockSpec((None, tk, tn), lambda i, j, kk: (j, kk, 0))
    elif mode == "nt":
        a_spec = pl.BlockSpec((tm, tk), lambda i, j, kk: (i, kk))
        b_spec = pl.BlockSpec((tn, tk), lambda i, j, kk: (j, kk))
        if stacked:
            b_spec = pl.BlockSpec((kb, tn, n_blk), lambda i, j, kk: (kk, j, 0))
    else:
        a_spec = pl.BlockSpec((tk, tm), lambda i, j, kk: (kk, i))
        b_spec = pl.BlockSpec((tk, tn), lambda i, j, kk: (kk, j))
        if stacked:
            out_spec = pl.BlockSpec((kb, tm, n_blk), lambda i, j, kk: (j, i, 0))
            out_shape = jax.ShapeDtypeStruct((N_DEV, m, n_blk), F32)
    extra = [] if relu2_bwd_of is None else [relu2_bwd_of]
    extra_specs = [pl.BlockSpec((tm, tn), lambda i, j, kk: (i, j))] * len(extra)

    def body(a_ref, b_ref, *rest):
        o_ref = rest[-2] if relu2_copy else rest[-1]
        kk = pl.program_id(2)
        av = a_ref[...]
        if a_relu2:
            av = jnp.maximum(av, 0.0)
            av = av * av
        if stacked and mode == "nt":
            part = _dot(av[:, :n_blk], b_ref[0], mode)
            for q in range(1, kb):
                part = part + _dot(av[:, q * n_blk:(q + 1) * n_blk], b_ref[q], mode)
        else:
            part = _dot(av, b_ref[...], mode)
        if stacked and mode == "tn":
            part = jnp.stack([part[:, q * n_blk:(q + 1) * n_blk] for q in range(kb)])
        if out_dtype != F32:
            if relu2_bwd_of is not None:
                part = part * (2.0 * jnp.maximum(rest[0][...], 0.0))
            o_ref[...] = part.astype(out_dtype)
            return
        if relu2_copy:
            r = jnp.maximum(part, 0.0)
            rest[-1][...] = (r * r).astype(BF16)

        @pl.when(kk == 0)
        def _():
            o_ref[...] = part

        @pl.when(kk != 0)
        def _():
            o_ref[...] += part

        if relu2_bwd_of is not None:
            @pl.when(kk == nk - 1)
            def _():
                o_ref[...] *= 2.0 * jnp.maximum(rest[0][...], 0.0)

    return pl.pallas_call(
        body,
        name=name,
        grid=(m // tm, n // tn, nk),
        in_specs=[a_spec, b_spec] + extra_specs + [pl.BlockSpec(memory_space=pl.ANY)] * len(after),
        out_specs=[out_spec, out_spec] if relu2_copy else out_spec,
        out_shape=[out_shape, jax.ShapeDtypeStruct((m, n), BF16)] if relu2_copy else out_shape,
        compiler_params=_params("parallel", "parallel", "arbitrary"),
    )(a, b, *extra, *after)


def _group_matmul(a, w, mode, name):
    if mode in ("nn", "nt"):
        g, dk, dn = w.shape
        m = a.shape[0]
        d_in, d_out = (dk, dn) if mode == "nn" else (dn, dk)
        tm = _pick(m, (1088, 1024, 544, 512, 272, 256, 128, 64, 32, 16, 8))

        def body(a_ref, w_ref, o_ref):
            o_ref[...] = _dot(a_ref[...], w_ref[0], mode)

        return pl.pallas_call(
            body,
            name=name,
            grid=(g, m // tm),
            in_specs=[pl.BlockSpec((tm, d_in), lambda h, i: (i, h)), pl.BlockSpec((1, dk, dn), lambda h, i: (h, 0, 0))],
            out_specs=pl.BlockSpec((tm, d_out), lambda h, i: (i, h)),
            out_shape=jax.ShapeDtypeStruct((m, g * d_out), F32),
            compiler_params=_params("parallel", "parallel"),
        )(a, w)
    b = w
    m = a.shape[0]
    dk = dn = LANES
    g = a.shape[1] // dk
    tm = _pick(m, (1088, 1024, 544, 512, 272, 256, 128, 64, 32, 16, 8))

    def body(a_ref, b_ref, o_ref):
        part = _dot(a_ref[...], b_ref[...], "tn")

        @pl.when(pl.program_id(1) == 0)
        def _():
            o_ref[0] = part

        @pl.when(pl.program_id(1) != 0)
        def _():
            o_ref[0] += part

    return pl.pallas_call(
        body,
        name=name,
        grid=(g, m // tm),
        in_specs=[pl.BlockSpec((tm, dk), lambda h, i: (i, h)), pl.BlockSpec((tm, dn), lambda h, i: (i, h))],
        out_specs=pl.BlockSpec((1, dk, dn), lambda h, i: (h, 0, 0)),
        out_shape=jax.ShapeDtypeStruct((g, dk, dn), F32),
        compiler_params=_params("parallel", "arbitrary"),
    )(a, b)


def _make_group_linear(name):
    @jax.custom_vjp
    def op(x, w):
        return _group_matmul(x, w, "nn", name + "_fwd")

    def fwd(x, w):
        return op(x, w), (x, w)

    def bwd(res, dy):
        x, w = res
        return _group_matmul(dy, w, "nt", name + "_dx"), _group_matmul(x, dy, "tn", name + "_dw")

    op.defvjp(fwd, bwd)
    return op


def _my_place():
    return lax.axis_index("x"), lax.axis_index("y"), lax.axis_index("c")


def _all_gather(shard, name, after=()):
    shape, dtype = shard.shape, shard.dtype

    def body(x_ref, *rest):
        out_ref, send_sems, recv_sems, local_sem = rest[len(after):]
        x, y, c = _my_place()
        me, sibling = (x, y, c), (x, y, 1 - c)
        chips = [(1 - x, y), (x, 1 - y), (1 - x, 1 - y)]

        def slot(px, py, pc):
            return out_ref.at[4 * px + 2 * py + pc]

        def copy(k, block, to, src=None):
            return pltpu.make_async_remote_copy(
                src_ref=slot(*block) if src is None else src,
                dst_ref=slot(*block),
                send_sem=send_sems.at[k],
                recv_sem=recv_sems.at[k],
                device_id=to,
                device_id_type=pl.DeviceIdType.MESH,
            )

        mine = pltpu.make_async_copy(x_ref, slot(*me), local_sem)
        mine.start()
        first = [copy(0, me, sibling, src=x_ref)]
        first += [copy(1 + j, me, (*chip, c), src=x_ref) for j, chip in enumerate(chips)]
        for cp in first:
            cp.start()
        passed = [copy(4 + j, (*chip, c), sibling) for j, chip in enumerate(chips)]
        for j, chip in enumerate(chips):
            copy(1 + j, (*chip, c), me).wait_recv()
            passed[j].start()
        copy(0, sibling, me).wait_recv()
        for j, chip in enumerate(chips):
            copy(4 + j, (*chip, 1 - c), me).wait_recv()
        for cp in first + passed:
            cp.wait_send()
        mine.wait()

    return pl.pallas_call(
        body,
        name=name,
        out_shape=jax.ShapeDtypeStruct((N_DEV,) + shape, dtype),
        in_specs=[pl.BlockSpec(memory_space=pl.ANY)] * (1 + len(after)),
        out_specs=pl.BlockSpec(memory_space=pl.ANY),
        scratch_shapes=[pltpu.SemaphoreType.DMA((7,)), pltpu.SemaphoreType.DMA((7,)), pltpu.SemaphoreType.DMA],
    )(shard, *after)


def _all_to_all(stacked, name, after=()):
    def body(x_ref, *rest):
        out_ref, send_sems, recv_sems, local_sem = rest[len(after):]
        x, y, c = _my_place()
        me = 4 * x + 2 * y + c
        mine = pltpu.make_async_copy(x_ref.at[me], out_ref.at[me], local_sem)
        mine.start()
        copies = []
        for k in range(1, N_DEV):
            px, py, pc = x ^ ((k >> 2) & 1), y ^ ((k >> 1) & 1), c ^ (k & 1)
            peer = 4 * px + 2 * py + pc
            copies.append(
                pltpu.make_async_remote_copy(
                    src_ref=x_ref.at[peer],
                    dst_ref=out_ref.at[me],
                    send_sem=send_sems.at[k - 1],
                    recv_sem=recv_sems.at[k - 1],
                    device_id=(px, py, pc),
                    device_id_type=pl.DeviceIdType.MESH,
                )
            )
        for cp in copies:
            cp.start()
        for cp in copies:
            cp.wait_recv()
        for cp in copies:
            cp.wait_send()
        mine.wait()

    return pl.pallas_call(
        body,
        name=name,
        out_shape=jax.ShapeDtypeStruct(stacked.shape, stacked.dtype),
        in_specs=[pl.BlockSpec(memory_space=pl.ANY)] * (1 + len(after)),
        out_specs=pl.BlockSpec(memory_space=pl.ANY),
        scratch_shapes=[pltpu.SemaphoreType.DMA((7,)), pltpu.SemaphoreType.DMA((7,)), pltpu.SemaphoreType.DMA],
    )(stacked, *after)


def _sum_blocks(stacked, name):
    _, r, c = stacked.shape
    tr = _pick(r, (256, 128, 64, 32, 16, 8))

    def body(x_ref, o_ref):
        s = [x_ref[j] for j in range(N_DEV)]
        o_ref[...] = ((s[0] + s[1]) + (s[2] + s[3])) + ((s[4] + s[5]) + (s[6] + s[7]))

    return pl.pallas_call(
        body,
        name=name,
        grid=(r // tr,),
        in_specs=[pl.BlockSpec((N_DEV, tr, c), lambda i: (0, i, 0))],
        out_specs=pl.BlockSpec((tr, c), lambda i: (i, 0)),
        out_shape=jax.ShapeDtypeStruct((r, c), stacked.dtype),
        compiler_params=_params("parallel"),
    )(stacked)


def _stack_cols(full):
    k, n8 = full.shape
    return full.reshape(k, N_DEV, n8 // N_DEV).transpose(1, 0, 2)


def _unstack_cols(stacked):
    j, k, n = stacked.shape
    return stacked.transpose(1, 0, 2).reshape(k, j * n)


def _split_cols(p, cuts):
    bounds = (0,) + tuple(cuts) + (p.shape[1],)

    @jax.custom_vjp
    def op(z):
        return tuple(z[:, lo:hi] for lo, hi in zip(bounds[:-1], bounds[1:]))

    op.defvjp(lambda z: (op(z), None), lambda _, cots: (jnp.concatenate(cots, axis=1),))
    return op(p)


def _gather_weight(shard, cols, name):
    g = _all_gather(shard.astype(BF16), name)
    return _unstack_cols(g) if cols else g.reshape(-1, shard.shape[1])


def _scatter_grad(full, cols, name, after=()):
    if cols:
        st = _stack_cols(full)
    else:
        st = full.reshape(N_DEV, full.shape[0] // N_DEV, full.shape[1])
    return _sum_blocks(_all_to_all(st, name + "_a2a", after), name + "_sum")


def _make_fsdp_linear(cols, name, unpad=None):
    @jax.custom_vjp
    def op(x, w_shard, w_full):
        return _matmul(x, w_full, "nn", name + "_fwd")

    def fwd(x, w_shard, w_full):
        return op(x, w_shard, w_full), (x, w_full)

    def bwd(res, dy):
        x, w = res
        dx = _matmul(dy, w, "nt", name + "_dx")
        dw = _matmul(x, dy, "tn", name + "_dw")
        dw = dw if unpad is None else unpad(dw)
        return dx, _scatter_grad(dw, cols, name + "_rs"), jnp.zeros_like(w)

    op.defvjp(fwd, bwd)
    return op


def _make_fsdp_param(name):
    @jax.custom_vjp
    def op(shard, full):
        return full

    def fwd(shard, full):
        return full, None

    def bwd(_, g):
        return _scatter_grad(g, True, name + "_rs"), jnp.zeros_like(g)

    op.defvjp(fwd, bwd)
    return op


def _allreduce_replicated(gs, name, after=()):
    flat = jnp.concatenate([g.reshape(-1) for g in gs])
    n = flat.shape[0]
    rows = -(-n // (256 * LANES)) * 256
    packed = jnp.pad(flat, (0, rows * LANES - n)).reshape(rows, LANES)
    total = _sum_blocks(_all_gather(packed, name + "_ag", after), name + "_sum").reshape(-1)
    out, off = [], 0
    for g in gs:
        out.append(total[off:off + g.size].reshape(g.shape))
        off += g.size
    return out


_HBM = pl.BlockSpec(memory_space=pltpu.HBM)
_SEM = pl.BlockSpec(memory_space=pltpu.SEMAPHORE)
_SIDE_EFFECT = pltpu.SideEffectType.DATAFLOW_SIDE_EFFECTING
_N_PEERS = N_DEV - 1


def _peer(k):
    x, y, c = _my_place()
    return x ^ ((k >> 2) & 1), y ^ ((k >> 1) & 1), c ^ (k & 1)


def _exchange_start(src, land_shape, gather, name, after=()):
    def body(src_ref, land_ref, *rest):
        send_sems, recv_sems, src_thru, land_thru, token = rest[len(after):]
        x, y, c = _my_place()
        me = 4 * x + 2 * y + c
        for k in range(1, N_DEV):
            px, py, pc = _peer(k)
            pltpu.make_async_remote_copy(
                src_ref=src_ref if gather else src_ref.at[4 * px + 2 * py + pc],
                dst_ref=land_ref.at[me] if gather else land_ref.at[k - 1],
                send_sem=send_sems.at[k - 1],
                recv_sem=recv_sems.at[k - 1],
                device_id=(px, py, pc),
                device_id_type=pl.DeviceIdType.MESH,
            ).start()
        token[...] = jnp.zeros_like(token)

    return pl.pallas_call(
        body,
        name=name,
        out_shape=(
            pltpu.SemaphoreType.DMA((_N_PEERS,)),
            pltpu.SemaphoreType.DMA((_N_PEERS,)),
            pltpu.HBM(src.shape, src.dtype),
            pltpu.HBM(land_shape, src.dtype),
            jax.ShapeDtypeStruct((8, LANES), F32),
        ),
        in_specs=(_HBM, _HBM) + (pl.BlockSpec(memory_space=pl.ANY),) * len(after),
        out_specs=(_SEM, _SEM, _HBM, _HBM, pl.BlockSpec(memory_space=pltpu.VMEM)),
        input_output_aliases={0: 2, 1: 3},
        compiler_params=pltpu.CompilerParams(has_side_effects=_SIDE_EFFECT),
    )(pltpu.with_memory_space_constraint(src, pltpu.HBM),
      pltpu.with_memory_space_constraint(lax.empty(land_shape, src.dtype), pltpu.HBM), *after)


def _gather_start_all(shards, name):
    n = len(shards)

    def body(*refs):
        srcs, lands = refs[:n], refs[n:2 * n]
        outs = refs[2 * n:]
        send_sems, recv_sems, token = outs[:n], outs[n:2 * n], outs[-1]
        x, y, c = _my_place()
        me = 4 * x + 2 * y + c
        for i in range(n):
            for k in range(1, N_DEV):
                pltpu.make_async_remote_copy(
                    src_ref=srcs[i],
                    dst_ref=lands[i].at[me],
                    send_sem=send_sems[i].at[k - 1],
                    recv_sem=recv_sems[i].at[k - 1],
                    device_id=_peer(k),
                    device_id_type=pl.DeviceIdType.MESH,
                ).start()
        token[...] = jnp.zeros_like(token)

    lands = [(N_DEV,) + s.shape for s in shards]
    sems = tuple(pltpu.SemaphoreType.DMA((_N_PEERS,)) for _ in range(2 * n))
    res = pl.pallas_call(
        body,
        name=name,
        out_shape=sems + tuple(pltpu.HBM(s.shape, s.dtype) for s in shards)
        + tuple(pltpu.HBM(ls, s.dtype) for ls, s in zip(lands, shards)) + (jax.ShapeDtypeStruct((8, LANES), F32),),
        in_specs=(_HBM,) * (2 * n),
        out_specs=(_SEM,) * (2 * n) + (_HBM,) * (2 * n) + (pl.BlockSpec(memory_space=pltpu.VMEM),),
        input_output_aliases={i: 2 * n + i for i in range(2 * n)},
        compiler_params=pltpu.CompilerParams(has_side_effects=_SIDE_EFFECT),
    )(*[pltpu.with_memory_space_constraint(s, pltpu.HBM) for s in shards],
      *[pltpu.with_memory_space_constraint(lax.empty(ls, s.dtype), pltpu.HBM) for ls, s in zip(lands, shards)])
    return [(res[i], res[n + i], res[2 * n + i], res[3 * n + i], res[-1]) for i in range(n)]


def _exchange_wait(handle, gather, after, name):
    send_sems, recv_sems, src_thru, land_thru, _ = handle

    def body(src_ref, land_ref, send_sems, recv_sems, after_ref, src_dead, got_ref):
        for k in range(1, N_DEV):
            cp = pltpu.make_async_remote_copy(
                src_ref=src_ref if gather else src_ref.at[k],
                dst_ref=land_ref.at[k - 1],
                send_sem=send_sems.at[k - 1],
                recv_sem=recv_sems.at[k - 1],
                device_id=_peer(k),
                device_id_type=pl.DeviceIdType.MESH,
            )
            cp.wait_send()
            cp.wait_recv()

    return pl.pallas_call(
        body,
        name=name,
        out_shape=(pltpu.HBM(src_thru.shape, src_thru.dtype), pltpu.HBM(land_thru.shape, land_thru.dtype)),
        in_specs=(_HBM, _HBM, _SEM, _SEM, pl.BlockSpec(memory_space=pl.ANY)),
        out_specs=(_HBM, _HBM),
        input_output_aliases={0: 0, 1: 1},
        compiler_params=pltpu.CompilerParams(has_side_effects=_SIDE_EFFECT),
    )(src_thru, land_thru, send_sems, recv_sems, after)[1]


def _sum_own_and_peers(own, land, name):
    r, c = own.shape
    tr = _pick(r, (256, 128, 64, 32, 16, 8))

    def body(o_ref, l_ref, out_ref):
        s = [l_ref[j] for j in range(_N_PEERS)]
        out_ref[...] = ((o_ref[...] + s[0]) + (s[1] + s[2])) + ((s[3] + s[4]) + (s[5] + s[6]))

    return pl.pallas_call(
        body,
        name=name,
        grid=(r // tr,),
        in_specs=[pl.BlockSpec((tr, c), lambda i: (i, 0)), pl.BlockSpec((_N_PEERS, tr, c), lambda i: (0, i, 0))],
        out_specs=pl.BlockSpec((tr, c), lambda i: (i, 0)),
        out_shape=jax.ShapeDtypeStruct((r, c), own.dtype),
        compiler_params=_params("parallel"),
    )(own, land)


class _Cols:
    def __init__(self, array, width, block):
        self.array, self.width, self.block = array, width, block
        self.shape, self.dtype = (array.shape[0], width), array.dtype


def _base(a):
    return a.array if isinstance(a, _Cols) else a


def _col_block(a):
    return a.block if isinstance(a, _Cols) else 0


def _make_rowwise(f, name, n_rows, n_tabs, n_pars):
    n_in = n_rows + n_tabs + n_pars

    def specs(args, tm):
        blocked = [pl.BlockSpec((tm, a.shape[1]), lambda i, blk=_col_block(a): (i, blk)) for a in args[: n_rows + n_tabs]]
        whole = [pl.BlockSpec(a.shape, lambda i: (0, 0)) for a in args[n_rows + n_tabs:]]
        return blocked + whole

    def out_struct(args, tm):
        blk = [jax.ShapeDtypeStruct((tm, a.shape[1]), a.dtype) for a in args[: n_rows + n_tabs]]
        blk += [jax.ShapeDtypeStruct(a.shape, a.dtype) for a in args[n_rows + n_tabs:]]
        return jax.eval_shape(f, *blk)

    def fwd_call(*args):
        r = args[0].shape[0]
        tm = _row_tile(r, max(a.shape[1] for a in args[:n_rows]))
        ro, so = out_struct(args, tm)

        def body(*refs):
            vals = [x[...] for x in refs[:n_in]]
            outs = refs[n_in:]
            rv, sv = f(*vals)
            for o, v in zip(outs[: len(ro)], rv):
                o[...] = v
            for o, v in zip(outs[len(ro):], sv):
                @pl.when(pl.program_id(0) == 0)
                def _(o=o, v=v):
                    o[...] = v

                @pl.when(pl.program_id(0) != 0)
                def _(o=o, v=v):
                    o[...] += v

        out_shape = [jax.ShapeDtypeStruct((r, s.shape[1]), s.dtype) for s in ro]
        out_shape += [jax.ShapeDtypeStruct(s.shape, s.dtype) for s in so]
        out_specs = [pl.BlockSpec((tm, s.shape[1]), lambda i: (i, 0)) for s in ro]
        out_specs += [pl.BlockSpec(s.shape, lambda i: (0, 0)) for s in so]
        res = pl.pallas_call(
            body,
            name=name + "_fwd",
            grid=(r // tm,),
            in_specs=specs(args, tm),
            out_specs=out_specs,
            out_shape=out_shape,
            compiler_params=_params("arbitrary" if so else "parallel"),
        )(*[_base(a) for a in args])
        return tuple(res[: len(ro)]), tuple(res[len(ro):])

    def bwd_call(args, cots, more=(), row_dtypes=None):
        r = args[0].shape[0]
        tm = _row_tile(r, max(a.shape[1] for a in args[:n_rows]))
        ro, so = out_struct(args, tm)
        crow, csum = cots
        rows, tabs, pars = args[:n_rows], args[n_rows:n_rows + n_tabs], args[n_rows + n_tabs:]
        n_c = len(crow) + len(csum)

        def body(*refs):
            vals = [x[...] for x in refs[:n_in]]
            cv = [x[...] for x in refs[n_in:n_in + n_c]]
            for x in refs[n_in + n_c:n_in + n_c + len(more)]:
                cv[0] = cv[0] + x[...]
            outs = refs[n_in + n_c + len(more):]
            tv = vals[n_rows:n_rows + n_tabs]

            def g(*dargs):
                return f(*dargs[:n_rows], *tv, *dargs[n_rows:])

            _, vjp = jax.vjp(g, *vals[:n_rows], *vals[n_rows + n_tabs:])
            d = vjp((tuple(cv[: len(crow)]), tuple(cv[len(crow):])))
            for o, v in zip(outs[:n_rows], d[:n_rows]):
                o[...] = v.astype(o.dtype)
            for o, v in zip(outs[n_rows:], d[n_rows:]):
                @pl.when(pl.program_id(0) == 0)
                def _(o=o, v=v):
                    o[...] = v

                @pl.when(pl.program_id(0) != 0)
                def _(o=o, v=v):
                    o[...] += v

        in_specs = specs(args, tm)
        in_specs += [pl.BlockSpec((tm, c.shape[1]), lambda i: (i, 0)) for c in crow]
        in_specs += [pl.BlockSpec(c.shape, lambda i: (0, 0)) for c in csum]
        in_specs += [pl.BlockSpec((tm, c.shape[1]), lambda i: (i, 0)) for c in more]
        out_shape = [jax.ShapeDtypeStruct(a.shape, dt) for a, dt in zip(rows, row_dtypes or [a.dtype for a in rows])]
        out_shape += [jax.ShapeDtypeStruct(a.shape, a.dtype) for a in pars]
        out_specs = [pl.BlockSpec((tm, a.shape[1]), lambda i: (i, 0)) for a in rows]
        out_specs += [pl.BlockSpec(a.shape, lambda i: (0, 0)) for a in pars]
        res = pl.pallas_call(
            body,
            name=name + "_bwd",
            grid=(r // tm,),
            in_specs=in_specs,
            out_specs=out_specs,
            out_shape=out_shape,
            compiler_params=_params("arbitrary" if pars else "parallel"),
        )(*[_base(a) for a in args], *crow, *csum, *more)
        return tuple(res[:n_rows]), tuple(res[n_rows:])

    @jax.custom_vjp
    def op(rows, tabs, pars):
        return fwd_call(*rows, *tabs, *pars)

    op.fwd_call, op.bwd_call = fwd_call, bwd_call

    def fwd(rows, tabs, pars):
        return fwd_call(*rows, *tabs, *pars), (rows, tabs, pars)

    def bwd(res, cots):
        rows, tabs, pars = res
        drows, dpars = bwd_call(tuple(rows) + tuple(tabs) + tuple(pars), cots)
        return drows, tuple(jnp.zeros_like(t) for t in tabs), dpars

    op.defvjp(fwd, bwd)
    return op


def _sigmoid(x):
    return 0.5 * (jnp.tanh(0.5 * x) + 1.0)


@jax.custom_jvp
def _softplus(x):
    e = jnp.exp(-jnp.abs(x))
    u = 1.0 + e
    log1p_e = jnp.where(u == 1.0, e, e * jnp.log(u) / jnp.where(u == 1.0, 1.0, u - 1.0))
    return jnp.maximum(x, 0.0) + log1p_e


@_softplus.defjvp
def _softplus_jvp(primals, tangents):
    (x,), (t,) = primals, tangents
    return _softplus(x), t * _sigmoid(x)


def _gelu(x):
    return 0.5 * x * (1.0 + jnp.tanh(math.sqrt(2.0 / math.pi) * (x + 0.044715 * (x * x * x))))


def _ln_res_f(h, mix, g, b):
    z = DN_ALPHA * h + mix
    mu = jnp.mean(z, axis=-1, keepdims=True)
    zc = z - mu
    var = jnp.mean(zc * zc, axis=-1, keepdims=True)
    return (zc * lax.rsqrt(var + EPS) * g + b,), ()


def _ln_res_copy_f(h, mix, g, b):
    (out,), _ = _ln_res_f(h, mix, g, b)
    return (out, out.astype(BF16)), ()


def _rmsnorm_f(x, g):
    return (x * lax.rsqrt(jnp.mean(x * x, axis=-1, keepdims=True) + EPS) * g,), ()


def _lru_gates_f(ga, gx, xc, b_a, b_x, lam):
    r = _sigmoid(ga + b_a)
    i = _sigmoid(gx + b_x)
    log_a = -LRU_C * r * _softplus(-lam)
    a = jnp.exp(log_a)
    one_minus_a2 = jnp.tanh(-log_a) * (jnp.exp(2.0 * log_a) + 1.0)
    return (a, jnp.sqrt(one_minus_a2) * (i * xc)), ()


def _lru_out_f(hh, p_gate):
    return (hh * _gelu(p_gate),), ()


def _rope_ret_f(q, k, cos2, sin2):
    d = cos2.shape[1]
    half = d // 2
    k_scale = d ** -0.5

    def rope(x):
        outs = []
        for h in range(x.shape[1] // d):
            xh = x[:, h * d:(h + 1) * d]
            rot = jnp.concatenate([xh[:, half:], xh[:, :half]], axis=1)
            outs.append(xh * cos2 + rot * sin2)
        return jnp.concatenate(outs, axis=1)

    return (rope(q), rope(k) * k_scale), ()


def _ret_out_f(o, g):
    d = o.shape[1] // RET_HEADS
    outs = []
    for h in range(RET_HEADS):
        oh = o[:, h * d:(h + 1) * d]
        outs.append(oh * lax.rsqrt(jnp.mean(oh * oh, axis=-1, keepdims=True) + EPS))
    y = jnp.concatenate(outs, axis=1)
    return (g * _sigmoid(g) * y,), ()


def _loss_f(y, t, mask):
    e = (y - t) * mask
    per_row = jnp.sum(e * e, axis=-1, keepdims=True) * (0.5 / y.shape[1])
    total = jnp.sum(per_row, axis=0, keepdims=True)
    return (), (jnp.broadcast_to(total, (1, LANES)),)


def _shift_down(x, s):
    if s == 0:
        return x
    t = x.shape[0]
    row = lax.broadcasted_iota(jnp.int32, x.shape, 0)
    return jnp.where(row >= s, pltpu.roll(x, s, 0), 0.0)


def _shift_up(x, s):
    if s == 0:
        return x
    t = x.shape[0]
    row = lax.broadcasted_iota(jnp.int32, x.shape, 0)
    return jnp.where(row < t - s, pltpu.roll(x, t - s, 0), 0.0)


def _conv_fwd(x, w, b, name):
    bsz, t, c = x.shape
    width = w.shape[0]

    def body(x_ref, w_ref, b_ref, y_ref):
        xv = x_ref[0]
        acc = jnp.broadcast_to(b_ref[...], xv.shape)
        for k in range(width):
            acc = acc + w_ref[k:k + 1, :] * _shift_down(xv, width - 1 - k)
        y_ref[0] = acc

    return pl.pallas_call(
        body,
        name=name,
        grid=(bsz, c // LANES),
        in_specs=[
            pl.BlockSpec((1, t, LANES), lambda i, j: (i, 0, j)),
            pl.BlockSpec((width, LANES), lambda i, j: (0, j)),
            pl.BlockSpec((1, LANES), lambda i, j: (0, j)),
        ],
        out_specs=pl.BlockSpec((1, t, LANES), lambda i, j: (i, 0, j)),
        out_shape=jax.ShapeDtypeStruct(x.shape, F32),
        compiler_params=_params("parallel", "parallel"),
    )(x, w, b)


def _conv_bwd(x, w, dy, name):
    bsz, t, c = x.shape
    width = w.shape[0]

    def body(x_ref, w_ref, dy_ref, dx_ref, dw_ref, db_ref):
        xv, g = x_ref[0], dy_ref[0]
        dx = jnp.zeros_like(xv)
        dws = []
        for k in range(width):
            s = width - 1 - k
            dx = dx + w_ref[k:k + 1, :] * _shift_up(g, s)
            dws.append(jnp.sum(g * _shift_down(xv, s), axis=0, keepdims=True))
        dx_ref[0] = dx
        dw = jnp.concatenate(dws, axis=0)
        db = jnp.sum(g, axis=0, keepdims=True)

        @pl.when(pl.program_id(1) == 0)
        def _():
            dw_ref[...] = dw
            db_ref[...] = db

        @pl.when(pl.program_id(1) != 0)
        def _():
            dw_ref[...] += dw
            db_ref[...] += db

    return pl.pallas_call(
        body,
        name=name,
        grid=(c // LANES, bsz),
        in_specs=[
            pl.BlockSpec((1, t, LANES), lambda j, i: (i, 0, j)),
            pl.BlockSpec((width, LANES), lambda j, i: (0, j)),
            pl.BlockSpec((1, t, LANES), lambda j, i: (i, 0, j)),
        ],
        out_specs=[
            pl.BlockSpec((1, t, LANES), lambda j, i: (i, 0, j)),
            pl.BlockSpec((width, LANES), lambda j, i: (0, j)),
            pl.BlockSpec((1, LANES), lambda j, i: (0, j)),
        ],
        out_shape=[
            jax.ShapeDtypeStruct(x.shape, F32),
            jax.ShapeDtypeStruct(w.shape, F32),
            jax.ShapeDtypeStruct((1, c), F32),
        ],
        compiler_params=_params("parallel", "arbitrary"),
    )(x, w, dy)


def _make_conv(name):
    @jax.custom_vjp
    def op(x, w, b):
        return _conv_fwd(x, w, b, name + "_fwd")

    def fwd(x, w, b):
        return op(x, w, b), (x, w)

    def bwd(res, dy):
        x, w = res
        return tuple(_conv_bwd(x, w, dy, name + "_bwd"))

    op.defvjp(fwd, bwd)
    return op


_SCAN_ROWS = 8


def _scan_fwd(a, b, name):
    bsz, t, c = a.shape
    cw = _pick(c, (4 * LANES, 2 * LANES, LANES))

    def body(a_ref, b_ref, h_ref):
        row = lax.broadcasted_iota(jnp.int32, (_SCAN_ROWS, cw), 0)

        def step(i, carry):
            r0 = pl.multiple_of(i * _SCAN_ROWS, _SCAN_ROWS)
            av, bv = a_ref[0, pl.ds(r0, _SCAN_ROWS), :], b_ref[0, pl.ds(r0, _SCAN_ROWS), :]
            for s in (1, 2, 4):
                a_sh = jnp.where(row >= s, pltpu.roll(av, s, 0), 1.0)
                b_sh = jnp.where(row >= s, pltpu.roll(bv, s, 0), 0.0)
                bv = av * b_sh + bv
                av = av * a_sh
            hv = bv + av * carry
            h_ref[0, pl.ds(r0, _SCAN_ROWS), :] = hv
            return hv[_SCAN_ROWS - 1:, :]

        lax.fori_loop(0, t // _SCAN_ROWS, step, jnp.zeros((1, cw), F32), unroll=2)

    spec = pl.BlockSpec((1, t, cw), lambda i, j: (i, 0, j))
    return pl.pallas_call(
        body,
        name=name,
        grid=(bsz, c // cw),
        in_specs=[spec, spec],
        out_specs=spec,
        out_shape=jax.ShapeDtypeStruct(a.shape, F32),
        compiler_params=_params("parallel", "parallel"),
    )(a, b)


def _scan_bwd(a, h, g, name):
    bsz, t, c = a.shape
    cw = _pick(c, (2 * LANES, LANES))

    def body(a_ref, h_ref, g_ref, da_ref, db_ref):
        rows = _SCAN_ROWS
        row = lax.broadcasted_iota(jnp.int32, (rows, cw), 0)
        n_tiles = t // rows

        def step(n, carry):
            lam_next, a_next = carry
            i = n_tiles - 1 - n
            r0 = pl.multiple_of(i * rows, rows)
            rp = pl.multiple_of(jnp.maximum(i - 1, 0) * rows, rows)
            av, gv, hv = a_ref[0, pl.ds(r0, rows), :], g_ref[0, pl.ds(r0, rows), :], h_ref[0, pl.ds(r0, rows), :]
            h_before = jnp.where(i > 0, h_ref[0, pl.ds(rp, rows), :][rows - 1:, :], 0.0)
            cv = jnp.where(row < rows - 1, pltpu.roll(av, rows - 1, 0), a_next)
            for s in (1, 2, 4):
                c_sh = jnp.where(row < rows - s, pltpu.roll(cv, rows - s, 0), 1.0)
                g_sh = jnp.where(row < rows - s, pltpu.roll(gv, rows - s, 0), 0.0)
                gv = cv * g_sh + gv
                cv = cv * c_sh
            lam = gv + cv * lam_next
            db_ref[0, pl.ds(r0, rows), :] = lam
            da_ref[0, pl.ds(r0, rows), :] = lam * jnp.where(row >= 1, pltpu.roll(hv, 1, 0), h_before)
            return lam[:1, :], av[:1, :]

        zero = jnp.zeros((1, cw), F32)
        lax.fori_loop(0, n_tiles, step, (zero, zero), unroll=2)

    spec = pl.BlockSpec((1, t, cw), lambda i, j: (i, 0, j))
    return pl.pallas_call(
        body,
        name=name,
        grid=(bsz, c // cw),
        in_specs=[spec, spec, spec],
        out_specs=[spec, spec],
        out_shape=[jax.ShapeDtypeStruct(a.shape, F32)] * 2,
        compiler_params=_params("parallel", "parallel"),
    )(a, h, g)


def _make_scan(name):
    @jax.custom_vjp
    def op(a, b):
        return _scan_fwd(a, b, name + "_fwd")

    def fwd(a, b):
        h = op(a, b)
        return h, (a, h)

    def bwd(res, g):
        a, h = res
        da, db = _scan_bwd(a, h, g, name + "_bwd")
        return da, db

    op.defvjp(fwd, bwd)
    return op


def _query_blocks(t):
    blocks, start = [], 0
    while start < t:
        rows = 2 * SEQ_BLOCK if start + 2 * SEQ_BLOCK <= t else SEQ_BLOCK
        blocks.append((start, rows))
        start += rows
    return blocks


def _attn_probs(q, k, start, scale):
    tq, tk = q.shape[0], k.shape[0]
    s = _dot(q, k, "nt") * scale
    qpos = start + lax.broadcasted_iota(jnp.int32, (tq, tk), 0)
    kpos = lax.broadcasted_iota(jnp.int32, (tq, tk), 1)
    s = jnp.where(kpos <= qpos, s, NEG_INF)
    e = jnp.exp(s - jnp.max(s, axis=-1, keepdims=True))
    return e / jnp.sum(e, axis=-1, keepdims=True)


_MLA_SCALE = (MLA_NOPE + MLA_ROPE) ** -0.5


def _attn_specs(t):
    head = pl.BlockSpec((1, t, LANES), lambda b, h: (b, 0, h))
    shared = pl.BlockSpec((1, t, LANES), lambda b, h: (b, 0, 0))
    return head, shared


def _attn_fwd(q, kv, kpe, name):
    bsz, t, hl = q.shape
    head, shared = _attn_specs(t)

    def body(q_ref, kv_ref, kpe_ref, o_ref, k_s, v_s):
        lane = lax.broadcasted_iota(jnp.int32, (t, LANES), 1)
        kvh = kv_ref[0]
        k_s[...] = jnp.where(lane < MLA_NOPE, kvh, kpe_ref[0]).astype(BF16)
        v_s[...] = kvh.astype(BF16)
        for start, rows in _query_blocks(t):
            n = start + rows
            p = _attn_probs(q_ref[0, start:n, :], k_s[:n, :], start, _MLA_SCALE)
            o_ref[0, start:n, :] = _dot(p, v_s[:n, :], "nn")

    return pl.pallas_call(
        body,
        name=name,
        grid=(bsz, hl // LANES),
        in_specs=[head, head, shared],
        out_specs=head,
        out_shape=jax.ShapeDtypeStruct(q.shape, F32),
        scratch_shapes=[pltpu.VMEM((t, LANES), BF16), pltpu.VMEM((t, LANES), BF16)],
        compiler_params=_params("parallel", "parallel"),
    )(q, kv, kpe)


def _attn_bwd(q, kv, kpe, do, name):
    bsz, t, hl = q.shape
    head, shared = _attn_specs(t)

    def body(q_ref, kv_ref, kpe_ref, do_ref, dq_ref, dkv_ref, dkpe_ref, k_s, v_s, dk_s, dv_s):
        lane = lax.broadcasted_iota(jnp.int32, (t, LANES), 1)
        kvh = kv_ref[0]
        k_s[...] = jnp.where(lane < MLA_NOPE, kvh, kpe_ref[0]).astype(BF16)
        v_s[...] = kvh.astype(BF16)
        for start, rows in reversed(_query_blocks(t)):
            n = start + rows
            qb = q_ref[0, start:n, :]
            dob = jnp.where(lane[:rows] >= MLA_NOPE, do_ref[0, start:n, :], 0.0)
            kk, vv = k_s[:n, :], v_s[:n, :]
            p = _attn_probs(qb, kk, start, _MLA_SCALE)
            dp = _dot(dob, vv, "nt")
            ds = p * (dp - jnp.sum(dp * p, axis=-1, keepdims=True)) * _MLA_SCALE
            dq_ref[0, start:n, :] = _dot(ds, kk, "nn")
            if n == t:
                dk_s[...] = _dot(ds, qb, "tn")
                dv_s[...] = _dot(p, dob, "tn")
            else:
                dk_s[:n, :] += _dot(ds, qb, "tn")
                dv_s[:n, :] += _dot(p, dob, "tn")
        dk = dk_s[...]
        dkv_ref[0] = jnp.where(lane < MLA_NOPE, dk, dv_s[...])
        dkpe = jnp.where(lane >= MLA_NOPE, dk, 0.0)

        @pl.when(pl.program_id(1) == 0)
        def _():
            dkpe_ref[0] = dkpe

        @pl.when(pl.program_id(1) != 0)
        def _():
            dkpe_ref[0] += dkpe

    return pl.pallas_call(
        body,
        name=name,
        grid=(bsz, hl // LANES),
        in_specs=[head, head, shared, head],
        out_specs=[head, head, shared],
        out_shape=[
            jax.ShapeDtypeStruct(q.shape, F32),
            jax.ShapeDtypeStruct(kv.shape, F32),
            jax.ShapeDtypeStruct(kpe.shape, F32),
        ],
        scratch_shapes=[pltpu.VMEM((t, LANES), BF16), pltpu.VMEM((t, LANES), BF16),
                        pltpu.VMEM((t, LANES), F32), pltpu.VMEM((t, LANES), F32)],
        compiler_params=_params("parallel", "arbitrary"),
    )(q, kv, kpe, do)


def _make_attention(name):
    @jax.custom_vjp
    def op(q, kv, kpe):
        return _attn_fwd(q, kv, kpe, name + "_fwd")

    def fwd(q, kv, kpe):
        return op(q, kv, kpe), (q, kv, kpe)

    def bwd(res, do):
        return tuple(_attn_bwd(*res, do, name + "_bwd"))

    op.defvjp(fwd, bwd)
    return op


_ROPE_SHIFT = MLA_ROPE // 2


def _rope_lanes_call(x, c, sm, sp, transpose, name):
    r, width = x.shape
    tm = _row_tile(r, width)

    def body(x_ref, c_ref, sm_ref, sp_ref, y_ref):
        cv, smv, spv = c_ref[...], sm_ref[...], sp_ref[...]
        for b in range(width // LANES):
            xb = x_ref[:, b * LANES:(b + 1) * LANES]
            if transpose:
                yb = xb * cv + pltpu.roll(xb * smv, _ROPE_SHIFT, 1) + pltpu.roll(xb * spv, LANES - _ROPE_SHIFT, 1)
            else:
                yb = xb * cv + pltpu.roll(xb, LANES - _ROPE_SHIFT, 1) * smv + pltpu.roll(xb, _ROPE_SHIFT, 1) * spv
            y_ref[:, b * LANES:(b + 1) * LANES] = yb

    tab = pl.BlockSpec((tm, LANES), lambda i: (i, 0))
    blk = pl.BlockSpec((tm, width), lambda i: (i, 0))
    return pl.pallas_call(
        body,
        name=name,
        grid=(r // tm,),
        in_specs=[blk, tab, tab, tab],
        out_specs=blk,
        out_shape=jax.ShapeDtypeStruct(x.shape, F32),
        compiler_params=_params("parallel"),
    )(x, c, sm, sp)


def _make_rope_lanes(name):
    @jax.custom_vjp
    def op(x, c, sm, sp):
        return _rope_lanes_call(x, c, sm, sp, False, name + "_fwd")

    def fwd(x, c, sm, sp):
        return op(x, c, sm, sp), (c, sm, sp)

    def bwd(res, dy):
        c, sm, sp = res
        return _rope_lanes_call(dy, c, sm, sp, True, name + "_bwd"), jnp.zeros_like(c), jnp.zeros_like(sm), jnp.zeros_like(sp)

    op.defvjp(fwd, bwd)
    return op


def _ret_chunk_rows(t):
    return t // 4 if t % 32 == 0 else SEQ_BLOCK


def _ret_decays(c, log_gamma):
    row = lax.broadcasted_iota(jnp.int32, (c, 1), 0)
    col = lax.broadcasted_iota(jnp.int32, (1, c), 1)
    rowf = row.astype(F32)
    d = jnp.where(row >= col, jnp.exp(log_gamma * rowf) * jnp.exp(-log_gamma * col.astype(F32)), 0.0)
    return d, jnp.exp(log_gamma * (rowf + 1.0)), jnp.exp(log_gamma * (c - 1.0 - rowf)), jnp.exp(log_gamma * c)


def _ret_specs(c, dk, dv, v_block0, n_chunks, reverse):
    pos = (lambda i: n_chunks - 1 - i) if reverse else (lambda i: i)
    return (
        pl.BlockSpec(memory_space=pltpu.SMEM),
        pl.BlockSpec((1, c, dk), lambda b, h, i: (b, pos(i), h)),
        pl.BlockSpec((1, c, dv), lambda b, h, i: (b, pos(i), h + v_block0)),
        pl.BlockSpec((1, c, dv), lambda b, h, i: (b, pos(i), h)),
        pl.BlockSpec((1, 1, dk, dv), lambda b, h, i: (b, h * n_chunks + pos(i), 0, 0)),
    )


def _ret_fwd(lg, q, k, v, name, dv=None, v_block0=0):
    bsz, t, hdk = q.shape
    heads = lg.shape[0]
    dk, dv = hdk // heads, dv or v.shape[2] // heads
    c = _ret_chunk_rows(t)
    n_chunks = t // c
    lg_spec, qk_spec, v_spec, o_spec, s_spec = _ret_specs(c, dk, dv, v_block0, n_chunks, False)

    def body(lg_ref, q_ref, k_ref, v_ref, o_ref, s_ref, state):
        @pl.when(pl.program_id(2) == 0)
        def _():
            state[...] = jnp.zeros((dk, dv), F32)

        d, a, b, g = _ret_decays(c, lg_ref[pl.program_id(1)])
        qb, kb, vb, s_in = q_ref[0], k_ref[0], v_ref[0], state[...]
        s_ref[0, 0] = s_in
        o_ref[0] = _dot(_dot(qb, kb, "nt") * d, vb, "nn") + a * _dot(qb, s_in, "nn")
        state[...] = g * s_in + _dot(kb * b, vb, "tn")

    return pl.pallas_call(
        body,
        name=name,
        grid=(bsz, heads, n_chunks),
        in_specs=[lg_spec, qk_spec, qk_spec, v_spec],
        out_specs=[o_spec, s_spec],
        out_shape=[jax.ShapeDtypeStruct((bsz, t, heads * dv), F32),
                   jax.ShapeDtypeStruct((bsz, heads * n_chunks, dk, dv), F32)],
        scratch_shapes=[pltpu.VMEM((dk, dv), F32)],
        compiler_params=_params("parallel", "parallel", "arbitrary"),
    )(lg, q, k, v)


def _ret_bwd(lg, q, k, v, states, do, name, v_block0=0, dv_dtype=F32):
    bsz, t, hdk = q.shape
    heads = lg.shape[0]
    dk, dv = hdk // heads, do.shape[2] // heads
    c = _ret_chunk_rows(t)
    n_chunks = t // c
    lg_spec, qk_spec, v_spec, o_spec, s_spec = _ret_specs(c, dk, dv, v_block0, n_chunks, True)

    def body(lg_ref, q_ref, k_ref, v_ref, s_ref, do_ref, dq_ref, dk_ref, dv_ref, dstate):
        @pl.when(pl.program_id(2) == 0)
        def _():
            dstate[...] = jnp.zeros((dk, dv), F32)

        d, a, b, g = _ret_decays(c, lg_ref[pl.program_id(1)])
        qb, kb, vb, dob, s_in, ds_out = q_ref[0], k_ref[0], v_ref[0], do_ref[0], s_ref[0, 0], dstate[...]
        scores = _dot(qb, kb, "nt") * d
        dscores = _dot(dob, vb, "nt") * d
        dq_ref[0] = _dot(dscores, kb, "nn") + a * _dot(dob, s_in, "nt")
        dk_ref[0] = _dot(dscores, qb, "tn") + b * _dot(vb, ds_out, "nt")
        dv_ref[0] = (_dot(scores, dob, "tn") + _dot(kb * b, ds_out, "nn")).astype(dv_dtype)
        dstate[...] = g * ds_out + _dot(qb, a * dob, "tn")

    return pl.pallas_call(
        body,
        name=name,
        grid=(bsz, heads, n_chunks),
        in_specs=[lg_spec, qk_spec, qk_spec, v_spec, s_spec, o_spec],
        out_specs=[qk_spec, qk_spec, o_spec],
        out_shape=[
            jax.ShapeDtypeStruct(q.shape, F32),
            jax.ShapeDtypeStruct(k.shape, F32),
            jax.ShapeDtypeStruct(do.shape, dv_dtype),
        ],
        scratch_shapes=[pltpu.VMEM((dk, dv), F32)],
        compiler_params=_params("parallel", "parallel", "arbitrary"),
    )(lg, q, k, v, states, do)


def _adamw(w, g, m, v, name):
    r, c = w.shape
    tr = _pick(r, (256, 128, 64, 32, 16, 8))

    def body(w_ref, g_ref, m_ref, v_ref, d_ref, nm_ref, nv_ref):
        gv = g_ref[...]
        nm = ADAM_B1 * m_ref[...] + (1.0 - ADAM_B1) * gv
        nv = ADAM_B2 * v_ref[...] + (1.0 - ADAM_B2) * (gv * gv)
        m_hat = nm / (1.0 - ADAM_B1 ** ADAM_STEP)
        v_hat = nv / (1.0 - ADAM_B2 ** ADAM_STEP)
        d_ref[...] = -ADAM_LR * (m_hat / (jnp.sqrt(v_hat) + ADAM_EPS) + ADAM_WD * w_ref[...])
        nm_ref[...] = nm
        nv_ref[...] = nv

    spec = pl.BlockSpec((tr, c), lambda i: (i, 0))
    return pl.pallas_call(
        body,
        name=name,
        grid=(r // tr,),
        in_specs=[spec] * 4,
        out_specs=[spec] * 3,
        out_shape=[jax.ShapeDtypeStruct((r, c), F32)] * 3,
        compiler_params=_params("parallel"),
    )(w, g, m, v)


def _rope_tables(t, half, reps):
    inv = ROPE_BASE ** (-jnp.arange(half, dtype=F32) / half)
    ang = jnp.arange(t, dtype=jnp.int32).astype(F32)[:, None] * inv[None, :]
    return jnp.tile(jnp.cos(ang), (1, reps)), jnp.tile(jnp.sin(ang), (1, reps))


def _padded_len(seq):
    return -(-(N_META + seq) // SEQ_BLOCK) * SEQ_BLOCK


def _embed(meta, x):
    bsz, seq, d = x.shape
    t = _padded_len(seq)
    return jnp.concatenate(
        [jnp.broadcast_to(meta[None], (bsz, N_META, d)), x, jnp.zeros((bsz, t - N_META - seq, d), F32)], axis=1
    ).reshape(bsz * t, d)


def _even_mixer(p, conv_w_shard, conv_b, w_rg_a, b_rg_a, w_rg_x, b_rg_x, lru_lambda, q_norm_g, w_uq_shard,
                kv_norm_g, w_ukv_shard, gathered, bsz):
    conv_w_full, w_uq_full, w_ukv_full = gathered
    r = p.shape[0]
    t = r // bsz

    def tile_rows(tab):
        return jnp.tile(tab, (bsz, 1))

    lru_w = w_rg_a.shape[2] * w_rg_a.shape[1]
    q_rank, kv_rank = q_norm_g.shape[1], kv_norm_g.shape[1]
    p_gate, p_rec, p_q, p_kv, p_kpe = _split_cols(
        p, (lru_w, 2 * lru_w, 2 * lru_w + q_rank, 2 * lru_w + q_rank + kv_rank))

    conv_w = _make_fsdp_param("conv_w")(conv_w_shard[0], conv_w_full)
    xc = _make_conv("conv")(p_rec.reshape(bsz, t, lru_w), conv_w, conv_b).reshape(r, lru_w)
    ga = _make_group_linear("rg_a")(xc, w_rg_a[0])
    gx = _make_group_linear("rg_x")(xc, w_rg_x[0])
    (a, bb), _ = _make_rowwise(_lru_gates_f, "lru_gates", 3, 0, 3)((ga, gx, xc), (), (b_rg_a, b_rg_x, lru_lambda))
    hh = _make_scan("lru_scan")(a.reshape(bsz, t, lru_w), bb.reshape(bsz, t, lru_w)).reshape(r, lru_w)
    (y_rec,), _ = _make_rowwise(_lru_out_f, "lru_out", 2, 0, 0)((hh, p_gate), (), ())

    (qn,), _ = _make_rowwise(_rmsnorm_f, "q_norm", 1, 0, 1)((p_q,), (), (q_norm_g,))
    (kvn,), _ = _make_rowwise(_rmsnorm_f, "kv_norm", 1, 0, 1)((p_kv,), (), (kv_norm_g,))
    d_head = MLA_NOPE + MLA_ROPE
    w_uq_pad = jnp.pad(w_uq_full.reshape(q_rank, MLA_HEADS, d_head), ((0, 0), (0, 0), (0, LANES - d_head)))

    def unpad_uq(dw):
        return dw.reshape(q_rank, MLA_HEADS, LANES)[:, :, :d_head].reshape(q_rank, MLA_HEADS * d_head)

    q = _make_fsdp_linear(True, "ev_uq", unpad_uq)(qn, w_uq_shard[0], w_uq_pad.reshape(q_rank, MLA_HEADS * LANES))
    kv = _make_fsdp_linear(True, "ev_ukv")(kvn, w_ukv_shard[0], w_ukv_full)
    half = MLA_ROPE // 2
    cos, sin = _rope_tables(t, half, 1)
    one, zero = jnp.ones((t, MLA_NOPE), F32), jnp.zeros((t, MLA_NOPE), F32)
    tail = LANES - MLA_NOPE - MLA_ROPE
    c_tab = tile_rows(jnp.concatenate([one, cos, cos, one[:, :tail]], axis=1))
    sm_tab = tile_rows(jnp.concatenate([zero, -sin, zero[:, :half + tail]], axis=1))
    sp_tab = tile_rows(jnp.concatenate([zero, zero[:, :half], sin, zero[:, :tail]], axis=1))
    q = _make_rope_lanes("rope_q")(q, c_tab, sm_tab, sp_tab)
    kpe = _make_rope_lanes("rope_k")(p_kpe, c_tab, sm_tab, sp_tab)
    o = _make_attention("mla")(q.reshape(bsz, t, -1), kv.reshape(bsz, t, -1), kpe.reshape(bsz, t, LANES))
    return jnp.concatenate([y_rec, o.reshape(r, -1)], axis=1)


def _odd_mixer_fwd(p, bsz):
    r, width = p.shape
    t = r // bsz
    qk = width // 6
    dk = qk // RET_HEADS
    cos2, sin2 = _rope_tables(t, dk // 2, 2)
    sin2 = jnp.concatenate([-sin2[:, :dk // 2], sin2[:, dk // 2:]], axis=1)
    rope_args = (_Cols(p, qk, 0), _Cols(p, qk, 1), jnp.tile(cos2, (bsz, 1)), jnp.tile(sin2, (bsz, 1)))
    (rq, rk), _ = _make_rowwise(_rope_ret_f, "rope_ret", 2, 2, 0).fwd_call(*rope_args)
    lg = jnp.log(1.0 - 2.0 ** (-5.0 - jnp.arange(RET_HEADS, dtype=F32)))
    ret_args = (lg, rq.reshape(bsz, t, qk), rk.reshape(bsz, t, qk), p.reshape(bsz, t, width))
    o, states = _ret_fwd(*ret_args, "ret_fwd", dv=2 * dk, v_block0=qk // dk)
    gate_args = (o.reshape(r, 2 * qk), _Cols(p, 2 * qk, 2))
    (y,), _ = _make_rowwise(_ret_out_f, "ret_out", 2, 0, 0).fwd_call(*gate_args)
    return y, (rope_args, ret_args + (states,), gate_args)


def _odd_mixer_bwd(res, dy):
    rope_args, ret_args, gate_args = res
    bsz, t, qk = ret_args[1].shape
    dk = qk // RET_HEADS
    (do, dg), _ = _make_rowwise(_ret_out_f, "ret_out", 2, 0, 0).bwd_call(
        gate_args, ((dy,), ()), row_dtypes=(F32, BF16))
    drq, drk, dv = _ret_bwd(*ret_args, do.reshape(bsz, t, 2 * qk), "ret_bwd", v_block0=qk // dk, dv_dtype=BF16)
    (dq, dkk), _ = _make_rowwise(_rope_ret_f, "rope_ret", 2, 2, 0).bwd_call(
        rope_args, ((drq.reshape(bsz * t, qk), drk.reshape(bsz * t, qk)), ()), row_dtypes=(BF16, BF16))
    return jnp.concatenate([dq, dkk, dv.reshape(bsz * t, 2 * qk), dg], axis=1)


def _local_loss(h, target):
    bsz, seq, d = target.shape
    t = _padded_len(seq)
    t_real = N_META + seq
    pos = jnp.arange(t, dtype=jnp.int32)
    mask = jnp.tile(((pos >= N_META) & (pos < t_real)).astype(F32)[:, None], (bsz, 1))
    tgt = jnp.concatenate(
        [jnp.zeros((bsz, N_META, d), F32), target, jnp.zeros((bsz, t - t_real, d), F32)], axis=1).reshape(bsz * t, d)
    _, (total,) = _make_rowwise(_loss_f, "loss", 1, 2, 0)((h,), (tgt, mask), ())
    return total[0, 0]


_WEIGHTS = ("meta_tokens", "ev_w_in", "ev_conv_w", "ev_conv_b", "ev_w_rg_a", "ev_b_rg_a", "ev_w_rg_x", "ev_b_rg_x",
            "ev_lru_lambda", "ev_q_norm_g", "ev_w_uq", "ev_kv_norm_g", "ev_w_ukv", "ev_w_out", "od_w_in", "od_w_out",
            "ln_mix_g", "ln_mix_b", "mlp_w1", "mlp_w2", "ln_mlp_g", "ln_mlp_b")


def kernel(x, meta_tokens, ev_w_in, ev_conv_w, ev_conv_b, ev_w_rg_a, ev_b_rg_a, ev_w_rg_x, ev_b_rg_x, ev_lru_lambda, ev_q_norm_g, ev_w_uq, ev_kv_norm_g, ev_w_ukv, ev_w_out, od_w_in, od_w_out, ln_mix_g, ln_mix_b, mlp_w1, mlp_w2, ln_mlp_g, ln_mlp_b, loss_target, m_meta_tokens, m_ev_w_in, m_ev_conv_w, m_ev_conv_b, m_ev_w_rg_a, m_ev_b_rg_a, m_ev_w_rg_x, m_ev_b_rg_x, m_ev_lru_lambda, m_ev_q_norm_g, m_ev_w_uq, m_ev_kv_norm_g, m_ev_w_ukv, m_ev_w_out, m_od_w_in, m_od_w_out, m_ln_mix_g, m_ln_mix_b, m_mlp_w1, m_mlp_w2, m_ln_mlp_g, m_ln_mlp_b, v_meta_tokens, v_ev_w_in, v_ev_conv_w, v_ev_conv_b, v_ev_w_rg_a, v_ev_b_rg_a, v_ev_w_rg_x, v_ev_b_rg_x, v_ev_lru_lambda, v_ev_q_norm_g, v_ev_w_uq, v_ev_kv_norm_g, v_ev_w_ukv, v_ev_w_out, v_od_w_in, v_od_w_out, v_ln_mix_g, v_ln_mix_b, v_mlp_w1, v_mlp_w2, v_ln_mlp_g, v_ln_mlp_b):
    args = locals()
    weights = {n: args[n] for n in _WEIGHTS}
    bsz = x.shape[0]
    my_x, my_y, my_c = _my_place()
    me = 4 * my_x + 2 * my_y + my_c

    big = (("ev_in", ev_w_in[0], True), ("ev_out", ev_w_out[0], False), ("mlp0_w1", mlp_w1[0], True),
           ("mlp0_w2", mlp_w2[0], False), ("od_in", od_w_in[0], True), ("od_out", od_w_out[0], False),
           ("mlp1_w1", mlp_w1[1], True), ("mlp1_w2", mlp_w2[1], False))
    small_sharded = (("meta", meta_tokens, F32), ("conv_w", ev_conv_w[0], F32), ("ev_uq", ev_w_uq[0], BF16),
                     ("ev_ukv", ev_w_ukv[0], BF16))
    to_gather = (tuple((nm, s.astype(dt), True) for nm, s, dt in small_sharded)
                 + tuple((nm, s.astype(BF16), cols) for nm, s, cols in big))
    handles = _gather_start_all([s for _, s, _ in to_gather], "ag_start")
    gathers = {nm: (s, cols, h) for (nm, s, cols), h in zip(to_gather, handles)}
    gather_tokens = (handles[0][4],)

    def full_weight(nm, after):
        shard16, cols, handle = gathers[nm]
        land = _exchange_wait(handle, True, after, "ag_wait_" + nm)
        land = lax.dynamic_update_index_in_dim(land, shard16, me, 0)
        if cols and shard16.shape[1] % LANES == 0:
            return land, True
        return (_unstack_cols(land) if cols else land.reshape(-1, shard16.shape[1])), False

    meta_full, conv_w_full, w_uq_full, w_ukv_full = (
        _unstack_cols(lax.dynamic_update_index_in_dim(
            _exchange_wait(gathers[nm][2], True, gather_tokens[-1], "ag_wait_" + nm), gathers[nm][0], me, 0))
        for nm, _, _ in small_sharded)

    pending = []

    def linear_bwd(nm, x_in, w, dy, cols, unpad=None, **fused):
        w_full, w_stacked = w
        a_relu2 = fused.pop("a_relu2", False)
        if w_stacked:
            stacked = _matmul(x_in, dy, "tn", nm + "_dw", stacked=True, a_relu2=a_relu2)
            own = lax.dynamic_index_in_dim(stacked, me, 0, keepdims=False)
        else:
            dw = _matmul(x_in, dy, "tn", nm + "_dw", a_relu2=a_relu2)
            dw = dw if unpad is None else unpad(dw)
            n = dw.shape[1] // N_DEV
            if cols:
                stacked = _stack_cols(dw)
                own = lax.dynamic_slice_in_dim(dw, me * n, n, axis=1)
            else:
                stacked = dw.reshape(N_DEV, dw.shape[0] // N_DEV, dw.shape[1])
                own = lax.dynamic_index_in_dim(stacked, me, 0, keepdims=False)
        handle = _exchange_start(stacked, (_N_PEERS,) + stacked.shape[1:], False, "rs_start_" + nm)
        pending.append((nm, own, handle))
        return _matmul(dy, w_full, "nt", nm + "_dx", after=(handle[4],), stacked=w_stacked, **fused)

    def linear_fwd(nm, x_in, w, **fused):
        return _matmul(x_in, w[0], "nn", nm + "_fwd", stacked=w[1], **fused)

    def mlp_fwd(h, h16, l):
        w1 = full_weight(f"mlp{l}_w1", h16)
        u, a16 = linear_fwd(f"mlp{l}_w1", h16, w1, relu2_copy=True)
        w2 = full_weight(f"mlp{l}_w2", a16)
        f = linear_fwd(f"mlp{l}_w2", a16, w2)
        ln_args = (h, f, ln_mlp_g[l:l + 1], ln_mlp_b[l:l + 1])
        return ln_fwd(f"mlp{l}_ln", *ln_args), (h16, w1, u, a16, w2, ln_args)

    def mlp_bwd(dout, res, l):
        h16, w1, u, a16, w2, ln_args = res
        dh, df, dg, db = ln_bwd(f"mlp{l}_ln", ln_args, dout)
        du = linear_bwd(f"mlp{l}_w2", a16, w2, df, False, relu2_bwd_of=u, out_dtype=BF16)
        return (dh, linear_bwd(f"mlp{l}_w1", h16, w1, du, True)), dg, db

    def ln_fwd(nm, h, mix, g, b):
        return _make_rowwise(_ln_res_copy_f, nm, 2, 0, 2).fwd_call(h, mix, g, b)[0]

    def ln_bwd(nm, ln_args, pieces):
        (dh, dmix), (dg, db) = _make_rowwise(_ln_res_f, nm, 2, 0, 2).bwd_call(
            ln_args, ((pieces[0],), ()), more=tuple(pieces[1:]), row_dtypes=(F32, BF16))
        return dh, dmix, dg, db

    h0, vjp_embed = jax.vjp(_embed, meta_full, x)
    n_in = ev_w_in.shape[2] * N_DEV
    kpe0, pad_lo, pad_hi = n_in - MLA_ROPE, MLA_NOPE, LANES - MLA_NOPE - MLA_ROPE
    w_in = full_weight("ev_in", h0)[0]
    zeros_in = jnp.zeros((w_in.shape[0], pad_lo), BF16)
    w_ev_in = (jnp.concatenate([w_in[:, :kpe0], zeros_in, w_in[:, kpe0:], zeros_in[:, :pad_hi]], axis=1), False)

    def unpad_in(dw):
        return jnp.concatenate([dw[:, :kpe0], dw[:, kpe0 + pad_lo:kpe0 + pad_lo + MLA_ROPE]], axis=1)

    p0 = _matmul(h0, w_ev_in[0], "nn", "ev_in_fwd", after=gather_tokens)
    small = (ev_conv_w, ev_conv_b, ev_w_rg_a, ev_b_rg_a, ev_w_rg_x, ev_b_rg_x, ev_lru_lambda, ev_q_norm_g, ev_w_uq,
             ev_kv_norm_g, ev_w_ukv)
    y0, vjp_even = jax.vjp(lambda p, *s: _even_mixer(p, *s, (conv_w_full, w_uq_full, w_ukv_full), bsz), p0, *small)
    w_out = full_weight("ev_out", y0)[0]
    lru_w, d_model = y0.shape[1] - MLA_HEADS * LANES, w_out.shape[1]
    w_att = w_out[lru_w:].reshape(MLA_HEADS, MLA_V, d_model)
    w_att = jnp.concatenate([jnp.zeros((MLA_HEADS, LANES - MLA_V, d_model), BF16), w_att], axis=1)
    w_ev_out = (jnp.concatenate([w_out[:lru_w], w_att.reshape(MLA_HEADS * LANES, d_model)], axis=0), False)

    def unpad_out(dw):
        d_att = dw[lru_w:].reshape(MLA_HEADS, LANES, d_model)[:, LANES - MLA_V:].reshape(MLA_HEADS * MLA_V, d_model)
        return jnp.concatenate([dw[:lru_w], d_att], axis=0)

    mix0 = linear_fwd("ev_out", y0, w_ev_out)
    ln0_args = (h0, mix0, ln_mix_g[0:1], ln_mix_b[0:1])
    h1, h1_16 = ln_fwd("mix0_ln", *ln0_args)
    (h2, h2_16), res_mlp0 = mlp_fwd(h1, h1_16, 0)
    w_od_in = full_weight("od_in", h2_16)
    p1 = linear_fwd("od_in", h2_16, w_od_in)
    y1, res_odd = _odd_mixer_fwd(p1, bsz)
    w_od_out = full_weight("od_out", y1)
    mix1 = linear_fwd("od_out", y1, w_od_out)
    ln1_args = (h2, mix1, ln_mix_g[1:2], ln_mix_b[1:2])
    h3, h3_16 = ln_fwd("mix1_ln", *ln1_args)
    (h4, _), res_mlp1 = mlp_fwd(h3, h3_16, 1)
    loss_local, vjp_loss = jax.vjp(lambda h: _local_loss(h, loss_target), h4)

    dh4 = vjp_loss(jnp.ones((), F32))
    dh3, dg_mlp1, db_mlp1 = mlp_bwd(dh4, res_mlp1, 1)
    dh2, dmix1, dg_mix1, db_mix1 = ln_bwd("mix1_ln", ln1_args, dh3)
    dp1 = _odd_mixer_bwd(res_odd, linear_bwd("od_out", y1, w_od_out, dmix1, False))
    dh2 = (dh2, linear_bwd("od_in", h2_16, w_od_in, dp1, True))
    dh1, dg_mlp0, db_mlp0 = mlp_bwd(dh2, res_mlp0, 0)
    dh0, dmix0, dg_mix0, db_mix0 = ln_bwd("mix0_ln", ln0_args, dh1)
    dp0, *dsmall = vjp_even(linear_bwd("ev_out", y0, w_ev_out, dmix0, False, unpad=unpad_out))
    dh0 = dh0 + linear_bwd("ev_in", h0, w_ev_in, dp0.astype(BF16), True, unpad=unpad_in)
    g_meta_full, grad_x = vjp_embed(dh0)
    (g_conv_w, g_conv_b, g_w_rg_a, g_b_rg_a, g_w_rg_x, g_b_rg_x, g_lambda, g_q_norm, g_w_uq, g_kv_norm, g_w_ukv) = dsmall

    after, summed = grad_x, {}
    for nm, own, handle in pending:
        land = _exchange_wait(handle, False, after, "rs_wait_" + nm)
        summed[nm] = after = _sum_own_and_peers(own, land, "rs_sum_" + nm)

    g_meta = _scatter_grad(g_meta_full, True, "meta_rs", after=(after,))
    rep_names = ("ev_conv_b", "ev_w_rg_a", "ev_b_rg_a", "ev_w_rg_x", "ev_b_rg_x", "ev_lru_lambda", "ev_q_norm_g",
                 "ev_kv_norm_g", "ln_mix_g", "ln_mix_b", "ln_mlp_g", "ln_mlp_b")
    rep_local = (g_conv_b, g_w_rg_a, g_b_rg_a, g_w_rg_x, g_b_rg_x, g_lambda, g_q_norm, g_kv_norm,
                 jnp.concatenate([dg_mix0, dg_mix1]), jnp.concatenate([db_mix0, db_mix1]),
                 jnp.concatenate([dg_mlp0, dg_mlp1]), jnp.concatenate([db_mlp0, db_mlp1]))
    grad_w = dict(zip(rep_names, _allreduce_replicated(rep_local, "rep", after=(g_meta,))))
    grad_w.update(meta_tokens=g_meta, ev_conv_w=g_conv_w, ev_w_uq=g_w_uq, ev_w_ukv=g_w_ukv)
    grad_w.update(ev_w_in=summed["ev_in"][None], ev_w_out=summed["ev_out"][None], od_w_in=summed["od_in"][None],
                  od_w_out=summed["od_out"][None], mlp_w1=jnp.stack([summed["mlp0_w1"], summed["mlp1_w1"]]),
                  mlp_w2=jnp.stack([summed["mlp0_w2"], summed["mlp1_w2"]]))

    loss = lax.psum(loss_local, MESH_AXES)
    delta, new_m, new_v = {}, {}, {}
    for n in _WEIGHTS:
        w, g, m, v = weights[n], grad_w[n], args["m_" + n], args["v_" + n]
        two_d = (-1, w.shape[-1])
        d2, m2, v2 = _adamw(w.reshape(two_d), g.reshape(two_d), m.reshape(two_d), v.reshape(two_d), "adamw_" + n)
        delta[n], new_m[n], new_v[n] = d2.reshape(w.shape), m2.reshape(w.shape), v2.reshape(w.shape)
    return (loss, grad_x, *[grad_w[n] for n in _WEIGHTS], *[delta[n] for n in _WEIGHTS],
            *[new_m[n] for n in _WEIGHTS], *[new_v[n] for n in _WEIGHTS])
```

```python
import functools
import math

import jax
import jax.numpy as jnp
from jax import lax
from jax.experimental import pallas as pl
from jax.experimental.pallas import tpu as pltpu

F32 = jnp.float32
BF16 = jnp.bfloat16

N_DEV = 8
MESH_AXES = ("x", "y", "c")
LANES = 128
SEQ_BLOCK = 128

N_META = 16
LRU_C = 8.0
MLA_HEADS = 8
MLA_NOPE = 64
MLA_ROPE = 32
MLA_V = 64
RET_HEADS = 4
ROPE_BASE = 10000.0
DEPTH = 2
DN_ALPHA = (2 * DEPTH) ** 0.25
EPS = 1e-5
NEG_INF = -1e30

ADAM_LR = 0.001
ADAM_B1 = 0.9
ADAM_B2 = 0.999
ADAM_EPS = 1e-08
ADAM_WD = 0.01
ADAM_STEP = 10

VMEM_LIMIT = 56 * 1024 * 1024


def _params(*sem):
    return pltpu.CompilerParams(dimension_semantics=sem, vmem_limit_bytes=VMEM_LIMIT)


def _pick(n, cands):
    for c in cands:
        if n % c == 0:
            return c
    return n


def _row_tile(r, width):
    cands = (256, 128, 64, 32, 16, 8) if width <= 1024 else (128, 64, 32, 16, 8)
    return _pick(r, cands)


_DIMS = {"nn": (((1,), (0,)), ((), ())), "nt": (((1,), (1,)), ((), ())), "tn": (((0,), (0,)), ((), ()))}


def _dot(a, b, mode):
    return lax.dot_general(a.astype(BF16), b.astype(BF16), _DIMS[mode], preferred_element_type=F32)


def _matmul(a, b, mode, name, after=(), stacked=False, relu2_bwd_of=None, out_dtype=F32, out_relu2=False):
    if stacked:
        n_blk = b.shape[2] if mode != "tn" else b.shape[1] // N_DEV
    if mode == "nn":
        (m, k), n = a.shape, (N_DEV * n_blk if stacked else b.shape[1])
    elif mode == "nt":
        (m, k), n = a.shape, (b.shape[1] if stacked else b.shape[0])
    else:
        (k, m), n = a.shape, b.shape[1]
    tm = _pick(m, (1088, 1024, 544, 512, 272, 256, 128, 64, 32, 16, 8))
    tn = _pick(n, (1024, 512, 256, 128))
    tk = _pick(k, (1088, 1024, 544, 512, 272, 256, 128))
    kb = 2
    if stacked and mode == "nn":
        tn = n_blk
    if stacked and mode == "tn":
        tn = kb * n_blk
    if stacked and mode == "nt":
        tk = kb * n_blk
    nk = k // tk
    assert out_dtype == F32 or (nk == 1 and not (stacked and mode == "tn")), "narrow results need a single k step"
    assert not out_relu2 or nk == 1, "relu^2 is applied to a finished tile"

    out_spec = pl.BlockSpec((tm, tn), lambda i, j, kk: (i, j))
    out_shape = jax.ShapeDtypeStruct((m, n), out_dtype)
    if mode == "nn":
        a_spec = pl.BlockSpec((tm, tk), lambda i, j, kk: (i, kk))
        b_spec = pl.BlockSpec((tk, tn), lambda i, j, kk: (kk, j))
        if stacked:
            b_spec = pl.BlockSpec((None, tk, tn), lambda i, j, kk: (j, kk, 0))
    elif mode == "nt":
        a_spec = pl.BlockSpec((tm, tk), lambda i, j, kk: (i, kk))
        b_spec = pl.BlockSpec((tn, tk), lambda i, j, kk: (j, kk))
        if stacked:
            b_spec = pl.BlockSpec((kb, tn, n_blk), lambda i, j, kk: (kk, j, 0))
    else:
        a_spec = pl.BlockSpec((tk, tm), lambda i, j, kk: (kk, i))
        b_spec = pl.BlockSpec((tk, tn), lambda i, j, kk: (kk, j))
        if stacked:
            out_spec = pl.BlockSpec((kb, tm, n_blk), lambda i, j, kk: (j, i, 0))
            out_shape = jax.ShapeDtypeStruct((N_DEV, m, n_blk), F32)
    extra = [] if relu2_bwd_of is None else [relu2_bwd_of]
    extra_specs = [pl.BlockSpec((tm, tn), lambda i, j, kk: (i, j))] * len(extra)

    def body(a_ref, b_ref, *rest):
        def relu2_slope():
            return 2.0 * jnp.sqrt(rest[0][...].astype(F32))

        o_ref = rest[-1]
        kk = pl.program_id(2)
        av = a_ref[...]
        if stacked and mode == "nt":
            part = _dot(av[:, :n_blk], b_ref[0], mode)
            for q in range(1, kb):
                part = part + _dot(av[:, q * n_blk:(q + 1) * n_blk], b_ref[q], mode)
        else:
            part = _dot(av, b_ref[...], mode)
        if stacked and mode == "tn":
            part = jnp.stack([part[:, q * n_blk:(q + 1) * n_blk] for q in range(kb)])
        if nk == 1:
            if out_relu2:
                part = jnp.maximum(part, 0.0)
                part = part * part
            if relu2_bwd_of is not None:
                part = part * relu2_slope()
            o_ref[...] = part.astype(out_dtype)
            return

        @pl.when(kk == 0)
        def _():
            o_ref[...] = part

        @pl.when(kk != 0)
        def _():
            o_ref[...] += part

        if relu2_bwd_of is not None:
            @pl.when(kk == nk - 1)
            def _():
                o_ref[...] *= relu2_slope()

    return pl.pallas_call(
        body,
        name=name,
        grid=(m // tm, n // tn, nk),
        in_specs=[a_spec, b_spec] + extra_specs + [pl.BlockSpec(memory_space=pl.ANY)] * len(after),
        out_specs=out_spec,
        out_shape=out_shape,
        compiler_params=_params("parallel", "parallel", "arbitrary"),
    )(a, b, *extra, *after)


def _group_matmul(a, w, mode, name):
    if mode in ("nn", "nt"):
        g, dk, dn = w.shape
        m = a.shape[0]
        d_in, d_out = (dk, dn) if mode == "nn" else (dn, dk)
        tm = _pick(m, (1088, 1024, 544, 512, 272, 256, 128, 64, 32, 16, 8))

        def body(a_ref, w_ref, o_ref):
            o_ref[...] = _dot(a_ref[...], w_ref[0], mode)

        return pl.pallas_call(
            body,
            name=name,
            grid=(g, m // tm),
            in_specs=[pl.BlockSpec((tm, d_in), lambda h, i: (i, h)), pl.BlockSpec((1, dk, dn), lambda h, i: (h, 0, 0))],
            out_specs=pl.BlockSpec((tm, d_out), lambda h, i: (i, h)),
            out_shape=jax.ShapeDtypeStruct((m, g * d_out), F32),
            compiler_params=_params("parallel", "parallel"),
        )(a, w)
    b = w
    m = a.shape[0]
    dk = dn = LANES
    g = a.shape[1] // dk
    tm = _pick(m, (1088, 1024, 544, 512, 272, 256, 128, 64, 32, 16, 8))

    def body(a_ref, b_ref, o_ref):
        part = _dot(a_ref[...], b_ref[...], "tn")

        @pl.when(pl.program_id(1) == 0)
        def _():
            o_ref[0] = part

        @pl.when(pl.program_id(1) != 0)
        def _():
            o_ref[0] += part

    return pl.pallas_call(
        body,
        name=name,
        grid=(g, m // tm),
        in_specs=[pl.BlockSpec((tm, dk), lambda h, i: (i, h)), pl.BlockSpec((tm, dn), lambda h, i: (i, h))],
        out_specs=pl.BlockSpec((1, dk, dn), lambda h, i: (h, 0, 0)),
        out_shape=jax.ShapeDtypeStruct((g, dk, dn), F32),
        compiler_params=_params("parallel", "arbitrary"),
    )(a, b)


def _make_group_linear(name):
    @jax.custom_vjp
    def op(x, w):
        return _group_matmul(x, w, "nn", name + "_fwd")

    def fwd(x, w):
        return op(x, w), (x, w)

    def bwd(res, dy):
        x, w = res
        return _group_matmul(dy, w, "nt", name + "_dx"), _group_matmul(x, dy, "tn", name + "_dw")

    op.defvjp(fwd, bwd)
    return op


def _my_place():
    return lax.axis_index("x"), lax.axis_index("y"), lax.axis_index("c")


def _all_gather(shard, name, after=()):
    shape, dtype = shard.shape, shard.dtype

    def body(x_ref, *rest):
        out_ref, send_sems, recv_sems, local_sem = rest[len(after):]
        x, y, c = _my_place()
        me, sibling = (x, y, c), (x, y, 1 - c)
        chips = [(1 - x, y), (x, 1 - y), (1 - x, 1 - y)]

        def slot(px, py, pc):
            return out_ref.at[4 * px + 2 * py + pc]

        def copy(k, block, to, src=None):
            return pltpu.make_async_remote_copy(
                src_ref=slot(*block) if src is None else src,
                dst_ref=slot(*block),
                send_sem=send_sems.at[k],
                recv_sem=recv_sems.at[k],
                device_id=to,
                device_id_type=pl.DeviceIdType.MESH,
            )

        mine = pltpu.make_async_copy(x_ref, slot(*me), local_sem)
        mine.start()
        first = [copy(0, me, sibling, src=x_ref)]
        first += [copy(1 + j, me, (*chip, c), src=x_ref) for j, chip in enumerate(chips)]
        for cp in first:
            cp.start()
        passed = [copy(4 + j, (*chip, c), sibling) for j, chip in enumerate(chips)]
        for j, chip in enumerate(chips):
            copy(1 + j, (*chip, c), me).wait_recv()
            passed[j].start()
        copy(0, sibling, me).wait_recv()
        for j, chip in enumerate(chips):
            copy(4 + j, (*chip, 1 - c), me).wait_recv()
        for cp in first + passed:
            cp.wait_send()
        mine.wait()

    return pl.pallas_call(
        body,
        name=name,
        out_shape=jax.ShapeDtypeStruct((N_DEV,) + shape, dtype),
        in_specs=[pl.BlockSpec(memory_space=pl.ANY)] * (1 + len(after)),
        out_specs=pl.BlockSpec(memory_space=pl.ANY),
        scratch_shapes=[pltpu.SemaphoreType.DMA((7,)), pltpu.SemaphoreType.DMA((7,)), pltpu.SemaphoreType.DMA],
    )(shard, *after)


def _all_to_all(stacked, name, after=()):
    def body(x_ref, *rest):
        out_ref, send_sems, recv_sems, local_sem = rest[len(after):]
        x, y, c = _my_place()
        me = 4 * x + 2 * y + c
        mine = pltpu.make_async_copy(x_ref.at[me], out_ref.at[me], local_sem)
        mine.start()
        copies = []
        for k in range(1, N_DEV):
            px, py, pc = x ^ ((k >> 2) & 1), y ^ ((k >> 1) & 1), c ^ (k & 1)
            peer = 4 * px + 2 * py + pc
            copies.append(
                pltpu.make_async_remote_copy(
                    src_ref=x_ref.at[peer],
                    dst_ref=out_ref.at[me],
                    send_sem=send_sems.at[k - 1],
                    recv_sem=recv_sems.at[k - 1],
                    device_id=(px, py, pc),
                    device_id_type=pl.DeviceIdType.MESH,
                )
            )
        for cp in copies:
            cp.start()
        for cp in copies:
            cp.wait_recv()
        for cp in copies:
            cp.wait_send()
        mine.wait()

    return pl.pallas_call(
        body,
        name=name,
        out_shape=jax.ShapeDtypeStruct(stacked.shape, stacked.dtype),
        in_specs=[pl.BlockSpec(memory_space=pl.ANY)] * (1 + len(after)),
        out_specs=pl.BlockSpec(memory_space=pl.ANY),
        scratch_shapes=[pltpu.SemaphoreType.DMA((7,)), pltpu.SemaphoreType.DMA((7,)), pltpu.SemaphoreType.DMA],
    )(stacked, *after)


def _sum_blocks(stacked, name):
    _, r, c = stacked.shape
    tr = _pick(r, (256, 128, 64, 32, 16, 8))

    def body(x_ref, o_ref):
        s = [x_ref[j] for j in range(N_DEV)]
        o_ref[...] = ((s[0] + s[1]) + (s[2] + s[3])) + ((s[4] + s[5]) + (s[6] + s[7]))

    return pl.pallas_call(
        body,
        name=name,
        grid=(r // tr,),
        in_specs=[pl.BlockSpec((N_DEV, tr, c), lambda i: (0, i, 0))],
        out_specs=pl.BlockSpec((tr, c), lambda i: (i, 0)),
        out_shape=jax.ShapeDtypeStruct((r, c), stacked.dtype),
        compiler_params=_params("parallel"),
    )(stacked)


def _stack_cols(full):
    k, n8 = full.shape
    return full.reshape(k, N_DEV, n8 // N_DEV).transpose(1, 0, 2)


def _unstack_cols(stacked):
    j, k, n = stacked.shape
    return stacked.transpose(1, 0, 2).reshape(k, j * n)


def _split_cols(p, cuts):
    bounds = (0,) + tuple(cuts) + (p.shape[1],)

    @jax.custom_vjp
    def op(z):
        return tuple(z[:, lo:hi] for lo, hi in zip(bounds[:-1], bounds[1:]))

    op.defvjp(lambda z: (op(z), None), lambda _, cots: (jnp.concatenate(cots, axis=1),))
    return op(p)


def _gather_weight(shard, cols, name):
    g = _all_gather(shard.astype(BF16), name)
    return _unstack_cols(g) if cols else g.reshape(-1, shard.shape[1])


def _scatter_grad(full, cols, name, after=()):
    if cols:
        st = _stack_cols(full)
    else:
        st = full.reshape(N_DEV, full.shape[0] // N_DEV, full.shape[1])
    return _sum_blocks(_all_to_all(st, name + "_a2a", after), name + "_sum")


def _make_fsdp_linear(cols, name, unpad=None):
    @jax.custom_vjp
    def op(x, w_shard, w_full):
        return _matmul(x, w_full, "nn", name + "_fwd")

    def fwd(x, w_shard, w_full):
        return op(x, w_shard, w_full), (x, w_full)

    def bwd(res, dy):
        x, w = res
        dx = _matmul(dy, w, "nt", name + "_dx")
        dw = _matmul(x, dy, "tn", name + "_dw")
        dw = dw if unpad is None else unpad(dw)
        return dx, _scatter_grad(dw, cols, name + "_rs"), jnp.zeros_like(w)

    op.defvjp(fwd, bwd)
    return op


def _make_fsdp_param(name):
    @jax.custom_vjp
    def op(shard, full):
        return full

    def fwd(shard, full):
        return full, None

    def bwd(_, g):
        return _scatter_grad(g, True, name + "_rs"), jnp.zeros_like(g)

    op.defvjp(fwd, bwd)
    return op


def _allreduce_replicated(gs, name, after=()):
    flat = jnp.concatenate([g.reshape(-1) for g in gs])
    n = flat.shape[0]
    rows = -(-n // (256 * LANES)) * 256
    packed = jnp.pad(flat, (0, rows * LANES - n)).reshape(rows, LANES)
    total = _sum_blocks(_all_gather(packed, name + "_ag", after), name + "_sum").reshape(-1)
    out, off = [], 0
    for g in gs:
        out.append(total[off:off + g.size].reshape(g.shape))
        off += g.size
    return out


_HBM = pl.BlockSpec(memory_space=pltpu.HBM)
_SEM = pl.BlockSpec(memory_space=pltpu.SEMAPHORE)
_SIDE_EFFECT = pltpu.SideEffectType.DATAFLOW_SIDE_EFFECTING
_N_PEERS = N_DEV - 1


def _peer(k):
    x, y, c = _my_place()
    return x ^ ((k >> 2) & 1), y ^ ((k >> 1) & 1), c ^ (k & 1)


def _exchange_start(src, land_shape, gather, name, after=()):
    def body(src_ref, land_ref, *rest):
        send_sems, recv_sems, src_thru, land_thru, token = rest[len(after):]
        x, y, c = _my_place()
        me = 4 * x + 2 * y + c
        for k in range(1, N_DEV):
            px, py, pc = _peer(k)
            pltpu.make_async_remote_copy(
                src_ref=src_ref if gather else src_ref.at[4 * px + 2 * py + pc],
                dst_ref=land_ref.at[me] if gather else land_ref.at[k - 1],
                send_sem=send_sems.at[k - 1],
                recv_sem=recv_sems.at[k - 1],
                device_id=(px, py, pc),
                device_id_type=pl.DeviceIdType.MESH,
            ).start()
        token[...] = jnp.zeros_like(token)

    return pl.pallas_call(
        body,
        name=name,
        out_shape=(
            pltpu.SemaphoreType.DMA((_N_PEERS,)),
            pltpu.SemaphoreType.DMA((_N_PEERS,)),
            pltpu.HBM(src.shape, src.dtype),
            pltpu.HBM(land_shape, src.dtype),
            jax.ShapeDtypeStruct((8, LANES), F32),
        ),
        in_specs=(_HBM, _HBM) + (pl.BlockSpec(memory_space=pl.ANY),) * len(after),
        out_specs=(_SEM, _SEM, _HBM, _HBM, pl.BlockSpec(memory_space=pltpu.VMEM)),
        input_output_aliases={0: 2, 1: 3},
        compiler_params=pltpu.CompilerParams(has_side_effects=_SIDE_EFFECT),
    )(pltpu.with_memory_space_constraint(src, pltpu.HBM),
      pltpu.with_memory_space_constraint(lax.empty(land_shape, src.dtype), pltpu.HBM), *after)


def _gather_start_all(shards, name):
    n = len(shards)

    def body(*refs):
        srcs, lands = refs[:n], refs[n:2 * n]
        outs = refs[2 * n:]
        send_sems, recv_sems, token = outs[:n], outs[n:2 * n], outs[-1]
        x, y, c = _my_place()
        me = 4 * x + 2 * y + c
        for i in range(n):
            for k in range(1, N_DEV):
                pltpu.make_async_remote_copy(
                    src_ref=srcs[i],
                    dst_ref=lands[i].at[me],
                    send_sem=send_sems[i].at[k - 1],
                    recv_sem=recv_sems[i].at[k - 1],
                    device_id=_peer(k),
                    device_id_type=pl.DeviceIdType.MESH,
                ).start()
        token[...] = jnp.zeros_like(token)

    lands = [(N_DEV,) + s.shape for s in shards]
    sems = tuple(pltpu.SemaphoreType.DMA((_N_PEERS,)) for _ in range(2 * n))
    res = pl.pallas_call(
        body,
        name=name,
        out_shape=sems + tuple(pltpu.HBM(s.shape, s.dtype) for s in shards)
        + tuple(pltpu.HBM(ls, s.dtype) for ls, s in zip(lands, shards)) + (jax.ShapeDtypeStruct((8, LANES), F32),),
        in_specs=(_HBM,) * (2 * n),
        out_specs=(_SEM,) * (2 * n) + (_HBM,) * (2 * n) + (pl.BlockSpec(memory_space=pltpu.VMEM),),
        input_output_aliases={i: 2 * n + i for i in range(2 * n)},
        compiler_params=pltpu.CompilerParams(has_side_effects=_SIDE_EFFECT),
    )(*[pltpu.with_memory_space_constraint(s, pltpu.HBM) for s in shards],
      *[pltpu.with_memory_space_constraint(lax.empty(ls, s.dtype), pltpu.HBM) for ls, s in zip(lands, shards)])
    return [(res[i], res[n + i], res[2 * n + i], res[3 * n + i], res[-1]) for i in range(n)]


def _exchange_wait(handle, gather, after, name):
    send_sems, recv_sems, src_thru, land_thru, _ = handle

    def body(src_ref, land_ref, send_sems, recv_sems, after_ref, src_dead, got_ref):
        for k in range(1, N_DEV):
            cp = pltpu.make_async_remote_copy(
                src_ref=src_ref if gather else src_ref.at[k],
                dst_ref=land_ref.at[k - 1],
                send_sem=send_sems.at[k - 1],
                recv_sem=recv_sems.at[k - 1],
                device_id=_peer(k),
                device_id_type=pl.DeviceIdType.MESH,
            )
            cp.wait_send()
            cp.wait_recv()

    return pl.pallas_call(
        body,
        name=name,
        out_shape=(pltpu.HBM(src_thru.shape, src_thru.dtype), pltpu.HBM(land_thru.shape, land_thru.dtype)),
        in_specs=(_HBM, _HBM, _SEM, _SEM, pl.BlockSpec(memory_space=pl.ANY)),
        out_specs=(_HBM, _HBM),
        input_output_aliases={0: 0, 1: 1},
        compiler_params=pltpu.CompilerParams(has_side_effects=_SIDE_EFFECT),
    )(src_thru, land_thru, send_sems, recv_sems, after)[1]


def _sum_own_and_peers(own, land, name):
    r, c = own.shape
    tr = _pick(r, (256, 128, 64, 32, 16, 8))

    def body(o_ref, l_ref, out_ref):
        s = [l_ref[j] for j in range(_N_PEERS)]
        out_ref[...] = ((o_ref[...] + s[0]) + (s[1] + s[2])) + ((s[3] + s[4]) + (s[5] + s[6]))

    return pl.pallas_call(
        body,
        name=name,
        grid=(r // tr,),
        in_specs=[pl.BlockSpec((tr, c), lambda i: (i, 0)), pl.BlockSpec((_N_PEERS, tr, c), lambda i: (0, i, 0))],
        out_specs=pl.BlockSpec((tr, c), lambda i: (i, 0)),
        out_shape=jax.ShapeDtypeStruct((r, c), own.dtype),
        compiler_params=_params("parallel"),
    )(own, land)


class _Cols:
    def __init__(self, array, width, block):
        self.array, self.width, self.block = array, width, block
        self.shape, self.dtype = (array.shape[0], width), array.dtype


def _base(a):
    return a.array if isinstance(a, _Cols) else a


def _col_block(a):
    return a.block if isinstance(a, _Cols) else 0


def _make_rowwise(f, name, n_rows, n_tabs, n_pars):
    n_in = n_rows + n_tabs + n_pars

    def specs(args, tm):
        blocked = [pl.BlockSpec((tm, a.shape[1]), lambda i, blk=_col_block(a): (i, blk)) for a in args[: n_rows + n_tabs]]
        whole = [pl.BlockSpec(a.shape, lambda i: (0, 0)) for a in args[n_rows + n_tabs:]]
        return blocked + whole

    def out_struct(args, tm):
        blk = [jax.ShapeDtypeStruct((tm, a.shape[1]), a.dtype) for a in args[: n_rows + n_tabs]]
        blk += [jax.ShapeDtypeStruct(a.shape, a.dtype) for a in args[n_rows + n_tabs:]]
        return jax.eval_shape(f, *blk)

    def fwd_call(*args):
        r = args[0].shape[0]
        tm = _row_tile(r, max(a.shape[1] for a in args[:n_rows]))
        ro, so = out_struct(args, tm)

        def body(*refs):
            vals = [x[...] for x in refs[:n_in]]
            outs = refs[n_in:]
            rv, sv = f(*vals)
            for o, v in zip(outs[: len(ro)], rv):
                o[...] = v
            for o, v in zip(outs[len(ro):], sv):
                @pl.when(pl.program_id(0) == 0)
                def _(o=o, v=v):
                    o[...] = v

                @pl.when(pl.program_id(0) != 0)
                def _(o=o, v=v):
                    o[...] += v

        out_shape = [jax.ShapeDtypeStruct((r, s.shape[1]), s.dtype) for s in ro]
        out_shape += [jax.ShapeDtypeStruct(s.shape, s.dtype) for s in so]
        out_specs = [pl.BlockSpec((tm, s.shape[1]), lambda i: (i, 0)) for s in ro]
        out_specs += [pl.BlockSpec(s.shape, lambda i: (0, 0)) for s in so]
        res = pl.pallas_call(
            body,
            name=name + "_fwd",
            grid=(r // tm,),
            in_specs=specs(args, tm),
            out_specs=out_specs,
            out_shape=out_shape,
            compiler_params=_params("arbitrary" if so else "parallel"),
        )(*[_base(a) for a in args])
        return tuple(res[: len(ro)]), tuple(res[len(ro):])

    def bwd_call(args, cots, more=(), row_dtypes=None):
        r = args[0].shape[0]
        tm = _row_tile(r, max(a.shape[1] for a in args[:n_rows]))
        ro, so = out_struct(args, tm)
        crow, csum = cots
        rows, tabs, pars = args[:n_rows], args[n_rows:n_rows + n_tabs], args[n_rows + n_tabs:]
        n_c = len(crow) + len(csum)

        def body(*refs):
            vals = [x[...] for x in refs[:n_in]]
            cv = [x[...] for x in refs[n_in:n_in + n_c]]
            for x in refs[n_in + n_c:n_in + n_c + len(more)]:
                cv[0] = cv[0] + x[...]
            outs = refs[n_in + n_c + len(more):]
            tv = vals[n_rows:n_rows + n_tabs]

            def g(*dargs):
                return f(*dargs[:n_rows], *tv, *dargs[n_rows:])

            _, vjp = jax.vjp(g, *vals[:n_rows], *vals[n_rows + n_tabs:])
            d = vjp((tuple(cv[: len(crow)]), tuple(cv[len(crow):])))
            for o, v in zip(outs[:n_rows], d[:n_rows]):
                o[...] = v.astype(o.dtype)
            for o, v in zip(outs[n_rows:], d[n_rows:]):
                @pl.when(pl.program_id(0) == 0)
                def _(o=o, v=v):
                    o[...] = v

                @pl.when(pl.program_id(0) != 0)
                def _(o=o, v=v):
                    o[...] += v

        in_specs = specs(args, tm)
        in_specs += [pl.BlockSpec((tm, c.shape[1]), lambda i: (i, 0)) for c in crow]
        in_specs += [pl.BlockSpec(c.shape, lambda i: (0, 0)) for c in csum]
        in_specs += [pl.BlockSpec((tm, c.shape[1]), lambda i: (i, 0)) for c in more]
        out_shape = [jax.ShapeDtypeStruct(a.shape, dt) for a, dt in zip(rows, row_dtypes or [a.dtype for a in rows])]
        out_shape += [jax.ShapeDtypeStruct(a.shape, a.dtype) for a in pars]
        out_specs = [pl.BlockSpec((tm, a.shape[1]), lambda i: (i, 0)) for a in rows]
        out_specs += [pl.BlockSpec(a.shape, lambda i: (0, 0)) for a in pars]
        res = pl.pallas_call(
            body,
            name=name + "_bwd",
            grid=(r // tm,),
            in_specs=in_specs,
            out_specs=out_specs,
            out_shape=out_shape,
            compiler_params=_params("arbitrary" if pars else "parallel"),
        )(*[_base(a) for a in args], *crow, *csum, *more)
        return tuple(res[:n_rows]), tuple(res[n_rows:])

    @jax.custom_vjp
    def op(rows, tabs, pars):
        return fwd_call(*rows, *tabs, *pars)

    op.fwd_call, op.bwd_call = fwd_call, bwd_call

    def fwd(rows, tabs, pars):
        return fwd_call(*rows, *tabs, *pars), (rows, tabs, pars)

    def bwd(res, cots):
        rows, tabs, pars = res
        drows, dpars = bwd_call(tuple(rows) + tuple(tabs) + tuple(pars), cots)
        return drows, tuple(jnp.zeros_like(t) for t in tabs), dpars

    op.defvjp(fwd, bwd)
    return op


def _sigmoid(x):
    return 0.5 * (jnp.tanh(0.5 * x) + 1.0)


@jax.custom_jvp
def _softplus(x):
    e = jnp.exp(-jnp.abs(x))
    u = 1.0 + e
    log1p_e = jnp.where(u == 1.0, e, e * jnp.log(u) / jnp.where(u == 1.0, 1.0, u - 1.0))
    return jnp.maximum(x, 0.0) + log1p_e


@_softplus.defjvp
def _softplus_jvp(primals, tangents):
    (x,), (t,) = primals, tangents
    return _softplus(x), t * _sigmoid(x)


def _gelu(x):
    return 0.5 * x * (1.0 + jnp.tanh(math.sqrt(2.0 / math.pi) * (x + 0.044715 * (x * x * x))))


def _ln_res_f(h, mix, g, b):
    z = DN_ALPHA * h + mix
    mu = jnp.mean(z, axis=-1, keepdims=True)
    zc = z - mu
    var = jnp.mean(zc * zc, axis=-1, keepdims=True)
    return (zc * lax.rsqrt(var + EPS) * g + b,), ()


def _ln_res_copy_f(h, mix, g, b):
    (out,), _ = _ln_res_f(h, mix, g, b)
    return (out, out.astype(BF16)), ()


def _rmsnorm_f(x, g):
    return (x * lax.rsqrt(jnp.mean(x * x, axis=-1, keepdims=True) + EPS) * g,), ()


def _lru_gates_f(ga, gx, xc, b_a, b_x, lam):
    r = _sigmoid(ga + b_a)
    i = _sigmoid(gx + b_x)
    log_a = -LRU_C * r * _softplus(-lam)
    a = jnp.exp(log_a)
    one_minus_a2 = jnp.tanh(-log_a) * (jnp.exp(2.0 * log_a) + 1.0)
    return (a, jnp.sqrt(one_minus_a2) * (i * xc)), ()


def _lru_out_f(hh, p_gate):
    return (hh * _gelu(p_gate),), ()


def _rope_ret_f(q, k, cos2, sin2):
    d = cos2.shape[1]
    half = d // 2
    k_scale = d ** -0.5

    def rope(x):
        outs = []
        for h in range(x.shape[1] // d):
            xh = x[:, h * d:(h + 1) * d]
            rot = jnp.concatenate([xh[:, half:], xh[:, :half]], axis=1)
            outs.append(xh * cos2 + rot * sin2)
        return jnp.concatenate(outs, axis=1)

    return (rope(q), rope(k) * k_scale), ()


def _ret_out_f(o, g):
    d = o.shape[1] // RET_HEADS
    outs = []
    for h in range(RET_HEADS):
        oh = o[:, h * d:(h + 1) * d]
        outs.append(oh * lax.rsqrt(jnp.mean(oh * oh, axis=-1, keepdims=True) + EPS))
    y = jnp.concatenate(outs, axis=1)
    return (g * _sigmoid(g) * y,), ()


def _loss_f(y, t, mask):
    e = (y - t) * mask
    per_row = jnp.sum(e * e, axis=-1, keepdims=True) * (0.5 / y.shape[1])
    total = jnp.sum(per_row, axis=0, keepdims=True)
    return (), (jnp.broadcast_to(total, (1, LANES)),)


def _shift_down(x, s):
    if s == 0:
        return x
    t = x.shape[0]
    row = lax.broadcasted_iota(jnp.int32, x.shape, 0)
    return jnp.where(row >= s, pltpu.roll(x, s, 0), 0.0)


def _shift_up(x, s):
    if s == 0:
        return x
    t = x.shape[0]
    row = lax.broadcasted_iota(jnp.int32, x.shape, 0)
    return jnp.where(row < t - s, pltpu.roll(x, t - s, 0), 0.0)


def _conv_fwd(x, w, b, name):
    bsz, t, c = x.shape
    width = w.shape[0]

    def body(x_ref, w_ref, b_ref, y_ref):
        xv = x_ref[0]
        acc = jnp.broadcast_to(b_ref[...], xv.shape)
        for k in range(width):
            acc = acc + w_ref[k:k + 1, :] * _shift_down(xv, width - 1 - k)
        y_ref[0] = acc

    return pl.pallas_call(
        body,
        name=name,
        grid=(bsz, c // LANES),
        in_specs=[
            pl.BlockSpec((1, t, LANES), lambda i, j: (i, 0, j)),
            pl.BlockSpec((width, LANES), lambda i, j: (0, j)),
            pl.BlockSpec((1, LANES), lambda i, j: (0, j)),
        ],
        out_specs=pl.BlockSpec((1, t, LANES), lambda i, j: (i, 0, j)),
        out_shape=jax.ShapeDtypeStruct(x.shape, F32),
        compiler_params=_params("parallel", "parallel"),
    )(x, w, b)


def _conv_bwd(x, w, dy, name):
    bsz, t, c = x.shape
    width = w.shape[0]

    def body(x_ref, w_ref, dy_ref, dx_ref, dw_ref, db_ref):
        xv, g = x_ref[0], dy_ref[0]
        dx = jnp.zeros_like(xv)
        dws = []
        for k in range(width):
            s = width - 1 - k
            dx = dx + w_ref[k:k + 1, :] * _shift_up(g, s)
            dws.append(jnp.sum(g * _shift_down(xv, s), axis=0, keepdims=True))
        dx_ref[0] = dx
        dw = jnp.concatenate(dws, axis=0)
        db = jnp.sum(g, axis=0, keepdims=True)

        @pl.when(pl.program_id(1) == 0)
        def _():
            dw_ref[...] = dw
            db_ref[...] = db

        @pl.when(pl.program_id(1) != 0)
        def _():
            dw_ref[...] += dw
            db_ref[...] += db

    return pl.pallas_call(
        body,
        name=name,
        grid=(c // LANES, bsz),
        in_specs=[
            pl.BlockSpec((1, t, LANES), lambda j, i: (i, 0, j)),
            pl.BlockSpec((width, LANES), lambda j, i: (0, j)),
            pl.BlockSpec((1, t, LANES), lambda j, i: (i, 0, j)),
        ],
        out_specs=[
            pl.BlockSpec((1, t, LANES), lambda j, i: (i, 0, j)),
            pl.BlockSpec((width, LANES), lambda j, i: (0, j)),
            pl.BlockSpec((1, LANES), lambda j, i: (0, j)),
        ],
        out_shape=[
            jax.ShapeDtypeStruct(x.shape, F32),
            jax.ShapeDtypeStruct(w.shape, F32),
            jax.ShapeDtypeStruct((1, c), F32),
        ],
        compiler_params=_params("parallel", "arbitrary"),
    )(x, w, dy)


def _make_conv(name):
    @jax.custom_vjp
    def op(x, w, b):
        return _conv_fwd(x, w, b, name + "_fwd")

    def fwd(x, w, b):
        return op(x, w, b), (x, w)

    def bwd(res, dy):
        x, w = res
        return tuple(_conv_bwd(x, w, dy, name + "_bwd"))

    op.defvjp(fwd, bwd)
    return op


_SCAN_ROWS = 8


def _scan_fwd(a, b, name):
    bsz, t, c = a.shape
    cw = _pick(c, (4 * LANES, 2 * LANES, LANES))

    def body(a_ref, b_ref, h_ref):
        row = lax.broadcasted_iota(jnp.int32, (_SCAN_ROWS, cw), 0)

        def step(i, carry):
            r0 = pl.multiple_of(i * _SCAN_ROWS, _SCAN_ROWS)
            av, bv = a_ref[0, pl.ds(r0, _SCAN_ROWS), :], b_ref[0, pl.ds(r0, _SCAN_ROWS), :]
            for s in (1, 2, 4):
                a_sh = jnp.where(row >= s, pltpu.roll(av, s, 0), 1.0)
                b_sh = jnp.where(row >= s, pltpu.roll(bv, s, 0), 0.0)
                bv = av * b_sh + bv
                av = av * a_sh
            hv = bv + av * carry
            h_ref[0, pl.ds(r0, _SCAN_ROWS), :] = hv
            return hv[_SCAN_ROWS - 1:, :]

        lax.fori_loop(0, t // _SCAN_ROWS, step, jnp.zeros((1, cw), F32), unroll=2)

    spec = pl.BlockSpec((1, t, cw), lambda i, j: (i, 0, j))
    return pl.pallas_call(
        body,
        name=name,
        grid=(bsz, c // cw),
        in_specs=[spec, spec],
        out_specs=spec,
        out_shape=jax.ShapeDtypeStruct(a.shape, F32),
        compiler_params=_params("parallel", "parallel"),
    )(a, b)


def _scan_bwd(a, h, g, name):
    bsz, t, c = a.shape
    cw = _pick(c, (2 * LANES, LANES))

    def body(a_ref, h_ref, g_ref, da_ref, db_ref):
        rows = _SCAN_ROWS
        row = lax.broadcasted_iota(jnp.int32, (rows, cw), 0)
        n_tiles = t // rows

        def step(n, carry):
            lam_next, a_next = carry
            i = n_tiles - 1 - n
            r0 = pl.multiple_of(i * rows, rows)
            rp = pl.multiple_of(jnp.maximum(i - 1, 0) * rows, rows)
            av, gv, hv = a_ref[0, pl.ds(r0, rows), :], g_ref[0, pl.ds(r0, rows), :], h_ref[0, pl.ds(r0, rows), :]
            h_before = jnp.where(i > 0, h_ref[0, pl.ds(rp, rows), :][rows - 1:, :], 0.0)
            cv = jnp.where(row < rows - 1, pltpu.roll(av, rows - 1, 0), a_next)
            for s in (1, 2, 4):
                c_sh = jnp.where(row < rows - s, pltpu.roll(cv, rows - s, 0), 1.0)
                g_sh = jnp.where(row < rows - s, pltpu.roll(gv, rows - s, 0), 0.0)
                gv = cv * g_sh + gv
                cv = cv * c_sh
            lam = gv + cv * lam_next
            db_ref[0, pl.ds(r0, rows), :] = lam
            da_ref[0, pl.ds(r0, rows), :] = lam * jnp.where(row >= 1, pltpu.roll(hv, 1, 0), h_before)
            return lam[:1, :], av[:1, :]

        zero = jnp.zeros((1, cw), F32)
        lax.fori_loop(0, n_tiles, step, (zero, zero), unroll=2)

    spec = pl.BlockSpec((1, t, cw), lambda i, j: (i, 0, j))
    return pl.pallas_call(
        body,
        name=name,
        grid=(bsz, c // cw),
        in_specs=[spec, spec, spec],
        out_specs=[spec, spec],
        out_shape=[jax.ShapeDtypeStruct(a.shape, F32)] * 2,
        compiler_params=_params("parallel", "parallel"),
    )(a, h, g)


def _make_scan(name):
    @jax.custom_vjp
    def op(a, b):
        return _scan_fwd(a, b, name + "_fwd")

    def fwd(a, b):
        h = op(a, b)
        return h, (a, h)

    def bwd(res, g):
        a, h = res
        da, db = _scan_bwd(a, h, g, name + "_bwd")
        return da, db

    op.defvjp(fwd, bwd)
    return op


def _query_blocks(t):
    blocks, start = [], 0
    while start < t:
        rows = 2 * SEQ_BLOCK if start + 2 * SEQ_BLOCK <= t else SEQ_BLOCK
        blocks.append((start, rows))
        start += rows
    return blocks


def _attn_exp(q, k, start, scale):
    tq, tk = q.shape[0], k.shape[0]
    s = _dot(q, k, "nt") * scale
    qpos = start + lax.broadcasted_iota(jnp.int32, (tq, tk), 0)
    kpos = lax.broadcasted_iota(jnp.int32, (tq, tk), 1)
    s = jnp.where(kpos <= qpos, s, NEG_INF)
    e = jnp.exp(s - jnp.max(s, axis=-1, keepdims=True))
    return e, 1.0 / jnp.sum(e, axis=-1, keepdims=True)


_MLA_SCALE = (MLA_NOPE + MLA_ROPE) ** -0.5


def _attn_specs(t):
    head = pl.BlockSpec((1, t, LANES), lambda b, h: (b, 0, h))
    shared = pl.BlockSpec((1, t, LANES), lambda b, h: (b, 0, 0))
    return head, shared


def _attn_fwd(q, kv, kpe, name):
    bsz, t, hl = q.shape
    head, shared = _attn_specs(t)

    def body(q_ref, kv_ref, kpe_ref, o_ref, k_s, v_s):
        lane = lax.broadcasted_iota(jnp.int32, (t, LANES), 1)
        kvh = kv_ref[0]
        k_s[...] = jnp.where(lane < MLA_NOPE, kvh, kpe_ref[0]).astype(BF16)
        v_s[...] = kvh.astype(BF16)
        for start, rows in _query_blocks(t):
            n = start + rows
            e, inv_l = _attn_exp(q_ref[0, start:n, :], k_s[:n, :], start, _MLA_SCALE)
            o_ref[0, start:n, :] = _dot(e, v_s[:n, :], "nn") * inv_l

    return pl.pallas_call(
        body,
        name=name,
        grid=(bsz, hl // LANES),
        in_specs=[head, head, shared],
        out_specs=head,
        out_shape=jax.ShapeDtypeStruct(q.shape, F32),
        scratch_shapes=[pltpu.VMEM((t, LANES), BF16), pltpu.VMEM((t, LANES), BF16)],
        compiler_params=_params("parallel", "parallel"),
    )(q, kv, kpe)


def _attn_bwd(q, kv, kpe, do, name):
    bsz, t, hl = q.shape
    head, shared = _attn_specs(t)

    def body(q_ref, kv_ref, kpe_ref, do_ref, dq_ref, dkv_ref, dkpe_ref, k_s, v_s, dk_s, dv_s):
        lane = lax.broadcasted_iota(jnp.int32, (t, LANES), 1)
        kvh = kv_ref[0]
        k_s[...] = jnp.where(lane < MLA_NOPE, kvh, kpe_ref[0]).astype(BF16)
        v_s[...] = kvh.astype(BF16)
        for start, rows in reversed(_query_blocks(t)):
            n = start + rows
            qb = q_ref[0, start:n, :]
            dob = jnp.where(lane[:rows] >= MLA_NOPE, do_ref[0, start:n, :], 0.0)
            kk, vv = k_s[:n, :], v_s[:n, :]
            e, inv_l = _attn_exp(qb, kk, start, _MLA_SCALE)
            dob = dob * inv_l
            dp = _dot(dob, vv, "nt")
            ds = e * (dp - inv_l * jnp.sum(dp * e, axis=-1, keepdims=True)) * _MLA_SCALE
            dq_ref[0, start:n, :] = _dot(ds, kk, "nn")
            if n == t:
                dk_s[...] = _dot(ds, qb, "tn")
                dv_s[...] = _dot(e, dob, "tn")
            else:
                dk_s[:n, :] += _dot(ds, qb, "tn")
                dv_s[:n, :] += _dot(e, dob, "tn")
        dk = dk_s[...]
        dkv_ref[0] = jnp.where(lane < MLA_NOPE, dk, dv_s[...])
        dkpe = jnp.where(lane >= MLA_NOPE, dk, 0.0)

        @pl.when(pl.program_id(1) == 0)
        def _():
            dkpe_ref[0] = dkpe

        @pl.when(pl.program_id(1) != 0)
        def _():
            dkpe_ref[0] += dkpe

    return pl.pallas_call(
        body,
        name=name,
        grid=(bsz, hl // LANES),
        in_specs=[head, head, shared, head],
        out_specs=[head, head, shared],
        out_shape=[
            jax.ShapeDtypeStruct(q.shape, F32),
            jax.ShapeDtypeStruct(kv.shape, F32),
            jax.ShapeDtypeStruct(kpe.shape, F32),
        ],
        scratch_shapes=[pltpu.VMEM((t, LANES), BF16), pltpu.VMEM((t, LANES), BF16),
                        pltpu.VMEM((t, LANES), F32), pltpu.VMEM((t, LANES), F32)],
        compiler_params=_params("parallel", "arbitrary"),
    )(q, kv, kpe, do)


def _make_attention(name):
    @jax.custom_vjp
    def op(q, kv, kpe):
        return _attn_fwd(q, kv, kpe, name + "_fwd")

    def fwd(q, kv, kpe):
        return op(q, kv, kpe), (q, kv, kpe)

    def bwd(res, do):
        return tuple(_attn_bwd(*res, do, name + "_bwd"))

    op.defvjp(fwd, bwd)
    return op


_ROPE_SHIFT = MLA_ROPE // 2


def _rope_lanes_call(x, c, sm, sp, transpose, name):
    r, width = x.shape
    tm = _row_tile(r, width)

    def body(x_ref, c_ref, sm_ref, sp_ref, y_ref):
        cv, smv, spv = c_ref[...], sm_ref[...], sp_ref[...]
        for b in range(width // LANES):
            xb = x_ref[:, b * LANES:(b + 1) * LANES]
            if transpose:
                yb = xb * cv + pltpu.roll(xb * smv, _ROPE_SHIFT, 1) + pltpu.roll(xb * spv, LANES - _ROPE_SHIFT, 1)
            else:
                yb = xb * cv + pltpu.roll(xb, LANES - _ROPE_SHIFT, 1) * smv + pltpu.roll(xb, _ROPE_SHIFT, 1) * spv
            y_ref[:, b * LANES:(b + 1) * LANES] = yb

    tab = pl.BlockSpec((tm, LANES), lambda i: (i, 0))
    blk = pl.BlockSpec((tm, width), lambda i: (i, 0))
    return pl.pallas_call(
        body,
        name=name,
        grid=(r // tm,),
        in_specs=[blk, tab, tab, tab],
        out_specs=blk,
        out_shape=jax.ShapeDtypeStruct(x.shape, F32),
        compiler_params=_params("parallel"),
    )(x, c, sm, sp)


def _make_rope_lanes(name):
    @jax.custom_vjp
    def op(x, c, sm, sp):
        return _rope_lanes_call(x, c, sm, sp, False, name + "_fwd")

    def fwd(x, c, sm, sp):
        return op(x, c, sm, sp), (c, sm, sp)

    def bwd(res, dy):
        c, sm, sp = res
        return _rope_lanes_call(dy, c, sm, sp, True, name + "_bwd"), jnp.zeros_like(c), jnp.zeros_like(sm), jnp.zeros_like(sp)

    op.defvjp(fwd, bwd)
    return op


def _ret_chunk_rows(t):
    return t // 4 if t % 32 == 0 else SEQ_BLOCK


def _ret_decays(c, log_gamma):
    row = lax.broadcasted_iota(jnp.int32, (c, 1), 0)
    col = lax.broadcasted_iota(jnp.int32, (1, c), 1)
    rowf = row.astype(F32)
    d = jnp.where(row >= col, jnp.exp(log_gamma * rowf) * jnp.exp(-log_gamma * col.astype(F32)), 0.0)
    return d, jnp.exp(log_gamma * (rowf + 1.0)), jnp.exp(log_gamma * (c - 1.0 - rowf)), jnp.exp(log_gamma * c)


def _ret_specs(c, dk, dv, v_block0, n_chunks, reverse):
    pos = (lambda i: n_chunks - 1 - i) if reverse else (lambda i: i)
    return (
        pl.BlockSpec(memory_space=pltpu.SMEM),
        pl.BlockSpec((1, c, dk), lambda b, h, i: (b, pos(i), h)),
        pl.BlockSpec((1, c, dv), lambda b, h, i: (b, pos(i), h + v_block0)),
        pl.BlockSpec((1, c, dv), lambda b, h, i: (b, pos(i), h)),
        pl.BlockSpec((1, 1, dk, dv), lambda b, h, i: (b, h * n_chunks + pos(i), 0, 0)),
    )


def _ret_fwd(lg, q, k, v, name, dv=None, v_block0=0):
    bsz, t, hdk = q.shape
    heads = lg.shape[0]
    dk, dv = hdk // heads, dv or v.shape[2] // heads
    c = _ret_chunk_rows(t)
    n_chunks = t // c
    lg_spec, qk_spec, v_spec, o_spec, s_spec = _ret_specs(c, dk, dv, v_block0, n_chunks, False)

    def body(lg_ref, q_ref, k_ref, v_ref, o_ref, s_ref, state):
        @pl.when(pl.program_id(2) == 0)
        def _():
            state[...] = jnp.zeros((dk, dv), F32)

        d, a, b, g = _ret_decays(c, lg_ref[pl.program_id(1)])
        qb, kb, vb, s_in = q_ref[0], k_ref[0], v_ref[0], state[...]
        s_ref[0, 0] = s_in
        o_ref[0] = _dot(_dot(qb, kb, "nt") * d, vb, "nn") + a * _dot(qb, s_in, "nn")
        state[...] = g * s_in + _dot(kb * b, vb, "tn")

    return pl.pallas_call(
        body,
        name=name,
        grid=(bsz, heads, n_chunks),
        in_specs=[lg_spec, qk_spec, qk_spec, v_spec],
        out_specs=[o_spec, s_spec],
        out_shape=[jax.ShapeDtypeStruct((bsz, t, heads * dv), F32),
                   jax.ShapeDtypeStruct((bsz, heads * n_chunks, dk, dv), F32)],
        scratch_shapes=[pltpu.VMEM((dk, dv), F32)],
        compiler_params=_params("parallel", "parallel", "arbitrary"),
    )(lg, q, k, v)


def _ret_bwd(lg, q, k, v, states, do, name, v_block0=0, dv_dtype=F32):
    bsz, t, hdk = q.shape
    heads = lg.shape[0]
    dk, dv = hdk // heads, do.shape[2] // heads
    c = _ret_chunk_rows(t)
    n_chunks = t // c
    lg_spec, qk_spec, v_spec, o_spec, s_spec = _ret_specs(c, dk, dv, v_block0, n_chunks, True)

    def body(lg_ref, q_ref, k_ref, v_ref, s_ref, do_ref, dq_ref, dk_ref, dv_ref, dstate):
        @pl.when(pl.program_id(2) == 0)
        def _():
            dstate[...] = jnp.zeros((dk, dv), F32)

        d, a, b, g = _ret_decays(c, lg_ref[pl.program_id(1)])
        qb, kb, vb, dob, s_in, ds_out = q_ref[0], k_ref[0], v_ref[0], do_ref[0], s_ref[0, 0], dstate[...]
        scores = _dot(qb, kb, "nt") * d
        dscores = _dot(dob, vb, "nt") * d
        dq_ref[0] = _dot(dscores, kb, "nn") + a * _dot(dob, s_in, "nt")
        dk_ref[0] = _dot(dscores, qb, "tn") + b * _dot(vb, ds_out, "nt")
        dv_ref[0] = (_dot(scores, dob, "tn") + _dot(kb * b, ds_out, "nn")).astype(dv_dtype)
        dstate[...] = g * ds_out + _dot(qb, a * dob, "tn")

    return pl.pallas_call(
        body,
        name=name,
        grid=(bsz, heads, n_chunks),
        in_specs=[lg_spec, qk_spec, qk_spec, v_spec, s_spec, o_spec],
        out_specs=[qk_spec, qk_spec, o_spec],
        out_shape=[
            jax.ShapeDtypeStruct(q.shape, F32),
            jax.ShapeDtypeStruct(k.shape, F32),
            jax.ShapeDtypeStruct(do.shape, dv_dtype),
        ],
        scratch_shapes=[pltpu.VMEM((dk, dv), F32)],
        compiler_params=_params("parallel", "parallel", "arbitrary"),
    )(lg, q, k, v, states, do)


def _adamw(w, g, m, v, name):
    r, c = w.shape
    tr = _pick(r, (256, 128, 64, 32, 16, 8))

    def body(w_ref, g_ref, m_ref, v_ref, d_ref, nm_ref, nv_ref):
        gv = g_ref[...]
        nm = ADAM_B1 * m_ref[...] + (1.0 - ADAM_B1) * gv
        nv = ADAM_B2 * v_ref[...] + (1.0 - ADAM_B2) * (gv * gv)
        m_hat = nm / (1.0 - ADAM_B1 ** ADAM_STEP)
        v_hat = nv / (1.0 - ADAM_B2 ** ADAM_STEP)
        d_ref[...] = -ADAM_LR * (m_hat / (jnp.sqrt(v_hat) + ADAM_EPS) + ADAM_WD * w_ref[...])
        nm_ref[...] = nm
        nv_ref[...] = nv

    spec = pl.BlockSpec((tr, c), lambda i: (i, 0))
    return pl.pallas_call(
        body,
        name=name,
        grid=(r // tr,),
        in_specs=[spec] * 4,
        out_specs=[spec] * 3,
        out_shape=[jax.ShapeDtypeStruct((r, c), F32)] * 3,
        compiler_params=_params("parallel"),
    )(w, g, m, v)


def _rope_tables(t, half, reps):
    inv = ROPE_BASE ** (-jnp.arange(half, dtype=F32) / half)
    ang = jnp.arange(t, dtype=jnp.int32).astype(F32)[:, None] * inv[None, :]
    return jnp.tile(jnp.cos(ang), (1, reps)), jnp.tile(jnp.sin(ang), (1, reps))


def _padded_len(seq):
    return -(-(N_META + seq) // SEQ_BLOCK) * SEQ_BLOCK


def _embed(meta, x):
    bsz, seq, d = x.shape
    t = _padded_len(seq)
    return jnp.concatenate(
        [jnp.broadcast_to(meta[None], (bsz, N_META, d)), x, jnp.zeros((bsz, t - N_META - seq, d), F32)], axis=1
    ).reshape(bsz * t, d)


def _even_mixer(p, conv_w_shard, conv_b, w_rg_a, b_rg_a, w_rg_x, b_rg_x, lru_lambda, q_norm_g, w_uq_shard,
                kv_norm_g, w_ukv_shard, gathered, bsz):
    conv_w_full, w_uq_full, w_ukv_full = gathered
    r = p.shape[0]
    t = r // bsz

    def tile_rows(tab):
        return jnp.tile(tab, (bsz, 1))

    lru_w = w_rg_a.shape[2] * w_rg_a.shape[1]
    q_rank, kv_rank = q_norm_g.shape[1], kv_norm_g.shape[1]
    p_gate, p_rec, p_q, p_kv, p_kpe = _split_cols(
        p, (lru_w, 2 * lru_w, 2 * lru_w + q_rank, 2 * lru_w + q_rank + kv_rank))

    conv_w = _make_fsdp_param("conv_w")(conv_w_shard[0], conv_w_full)
    xc = _make_conv("conv")(p_rec.reshape(bsz, t, lru_w), conv_w, conv_b).reshape(r, lru_w)
    ga = _make_group_linear("rg_a")(xc, w_rg_a[0])
    gx = _make_group_linear("rg_x")(xc, w_rg_x[0])
    (a, bb), _ = _make_rowwise(_lru_gates_f, "lru_gates", 3, 0, 3)((ga, gx, xc), (), (b_rg_a, b_rg_x, lru_lambda))
    hh = _make_scan("lru_scan")(a.reshape(bsz, t, lru_w), bb.reshape(bsz, t, lru_w)).reshape(r, lru_w)
    (y_rec,), _ = _make_rowwise(_lru_out_f, "lru_out", 2, 0, 0)((hh, p_gate), (), ())

    (qn,), _ = _make_rowwise(_rmsnorm_f, "q_norm", 1, 0, 1)((p_q,), (), (q_norm_g,))
    (kvn,), _ = _make_rowwise(_rmsnorm_f, "kv_norm", 1, 0, 1)((p_kv,), (), (kv_norm_g,))
    d_head = MLA_NOPE + MLA_ROPE
    w_uq_pad = jnp.pad(w_uq_full.reshape(q_rank, MLA_HEADS, d_head), ((0, 0), (0, 0), (0, LANES - d_head)))

    def unpad_uq(dw):
        return dw.reshape(q_rank, MLA_HEADS, LANES)[:, :, :d_head].reshape(q_rank, MLA_HEADS * d_head)

    q = _make_fsdp_linear(True, "ev_uq", unpad_uq)(qn, w_uq_shard[0], w_uq_pad.reshape(q_rank, MLA_HEADS * LANES))
    kv = _make_fsdp_linear(True, "ev_ukv")(kvn, w_ukv_shard[0], w_ukv_full)
    half = MLA_ROPE // 2
    cos, sin = _rope_tables(t, half, 1)
    one, zero = jnp.ones((t, MLA_NOPE), F32), jnp.zeros((t, MLA_NOPE), F32)
    tail = LANES - MLA_NOPE - MLA_ROPE
    c_tab = tile_rows(jnp.concatenate([one, cos, cos, one[:, :tail]], axis=1))
    sm_tab = tile_rows(jnp.concatenate([zero, -sin, zero[:, :half + tail]], axis=1))
    sp_tab = tile_rows(jnp.concatenate([zero, zero[:, :half], sin, zero[:, :tail]], axis=1))
    q = _make_rope_lanes("rope_q")(q, c_tab, sm_tab, sp_tab)
    kpe = _make_rope_lanes("rope_k")(p_kpe, c_tab, sm_tab, sp_tab)
    o = _make_attention("mla")(q.reshape(bsz, t, -1), kv.reshape(bsz, t, -1), kpe.reshape(bsz, t, LANES))
    return jnp.concatenate([y_rec, o.reshape(r, -1)], axis=1)


def _odd_mixer_fwd(p, bsz):
    r, width = p.shape
    t = r // bsz
    qk = width // 6
    dk = qk // RET_HEADS
    cos2, sin2 = _rope_tables(t, dk // 2, 2)
    sin2 = jnp.concatenate([-sin2[:, :dk // 2], sin2[:, dk // 2:]], axis=1)
    rope_args = (_Cols(p, qk, 0), _Cols(p, qk, 1), jnp.tile(cos2, (bsz, 1)), jnp.tile(sin2, (bsz, 1)))
    (rq, rk), _ = _make_rowwise(_rope_ret_f, "rope_ret", 2, 2, 0).fwd_call(*rope_args)
    lg = jnp.log(1.0 - 2.0 ** (-5.0 - jnp.arange(RET_HEADS, dtype=F32)))
    ret_args = (lg, rq.reshape(bsz, t, qk), rk.reshape(bsz, t, qk), p.reshape(bsz, t, width))
    o, states = _ret_fwd(*ret_args, "ret_fwd", dv=2 * dk, v_block0=qk // dk)
    gate_args = (o.reshape(r, 2 * qk), _Cols(p, 2 * qk, 2))
    (y,), _ = _make_rowwise(_ret_out_f, "ret_out", 2, 0, 0).fwd_call(*gate_args)
    return y, (rope_args, ret_args + (states,), gate_args)


def _odd_mixer_bwd(res, dy):
    rope_args, ret_args, gate_args = res
    bsz, t, qk = ret_args[1].shape
    dk = qk // RET_HEADS
    (do, dg), _ = _make_rowwise(_ret_out_f, "ret_out", 2, 0, 0).bwd_call(
        gate_args, ((dy,), ()), row_dtypes=(F32, BF16))
    drq, drk, dv = _ret_bwd(*ret_args, do.reshape(bsz, t, 2 * qk), "ret_bwd", v_block0=qk // dk, dv_dtype=BF16)
    (dq, dkk), _ = _make_rowwise(_rope_ret_f, "rope_ret", 2, 2, 0).bwd_call(
        rope_args, ((drq.reshape(bsz * t, qk), drk.reshape(bsz * t, qk)), ()), row_dtypes=(BF16, BF16))
    return jnp.concatenate([dq, dkk, dv.reshape(bsz * t, 2 * qk), dg], axis=1)


def _local_loss(h, target):
    bsz, seq, d = target.shape
    t = _padded_len(seq)
    t_real = N_META + seq
    pos = jnp.arange(t, dtype=jnp.int32)
    mask = jnp.tile(((pos >= N_META) & (pos < t_real)).astype(F32)[:, None], (bsz, 1))
    tgt = jnp.concatenate(
        [jnp.zeros((bsz, N_META, d), F32), target, jnp.zeros((bsz, t - t_real, d), F32)], axis=1).reshape(bsz * t, d)
    _, (total,) = _make_rowwise(_loss_f, "loss", 1, 2, 0)((h,), (tgt, mask), ())
    return total[0, 0]


_WEIGHTS = ("meta_tokens", "ev_w_in", "ev_conv_w", "ev_conv_b", "ev_w_rg_a", "ev_b_rg_a", "ev_w_rg_x", "ev_b_rg_x",
            "ev_lru_lambda", "ev_q_norm_g", "ev_w_uq", "ev_kv_norm_g", "ev_w_ukv", "ev_w_out", "od_w_in", "od_w_out",
            "ln_mix_g", "ln_mix_b", "mlp_w1", "mlp_w2", "ln_mlp_g", "ln_mlp_b")


def kernel(x, meta_tokens, ev_w_in, ev_conv_w, ev_conv_b, ev_w_rg_a, ev_b_rg_a, ev_w_rg_x, ev_b_rg_x, ev_lru_lambda, ev_q_norm_g, ev_w_uq, ev_kv_norm_g, ev_w_ukv, ev_w_out, od_w_in, od_w_out, ln_mix_g, ln_mix_b, mlp_w1, mlp_w2, ln_mlp_g, ln_mlp_b, loss_target, m_meta_tokens, m_ev_w_in, m_ev_conv_w, m_ev_conv_b, m_ev_w_rg_a, m_ev_b_rg_a, m_ev_w_rg_x, m_ev_b_rg_x, m_ev_lru_lambda, m_ev_q_norm_g, m_ev_w_uq, m_ev_kv_norm_g, m_ev_w_ukv, m_ev_w_out, m_od_w_in, m_od_w_out, m_ln_mix_g, m_ln_mix_b, m_mlp_w1, m_mlp_w2, m_ln_mlp_g, m_ln_mlp_b, v_meta_tokens, v_ev_w_in, v_ev_conv_w, v_ev_conv_b, v_ev_w_rg_a, v_ev_b_rg_a, v_ev_w_rg_x, v_ev_b_rg_x, v_ev_lru_lambda, v_ev_q_norm_g, v_ev_w_uq, v_ev_kv_norm_g, v_ev_w_ukv, v_ev_w_out, v_od_w_in, v_od_w_out, v_ln_mix_g, v_ln_mix_b, v_mlp_w1, v_mlp_w2, v_ln_mlp_g, v_ln_mlp_b):
    args = locals()
    weights = {n: args[n] for n in _WEIGHTS}
    bsz = x.shape[0]
    my_x, my_y, my_c = _my_place()
    me = 4 * my_x + 2 * my_y + my_c

    big = (("ev_in", ev_w_in[0], True), ("ev_out", ev_w_out[0], False), ("mlp0_w1", mlp_w1[0], True),
           ("mlp0_w2", mlp_w2[0], False), ("od_in", od_w_in[0], True), ("od_out", od_w_out[0], False),
           ("mlp1_w1", mlp_w1[1], True), ("mlp1_w2", mlp_w2[1], False))
    small_sharded = (("meta", meta_tokens, F32), ("conv_w", ev_conv_w[0], F32), ("ev_uq", ev_w_uq[0], BF16),
                     ("ev_ukv", ev_w_ukv[0], BF16))
    to_gather = (tuple((nm, s.astype(dt), True) for nm, s, dt in small_sharded)
                 + tuple((nm, s.astype(BF16), cols) for nm, s, cols in big))
    handles = _gather_start_all([s for _, s, _ in to_gather], "ag_start")
    gathers = {nm: (s, cols, h) for (nm, s, cols), h in zip(to_gather, handles)}
    gather_tokens = (handles[0][4],)

    def full_weight(nm, after):
        shard16, cols, handle = gathers[nm]
        land = _exchange_wait(handle, True, after, "ag_wait_" + nm)
        land = lax.dynamic_update_index_in_dim(land, shard16, me, 0)
        if cols and shard16.shape[1] % LANES == 0:
            return land, True
        return (_unstack_cols(land) if cols else land.reshape(-1, shard16.shape[1])), False

    meta_full, conv_w_full, w_uq_full, w_ukv_full = (
        _unstack_cols(lax.dynamic_update_index_in_dim(
            _exchange_wait(gathers[nm][2], True, gather_tokens[-1], "ag_wait_" + nm), gathers[nm][0], me, 0))
        for nm, _, _ in small_sharded)

    pending = []

    def linear_bwd(nm, x_in, w, dy, cols, unpad=None, **fused):
        w_full, w_stacked = w
        if w_stacked:
            stacked = _matmul(x_in, dy, "tn", nm + "_dw", stacked=True)
            own = lax.dynamic_index_in_dim(stacked, me, 0, keepdims=False)
        else:
            dw = _matmul(x_in, dy, "tn", nm + "_dw")
            dw = dw if unpad is None else unpad(dw)
            n = dw.shape[1] // N_DEV
            if cols:
                stacked = _stack_cols(dw)
                own = lax.dynamic_slice_in_dim(dw, me * n, n, axis=1)
            else:
                stacked = dw.reshape(N_DEV, dw.shape[0] // N_DEV, dw.shape[1])
                own = lax.dynamic_index_in_dim(stacked, me, 0, keepdims=False)
        handle = _exchange_start(stacked, (_N_PEERS,) + stacked.shape[1:], False, "rs_start_" + nm)
        pending.append((nm, own, handle))
        return _matmul(dy, w_full, "nt", nm + "_dx", after=(handle[4],), stacked=w_stacked, **fused)

    def linear_fwd(nm, x_in, w, **fused):
        return _matmul(x_in, w[0], "nn", nm + "_fwd", stacked=w[1], **fused)

    def mlp_fwd(h, h16, l):
        w1 = full_weight(f"mlp{l}_w1", h16)
        a16 = linear_fwd(f"mlp{l}_w1", h16, w1, out_relu2=True, out_dtype=BF16)
        w2 = full_weight(f"mlp{l}_w2", a16)
        f = linear_fwd(f"mlp{l}_w2", a16, w2)
        ln_args = (h, f, ln_mlp_g[l:l + 1], ln_mlp_b[l:l + 1])
        return ln_fwd(f"mlp{l}_ln", *ln_args), (h16, w1, a16, w2, ln_args)

    def mlp_bwd(dout, res, l):
        h16, w1, a16, w2, ln_args = res
        dh, df, dg, db = ln_bwd(f"mlp{l}_ln", ln_args, dout)
        du = linear_bwd(f"mlp{l}_w2", a16, w2, df, False, relu2_bwd_of=a16, out_dtype=BF16)
        return (dh, linear_bwd(f"mlp{l}_w1", h16, w1, du, True)), dg, db

    def ln_fwd(nm, h, mix, g, b):
        return _make_rowwise(_ln_res_copy_f, nm, 2, 0, 2).fwd_call(h, mix, g, b)[0]

    def ln_bwd(nm, ln_args, pieces):
        (dh, dmix), (dg, db) = _make_rowwise(_ln_res_f, nm, 2, 0, 2).bwd_call(
            ln_args, ((pieces[0],), ()), more=tuple(pieces[1:]), row_dtypes=(F32, BF16))
        return dh, dmix, dg, db

    h0, vjp_embed = jax.vjp(_embed, meta_full, x)
    n_in = ev_w_in.shape[2] * N_DEV
    kpe0, pad_lo, pad_hi = n_in - MLA_ROPE, MLA_NOPE, LANES - MLA_NOPE - MLA_ROPE
    w_in = full_weight("ev_in", h0)[0]
    zeros_in = jnp.zeros((w_in.shape[0], pad_lo), BF16)
    w_ev_in = (jnp.concatenate([w_in[:, :kpe0], zeros_in, w_in[:, kpe0:], zeros_in[:, :pad_hi]], axis=1), False)

    def unpad_in(dw):
        return jnp.concatenate([dw[:, :kpe0], dw[:, kpe0 + pad_lo:kpe0 + pad_lo + MLA_ROPE]], axis=1)

    p0 = _matmul(h0, w_ev_in[0], "nn", "ev_in_fwd", after=gather_tokens)
    small = (ev_conv_w, ev_conv_b, ev_w_rg_a, ev_b_rg_a, ev_w_rg_x, ev_b_rg_x, ev_lru_lambda, ev_q_norm_g, ev_w_uq,
             ev_kv_norm_g, ev_w_ukv)
    y0, vjp_even = jax.vjp(lambda p, *s: _even_mixer(p, *s, (conv_w_full, w_uq_full, w_ukv_full), bsz), p0, *small)
    w_out = full_weight("ev_out", y0)[0]
    lru_w, d_model = y0.shape[1] - MLA_HEADS * LANES, w_out.shape[1]
    w_att = w_out[lru_w:].reshape(MLA_HEADS, MLA_V, d_model)
    w_att = jnp.concatenate([jnp.zeros((MLA_HEADS, LANES - MLA_V, d_model), BF16), w_att], axis=1)
    w_ev_out = (jnp.concatenate([w_out[:lru_w], w_att.reshape(MLA_HEADS * LANES, d_model)], axis=0), False)

    def unpad_out(dw):
        d_att = dw[lru_w:].reshape(MLA_HEADS, LANES, d_model)[:, LANES - MLA_V:].reshape(MLA_HEADS * MLA_V, d_model)
        return jnp.concatenate([dw[:lru_w], d_att], axis=0)

    mix0 = linear_fwd("ev_out", y0, w_ev_out)
    ln0_args = (h0, mix0, ln_mix_g[0:1], ln_mix_b[0:1])
    h1, h1_16 = ln_fwd("mix0_ln", *ln0_args)
    (h2, h2_16), res_mlp0 = mlp_fwd(h1, h1_16, 0)
    w_od_in = full_weight("od_in", h2_16)
    p1 = linear_fwd("od_in", h2_16, w_od_in)
    y1, res_odd = _odd_mixer_fwd(p1, bsz)
    w_od_out = full_weight("od_out", y1)
    mix1 = linear_fwd("od_out", y1, w_od_out)
    ln1_args = (h2, mix1, ln_mix_g[1:2], ln_mix_b[1:2])
    h3, h3_16 = ln_fwd("mix1_ln", *ln1_args)
    (h4, _), res_mlp1 = mlp_fwd(h3, h3_16, 1)
    loss_local, vjp_loss = jax.vjp(lambda h: _local_loss(h, loss_target), h4)

    dh4 = vjp_loss(jnp.ones((), F32))
    dh3, dg_mlp1, db_mlp1 = mlp_bwd(dh4, res_mlp1, 1)
    dh2, dmix1, dg_mix1, db_mix1 = ln_bwd("mix1_ln", ln1_args, dh3)
    dp1 = _odd_mixer_bwd(res_odd, linear_bwd("od_out", y1, w_od_out, dmix1, False))
    dh2 = (dh2, linear_bwd("od_in", h2_16, w_od_in, dp1, True))
    dh1, dg_mlp0, db_mlp0 = mlp_bwd(dh2, res_mlp0, 0)
    dh0, dmix0, dg_mix0, db_mix0 = ln_bwd("mix0_ln", ln0_args, dh1)
    dp0, *dsmall = vjp_even(linear_bwd("ev_out", y0, w_ev_out, dmix0, False, unpad=unpad_out))
    dh0 = dh0 + linear_bwd("ev_in", h0, w_ev_in, dp0.astype(BF16), True, unpad=unpad_in)
    g_meta_full, grad_x = vjp_embed(dh0)
    (g_conv_w, g_conv_b, g_w_rg_a, g_b_rg_a, g_w_rg_x, g_b_rg_x, g_lambda, g_q_norm, g_w_uq, g_kv_norm, g_w_ukv) = dsmall

    after, summed = grad_x, {}
    for nm, own, handle in pending:
        land = _exchange_wait(handle, False, after, "rs_wait_" + nm)
        summed[nm] = after = _sum_own_and_peers(own, land, "rs_sum_" + nm)

    g_meta = _scatter_grad(g_meta_full, True, "meta_rs", after=(after,))
    rep_names = ("ev_conv_b", "ev_w_rg_a", "ev_b_rg_a", "ev_w_rg_x", "ev_b_rg_x", "ev_lru_lambda", "ev_q_norm_g",
                 "ev_kv_norm_g", "ln_mix_g", "ln_mix_b", "ln_mlp_g", "ln_mlp_b")
    rep_local = (g_conv_b, g_w_rg_a, g_b_rg_a, g_w_rg_x, g_b_rg_x, g_lambda, g_q_norm, g_kv_norm,
                 jnp.concatenate([dg_mix0, dg_mix1]), jnp.concatenate([db_mix0, db_mix1]),
                 jnp.concatenate([dg_mlp0, dg_mlp1]), jnp.concatenate([db_mlp0, db_mlp1]))
    grad_w = dict(zip(rep_names, _allreduce_replicated(rep_local, "rep", after=(g_meta,))))
    grad_w.update(meta_tokens=g_meta, ev_conv_w=g_conv_w, ev_w_uq=g_w_uq, ev_w_ukv=g_w_ukv)
    grad_w.update(ev_w_in=summed["ev_in"][None], ev_w_out=summed["ev_out"][None], od_w_in=summed["od_in"][None],
                  od_w_out=summed["od_out"][None], mlp_w1=jnp.stack([summed["mlp0_w1"], summed["mlp1_w1"]]),
                  mlp_w2=jnp.stack([summed["mlp0_w2"], summed["mlp1_w2"]]))

    loss = lax.psum(loss_local, MESH_AXES)
    delta, new_m, new_v = {}, {}, {}
    for n in _WEIGHTS:
        w, g, m, v = weights[n], grad_w[n], args["m_" + n], args["v_" + n]
        two_d = (-1, w.shape[-1])
        d2, m2, v2 = _adamw(w.reshape(two_d), g.reshape(two_d), m.reshape(two_d), v.reshape(two_d), "adamw_" + n)
        delta[n], new_m[n], new_v[n] = d2.reshape(w.shape), m2.reshape(w.shape), v2.reshape(w.shape)
    return (loss, grad_x, *[grad_w[n] for n in _WEIGHTS], *[delta[n] for n in _WEIGHTS],
            *[new_m[n] for n in _WEIGHTS], *[new_v[n] for n in _WEIGHTS])
```

```python
import functools
import math

import jax
import jax.numpy as jnp
from jax import lax
from jax.experimental import pallas as pl
from jax.experimental.pallas import tpu as pltpu

F32 = jnp.float32
BF16 = jnp.bfloat16

N_DEV = 8
MESH_AXES = ("x", "y", "c")
LANES = 128
SEQ_BLOCK = 128

N_META = 16
LRU_C = 8.0
MLA_HEADS = 8
MLA_NOPE = 64
MLA_ROPE = 32
MLA_V = 64
RET_HEADS = 4
ROPE_BASE = 10000.0
DEPTH = 2
DN_ALPHA = (2 * DEPTH) ** 0.25
EPS = 1e-5
NEG_INF = -1e30

ADAM_LR = 0.001
ADAM_B1 = 0.9
ADAM_B2 = 0.999
ADAM_EPS = 1e-08
ADAM_WD = 0.01
ADAM_STEP = 10

VMEM_LIMIT = 56 * 1024 * 1024


def _params(*sem):
    return pltpu.CompilerParams(dimension_semantics=sem, vmem_limit_bytes=VMEM_LIMIT)


def _pick(n, cands):
    for c in cands:
        if n % c == 0:
            return c
    return n


def _row_tile(r, width):
    cands = (256, 128, 64, 32, 16, 8) if width <= 1024 else (128, 64, 32, 16, 8)
    return _pick(r, cands)


_DIMS = {"nn": (((1,), (0,)), ((), ())), "nt": (((1,), (1,)), ((), ())), "tn": (((0,), (0,)), ((), ()))}


def _dot(a, b, mode):
    return lax.dot_general(a.astype(BF16), b.astype(BF16), _DIMS[mode], preferred_element_type=F32)


def _matmul(a, b, mode, name, after=(), stacked=False, relu2_bwd_of=None, out_dtype=F32, out_relu2=False):
    if stacked:
        n_blk = b.shape[2] if mode != "tn" else b.shape[1] // N_DEV
    if mode == "nn":
        (m, k), n = a.shape, (N_DEV * n_blk if stacked else b.shape[1])
    elif mode == "nt":
        (m, k), n = a.shape, (b.shape[1] if stacked else b.shape[0])
    else:
        (k, m), n = a.shape, b.shape[1]
    tm = _pick(m, (1088, 1024, 544, 512, 272, 256, 128, 64, 32, 16, 8))
    tn = _pick(n, (1024, 512, 256, 128))
    tk = _pick(k, (1088, 1024, 544, 512, 272, 256, 128))
    kb = 2
    if stacked and mode == "nn":
        tn = n_blk
    if stacked and mode == "tn":
        tn = kb * n_blk
    if stacked and mode == "nt":
        tk = kb * n_blk
    nk = k // tk
    assert out_dtype == F32 or (nk == 1 and not (stacked and mode == "tn")), "narrow results need a single k step"
    assert not out_relu2 or nk == 1, "relu^2 is applied to a finished tile"

    out_spec = pl.BlockSpec((tm, tn), lambda i, j, kk: (i, j))
    out_shape = jax.ShapeDtypeStruct((m, n), out_dtype)
    if mode == "nn":
        a_spec = pl.BlockSpec((tm, tk), lambda i, j, kk: (i, kk))
        b_spec = pl.BlockSpec((tk, tn), lambda i, j, kk: (kk, j))
        if stacked:
            b_spec = pl.BlockSpec((None, tk, tn), lambda i, j, kk: (j, kk, 0))
    elif mode == "nt":
        a_spec = pl.BlockSpec((tm, tk), lambda i, j, kk: (i, kk))
        b_spec = pl.BlockSpec((tn, tk), lambda i, j, kk: (j, kk))
        if stacked:
            b_spec = pl.BlockSpec((kb, tn, n_blk), lambda i, j, kk: (kk, j, 0))
    else:
        a_spec = pl.BlockSpec((tk, tm), lambda i, j, kk: (kk, i))
        b_spec = pl.BlockSpec((tk, tn), lambda i, j, kk: (kk, j))
        if stacked:
            out_spec = pl.BlockSpec((kb, tm, n_blk), lambda i, j, kk: (j, i, 0))
            out_shape = jax.ShapeDtypeStruct((N_DEV, m, n_blk), F32)
    extra = [] if relu2_bwd_of is None else [relu2_bwd_of]
    extra_specs = [pl.BlockSpec((tm, tn), lambda i, j, kk: (i, j))] * len(extra)

    def body(a_ref, b_ref, *rest):
        def relu2_slope():
            return 2.0 * jnp.sqrt(rest[0][...].astype(F32))

        o_ref = rest[-1]
        kk = pl.program_id(2)
        av = a_ref[...]
        if stacked and mode == "nt":
            part = _dot(av[:, :n_blk], b_ref[0], mode)
            for q in range(1, kb):
                part = part + _dot(av[:, q * n_blk:(q + 1) * n_blk], b_ref[q], mode)
        else:
            part = _dot(av, b_ref[...], mode)
        if stacked and mode == "tn":
            part = jnp.stack([part[:, q * n_blk:(q + 1) * n_blk] for q in range(kb)])
        if nk == 1:
            if out_relu2:
                part = jnp.maximum(part, 0.0)
                part = part * part
            if relu2_bwd_of is not None:
                part = part * relu2_slope()
            o_ref[...] = part.astype(out_dtype)
            return

        @pl.when(kk == 0)
        def _():
            o_ref[...] = part

        @pl.when(kk != 0)
        def _():
            o_ref[...] += part

        if relu2_bwd_of is not None:
            @pl.when(kk == nk - 1)
            def _():
                o_ref[...] *= relu2_slope()

    return pl.pallas_call(
        body,
        name=name,
        grid=(m // tm, n // tn, nk),
        in_specs=[a_spec, b_spec] + extra_specs + [pl.BlockSpec(memory_space=pl.ANY)] * len(after),
        out_specs=out_spec,
        out_shape=out_shape,
        compiler_params=_params("parallel", "parallel", "arbitrary"),
    )(a, b, *extra, *after)


def _group_matmul(a, w, mode, name):
    if mode in ("nn", "nt"):
        g, dk, dn = w.shape
        m = a.shape[0]
        d_in, d_out = (dk, dn) if mode == "nn" else (dn, dk)
        tm = _pick(m, (1088, 1024, 544, 512, 272, 256, 128, 64, 32, 16, 8))

        def body(a_ref, w_ref, o_ref):
            o_ref[...] = _dot(a_ref[...], w_ref[0], mode)

        return pl.pallas_call(
            body,
            name=name,
            grid=(g, m // tm),
            in_specs=[pl.BlockSpec((tm, d_in), lambda h, i: (i, h)), pl.BlockSpec((1, dk, dn), lambda h, i: (h, 0, 0))],
            out_specs=pl.BlockSpec((tm, d_out), lambda h, i: (i, h)),
            out_shape=jax.ShapeDtypeStruct((m, g * d_out), F32),
            compiler_params=_params("parallel", "parallel"),
        )(a, w)
    b = w
    m = a.shape[0]
    dk = dn = LANES
    g = a.shape[1] // dk
    tm = _pick(m, (1088, 1024, 544, 512, 272, 256, 128, 64, 32, 16, 8))

    def body(a_ref, b_ref, o_ref):
        part = _dot(a_ref[...], b_ref[...], "tn")

        @pl.when(pl.program_id(1) == 0)
        def _():
            o_ref[0] = part

        @pl.when(pl.program_id(1) != 0)
        def _():
            o_ref[0] += part

    return pl.pallas_call(
        body,
        name=name,
        grid=(g, m // tm),
        in_specs=[pl.BlockSpec((tm, dk), lambda h, i: (i, h)), pl.BlockSpec((tm, dn), lambda h, i: (i, h))],
        out_specs=pl.BlockSpec((1, dk, dn), lambda h, i: (h, 0, 0)),
        out_shape=jax.ShapeDtypeStruct((g, dk, dn), F32),
        compiler_params=_params("parallel", "arbitrary"),
    )(a, b)


def _make_group_linear(name):
    @jax.custom_vjp
    def op(x, w):
        return _group_matmul(x, w, "nn", name + "_fwd")

    def fwd(x, w):
        return op(x, w), (x, w)

    def bwd(res, dy):
        x, w = res
        return _group_matmul(dy, w, "nt", name + "_dx"), _group_matmul(x, dy, "tn", name + "_dw")

    op.defvjp(fwd, bwd)
    return op


def _my_place():
    return lax.axis_index("x"), lax.axis_index("y"), lax.axis_index("c")


def _all_gather(shard, name, after=()):
    shape, dtype = shard.shape, shard.dtype

    def body(x_ref, *rest):
        out_ref, send_sems, recv_sems, local_sem = rest[len(after):]
        x, y, c = _my_place()
        me, sibling = (x, y, c), (x, y, 1 - c)
        chips = [(1 - x, y), (x, 1 - y), (1 - x, 1 - y)]

        def slot(px, py, pc):
            return out_ref.at[4 * px + 2 * py + pc]

        def copy(k, block, to, src=None):
            return pltpu.make_async_remote_copy(
                src_ref=slot(*block) if src is None else src,
                dst_ref=slot(*block),
                send_sem=send_sems.at[k],
                recv_sem=recv_sems.at[k],
                device_id=to,
                device_id_type=pl.DeviceIdType.MESH,
            )

        mine = pltpu.make_async_copy(x_ref, slot(*me), local_sem)
        mine.start()
        first = [copy(0, me, sibling, src=x_ref)]
        first += [copy(1 + j, me, (*chip, c), src=x_ref) for j, chip in enumerate(chips)]
        for cp in first:
            cp.start()
        passed = [copy(4 + j, (*chip, c), sibling) for j, chip in enumerate(chips)]
        for j, chip in enumerate(chips):
            copy(1 + j, (*chip, c), me).wait_recv()
            passed[j].start()
        copy(0, sibling, me).wait_recv()
        for j, chip in enumerate(chips):
            copy(4 + j, (*chip, 1 - c), me).wait_recv()
        for cp in first + passed:
            cp.wait_send()
        mine.wait()

    return pl.pallas_call(
        body,
        name=name,
        out_shape=jax.ShapeDtypeStruct((N_DEV,) + shape, dtype),
        in_specs=[pl.BlockSpec(memory_space=pl.ANY)] * (1 + len(after)),
        out_specs=pl.BlockSpec(memory_space=pl.ANY),
        scratch_shapes=[pltpu.SemaphoreType.DMA((7,)), pltpu.SemaphoreType.DMA((7,)), pltpu.SemaphoreType.DMA],
    )(shard, *after)


def _all_to_all(stacked, name, after=()):
    def body(x_ref, *rest):
        out_ref, send_sems, recv_sems, local_sem = rest[len(after):]
        x, y, c = _my_place()
        me = 4 * x + 2 * y + c
        mine = pltpu.make_async_copy(x_ref.at[me], out_ref.at[me], local_sem)
        mine.start()
        copies = []
        for k in range(1, N_DEV):
            px, py, pc = x ^ ((k >> 2) & 1), y ^ ((k >> 1) & 1), c ^ (k & 1)
            peer = 4 * px + 2 * py + pc
            copies.append(
                pltpu.make_async_remote_copy(
                    src_ref=x_ref.at[peer],
                    dst_ref=out_ref.at[me],
                    send_sem=send_sems.at[k - 1],
                    recv_sem=recv_sems.at[k - 1],
                    device_id=(px, py, pc),
                    device_id_type=pl.DeviceIdType.MESH,
                )
            )
        for cp in copies:
            cp.start()
        for cp in copies:
            cp.wait_recv()
        for cp in copies:
            cp.wait_send()
        mine.wait()

    return pl.pallas_call(
        body,
        name=name,
        out_shape=jax.ShapeDtypeStruct(stacked.shape, stacked.dtype),
        in_specs=[pl.BlockSpec(memory_space=pl.ANY)] * (1 + len(after)),
        out_specs=pl.BlockSpec(memory_space=pl.ANY),
        scratch_shapes=[pltpu.SemaphoreType.DMA((7,)), pltpu.SemaphoreType.DMA((7,)), pltpu.SemaphoreType.DMA],
    )(stacked, *after)


def _sum_blocks(stacked, name):
    _, r, c = stacked.shape
    tr = _pick(r, (256, 128, 64, 32, 16, 8))

    def body(x_ref, o_ref):
        s = [x_ref[j] for j in range(N_DEV)]
        o_ref[...] = ((s[0] + s[1]) + (s[2] + s[3])) + ((s[4] + s[5]) + (s[6] + s[7]))

    return pl.pallas_call(
        body,
        name=name,
        grid=(r // tr,),
        in_specs=[pl.BlockSpec((N_DEV, tr, c), lambda i: (0, i, 0))],
        out_specs=pl.BlockSpec((tr, c), lambda i: (i, 0)),
        out_shape=jax.ShapeDtypeStruct((r, c), stacked.dtype),
        compiler_params=_params("parallel"),
    )(stacked)


def _stack_cols(full):
    k, n8 = full.shape
    return full.reshape(k, N_DEV, n8 // N_DEV).transpose(1, 0, 2)


def _unstack_cols(stacked):
    j, k, n = stacked.shape
    return stacked.transpose(1, 0, 2).reshape(k, j * n)


def _split_cols(p, cuts):
    bounds = (0,) + tuple(cuts) + (p.shape[1],)

    @jax.custom_vjp
    def op(z):
        return tuple(z[:, lo:hi] for lo, hi in zip(bounds[:-1], bounds[1:]))

    op.defvjp(lambda z: (op(z), None), lambda _, cots: (jnp.concatenate(cots, axis=1),))
    return op(p)


def _gather_weight(shard, cols, name):
    g = _all_gather(shard.astype(BF16), name)
    return _unstack_cols(g) if cols else g.reshape(-1, shard.shape[1])


def _scatter_grad(full, cols, name, after=()):
    if cols:
        st = _stack_cols(full)
    else:
        st = full.reshape(N_DEV, full.shape[0] // N_DEV, full.shape[1])
    return _sum_blocks(_all_to_all(st, name + "_a2a", after), name + "_sum")


def _make_fsdp_linear(cols, name, unpad=None):
    @jax.custom_vjp
    def op(x, w_shard, w_full):
        return _matmul(x, w_full, "nn", name + "_fwd")

    def fwd(x, w_shard, w_full):
        return op(x, w_shard, w_full), (x, w_full)

    def bwd(res, dy):
        x, w = res
        dx = _matmul(dy, w, "nt", name + "_dx")
        dw = _matmul(x, dy, "tn", name + "_dw")
        dw = dw if unpad is None else unpad(dw)
        return dx, _scatter_grad(dw, cols, name + "_rs"), jnp.zeros_like(w)

    op.defvjp(fwd, bwd)
    return op


def _make_fsdp_param(name):
    @jax.custom_vjp
    def op(shard, full):
        return full

    def fwd(shard, full):
        return full, None

    def bwd(_, g):
        return _scatter_grad(g, True, name + "_rs"), jnp.zeros_like(g)

    op.defvjp(fwd, bwd)
    return op


def _pack_rows(gs):
    flat = jnp.concatenate([g.reshape(-1) for g in gs])
    n = flat.shape[0]
    rows = -(-n // (256 * LANES)) * 256
    return jnp.pad(flat, (0, rows * LANES - n)).reshape(rows, LANES)


def _unpack_rows(packed, like):
    flat, out, off = packed.reshape(-1), [], 0
    for g in like:
        out.append(flat[off:off + g.size].reshape(g.shape))
        off += g.size
    return out


_HBM = pl.BlockSpec(memory_space=pltpu.HBM)
_SEM = pl.BlockSpec(memory_space=pltpu.SEMAPHORE)
_SIDE_EFFECT = pltpu.SideEffectType.DATAFLOW_SIDE_EFFECTING
_N_PEERS = N_DEV - 1


def _peer(k):
    x, y, c = _my_place()
    return x ^ ((k >> 2) & 1), y ^ ((k >> 1) & 1), c ^ (k & 1)


def _exchange_start(src, land_shape, gather, name, after=()):
    def body(src_ref, land_ref, *rest):
        send_sems, recv_sems, src_thru, land_thru, token = rest[len(after):]
        x, y, c = _my_place()
        me = 4 * x + 2 * y + c
        for k in range(1, N_DEV):
            px, py, pc = _peer(k)
            pltpu.make_async_remote_copy(
                src_ref=src_ref if gather else src_ref.at[4 * px + 2 * py + pc],
                dst_ref=land_ref.at[me] if gather else land_ref.at[k - 1],
                send_sem=send_sems.at[k - 1],
                recv_sem=recv_sems.at[k - 1],
                device_id=(px, py, pc),
                device_id_type=pl.DeviceIdType.MESH,
            ).start()
        token[...] = jnp.zeros_like(token)

    return pl.pallas_call(
        body,
        name=name,
        out_shape=(
            pltpu.SemaphoreType.DMA((_N_PEERS,)),
            pltpu.SemaphoreType.DMA((_N_PEERS,)),
            pltpu.HBM(src.shape, src.dtype),
            pltpu.HBM(land_shape, src.dtype),
            jax.ShapeDtypeStruct((8, LANES), F32),
        ),
        in_specs=(_HBM, _HBM) + (pl.BlockSpec(memory_space=pl.ANY),) * len(after),
        out_specs=(_SEM, _SEM, _HBM, _HBM, pl.BlockSpec(memory_space=pltpu.VMEM)),
        input_output_aliases={0: 2, 1: 3},
        compiler_params=pltpu.CompilerParams(has_side_effects=_SIDE_EFFECT),
    )(pltpu.with_memory_space_constraint(src, pltpu.HBM),
      pltpu.with_memory_space_constraint(lax.empty(land_shape, src.dtype), pltpu.HBM), *after)


def _gather_start_all(shards, name):
    n = len(shards)

    def body(*refs):
        srcs, lands = refs[:n], refs[n:2 * n]
        outs = refs[2 * n:]
        send_sems, recv_sems, token = outs[:n], outs[n:2 * n], outs[-1]
        x, y, c = _my_place()
        me = 4 * x + 2 * y + c
        for i in range(n):
            for k in range(1, N_DEV):
                pltpu.make_async_remote_copy(
                    src_ref=srcs[i],
                    dst_ref=lands[i].at[me],
                    send_sem=send_sems[i].at[k - 1],
                    recv_sem=recv_sems[i].at[k - 1],
                    device_id=_peer(k),
                    device_id_type=pl.DeviceIdType.MESH,
                ).start()
        token[...] = jnp.zeros_like(token)

    lands = [(N_DEV,) + s.shape for s in shards]
    sems = tuple(pltpu.SemaphoreType.DMA((_N_PEERS,)) for _ in range(2 * n))
    res = pl.pallas_call(
        body,
        name=name,
        out_shape=sems + tuple(pltpu.HBM(s.shape, s.dtype) for s in shards)
        + tuple(pltpu.HBM(ls, s.dtype) for ls, s in zip(lands, shards)) + (jax.ShapeDtypeStruct((8, LANES), F32),),
        in_specs=(_HBM,) * (2 * n),
        out_specs=(_SEM,) * (2 * n) + (_HBM,) * (2 * n) + (pl.BlockSpec(memory_space=pltpu.VMEM),),
        input_output_aliases={i: 2 * n + i for i in range(2 * n)},
        compiler_params=pltpu.CompilerParams(has_side_effects=_SIDE_EFFECT),
    )(*[pltpu.with_memory_space_constraint(s, pltpu.HBM) for s in shards],
      *[pltpu.with_memory_space_constraint(lax.empty(ls, s.dtype), pltpu.HBM) for ls, s in zip(lands, shards)])
    return [(res[i], res[n + i], res[2 * n + i], res[3 * n + i], res[-1]) for i in range(n)]


def _exchange_wait(handle, gather, after, name):
    send_sems, recv_sems, src_thru, land_thru, _ = handle

    def body(src_ref, land_ref, send_sems, recv_sems, after_ref, src_dead, got_ref):
        for k in range(1, N_DEV):
            cp = pltpu.make_async_remote_copy(
                src_ref=src_ref if gather else src_ref.at[k],
                dst_ref=land_ref.at[k - 1],
                send_sem=send_sems.at[k - 1],
                recv_sem=recv_sems.at[k - 1],
                device_id=_peer(k),
                device_id_type=pl.DeviceIdType.MESH,
            )
            cp.wait_send()
            cp.wait_recv()

    return pl.pallas_call(
        body,
        name=name,
        out_shape=(pltpu.HBM(src_thru.shape, src_thru.dtype), pltpu.HBM(land_thru.shape, land_thru.dtype)),
        in_specs=(_HBM, _HBM, _SEM, _SEM, pl.BlockSpec(memory_space=pl.ANY)),
        out_specs=(_HBM, _HBM),
        input_output_aliases={0: 0, 1: 1},
        compiler_params=pltpu.CompilerParams(has_side_effects=_SIDE_EFFECT),
    )(src_thru, land_thru, send_sems, recv_sems, after)[1]


def _sum_own_and_peers(own, land, name):
    r, c = own.shape
    tr = _pick(r, (256, 128, 64, 32, 16, 8))

    def body(o_ref, l_ref, out_ref):
        s = [l_ref[j] for j in range(_N_PEERS)]
        out_ref[...] = ((o_ref[...] + s[0]) + (s[1] + s[2])) + ((s[3] + s[4]) + (s[5] + s[6]))

    return pl.pallas_call(
        body,
        name=name,
        grid=(r // tr,),
        in_specs=[pl.BlockSpec((tr, c), lambda i: (i, 0)), pl.BlockSpec((_N_PEERS, tr, c), lambda i: (0, i, 0))],
        out_specs=pl.BlockSpec((tr, c), lambda i: (i, 0)),
        out_shape=jax.ShapeDtypeStruct((r, c), own.dtype),
        compiler_params=_params("parallel"),
    )(own, land)


class _Cols:
    def __init__(self, array, width, block):
        self.array, self.width, self.block = array, width, block
        self.shape, self.dtype = (array.shape[0], width), array.dtype


def _base(a):
    return a.array if isinstance(a, _Cols) else a


def _col_block(a):
    return a.block if isinstance(a, _Cols) else 0


def _make_rowwise(f, name, n_rows, n_tabs, n_pars):
    n_in = n_rows + n_tabs + n_pars

    def specs(args, tm):
        blocked = [pl.BlockSpec((tm, a.shape[1]), lambda i, blk=_col_block(a): (i, blk)) for a in args[: n_rows + n_tabs]]
        whole = [pl.BlockSpec(a.shape, lambda i: (0, 0)) for a in args[n_rows + n_tabs:]]
        return blocked + whole

    def out_struct(args, tm):
        blk = [jax.ShapeDtypeStruct((tm, a.shape[1]), a.dtype) for a in args[: n_rows + n_tabs]]
        blk += [jax.ShapeDtypeStruct(a.shape, a.dtype) for a in args[n_rows + n_tabs:]]
        return jax.eval_shape(f, *blk)

    def fwd_call(*args):
        r = args[0].shape[0]
        tm = _row_tile(r, max(a.shape[1] for a in args[:n_rows]))
        ro, so = out_struct(args, tm)

        def body(*refs):
            vals = [x[...] for x in refs[:n_in]]
            outs = refs[n_in:]
            rv, sv = f(*vals)
            for o, v in zip(outs[: len(ro)], rv):
                o[...] = v
            for o, v in zip(outs[len(ro):], sv):
                @pl.when(pl.program_id(0) == 0)
                def _(o=o, v=v):
                    o[...] = v

                @pl.when(pl.program_id(0) != 0)
                def _(o=o, v=v):
                    o[...] += v

        out_shape = [jax.ShapeDtypeStruct((r, s.shape[1]), s.dtype) for s in ro]
        out_shape += [jax.ShapeDtypeStruct(s.shape, s.dtype) for s in so]
        out_specs = [pl.BlockSpec((tm, s.shape[1]), lambda i: (i, 0)) for s in ro]
        out_specs += [pl.BlockSpec(s.shape, lambda i: (0, 0)) for s in so]
        res = pl.pallas_call(
            body,
            name=name + "_fwd",
            grid=(r // tm,),
            in_specs=specs(args, tm),
            out_specs=out_specs,
            out_shape=out_shape,
            compiler_params=_params("arbitrary" if so else "parallel"),
        )(*[_base(a) for a in args])
        return tuple(res[: len(ro)]), tuple(res[len(ro):])

    def bwd_call(args, cots, more=(), row_dtypes=None):
        r = args[0].shape[0]
        tm = _row_tile(r, max(a.shape[1] for a in args[:n_rows]))
        ro, so = out_struct(args, tm)
        crow, csum = cots
        rows, tabs, pars = args[:n_rows], args[n_rows:n_rows + n_tabs], args[n_rows + n_tabs:]
        n_c = len(crow) + len(csum)

        def body(*refs):
            vals = [x[...] for x in refs[:n_in]]
            cv = [x[...] for x in refs[n_in:n_in + n_c]]
            for x in refs[n_in + n_c:n_in + n_c + len(more)]:
                cv[0] = cv[0] + x[...]
            outs = refs[n_in + n_c + len(more):]
            tv = vals[n_rows:n_rows + n_tabs]

            def g(*dargs):
                return f(*dargs[:n_rows], *tv, *dargs[n_rows:])

            _, vjp = jax.vjp(g, *vals[:n_rows], *vals[n_rows + n_tabs:])
            d = vjp((tuple(cv[: len(crow)]), tuple(cv[len(crow):])))
            for o, v in zip(outs[:n_rows], d[:n_rows]):
                o[...] = v.astype(o.dtype)
            for o, v in zip(outs[n_rows:], d[n_rows:]):
                @pl.when(pl.program_id(0) == 0)
                def _(o=o, v=v):
                    o[...] = v

                @pl.when(pl.program_id(0) != 0)
                def _(o=o, v=v):
                    o[...] += v

        in_specs = specs(args, tm)
        in_specs += [pl.BlockSpec((tm, c.shape[1]), lambda i: (i, 0)) for c in crow]
        in_specs += [pl.BlockSpec(c.shape, lambda i: (0, 0)) for c in csum]
        in_specs += [pl.BlockSpec((tm, c.shape[1]), lambda i: (i, 0)) for c in more]
        out_shape = [jax.ShapeDtypeStruct(a.shape, dt) for a, dt in zip(rows, row_dtypes or [a.dtype for a in rows])]
        out_shape += [jax.ShapeDtypeStruct(a.shape, a.dtype) for a in pars]
        out_specs = [pl.BlockSpec((tm, a.shape[1]), lambda i: (i, 0)) for a in rows]
        out_specs += [pl.BlockSpec(a.shape, lambda i: (0, 0)) for a in pars]
        res = pl.pallas_call(
            body,
            name=name + "_bwd",
            grid=(r // tm,),
            in_specs=in_specs,
            out_specs=out_specs,
            out_shape=out_shape,
            compiler_params=_params("arbitrary" if pars else "parallel"),
        )(*[_base(a) for a in args], *crow, *csum, *more)
        return tuple(res[:n_rows]), tuple(res[n_rows:])

    @jax.custom_vjp
    def op(rows, tabs, pars):
        return fwd_call(*rows, *tabs, *pars)

    op.fwd_call, op.bwd_call = fwd_call, bwd_call

    def fwd(rows, tabs, pars):
        return fwd_call(*rows, *tabs, *pars), (rows, tabs, pars)

    def bwd(res, cots):
        rows, tabs, pars = res
        drows, dpars = bwd_call(tuple(rows) + tuple(tabs) + tuple(pars), cots)
        return drows, tuple(jnp.zeros_like(t) for t in tabs), dpars

    op.defvjp(fwd, bwd)
    return op


def _sigmoid(x):
    return 0.5 * (jnp.tanh(0.5 * x) + 1.0)


@jax.custom_jvp
def _softplus(x):
    e = jnp.exp(-jnp.abs(x))
    u = 1.0 + e
    log1p_e = jnp.where(u == 1.0, e, e * jnp.log(u) / jnp.where(u == 1.0, 1.0, u - 1.0))
    return jnp.maximum(x, 0.0) + log1p_e


@_softplus.defjvp
def _softplus_jvp(primals, tangents):
    (x,), (t,) = primals, tangents
    return _softplus(x), t * _sigmoid(x)


def _gelu(x):
    return 0.5 * x * (1.0 + jnp.tanh(math.sqrt(2.0 / math.pi) * (x + 0.044715 * (x * x * x))))


def _ln_res_f(h, mix, g, b):
    z = DN_ALPHA * h + mix
    mu = jnp.mean(z, axis=-1, keepdims=True)
    zc = z - mu
    var = jnp.mean(zc * zc, axis=-1, keepdims=True)
    return (zc * lax.rsqrt(var + EPS) * g + b,), ()


def _ln_res_copy_f(h, mix, g, b):
    (out,), _ = _ln_res_f(h, mix, g, b)
    return (out, out.astype(BF16)), ()


def _rmsnorm_f(x, g):
    return (x * lax.rsqrt(jnp.mean(x * x, axis=-1, keepdims=True) + EPS) * g,), ()


def _lru_gates_f(ga, gx, xc, b_a, b_x, lam):
    r = _sigmoid(ga + b_a)
    i = _sigmoid(gx + b_x)
    log_a = -LRU_C * r * _softplus(-lam)
    a = jnp.exp(log_a)
    one_minus_a2 = jnp.tanh(-log_a) * (jnp.exp(2.0 * log_a) + 1.0)
    return (a, jnp.sqrt(one_minus_a2) * (i * xc)), ()


def _lru_out_f(hh, p_gate):
    return (hh * _gelu(p_gate),), ()


def _rope_ret_f(q, k, cos2, sin2):
    d = cos2.shape[1]
    half = d // 2
    k_scale = d ** -0.5

    def rope(x):
        outs = []
        for h in range(x.shape[1] // d):
            xh = x[:, h * d:(h + 1) * d]
            rot = jnp.concatenate([xh[:, half:], xh[:, :half]], axis=1)
            outs.append(xh * cos2 + rot * sin2)
        return jnp.concatenate(outs, axis=1)

    return (rope(q), rope(k) * k_scale), ()


def _ret_out_f(o, g):
    d = o.shape[1] // RET_HEADS
    outs = []
    for h in range(RET_HEADS):
        oh = o[:, h * d:(h + 1) * d]
        outs.append(oh * lax.rsqrt(jnp.mean(oh * oh, axis=-1, keepdims=True) + EPS))
    y = jnp.concatenate(outs, axis=1)
    return (g * _sigmoid(g) * y,), ()


def _ret_out_bf16_f(o, g):
    (y,), _ = _ret_out_f(o, g)
    return (y.astype(BF16),), ()


def _loss_f(y, t, mask):
    e = (y - t) * mask
    per_row = jnp.sum(e * e, axis=-1, keepdims=True) * (0.5 / y.shape[1])
    total = jnp.sum(per_row, axis=0, keepdims=True)
    return (), (jnp.broadcast_to(total, (1, LANES)),)


def _shift_down(x, s):
    if s == 0:
        return x
    t = x.shape[0]
    row = lax.broadcasted_iota(jnp.int32, x.shape, 0)
    return jnp.where(row >= s, pltpu.roll(x, s, 0), 0.0)


def _shift_up(x, s):
    if s == 0:
        return x
    t = x.shape[0]
    row = lax.broadcasted_iota(jnp.int32, x.shape, 0)
    return jnp.where(row < t - s, pltpu.roll(x, t - s, 0), 0.0)


def _conv_fwd(x, w, b, name):
    bsz, t, c = x.shape
    width = w.shape[0]

    def body(x_ref, w_ref, b_ref, y_ref):
        xv = x_ref[0]
        acc = jnp.broadcast_to(b_ref[...], xv.shape)
        for k in range(width):
            acc = acc + w_ref[k:k + 1, :] * _shift_down(xv, width - 1 - k)
        y_ref[0] = acc

    return pl.pallas_call(
        body,
        name=name,
        grid=(bsz, c // LANES),
        in_specs=[
            pl.BlockSpec((1, t, LANES), lambda i, j: (i, 0, j)),
            pl.BlockSpec((width, LANES), lambda i, j: (0, j)),
            pl.BlockSpec((1, LANES), lambda i, j: (0, j)),
        ],
        out_specs=pl.BlockSpec((1, t, LANES), lambda i, j: (i, 0, j)),
        out_shape=jax.ShapeDtypeStruct(x.shape, F32),
        compiler_params=_params("parallel", "parallel"),
    )(x, w, b)


def _conv_bwd(x, w, dy, name):
    bsz, t, c = x.shape
    width = w.shape[0]

    def body(x_ref, w_ref, dy_ref, dx_ref, dw_ref, db_ref):
        xv, g = x_ref[0], dy_ref[0]
        dx = jnp.zeros_like(xv)
        dws = []
        for k in range(width):
            s = width - 1 - k
            dx = dx + w_ref[k:k + 1, :] * _shift_up(g, s)
            dws.append(jnp.sum(g * _shift_down(xv, s), axis=0, keepdims=True))
        dx_ref[0] = dx
        dw = jnp.concatenate(dws, axis=0)
        db = jnp.sum(g, axis=0, keepdims=True)

        @pl.when(pl.program_id(1) == 0)
        def _():
            dw_ref[...] = dw
            db_ref[...] = db

        @pl.when(pl.program_id(1) != 0)
        def _():
            dw_ref[...] += dw
            db_ref[...] += db

    return pl.pallas_call(
        body,
        name=name,
        grid=(c // LANES, bsz),
        in_specs=[
            pl.BlockSpec((1, t, LANES), lambda j, i: (i, 0, j)),
            pl.BlockSpec((width, LANES), lambda j, i: (0, j)),
            pl.BlockSpec((1, t, LANES), lambda j, i: (i, 0, j)),
        ],
        out_specs=[
            pl.BlockSpec((1, t, LANES), lambda j, i: (i, 0, j)),
            pl.BlockSpec((width, LANES), lambda j, i: (0, j)),
            pl.BlockSpec((1, LANES), lambda j, i: (0, j)),
        ],
        out_shape=[
            jax.ShapeDtypeStruct(x.shape, F32),
            jax.ShapeDtypeStruct(w.shape, F32),
            jax.ShapeDtypeStruct((1, c), F32),
        ],
        compiler_params=_params("parallel", "arbitrary"),
    )(x, w, dy)


def _make_conv(name):
    @jax.custom_vjp
    def op(x, w, b):
        return _conv_fwd(x, w, b, name + "_fwd")

    def fwd(x, w, b):
        return op(x, w, b), (x, w)

    def bwd(res, dy):
        x, w = res
        return tuple(_conv_bwd(x, w, dy, name + "_bwd"))

    op.defvjp(fwd, bwd)
    return op


_SCAN_ROWS = 8


def _scan_fwd(a, b, name):
    bsz, t, c = a.shape
    cw = _pick(c, (4 * LANES, 2 * LANES, LANES))

    def body(a_ref, b_ref, h_ref):
        row = lax.broadcasted_iota(jnp.int32, (_SCAN_ROWS, cw), 0)

        def step(i, carry):
            r0 = pl.multiple_of(i * _SCAN_ROWS, _SCAN_ROWS)
            av, bv = a_ref[0, pl.ds(r0, _SCAN_ROWS), :], b_ref[0, pl.ds(r0, _SCAN_ROWS), :]
            for s in (1, 2, 4):
                a_sh = jnp.where(row >= s, pltpu.roll(av, s, 0), 1.0)
                b_sh = jnp.where(row >= s, pltpu.roll(bv, s, 0), 0.0)
                bv = av * b_sh + bv
                av = av * a_sh
            hv = bv + av * carry
            h_ref[0, pl.ds(r0, _SCAN_ROWS), :] = hv
            return hv[_SCAN_ROWS - 1:, :]

        lax.fori_loop(0, t // _SCAN_ROWS, step, jnp.zeros((1, cw), F32), unroll=2)

    spec = pl.BlockSpec((1, t, cw), lambda i, j: (i, 0, j))
    return pl.pallas_call(
        body,
        name=name,
        grid=(bsz, c // cw),
        in_specs=[spec, spec],
        out_specs=spec,
        out_shape=jax.ShapeDtypeStruct(a.shape, F32),
        compiler_params=_params("parallel", "parallel"),
    )(a, b)


def _scan_bwd(a, h, g, name):
    bsz, t, c = a.shape
    cw = _pick(c, (2 * LANES, LANES))

    def body(a_ref, h_ref, g_ref, da_ref, db_ref):
        rows = _SCAN_ROWS
        row = lax.broadcasted_iota(jnp.int32, (rows, cw), 0)
        n_tiles = t // rows

        def step(n, carry):
            lam_next, a_next = carry
            i = n_tiles - 1 - n
            r0 = pl.multiple_of(i * rows, rows)
            rp = pl.multiple_of(jnp.maximum(i - 1, 0) * rows, rows)
            av, gv, hv = a_ref[0, pl.ds(r0, rows), :], g_ref[0, pl.ds(r0, rows), :], h_ref[0, pl.ds(r0, rows), :]
            h_before = jnp.where(i > 0, h_ref[0, pl.ds(rp, rows), :][rows - 1:, :], 0.0)
            cv = jnp.where(row < rows - 1, pltpu.roll(av, rows - 1, 0), a_next)
            for s in (1, 2, 4):
                c_sh = jnp.where(row < rows - s, pltpu.roll(cv, rows - s, 0), 1.0)
                g_sh = jnp.where(row < rows - s, pltpu.roll(gv, rows - s, 0), 0.0)
                gv = cv * g_sh + gv
                cv = cv * c_sh
            lam = gv + cv * lam_next
            db_ref[0, pl.ds(r0, rows), :] = lam
            da_ref[0, pl.ds(r0, rows), :] = lam * jnp.where(row >= 1, pltpu.roll(hv, 1, 0), h_before)
            return lam[:1, :], av[:1, :]

        zero = jnp.zeros((1, cw), F32)
        lax.fori_loop(0, n_tiles, step, (zero, zero), unroll=2)

    spec = pl.BlockSpec((1, t, cw), lambda i, j: (i, 0, j))
    return pl.pallas_call(
        body,
        name=name,
        grid=(bsz, c // cw),
        in_specs=[spec, spec, spec],
        out_specs=[spec, spec],
        out_shape=[jax.ShapeDtypeStruct(a.shape, F32)] * 2,
        compiler_params=_params("parallel", "parallel"),
    )(a, h, g)


def _make_scan(name):
    @jax.custom_vjp
    def op(a, b):
        return _scan_fwd(a, b, name + "_fwd")

    def fwd(a, b):
        h = op(a, b)
        return h, (a, h)

    def bwd(res, g):
        a, h = res
        da, db = _scan_bwd(a, h, g, name + "_bwd")
        return da, db

    op.defvjp(fwd, bwd)
    return op


def _query_blocks(t):
    blocks, start = [], 0
    while start < t:
        rows = 2 * SEQ_BLOCK if start + 2 * SEQ_BLOCK <= t else SEQ_BLOCK
        blocks.append((start, rows))
        start += rows
    return blocks


def _attn_exp(q, k, start, scale):
    tq, tk = q.shape[0], k.shape[0]
    s = _dot(q, k, "nt") * scale
    qpos = start + lax.broadcasted_iota(jnp.int32, (tq, tk), 0)
    kpos = lax.broadcasted_iota(jnp.int32, (tq, tk), 1)
    s = jnp.where(kpos <= qpos, s, NEG_INF)
    e = jnp.exp(s - jnp.max(s, axis=-1, keepdims=True))
    return e, 1.0 / jnp.sum(e, axis=-1, keepdims=True)


_MLA_SCALE = (MLA_NOPE + MLA_ROPE) ** -0.5


def _attn_specs(t):
    head = pl.BlockSpec((1, t, LANES), lambda b, h: (b, 0, h))
    shared = pl.BlockSpec((1, t, LANES), lambda b, h: (b, 0, 0))
    return head, shared


def _attn_fwd(q, kv, kpe, name):
    bsz, t, hl = q.shape
    head, shared = _attn_specs(t)

    def body(q_ref, kv_ref, kpe_ref, o_ref, k_s, v_s):
        lane = lax.broadcasted_iota(jnp.int32, (t, LANES), 1)
        kvh = kv_ref[0]
        k_s[...] = jnp.where(lane < MLA_NOPE, kvh, kpe_ref[0]).astype(BF16)
        v_s[...] = kvh.astype(BF16)
        for start, rows in _query_blocks(t):
            n = start + rows
            e, inv_l = _attn_exp(q_ref[0, start:n, :], k_s[:n, :], start, _MLA_SCALE)
            o_ref[0, start:n, :] = _dot(e, v_s[:n, :], "nn") * inv_l

    return pl.pallas_call(
        body,
        name=name,
        grid=(bsz, hl // LANES),
        in_specs=[head, head, shared],
        out_specs=head,
        out_shape=jax.ShapeDtypeStruct(q.shape, F32),
        scratch_shapes=[pltpu.VMEM((t, LANES), BF16), pltpu.VMEM((t, LANES), BF16)],
        compiler_params=_params("parallel", "parallel"),
    )(q, kv, kpe)


def _attn_bwd(q, kv, kpe, do, name):
    bsz, t, hl = q.shape
    head, shared = _attn_specs(t)

    def body(q_ref, kv_ref, kpe_ref, do_ref, dq_ref, dkv_ref, dkpe_ref, k_s, v_s, dk_s, dv_s):
        lane = lax.broadcasted_iota(jnp.int32, (t, LANES), 1)
        kvh = kv_ref[0]
        k_s[...] = jnp.where(lane < MLA_NOPE, kvh, kpe_ref[0]).astype(BF16)
        v_s[...] = kvh.astype(BF16)
        for start, rows in reversed(_query_blocks(t)):
            n = start + rows
            qb = q_ref[0, start:n, :]
            dob = jnp.where(lane[:rows] >= MLA_NOPE, do_ref[0, start:n, :], 0.0)
            kk, vv = k_s[:n, :], v_s[:n, :]
            e, inv_l = _attn_exp(qb, kk, start, _MLA_SCALE)
            p = e * inv_l
            dp = _dot(dob, vv, "nt")
            ds = p * (dp - jnp.sum(dp * p, axis=-1, keepdims=True)) * _MLA_SCALE
            dq_ref[0, start:n, :] = _dot(ds, kk, "nn")
            if n == t:
                dk_s[...] = _dot(ds, qb, "tn")
                dv_s[...] = _dot(p, dob, "tn")
            else:
                dk_s[:n, :] += _dot(ds, qb, "tn")
                dv_s[:n, :] += _dot(p, dob, "tn")
        dk = dk_s[...]
        dkv_ref[0] = jnp.where(lane < MLA_NOPE, dk, dv_s[...])
        dkpe = jnp.where(lane >= MLA_NOPE, dk, 0.0)

        @pl.when(pl.program_id(1) == 0)
        def _():
            dkpe_ref[0] = dkpe

        @pl.when(pl.program_id(1) != 0)
        def _():
            dkpe_ref[0] += dkpe

    return pl.pallas_call(
        body,
        name=name,
        grid=(bsz, hl // LANES),
        in_specs=[head, head, shared, head],
        out_specs=[head, head, shared],
        out_shape=[
            jax.ShapeDtypeStruct(q.shape, F32),
            jax.ShapeDtypeStruct(kv.shape, F32),
            jax.ShapeDtypeStruct(kpe.shape, F32),
        ],
        scratch_shapes=[pltpu.VMEM((t, LANES), BF16), pltpu.VMEM((t, LANES), BF16),
                        pltpu.VMEM((t, LANES), F32), pltpu.VMEM((t, LANES), F32)],
        compiler_params=_params("parallel", "arbitrary"),
    )(q, kv, kpe, do)


def _make_attention(name):
    @jax.custom_vjp
    def op(q, kv, kpe):
        return _attn_fwd(q, kv, kpe, name + "_fwd")

    def fwd(q, kv, kpe):
        return op(q, kv, kpe), (q, kv, kpe)

    def bwd(res, do):
        return tuple(_attn_bwd(*res, do, name + "_bwd"))

    op.defvjp(fwd, bwd)
    return op


_ROPE_SHIFT = MLA_ROPE // 2


def _rope_lanes_call(x, c, sm, sp, transpose, name):
    r, width = x.shape
    tm = _row_tile(r, width)

    def body(x_ref, c_ref, sm_ref, sp_ref, y_ref):
        cv, smv, spv = c_ref[...], sm_ref[...], sp_ref[...]
        for b in range(width // LANES):
            xb = x_ref[:, b * LANES:(b + 1) * LANES]
            if transpose:
                yb = xb * cv + pltpu.roll(xb * smv, _ROPE_SHIFT, 1) + pltpu.roll(xb * spv, LANES - _ROPE_SHIFT, 1)
            else:
                yb = xb * cv + pltpu.roll(xb, LANES - _ROPE_SHIFT, 1) * smv + pltpu.roll(xb, _ROPE_SHIFT, 1) * spv
            y_ref[:, b * LANES:(b + 1) * LANES] = yb

    tab = pl.BlockSpec((tm, LANES), lambda i: (i, 0))
    blk = pl.BlockSpec((tm, width), lambda i: (i, 0))
    return pl.pallas_call(
        body,
        name=name,
        grid=(r // tm,),
        in_specs=[blk, tab, tab, tab],
        out_specs=blk,
        out_shape=jax.ShapeDtypeStruct(x.shape, F32),
        compiler_params=_params("parallel"),
    )(x, c, sm, sp)


def _make_rope_lanes(name):
    @jax.custom_vjp
    def op(x, c, sm, sp):
        return _rope_lanes_call(x, c, sm, sp, False, name + "_fwd")

    def fwd(x, c, sm, sp):
        return op(x, c, sm, sp), (c, sm, sp)

    def bwd(res, dy):
        c, sm, sp = res
        return _rope_lanes_call(dy, c, sm, sp, True, name + "_bwd"), jnp.zeros_like(c), jnp.zeros_like(sm), jnp.zeros_like(sp)

    op.defvjp(fwd, bwd)
    return op


def _ret_chunk_rows(t):
    return t // 4 if t % 32 == 0 else SEQ_BLOCK


def _ret_decays(c, log_gamma):
    row = lax.broadcasted_iota(jnp.int32, (c, 1), 0)
    col = lax.broadcasted_iota(jnp.int32, (1, c), 1)
    rowf = row.astype(F32)
    d = jnp.where(row >= col, jnp.exp(log_gamma * rowf) * jnp.exp(-log_gamma * col.astype(F32)), 0.0)
    return d, jnp.exp(log_gamma * (rowf + 1.0)), jnp.exp(log_gamma * (c - 1.0 - rowf)), jnp.exp(log_gamma * c)


def _ret_specs(c, dk, dv, v_block0, n_chunks, reverse):
    pos = (lambda i: n_chunks - 1 - i) if reverse else (lambda i: i)
    return (
        pl.BlockSpec(memory_space=pltpu.SMEM),
        pl.BlockSpec((1, c, dk), lambda b, h, i: (b, pos(i), h)),
        pl.BlockSpec((1, c, dv), lambda b, h, i: (b, pos(i), h + v_block0)),
        pl.BlockSpec((1, c, dv), lambda b, h, i: (b, pos(i), h)),
        pl.BlockSpec((1, 1, dk, dv), lambda b, h, i: (b, h * n_chunks + pos(i), 0, 0)),
    )


def _ret_fwd(lg, q, k, v, name, dv=None, v_block0=0):
    bsz, t, hdk = q.shape
    heads = lg.shape[0]
    dk, dv = hdk // heads, dv or v.shape[2] // heads
    c = _ret_chunk_rows(t)
    n_chunks = t // c
    lg_spec, qk_spec, v_spec, o_spec, s_spec = _ret_specs(c, dk, dv, v_block0, n_chunks, False)

    def body(lg_ref, q_ref, k_ref, v_ref, o_ref, s_ref, state):
        @pl.when(pl.program_id(2) == 0)
        def _():
            state[...] = jnp.zeros((dk, dv), F32)

        d, a, b, g = _ret_decays(c, lg_ref[pl.program_id(1)])
        qb, kb, vb, s_in = q_ref[0], k_ref[0], v_ref[0], state[...]
        s_ref[0, 0] = s_in
        o_ref[0] = _dot(_dot(qb, kb, "nt") * d, vb, "nn") + a * _dot(qb, s_in, "nn")
        state[...] = g * s_in + _dot(kb * b, vb, "tn")

    return pl.pallas_call(
        body,
        name=name,
        grid=(bsz, heads, n_chunks),
        in_specs=[lg_spec, qk_spec, qk_spec, v_spec],
        out_specs=[o_spec, s_spec],
        out_shape=[jax.ShapeDtypeStruct((bsz, t, heads * dv), F32),
                   jax.ShapeDtypeStruct((bsz, heads * n_chunks, dk, dv), F32)],
        scratch_shapes=[pltpu.VMEM((dk, dv), F32)],
        compiler_params=_params("parallel", "parallel", "arbitrary"),
    )(lg, q, k, v)


def _ret_bwd(lg, q, k, v, states, do, name, v_block0=0, dv_dtype=F32):
    bsz, t, hdk = q.shape
    heads = lg.shape[0]
    dk, dv = hdk // heads, do.shape[2] // heads
    c = _ret_chunk_rows(t)
    n_chunks = t // c
    lg_spec, qk_spec, v_spec, o_spec, s_spec = _ret_specs(c, dk, dv, v_block0, n_chunks, True)

    def body(lg_ref, q_ref, k_ref, v_ref, s_ref, do_ref, dq_ref, dk_ref, dv_ref, dstate):
        @pl.when(pl.program_id(2) == 0)
        def _():
            dstate[...] = jnp.zeros((dk, dv), F32)

        d, a, b, g = _ret_decays(c, lg_ref[pl.program_id(1)])
        qb, kb, vb, dob, s_in, ds_out = q_ref[0], k_ref[0], v_ref[0], do_ref[0], s_ref[0, 0], dstate[...]
        scores = _dot(qb, kb, "nt") * d
        dscores = _dot(dob, vb, "nt") * d
        dq_ref[0] = _dot(dscores, kb, "nn") + a * _dot(dob, s_in, "nt")
        dk_ref[0] = _dot(dscores, qb, "tn") + b * _dot(vb, ds_out, "nt")
        dv_ref[0] = (_dot(scores, dob, "tn") + _dot(kb * b, ds_out, "nn")).astype(dv_dtype)
        dstate[...] = g * ds_out + _dot(qb, a * dob, "tn")

    return pl.pallas_call(
        body,
        name=name,
        grid=(bsz, heads, n_chunks),
        in_specs=[lg_spec, qk_spec, qk_spec, v_spec, s_spec, o_spec],
        out_specs=[qk_spec, qk_spec, o_spec],
        out_shape=[
            jax.ShapeDtypeStruct(q.shape, F32),
            jax.ShapeDtypeStruct(k.shape, F32),
            jax.ShapeDtypeStruct(do.shape, dv_dtype),
        ],
        scratch_shapes=[pltpu.VMEM((dk, dv), F32)],
        compiler_params=_params("parallel", "parallel", "arbitrary"),
    )(lg, q, k, v, states, do)


def _adamw(w, g, m, v, name):
    r, c = w.shape
    tr = _pick(r, (256, 128, 64, 32, 16, 8))

    def body(w_ref, g_ref, m_ref, v_ref, d_ref, nm_ref, nv_ref):
        gv = g_ref[...]
        nm = ADAM_B1 * m_ref[...] + (1.0 - ADAM_B1) * gv
        nv = ADAM_B2 * v_ref[...] + (1.0 - ADAM_B2) * (gv * gv)
        m_hat = nm / (1.0 - ADAM_B1 ** ADAM_STEP)
        v_hat = nv / (1.0 - ADAM_B2 ** ADAM_STEP)
        d_ref[...] = -ADAM_LR * (m_hat / (jnp.sqrt(v_hat) + ADAM_EPS) + ADAM_WD * w_ref[...])
        nm_ref[...] = nm
        nv_ref[...] = nv

    spec = pl.BlockSpec((tr, c), lambda i: (i, 0))
    return pl.pallas_call(
        body,
        name=name,
        grid=(r // tr,),
        in_specs=[spec] * 4,
        out_specs=[spec] * 3,
        out_shape=[jax.ShapeDtypeStruct((r, c), F32)] * 3,
        compiler_params=_params("parallel"),
    )(w, g, m, v)


def _rope_tables(t, half, reps):
    inv = ROPE_BASE ** (-jnp.arange(half, dtype=F32) / half)
    ang = jnp.arange(t, dtype=jnp.int32).astype(F32)[:, None] * inv[None, :]
    return jnp.tile(jnp.cos(ang), (1, reps)), jnp.tile(jnp.sin(ang), (1, reps))


def _padded_len(seq):
    return -(-(N_META + seq) // SEQ_BLOCK) * SEQ_BLOCK


def _embed(meta, x):
    bsz, seq, d = x.shape
    t = _padded_len(seq)
    return jnp.concatenate(
        [jnp.broadcast_to(meta[None], (bsz, N_META, d)), x, jnp.zeros((bsz, t - N_META - seq, d), F32)], axis=1
    ).reshape(bsz * t, d)


def _even_mixer(p, conv_w_shard, conv_b, w_rg_a, b_rg_a, w_rg_x, b_rg_x, lru_lambda, q_norm_g, w_uq_shard,
                kv_norm_g, w_ukv_shard, gathered, bsz):
    conv_w_full, w_uq_full, w_ukv_full = gathered
    r = p.shape[0]
    t = r // bsz

    def tile_rows(tab):
        return jnp.tile(tab, (bsz, 1))

    lru_w = w_rg_a.shape[2] * w_rg_a.shape[1]
    q_rank, kv_rank = q_norm_g.shape[1], kv_norm_g.shape[1]
    p_gate, p_rec, p_q, p_kv, p_kpe = _split_cols(
        p, (lru_w, 2 * lru_w, 2 * lru_w + q_rank, 2 * lru_w + q_rank + kv_rank))

    conv_w = _make_fsdp_param("conv_w")(conv_w_shard[0], conv_w_full)
    xc = _make_conv("conv")(p_rec.reshape(bsz, t, lru_w), conv_w, conv_b).reshape(r, lru_w)
    ga = _make_group_linear("rg_a")(xc, w_rg_a[0])
    gx = _make_group_linear("rg_x")(xc, w_rg_x[0])
    (a, bb), _ = _make_rowwise(_lru_gates_f, "lru_gates", 3, 0, 3)((ga, gx, xc), (), (b_rg_a, b_rg_x, lru_lambda))
    hh = _make_scan("lru_scan")(a.reshape(bsz, t, lru_w), bb.reshape(bsz, t, lru_w)).reshape(r, lru_w)
    (y_rec,), _ = _make_rowwise(_lru_out_f, "lru_out", 2, 0, 0)((hh, p_gate), (), ())

    (qn,), _ = _make_rowwise(_rmsnorm_f, "q_norm", 1, 0, 1)((p_q,), (), (q_norm_g,))
    (kvn,), _ = _make_rowwise(_rmsnorm_f, "kv_norm", 1, 0, 1)((p_kv,), (), (kv_norm_g,))
    d_head = MLA_NOPE + MLA_ROPE
    w_uq_pad = jnp.pad(w_uq_full.reshape(q_rank, MLA_HEADS, d_head), ((0, 0), (0, 0), (0, LANES - d_head)))

    def unpad_uq(dw):
        return dw.reshape(q_rank, MLA_HEADS, LANES)[:, :, :d_head].reshape(q_rank, MLA_HEADS * d_head)

    q = _make_fsdp_linear(True, "ev_uq", unpad_uq)(qn, w_uq_shard[0], w_uq_pad.reshape(q_rank, MLA_HEADS * LANES))
    kv = _make_fsdp_linear(True, "ev_ukv")(kvn, w_ukv_shard[0], w_ukv_full)
    half = MLA_ROPE // 2
    cos, sin = _rope_tables(t, half, 1)
    one, zero = jnp.ones((t, MLA_NOPE), F32), jnp.zeros((t, MLA_NOPE), F32)
    tail = LANES - MLA_NOPE - MLA_ROPE
    c_tab = tile_rows(jnp.concatenate([one, cos, cos, one[:, :tail]], axis=1))
    sm_tab = tile_rows(jnp.concatenate([zero, -sin, zero[:, :half + tail]], axis=1))
    sp_tab = tile_rows(jnp.concatenate([zero, zero[:, :half], sin, zero[:, :tail]], axis=1))
    q = _make_rope_lanes("rope_q")(q, c_tab, sm_tab, sp_tab)
    kpe = _make_rope_lanes("rope_k")(p_kpe, c_tab, sm_tab, sp_tab)
    o = _make_attention("mla")(q.reshape(bsz, t, -1), kv.reshape(bsz, t, -1), kpe.reshape(bsz, t, LANES))
    return jnp.concatenate([y_rec, o.reshape(r, -1)], axis=1)


def _odd_mixer_fwd(p, bsz):
    r, width = p.shape
    t = r // bsz
    qk = width // 6
    dk = qk // RET_HEADS
    cos2, sin2 = _rope_tables(t, dk // 2, 2)
    sin2 = jnp.concatenate([-sin2[:, :dk // 2], sin2[:, dk // 2:]], axis=1)
    rope_args = (_Cols(p, qk, 0), _Cols(p, qk, 1), jnp.tile(cos2, (bsz, 1)), jnp.tile(sin2, (bsz, 1)))
    (rq, rk), _ = _make_rowwise(_rope_ret_f, "rope_ret", 2, 2, 0).fwd_call(*rope_args)
    lg = jnp.log(1.0 - 2.0 ** (-5.0 - jnp.arange(RET_HEADS, dtype=F32)))
    ret_args = (lg, rq.reshape(bsz, t, qk), rk.reshape(bsz, t, qk), p.reshape(bsz, t, width))
    o, states = _ret_fwd(*ret_args, "ret_fwd", dv=2 * dk, v_block0=qk // dk)
    gate_args = (o.reshape(r, 2 * qk), _Cols(p, 2 * qk, 2))
    (y,), _ = _make_rowwise(_ret_out_bf16_f, "ret_out", 2, 0, 0).fwd_call(*gate_args)
    return y, (rope_args, ret_args + (states,), gate_args)


def _odd_mixer_bwd(res, dy):
    rope_args, ret_args, gate_args = res
    bsz, t, qk = ret_args[1].shape
    dk = qk // RET_HEADS
    (do, dg), _ = _make_rowwise(_ret_out_f, "ret_out", 2, 0, 0).bwd_call(
        gate_args, ((dy,), ()), row_dtypes=(F32, BF16))
    drq, drk, dv = _ret_bwd(*ret_args, do.reshape(bsz, t, 2 * qk), "ret_bwd", v_block0=qk // dk, dv_dtype=BF16)
    (dq, dkk), _ = _make_rowwise(_rope_ret_f, "rope_ret", 2, 2, 0).bwd_call(
        rope_args, ((drq.reshape(bsz * t, qk), drk.reshape(bsz * t, qk)), ()), row_dtypes=(BF16, BF16))
    return jnp.concatenate([dq, dkk, dv.reshape(bsz * t, 2 * qk), dg], axis=1)


def _local_loss(h, target):
    bsz, seq, d = target.shape
    t = _padded_len(seq)
    t_real = N_META + seq
    pos = jnp.arange(t, dtype=jnp.int32)
    mask = jnp.tile(((pos >= N_META) & (pos < t_real)).astype(F32)[:, None], (bsz, 1))
    tgt = jnp.concatenate(
        [jnp.zeros((bsz, N_META, d), F32), target, jnp.zeros((bsz, t - t_real, d), F32)], axis=1).reshape(bsz * t, d)
    _, (total,) = _make_rowwise(_loss_f, "loss", 1, 2, 0)((h,), (tgt, mask), ())
    return total[0, 0]


_WEIGHTS = ("meta_tokens", "ev_w_in", "ev_conv_w", "ev_conv_b", "ev_w_rg_a", "ev_b_rg_a", "ev_w_rg_x", "ev_b_rg_x",
            "ev_lru_lambda", "ev_q_norm_g", "ev_w_uq", "ev_kv_norm_g", "ev_w_ukv", "ev_w_out", "od_w_in", "od_w_out",
            "ln_mix_g", "ln_mix_b", "mlp_w1", "mlp_w2", "ln_mlp_g", "ln_mlp_b")


def kernel(x, meta_tokens, ev_w_in, ev_conv_w, ev_conv_b, ev_w_rg_a, ev_b_rg_a, ev_w_rg_x, ev_b_rg_x, ev_lru_lambda, ev_q_norm_g, ev_w_uq, ev_kv_norm_g, ev_w_ukv, ev_w_out, od_w_in, od_w_out, ln_mix_g, ln_mix_b, mlp_w1, mlp_w2, ln_mlp_g, ln_mlp_b, loss_target, m_meta_tokens, m_ev_w_in, m_ev_conv_w, m_ev_conv_b, m_ev_w_rg_a, m_ev_b_rg_a, m_ev_w_rg_x, m_ev_b_rg_x, m_ev_lru_lambda, m_ev_q_norm_g, m_ev_w_uq, m_ev_kv_norm_g, m_ev_w_ukv, m_ev_w_out, m_od_w_in, m_od_w_out, m_ln_mix_g, m_ln_mix_b, m_mlp_w1, m_mlp_w2, m_ln_mlp_g, m_ln_mlp_b, v_meta_tokens, v_ev_w_in, v_ev_conv_w, v_ev_conv_b, v_ev_w_rg_a, v_ev_b_rg_a, v_ev_w_rg_x, v_ev_b_rg_x, v_ev_lru_lambda, v_ev_q_norm_g, v_ev_w_uq, v_ev_kv_norm_g, v_ev_w_ukv, v_ev_w_out, v_od_w_in, v_od_w_out, v_ln_mix_g, v_ln_mix_b, v_mlp_w1, v_mlp_w2, v_ln_mlp_g, v_ln_mlp_b):
    args = locals()
    weights = {n: args[n] for n in _WEIGHTS}
    bsz = x.shape[0]
    my_x, my_y, my_c = _my_place()
    me = 4 * my_x + 2 * my_y + my_c

    big = (("ev_in", ev_w_in[0], True), ("ev_out", ev_w_out[0], False), ("mlp0_w1", mlp_w1[0], True),
           ("mlp0_w2", mlp_w2[0], False), ("od_in", od_w_in[0], True), ("od_out", od_w_out[0], False),
           ("mlp1_w1", mlp_w1[1], True), ("mlp1_w2", mlp_w2[1], False))
    small_sharded = (("meta", meta_tokens, F32), ("conv_w", ev_conv_w[0], F32), ("ev_uq", ev_w_uq[0], BF16),
                     ("ev_ukv", ev_w_ukv[0], BF16))
    to_gather = (tuple((nm, s.astype(dt), True) for nm, s, dt in small_sharded)
                 + tuple((nm, s.astype(BF16), cols) for nm, s, cols in big))
    handles = _gather_start_all([s for _, s, _ in to_gather], "ag_start")
    gathers = {nm: (s, cols, h) for (nm, s, cols), h in zip(to_gather, handles)}
    gather_tokens = (handles[0][4],)

    def full_weight(nm, after):
        shard16, cols, handle = gathers[nm]
        land = _exchange_wait(handle, True, after, "ag_wait_" + nm)
        land = lax.dynamic_update_index_in_dim(land, shard16, me, 0)
        if cols and shard16.shape[1] % LANES == 0:
            return land, True
        return (_unstack_cols(land) if cols else land.reshape(-1, shard16.shape[1])), False

    meta_full, conv_w_full, w_uq_full, w_ukv_full = (
        _unstack_cols(lax.dynamic_update_index_in_dim(
            _exchange_wait(gathers[nm][2], True, gather_tokens[-1], "ag_wait_" + nm), gathers[nm][0], me, 0))
        for nm, _, _ in small_sharded)

    pending = []

    def linear_bwd(nm, x_in, w, dy, cols, unpad=None, **fused):
        w_full, w_stacked = w
        if w_stacked:
            stacked = _matmul(x_in, dy, "tn", nm + "_dw", stacked=True)
            own = lax.dynamic_index_in_dim(stacked, me, 0, keepdims=False)
        else:
            dw = _matmul(x_in, dy, "tn", nm + "_dw")
            dw = dw if unpad is None else unpad(dw)
            n = dw.shape[1] // N_DEV
            if cols:
                stacked = _stack_cols(dw)
                own = lax.dynamic_slice_in_dim(dw, me * n, n, axis=1)
            else:
                stacked = dw.reshape(N_DEV, dw.shape[0] // N_DEV, dw.shape[1])
                own = lax.dynamic_index_in_dim(stacked, me, 0, keepdims=False)
        handle = _exchange_start(stacked, (_N_PEERS,) + stacked.shape[1:], False, "rs_start_" + nm)
        pending.append((nm, own, handle))
        return _matmul(dy, w_full, "nt", nm + "_dx", after=(handle[4],), stacked=w_stacked, **fused)

    def linear_fwd(nm, x_in, w, **fused):
        return _matmul(x_in, w[0], "nn", nm + "_fwd", stacked=w[1], **fused)

    def mlp_fwd(h, h16, l):
        w1 = full_weight(f"mlp{l}_w1", h16)
        a16 = linear_fwd(f"mlp{l}_w1", h16, w1, out_relu2=True, out_dtype=BF16)
        w2 = full_weight(f"mlp{l}_w2", a16)
        f = linear_fwd(f"mlp{l}_w2", a16, w2)
        ln_args = (h, f, ln_mlp_g[l:l + 1], ln_mlp_b[l:l + 1])
        return ln_fwd(f"mlp{l}_ln", *ln_args), (h16, w1, a16, w2, ln_args)

    def mlp_bwd(dout, res, l):
        h16, w1, a16, w2, ln_args = res
        dh, df, dg, db = ln_bwd(f"mlp{l}_ln", ln_args, dout)
        du = linear_bwd(f"mlp{l}_w2", a16, w2, df, False, relu2_bwd_of=a16, out_dtype=BF16)
        return (dh, linear_bwd(f"mlp{l}_w1", h16, w1, du, True)), dg, db

    def ln_fwd(nm, h, mix, g, b):
        return _make_rowwise(_ln_res_copy_f, nm, 2, 0, 2).fwd_call(h, mix, g, b)[0]

    def ln_bwd(nm, ln_args, pieces):
        (dh, dmix), (dg, db) = _make_rowwise(_ln_res_f, nm, 2, 0, 2).bwd_call(
            ln_args, ((pieces[0],), ()), more=tuple(pieces[1:]), row_dtypes=(F32, BF16))
        return dh, dmix, dg, db

    h0, vjp_embed = jax.vjp(_embed, meta_full, x)
    n_in = ev_w_in.shape[2] * N_DEV
    kpe0, pad_lo, pad_hi = n_in - MLA_ROPE, MLA_NOPE, LANES - MLA_NOPE - MLA_ROPE
    w_in = full_weight("ev_in", h0)[0]
    zeros_in = jnp.zeros((w_in.shape[0], pad_lo), BF16)
    w_ev_in = (jnp.concatenate([w_in[:, :kpe0], zeros_in, w_in[:, kpe0:], zeros_in[:, :pad_hi]], axis=1), False)

    def unpad_in(dw):
        return jnp.concatenate([dw[:, :kpe0], dw[:, kpe0 + pad_lo:kpe0 + pad_lo + MLA_ROPE]], axis=1)

    p0 = _matmul(h0, w_ev_in[0], "nn", "ev_in_fwd", after=gather_tokens)
    small = (ev_conv_w, ev_conv_b, ev_w_rg_a, ev_b_rg_a, ev_w_rg_x, ev_b_rg_x, ev_lru_lambda, ev_q_norm_g, ev_w_uq,
             ev_kv_norm_g, ev_w_ukv)
    y0, vjp_even = jax.vjp(lambda p, *s: _even_mixer(p, *s, (conv_w_full, w_uq_full, w_ukv_full), bsz), p0, *small)
    w_out = full_weight("ev_out", y0)[0]
    lru_w, d_model = y0.shape[1] - MLA_HEADS * LANES, w_out.shape[1]
    w_att = w_out[lru_w:].reshape(MLA_HEADS, MLA_V, d_model)
    w_att = jnp.concatenate([jnp.zeros((MLA_HEADS, LANES - MLA_V, d_model), BF16), w_att], axis=1)
    w_ev_out = (jnp.concatenate([w_out[:lru_w], w_att.reshape(MLA_HEADS * LANES, d_model)], axis=0), False)

    def unpad_out(dw):
        d_att = dw[lru_w:].reshape(MLA_HEADS, LANES, d_model)[:, LANES - MLA_V:].reshape(MLA_HEADS * MLA_V, d_model)
        return jnp.concatenate([dw[:lru_w], d_att], axis=0)

    mix0 = linear_fwd("ev_out", y0, w_ev_out)
    ln0_args = (h0, mix0, ln_mix_g[0:1], ln_mix_b[0:1])
    h1, h1_16 = ln_fwd("mix0_ln", *ln0_args)
    (h2, h2_16), res_mlp0 = mlp_fwd(h1, h1_16, 0)
    w_od_in = full_weight("od_in", h2_16)
    p1 = linear_fwd("od_in", h2_16, w_od_in)
    y1, res_odd = _odd_mixer_fwd(p1, bsz)
    w_od_out = full_weight("od_out", y1)
    mix1 = linear_fwd("od_out", y1, w_od_out)
    ln1_args = (h2, mix1, ln_mix_g[1:2], ln_mix_b[1:2])
    h3, h3_16 = ln_fwd("mix1_ln", *ln1_args)
    (h4, _), res_mlp1 = mlp_fwd(h3, h3_16, 1)
    loss_local, vjp_loss = jax.vjp(lambda h: _local_loss(h, loss_target), h4)

    dh4 = vjp_loss(jnp.ones((), F32))
    dh3, dg_mlp1, db_mlp1 = mlp_bwd(dh4, res_mlp1, 1)
    dh2, dmix1, dg_mix1, db_mix1 = ln_bwd("mix1_ln", ln1_args, dh3)
    dp1 = _odd_mixer_bwd(res_odd, linear_bwd("od_out", y1, w_od_out, dmix1, False))
    dh2 = (dh2, linear_bwd("od_in", h2_16, w_od_in, dp1, True))
    dh1, dg_mlp0, db_mlp0 = mlp_bwd(dh2, res_mlp0, 0)
    dh0, dmix0, dg_mix0, db_mix0 = ln_bwd("mix0_ln", ln0_args, dh1)
    dp0, *dsmall = vjp_even(linear_bwd("ev_out", y0, w_ev_out, dmix0, False, unpad=unpad_out))
    dh0 = dh0 + linear_bwd("ev_in", h0, w_ev_in, dp0.astype(BF16), True, unpad=unpad_in)
    g_meta_full, grad_x = vjp_embed(dh0)
    (g_conv_w, g_conv_b, g_w_rg_a, g_b_rg_a, g_w_rg_x, g_b_rg_x, g_lambda, g_q_norm, g_w_uq, g_kv_norm, g_w_ukv) = dsmall

    rep_names = ("ev_conv_b", "ev_w_rg_a", "ev_b_rg_a", "ev_w_rg_x", "ev_b_rg_x", "ev_lru_lambda", "ev_q_norm_g",
                 "ev_kv_norm_g", "ln_mix_g", "ln_mix_b", "ln_mlp_g", "ln_mlp_b")
    rep_local = (g_conv_b, g_w_rg_a, g_b_rg_a, g_w_rg_x, g_b_rg_x, g_lambda, g_q_norm, g_kv_norm,
                 jnp.concatenate([dg_mix0, dg_mix1]), jnp.concatenate([db_mix0, db_mix1]),
                 jnp.concatenate([dg_mlp0, dg_mlp1]), jnp.concatenate([db_mlp0, db_mlp1]))
    rep_packed = _pack_rows(rep_local)
    rep_handle = _exchange_start(rep_packed, (N_DEV,) + rep_packed.shape, True, "rep_start", after=(grad_x,))

    after, summed = rep_handle[4], {}
    for nm, own, handle in pending:
        land = _exchange_wait(handle, False, after, "rs_wait_" + nm)
        summed[nm] = after = _sum_own_and_peers(own, land, "rs_sum_" + nm)

    g_meta = _scatter_grad(g_meta_full, True, "meta_rs", after=(after,))
    rep_land = lax.dynamic_update_index_in_dim(_exchange_wait(rep_handle, True, g_meta, "rep_wait"), rep_packed, me, 0)
    grad_w = dict(zip(rep_names, _unpack_rows(_sum_blocks(rep_land, "rep_sum"), rep_local)))
    grad_w.update(meta_tokens=g_meta, ev_conv_w=g_conv_w, ev_w_uq=g_w_uq, ev_w_ukv=g_w_ukv)
    grad_w.update(ev_w_in=summed["ev_in"][None], ev_w_out=summed["ev_out"][None], od_w_in=summed["od_in"][None],
                  od_w_out=summed["od_out"][None], mlp_w1=jnp.stack([summed["mlp0_w1"], summed["mlp1_w1"]]),
                  mlp_w2=jnp.stack([summed["mlp0_w2"], summed["mlp1_w2"]]))

    loss = lax.psum(loss_local, MESH_AXES)
    delta, new_m, new_v = {}, {}, {}
    for n in _WEIGHTS:
        w, g, m, v = weights[n], grad_w[n], args["m_" + n], args["v_" + n]
        two_d = (-1, w.shape[-1])
        d2, m2, v2 = _adamw(w.reshape(two_d), g.reshape(two_d), m.reshape(two_d), v.reshape(two_d), "adamw_" + n)
        delta[n], new_m[n], new_v[n] = d2.reshape(w.shape), m2.reshape(w.shape), v2.reshape(w.shape)
    return (loss, grad_x, *[grad_w[n] for n in _WEIGHTS], *[delta[n] for n in _WEIGHTS],
            *[new_m[n] for n in _WEIGHTS], *[new_v[n] for n in _WEIGHTS])
```

```python
import functools
import math

import jax
import jax.numpy as jnp
from jax import lax
from jax.experimental import pallas as pl
from jax.experimental.pallas import tpu as pltpu

F32 = jnp.float32
BF16 = jnp.bfloat16

N_DEV = 8
MESH_AXES = ("x", "y", "c")
LANES = 128
SEQ_BLOCK = 128

N_META = 16
LRU_C = 8.0
MLA_HEADS = 8
MLA_NOPE = 64
MLA_ROPE = 32
MLA_V = 64
RET_HEADS = 4
ROPE_BASE = 10000.0
DEPTH = 2
DN_ALPHA = (2 * DEPTH) ** 0.25
EPS = 1e-5
NEG_INF = -1e30

ADAM_LR = 0.001
ADAM_B1 = 0.9
ADAM_B2 = 0.999
ADAM_EPS = 1e-08
ADAM_WD = 0.01
ADAM_STEP = 10

VMEM_LIMIT = 56 * 1024 * 1024


def _params(*sem):
    return pltpu.CompilerParams(dimension_semantics=sem, vmem_limit_bytes=VMEM_LIMIT)


def _pick(n, cands):
    for c in cands:
        if n % c == 0:
            return c
    return n


def _row_tile(r, width):
    cands = (256, 128, 64, 32, 16, 8) if width <= 1024 else (128, 64, 32, 16, 8)
    return _pick(r, cands)


_DIMS = {"nn": (((1,), (0,)), ((), ())), "nt": (((1,), (1,)), ((), ())), "tn": (((0,), (0,)), ((), ()))}


def _dot(a, b, mode):
    return lax.dot_general(a.astype(BF16), b.astype(BF16), _DIMS[mode], preferred_element_type=F32)


def _matmul(a, b, mode, name, after=(), stacked=False, relu2_bwd_of=None, out_dtype=F32, out_relu2=False):
    if stacked:
        n_blk = b.shape[2] if mode != "tn" else b.shape[1] // N_DEV
    if mode == "nn":
        (m, k), n = a.shape, (N_DEV * n_blk if stacked else b.shape[1])
    elif mode == "nt":
        (m, k), n = a.shape, (b.shape[1] if stacked else b.shape[0])
    else:
        (k, m), n = a.shape, b.shape[1]
    tm = _pick(m, (1088, 1024, 544, 512, 272, 256, 128, 64, 32, 16, 8))
    tn = _pick(n, (1024, 512, 256, 128))
    tk = _pick(k, (1088, 1024, 544, 512, 272, 256, 128))
    kb = 2
    if stacked and mode == "nn":
        tn = n_blk
    if stacked and mode == "tn":
        tn = kb * n_blk
    if stacked and mode == "nt":
        tk = kb * n_blk
    nk = k // tk
    assert out_dtype == F32 or (nk == 1 and not (stacked and mode == "tn")), "narrow results need a single k step"
    assert not out_relu2 or nk == 1, "relu^2 is applied to a finished tile"

    out_spec = pl.BlockSpec((tm, tn), lambda i, j, kk: (i, j))
    out_shape = jax.ShapeDtypeStruct((m, n), out_dtype)
    if mode == "nn":
        a_spec = pl.BlockSpec((tm, tk), lambda i, j, kk: (i, kk))
        b_spec = pl.BlockSpec((tk, tn), lambda i, j, kk: (kk, j))
        if stacked:
            b_spec = pl.BlockSpec((None, tk, tn), lambda i, j, kk: (j, kk, 0))
    elif mode == "nt":
        a_spec = pl.BlockSpec((tm, tk), lambda i, j, kk: (i, kk))
        b_spec = pl.BlockSpec((tn, tk), lambda i, j, kk: (j, kk))
        if stacked:
            b_spec = pl.BlockSpec((kb, tn, n_blk), lambda i, j, kk: (kk, j, 0))
    else:
        a_spec = pl.BlockSpec((tk, tm), lambda i, j, kk: (kk, i))
        b_spec = pl.BlockSpec((tk, tn), lambda i, j, kk: (kk, j))
        if stacked:
            out_spec = pl.BlockSpec((kb, tm, n_blk), lambda i, j, kk: (j, i, 0))
            out_shape = jax.ShapeDtypeStruct((N_DEV, m, n_blk), F32)
    extra = [] if relu2_bwd_of is None else [relu2_bwd_of]
    extra_specs = [pl.BlockSpec((tm, tn), lambda i, j, kk: (i, j))] * len(extra)

    def body(a_ref, b_ref, *rest):
        def relu2_slope():
            return 2.0 * jnp.sqrt(rest[0][...].astype(F32))

        o_ref = rest[-1]
        kk = pl.program_id(2)
        av = a_ref[...]
        if stacked and mode == "nt":
            part = _dot(av[:, :n_blk], b_ref[0], mode)
            for q in range(1, kb):
                part = part + _dot(av[:, q * n_blk:(q + 1) * n_blk], b_ref[q], mode)
        else:
            part = _dot(av, b_ref[...], mode)
        if stacked and mode == "tn":
            part = jnp.stack([part[:, q * n_blk:(q + 1) * n_blk] for q in range(kb)])
        if nk == 1:
            if out_relu2:
                part = jnp.maximum(part, 0.0)
                part = part * part
            if relu2_bwd_of is not None:
                part = part * relu2_slope()
            o_ref[...] = part.astype(out_dtype)
            return

        @pl.when(kk == 0)
        def _():
            o_ref[...] = part

        @pl.when(kk != 0)
        def _():
            o_ref[...] += part

        if relu2_bwd_of is not None:
            @pl.when(kk == nk - 1)
            def _():
                o_ref[...] *= relu2_slope()

    return pl.pallas_call(
        body,
        name=name,
        grid=(m // tm, n // tn, nk),
        in_specs=[a_spec, b_spec] + extra_specs + [pl.BlockSpec(memory_space=pl.ANY)] * len(after),
        out_specs=out_spec,
        out_shape=out_shape,
        compiler_params=_params("parallel", "parallel", "arbitrary"),
    )(a, b, *extra, *after)


def _group_matmul(a, w, mode, name):
    if mode in ("nn", "nt"):
        g, dk, dn = w.shape
        m = a.shape[0]
        d_in, d_out = (dk, dn) if mode == "nn" else (dn, dk)
        tm = _pick(m, (1088, 1024, 544, 512, 272, 256, 128, 64, 32, 16, 8))

        def body(a_ref, w_ref, o_ref):
            o_ref[...] = _dot(a_ref[...], w_ref[0], mode)

        return pl.pallas_call(
            body,
            name=name,
            grid=(g, m // tm),
            in_specs=[pl.BlockSpec((tm, d_in), lambda h, i: (i, h)), pl.BlockSpec((1, dk, dn), lambda h, i: (h, 0, 0))],
            out_specs=pl.BlockSpec((tm, d_out), lambda h, i: (i, h)),
            out_shape=jax.ShapeDtypeStruct((m, g * d_out), F32),
            compiler_params=_params("parallel", "parallel"),
        )(a, w)
    b = w
    m = a.shape[0]
    dk = dn = LANES
    g = a.shape[1] // dk
    tm = _pick(m, (1088, 1024, 544, 512, 272, 256, 128, 64, 32, 16, 8))

    def body(a_ref, b_ref, o_ref):
        part = _dot(a_ref[...], b_ref[...], "tn")

        @pl.when(pl.program_id(1) == 0)
        def _():
            o_ref[0] = part

        @pl.when(pl.program_id(1) != 0)
        def _():
            o_ref[0] += part

    return pl.pallas_call(
        body,
        name=name,
        grid=(g, m // tm),
        in_specs=[pl.BlockSpec((tm, dk), lambda h, i: (i, h)), pl.BlockSpec((tm, dn), lambda h, i: (i, h))],
        out_specs=pl.BlockSpec((1, dk, dn), lambda h, i: (h, 0, 0)),
        out_shape=jax.ShapeDtypeStruct((g, dk, dn), F32),
        compiler_params=_params("parallel", "arbitrary"),
    )(a, b)


def _make_group_linear(name):
    @jax.custom_vjp
    def op(x, w):
        return _group_matmul(x, w, "nn", name + "_fwd")

    def fwd(x, w):
        return op(x, w), (x, w)

    def bwd(res, dy):
        x, w = res
        return _group_matmul(dy, w, "nt", name + "_dx"), _group_matmul(x, dy, "tn", name + "_dw")

    op.defvjp(fwd, bwd)
    return op


def _my_place():
    return lax.axis_index("x"), lax.axis_index("y"), lax.axis_index("c")


def _all_gather(shard, name, after=()):
    shape, dtype = shard.shape, shard.dtype

    def body(x_ref, *rest):
        out_ref, send_sems, recv_sems, local_sem = rest[len(after):]
        x, y, c = _my_place()
        me, sibling = (x, y, c), (x, y, 1 - c)
        chips = [(1 - x, y), (x, 1 - y), (1 - x, 1 - y)]

        def slot(px, py, pc):
            return out_ref.at[4 * px + 2 * py + pc]

        def copy(k, block, to, src=None):
            return pltpu.make_async_remote_copy(
                src_ref=slot(*block) if src is None else src,
                dst_ref=slot(*block),
                send_sem=send_sems.at[k],
                recv_sem=recv_sems.at[k],
                device_id=to,
                device_id_type=pl.DeviceIdType.MESH,
            )

        mine = pltpu.make_async_copy(x_ref, slot(*me), local_sem)
        mine.start()
        first = [copy(0, me, sibling, src=x_ref)]
        first += [copy(1 + j, me, (*chip, c), src=x_ref) for j, chip in enumerate(chips)]
        for cp in first:
            cp.start()
        passed = [copy(4 + j, (*chip, c), sibling) for j, chip in enumerate(chips)]
        for j, chip in enumerate(chips):
            copy(1 + j, (*chip, c), me).wait_recv()
            passed[j].start()
        copy(0, sibling, me).wait_recv()
        for j, chip in enumerate(chips):
            copy(4 + j, (*chip, 1 - c), me).wait_recv()
        for cp in first + passed:
            cp.wait_send()
        mine.wait()

    return pl.pallas_call(
        body,
        name=name,
        out_shape=jax.ShapeDtypeStruct((N_DEV,) + shape, dtype),
        in_specs=[pl.BlockSpec(memory_space=pl.ANY)] * (1 + len(after)),
        out_specs=pl.BlockSpec(memory_space=pl.ANY),
        scratch_shapes=[pltpu.SemaphoreType.DMA((7,)), pltpu.SemaphoreType.DMA((7,)), pltpu.SemaphoreType.DMA],
    )(shard, *after)


def _all_to_all(stacked, name, after=()):
    def body(x_ref, *rest):
        out_ref, send_sems, recv_sems, local_sem = rest[len(after):]
        x, y, c = _my_place()
        me = 4 * x + 2 * y + c
        mine = pltpu.make_async_copy(x_ref.at[me], out_ref.at[me], local_sem)
        mine.start()
        copies = []
        for k in range(1, N_DEV):
            px, py, pc = x ^ ((k >> 2) & 1), y ^ ((k >> 1) & 1), c ^ (k & 1)
            peer = 4 * px + 2 * py + pc
            copies.append(
                pltpu.make_async_remote_copy(
                    src_ref=x_ref.at[peer],
                    dst_ref=out_ref.at[me],
                    send_sem=send_sems.at[k - 1],
                    recv_sem=recv_sems.at[k - 1],
                    device_id=(px, py, pc),
                    device_id_type=pl.DeviceIdType.MESH,
                )
            )
        for cp in copies:
            cp.start()
        for cp in copies:
            cp.wait_recv()
        for cp in copies:
            cp.wait_send()
        mine.wait()

    return pl.pallas_call(
        body,
        name=name,
        out_shape=jax.ShapeDtypeStruct(stacked.shape, stacked.dtype),
        in_specs=[pl.BlockSpec(memory_space=pl.ANY)] * (1 + len(after)),
        out_specs=pl.BlockSpec(memory_space=pl.ANY),
        scratch_shapes=[pltpu.SemaphoreType.DMA((7,)), pltpu.SemaphoreType.DMA((7,)), pltpu.SemaphoreType.DMA],
    )(stacked, *after)


def _sum_blocks(stacked, name):
    _, r, c = stacked.shape
    tr = _pick(r, (256, 128, 64, 32, 16, 8))

    def body(x_ref, o_ref):
        s = [x_ref[j] for j in range(N_DEV)]
        o_ref[...] = ((s[0] + s[1]) + (s[2] + s[3])) + ((s[4] + s[5]) + (s[6] + s[7]))

    return pl.pallas_call(
        body,
        name=name,
        grid=(r // tr,),
        in_specs=[pl.BlockSpec((N_DEV, tr, c), lambda i: (0, i, 0))],
        out_specs=pl.BlockSpec((tr, c), lambda i: (i, 0)),
        out_shape=jax.ShapeDtypeStruct((r, c), stacked.dtype),
        compiler_params=_params("parallel"),
    )(stacked)


def _stack_cols(full):
    k, n8 = full.shape
    return full.reshape(k, N_DEV, n8 // N_DEV).transpose(1, 0, 2)


def _unstack_cols(stacked):
    j, k, n = stacked.shape
    return stacked.transpose(1, 0, 2).reshape(k, j * n)


def _split_cols(p, cuts):
    bounds = (0,) + tuple(cuts) + (p.shape[1],)

    @jax.custom_vjp
    def op(z):
        return tuple(z[:, lo:hi] for lo, hi in zip(bounds[:-1], bounds[1:]))

    op.defvjp(lambda z: (op(z), None), lambda _, cots: (jnp.concatenate(cots, axis=1),))
    return op(p)


def _gather_weight(shard, cols, name):
    g = _all_gather(shard.astype(BF16), name)
    return _unstack_cols(g) if cols else g.reshape(-1, shard.shape[1])


def _scatter_grad(full, cols, name, after=()):
    if cols:
        st = _stack_cols(full)
    else:
        st = full.reshape(N_DEV, full.shape[0] // N_DEV, full.shape[1])
    return _sum_blocks(_all_to_all(st, name + "_a2a", after), name + "_sum")


def _make_slot_linear(name):
    @jax.custom_vjp
    def op(x, w_full, slot):
        return _matmul(x, w_full, "nn", name + "_fwd")

    def fwd(x, w_full, slot):
        return op(x, w_full, slot), (x, w_full)

    def bwd(res, dy):
        x, w = res
        return _matmul(dy, w, "nt", name + "_dx"), jnp.zeros_like(w), _matmul(x, dy, "tn", name + "_dw")

    op.defvjp(fwd, bwd)
    return op


def _pack_rows(gs):
    flat = jnp.concatenate([g.reshape(-1) for g in gs])
    n = flat.shape[0]
    rows = -(-n // (256 * LANES)) * 256
    return jnp.pad(flat, (0, rows * LANES - n)).reshape(rows, LANES)


def _unpack_rows(packed, like):
    flat, out, off = packed.reshape(-1), [], 0
    for g in like:
        out.append(flat[off:off + g.size].reshape(g.shape))
        off += g.size
    return out


_HBM = pl.BlockSpec(memory_space=pltpu.HBM)
_SEM = pl.BlockSpec(memory_space=pltpu.SEMAPHORE)
_SIDE_EFFECT = pltpu.SideEffectType.DATAFLOW_SIDE_EFFECTING
_N_PEERS = N_DEV - 1


def _peer(k):
    x, y, c = _my_place()
    return x ^ ((k >> 2) & 1), y ^ ((k >> 1) & 1), c ^ (k & 1)


def _exchange_start(src, land_shape, gather, name, after=()):
    def body(src_ref, land_ref, *rest):
        send_sems, recv_sems, src_thru, land_thru, token = rest[len(after):]
        x, y, c = _my_place()
        me = 4 * x + 2 * y + c
        for k in range(1, N_DEV):
            px, py, pc = _peer(k)
            pltpu.make_async_remote_copy(
                src_ref=src_ref if gather else src_ref.at[4 * px + 2 * py + pc],
                dst_ref=land_ref.at[me] if gather else land_ref.at[k - 1],
                send_sem=send_sems.at[k - 1],
                recv_sem=recv_sems.at[k - 1],
                device_id=(px, py, pc),
                device_id_type=pl.DeviceIdType.MESH,
            ).start()
        token[...] = jnp.zeros_like(token)

    return pl.pallas_call(
        body,
        name=name,
        out_shape=(
            pltpu.SemaphoreType.DMA((_N_PEERS,)),
            pltpu.SemaphoreType.DMA((_N_PEERS,)),
            pltpu.HBM(src.shape, src.dtype),
            pltpu.HBM(land_shape, src.dtype),
            jax.ShapeDtypeStruct((8, LANES), F32),
        ),
        in_specs=(_HBM, _HBM) + (pl.BlockSpec(memory_space=pl.ANY),) * len(after),
        out_specs=(_SEM, _SEM, _HBM, _HBM, pl.BlockSpec(memory_space=pltpu.VMEM)),
        input_output_aliases={0: 2, 1: 3},
        compiler_params=pltpu.CompilerParams(has_side_effects=_SIDE_EFFECT),
    )(pltpu.with_memory_space_constraint(src, pltpu.HBM),
      pltpu.with_memory_space_constraint(lax.empty(land_shape, src.dtype), pltpu.HBM), *after)


def _gather_start_all(shards, name):
    n = len(shards)

    def body(*refs):
        srcs, lands = refs[:n], refs[n:2 * n]
        outs = refs[2 * n:]
        send_sems, recv_sems, token = outs[:n], outs[n:2 * n], outs[-1]
        x, y, c = _my_place()
        me = 4 * x + 2 * y + c
        for i in range(n):
            for k in range(1, N_DEV):
                pltpu.make_async_remote_copy(
                    src_ref=srcs[i],
                    dst_ref=lands[i].at[me],
                    send_sem=send_sems[i].at[k - 1],
                    recv_sem=recv_sems[i].at[k - 1],
                    device_id=_peer(k),
                    device_id_type=pl.DeviceIdType.MESH,
                ).start()
        token[...] = jnp.zeros_like(token)

    lands = [(N_DEV,) + s.shape for s in shards]
    sems = tuple(pltpu.SemaphoreType.DMA((_N_PEERS,)) for _ in range(2 * n))
    res = pl.pallas_call(
        body,
        name=name,
        out_shape=sems + tuple(pltpu.HBM(s.shape, s.dtype) for s in shards)
        + tuple(pltpu.HBM(ls, s.dtype) for ls, s in zip(lands, shards)) + (jax.ShapeDtypeStruct((8, LANES), F32),),
        in_specs=(_HBM,) * (2 * n),
        out_specs=(_SEM,) * (2 * n) + (_HBM,) * (2 * n) + (pl.BlockSpec(memory_space=pltpu.VMEM),),
        input_output_aliases={i: 2 * n + i for i in range(2 * n)},
        compiler_params=pltpu.CompilerParams(has_side_effects=_SIDE_EFFECT),
    )(*[pltpu.with_memory_space_constraint(s, pltpu.HBM) for s in shards],
      *[pltpu.with_memory_space_constraint(lax.empty(ls, s.dtype), pltpu.HBM) for ls, s in zip(lands, shards)])
    return [(res[i], res[n + i], res[2 * n + i], res[3 * n + i], res[-1]) for i in range(n)]


def _exchange_wait(handle, gather, after, name):
    send_sems, recv_sems, src_thru, land_thru, _ = handle

    def body(src_ref, land_ref, send_sems, recv_sems, after_ref, src_dead, got_ref):
        for k in range(1, N_DEV):
            cp = pltpu.make_async_remote_copy(
                src_ref=src_ref if gather else src_ref.at[k],
                dst_ref=land_ref.at[k - 1],
                send_sem=send_sems.at[k - 1],
                recv_sem=recv_sems.at[k - 1],
                device_id=_peer(k),
                device_id_type=pl.DeviceIdType.MESH,
            )
            cp.wait_send()
            cp.wait_recv()

    return pl.pallas_call(
        body,
        name=name,
        out_shape=(pltpu.HBM(src_thru.shape, src_thru.dtype), pltpu.HBM(land_thru.shape, land_thru.dtype)),
        in_specs=(_HBM, _HBM, _SEM, _SEM, pl.BlockSpec(memory_space=pl.ANY)),
        out_specs=(_HBM, _HBM),
        input_output_aliases={0: 0, 1: 1},
        compiler_params=pltpu.CompilerParams(has_side_effects=_SIDE_EFFECT),
    )(src_thru, land_thru, send_sems, recv_sems, after)[1]


def _sum_own_and_peers(own, land, name):
    r, c = own.shape
    tr = _pick(r, (256, 128, 64, 32, 16, 8))

    def body(o_ref, l_ref, out_ref):
        s = [l_ref[j] for j in range(_N_PEERS)]
        out_ref[...] = ((o_ref[...] + s[0]) + (s[1] + s[2])) + ((s[3] + s[4]) + (s[5] + s[6]))

    return pl.pallas_call(
        body,
        name=name,
        grid=(r // tr,),
        in_specs=[pl.BlockSpec((tr, c), lambda i: (i, 0)), pl.BlockSpec((_N_PEERS, tr, c), lambda i: (0, i, 0))],
        out_specs=pl.BlockSpec((tr, c), lambda i: (i, 0)),
        out_shape=jax.ShapeDtypeStruct((r, c), own.dtype),
        compiler_params=_params("parallel"),
    )(own, land)


class _Cols:
    def __init__(self, array, width, block):
        self.array, self.width, self.block = array, width, block
        self.shape, self.dtype = (array.shape[0], width), array.dtype


def _base(a):
    return a.array if isinstance(a, _Cols) else a


def _col_block(a):
    return a.block if isinstance(a, _Cols) else 0


def _make_rowwise(f, name, n_rows, n_tabs, n_pars):
    n_in = n_rows + n_tabs + n_pars

    def specs(args, tm):
        blocked = [pl.BlockSpec((tm, a.shape[1]), lambda i, blk=_col_block(a): (i, blk)) for a in args[: n_rows + n_tabs]]
        whole = [pl.BlockSpec(a.shape, lambda i: (0, 0)) for a in args[n_rows + n_tabs:]]
        return blocked + whole

    def out_struct(args, tm):
        blk = [jax.ShapeDtypeStruct((tm, a.shape[1]), a.dtype) for a in args[: n_rows + n_tabs]]
        blk += [jax.ShapeDtypeStruct(a.shape, a.dtype) for a in args[n_rows + n_tabs:]]
        return jax.eval_shape(f, *blk)

    def fwd_call(*args):
        r = args[0].shape[0]
        tm = _row_tile(r, max(a.shape[1] for a in args[:n_rows]))
        ro, so = out_struct(args, tm)

        def body(*refs):
            vals = [x[...] for x in refs[:n_in]]
            outs = refs[n_in:]
            rv, sv = f(*vals)
            for o, v in zip(outs[: len(ro)], rv):
                o[...] = v
            for o, v in zip(outs[len(ro):], sv):
                @pl.when(pl.program_id(0) == 0)
                def _(o=o, v=v):
                    o[...] = v

                @pl.when(pl.program_id(0) != 0)
                def _(o=o, v=v):
                    o[...] += v

        out_shape = [jax.ShapeDtypeStruct((r, s.shape[1]), s.dtype) for s in ro]
        out_shape += [jax.ShapeDtypeStruct(s.shape, s.dtype) for s in so]
        out_specs = [pl.BlockSpec((tm, s.shape[1]), lambda i: (i, 0)) for s in ro]
        out_specs += [pl.BlockSpec(s.shape, lambda i: (0, 0)) for s in so]
        res = pl.pallas_call(
            body,
            name=name + "_fwd",
            grid=(r // tm,),
            in_specs=specs(args, tm),
            out_specs=out_specs,
            out_shape=out_shape,
            compiler_params=_params("arbitrary" if so else "parallel"),
        )(*[_base(a) for a in args])
        return tuple(res[: len(ro)]), tuple(res[len(ro):])

    def bwd_call(args, cots, more=(), row_dtypes=None):
        r = args[0].shape[0]
        tm = _row_tile(r, max(a.shape[1] for a in args[:n_rows]))
        ro, so = out_struct(args, tm)
        crow, csum = cots
        rows, tabs, pars = args[:n_rows], args[n_rows:n_rows + n_tabs], args[n_rows + n_tabs:]
        n_c = len(crow) + len(csum)

        def body(*refs):
            vals = [x[...] for x in refs[:n_in]]
            cv = [x[...] for x in refs[n_in:n_in + n_c]]
            for x in refs[n_in + n_c:n_in + n_c + len(more)]:
                cv[0] = cv[0] + x[...]
            outs = refs[n_in + n_c + len(more):]
            tv = vals[n_rows:n_rows + n_tabs]

            def g(*dargs):
                return f(*dargs[:n_rows], *tv, *dargs[n_rows:])

            _, vjp = jax.vjp(g, *vals[:n_rows], *vals[n_rows + n_tabs:])
            d = vjp((tuple(cv[: len(crow)]), tuple(cv[len(crow):])))
            for o, v in zip(outs[:n_rows], d[:n_rows]):
                o[...] = v.astype(o.dtype)
            for o, v in zip(outs[n_rows:], d[n_rows:]):
                @pl.when(pl.program_id(0) == 0)
                def _(o=o, v=v):
                    o[...] = v

                @pl.when(pl.program_id(0) != 0)
                def _(o=o, v=v):
                    o[...] += v

        in_specs = specs(args, tm)
        in_specs += [pl.BlockSpec((tm, c.shape[1]), lambda i: (i, 0)) for c in crow]
        in_specs += [pl.BlockSpec(c.shape, lambda i: (0, 0)) for c in csum]
        in_specs += [pl.BlockSpec((tm, c.shape[1]), lambda i: (i, 0)) for c in more]
        out_shape = [jax.ShapeDtypeStruct(a.shape, dt) for a, dt in zip(rows, row_dtypes or [a.dtype for a in rows])]
        out_shape += [jax.ShapeDtypeStruct(a.shape, a.dtype) for a in pars]
        out_specs = [pl.BlockSpec((tm, a.shape[1]), lambda i: (i, 0)) for a in rows]
        out_specs += [pl.BlockSpec(a.shape, lambda i: (0, 0)) for a in pars]
        res = pl.pallas_call(
            body,
            name=name + "_bwd",
            grid=(r // tm,),
            in_specs=in_specs,
            out_specs=out_specs,
            out_shape=out_shape,
            compiler_params=_params("arbitrary" if pars else "parallel"),
        )(*[_base(a) for a in args], *crow, *csum, *more)
        return tuple(res[:n_rows]), tuple(res[n_rows:])

    @jax.custom_vjp
    def op(rows, tabs, pars):
        return fwd_call(*rows, *tabs, *pars)

    op.fwd_call, op.bwd_call = fwd_call, bwd_call

    def fwd(rows, tabs, pars):
        return fwd_call(*rows, *tabs, *pars), (rows, tabs, pars)

    def bwd(res, cots):
        rows, tabs, pars = res
        drows, dpars = bwd_call(tuple(rows) + tuple(tabs) + tuple(pars), cots)
        return drows, tuple(jnp.zeros_like(t) for t in tabs), dpars

    op.defvjp(fwd, bwd)
    return op


def _sigmoid(x):
    return 0.5 * (jnp.tanh(0.5 * x) + 1.0)


@jax.custom_jvp
def _softplus(x):
    e = jnp.exp(-jnp.abs(x))
    u = 1.0 + e
    log1p_e = jnp.where(u == 1.0, e, e * jnp.log(u) / jnp.where(u == 1.0, 1.0, u - 1.0))
    return jnp.maximum(x, 0.0) + log1p_e


@_softplus.defjvp
def _softplus_jvp(primals, tangents):
    (x,), (t,) = primals, tangents
    return _softplus(x), t * _sigmoid(x)


def _gelu(x):
    return 0.5 * x * (1.0 + jnp.tanh(math.sqrt(2.0 / math.pi) * (x + 0.044715 * (x * x * x))))


def _ln_res_f(h, mix, g, b):
    z = DN_ALPHA * h + mix
    mu = jnp.mean(z, axis=-1, keepdims=True)
    zc = z - mu
    var = jnp.mean(zc * zc, axis=-1, keepdims=True)
    return (zc * lax.rsqrt(var + EPS) * g + b,), ()


def _ln_res_copy_f(h, mix, g, b):
    (out,), _ = _ln_res_f(h, mix, g, b)
    return (out, out.astype(BF16)), ()


def _rmsnorm_f(x, g):
    return (x * lax.rsqrt(jnp.mean(x * x, axis=-1, keepdims=True) + EPS) * g,), ()


def _lru_gates_f(ga, gx, xc, b_a, b_x, lam):
    r = _sigmoid(ga + b_a)
    i = _sigmoid(gx + b_x)
    log_a = -LRU_C * r * _softplus(-lam)
    a = jnp.exp(log_a)
    one_minus_a2 = jnp.tanh(-log_a) * (jnp.exp(2.0 * log_a) + 1.0)
    return (a, jnp.sqrt(one_minus_a2) * (i * xc)), ()


def _lru_out_f(hh, p_gate):
    return (hh * _gelu(p_gate),), ()


def _rope_ret_f(q, k, cos2, sin2):
    d = cos2.shape[1]
    half = d // 2
    k_scale = d ** -0.5

    def rope(x):
        outs = []
        for h in range(x.shape[1] // d):
            xh = x[:, h * d:(h + 1) * d]
            rot = jnp.concatenate([xh[:, half:], xh[:, :half]], axis=1)
            outs.append(xh * cos2 + rot * sin2)
        return jnp.concatenate(outs, axis=1)

    return (rope(q), rope(k) * k_scale), ()


def _ret_out_f(o, g):
    d = o.shape[1] // RET_HEADS
    outs = []
    for h in range(RET_HEADS):
        oh = o[:, h * d:(h + 1) * d]
        outs.append(oh * lax.rsqrt(jnp.mean(oh * oh, axis=-1, keepdims=True) + EPS))
    y = jnp.concatenate(outs, axis=1)
    return (g * _sigmoid(g) * y,), ()


def _ret_out_bf16_f(o, g):
    (y,), _ = _ret_out_f(o, g)
    return (y.astype(BF16),), ()


def _loss_f(y, t, mask):
    e = (y - t) * mask
    per_row = jnp.sum(e * e, axis=-1, keepdims=True) * (0.5 / y.shape[1])
    total = jnp.sum(per_row, axis=0, keepdims=True)
    return (), (jnp.broadcast_to(total, (1, LANES)),)


def _shift_down(x, s):
    if s == 0:
        return x
    t = x.shape[0]
    row = lax.broadcasted_iota(jnp.int32, x.shape, 0)
    return jnp.where(row >= s, pltpu.roll(x, s, 0), 0.0)


def _shift_up(x, s):
    if s == 0:
        return x
    t = x.shape[0]
    row = lax.broadcasted_iota(jnp.int32, x.shape, 0)
    return jnp.where(row < t - s, pltpu.roll(x, t - s, 0), 0.0)


def _conv_fwd(x, w, b, name):
    bsz, t, c = x.shape
    width = w.shape[0]

    def body(x_ref, w_ref, b_ref, y_ref):
        xv = x_ref[0]
        acc = jnp.broadcast_to(b_ref[...], xv.shape)
        for k in range(width):
            acc = acc + w_ref[k:k + 1, :] * _shift_down(xv, width - 1 - k)
        y_ref[0] = acc

    return pl.pallas_call(
        body,
        name=name,
        grid=(bsz, c // LANES),
        in_specs=[
            pl.BlockSpec((1, t, LANES), lambda i, j: (i, 0, j)),
            pl.BlockSpec((width, LANES), lambda i, j: (0, j)),
            pl.BlockSpec((1, LANES), lambda i, j: (0, j)),
        ],
        out_specs=pl.BlockSpec((1, t, LANES), lambda i, j: (i, 0, j)),
        out_shape=jax.ShapeDtypeStruct(x.shape, F32),
        compiler_params=_params("parallel", "parallel"),
    )(x, w, b)


def _conv_bwd(x, w, dy, name):
    bsz, t, c = x.shape
    width = w.shape[0]

    def body(x_ref, w_ref, dy_ref, dx_ref, dw_ref, db_ref):
        xv, g = x_ref[0], dy_ref[0]
        dx = jnp.zeros_like(xv)
        dws = []
        for k in range(width):
            s = width - 1 - k
            dx = dx + w_ref[k:k + 1, :] * _shift_up(g, s)
            dws.append(jnp.sum(g * _shift_down(xv, s), axis=0, keepdims=True))
        dx_ref[0] = dx
        dw = jnp.concatenate(dws, axis=0)
        db = jnp.sum(g, axis=0, keepdims=True)

        @pl.when(pl.program_id(1) == 0)
        def _():
            dw_ref[...] = dw
            db_ref[...] = db

        @pl.when(pl.program_id(1) != 0)
        def _():
            dw_ref[...] += dw
            db_ref[...] += db

    return pl.pallas_call(
        body,
        name=name,
        grid=(c // LANES, bsz),
        in_specs=[
            pl.BlockSpec((1, t, LANES), lambda j, i: (i, 0, j)),
            pl.BlockSpec((width, LANES), lambda j, i: (0, j)),
            pl.BlockSpec((1, t, LANES), lambda j, i: (i, 0, j)),
        ],
        out_specs=[
            pl.BlockSpec((1, t, LANES), lambda j, i: (i, 0, j)),
            pl.BlockSpec((width, LANES), lambda j, i: (0, j)),
            pl.BlockSpec((1, LANES), lambda j, i: (0, j)),
        ],
        out_shape=[
            jax.ShapeDtypeStruct(x.shape, F32),
            jax.ShapeDtypeStruct(w.shape, F32),
            jax.ShapeDtypeStruct((1, c), F32),
        ],
        compiler_params=_params("parallel", "arbitrary"),
    )(x, w, dy)


def _make_conv(name):
    @jax.custom_vjp
    def op(x, w, b):
        return _conv_fwd(x, w, b, name + "_fwd")

    def fwd(x, w, b):
        return op(x, w, b), (x, w)

    def bwd(res, dy):
        x, w = res
        return tuple(_conv_bwd(x, w, dy, name + "_bwd"))

    op.defvjp(fwd, bwd)
    return op


_SCAN_ROWS = 8


def _scan_fwd(a, b, name):
    bsz, t, c = a.shape
    cw = _pick(c, (4 * LANES, 2 * LANES, LANES))

    def body(a_ref, b_ref, h_ref):
        row = lax.broadcasted_iota(jnp.int32, (_SCAN_ROWS, cw), 0)

        def step(i, carry):
            r0 = pl.multiple_of(i * _SCAN_ROWS, _SCAN_ROWS)
            av, bv = a_ref[0, pl.ds(r0, _SCAN_ROWS), :], b_ref[0, pl.ds(r0, _SCAN_ROWS), :]
            for s in (1, 2, 4):
                a_sh = jnp.where(row >= s, pltpu.roll(av, s, 0), 1.0)
                b_sh = jnp.where(row >= s, pltpu.roll(bv, s, 0), 0.0)
                bv = av * b_sh + bv
                av = av * a_sh
            hv = bv + av * carry
            h_ref[0, pl.ds(r0, _SCAN_ROWS), :] = hv
            return hv[_SCAN_ROWS - 1:, :]

        lax.fori_loop(0, t // _SCAN_ROWS, step, jnp.zeros((1, cw), F32), unroll=2)

    spec = pl.BlockSpec((1, t, cw), lambda i, j: (i, 0, j))
    return pl.pallas_call(
        body,
        name=name,
        grid=(bsz, c // cw),
        in_specs=[spec, spec],
        out_specs=spec,
        out_shape=jax.ShapeDtypeStruct(a.shape, F32),
        compiler_params=_params("parallel", "parallel"),
    )(a, b)


def _scan_bwd(a, h, g, name):
    bsz, t, c = a.shape
    cw = _pick(c, (2 * LANES, LANES))

    def body(a_ref, h_ref, g_ref, da_ref, db_ref):
        rows = _SCAN_ROWS
        row = lax.broadcasted_iota(jnp.int32, (rows, cw), 0)
        n_tiles = t // rows

        def step(n, carry):
            lam_next, a_next = carry
            i = n_tiles - 1 - n
            r0 = pl.multiple_of(i * rows, rows)
            rp = pl.multiple_of(jnp.maximum(i - 1, 0) * rows, rows)
            av, gv, hv = a_ref[0, pl.ds(r0, rows), :], g_ref[0, pl.ds(r0, rows), :], h_ref[0, pl.ds(r0, rows), :]
            h_before = jnp.where(i > 0, h_ref[0, pl.ds(rp, rows), :][rows - 1:, :], 0.0)
            cv = jnp.where(row < rows - 1, pltpu.roll(av, rows - 1, 0), a_next)
            for s in (1, 2, 4):
                c_sh = jnp.where(row < rows - s, pltpu.roll(cv, rows - s, 0), 1.0)
                g_sh = jnp.where(row < rows - s, pltpu.roll(gv, rows - s, 0), 0.0)
                gv = cv * g_sh + gv
                cv = cv * c_sh
            lam = gv + cv * lam_next
            db_ref[0, pl.ds(r0, rows), :] = lam
            da_ref[0, pl.ds(r0, rows), :] = lam * jnp.where(row >= 1, pltpu.roll(hv, 1, 0), h_before)
            return lam[:1, :], av[:1, :]

        zero = jnp.zeros((1, cw), F32)
        lax.fori_loop(0, n_tiles, step, (zero, zero), unroll=2)

    spec = pl.BlockSpec((1, t, cw), lambda i, j: (i, 0, j))
    return pl.pallas_call(
        body,
        name=name,
        grid=(bsz, c // cw),
        in_specs=[spec, spec, spec],
        out_specs=[spec, spec],
        out_shape=[jax.ShapeDtypeStruct(a.shape, F32)] * 2,
        compiler_params=_params("parallel", "parallel"),
    )(a, h, g)


def _make_scan(name):
    @jax.custom_vjp
    def op(a, b):
        return _scan_fwd(a, b, name + "_fwd")

    def fwd(a, b):
        h = op(a, b)
        return h, (a, h)

    def bwd(res, g):
        a, h = res
        da, db = _scan_bwd(a, h, g, name + "_bwd")
        return da, db

    op.defvjp(fwd, bwd)
    return op


def _query_blocks(t):
    blocks, start = [], 0
    while start < t:
        rows = 2 * SEQ_BLOCK if start + 2 * SEQ_BLOCK <= t else SEQ_BLOCK
        blocks.append((start, rows))
        start += rows
    return blocks


def _attn_exp(q, k, start, scale):
    tq, tk = q.shape[0], k.shape[0]
    s = _dot(q, k, "nt") * scale
    qpos = start + lax.broadcasted_iota(jnp.int32, (tq, tk), 0)
    kpos = lax.broadcasted_iota(jnp.int32, (tq, tk), 1)
    s = jnp.where(kpos <= qpos, s, NEG_INF)
    e = jnp.exp(s - jnp.max(s, axis=-1, keepdims=True))
    return e, 1.0 / jnp.sum(e, axis=-1, keepdims=True)


_MLA_SCALE = (MLA_NOPE + MLA_ROPE) ** -0.5


def _attn_specs(t):
    head = pl.BlockSpec((1, t, LANES), lambda b, h: (b, 0, h))
    shared = pl.BlockSpec((1, t, LANES), lambda b, h: (b, 0, 0))
    return head, shared


def _attn_fwd(q, kv, kpe, name):
    bsz, t, hl = q.shape
    head, shared = _attn_specs(t)

    def body(q_ref, kv_ref, kpe_ref, o_ref, k_s, v_s):
        lane = lax.broadcasted_iota(jnp.int32, (t, LANES), 1)
        kvh = kv_ref[0]
        k_s[...] = jnp.where(lane < MLA_NOPE, kvh, kpe_ref[0]).astype(BF16)
        v_s[...] = kvh.astype(BF16)
        for start, rows in _query_blocks(t):
            n = start + rows
            e, inv_l = _attn_exp(q_ref[0, start:n, :], k_s[:n, :], start, _MLA_SCALE)
            o_ref[0, start:n, :] = _dot(e, v_s[:n, :], "nn") * inv_l

    return pl.pallas_call(
        body,
        name=name,
        grid=(bsz, hl // LANES),
        in_specs=[head, head, shared],
        out_specs=head,
        out_shape=jax.ShapeDtypeStruct(q.shape, F32),
        scratch_shapes=[pltpu.VMEM((t, LANES), BF16), pltpu.VMEM((t, LANES), BF16)],
        compiler_params=_params("parallel", "parallel"),
    )(q, kv, kpe)


def _attn_bwd(q, kv, kpe, do, name):
    bsz, t, hl = q.shape
    head, shared = _attn_specs(t)

    def body(q_ref, kv_ref, kpe_ref, do_ref, dq_ref, dkv_ref, dkpe_ref, k_s, v_s, dk_s, dv_s):
        lane = lax.broadcasted_iota(jnp.int32, (t, LANES), 1)
        kvh = kv_ref[0]
        k_s[...] = jnp.where(lane < MLA_NOPE, kvh, kpe_ref[0]).astype(BF16)
        v_s[...] = kvh.astype(BF16)
        for start, rows in reversed(_query_blocks(t)):
            n = start + rows
            qb = q_ref[0, start:n, :]
            dob = jnp.where(lane[:rows] >= MLA_NOPE, do_ref[0, start:n, :], 0.0)
            kk, vv = k_s[:n, :], v_s[:n, :]
            e, inv_l = _attn_exp(qb, kk, start, _MLA_SCALE)
            p = e * inv_l
            dp = _dot(dob, vv, "nt")
            ds = p * (dp - jnp.sum(dp * p, axis=-1, keepdims=True)) * _MLA_SCALE
            dq_ref[0, start:n, :] = _dot(ds, kk, "nn")
            if n == t:
                dk_s[...] = _dot(ds, qb, "tn")
                dv_s[...] = _dot(p, dob, "tn")
            else:
                dk_s[:n, :] += _dot(ds, qb, "tn")
                dv_s[:n, :] += _dot(p, dob, "tn")
        dk = dk_s[...]
        dkv_ref[0] = jnp.where(lane < MLA_NOPE, dk, dv_s[...])
        dkpe = jnp.where(lane >= MLA_NOPE, dk, 0.0)

        @pl.when(pl.program_id(1) == 0)
        def _():
            dkpe_ref[0] = dkpe

        @pl.when(pl.program_id(1) != 0)
        def _():
            dkpe_ref[0] += dkpe

    return pl.pallas_call(
        body,
        name=name,
        grid=(bsz, hl // LANES),
        in_specs=[head, head, shared, head],
        out_specs=[head, head, shared],
        out_shape=[
            jax.ShapeDtypeStruct(q.shape, F32),
            jax.ShapeDtypeStruct(kv.shape, F32),
            jax.ShapeDtypeStruct(kpe.shape, F32),
        ],
        scratch_shapes=[pltpu.VMEM((t, LANES), BF16), pltpu.VMEM((t, LANES), BF16),
                        pltpu.VMEM((t, LANES), F32), pltpu.VMEM((t, LANES), F32)],
        compiler_params=_params("parallel", "arbitrary"),
    )(q, kv, kpe, do)


def _make_attention(name):
    @jax.custom_vjp
    def op(q, kv, kpe):
        return _attn_fwd(q, kv, kpe, name + "_fwd")

    def fwd(q, kv, kpe):
        return op(q, kv, kpe), (q, kv, kpe)

    def bwd(res, do):
        return tuple(_attn_bwd(*res, do, name + "_bwd"))

    op.defvjp(fwd, bwd)
    return op


_ROPE_SHIFT = MLA_ROPE // 2


def _rope_lanes_call(x, c, sm, sp, transpose, name):
    r, width = x.shape
    tm = _row_tile(r, width)

    def body(x_ref, c_ref, sm_ref, sp_ref, y_ref):
        cv, smv, spv = c_ref[...], sm_ref[...], sp_ref[...]
        for b in range(width // LANES):
            xb = x_ref[:, b * LANES:(b + 1) * LANES]
            if transpose:
                yb = xb * cv + pltpu.roll(xb * smv, _ROPE_SHIFT, 1) + pltpu.roll(xb * spv, LANES - _ROPE_SHIFT, 1)
            else:
                yb = xb * cv + pltpu.roll(xb, LANES - _ROPE_SHIFT, 1) * smv + pltpu.roll(xb, _ROPE_SHIFT, 1) * spv
            y_ref[:, b * LANES:(b + 1) * LANES] = yb

    tab = pl.BlockSpec((tm, LANES), lambda i: (i, 0))
    blk = pl.BlockSpec((tm, width), lambda i: (i, 0))
    return pl.pallas_call(
        body,
        name=name,
        grid=(r // tm,),
        in_specs=[blk, tab, tab, tab],
        out_specs=blk,
        out_shape=jax.ShapeDtypeStruct(x.shape, F32),
        compiler_params=_params("parallel"),
    )(x, c, sm, sp)


def _make_rope_lanes(name):
    @jax.custom_vjp
    def op(x, c, sm, sp):
        return _rope_lanes_call(x, c, sm, sp, False, name + "_fwd")

    def fwd(x, c, sm, sp):
        return op(x, c, sm, sp), (c, sm, sp)

    def bwd(res, dy):
        c, sm, sp = res
        return _rope_lanes_call(dy, c, sm, sp, True, name + "_bwd"), jnp.zeros_like(c), jnp.zeros_like(sm), jnp.zeros_like(sp)

    op.defvjp(fwd, bwd)
    return op


def _ret_chunk_rows(t):
    return t // 4 if t % 32 == 0 else SEQ_BLOCK


def _ret_decays(c, log_gamma):
    row = lax.broadcasted_iota(jnp.int32, (c, 1), 0)
    col = lax.broadcasted_iota(jnp.int32, (1, c), 1)
    rowf = row.astype(F32)
    d = jnp.where(row >= col, jnp.exp(log_gamma * rowf) * jnp.exp(-log_gamma * col.astype(F32)), 0.0)
    return d, jnp.exp(log_gamma * (rowf + 1.0)), jnp.exp(log_gamma * (c - 1.0 - rowf)), jnp.exp(log_gamma * c)


def _ret_specs(c, dk, dv, v_block0, n_chunks, reverse):
    pos = (lambda i: n_chunks - 1 - i) if reverse else (lambda i: i)
    return (
        pl.BlockSpec(memory_space=pltpu.SMEM),
        pl.BlockSpec((1, c, dk), lambda b, h, i: (b, pos(i), h)),
        pl.BlockSpec((1, c, dv), lambda b, h, i: (b, pos(i), h + v_block0)),
        pl.BlockSpec((1, c, dv), lambda b, h, i: (b, pos(i), h)),
        pl.BlockSpec((1, 1, dk, dv), lambda b, h, i: (b, h * n_chunks + pos(i), 0, 0)),
    )


def _ret_fwd(lg, q, k, v, name, dv=None, v_block0=0):
    bsz, t, hdk = q.shape
    heads = lg.shape[0]
    dk, dv = hdk // heads, dv or v.shape[2] // heads
    c = _ret_chunk_rows(t)
    n_chunks = t // c
    lg_spec, qk_spec, v_spec, o_spec, s_spec = _ret_specs(c, dk, dv, v_block0, n_chunks, False)

    def body(lg_ref, q_ref, k_ref, v_ref, o_ref, s_ref, state):
        @pl.when(pl.program_id(2) == 0)
        def _():
            state[...] = jnp.zeros((dk, dv), F32)

        d, a, b, g = _ret_decays(c, lg_ref[pl.program_id(1)])
        qb, kb, vb, s_in = q_ref[0], k_ref[0], v_ref[0], state[...]
        s_ref[0, 0] = s_in
        o_ref[0] = _dot(_dot(qb, kb, "nt") * d, vb, "nn") + a * _dot(qb, s_in, "nn")
        state[...] = g * s_in + _dot(kb * b, vb, "tn")

    return pl.pallas_call(
        body,
        name=name,
        grid=(bsz, heads, n_chunks),
        in_specs=[lg_spec, qk_spec, qk_spec, v_spec],
        out_specs=[o_spec, s_spec],
        out_shape=[jax.ShapeDtypeStruct((bsz, t, heads * dv), F32),
                   jax.ShapeDtypeStruct((bsz, heads * n_chunks, dk, dv), F32)],
        scratch_shapes=[pltpu.VMEM((dk, dv), F32)],
        compiler_params=_params("parallel", "parallel", "arbitrary"),
    )(lg, q, k, v)


def _ret_bwd(lg, q, k, v, states, do, name, v_block0=0, dv_dtype=F32):
    bsz, t, hdk = q.shape
    heads = lg.shape[0]
    dk, dv = hdk // heads, do.shape[2] // heads
    c = _ret_chunk_rows(t)
    n_chunks = t // c
    lg_spec, qk_spec, v_spec, o_spec, s_spec = _ret_specs(c, dk, dv, v_block0, n_chunks, True)

    def body(lg_ref, q_ref, k_ref, v_ref, s_ref, do_ref, dq_ref, dk_ref, dv_ref, dstate):
        @pl.when(pl.program_id(2) == 0)
        def _():
            dstate[...] = jnp.zeros((dk, dv), F32)

        d, a, b, g = _ret_decays(c, lg_ref[pl.program_id(1)])
        qb, kb, vb, dob, s_in, ds_out = q_ref[0], k_ref[0], v_ref[0], do_ref[0], s_ref[0, 0], dstate[...]
        scores = _dot(qb, kb, "nt") * d
        dscores = _dot(dob, vb, "nt") * d
        dq_ref[0] = _dot(dscores, kb, "nn") + a * _dot(dob, s_in, "nt")
        dk_ref[0] = _dot(dscores, qb, "tn") + b * _dot(vb, ds_out, "nt")
        dv_ref[0] = (_dot(scores, dob, "tn") + _dot(kb * b, ds_out, "nn")).astype(dv_dtype)
        dstate[...] = g * ds_out + _dot(qb, a * dob, "tn")

    return pl.pallas_call(
        body,
        name=name,
        grid=(bsz, heads, n_chunks),
        in_specs=[lg_spec, qk_spec, qk_spec, v_spec, s_spec, o_spec],
        out_specs=[qk_spec, qk_spec, o_spec],
        out_shape=[
            jax.ShapeDtypeStruct(q.shape, F32),
            jax.ShapeDtypeStruct(k.shape, F32),
            jax.ShapeDtypeStruct(do.shape, dv_dtype),
        ],
        scratch_shapes=[pltpu.VMEM((dk, dv), F32)],
        compiler_params=_params("parallel", "parallel", "arbitrary"),
    )(lg, q, k, v, states, do)


def _adamw(w, g, m, v, name):
    r, c = w.shape
    tr = _pick(r, (256, 128, 64, 32, 16, 8))

    def body(w_ref, g_ref, m_ref, v_ref, d_ref, nm_ref, nv_ref):
        gv = g_ref[...]
        nm = ADAM_B1 * m_ref[...] + (1.0 - ADAM_B1) * gv
        nv = ADAM_B2 * v_ref[...] + (1.0 - ADAM_B2) * (gv * gv)
        m_hat = nm / (1.0 - ADAM_B1 ** ADAM_STEP)
        v_hat = nv / (1.0 - ADAM_B2 ** ADAM_STEP)
        d_ref[...] = -ADAM_LR * (m_hat / (jnp.sqrt(v_hat) + ADAM_EPS) + ADAM_WD * w_ref[...])
        nm_ref[...] = nm
        nv_ref[...] = nv

    spec = pl.BlockSpec((tr, c), lambda i: (i, 0))
    return pl.pallas_call(
        body,
        name=name,
        grid=(r // tr,),
        in_specs=[spec] * 4,
        out_specs=[spec] * 3,
        out_shape=[jax.ShapeDtypeStruct((r, c), F32)] * 3,
        compiler_params=_params("parallel"),
    )(w, g, m, v)


def _rope_tables(t, half, reps):
    inv = ROPE_BASE ** (-jnp.arange(half, dtype=F32) / half)
    ang = jnp.arange(t, dtype=jnp.int32).astype(F32)[:, None] * inv[None, :]
    return jnp.tile(jnp.cos(ang), (1, reps)), jnp.tile(jnp.sin(ang), (1, reps))


def _padded_len(seq):
    return -(-(N_META + seq) // SEQ_BLOCK) * SEQ_BLOCK


def _embed(meta, x):
    bsz, seq, d = x.shape
    t = _padded_len(seq)
    return jnp.concatenate(
        [jnp.broadcast_to(meta[None], (bsz, N_META, d)), x, jnp.zeros((bsz, t - N_META - seq, d), F32)], axis=1
    ).reshape(bsz * t, d)


def _even_mixer(p, conv_w, conv_b, w_rg_a, b_rg_a, w_rg_x, b_rg_x, lru_lambda, q_norm_g, uq_slot, kv_norm_g,
                ukv_slot, gathered, bsz):
    w_uq_pad, w_ukv_full = gathered
    r = p.shape[0]
    t = r // bsz

    def tile_rows(tab):
        return jnp.tile(tab, (bsz, 1))

    lru_w = w_rg_a.shape[2] * w_rg_a.shape[1]
    q_rank, kv_rank = q_norm_g.shape[1], kv_norm_g.shape[1]
    p_gate, p_rec, p_q, p_kv, p_kpe = _split_cols(
        p, (lru_w, 2 * lru_w, 2 * lru_w + q_rank, 2 * lru_w + q_rank + kv_rank))

    xc = _make_conv("conv")(p_rec.reshape(bsz, t, lru_w), conv_w, conv_b).reshape(r, lru_w)
    ga = _make_group_linear("rg_a")(xc, w_rg_a[0])
    gx = _make_group_linear("rg_x")(xc, w_rg_x[0])
    (a, bb), _ = _make_rowwise(_lru_gates_f, "lru_gates", 3, 0, 3)((ga, gx, xc), (), (b_rg_a, b_rg_x, lru_lambda))
    hh = _make_scan("lru_scan")(a.reshape(bsz, t, lru_w), bb.reshape(bsz, t, lru_w)).reshape(r, lru_w)
    (y_rec,), _ = _make_rowwise(_lru_out_f, "lru_out", 2, 0, 0)((hh, p_gate), (), ())

    (qn,), _ = _make_rowwise(_rmsnorm_f, "q_norm", 1, 0, 1)((p_q,), (), (q_norm_g,))
    (kvn,), _ = _make_rowwise(_rmsnorm_f, "kv_norm", 1, 0, 1)((p_kv,), (), (kv_norm_g,))
    q = _make_slot_linear("ev_uq")(qn, w_uq_pad, uq_slot)
    kv = _make_slot_linear("ev_ukv")(kvn, w_ukv_full, ukv_slot)
    half = MLA_ROPE // 2
    cos, sin = _rope_tables(t, half, 1)
    one, zero = jnp.ones((t, MLA_NOPE), F32), jnp.zeros((t, MLA_NOPE), F32)
    tail = LANES - MLA_NOPE - MLA_ROPE
    c_tab = tile_rows(jnp.concatenate([one, cos, cos, one[:, :tail]], axis=1))
    sm_tab = tile_rows(jnp.concatenate([zero, -sin, zero[:, :half + tail]], axis=1))
    sp_tab = tile_rows(jnp.concatenate([zero, zero[:, :half], sin, zero[:, :tail]], axis=1))
    q = _make_rope_lanes("rope_q")(q, c_tab, sm_tab, sp_tab)
    kpe = _make_rope_lanes("rope_k")(p_kpe, c_tab, sm_tab, sp_tab)
    o = _make_attention("mla")(q.reshape(bsz, t, -1), kv.reshape(bsz, t, -1), kpe.reshape(bsz, t, LANES))
    return jnp.concatenate([y_rec, o.reshape(r, -1)], axis=1)


def _odd_mixer_fwd(p, bsz):
    r, width = p.shape
    t = r // bsz
    qk = width // 6
    dk = qk // RET_HEADS
    cos2, sin2 = _rope_tables(t, dk // 2, 2)
    sin2 = jnp.concatenate([-sin2[:, :dk // 2], sin2[:, dk // 2:]], axis=1)
    rope_args = (_Cols(p, qk, 0), _Cols(p, qk, 1), jnp.tile(cos2, (bsz, 1)), jnp.tile(sin2, (bsz, 1)))
    (rq, rk), _ = _make_rowwise(_rope_ret_f, "rope_ret", 2, 2, 0).fwd_call(*rope_args)
    lg = jnp.log(1.0 - 2.0 ** (-5.0 - jnp.arange(RET_HEADS, dtype=F32)))
    ret_args = (lg, rq.reshape(bsz, t, qk), rk.reshape(bsz, t, qk), p.reshape(bsz, t, width))
    o, states = _ret_fwd(*ret_args, "ret_fwd", dv=2 * dk, v_block0=qk // dk)
    gate_args = (o.reshape(r, 2 * qk), _Cols(p, 2 * qk, 2))
    (y,), _ = _make_rowwise(_ret_out_bf16_f, "ret_out", 2, 0, 0).fwd_call(*gate_args)
    return y, (rope_args, ret_args + (states,), gate_args)


def _odd_mixer_bwd(res, dy):
    rope_args, ret_args, gate_args = res
    bsz, t, qk = ret_args[1].shape
    dk = qk // RET_HEADS
    (do, dg), _ = _make_rowwise(_ret_out_f, "ret_out", 2, 0, 0).bwd_call(
        gate_args, ((dy,), ()), row_dtypes=(F32, BF16))
    drq, drk, dv = _ret_bwd(*ret_args, do.reshape(bsz, t, 2 * qk), "ret_bwd", v_block0=qk // dk, dv_dtype=BF16)
    (dq, dkk), _ = _make_rowwise(_rope_ret_f, "rope_ret", 2, 2, 0).bwd_call(
        rope_args, ((drq.reshape(bsz * t, qk), drk.reshape(bsz * t, qk)), ()), row_dtypes=(BF16, BF16))
    return jnp.concatenate([dq, dkk, dv.reshape(bsz * t, 2 * qk), dg], axis=1)


def _local_loss(h, target):
    bsz, seq, d = target.shape
    t = _padded_len(seq)
    t_real = N_META + seq
    pos = jnp.arange(t, dtype=jnp.int32)
    mask = jnp.tile(((pos >= N_META) & (pos < t_real)).astype(F32)[:, None], (bsz, 1))
    tgt = jnp.concatenate(
        [jnp.zeros((bsz, N_META, d), F32), target, jnp.zeros((bsz, t - t_real, d), F32)], axis=1).reshape(bsz * t, d)
    _, (total,) = _make_rowwise(_loss_f, "loss", 1, 2, 0)((h,), (tgt, mask), ())
    return total[0, 0]


_WEIGHTS = ("meta_tokens", "ev_w_in", "ev_conv_w", "ev_conv_b", "ev_w_rg_a", "ev_b_rg_a", "ev_w_rg_x", "ev_b_rg_x",
            "ev_lru_lambda", "ev_q_norm_g", "ev_w_uq", "ev_kv_norm_g", "ev_w_ukv", "ev_w_out", "od_w_in", "od_w_out",
            "ln_mix_g", "ln_mix_b", "mlp_w1", "mlp_w2", "ln_mlp_g", "ln_mlp_b")


def kernel(x, meta_tokens, ev_w_in, ev_conv_w, ev_conv_b, ev_w_rg_a, ev_b_rg_a, ev_w_rg_x, ev_b_rg_x, ev_lru_lambda, ev_q_norm_g, ev_w_uq, ev_kv_norm_g, ev_w_ukv, ev_w_out, od_w_in, od_w_out, ln_mix_g, ln_mix_b, mlp_w1, mlp_w2, ln_mlp_g, ln_mlp_b, loss_target, m_meta_tokens, m_ev_w_in, m_ev_conv_w, m_ev_conv_b, m_ev_w_rg_a, m_ev_b_rg_a, m_ev_w_rg_x, m_ev_b_rg_x, m_ev_lru_lambda, m_ev_q_norm_g, m_ev_w_uq, m_ev_kv_norm_g, m_ev_w_ukv, m_ev_w_out, m_od_w_in, m_od_w_out, m_ln_mix_g, m_ln_mix_b, m_mlp_w1, m_mlp_w2, m_ln_mlp_g, m_ln_mlp_b, v_meta_tokens, v_ev_w_in, v_ev_conv_w, v_ev_conv_b, v_ev_w_rg_a, v_ev_b_rg_a, v_ev_w_rg_x, v_ev_b_rg_x, v_ev_lru_lambda, v_ev_q_norm_g, v_ev_w_uq, v_ev_kv_norm_g, v_ev_w_ukv, v_ev_w_out, v_od_w_in, v_od_w_out, v_ln_mix_g, v_ln_mix_b, v_mlp_w1, v_mlp_w2, v_ln_mlp_g, v_ln_mlp_b):
    args = locals()
    weights = {n: args[n] for n in _WEIGHTS}
    bsz = x.shape[0]
    my_x, my_y, my_c = _my_place()
    me = 4 * my_x + 2 * my_y + my_c

    big = (("ev_in", ev_w_in[0], True), ("ev_out", ev_w_out[0], False), ("mlp0_w1", mlp_w1[0], True),
           ("mlp0_w2", mlp_w2[0], False), ("od_in", od_w_in[0], True), ("od_out", od_w_out[0], False),
           ("mlp1_w1", mlp_w1[1], True), ("mlp1_w2", mlp_w2[1], False))
    small_sharded = (("meta", meta_tokens, F32), ("conv_w", ev_conv_w[0], F32), ("ev_uq", ev_w_uq[0], BF16),
                     ("ev_ukv", ev_w_ukv[0], BF16))
    to_gather = (tuple((nm, s.astype(dt), True) for nm, s, dt in small_sharded)
                 + tuple((nm, s.astype(BF16), cols) for nm, s, cols in big))
    handles = _gather_start_all([s for _, s, _ in to_gather], "ag_start")
    gathers = {nm: (s, cols, h) for (nm, s, cols), h in zip(to_gather, handles)}
    gather_tokens = (handles[0][4],)

    def full_weight(nm, after):
        shard16, cols, handle = gathers[nm]
        land = _exchange_wait(handle, True, after, "ag_wait_" + nm)
        land = lax.dynamic_update_index_in_dim(land, shard16, me, 0)
        if cols and shard16.shape[1] % LANES == 0:
            return land, True
        return (_unstack_cols(land) if cols else land.reshape(-1, shard16.shape[1])), False

    meta_full, conv_w_full, w_uq_full, w_ukv_full = (
        _unstack_cols(lax.dynamic_update_index_in_dim(
            _exchange_wait(gathers[nm][2], True, gather_tokens[-1], "ag_wait_" + nm), gathers[nm][0], me, 0))
        for nm, _, _ in small_sharded)

    pending = []

    def linear_bwd(nm, x_in, w, dy, cols, unpad=None, **fused):
        w_full, w_stacked = w
        if w_stacked:
            stacked = _matmul(x_in, dy, "tn", nm + "_dw", stacked=True)
            own = lax.dynamic_index_in_dim(stacked, me, 0, keepdims=False)
        else:
            dw = _matmul(x_in, dy, "tn", nm + "_dw")
            dw = dw if unpad is None else unpad(dw)
            n = dw.shape[1] // N_DEV
            if cols:
                stacked = _stack_cols(dw)
                own = lax.dynamic_slice_in_dim(dw, me * n, n, axis=1)
            else:
                stacked = dw.reshape(N_DEV, dw.shape[0] // N_DEV, dw.shape[1])
                own = lax.dynamic_index_in_dim(stacked, me, 0, keepdims=False)
        handle = _exchange_start(stacked, (_N_PEERS,) + stacked.shape[1:], False, "rs_start_" + nm)
        pending.append((nm, own, handle))
        return _matmul(dy, w_full, "nt", nm + "_dx", after=(handle[4],), stacked=w_stacked, **fused)

    def linear_fwd(nm, x_in, w, **fused):
        return _matmul(x_in, w[0], "nn", nm + "_fwd", stacked=w[1], **fused)

    def mlp_fwd(h, h16, l):
        w1 = full_weight(f"mlp{l}_w1", h16)
        a16 = linear_fwd(f"mlp{l}_w1", h16, w1, out_relu2=True, out_dtype=BF16)
        w2 = full_weight(f"mlp{l}_w2", a16)
        f = linear_fwd(f"mlp{l}_w2", a16, w2)
        ln_args = (h, f, ln_mlp_g[l:l + 1], ln_mlp_b[l:l + 1])
        return ln_fwd(f"mlp{l}_ln", *ln_args), (h16, w1, a16, w2, ln_args)

    def mlp_bwd(dout, res, l):
        h16, w1, a16, w2, ln_args = res
        dh, df, dg, db = ln_bwd(f"mlp{l}_ln", ln_args, dout)
        du = linear_bwd(f"mlp{l}_w2", a16, w2, df, False, relu2_bwd_of=a16, out_dtype=BF16)
        return (dh, linear_bwd(f"mlp{l}_w1", h16, w1, du, True)), dg, db

    def ln_fwd(nm, h, mix, g, b):
        return _make_rowwise(_ln_res_copy_f, nm, 2, 0, 2).fwd_call(h, mix, g, b)[0]

    def ln_bwd(nm, ln_args, pieces):
        (dh, dmix), (dg, db) = _make_rowwise(_ln_res_f, nm, 2, 0, 2).bwd_call(
            ln_args, ((pieces[0],), ()), more=tuple(pieces[1:]), row_dtypes=(F32, BF16))
        return dh, dmix, dg, db

    h0, vjp_embed = jax.vjp(_embed, meta_full, x)
    n_in = ev_w_in.shape[2] * N_DEV
    kpe0, pad_lo, pad_hi = n_in - MLA_ROPE, MLA_NOPE, LANES - MLA_NOPE - MLA_ROPE
    w_in = full_weight("ev_in", h0)[0]
    zeros_in = jnp.zeros((w_in.shape[0], pad_lo), BF16)
    w_ev_in = (jnp.concatenate([w_in[:, :kpe0], zeros_in, w_in[:, kpe0:], zeros_in[:, :pad_hi]], axis=1), False)

    def unpad_in(dw):
        return jnp.concatenate([dw[:, :kpe0], dw[:, kpe0 + pad_lo:kpe0 + pad_lo + MLA_ROPE]], axis=1)

    p0 = _matmul(h0, w_ev_in[0], "nn", "ev_in_fwd", after=gather_tokens)
    q_rank, d_head = w_uq_full.shape[0], MLA_NOPE + MLA_ROPE
    w_uq_pad = jnp.pad(w_uq_full.reshape(q_rank, MLA_HEADS, d_head), ((0, 0), (0, 0), (0, LANES - d_head)))
    w_uq_pad = w_uq_pad.reshape(q_rank, MLA_HEADS * LANES)
    small = (conv_w_full, ev_conv_b, ev_w_rg_a, ev_b_rg_a, ev_w_rg_x, ev_b_rg_x, ev_lru_lambda, ev_q_norm_g,
             jnp.zeros(w_uq_pad.shape, F32), ev_kv_norm_g, jnp.zeros(w_ukv_full.shape, F32))
    y0, vjp_even = jax.vjp(lambda p, *s: _even_mixer(p, *s, (w_uq_pad, w_ukv_full), bsz), p0, *small)
    w_out = full_weight("ev_out", y0)[0]
    lru_w, d_model = y0.shape[1] - MLA_HEADS * LANES, w_out.shape[1]
    w_att = w_out[lru_w:].reshape(MLA_HEADS, MLA_V, d_model)
    w_att = jnp.concatenate([jnp.zeros((MLA_HEADS, LANES - MLA_V, d_model), BF16), w_att], axis=1)
    w_ev_out = (jnp.concatenate([w_out[:lru_w], w_att.reshape(MLA_HEADS * LANES, d_model)], axis=0), False)

    def unpad_out(dw):
        d_att = dw[lru_w:].reshape(MLA_HEADS, LANES, d_model)[:, LANES - MLA_V:].reshape(MLA_HEADS * MLA_V, d_model)
        return jnp.concatenate([dw[:lru_w], d_att], axis=0)

    mix0 = linear_fwd("ev_out", y0, w_ev_out)
    ln0_args = (h0, mix0, ln_mix_g[0:1], ln_mix_b[0:1])
    h1, h1_16 = ln_fwd("mix0_ln", *ln0_args)
    (h2, h2_16), res_mlp0 = mlp_fwd(h1, h1_16, 0)
    w_od_in = full_weight("od_in", h2_16)
    p1 = linear_fwd("od_in", h2_16, w_od_in)
    y1, res_odd = _odd_mixer_fwd(p1, bsz)
    w_od_out = full_weight("od_out", y1)
    mix1 = linear_fwd("od_out", y1, w_od_out)
    ln1_args = (h2, mix1, ln_mix_g[1:2], ln_mix_b[1:2])
    h3, h3_16 = ln_fwd("mix1_ln", *ln1_args)
    (h4, _), res_mlp1 = mlp_fwd(h3, h3_16, 1)
    loss_local, vjp_loss = jax.vjp(lambda h: _local_loss(h, loss_target), h4)

    dh4 = vjp_loss(jnp.ones((), F32))
    dh3, dg_mlp1, db_mlp1 = mlp_bwd(dh4, res_mlp1, 1)
    dh2, dmix1, dg_mix1, db_mix1 = ln_bwd("mix1_ln", ln1_args, dh3)
    dp1 = _odd_mixer_bwd(res_odd, linear_bwd("od_out", y1, w_od_out, dmix1, False))
    dh2 = (dh2, linear_bwd("od_in", h2_16, w_od_in, dp1, True))
    dh1, dg_mlp0, db_mlp0 = mlp_bwd(dh2, res_mlp0, 0)
    dh0, dmix0, dg_mix0, db_mix0 = ln_bwd("mix0_ln", ln0_args, dh1)
    dp0, *dsmall = vjp_even(linear_bwd("ev_out", y0, w_ev_out, dmix0, False, unpad=unpad_out))
    dh0 = dh0 + linear_bwd("ev_in", h0, w_ev_in, dp0.astype(BF16), True, unpad=unpad_in)
    g_meta_full, grad_x = vjp_embed(dh0)
    (g_conv_w_full, g_conv_b, g_w_rg_a, g_b_rg_a, g_w_rg_x, g_b_rg_x, g_lambda, g_q_norm, g_uq_pad, g_kv_norm,
     g_ukv_full) = dsmall

    for nm, dw in (("ev_uq", g_uq_pad.reshape(q_rank, MLA_HEADS, LANES)[:, :, :d_head].reshape(q_rank, -1)),
                   ("ev_ukv", g_ukv_full)):
        n = dw.shape[1] // N_DEV
        handle = _exchange_start(_stack_cols(dw), (_N_PEERS, dw.shape[0], n), False, "rs_start_" + nm)
        pending.append((nm, lax.dynamic_slice_in_dim(dw, me * n, n, axis=1), handle))

    rep_names = ("ev_conv_b", "ev_w_rg_a", "ev_b_rg_a", "ev_w_rg_x", "ev_b_rg_x", "ev_lru_lambda", "ev_q_norm_g",
                 "ev_kv_norm_g", "ln_mix_g", "ln_mix_b", "ln_mlp_g", "ln_mlp_b")
    rep_local = (g_conv_b, g_w_rg_a, g_b_rg_a, g_w_rg_x, g_b_rg_x, g_lambda, g_q_norm, g_kv_norm,
                 jnp.concatenate([dg_mix0, dg_mix1]), jnp.concatenate([db_mix0, db_mix1]),
                 jnp.concatenate([dg_mlp0, dg_mlp1]), jnp.concatenate([db_mlp0, db_mlp1]), g_conv_w_full, g_meta_full)
    rep_packed = _pack_rows(rep_local)
    rep_handle = _exchange_start(rep_packed, (N_DEV,) + rep_packed.shape, True, "rep_start", after=(grad_x,))

    after, summed = rep_handle[4], {}
    for nm, own, handle in pending:
        land = _exchange_wait(handle, False, after, "rs_wait_" + nm)
        summed[nm] = after = _sum_own_and_peers(own, land, "rs_sum_" + nm)

    rep_land = lax.dynamic_update_index_in_dim(_exchange_wait(rep_handle, True, after, "rep_wait"), rep_packed, me, 0)
    *rep_total, t_conv_w, t_meta = _unpack_rows(_sum_blocks(rep_land, "rep_sum"), rep_local)
    grad_w = dict(zip(rep_names, rep_total))
    n_conv, n_meta = ev_conv_w.shape[2], meta_tokens.shape[1]
    grad_w.update(meta_tokens=lax.dynamic_slice_in_dim(t_meta, me * n_meta, n_meta, axis=1),
                  ev_conv_w=lax.dynamic_slice_in_dim(t_conv_w, me * n_conv, n_conv, axis=1)[None],
                  ev_w_uq=summed["ev_uq"][None], ev_w_ukv=summed["ev_ukv"][None])
    grad_w.update(ev_w_in=summed["ev_in"][None], ev_w_out=summed["ev_out"][None], od_w_in=summed["od_in"][None],
                  od_w_out=summed["od_out"][None], mlp_w1=jnp.stack([summed["mlp0_w1"], summed["mlp1_w1"]]),
                  mlp_w2=jnp.stack([summed["mlp0_w2"], summed["mlp1_w2"]]))

    loss = lax.psum(loss_local, MESH_AXES)
    delta, new_m, new_v = {}, {}, {}
    for n in _WEIGHTS:
        w, g, m, v = weights[n], grad_w[n], args["m_" + n], args["v_" + n]
        two_d = (-1, w.shape[-1])
        d2, m2, v2 = _adamw(w.reshape(two_d), g.reshape(two_d), m.reshape(two_d), v.reshape(two_d), "adamw_" + n)
        delta[n], new_m[n], new_v[n] = d2.reshape(w.shape), m2.reshape(w.shape), v2.reshape(w.shape)
    return (loss, grad_x, *[grad_w[n] for n in _WEIGHTS], *[delta[n] for n in _WEIGHTS],
            *[new_m[n] for n in _WEIGHTS], *[new_v[n] for n in _WEIGHTS])
```

```python
import functools
import math

import jax
import jax.numpy as jnp
from jax import lax
from jax.experimental import pallas as pl
from jax.experimental.pallas import tpu as pltpu

F32 = jnp.float32
BF16 = jnp.bfloat16

N_DEV = 8
MESH_AXES = ("x", "y", "c")
LANES = 128
SEQ_BLOCK = 128

N_META = 16
LRU_C = 8.0
MLA_HEADS = 8
MLA_NOPE = 64
MLA_ROPE = 32
MLA_V = 64
RET_HEADS = 4
ROPE_BASE = 10000.0
DEPTH = 2
DN_ALPHA = (2 * DEPTH) ** 0.25
EPS = 1e-5
NEG_INF = -1e30

ADAM_LR = 0.001
ADAM_B1 = 0.9
ADAM_B2 = 0.999
ADAM_EPS = 1e-08
ADAM_WD = 0.01
ADAM_STEP = 10

VMEM_LIMIT = 56 * 1024 * 1024


def _params(*sem):
    return pltpu.CompilerParams(dimension_semantics=sem, vmem_limit_bytes=VMEM_LIMIT)


def _pick(n, cands):
    for c in cands:
        if n % c == 0:
            return c
    return n


def _row_tile(r, width):
    cands = (256, 128, 64, 32, 16, 8) if width <= 1024 else (128, 64, 32, 16, 8)
    return _pick(r, cands)


_DIMS = {"nn": (((1,), (0,)), ((), ())), "nt": (((1,), (1,)), ((), ())), "tn": (((0,), (0,)), ((), ()))}


def _dot(a, b, mode):
    return lax.dot_general(a.astype(BF16), b.astype(BF16), _DIMS[mode], preferred_element_type=F32)


def _matmul(a, b, mode, name, after=(), stacked=False, relu2_bwd_of=None, out_dtype=F32, out_relu2=False):
    if stacked:
        n_blk = b.shape[2] if mode != "tn" else b.shape[1] // N_DEV
    if mode == "nn":
        (m, k), n = a.shape, (N_DEV * n_blk if stacked else b.shape[1])
    elif mode == "nt":
        (m, k), n = a.shape, (b.shape[1] if stacked else b.shape[0])
    else:
        (k, m), n = a.shape, b.shape[1]
    tm = _pick(m, (1088, 1024, 544, 512, 272, 256, 128, 64, 32, 16, 8))
    tn = _pick(n, (1024, 512, 256, 128))
    tk = _pick(k, (1088, 1024, 544, 512, 272, 256, 128))
    kb = 2
    if stacked and mode == "nn":
        tn = n_blk
    if stacked and mode == "tn":
        tn = kb * n_blk
    if stacked and mode == "nt":
        tk = kb * n_blk
    nk = k // tk
    assert out_dtype == F32 or (nk == 1 and not (stacked and mode == "tn")), "narrow results need a single k step"
    assert not out_relu2 or nk == 1, "relu^2 is applied to a finished tile"

    out_spec = pl.BlockSpec((tm, tn), lambda i, j, kk: (i, j))
    out_shape = jax.ShapeDtypeStruct((m, n), out_dtype)
    if mode == "nn":
        a_spec = pl.BlockSpec((tm, tk), lambda i, j, kk: (i, kk))
        b_spec = pl.BlockSpec((tk, tn), lambda i, j, kk: (kk, j))
        if stacked:
            b_spec = pl.BlockSpec((None, tk, tn), lambda i, j, kk: (j, kk, 0))
    elif mode == "nt":
        a_spec = pl.BlockSpec((tm, tk), lambda i, j, kk: (i, kk))
        b_spec = pl.BlockSpec((tn, tk), lambda i, j, kk: (j, kk))
        if stacked:
            b_spec = pl.BlockSpec((kb, tn, n_blk), lambda i, j, kk: (kk, j, 0))
    else:
        a_spec = pl.BlockSpec((tk, tm), lambda i, j, kk: (kk, i))
        b_spec = pl.BlockSpec((tk, tn), lambda i, j, kk: (kk, j))
        if stacked:
            out_spec = pl.BlockSpec((kb, tm, n_blk), lambda i, j, kk: (j, i, 0))
            out_shape = jax.ShapeDtypeStruct((N_DEV, m, n_blk), F32)
    extra = [] if relu2_bwd_of is None else [relu2_bwd_of]
    extra_specs = [pl.BlockSpec((tm, tn), lambda i, j, kk: (i, j))] * len(extra)

    def body(a_ref, b_ref, *rest):
        def relu2_slope():
            return 2.0 * jnp.sqrt(rest[0][...].astype(F32))

        o_ref = rest[-1]
        kk = pl.program_id(2)
        av = a_ref[...]
        if stacked and mode == "nt":
            part = _dot(av[:, :n_blk], b_ref[0], mode)
            for q in range(1, kb):
                part = part + _dot(av[:, q * n_blk:(q + 1) * n_blk], b_ref[q], mode)
        else:
            part = _dot(av, b_ref[...], mode)
        if stacked and mode == "tn":
            part = jnp.stack([part[:, q * n_blk:(q + 1) * n_blk] for q in range(kb)])
        if nk == 1:
            if out_relu2:
                part = jnp.maximum(part, 0.0)
                part = part * part
            if relu2_bwd_of is not None:
                part = part * relu2_slope()
            o_ref[...] = part.astype(out_dtype)
            return

        @pl.when(kk == 0)
        def _():
            o_ref[...] = part

        @pl.when(kk != 0)
        def _():
            o_ref[...] += part

        if relu2_bwd_of is not None:
            @pl.when(kk == nk - 1)
            def _():
                o_ref[...] *= relu2_slope()

    return pl.pallas_call(
        body,
        name=name,
        grid=(m // tm, n // tn, nk),
        in_specs=[a_spec, b_spec] + extra_specs + [pl.BlockSpec(memory_space=pl.ANY)] * len(after),
        out_specs=out_spec,
        out_shape=out_shape,
        compiler_params=_params("parallel", "parallel", "arbitrary"),
    )(a, b, *extra, *after)


def _group_matmul(a, w, mode, name):
    if mode in ("nn", "nt"):
        g, dk, dn = w.shape
        m = a.shape[0]
        d_in, d_out = (dk, dn) if mode == "nn" else (dn, dk)
        tm = _pick(m, (1088, 1024, 544, 512, 272, 256, 128, 64, 32, 16, 8))

        def body(a_ref, w_ref, o_ref):
            o_ref[...] = _dot(a_ref[...], w_ref[0], mode)

        return pl.pallas_call(
            body,
            name=name,
            grid=(g, m // tm),
            in_specs=[pl.BlockSpec((tm, d_in), lambda h, i: (i, h)), pl.BlockSpec((1, dk, dn), lambda h, i: (h, 0, 0))],
            out_specs=pl.BlockSpec((tm, d_out), lambda h, i: (i, h)),
            out_shape=jax.ShapeDtypeStruct((m, g * d_out), F32),
            compiler_params=_params("parallel", "parallel"),
        )(a, w)
    b = w
    m = a.shape[0]
    dk = dn = LANES
    g = a.shape[1] // dk
    tm = _pick(m, (1088, 1024, 544, 512, 272, 256, 128, 64, 32, 16, 8))

    def body(a_ref, b_ref, o_ref):
        part = _dot(a_ref[...], b_ref[...], "tn")

        @pl.when(pl.program_id(1) == 0)
        def _():
            o_ref[0] = part

        @pl.when(pl.program_id(1) != 0)
        def _():
            o_ref[0] += part

    return pl.pallas_call(
        body,
        name=name,
        grid=(g, m // tm),
        in_specs=[pl.BlockSpec((tm, dk), lambda h, i: (i, h)), pl.BlockSpec((tm, dn), lambda h, i: (i, h))],
        out_specs=pl.BlockSpec((1, dk, dn), lambda h, i: (h, 0, 0)),
        out_shape=jax.ShapeDtypeStruct((g, dk, dn), F32),
        compiler_params=_params("parallel", "arbitrary"),
    )(a, b)


def _make_group_linear(name):
    @jax.custom_vjp
    def op(x, w):
        return _group_matmul(x, w, "nn", name + "_fwd")

    def fwd(x, w):
        return op(x, w), (x, w)

    def bwd(res, dy):
        x, w = res
        return _group_matmul(dy, w, "nt", name + "_dx"), _group_matmul(x, dy, "tn", name + "_dw")

    op.defvjp(fwd, bwd)
    return op


def _my_place():
    return lax.axis_index("x"), lax.axis_index("y"), lax.axis_index("c")


def _all_gather(shard, name, after=()):
    shape, dtype = shard.shape, shard.dtype

    def body(x_ref, *rest):
        out_ref, send_sems, recv_sems, local_sem = rest[len(after):]
        x, y, c = _my_place()
        me, sibling = (x, y, c), (x, y, 1 - c)
        chips = [(1 - x, y), (x, 1 - y), (1 - x, 1 - y)]

        def slot(px, py, pc):
            return out_ref.at[4 * px + 2 * py + pc]

        def copy(k, block, to, src=None):
            return pltpu.make_async_remote_copy(
                src_ref=slot(*block) if src is None else src,
                dst_ref=slot(*block),
                send_sem=send_sems.at[k],
                recv_sem=recv_sems.at[k],
                device_id=to,
                device_id_type=pl.DeviceIdType.MESH,
            )

        mine = pltpu.make_async_copy(x_ref, slot(*me), local_sem)
        mine.start()
        first = [copy(0, me, sibling, src=x_ref)]
        first += [copy(1 + j, me, (*chip, c), src=x_ref) for j, chip in enumerate(chips)]
        for cp in first:
            cp.start()
        passed = [copy(4 + j, (*chip, c), sibling) for j, chip in enumerate(chips)]
        for j, chip in enumerate(chips):
            copy(1 + j, (*chip, c), me).wait_recv()
            passed[j].start()
        copy(0, sibling, me).wait_recv()
        for j, chip in enumerate(chips):
            copy(4 + j, (*chip, 1 - c), me).wait_recv()
        for cp in first + passed:
            cp.wait_send()
        mine.wait()

    return pl.pallas_call(
        body,
        name=name,
        out_shape=jax.ShapeDtypeStruct((N_DEV,) + shape, dtype),
        in_specs=[pl.BlockSpec(memory_space=pl.ANY)] * (1 + len(after)),
        out_specs=pl.BlockSpec(memory_space=pl.ANY),
        scratch_shapes=[pltpu.SemaphoreType.DMA((7,)), pltpu.SemaphoreType.DMA((7,)), pltpu.SemaphoreType.DMA],
    )(shard, *after)


def _all_to_all(stacked, name, after=()):
    def body(x_ref, *rest):
        out_ref, send_sems, recv_sems, local_sem = rest[len(after):]
        x, y, c = _my_place()
        me = 4 * x + 2 * y + c
        mine = pltpu.make_async_copy(x_ref.at[me], out_ref.at[me], local_sem)
        mine.start()
        copies = []
        for k in range(1, N_DEV):
            px, py, pc = x ^ ((k >> 2) & 1), y ^ ((k >> 1) & 1), c ^ (k & 1)
            peer = 4 * px + 2 * py + pc
            copies.append(
                pltpu.make_async_remote_copy(
                    src_ref=x_ref.at[peer],
                    dst_ref=out_ref.at[me],
                    send_sem=send_sems.at[k - 1],
                    recv_sem=recv_sems.at[k - 1],
                    device_id=(px, py, pc),
                    device_id_type=pl.DeviceIdType.MESH,
                )
            )
        for cp in copies:
            cp.start()
        for cp in copies:
            cp.wait_recv()
        for cp in copies:
            cp.wait_send()
        mine.wait()

    return pl.pallas_call(
        body,
        name=name,
        out_shape=jax.ShapeDtypeStruct(stacked.shape, stacked.dtype),
        in_specs=[pl.BlockSpec(memory_space=pl.ANY)] * (1 + len(after)),
        out_specs=pl.BlockSpec(memory_space=pl.ANY),
        scratch_shapes=[pltpu.SemaphoreType.DMA((7,)), pltpu.SemaphoreType.DMA((7,)), pltpu.SemaphoreType.DMA],
    )(stacked, *after)


def _sum_blocks(stacked, name):
    _, r, c = stacked.shape
    tr = _pick(r, (256, 128, 64, 32, 16, 8))

    def body(x_ref, o_ref):
        s = [x_ref[j] for j in range(N_DEV)]
        o_ref[...] = ((s[0] + s[1]) + (s[2] + s[3])) + ((s[4] + s[5]) + (s[6] + s[7]))

    return pl.pallas_call(
        body,
        name=name,
        grid=(r // tr,),
        in_specs=[pl.BlockSpec((N_DEV, tr, c), lambda i: (0, i, 0))],
        out_specs=pl.BlockSpec((tr, c), lambda i: (i, 0)),
        out_shape=jax.ShapeDtypeStruct((r, c), stacked.dtype),
        compiler_params=_params("parallel"),
    )(stacked)


def _stack_cols(full):
    k, n8 = full.shape
    return full.reshape(k, N_DEV, n8 // N_DEV).transpose(1, 0, 2)


def _unstack_cols(stacked):
    j, k, n = stacked.shape
    return stacked.transpose(1, 0, 2).reshape(k, j * n)


def _split_cols(p, cuts):
    bounds = (0,) + tuple(cuts) + (p.shape[1],)

    @jax.custom_vjp
    def op(z):
        return tuple(z[:, lo:hi] for lo, hi in zip(bounds[:-1], bounds[1:]))

    op.defvjp(lambda z: (op(z), None), lambda _, cots: (jnp.concatenate(cots, axis=1),))
    return op(p)


def _gather_weight(shard, cols, name):
    g = _all_gather(shard.astype(BF16), name)
    return _unstack_cols(g) if cols else g.reshape(-1, shard.shape[1])


def _scatter_grad(full, cols, name, after=()):
    if cols:
        st = _stack_cols(full)
    else:
        st = full.reshape(N_DEV, full.shape[0] // N_DEV, full.shape[1])
    return _sum_blocks(_all_to_all(st, name + "_a2a", after), name + "_sum")


def _make_slot_linear(name):
    @jax.custom_vjp
    def op(x, w_full, slot):
        return _matmul(x, w_full, "nn", name + "_fwd")

    def fwd(x, w_full, slot):
        return op(x, w_full, slot), (x, w_full)

    def bwd(res, dy):
        x, w = res
        return _matmul(dy, w, "nt", name + "_dx"), jnp.zeros_like(w), _matmul(x, dy, "tn", name + "_dw")

    op.defvjp(fwd, bwd)
    return op


def _pack_rows(gs):
    flat = jnp.concatenate([g.reshape(-1) for g in gs])
    n = flat.shape[0]
    rows = -(-n // (256 * LANES)) * 256
    return jnp.pad(flat, (0, rows * LANES - n)).reshape(rows, LANES)


def _unpack_rows(packed, like):
    flat, out, off = packed.reshape(-1), [], 0
    for g in like:
        out.append(flat[off:off + g.size].reshape(g.shape))
        off += g.size
    return out


_HBM = pl.BlockSpec(memory_space=pltpu.HBM)
_SEM = pl.BlockSpec(memory_space=pltpu.SEMAPHORE)
_SIDE_EFFECT = pltpu.SideEffectType.DATAFLOW_SIDE_EFFECTING
_N_PEERS = N_DEV - 1


def _peer(k):
    x, y, c = _my_place()
    return x ^ ((k >> 2) & 1), y ^ ((k >> 1) & 1), c ^ (k & 1)


def _exchange_start(src, land_shape, gather, name, after=()):
    def body(src_ref, land_ref, *rest):
        send_sems, recv_sems, src_thru, land_thru, token = rest[len(after):]
        x, y, c = _my_place()
        me = 4 * x + 2 * y + c
        for k in range(1, N_DEV):
            px, py, pc = _peer(k)
            pltpu.make_async_remote_copy(
                src_ref=src_ref if gather else src_ref.at[4 * px + 2 * py + pc],
                dst_ref=land_ref.at[me] if gather else land_ref.at[k - 1],
                send_sem=send_sems.at[k - 1],
                recv_sem=recv_sems.at[k - 1],
                device_id=(px, py, pc),
                device_id_type=pl.DeviceIdType.MESH,
            ).start()
        token[...] = jnp.zeros_like(token)

    return pl.pallas_call(
        body,
        name=name,
        out_shape=(
            pltpu.SemaphoreType.DMA((_N_PEERS,)),
            pltpu.SemaphoreType.DMA((_N_PEERS,)),
            pltpu.HBM(src.shape, src.dtype),
            pltpu.HBM(land_shape, src.dtype),
            jax.ShapeDtypeStruct((8, LANES), F32),
        ),
        in_specs=(_HBM, _HBM) + (pl.BlockSpec(memory_space=pl.ANY),) * len(after),
        out_specs=(_SEM, _SEM, _HBM, _HBM, pl.BlockSpec(memory_space=pltpu.VMEM)),
        input_output_aliases={0: 2, 1: 3},
        compiler_params=pltpu.CompilerParams(has_side_effects=_SIDE_EFFECT),
    )(pltpu.with_memory_space_constraint(src, pltpu.HBM),
      pltpu.with_memory_space_constraint(lax.empty(land_shape, src.dtype), pltpu.HBM), *after)


def _gather_start_all(shards, name):
    n = len(shards)

    def body(*refs):
        srcs, lands = refs[:n], refs[n:2 * n]
        outs = refs[2 * n:]
        send_sems, recv_sems, token = outs[:n], outs[n:2 * n], outs[-1]
        x, y, c = _my_place()
        me = 4 * x + 2 * y + c
        for i in range(n):
            for k in range(1, N_DEV):
                pltpu.make_async_remote_copy(
                    src_ref=srcs[i],
                    dst_ref=lands[i].at[me],
                    send_sem=send_sems[i].at[k - 1],
                    recv_sem=recv_sems[i].at[k - 1],
                    device_id=_peer(k),
                    device_id_type=pl.DeviceIdType.MESH,
                ).start()
        token[...] = jnp.zeros_like(token)

    lands = [(N_DEV,) + s.shape for s in shards]
    sems = tuple(pltpu.SemaphoreType.DMA((_N_PEERS,)) for _ in range(2 * n))
    res = pl.pallas_call(
        body,
        name=name,
        out_shape=sems + tuple(pltpu.HBM(s.shape, s.dtype) for s in shards)
        + tuple(pltpu.HBM(ls, s.dtype) for ls, s in zip(lands, shards)) + (jax.ShapeDtypeStruct((8, LANES), F32),),
        in_specs=(_HBM,) * (2 * n),
        out_specs=(_SEM,) * (2 * n) + (_HBM,) * (2 * n) + (pl.BlockSpec(memory_space=pltpu.VMEM),),
        input_output_aliases={i: 2 * n + i for i in range(2 * n)},
        compiler_params=pltpu.CompilerParams(has_side_effects=_SIDE_EFFECT),
    )(*[pltpu.with_memory_space_constraint(s, pltpu.HBM) for s in shards],
      *[pltpu.with_memory_space_constraint(lax.empty(ls, s.dtype), pltpu.HBM) for ls, s in zip(lands, shards)])
    return [(res[i], res[n + i], res[2 * n + i], res[3 * n + i], res[-1]) for i in range(n)]


def _exchange_wait(handle, gather, after, name):
    send_sems, recv_sems, src_thru, land_thru, _ = handle

    def body(src_ref, land_ref, send_sems, recv_sems, after_ref, src_dead, got_ref):
        for k in range(1, N_DEV):
            cp = pltpu.make_async_remote_copy(
                src_ref=src_ref if gather else src_ref.at[k],
                dst_ref=land_ref.at[k - 1],
                send_sem=send_sems.at[k - 1],
                recv_sem=recv_sems.at[k - 1],
                device_id=_peer(k),
                device_id_type=pl.DeviceIdType.MESH,
            )
            cp.wait_send()
            cp.wait_recv()

    return pl.pallas_call(
        body,
        name=name,
        out_shape=(pltpu.HBM(src_thru.shape, src_thru.dtype), pltpu.HBM(land_thru.shape, land_thru.dtype)),
        in_specs=(_HBM, _HBM, _SEM, _SEM, pl.BlockSpec(memory_space=pl.ANY)),
        out_specs=(_HBM, _HBM),
        input_output_aliases={0: 0, 1: 1},
        compiler_params=pltpu.CompilerParams(has_side_effects=_SIDE_EFFECT),
    )(src_thru, land_thru, send_sems, recv_sems, after)[1]


def _sum_own_and_peers(own, land, name):
    r, c = own.shape
    tr = _pick(r, (256, 128, 64, 32, 16, 8))

    def body(o_ref, l_ref, out_ref):
        s = [l_ref[j] for j in range(_N_PEERS)]
        out_ref[...] = ((o_ref[...] + s[0]) + (s[1] + s[2])) + ((s[3] + s[4]) + (s[5] + s[6]))

    return pl.pallas_call(
        body,
        name=name,
        grid=(r // tr,),
        in_specs=[pl.BlockSpec((tr, c), lambda i: (i, 0)), pl.BlockSpec((_N_PEERS, tr, c), lambda i: (0, i, 0))],
        out_specs=pl.BlockSpec((tr, c), lambda i: (i, 0)),
        out_shape=jax.ShapeDtypeStruct((r, c), own.dtype),
        compiler_params=_params("parallel"),
    )(own, land)


class _Cols:
    def __init__(self, array, width, block):
        self.array, self.width, self.block = array, width, block
        self.shape, self.dtype = (array.shape[0], width), array.dtype


def _base(a):
    return a.array if isinstance(a, _Cols) else a


def _col_block(a):
    return a.block if isinstance(a, _Cols) else 0


def _make_rowwise(f, name, n_rows, n_tabs, n_pars):
    n_in = n_rows + n_tabs + n_pars

    def specs(args, tm):
        blocked = [pl.BlockSpec((tm, a.shape[1]), lambda i, blk=_col_block(a): (i, blk)) for a in args[: n_rows + n_tabs]]
        whole = [pl.BlockSpec(a.shape, lambda i: (0, 0)) for a in args[n_rows + n_tabs:]]
        return blocked + whole

    def out_struct(args, tm):
        blk = [jax.ShapeDtypeStruct((tm, a.shape[1]), a.dtype) for a in args[: n_rows + n_tabs]]
        blk += [jax.ShapeDtypeStruct(a.shape, a.dtype) for a in args[n_rows + n_tabs:]]
        return jax.eval_shape(f, *blk)

    def fwd_call(*args):
        r = args[0].shape[0]
        tm = _row_tile(r, max(a.shape[1] for a in args[:n_rows]))
        ro, so = out_struct(args, tm)

        def body(*refs):
            vals = [x[...] for x in refs[:n_in]]
            outs = refs[n_in:]
            rv, sv = f(*vals)
            for o, v in zip(outs[: len(ro)], rv):
                o[...] = v
            for o, v in zip(outs[len(ro):], sv):
                @pl.when(pl.program_id(0) == 0)
                def _(o=o, v=v):
                    o[...] = v

                @pl.when(pl.program_id(0) != 0)
                def _(o=o, v=v):
                    o[...] += v

        out_shape = [jax.ShapeDtypeStruct((r, s.shape[1]), s.dtype) for s in ro]
        out_shape += [jax.ShapeDtypeStruct(s.shape, s.dtype) for s in so]
        out_specs = [pl.BlockSpec((tm, s.shape[1]), lambda i: (i, 0)) for s in ro]
        out_specs += [pl.BlockSpec(s.shape, lambda i: (0, 0)) for s in so]
        res = pl.pallas_call(
            body,
            name=name + "_fwd",
            grid=(r // tm,),
            in_specs=specs(args, tm),
            out_specs=out_specs,
            out_shape=out_shape,
            compiler_params=_params("arbitrary" if so else "parallel"),
        )(*[_base(a) for a in args])
        return tuple(res[: len(ro)]), tuple(res[len(ro):])

    def bwd_call(args, cots, more=(), row_dtypes=None):
        r = args[0].shape[0]
        tm = _row_tile(r, max(a.shape[1] for a in args[:n_rows]))
        ro, so = out_struct(args, tm)
        crow, csum = cots
        rows, tabs, pars = args[:n_rows], args[n_rows:n_rows + n_tabs], args[n_rows + n_tabs:]
        n_c = len(crow) + len(csum)

        def body(*refs):
            vals = [x[...] for x in refs[:n_in]]
            cv = [x[...] for x in refs[n_in:n_in + n_c]]
            for x in refs[n_in + n_c:n_in + n_c + len(more)]:
                cv[0] = cv[0] + x[...]
            outs = refs[n_in + n_c + len(more):]
            tv = vals[n_rows:n_rows + n_tabs]

            def g(*dargs):
                return f(*dargs[:n_rows], *tv, *dargs[n_rows:])

            _, vjp = jax.vjp(g, *vals[:n_rows], *vals[n_rows + n_tabs:])
            d = vjp((tuple(cv[: len(crow)]), tuple(cv[len(crow):])))
            for o, v in zip(outs[:n_rows], d[:n_rows]):
                o[...] = v.astype(o.dtype)
            for o, v in zip(outs[n_rows:], d[n_rows:]):
                @pl.when(pl.program_id(0) == 0)
                def _(o=o, v=v):
                    o[...] = v

                @pl.when(pl.program_id(0) != 0)
                def _(o=o, v=v):
                    o[...] += v

        in_specs = specs(args, tm)
        in_specs += [pl.BlockSpec((tm, c.shape[1]), lambda i: (i, 0)) for c in crow]
        in_specs += [pl.BlockSpec(c.shape, lambda i: (0, 0)) for c in csum]
        in_specs += [pl.BlockSpec((tm, c.shape[1]), lambda i: (i, 0)) for c in more]
        out_shape = [jax.ShapeDtypeStruct(a.shape, dt) for a, dt in zip(rows, row_dtypes or [a.dtype for a in rows])]
        out_shape += [jax.ShapeDtypeStruct(a.shape, a.dtype) for a in pars]
        out_specs = [pl.BlockSpec((tm, a.shape[1]), lambda i: (i, 0)) for a in rows]
        out_specs += [pl.BlockSpec(a.shape, lambda i: (0, 0)) for a in pars]
        res = pl.pallas_call(
            body,
            name=name + "_bwd",
            grid=(r // tm,),
            in_specs=in_specs,
            out_specs=out_specs,
            out_shape=out_shape,
            compiler_params=_params("arbitrary" if pars else "parallel"),
        )(*[_base(a) for a in args], *crow, *csum, *more)
        return tuple(res[:n_rows]), tuple(res[n_rows:])

    @jax.custom_vjp
    def op(rows, tabs, pars):
        return fwd_call(*rows, *tabs, *pars)

    op.fwd_call, op.bwd_call = fwd_call, bwd_call

    def fwd(rows, tabs, pars):
        return fwd_call(*rows, *tabs, *pars), (rows, tabs, pars)

    def bwd(res, cots):
        rows, tabs, pars = res
        drows, dpars = bwd_call(tuple(rows) + tuple(tabs) + tuple(pars), cots)
        return drows, tuple(jnp.zeros_like(t) for t in tabs), dpars

    op.defvjp(fwd, bwd)
    return op


def _sigmoid(x):
    return 0.5 * (jnp.tanh(0.5 * x) + 1.0)


@jax.custom_jvp
def _softplus(x):
    e = jnp.exp(-jnp.abs(x))
    u = 1.0 + e
    log1p_e = jnp.where(u == 1.0, e, e * jnp.log(u) / jnp.where(u == 1.0, 1.0, u - 1.0))
    return jnp.maximum(x, 0.0) + log1p_e


@_softplus.defjvp
def _softplus_jvp(primals, tangents):
    (x,), (t,) = primals, tangents
    return _softplus(x), t * _sigmoid(x)


def _gelu(x):
    return 0.5 * x * (1.0 + jnp.tanh(math.sqrt(2.0 / math.pi) * (x + 0.044715 * (x * x * x))))


def _ln_res_f(h, mix, g, b):
    z = DN_ALPHA * h + mix
    mu = jnp.mean(z, axis=-1, keepdims=True)
    zc = z - mu
    var = jnp.mean(zc * zc, axis=-1, keepdims=True)
    return (zc * lax.rsqrt(var + EPS) * g + b,), ()


def _ln_res_copy_f(h, mix, g, b):
    (out,), _ = _ln_res_f(h, mix, g, b)
    return (out, out.astype(BF16)), ()


def _rmsnorm_f(x, g):
    return (x * lax.rsqrt(jnp.mean(x * x, axis=-1, keepdims=True) + EPS) * g,), ()


def _lru_gates_f(ga, gx, xc, b_a, b_x, lam):
    r = _sigmoid(ga + b_a)
    i = _sigmoid(gx + b_x)
    log_a = -LRU_C * r * _softplus(-lam)
    a = jnp.exp(log_a)
    one_minus_a2 = jnp.tanh(-log_a) * (jnp.exp(2.0 * log_a) + 1.0)
    return (a, jnp.sqrt(one_minus_a2) * (i * xc)), ()


def _lru_out_f(hh, p_gate):
    return (hh * _gelu(p_gate),), ()


def _rope_ret_f(q, k, cos2, sin2):
    d = cos2.shape[1]
    half = d // 2
    k_scale = d ** -0.5

    def rope(x):
        outs = []
        for h in range(x.shape[1] // d):
            xh = x[:, h * d:(h + 1) * d]
            rot = jnp.concatenate([xh[:, half:], xh[:, :half]], axis=1)
            outs.append(xh * cos2 + rot * sin2)
        return jnp.concatenate(outs, axis=1)

    return (rope(q), rope(k) * k_scale), ()


def _ret_out_f(o, g):
    d = o.shape[1] // RET_HEADS
    outs = []
    for h in range(RET_HEADS):
        oh = o[:, h * d:(h + 1) * d]
        outs.append(oh * lax.rsqrt(jnp.mean(oh * oh, axis=-1, keepdims=True) + EPS))
    y = jnp.concatenate(outs, axis=1)
    return (g * _sigmoid(g) * y,), ()


def _ret_out_bf16_f(o, g):
    (y,), _ = _ret_out_f(o, g)
    return (y.astype(BF16),), ()


def _loss_f(y, t, mask):
    e = (y - t) * mask
    per_row = jnp.sum(e * e, axis=-1, keepdims=True) * (0.5 / y.shape[1])
    total = jnp.sum(per_row, axis=0, keepdims=True)
    return (), (jnp.broadcast_to(total, (1, LANES)),)


def _shift_down(x, s):
    if s == 0:
        return x
    t = x.shape[0]
    row = lax.broadcasted_iota(jnp.int32, x.shape, 0)
    return jnp.where(row >= s, pltpu.roll(x, s, 0), 0.0)


def _shift_up(x, s):
    if s == 0:
        return x
    t = x.shape[0]
    row = lax.broadcasted_iota(jnp.int32, x.shape, 0)
    return jnp.where(row < t - s, pltpu.roll(x, t - s, 0), 0.0)


def _conv_fwd(x, w, b, name):
    bsz, t, c = x.shape
    width = w.shape[0]

    def body(x_ref, w_ref, b_ref, y_ref):
        xv = x_ref[0]
        acc = jnp.broadcast_to(b_ref[...], xv.shape)
        for k in range(width):
            acc = acc + w_ref[k:k + 1, :] * _shift_down(xv, width - 1 - k)
        y_ref[0] = acc

    return pl.pallas_call(
        body,
        name=name,
        grid=(bsz, c // LANES),
        in_specs=[
            pl.BlockSpec((1, t, LANES), lambda i, j: (i, 0, j)),
            pl.BlockSpec((width, LANES), lambda i, j: (0, j)),
            pl.BlockSpec((1, LANES), lambda i, j: (0, j)),
        ],
        out_specs=pl.BlockSpec((1, t, LANES), lambda i, j: (i, 0, j)),
        out_shape=jax.ShapeDtypeStruct(x.shape, F32),
        compiler_params=_params("parallel", "parallel"),
    )(x, w, b)


def _conv_bwd(x, w, dy, name):
    bsz, t, c = x.shape
    width = w.shape[0]

    def body(x_ref, w_ref, dy_ref, dx_ref, dw_ref, db_ref):
        xv, g = x_ref[0], dy_ref[0]
        dx = jnp.zeros_like(xv)
        dws = []
        for k in range(width):
            s = width - 1 - k
            dx = dx + w_ref[k:k + 1, :] * _shift_up(g, s)
            dws.append(jnp.sum(g * _shift_down(xv, s), axis=0, keepdims=True))
        dx_ref[0] = dx
        dw = jnp.concatenate(dws, axis=0)
        db = jnp.sum(g, axis=0, keepdims=True)

        @pl.when(pl.program_id(1) == 0)
        def _():
            dw_ref[...] = dw
            db_ref[...] = db

        @pl.when(pl.program_id(1) != 0)
        def _():
            dw_ref[...] += dw
            db_ref[...] += db

    return pl.pallas_call(
        body,
        name=name,
        grid=(c // LANES, bsz),
        in_specs=[
            pl.BlockSpec((1, t, LANES), lambda j, i: (i, 0, j)),
            pl.BlockSpec((width, LANES), lambda j, i: (0, j)),
            pl.BlockSpec((1, t, LANES), lambda j, i: (i, 0, j)),
        ],
        out_specs=[
            pl.BlockSpec((1, t, LANES), lambda j, i: (i, 0, j)),
            pl.BlockSpec((width, LANES), lambda j, i: (0, j)),
            pl.BlockSpec((1, LANES), lambda j, i: (0, j)),
        ],
        out_shape=[
            jax.ShapeDtypeStruct(x.shape, F32),
            jax.ShapeDtypeStruct(w.shape, F32),
            jax.ShapeDtypeStruct((1, c), F32),
        ],
        compiler_params=_params("parallel", "arbitrary"),
    )(x, w, dy)


def _make_conv(name):
    @jax.custom_vjp
    def op(x, w, b):
        return _conv_fwd(x, w, b, name + "_fwd")

    def fwd(x, w, b):
        return op(x, w, b), (x, w)

    def bwd(res, dy):
        x, w = res
        return tuple(_conv_bwd(x, w, dy, name + "_bwd"))

    op.defvjp(fwd, bwd)
    return op


_SCAN_ROWS = 8


def _scan_fwd(a, b, name):
    bsz, t, c = a.shape
    cw = _pick(c, (4 * LANES, 2 * LANES, LANES))

    def body(a_ref, b_ref, h_ref):
        row = lax.broadcasted_iota(jnp.int32, (_SCAN_ROWS, cw), 0)

        def step(i, carry):
            r0 = pl.multiple_of(i * _SCAN_ROWS, _SCAN_ROWS)
            av, bv = a_ref[0, pl.ds(r0, _SCAN_ROWS), :], b_ref[0, pl.ds(r0, _SCAN_ROWS), :]
            for s in (1, 2, 4):
                a_sh = jnp.where(row >= s, pltpu.roll(av, s, 0), 1.0)
                b_sh = jnp.where(row >= s, pltpu.roll(bv, s, 0), 0.0)
                bv = av * b_sh + bv
                av = av * a_sh
            hv = bv + av * carry
            h_ref[0, pl.ds(r0, _SCAN_ROWS), :] = hv
            return hv[_SCAN_ROWS - 1:, :]

        lax.fori_loop(0, t // _SCAN_ROWS, step, jnp.zeros((1, cw), F32), unroll=2)

    spec = pl.BlockSpec((1, t, cw), lambda i, j: (i, 0, j))
    return pl.pallas_call(
        body,
        name=name,
        grid=(bsz, c // cw),
        in_specs=[spec, spec],
        out_specs=spec,
        out_shape=jax.ShapeDtypeStruct(a.shape, F32),
        compiler_params=_params("parallel", "parallel"),
    )(a, b)


def _scan_bwd(a, h, g, name):
    bsz, t, c = a.shape
    cw = _pick(c, (2 * LANES, LANES))

    def body(a_ref, h_ref, g_ref, da_ref, db_ref):
        rows = _SCAN_ROWS
        row = lax.broadcasted_iota(jnp.int32, (rows, cw), 0)
        n_tiles = t // rows

        def step(n, carry):
            lam_next, a_next = carry
            i = n_tiles - 1 - n
            r0 = pl.multiple_of(i * rows, rows)
            rp = pl.multiple_of(jnp.maximum(i - 1, 0) * rows, rows)
            av, gv, hv = a_ref[0, pl.ds(r0, rows), :], g_ref[0, pl.ds(r0, rows), :], h_ref[0, pl.ds(r0, rows), :]
            h_before = jnp.where(i > 0, h_ref[0, pl.ds(rp, rows), :][rows - 1:, :], 0.0)
            cv = jnp.where(row < rows - 1, pltpu.roll(av, rows - 1, 0), a_next)
            for s in (1, 2, 4):
                c_sh = jnp.where(row < rows - s, pltpu.roll(cv, rows - s, 0), 1.0)
                g_sh = jnp.where(row < rows - s, pltpu.roll(gv, rows - s, 0), 0.0)
                gv = cv * g_sh + gv
                cv = cv * c_sh
            lam = gv + cv * lam_next
            db_ref[0, pl.ds(r0, rows), :] = lam
            da_ref[0, pl.ds(r0, rows), :] = lam * jnp.where(row >= 1, pltpu.roll(hv, 1, 0), h_before)
            return lam[:1, :], av[:1, :]

        zero = jnp.zeros((1, cw), F32)
        lax.fori_loop(0, n_tiles, step, (zero, zero), unroll=2)

    spec = pl.BlockSpec((1, t, cw), lambda i, j: (i, 0, j))
    return pl.pallas_call(
        body,
        name=name,
        grid=(bsz, c // cw),
        in_specs=[spec, spec, spec],
        out_specs=[spec, spec],
        out_shape=[jax.ShapeDtypeStruct(a.shape, F32)] * 2,
        compiler_params=_params("parallel", "parallel"),
    )(a, h, g)


def _make_scan(name):
    @jax.custom_vjp
    def op(a, b):
        return _scan_fwd(a, b, name + "_fwd")

    def fwd(a, b):
        h = op(a, b)
        return h, (a, h)

    def bwd(res, g):
        a, h = res
        da, db = _scan_bwd(a, h, g, name + "_bwd")
        return da, db

    op.defvjp(fwd, bwd)
    return op


def _query_blocks(t):
    blocks, start = [], 0
    while start < t:
        rows = 2 * SEQ_BLOCK if start + 2 * SEQ_BLOCK <= t else SEQ_BLOCK
        blocks.append((start, rows))
        start += rows
    return blocks


def _attn_exp(q, k, start, scale):
    tq, tk = q.shape[0], k.shape[0]
    s = _dot(q, k, "nt") * scale
    qpos = start + lax.broadcasted_iota(jnp.int32, (tq, tk), 0)
    kpos = lax.broadcasted_iota(jnp.int32, (tq, tk), 1)
    s = jnp.where(kpos <= qpos, s, NEG_INF)
    e = jnp.exp(s - jnp.max(s, axis=-1, keepdims=True))
    return e, 1.0 / jnp.sum(e, axis=-1, keepdims=True)


_MLA_SCALE = (MLA_NOPE + MLA_ROPE) ** -0.5


def _attn_specs(t):
    head = pl.BlockSpec((1, t, LANES), lambda b, h: (b, 0, h))
    shared = pl.BlockSpec((1, t, LANES), lambda b, h: (b, 0, 0))
    return head, shared


def _attn_fwd(q, kv, kpe, name):
    bsz, t, hl = q.shape
    head, shared = _attn_specs(t)

    def body(q_ref, kv_ref, kpe_ref, o_ref, k_s, v_s):
        lane = lax.broadcasted_iota(jnp.int32, (t, LANES), 1)
        kvh = kv_ref[0]
        k_s[...] = jnp.where(lane < MLA_NOPE, kvh, kpe_ref[0]).astype(BF16)
        v_s[...] = kvh.astype(BF16)
        for start, rows in _query_blocks(t):
            n = start + rows
            e, inv_l = _attn_exp(q_ref[0, start:n, :], k_s[:n, :], start, _MLA_SCALE)
            o_ref[0, start:n, :] = _dot(e, v_s[:n, :], "nn") * inv_l

    return pl.pallas_call(
        body,
        name=name,
        grid=(bsz, hl // LANES),
        in_specs=[head, head, shared],
        out_specs=head,
        out_shape=jax.ShapeDtypeStruct(q.shape, F32),
        scratch_shapes=[pltpu.VMEM((t, LANES), BF16), pltpu.VMEM((t, LANES), BF16)],
        compiler_params=_params("parallel", "parallel"),
    )(q, kv, kpe)


def _attn_bwd(q, kv, kpe, do, name):
    bsz, t, hl = q.shape
    head, shared = _attn_specs(t)

    def body(q_ref, kv_ref, kpe_ref, do_ref, dq_ref, dkv_ref, dkpe_ref, k_s, v_s, dk_s, dv_s):
        lane = lax.broadcasted_iota(jnp.int32, (t, LANES), 1)
        kvh = kv_ref[0]
        k_s[...] = jnp.where(lane < MLA_NOPE, kvh, kpe_ref[0]).astype(BF16)
        v_s[...] = kvh.astype(BF16)
        for start, rows in reversed(_query_blocks(t)):
            n = start + rows
            qb = q_ref[0, start:n, :]
            dob = jnp.where(lane[:rows] >= MLA_NOPE, do_ref[0, start:n, :], 0.0)
            kk, vv = k_s[:n, :], v_s[:n, :]
            e, inv_l = _attn_exp(qb, kk, start, _MLA_SCALE)
            p = e * inv_l
            dp = _dot(dob, vv, "nt")
            ds = p * (dp - jnp.sum(dp * p, axis=-1, keepdims=True)) * _MLA_SCALE
            dq_ref[0, start:n, :] = _dot(ds, kk, "nn")
            if n == t:
                dk_s[...] = _dot(ds, qb, "tn")
                dv_s[...] = _dot(p, dob, "tn")
            else:
                dk_s[:n, :] += _dot(ds, qb, "tn")
                dv_s[:n, :] += _dot(p, dob, "tn")
        dk = dk_s[...]
        dkv_ref[0] = jnp.where(lane < MLA_NOPE, dk, dv_s[...])
        dkpe = jnp.where(lane >= MLA_NOPE, dk, 0.0)

        @pl.when(pl.program_id(1) == 0)
        def _():
            dkpe_ref[0] = dkpe

        @pl.when(pl.program_id(1) != 0)
        def _():
            dkpe_ref[0] += dkpe

    return pl.pallas_call(
        body,
        name=name,
        grid=(bsz, hl // LANES),
        in_specs=[head, head, shared, head],
        out_specs=[head, head, shared],
        out_shape=[
            jax.ShapeDtypeStruct(q.shape, F32),
            jax.ShapeDtypeStruct(kv.shape, F32),
            jax.ShapeDtypeStruct(kpe.shape, F32),
        ],
        scratch_shapes=[pltpu.VMEM((t, LANES), BF16), pltpu.VMEM((t, LANES), BF16),
                        pltpu.VMEM((t, LANES), F32), pltpu.VMEM((t, LANES), F32)],
        compiler_params=_params("parallel", "arbitrary"),
    )(q, kv, kpe, do)


def _make_attention(name):
    @jax.custom_vjp
    def op(q, kv, kpe):
        return _attn_fwd(q, kv, kpe, name + "_fwd")

    def fwd(q, kv, kpe):
        return op(q, kv, kpe), (q, kv, kpe)

    def bwd(res, do):
        return tuple(_attn_bwd(*res, do, name + "_bwd"))

    op.defvjp(fwd, bwd)
    return op


_ROPE_SHIFT = MLA_ROPE // 2


def _rope_lanes_call(x, c, sm, sp, transpose, name):
    r, width = x.shape
    tm = _row_tile(r, width)

    def body(x_ref, c_ref, sm_ref, sp_ref, y_ref):
        cv, smv, spv = c_ref[...], sm_ref[...], sp_ref[...]
        for b in range(width // LANES):
            xb = x_ref[:, b * LANES:(b + 1) * LANES]
            if transpose:
                yb = xb * cv + pltpu.roll(xb * smv, _ROPE_SHIFT, 1) + pltpu.roll(xb * spv, LANES - _ROPE_SHIFT, 1)
            else:
                yb = xb * cv + pltpu.roll(xb, LANES - _ROPE_SHIFT, 1) * smv + pltpu.roll(xb, _ROPE_SHIFT, 1) * spv
            y_ref[:, b * LANES:(b + 1) * LANES] = yb

    tab = pl.BlockSpec((tm, LANES), lambda i: (i, 0))
    blk = pl.BlockSpec((tm, width), lambda i: (i, 0))
    return pl.pallas_call(
        body,
        name=name,
        grid=(r // tm,),
        in_specs=[blk, tab, tab, tab],
        out_specs=blk,
        out_shape=jax.ShapeDtypeStruct(x.shape, F32),
        compiler_params=_params("parallel"),
    )(x, c, sm, sp)


def _make_rope_lanes(name):
    @jax.custom_vjp
    def op(x, c, sm, sp):
        return _rope_lanes_call(x, c, sm, sp, False, name + "_fwd")

    def fwd(x, c, sm, sp):
        return op(x, c, sm, sp), (c, sm, sp)

    def bwd(res, dy):
        c, sm, sp = res
        return _rope_lanes_call(dy, c, sm, sp, True, name + "_bwd"), jnp.zeros_like(c), jnp.zeros_like(sm), jnp.zeros_like(sp)

    op.defvjp(fwd, bwd)
    return op


def _ret_chunk_rows(t):
    return t // 4 if t % 32 == 0 else SEQ_BLOCK


def _ret_decays(c, log_gamma):
    row = lax.broadcasted_iota(jnp.int32, (c, 1), 0)
    col = lax.broadcasted_iota(jnp.int32, (1, c), 1)
    rowf = row.astype(F32)
    d = jnp.where(row >= col, jnp.exp(log_gamma * rowf) * jnp.exp(-log_gamma * col.astype(F32)), 0.0)
    return d, jnp.exp(log_gamma * (rowf + 1.0)), jnp.exp(log_gamma * (c - 1.0 - rowf)), jnp.exp(log_gamma * c)


def _ret_specs(c, dk, dv, v_block0, n_chunks, reverse):
    pos = (lambda i: n_chunks - 1 - i) if reverse else (lambda i: i)
    return (
        pl.BlockSpec(memory_space=pltpu.SMEM),
        pl.BlockSpec((1, c, dk), lambda b, h, i: (b, pos(i), h)),
        pl.BlockSpec((1, c, dv), lambda b, h, i: (b, pos(i), h + v_block0)),
        pl.BlockSpec((1, c, dv), lambda b, h, i: (b, pos(i), h)),
        pl.BlockSpec((1, 1, dk, dv), lambda b, h, i: (b, h * n_chunks + pos(i), 0, 0)),
    )


def _ret_fwd(lg, q, k, v, name, dv=None, v_block0=0):
    bsz, t, hdk = q.shape
    heads = lg.shape[0]
    dk, dv = hdk // heads, dv or v.shape[2] // heads
    c = _ret_chunk_rows(t)
    n_chunks = t // c
    lg_spec, qk_spec, v_spec, o_spec, s_spec = _ret_specs(c, dk, dv, v_block0, n_chunks, False)

    def body(lg_ref, q_ref, k_ref, v_ref, o_ref, s_ref, state):
        @pl.when(pl.program_id(2) == 0)
        def _():
            state[...] = jnp.zeros((dk, dv), F32)

        d, a, b, g = _ret_decays(c, lg_ref[pl.program_id(1)])
        qb, kb, vb, s_in = q_ref[0], k_ref[0], v_ref[0], state[...]
        s_ref[0, 0] = s_in
        o_ref[0] = _dot(_dot(qb, kb, "nt") * d, vb, "nn") + a * _dot(qb, s_in, "nn")
        state[...] = g * s_in + _dot(kb * b, vb, "tn")

    return pl.pallas_call(
        body,
        name=name,
        grid=(bsz, heads, n_chunks),
        in_specs=[lg_spec, qk_spec, qk_spec, v_spec],
        out_specs=[o_spec, s_spec],
        out_shape=[jax.ShapeDtypeStruct((bsz, t, heads * dv), F32),
                   jax.ShapeDtypeStruct((bsz, heads * n_chunks, dk, dv), F32)],
        scratch_shapes=[pltpu.VMEM((dk, dv), F32)],
        compiler_params=_params("parallel", "parallel", "arbitrary"),
    )(lg, q, k, v)


def _ret_bwd(lg, q, k, v, states, do, name, v_block0=0, dv_dtype=F32):
    bsz, t, hdk = q.shape
    heads = lg.shape[0]
    dk, dv = hdk // heads, do.shape[2] // heads
    c = _ret_chunk_rows(t)
    n_chunks = t // c
    lg_spec, qk_spec, v_spec, o_spec, s_spec = _ret_specs(c, dk, dv, v_block0, n_chunks, True)

    def body(lg_ref, q_ref, k_ref, v_ref, s_ref, do_ref, dq_ref, dk_ref, dv_ref, dstate):
        @pl.when(pl.program_id(2) == 0)
        def _():
            dstate[...] = jnp.zeros((dk, dv), F32)

        d, a, b, g = _ret_decays(c, lg_ref[pl.program_id(1)])
        qb, kb, vb, dob, s_in, ds_out = q_ref[0], k_ref[0], v_ref[0], do_ref[0], s_ref[0, 0], dstate[...]
        scores = _dot(qb, kb, "nt") * d
        dscores = _dot(dob, vb, "nt") * d
        dq_ref[0] = _dot(dscores, kb, "nn") + a * _dot(dob, s_in, "nt")
        dk_ref[0] = _dot(dscores, qb, "tn") + b * _dot(vb, ds_out, "nt")
        dv_ref[0] = (_dot(scores, dob, "tn") + _dot(kb * b, ds_out, "nn")).astype(dv_dtype)
        dstate[...] = g * ds_out + _dot(qb, a * dob, "tn")

    return pl.pallas_call(
        body,
        name=name,
        grid=(bsz, heads, n_chunks),
        in_specs=[lg_spec, qk_spec, qk_spec, v_spec, s_spec, o_spec],
        out_specs=[qk_spec, qk_spec, o_spec],
        out_shape=[
            jax.ShapeDtypeStruct(q.shape, F32),
            jax.ShapeDtypeStruct(k.shape, F32),
            jax.ShapeDtypeStruct(do.shape, dv_dtype),
        ],
        scratch_shapes=[pltpu.VMEM((dk, dv), F32)],
        compiler_params=_params("parallel", "parallel", "arbitrary"),
    )(lg, q, k, v, states, do)


def _adamw(w, g, m, v, name):
    r, c = w.shape
    tr = _pick(r, (256, 128, 64, 32, 16, 8))

    def body(w_ref, g_ref, m_ref, v_ref, d_ref, nm_ref, nv_ref):
        gv = g_ref[...]
        nm = ADAM_B1 * m_ref[...] + (1.0 - ADAM_B1) * gv
        nv = ADAM_B2 * v_ref[...] + (1.0 - ADAM_B2) * (gv * gv)
        m_hat = nm / (1.0 - ADAM_B1 ** ADAM_STEP)
        v_hat = nv / (1.0 - ADAM_B2 ** ADAM_STEP)
        d_ref[...] = -ADAM_LR * (m_hat / (jnp.sqrt(v_hat) + ADAM_EPS) + ADAM_WD * w_ref[...])
        nm_ref[...] = nm
        nv_ref[...] = nv

    spec = pl.BlockSpec((tr, c), lambda i: (i, 0))
    return pl.pallas_call(
        body,
        name=name,
        grid=(r // tr,),
        in_specs=[spec] * 4,
        out_specs=[spec] * 3,
        out_shape=[jax.ShapeDtypeStruct((r, c), F32)] * 3,
        compiler_params=_params("parallel"),
    )(w, g, m, v)


def _rope_tables(t, half, reps):
    inv = ROPE_BASE ** (-jnp.arange(half, dtype=F32) / half)
    ang = jnp.arange(t, dtype=jnp.int32).astype(F32)[:, None] * inv[None, :]
    return jnp.tile(jnp.cos(ang), (1, reps)), jnp.tile(jnp.sin(ang), (1, reps))


def _padded_len(seq):
    return -(-(N_META + seq) // SEQ_BLOCK) * SEQ_BLOCK


def _embed(meta, x):
    bsz, seq, d = x.shape
    t = _padded_len(seq)
    return jnp.concatenate(
        [jnp.broadcast_to(meta[None], (bsz, N_META, d)), x, jnp.zeros((bsz, t - N_META - seq, d), F32)], axis=1
    ).reshape(bsz * t, d)


def _even_mixer(p, conv_w, conv_b, w_rg_a, b_rg_a, w_rg_x, b_rg_x, lru_lambda, q_norm_g, uq_slot, kv_norm_g,
                ukv_slot, gathered, bsz):
    w_uq_pad, w_ukv_full = gathered
    r = p.shape[0]
    t = r // bsz

    def tile_rows(tab):
        return jnp.tile(tab, (bsz, 1))

    lru_w = w_rg_a.shape[2] * w_rg_a.shape[1]
    q_rank, kv_rank = q_norm_g.shape[1], kv_norm_g.shape[1]
    p_gate, p_rec, p_q, p_kv, p_kpe = _split_cols(
        p, (lru_w, 2 * lru_w, 2 * lru_w + q_rank, 2 * lru_w + q_rank + kv_rank))

    xc = _make_conv("conv")(p_rec.reshape(bsz, t, lru_w), conv_w, conv_b).reshape(r, lru_w)
    ga = _make_group_linear("rg_a")(xc, w_rg_a[0])
    gx = _make_group_linear("rg_x")(xc, w_rg_x[0])
    (a, bb), _ = _make_rowwise(_lru_gates_f, "lru_gates", 3, 0, 3)((ga, gx, xc), (), (b_rg_a, b_rg_x, lru_lambda))
    hh = _make_scan("lru_scan")(a.reshape(bsz, t, lru_w), bb.reshape(bsz, t, lru_w)).reshape(r, lru_w)
    (y_rec,), _ = _make_rowwise(_lru_out_f, "lru_out", 2, 0, 0)((hh, p_gate), (), ())

    (qn,), _ = _make_rowwise(_rmsnorm_f, "q_norm", 1, 0, 1)((p_q,), (), (q_norm_g,))
    (kvn,), _ = _make_rowwise(_rmsnorm_f, "kv_norm", 1, 0, 1)((p_kv,), (), (kv_norm_g,))
    q = _make_slot_linear("ev_uq")(qn, w_uq_pad, uq_slot)
    kv = _make_slot_linear("ev_ukv")(kvn, w_ukv_full, ukv_slot)
    half = MLA_ROPE // 2
    cos, sin = _rope_tables(t, half, 1)
    one, zero = jnp.ones((t, MLA_NOPE), F32), jnp.zeros((t, MLA_NOPE), F32)
    tail = LANES - MLA_NOPE - MLA_ROPE
    c_tab = tile_rows(jnp.concatenate([one, cos, cos, one[:, :tail]], axis=1))
    sm_tab = tile_rows(jnp.concatenate([zero, -sin, zero[:, :half + tail]], axis=1))
    sp_tab = tile_rows(jnp.concatenate([zero, zero[:, :half], sin, zero[:, :tail]], axis=1))
    q = _make_rope_lanes("rope_q")(q, c_tab, sm_tab, sp_tab)
    kpe = _make_rope_lanes("rope_k")(p_kpe, c_tab, sm_tab, sp_tab)
    o = _make_attention("mla")(q.reshape(bsz, t, -1), kv.reshape(bsz, t, -1), kpe.reshape(bsz, t, LANES))
    return jnp.concatenate([y_rec, o.reshape(r, -1)], axis=1)


def _odd_mixer_fwd(p, bsz):
    r, width = p.shape
    t = r // bsz
    qk = width // 6
    dk = qk // RET_HEADS
    cos2, sin2 = _rope_tables(t, dk // 2, 2)
    sin2 = jnp.concatenate([-sin2[:, :dk // 2], sin2[:, dk // 2:]], axis=1)
    rope_args = (_Cols(p, qk, 0), _Cols(p, qk, 1), jnp.tile(cos2, (bsz, 1)), jnp.tile(sin2, (bsz, 1)))
    (rq, rk), _ = _make_rowwise(_rope_ret_f, "rope_ret", 2, 2, 0).fwd_call(*rope_args)
    lg = jnp.log(1.0 - 2.0 ** (-5.0 - jnp.arange(RET_HEADS, dtype=F32)))
    ret_args = (lg, rq.reshape(bsz, t, qk), rk.reshape(bsz, t, qk), p.reshape(bsz, t, width))
    o, states = _ret_fwd(*ret_args, "ret_fwd", dv=2 * dk, v_block0=qk // dk)
    gate_args = (o.reshape(r, 2 * qk), _Cols(p, 2 * qk, 2))
    (y,), _ = _make_rowwise(_ret_out_bf16_f, "ret_out", 2, 0, 0).fwd_call(*gate_args)
    return y, (rope_args, ret_args + (states,), gate_args)


def _odd_mixer_bwd(res, dy):
    rope_args, ret_args, gate_args = res
    bsz, t, qk = ret_args[1].shape
    dk = qk // RET_HEADS
    (do, dg), _ = _make_rowwise(_ret_out_f, "ret_out", 2, 0, 0).bwd_call(
        gate_args, ((dy,), ()), row_dtypes=(F32, BF16))
    drq, drk, dv = _ret_bwd(*ret_args, do.reshape(bsz, t, 2 * qk), "ret_bwd", v_block0=qk // dk, dv_dtype=BF16)
    (dq, dkk), _ = _make_rowwise(_rope_ret_f, "rope_ret", 2, 2, 0).bwd_call(
        rope_args, ((drq.reshape(bsz * t, qk), drk.reshape(bsz * t, qk)), ()), row_dtypes=(BF16, BF16))
    return jnp.concatenate([dq, dkk, dv.reshape(bsz * t, 2 * qk), dg], axis=1)


def _local_loss(h, target):
    bsz, seq, d = target.shape
    t = _padded_len(seq)
    t_real = N_META + seq
    pos = jnp.arange(t, dtype=jnp.int32)
    mask = jnp.tile(((pos >= N_META) & (pos < t_real)).astype(F32)[:, None], (bsz, 1))
    tgt = jnp.concatenate(
        [jnp.zeros((bsz, N_META, d), F32), target, jnp.zeros((bsz, t - t_real, d), F32)], axis=1).reshape(bsz * t, d)
    _, (total,) = _make_rowwise(_loss_f, "loss", 1, 2, 0)((h,), (tgt, mask), ())
    return total[0, 0]


_WEIGHTS = ("meta_tokens", "ev_w_in", "ev_conv_w", "ev_conv_b", "ev_w_rg_a", "ev_b_rg_a", "ev_w_rg_x", "ev_b_rg_x",
            "ev_lru_lambda", "ev_q_norm_g", "ev_w_uq", "ev_kv_norm_g", "ev_w_ukv", "ev_w_out", "od_w_in", "od_w_out",
            "ln_mix_g", "ln_mix_b", "mlp_w1", "mlp_w2", "ln_mlp_g", "ln_mlp_b")


def kernel(x, meta_tokens, ev_w_in, ev_conv_w, ev_conv_b, ev_w_rg_a, ev_b_rg_a, ev_w_rg_x, ev_b_rg_x, ev_lru_lambda, ev_q_norm_g, ev_w_uq, ev_kv_norm_g, ev_w_ukv, ev_w_out, od_w_in, od_w_out, ln_mix_g, ln_mix_b, mlp_w1, mlp_w2, ln_mlp_g, ln_mlp_b, loss_target, m_meta_tokens, m_ev_w_in, m_ev_conv_w, m_ev_conv_b, m_ev_w_rg_a, m_ev_b_rg_a, m_ev_w_rg_x, m_ev_b_rg_x, m_ev_lru_lambda, m_ev_q_norm_g, m_ev_w_uq, m_ev_kv_norm_g, m_ev_w_ukv, m_ev_w_out, m_od_w_in, m_od_w_out, m_ln_mix_g, m_ln_mix_b, m_mlp_w1, m_mlp_w2, m_ln_mlp_g, m_ln_mlp_b, v_meta_tokens, v_ev_w_in, v_ev_conv_w, v_ev_conv_b, v_ev_w_rg_a, v_ev_b_rg_a, v_ev_w_rg_x, v_ev_b_rg_x, v_ev_lru_lambda, v_ev_q_norm_g, v_ev_w_uq, v_ev_kv_norm_g, v_ev_w_ukv, v_ev_w_out, v_od_w_in, v_od_w_out, v_ln_mix_g, v_ln_mix_b, v_mlp_w1, v_mlp_w2, v_ln_mlp_g, v_ln_mlp_b):
    args = locals()
    weights = {n: args[n] for n in _WEIGHTS}
    bsz = x.shape[0]
    my_x, my_y, my_c = _my_place()
    me = 4 * my_x + 2 * my_y + my_c

    big = (("ev_in", ev_w_in[0], True), ("ev_out", ev_w_out[0], False), ("mlp0_w1", mlp_w1[0], True),
           ("mlp0_w2", mlp_w2[0], False), ("od_in", od_w_in[0], True), ("od_out", od_w_out[0], False),
           ("mlp1_w1", mlp_w1[1], True), ("mlp1_w2", mlp_w2[1], False))
    small_sharded = (("meta", meta_tokens, F32), ("conv_w", ev_conv_w[0], F32), ("ev_uq", ev_w_uq[0], BF16),
                     ("ev_ukv", ev_w_ukv[0], BF16))
    to_gather = (tuple((nm, s.astype(dt), True) for nm, s, dt in small_sharded)
                 + tuple((nm, s.astype(BF16), cols) for nm, s, cols in big))
    handles = _gather_start_all([s for _, s, _ in to_gather], "ag_start")
    gathers = {nm: (s, cols, h) for (nm, s, cols), h in zip(to_gather, handles)}
    gather_tokens = (handles[0][4],)

    def full_weight(nm, after):
        shard16, cols, handle = gathers[nm]
        land = _exchange_wait(handle, True, after, "ag_wait_" + nm)
        land = lax.dynamic_update_index_in_dim(land, shard16, me, 0)
        if cols and shard16.shape[1] % LANES == 0:
            return land, True
        return (_unstack_cols(land) if cols else land.reshape(-1, shard16.shape[1])), False

    meta_full, conv_w_full, w_uq_full, w_ukv_full = (
        _unstack_cols(lax.dynamic_update_index_in_dim(
            _exchange_wait(gathers[nm][2], True, gather_tokens[-1], "ag_wait_" + nm), gathers[nm][0], me, 0))
        for nm, _, _ in small_sharded)

    pending = []

    def linear_bwd(nm, x_in, w, dy, cols, unpad=None, **fused):
        w_full, w_stacked = w
        own = None
        if w_stacked:
            stacked = _matmul(x_in, dy, "tn", nm + "_dw", stacked=True)
        else:
            dw = _matmul(x_in, dy, "tn", nm + "_dw")
            dw = dw if unpad is None else unpad(dw)
            n = dw.shape[1] // N_DEV
            if cols:
                stacked = _stack_cols(dw)
                own = lax.dynamic_slice_in_dim(dw, me * n, n, axis=1)
            else:
                stacked = dw.reshape(N_DEV, dw.shape[0] // N_DEV, dw.shape[1])
        handle = _exchange_start(stacked, (_N_PEERS,) + stacked.shape[1:], False, "rs_start_" + nm)
        if own is None:
            own = lax.dynamic_index_in_dim(handle[2], me, 0, keepdims=False)
        pending.append((nm, own, handle))
        return _matmul(dy, w_full, "nt", nm + "_dx", after=(handle[4],), stacked=w_stacked, **fused)

    def linear_fwd(nm, x_in, w, **fused):
        return _matmul(x_in, w[0], "nn", nm + "_fwd", stacked=w[1], **fused)

    def mlp_fwd(h, h16, l):
        w1 = full_weight(f"mlp{l}_w1", h16)
        a16 = linear_fwd(f"mlp{l}_w1", h16, w1, out_relu2=True, out_dtype=BF16)
        w2 = full_weight(f"mlp{l}_w2", a16)
        f = linear_fwd(f"mlp{l}_w2", a16, w2)
        ln_args = (h, f, ln_mlp_g[l:l + 1], ln_mlp_b[l:l + 1])
        return ln_fwd(f"mlp{l}_ln", *ln_args), (h16, w1, a16, w2, ln_args)

    def mlp_bwd(dout, res, l):
        h16, w1, a16, w2, ln_args = res
        dh, df, dg, db = ln_bwd(f"mlp{l}_ln", ln_args, dout)
        du = linear_bwd(f"mlp{l}_w2", a16, w2, df, False, relu2_bwd_of=a16, out_dtype=BF16)
        return (dh, linear_bwd(f"mlp{l}_w1", h16, w1, du, True)), dg, db

    def ln_fwd(nm, h, mix, g, b):
        return _make_rowwise(_ln_res_copy_f, nm, 2, 0, 2).fwd_call(h, mix, g, b)[0]

    def ln_bwd(nm, ln_args, pieces):
        (dh, dmix), (dg, db) = _make_rowwise(_ln_res_f, nm, 2, 0, 2).bwd_call(
            ln_args, ((pieces[0],), ()), more=tuple(pieces[1:]), row_dtypes=(F32, BF16))
        return dh, dmix, dg, db

    h0, vjp_embed = jax.vjp(_embed, meta_full, x)
    n_in = ev_w_in.shape[2] * N_DEV
    kpe0, pad_lo, pad_hi = n_in - MLA_ROPE, MLA_NOPE, LANES - MLA_NOPE - MLA_ROPE
    w_in = full_weight("ev_in", h0)[0]
    zeros_in = jnp.zeros((w_in.shape[0], pad_lo), BF16)
    w_ev_in = (jnp.concatenate([w_in[:, :kpe0], zeros_in, w_in[:, kpe0:], zeros_in[:, :pad_hi]], axis=1), False)

    def unpad_in(dw):
        return jnp.concatenate([dw[:, :kpe0], dw[:, kpe0 + pad_lo:kpe0 + pad_lo + MLA_ROPE]], axis=1)

    p0 = _matmul(h0, w_ev_in[0], "nn", "ev_in_fwd", after=gather_tokens)
    q_rank, d_head = w_uq_full.shape[0], MLA_NOPE + MLA_ROPE
    w_uq_pad = jnp.pad(w_uq_full.reshape(q_rank, MLA_HEADS, d_head), ((0, 0), (0, 0), (0, LANES - d_head)))
    w_uq_pad = w_uq_pad.reshape(q_rank, MLA_HEADS * LANES)
    small = (conv_w_full, ev_conv_b, ev_w_rg_a, ev_b_rg_a, ev_w_rg_x, ev_b_rg_x, ev_lru_lambda, ev_q_norm_g,
             jnp.zeros(w_uq_pad.shape, F32), ev_kv_norm_g, jnp.zeros(w_ukv_full.shape, F32))
    y0, vjp_even = jax.vjp(lambda p, *s: _even_mixer(p, *s, (w_uq_pad, w_ukv_full), bsz), p0, *small)
    w_out = full_weight("ev_out", y0)[0]
    lru_w, d_model = y0.shape[1] - MLA_HEADS * LANES, w_out.shape[1]
    w_att = w_out[lru_w:].reshape(MLA_HEADS, MLA_V, d_model)
    w_att = jnp.concatenate([jnp.zeros((MLA_HEADS, LANES - MLA_V, d_model), BF16), w_att], axis=1)
    w_ev_out = (jnp.concatenate([w_out[:lru_w], w_att.reshape(MLA_HEADS * LANES, d_model)], axis=0), False)

    def unpad_out(dw):
        d_att = dw[lru_w:].reshape(MLA_HEADS, LANES, d_model)[:, LANES - MLA_V:].reshape(MLA_HEADS * MLA_V, d_model)
        return jnp.concatenate([dw[:lru_w], d_att], axis=0)

    mix0 = linear_fwd("ev_out", y0, w_ev_out)
    ln0_args = (h0, mix0, ln_mix_g[0:1], ln_mix_b[0:1])
    h1, h1_16 = ln_fwd("mix0_ln", *ln0_args)
    (h2, h2_16), res_mlp0 = mlp_fwd(h1, h1_16, 0)
    w_od_in = full_weight("od_in", h2_16)
    p1 = linear_fwd("od_in", h2_16, w_od_in)
    y1, res_odd = _odd_mixer_fwd(p1, bsz)
    w_od_out = full_weight("od_out", y1)
    mix1 = linear_fwd("od_out", y1, w_od_out)
    ln1_args = (h2, mix1, ln_mix_g[1:2], ln_mix_b[1:2])
    h3, h3_16 = ln_fwd("mix1_ln", *ln1_args)
    (h4, _), res_mlp1 = mlp_fwd(h3, h3_16, 1)
    loss_local, vjp_loss = jax.vjp(lambda h: _local_loss(h, loss_target), h4)

    dh4 = vjp_loss(jnp.ones((), F32))
    dh3, dg_mlp1, db_mlp1 = mlp_bwd(dh4, res_mlp1, 1)
    dh2, dmix1, dg_mix1, db_mix1 = ln_bwd("mix1_ln", ln1_args, dh3)
    dp1 = _odd_mixer_bwd(res_odd, linear_bwd("od_out", y1, w_od_out, dmix1, False))
    dh2 = (dh2, linear_bwd("od_in", h2_16, w_od_in, dp1, True))
    dh1, dg_mlp0, db_mlp0 = mlp_bwd(dh2, res_mlp0, 0)
    dh0, dmix0, dg_mix0, db_mix0 = ln_bwd("mix0_ln", ln0_args, dh1)
    dp0, *dsmall = vjp_even(linear_bwd("ev_out", y0, w_ev_out, dmix0, False, unpad=unpad_out))
    dh0 = dh0 + linear_bwd("ev_in", h0, w_ev_in, dp0.astype(BF16), True, unpad=unpad_in)
    g_meta_full, grad_x = vjp_embed(dh0)
    (g_conv_w_full, g_conv_b, g_w_rg_a, g_b_rg_a, g_w_rg_x, g_b_rg_x, g_lambda, g_q_norm, g_uq_pad, g_kv_norm,
     g_ukv_full) = dsmall

    for nm, dw in (("ev_uq", g_uq_pad.reshape(q_rank, MLA_HEADS, LANES)[:, :, :d_head].reshape(q_rank, -1)),
                   ("ev_ukv", g_ukv_full)):
        n = dw.shape[1] // N_DEV
        handle = _exchange_start(_stack_cols(dw), (_N_PEERS, dw.shape[0], n), False, "rs_start_" + nm)
        pending.append((nm, lax.dynamic_slice_in_dim(dw, me * n, n, axis=1), handle))

    rep_names = ("ev_conv_b", "ev_w_rg_a", "ev_b_rg_a", "ev_w_rg_x", "ev_b_rg_x", "ev_lru_lambda", "ev_q_norm_g",
                 "ev_kv_norm_g", "ln_mix_g", "ln_mix_b", "ln_mlp_g", "ln_mlp_b")
    rep_local = (g_conv_b, g_w_rg_a, g_b_rg_a, g_w_rg_x, g_b_rg_x, g_lambda, g_q_norm, g_kv_norm,
                 jnp.concatenate([dg_mix0, dg_mix1]), jnp.concatenate([db_mix0, db_mix1]),
                 jnp.concatenate([dg_mlp0, dg_mlp1]), jnp.concatenate([db_mlp0, db_mlp1]), g_conv_w_full, g_meta_full)
    rep_packed = _pack_rows(rep_local)
    rep_handle = _exchange_start(rep_packed, (N_DEV,) + rep_packed.shape, True, "rep_start", after=(grad_x,))

    after, summed = rep_handle[4], {}
    for nm, own, handle in pending:
        land = _exchange_wait(handle, False, after, "rs_wait_" + nm)
        summed[nm] = after = _sum_own_and_peers(own, land, "rs_sum_" + nm)

    grad_w = dict(ev_w_uq=summed["ev_uq"][None], ev_w_ukv=summed["ev_ukv"][None],
                  ev_w_in=summed["ev_in"][None], ev_w_out=summed["ev_out"][None], od_w_in=summed["od_in"][None],
                  od_w_out=summed["od_out"][None], mlp_w1=jnp.stack([summed["mlp0_w1"], summed["mlp1_w1"]]),
                  mlp_w2=jnp.stack([summed["mlp0_w2"], summed["mlp1_w2"]]))

    loss = lax.psum(loss_local, MESH_AXES)
    delta, new_m, new_v = {}, {}, {}

    def adamw(n):
        w, g, m, v = weights[n], grad_w[n], args["m_" + n], args["v_" + n]
        two_d = (-1, w.shape[-1])
        d2, m2, v2 = _adamw(w.reshape(two_d), g.reshape(two_d), m.reshape(two_d), v.reshape(two_d), "adamw_" + n)
        delta[n], new_m[n], new_v[n] = d2.reshape(w.shape), m2.reshape(w.shape), v2.reshape(w.shape)
        return d2

    for n in tuple(grad_w):
        after = adamw(n)
    rep_land = lax.dynamic_update_index_in_dim(_exchange_wait(rep_handle, True, after, "rep_wait"), rep_packed, me, 0)
    *rep_total, t_conv_w, t_meta = _unpack_rows(_sum_blocks(rep_land, "rep_sum"), rep_local)
    n_conv, n_meta = ev_conv_w.shape[2], meta_tokens.shape[1]
    small_g = dict(zip(rep_names, rep_total))
    small_g.update(meta_tokens=lax.dynamic_slice_in_dim(t_meta, me * n_meta, n_meta, axis=1),
                   ev_conv_w=lax.dynamic_slice_in_dim(t_conv_w, me * n_conv, n_conv, axis=1)[None])
    grad_w.update(small_g)
    for n in small_g:
        adamw(n)
    return (loss, grad_x, *[grad_w[n] for n in _WEIGHTS], *[delta[n] for n in _WEIGHTS],
            *[new_m[n] for n in _WEIGHTS], *[new_v[n] for n in _WEIGHTS])
```

```python
import math

import jax
import jax.numpy as jnp
from jax import lax
from jax.experimental import pallas as pl
from jax.experimental.pallas import tpu as pltpu

F32 = jnp.float32
BF16 = jnp.bfloat16

N_DEV = 8
MESH_AXES = ("x", "y", "c")
LANES = 128
SEQ_BLOCK = 128

N_META = 16
LRU_C = 8.0
MLA_HEADS = 8
MLA_NOPE = 64
MLA_ROPE = 32
MLA_V = 64
RET_HEADS = 4
ROPE_BASE = 10000.0
DEPTH = 2
DN_ALPHA = (2 * DEPTH) ** 0.25
EPS = 1e-5
NEG_INF = -1e30

ADAM_LR = 0.001
ADAM_B1 = 0.9
ADAM_B2 = 0.999
ADAM_EPS = 1e-08
ADAM_WD = 0.01
ADAM_STEP = 10

VMEM_LIMIT = 56 * 1024 * 1024


def _params(*sem):
    return pltpu.CompilerParams(dimension_semantics=sem, vmem_limit_bytes=VMEM_LIMIT)


def _pick(n, cands):
    for c in cands:
        if n % c == 0:
            return c
    return n


def _row_tile(r, width):
    cands = (256, 128, 64, 32, 16, 8) if width <= 1024 else (128, 64, 32, 16, 8)
    return _pick(r, cands)


_DIMS = {"nn": (((1,), (0,)), ((), ())), "nt": (((1,), (1,)), ((), ())), "tn": (((0,), (0,)), ((), ()))}


def _dot(a, b, mode):
    return lax.dot_general(a.astype(BF16), b.astype(BF16), _DIMS[mode], preferred_element_type=F32)


def _matmul(a, b, mode, name, after=(), stacked=False, relu2_bwd_of=None, out_dtype=F32, out_relu2=False):
    if stacked:
        n_blk = b.shape[2] if mode != "tn" else b.shape[1] // N_DEV
    if mode == "nn":
        (m, k), n = a.shape, (N_DEV * n_blk if stacked else b.shape[1])
    elif mode == "nt":
        (m, k), n = a.shape, (b.shape[1] if stacked else b.shape[0])
    else:
        (k, m), n = a.shape, b.shape[1]
    tm = _pick(m, (1088, 1024, 544, 512, 272, 256, 128, 64, 32, 16, 8))
    tn = _pick(n, (1024, 512, 256, 128))
    tk = _pick(k, (1088, 1024, 544, 512, 272, 256, 128))
    kb = 2
    if stacked and mode == "nn":
        tn = n_blk
    if stacked and mode == "tn":
        tn = kb * n_blk
    if stacked and mode == "nt":
        tk = kb * n_blk
    nk = k // tk
    assert out_dtype == F32 or (nk == 1 and not (stacked and mode == "tn")), "narrow results need a single k step"
    assert not out_relu2 or nk == 1, "relu^2 is applied to a finished tile"

    out_spec = pl.BlockSpec((tm, tn), lambda i, j, kk: (i, j))
    out_shape = jax.ShapeDtypeStruct((m, n), out_dtype)
    if mode == "nn":
        a_spec = pl.BlockSpec((tm, tk), lambda i, j, kk: (i, kk))
        b_spec = pl.BlockSpec((tk, tn), lambda i, j, kk: (kk, j))
        if stacked:
            b_spec = pl.BlockSpec((None, tk, tn), lambda i, j, kk: (j, kk, 0))
    elif mode == "nt":
        a_spec = pl.BlockSpec((tm, tk), lambda i, j, kk: (i, kk))
        b_spec = pl.BlockSpec((tn, tk), lambda i, j, kk: (j, kk))
        if stacked:
            b_spec = pl.BlockSpec((kb, tn, n_blk), lambda i, j, kk: (kk, j, 0))
    else:
        a_spec = pl.BlockSpec((tk, tm), lambda i, j, kk: (kk, i))
        b_spec = pl.BlockSpec((tk, tn), lambda i, j, kk: (kk, j))
        if stacked:
            out_spec = pl.BlockSpec((kb, tm, n_blk), lambda i, j, kk: (j, i, 0))
            out_shape = jax.ShapeDtypeStruct((N_DEV, m, n_blk), F32)
    extra = [] if relu2_bwd_of is None else [relu2_bwd_of]
    extra_specs = [pl.BlockSpec((tm, tn), lambda i, j, kk: (i, j))] * len(extra)

    def body(a_ref, b_ref, *rest):
        def relu2_slope():
            return 2.0 * jnp.sqrt(rest[0][...].astype(F32))

        o_ref = rest[-1]
        kk = pl.program_id(2)
        av = a_ref[...]
        if stacked and mode == "nt":
            part = _dot(av[:, :n_blk], b_ref[0], mode)
            for q in range(1, kb):
                part = part + _dot(av[:, q * n_blk:(q + 1) * n_blk], b_ref[q], mode)
        else:
            part = _dot(av, b_ref[...], mode)
        if stacked and mode == "tn":
            part = jnp.stack([part[:, q * n_blk:(q + 1) * n_blk] for q in range(kb)])
        if nk == 1:
            if out_relu2:
                part = jnp.maximum(part, 0.0)
                part = part * part
            if relu2_bwd_of is not None:
                part = part * relu2_slope()
            o_ref[...] = part.astype(out_dtype)
            return

        @pl.when(kk == 0)
        def _():
            o_ref[...] = part

        @pl.when(kk != 0)
        def _():
            o_ref[...] += part

        if relu2_bwd_of is not None:
            @pl.when(kk == nk - 1)
            def _():
                o_ref[...] *= relu2_slope()

    return pl.pallas_call(
        body,
        name=name,
        grid=(m // tm, n // tn, nk),
        in_specs=[a_spec, b_spec] + extra_specs + [pl.BlockSpec(memory_space=pl.ANY)] * len(after),
        out_specs=out_spec,
        out_shape=out_shape,
        compiler_params=_params("parallel", "parallel", "arbitrary"),
    )(a, b, *extra, *after)


def _group_matmul(a, w, mode, name):
    if mode in ("nn", "nt"):
        g, dk, dn = w.shape
        m = a.shape[0]
        d_in, d_out = (dk, dn) if mode == "nn" else (dn, dk)
        tm = _pick(m, (1088, 1024, 544, 512, 272, 256, 128, 64, 32, 16, 8))

        def body(a_ref, w_ref, o_ref):
            o_ref[...] = _dot(a_ref[...], w_ref[0], mode)

        return pl.pallas_call(
            body,
            name=name,
            grid=(g, m // tm),
            in_specs=[pl.BlockSpec((tm, d_in), lambda h, i: (i, h)), pl.BlockSpec((1, dk, dn), lambda h, i: (h, 0, 0))],
            out_specs=pl.BlockSpec((tm, d_out), lambda h, i: (i, h)),
            out_shape=jax.ShapeDtypeStruct((m, g * d_out), F32),
            compiler_params=_params("parallel", "parallel"),
        )(a, w)
    b = w
    m = a.shape[0]
    dk = dn = LANES
    g = a.shape[1] // dk
    tm = _pick(m, (1088, 1024, 544, 512, 272, 256, 128, 64, 32, 16, 8))

    def body(a_ref, b_ref, o_ref):
        part = _dot(a_ref[...], b_ref[...], "tn")

        @pl.when(pl.program_id(1) == 0)
        def _():
            o_ref[0] = part

        @pl.when(pl.program_id(1) != 0)
        def _():
            o_ref[0] += part

    return pl.pallas_call(
        body,
        name=name,
        grid=(g, m // tm),
        in_specs=[pl.BlockSpec((tm, dk), lambda h, i: (i, h)), pl.BlockSpec((tm, dn), lambda h, i: (i, h))],
        out_specs=pl.BlockSpec((1, dk, dn), lambda h, i: (h, 0, 0)),
        out_shape=jax.ShapeDtypeStruct((g, dk, dn), F32),
        compiler_params=_params("parallel", "arbitrary"),
    )(a, b)


def _make_group_linear(name):
    @jax.custom_vjp
    def op(x, w):
        return _group_matmul(x, w, "nn", name + "_fwd")

    def fwd(x, w):
        return op(x, w), (x, w)

    def bwd(res, dy):
        x, w = res
        return _group_matmul(dy, w, "nt", name + "_dx"), _group_matmul(x, dy, "tn", name + "_dw")

    op.defvjp(fwd, bwd)
    return op


def _my_place():
    return lax.axis_index("x"), lax.axis_index("y"), lax.axis_index("c")


def _stack_cols(full):
    k, n8 = full.shape
    return full.reshape(k, N_DEV, n8 // N_DEV).transpose(1, 0, 2)


def _unstack_cols(stacked):
    j, k, n = stacked.shape
    return stacked.transpose(1, 0, 2).reshape(k, j * n)


def _split_cols(p, cuts):
    bounds = (0,) + tuple(cuts) + (p.shape[1],)

    @jax.custom_vjp
    def op(z):
        return tuple(z[:, lo:hi] for lo, hi in zip(bounds[:-1], bounds[1:]))

    op.defvjp(lambda z: (op(z), None), lambda _, cots: (jnp.concatenate(cots, axis=1),))
    return op(p)


def _make_slot_linear(name):
    @jax.custom_vjp
    def op(x, w_full, slot):
        return _matmul(x, w_full, "nn", name + "_fwd")

    def fwd(x, w_full, slot):
        return op(x, w_full, slot), (x, w_full)

    def bwd(res, dy):
        x, w = res
        return _matmul(dy, w, "nt", name + "_dx"), jnp.zeros_like(w), _matmul(x, dy, "tn", name + "_dw")

    op.defvjp(fwd, bwd)
    return op


def _pack_rows(gs):
    flat = jnp.concatenate([g.reshape(-1) for g in gs])
    n = flat.shape[0]
    rows = -(-n // (256 * LANES)) * 256
    return jnp.pad(flat, (0, rows * LANES - n)).reshape(rows, LANES)


def _unpack_rows(packed, like):
    flat, out, off = packed.reshape(-1), [], 0
    for g in like:
        out.append(flat[off:off + g.size].reshape(g.shape))
        off += g.size
    return out


_HBM = pl.BlockSpec(memory_space=pltpu.HBM)
_SEM = pl.BlockSpec(memory_space=pltpu.SEMAPHORE)
_SIDE_EFFECT = pltpu.SideEffectType.DATAFLOW_SIDE_EFFECTING
_N_PEERS = N_DEV - 1


def _peer(k):
    x, y, c = _my_place()
    return x ^ ((k >> 2) & 1), y ^ ((k >> 1) & 1), c ^ (k & 1)


def _exchange_start(src, land_shape, gather, name, after=()):
    def body(src_ref, land_ref, *rest):
        send_sems, recv_sems, src_thru, land_thru, token = rest[len(after):]
        x, y, c = _my_place()
        me = 4 * x + 2 * y + c
        for k in range(1, N_DEV):
            px, py, pc = _peer(k)
            pltpu.make_async_remote_copy(
                src_ref=src_ref if gather else src_ref.at[4 * px + 2 * py + pc],
                dst_ref=land_ref.at[me] if gather else land_ref.at[k - 1],
                send_sem=send_sems.at[k - 1],
                recv_sem=recv_sems.at[k - 1],
                device_id=(px, py, pc),
                device_id_type=pl.DeviceIdType.MESH,
            ).start()
        token[...] = jnp.zeros_like(token)

    return pl.pallas_call(
        body,
        name=name,
        out_shape=(
            pltpu.SemaphoreType.DMA((_N_PEERS,)),
            pltpu.SemaphoreType.DMA((_N_PEERS,)),
            pltpu.HBM(src.shape, src.dtype),
            pltpu.HBM(land_shape, src.dtype),
            jax.ShapeDtypeStruct((8, LANES), F32),
        ),
        in_specs=(_HBM, _HBM) + (pl.BlockSpec(memory_space=pl.ANY),) * len(after),
        out_specs=(_SEM, _SEM, _HBM, _HBM, pl.BlockSpec(memory_space=pltpu.VMEM)),
        input_output_aliases={0: 2, 1: 3},
        compiler_params=pltpu.CompilerParams(has_side_effects=_SIDE_EFFECT),
    )(pltpu.with_memory_space_constraint(src, pltpu.HBM),
      pltpu.with_memory_space_constraint(lax.empty(land_shape, src.dtype), pltpu.HBM), *after)


def _gather_start_all(shards, name):
    n = len(shards)

    def body(*refs):
        srcs, lands = refs[:n], refs[n:2 * n]
        outs = refs[2 * n:]
        send_sems, recv_sems, token = outs[:n], outs[n:2 * n], outs[-1]
        x, y, c = _my_place()
        me = 4 * x + 2 * y + c
        for i in range(n):
            for k in range(1, N_DEV):
                pltpu.make_async_remote_copy(
                    src_ref=srcs[i],
                    dst_ref=lands[i].at[me],
                    send_sem=send_sems[i].at[k - 1],
                    recv_sem=recv_sems[i].at[k - 1],
                    device_id=_peer(k),
                    device_id_type=pl.DeviceIdType.MESH,
                ).start()
        token[...] = jnp.zeros_like(token)

    lands = [(N_DEV,) + s.shape for s in shards]
    sems = tuple(pltpu.SemaphoreType.DMA((_N_PEERS,)) for _ in range(2 * n))
    res = pl.pallas_call(
        body,
        name=name,
        out_shape=sems + tuple(pltpu.HBM(s.shape, s.dtype) for s in shards)
        + tuple(pltpu.HBM(ls, s.dtype) for ls, s in zip(lands, shards)) + (jax.ShapeDtypeStruct((8, LANES), F32),),
        in_specs=(_HBM,) * (2 * n),
        out_specs=(_SEM,) * (2 * n) + (_HBM,) * (2 * n) + (pl.BlockSpec(memory_space=pltpu.VMEM),),
        input_output_aliases={i: 2 * n + i for i in range(2 * n)},
        compiler_params=pltpu.CompilerParams(has_side_effects=_SIDE_EFFECT),
    )(*[pltpu.with_memory_space_constraint(s, pltpu.HBM) for s in shards],
      *[pltpu.with_memory_space_constraint(lax.empty(ls, s.dtype), pltpu.HBM) for ls, s in zip(lands, shards)])
    return [(res[i], res[n + i], res[2 * n + i], res[3 * n + i], res[-1]) for i in range(n)]


def _exchange_wait(handle, gather, after, name):
    send_sems, recv_sems, src_thru, land_thru, _ = handle

    def body(src_ref, land_ref, send_sems, recv_sems, after_ref, src_dead, got_ref):
        for k in range(1, N_DEV):
            cp = pltpu.make_async_remote_copy(
                src_ref=src_ref if gather else src_ref.at[k],
                dst_ref=land_ref.at[k - 1],
                send_sem=send_sems.at[k - 1],
                recv_sem=recv_sems.at[k - 1],
                device_id=_peer(k),
                device_id_type=pl.DeviceIdType.MESH,
            )
            cp.wait_send()
            cp.wait_recv()

    return pl.pallas_call(
        body,
        name=name,
        out_shape=(pltpu.HBM(src_thru.shape, src_thru.dtype), pltpu.HBM(land_thru.shape, land_thru.dtype)),
        in_specs=(_HBM, _HBM, _SEM, _SEM, pl.BlockSpec(memory_space=pl.ANY)),
        out_specs=(_HBM, _HBM),
        input_output_aliases={0: 0, 1: 1},
        compiler_params=pltpu.CompilerParams(has_side_effects=_SIDE_EFFECT),
    )(src_thru, land_thru, send_sems, recv_sems, after)[1]


def _sum_own_and_peers(own, land, name):
    r, c = own.shape
    tr = _pick(r, (256, 128, 64, 32, 16, 8))

    def body(o_ref, l_ref, out_ref):
        s = [l_ref[j] for j in range(_N_PEERS)]
        out_ref[...] = ((o_ref[...] + s[0]) + (s[1] + s[2])) + ((s[3] + s[4]) + (s[5] + s[6]))

    return pl.pallas_call(
        body,
        name=name,
        grid=(r // tr,),
        in_specs=[pl.BlockSpec((tr, c), lambda i: (i, 0)), pl.BlockSpec((_N_PEERS, tr, c), lambda i: (0, i, 0))],
        out_specs=pl.BlockSpec((tr, c), lambda i: (i, 0)),
        out_shape=jax.ShapeDtypeStruct((r, c), own.dtype),
        compiler_params=_params("parallel"),
    )(own, land)


class _Cols:
    def __init__(self, array, width, block):
        self.array, self.width, self.block = array, width, block
        self.shape, self.dtype = (array.shape[0], width), array.dtype


def _base(a):
    return a.array if isinstance(a, _Cols) else a


def _col_block(a):
    return a.block if isinstance(a, _Cols) else 0


def _make_rowwise(f, name, n_rows, n_tabs, n_pars):
    n_in = n_rows + n_tabs + n_pars

    def specs(args, tm):
        blocked = [pl.BlockSpec((tm, a.shape[1]), lambda i, blk=_col_block(a): (i, blk)) for a in args[: n_rows + n_tabs]]
        whole = [pl.BlockSpec(a.shape, lambda i: (0, 0)) for a in args[n_rows + n_tabs:]]
        return blocked + whole

    def out_struct(args, tm):
        blk = [jax.ShapeDtypeStruct((tm, a.shape[1]), a.dtype) for a in args[: n_rows + n_tabs]]
        blk += [jax.ShapeDtypeStruct(a.shape, a.dtype) for a in args[n_rows + n_tabs:]]
        return jax.eval_shape(f, *blk)

    def fwd_call(*args):
        r = args[0].shape[0]
        tm = _row_tile(r, max(a.shape[1] for a in args[:n_rows]))
        ro, so = out_struct(args, tm)

        def body(*refs):
            vals = [x[...] for x in refs[:n_in]]
            outs = refs[n_in:]
            rv, sv = f(*vals)
            for o, v in zip(outs[: len(ro)], rv):
                o[...] = v
            for o, v in zip(outs[len(ro):], sv):
                @pl.when(pl.program_id(0) == 0)
                def _(o=o, v=v):
                    o[...] = v

                @pl.when(pl.program_id(0) != 0)
                def _(o=o, v=v):
                    o[...] += v

        out_shape = [jax.ShapeDtypeStruct((r, s.shape[1]), s.dtype) for s in ro]
        out_shape += [jax.ShapeDtypeStruct(s.shape, s.dtype) for s in so]
        out_specs = [pl.BlockSpec((tm, s.shape[1]), lambda i: (i, 0)) for s in ro]
        out_specs += [pl.BlockSpec(s.shape, lambda i: (0, 0)) for s in so]
        res = pl.pallas_call(
            body,
            name=name + "_fwd",
            grid=(r // tm,),
            in_specs=specs(args, tm),
            out_specs=out_specs,
            out_shape=out_shape,
            compiler_params=_params("arbitrary" if so else "parallel"),
        )(*[_base(a) for a in args])
        return tuple(res[: len(ro)]), tuple(res[len(ro):])

    def bwd_call(args, cots, more=(), row_dtypes=None):
        r = args[0].shape[0]
        tm = _row_tile(r, max(a.shape[1] for a in args[:n_rows]))
        ro, so = out_struct(args, tm)
        crow, csum = cots
        rows, tabs, pars = args[:n_rows], args[n_rows:n_rows + n_tabs], args[n_rows + n_tabs:]
        n_c = len(crow) + len(csum)

        def body(*refs):
            vals = [x[...] for x in refs[:n_in]]
            cv = [x[...] for x in refs[n_in:n_in + n_c]]
            for x in refs[n_in + n_c:n_in + n_c + len(more)]:
                cv[0] = cv[0] + x[...]
            outs = refs[n_in + n_c + len(more):]
            tv = vals[n_rows:n_rows + n_tabs]

            def g(*dargs):
                return f(*dargs[:n_rows], *tv, *dargs[n_rows:])

            _, vjp = jax.vjp(g, *vals[:n_rows], *vals[n_rows + n_tabs:])
            d = vjp((tuple(cv[: len(crow)]), tuple(cv[len(crow):])))
            for o, v in zip(outs[:n_rows], d[:n_rows]):
                o[...] = v.astype(o.dtype)
            for o, v in zip(outs[n_rows:], d[n_rows:]):
                @pl.when(pl.program_id(0) == 0)
                def _(o=o, v=v):
                    o[...] = v

                @pl.when(pl.program_id(0) != 0)
                def _(o=o, v=v):
                    o[...] += v

        in_specs = specs(args, tm)
        in_specs += [pl.BlockSpec((tm, c.shape[1]), lambda i: (i, 0)) for c in crow]
        in_specs += [pl.BlockSpec(c.shape, lambda i: (0, 0)) for c in csum]
        in_specs += [pl.BlockSpec((tm, c.shape[1]), lambda i: (i, 0)) for c in more]
        out_shape = [jax.ShapeDtypeStruct(a.shape, dt) for a, dt in zip(rows, row_dtypes or [a.dtype for a in rows])]
        out_shape += [jax.ShapeDtypeStruct(a.shape, a.dtype) for a in pars]
        out_specs = [pl.BlockSpec((tm, a.shape[1]), lambda i: (i, 0)) for a in rows]
        out_specs += [pl.BlockSpec(a.shape, lambda i: (0, 0)) for a in pars]
        res = pl.pallas_call(
            body,
            name=name + "_bwd",
            grid=(r // tm,),
            in_specs=in_specs,
            out_specs=out_specs,
            out_shape=out_shape,
            compiler_params=_params("arbitrary" if pars else "parallel"),
        )(*[_base(a) for a in args], *crow, *csum, *more)
        return tuple(res[:n_rows]), tuple(res[n_rows:])

    @jax.custom_vjp
    def op(rows, tabs, pars):
        return fwd_call(*rows, *tabs, *pars)

    op.fwd_call, op.bwd_call = fwd_call, bwd_call

    def fwd(rows, tabs, pars):
        return fwd_call(*rows, *tabs, *pars), (rows, tabs, pars)

    def bwd(res, cots):
        rows, tabs, pars = res
        drows, dpars = bwd_call(tuple(rows) + tuple(tabs) + tuple(pars), cots)
        return drows, tuple(jnp.zeros_like(t) for t in tabs), dpars

    op.defvjp(fwd, bwd)
    return op


def _sigmoid(x):
    return 0.5 * (jnp.tanh(0.5 * x) + 1.0)


@jax.custom_jvp
def _softplus(x):
    e = jnp.exp(-jnp.abs(x))
    u = 1.0 + e
    log1p_e = jnp.where(u == 1.0, e, e * jnp.log(u) / jnp.where(u == 1.0, 1.0, u - 1.0))
    return jnp.maximum(x, 0.0) + log1p_e


@_softplus.defjvp
def _softplus_jvp(primals, tangents):
    (x,), (t,) = primals, tangents
    return _softplus(x), t * _sigmoid(x)


def _gelu(x):
    return 0.5 * x * (1.0 + jnp.tanh(math.sqrt(2.0 / math.pi) * (x + 0.044715 * (x * x * x))))


def _ln_res_f(h, mix, g, b):
    z = DN_ALPHA * h + mix
    mu = jnp.mean(z, axis=-1, keepdims=True)
    zc = z - mu
    var = jnp.mean(zc * zc, axis=-1, keepdims=True)
    return (zc * lax.rsqrt(var + EPS) * g + b,), ()


def _ln_res_copy_f(h, mix, g, b):
    (out,), _ = _ln_res_f(h, mix, g, b)
    return (out, out.astype(BF16)), ()


def _rmsnorm_f(x, g):
    return (x * lax.rsqrt(jnp.mean(x * x, axis=-1, keepdims=True) + EPS) * g,), ()


def _lru_gates_f(ga, gx, xc, b_a, b_x, lam):
    r = _sigmoid(ga + b_a)
    i = _sigmoid(gx + b_x)
    log_a = -LRU_C * r * _softplus(-lam)
    a = jnp.exp(log_a)
    one_minus_a2 = jnp.tanh(-log_a) * (jnp.exp(2.0 * log_a) + 1.0)
    return (a, jnp.sqrt(one_minus_a2) * (i * xc)), ()


def _lru_out_f(hh, p_gate):
    return (hh * _gelu(p_gate),), ()


def _rope_ret_f(q, k, cos2, sin2):
    d = cos2.shape[1]
    half = d // 2
    k_scale = d ** -0.5

    def rope(x):
        outs = []
        for h in range(x.shape[1] // d):
            xh = x[:, h * d:(h + 1) * d]
            rot = jnp.concatenate([xh[:, half:], xh[:, :half]], axis=1)
            outs.append(xh * cos2 + rot * sin2)
        return jnp.concatenate(outs, axis=1)

    return (rope(q), rope(k) * k_scale), ()


def _ret_out_f(o, g):
    d = o.shape[1] // RET_HEADS
    outs = []
    for h in range(RET_HEADS):
        oh = o[:, h * d:(h + 1) * d]
        outs.append(oh * lax.rsqrt(jnp.mean(oh * oh, axis=-1, keepdims=True) + EPS))
    y = jnp.concatenate(outs, axis=1)
    return (g * _sigmoid(g) * y,), ()


def _ret_out_bf16_f(o, g):
    (y,), _ = _ret_out_f(o, g)
    return (y.astype(BF16),), ()


def _loss_f(y, t, mask):
    e = (y - t) * mask
    per_row = jnp.sum(e * e, axis=-1, keepdims=True) * (0.5 / y.shape[1])
    total = jnp.sum(per_row, axis=0, keepdims=True)
    return (), (jnp.broadcast_to(total, (1, LANES)),)


def _shift_down(x, s):
    if s == 0:
        return x
    t = x.shape[0]
    row = lax.broadcasted_iota(jnp.int32, x.shape, 0)
    return jnp.where(row >= s, pltpu.roll(x, s, 0), 0.0)


def _shift_up(x, s):
    if s == 0:
        return x
    t = x.shape[0]
    row = lax.broadcasted_iota(jnp.int32, x.shape, 0)
    return jnp.where(row < t - s, pltpu.roll(x, t - s, 0), 0.0)


def _conv_fwd(x, w, b, name):
    bsz, t, c = x.shape
    width = w.shape[0]

    def body(x_ref, w_ref, b_ref, y_ref):
        xv = x_ref[0]
        acc = jnp.broadcast_to(b_ref[...], xv.shape)
        for k in range(width):
            acc = acc + w_ref[k:k + 1, :] * _shift_down(xv, width - 1 - k)
        y_ref[0] = acc

    return pl.pallas_call(
        body,
        name=name,
        grid=(bsz, c // LANES),
        in_specs=[
            pl.BlockSpec((1, t, LANES), lambda i, j: (i, 0, j)),
            pl.BlockSpec((width, LANES), lambda i, j: (0, j)),
            pl.BlockSpec((1, LANES), lambda i, j: (0, j)),
        ],
        out_specs=pl.BlockSpec((1, t, LANES), lambda i, j: (i, 0, j)),
        out_shape=jax.ShapeDtypeStruct(x.shape, F32),
        compiler_params=_params("parallel", "parallel"),
    )(x, w, b)


def _conv_bwd(x, w, dy, name):
    bsz, t, c = x.shape
    width = w.shape[0]

    def body(x_ref, w_ref, dy_ref, dx_ref, dw_ref, db_ref):
        xv, g = x_ref[0], dy_ref[0]
        dx = jnp.zeros_like(xv)
        dws = []
        for k in range(width):
            s = width - 1 - k
            dx = dx + w_ref[k:k + 1, :] * _shift_up(g, s)
            dws.append(jnp.sum(g * _shift_down(xv, s), axis=0, keepdims=True))
        dx_ref[0] = dx
        dw = jnp.concatenate(dws, axis=0)
        db = jnp.sum(g, axis=0, keepdims=True)

        @pl.when(pl.program_id(1) == 0)
        def _():
            dw_ref[...] = dw
            db_ref[...] = db

        @pl.when(pl.program_id(1) != 0)
        def _():
            dw_ref[...] += dw
            db_ref[...] += db

    return pl.pallas_call(
        body,
        name=name,
        grid=(c // LANES, bsz),
        in_specs=[
            pl.BlockSpec((1, t, LANES), lambda j, i: (i, 0, j)),
            pl.BlockSpec((width, LANES), lambda j, i: (0, j)),
            pl.BlockSpec((1, t, LANES), lambda j, i: (i, 0, j)),
        ],
        out_specs=[
            pl.BlockSpec((1, t, LANES), lambda j, i: (i, 0, j)),
            pl.BlockSpec((width, LANES), lambda j, i: (0, j)),
            pl.BlockSpec((1, LANES), lambda j, i: (0, j)),
        ],
        out_shape=[
            jax.ShapeDtypeStruct(x.shape, F32),
            jax.ShapeDtypeStruct(w.shape, F32),
            jax.ShapeDtypeStruct((1, c), F32),
        ],
        compiler_params=_params("parallel", "arbitrary"),
    )(x, w, dy)


def _make_conv(name):
    @jax.custom_vjp
    def op(x, w, b):
        return _conv_fwd(x, w, b, name + "_fwd")

    def fwd(x, w, b):
        return op(x, w, b), (x, w)

    def bwd(res, dy):
        x, w = res
        return tuple(_conv_bwd(x, w, dy, name + "_bwd"))

    op.defvjp(fwd, bwd)
    return op


_SCAN_ROWS = 8


def _scan_fwd(a, b, name):
    bsz, t, c = a.shape
    cw = _pick(c, (4 * LANES, 2 * LANES, LANES))

    def body(a_ref, b_ref, h_ref):
        row = lax.broadcasted_iota(jnp.int32, (_SCAN_ROWS, cw), 0)

        def step(i, carry):
            r0 = pl.multiple_of(i * _SCAN_ROWS, _SCAN_ROWS)
            av, bv = a_ref[0, pl.ds(r0, _SCAN_ROWS), :], b_ref[0, pl.ds(r0, _SCAN_ROWS), :]
            for s in (1, 2, 4):
                a_sh = jnp.where(row >= s, pltpu.roll(av, s, 0), 1.0)
                b_sh = jnp.where(row >= s, pltpu.roll(bv, s, 0), 0.0)
                bv = av * b_sh + bv
                av = av * a_sh
            hv = bv + av * carry
            h_ref[0, pl.ds(r0, _SCAN_ROWS), :] = hv
            return hv[_SCAN_ROWS - 1:, :]

        lax.fori_loop(0, t // _SCAN_ROWS, step, jnp.zeros((1, cw), F32), unroll=2)

    spec = pl.BlockSpec((1, t, cw), lambda i, j: (i, 0, j))
    return pl.pallas_call(
        body,
        name=name,
        grid=(bsz, c // cw),
        in_specs=[spec, spec],
        out_specs=spec,
        out_shape=jax.ShapeDtypeStruct(a.shape, F32),
        compiler_params=_params("parallel", "parallel"),
    )(a, b)


def _scan_bwd(a, h, g, name):
    bsz, t, c = a.shape
    cw = _pick(c, (2 * LANES, LANES))

    def body(a_ref, h_ref, g_ref, da_ref, db_ref):
        rows = _SCAN_ROWS
        row = lax.broadcasted_iota(jnp.int32, (rows, cw), 0)
        n_tiles = t // rows

        def step(n, carry):
            lam_next, a_next = carry
            i = n_tiles - 1 - n
            r0 = pl.multiple_of(i * rows, rows)
            rp = pl.multiple_of(jnp.maximum(i - 1, 0) * rows, rows)
            av, gv, hv = a_ref[0, pl.ds(r0, rows), :], g_ref[0, pl.ds(r0, rows), :], h_ref[0, pl.ds(r0, rows), :]
            h_before = jnp.where(i > 0, h_ref[0, pl.ds(rp, rows), :][rows - 1:, :], 0.0)
            cv = jnp.where(row < rows - 1, pltpu.roll(av, rows - 1, 0), a_next)
            for s in (1, 2, 4):
                c_sh = jnp.where(row < rows - s, pltpu.roll(cv, rows - s, 0), 1.0)
                g_sh = jnp.where(row < rows - s, pltpu.roll(gv, rows - s, 0), 0.0)
                gv = cv * g_sh + gv
                cv = cv * c_sh
            lam = gv + cv * lam_next
            db_ref[0, pl.ds(r0, rows), :] = lam
            da_ref[0, pl.ds(r0, rows), :] = lam * jnp.where(row >= 1, pltpu.roll(hv, 1, 0), h_before)
            return lam[:1, :], av[:1, :]

        zero = jnp.zeros((1, cw), F32)
        lax.fori_loop(0, n_tiles, step, (zero, zero), unroll=2)

    spec = pl.BlockSpec((1, t, cw), lambda i, j: (i, 0, j))
    return pl.pallas_call(
        body,
        name=name,
        grid=(bsz, c // cw),
        in_specs=[spec, spec, spec],
        out_specs=[spec, spec],
        out_shape=[jax.ShapeDtypeStruct(a.shape, F32)] * 2,
        compiler_params=_params("parallel", "parallel"),
    )(a, h, g)


def _make_scan(name):
    @jax.custom_vjp
    def op(a, b):
        return _scan_fwd(a, b, name + "_fwd")

    def fwd(a, b):
        h = op(a, b)
        return h, (a, h)

    def bwd(res, g):
        a, h = res
        da, db = _scan_bwd(a, h, g, name + "_bwd")
        return da, db

    op.defvjp(fwd, bwd)
    return op


def _query_blocks(t):
    blocks, start = [], 0
    while start < t:
        rows = 2 * SEQ_BLOCK if start + 2 * SEQ_BLOCK <= t else SEQ_BLOCK
        blocks.append((start, rows))
        start += rows
    return blocks


def _attn_exp(q, k, start, scale):
    tq, tk = q.shape[0], k.shape[0]
    s = _dot(q, k, "nt") * scale
    qpos = start + lax.broadcasted_iota(jnp.int32, (tq, tk), 0)
    kpos = lax.broadcasted_iota(jnp.int32, (tq, tk), 1)
    s = jnp.where(kpos <= qpos, s, NEG_INF)
    e = jnp.exp(s - jnp.max(s, axis=-1, keepdims=True))
    return e, 1.0 / jnp.sum(e, axis=-1, keepdims=True)


_MLA_SCALE = (MLA_NOPE + MLA_ROPE) ** -0.5


def _attn_specs(t):
    head = pl.BlockSpec((1, t, LANES), lambda b, h: (b, 0, h))
    shared = pl.BlockSpec((1, t, LANES), lambda b, h: (b, 0, 0))
    return head, shared


def _attn_fwd(q, kv, kpe, name):
    bsz, t, hl = q.shape
    head, shared = _attn_specs(t)

    def body(q_ref, kv_ref, kpe_ref, o_ref, k_s, v_s):
        lane = lax.broadcasted_iota(jnp.int32, (t, LANES), 1)
        kvh = kv_ref[0]
        k_s[...] = jnp.where(lane < MLA_NOPE, kvh, kpe_ref[0]).astype(BF16)
        v_s[...] = kvh.astype(BF16)
        for start, rows in _query_blocks(t):
            n = start + rows
            e, inv_l = _attn_exp(q_ref[0, start:n, :], k_s[:n, :], start, _MLA_SCALE)
            o_ref[0, start:n, :] = _dot(e, v_s[:n, :], "nn") * inv_l

    return pl.pallas_call(
        body,
        name=name,
        grid=(bsz, hl // LANES),
        in_specs=[head, head, shared],
        out_specs=head,
        out_shape=jax.ShapeDtypeStruct(q.shape, F32),
        scratch_shapes=[pltpu.VMEM((t, LANES), BF16), pltpu.VMEM((t, LANES), BF16)],
        compiler_params=_params("parallel", "parallel"),
    )(q, kv, kpe)


def _attn_bwd(q, kv, kpe, do, name):
    bsz, t, hl = q.shape
    head, shared = _attn_specs(t)

    def body(q_ref, kv_ref, kpe_ref, do_ref, dq_ref, dkv_ref, dkpe_ref, k_s, v_s, dk_s, dv_s):
        lane = lax.broadcasted_iota(jnp.int32, (t, LANES), 1)
        kvh = kv_ref[0]
        k_s[...] = jnp.where(lane < MLA_NOPE, kvh, kpe_ref[0]).astype(BF16)
        v_s[...] = kvh.astype(BF16)
        for start, rows in reversed(_query_blocks(t)):
            n = start + rows
            qb = q_ref[0, start:n, :]
            dob = jnp.where(lane[:rows] >= MLA_NOPE, do_ref[0, start:n, :], 0.0)
            kk, vv = k_s[:n, :], v_s[:n, :]
            e, inv_l = _attn_exp(qb, kk, start, _MLA_SCALE)
            p = e * inv_l
            dp = _dot(dob, vv, "nt")
            ds = p * (dp - jnp.sum(dp * p, axis=-1, keepdims=True)) * _MLA_SCALE
            dq_ref[0, start:n, :] = _dot(ds, kk, "nn")
            if n == t:
                dk_s[...] = _dot(ds, qb, "tn")
                dv_s[...] = _dot(p, dob, "tn")
            else:
                dk_s[:n, :] += _dot(ds, qb, "tn")
                dv_s[:n, :] += _dot(p, dob, "tn")
        dk = dk_s[...]
        dkv_ref[0] = jnp.where(lane < MLA_NOPE, dk, dv_s[...])
        dkpe = jnp.where(lane >= MLA_NOPE, dk, 0.0)

        @pl.when(pl.program_id(1) == 0)
        def _():
            dkpe_ref[0] = dkpe

        @pl.when(pl.program_id(1) != 0)
        def _():
            dkpe_ref[0] += dkpe

    return pl.pallas_call(
        body,
        name=name,
        grid=(bsz, hl // LANES),
        in_specs=[head, head, shared, head],
        out_specs=[head, head, shared],
        out_shape=[
            jax.ShapeDtypeStruct(q.shape, F32),
            jax.ShapeDtypeStruct(kv.shape, F32),
            jax.ShapeDtypeStruct(kpe.shape, F32),
        ],
        scratch_shapes=[pltpu.VMEM((t, LANES), BF16), pltpu.VMEM((t, LANES), BF16),
                        pltpu.VMEM((t, LANES), F32), pltpu.VMEM((t, LANES), F32)],
        compiler_params=_params("parallel", "arbitrary"),
    )(q, kv, kpe, do)


def _make_attention(name):
    @jax.custom_vjp
    def op(q, kv, kpe):
        return _attn_fwd(q, kv, kpe, name + "_fwd")

    def fwd(q, kv, kpe):
        return op(q, kv, kpe), (q, kv, kpe)

    def bwd(res, do):
        return tuple(_attn_bwd(*res, do, name + "_bwd"))

    op.defvjp(fwd, bwd)
    return op


_ROPE_SHIFT = MLA_ROPE // 2


def _rope_lanes_call(x, c, sm, sp, transpose, name):
    r, width = x.shape
    tm = _row_tile(r, width)

    def body(x_ref, c_ref, sm_ref, sp_ref, y_ref):
        cv, smv, spv = c_ref[...], sm_ref[...], sp_ref[...]
        for b in range(width // LANES):
            xb = x_ref[:, b * LANES:(b + 1) * LANES]
            if transpose:
                yb = xb * cv + pltpu.roll(xb * smv, _ROPE_SHIFT, 1) + pltpu.roll(xb * spv, LANES - _ROPE_SHIFT, 1)
            else:
                yb = xb * cv + pltpu.roll(xb, LANES - _ROPE_SHIFT, 1) * smv + pltpu.roll(xb, _ROPE_SHIFT, 1) * spv
            y_ref[:, b * LANES:(b + 1) * LANES] = yb

    tab = pl.BlockSpec((tm, LANES), lambda i: (i, 0))
    blk = pl.BlockSpec((tm, width), lambda i: (i, 0))
    return pl.pallas_call(
        body,
        name=name,
        grid=(r // tm,),
        in_specs=[blk, tab, tab, tab],
        out_specs=blk,
        out_shape=jax.ShapeDtypeStruct(x.shape, F32),
        compiler_params=_params("parallel"),
    )(x, c, sm, sp)


def _make_rope_lanes(name):
    @jax.custom_vjp
    def op(x, c, sm, sp):
        return _rope_lanes_call(x, c, sm, sp, False, name + "_fwd")

    def fwd(x, c, sm, sp):
        return op(x, c, sm, sp), (c, sm, sp)

    def bwd(res, dy):
        c, sm, sp = res
        return _rope_lanes_call(dy, c, sm, sp, True, name + "_bwd"), jnp.zeros_like(c), jnp.zeros_like(sm), jnp.zeros_like(sp)

    op.defvjp(fwd, bwd)
    return op


def _ret_chunk_rows(t):
    return t // 4 if t % 32 == 0 else SEQ_BLOCK


def _ret_decays(c, log_gamma):
    row = lax.broadcasted_iota(jnp.int32, (c, 1), 0)
    col = lax.broadcasted_iota(jnp.int32, (1, c), 1)
    rowf = row.astype(F32)
    d = jnp.where(row >= col, jnp.exp(log_gamma * rowf) * jnp.exp(-log_gamma * col.astype(F32)), 0.0)
    return d, jnp.exp(log_gamma * (rowf + 1.0)), jnp.exp(log_gamma * (c - 1.0 - rowf)), jnp.exp(log_gamma * c)


def _ret_specs(c, dk, dv, v_block0, n_chunks, reverse):
    pos = (lambda i: n_chunks - 1 - i) if reverse else (lambda i: i)
    return (
        pl.BlockSpec(memory_space=pltpu.SMEM),
        pl.BlockSpec((1, c, dk), lambda b, h, i: (b, pos(i), h)),
        pl.BlockSpec((1, c, dv), lambda b, h, i: (b, pos(i), h + v_block0)),
        pl.BlockSpec((1, c, dv), lambda b, h, i: (b, pos(i), h)),
        pl.BlockSpec((1, 1, dk, dv), lambda b, h, i: (b, h * n_chunks + pos(i), 0, 0)),
    )


def _ret_fwd(lg, q, k, v, name, dv=None, v_block0=0):
    bsz, t, hdk = q.shape
    heads = lg.shape[0]
    dk, dv = hdk // heads, dv or v.shape[2] // heads
    c = _ret_chunk_rows(t)
    n_chunks = t // c
    lg_spec, qk_spec, v_spec, o_spec, s_spec = _ret_specs(c, dk, dv, v_block0, n_chunks, False)

    def body(lg_ref, q_ref, k_ref, v_ref, o_ref, s_ref, state):
        @pl.when(pl.program_id(2) == 0)
        def _():
            state[...] = jnp.zeros((dk, dv), F32)

        d, a, b, g = _ret_decays(c, lg_ref[pl.program_id(1)])
        qb, kb, vb, s_in = q_ref[0], k_ref[0], v_ref[0], state[...]
        s_ref[0, 0] = s_in
        o_ref[0] = _dot(_dot(qb, kb, "nt") * d, vb, "nn") + a * _dot(qb, s_in, "nn")
        state[...] = g * s_in + _dot(kb * b, vb, "tn")

    return pl.pallas_call(
        body,
        name=name,
        grid=(bsz, heads, n_chunks),
        in_specs=[lg_spec, qk_spec, qk_spec, v_spec],
        out_specs=[o_spec, s_spec],
        out_shape=[jax.ShapeDtypeStruct((bsz, t, heads * dv), F32),
                   jax.ShapeDtypeStruct((bsz, heads * n_chunks, dk, dv), F32)],
        scratch_shapes=[pltpu.VMEM((dk, dv), F32)],
        compiler_params=_params("parallel", "parallel", "arbitrary"),
    )(lg, q, k, v)


def _ret_bwd(lg, q, k, v, states, do, name, v_block0=0, dv_dtype=F32):
    bsz, t, hdk = q.shape
    heads = lg.shape[0]
    dk, dv = hdk // heads, do.shape[2] // heads
    c = _ret_chunk_rows(t)
    n_chunks = t // c
    lg_spec, qk_spec, v_spec, o_spec, s_spec = _ret_specs(c, dk, dv, v_block0, n_chunks, True)

    def body(lg_ref, q_ref, k_ref, v_ref, s_ref, do_ref, dq_ref, dk_ref, dv_ref, dstate):
        @pl.when(pl.program_id(2) == 0)
        def _():
            dstate[...] = jnp.zeros((dk, dv), F32)

        d, a, b, g = _ret_decays(c, lg_ref[pl.program_id(1)])
        qb, kb, vb, dob, s_in, ds_out = q_ref[0], k_ref[0], v_ref[0], do_ref[0], s_ref[0, 0], dstate[...]
        scores = _dot(qb, kb, "nt") * d
        dscores = _dot(dob, vb, "nt") * d
        dq_ref[0] = _dot(dscores, kb, "nn") + a * _dot(dob, s_in, "nt")
        dk_ref[0] = _dot(dscores, qb, "tn") + b * _dot(vb, ds_out, "nt")
        dv_ref[0] = (_dot(scores, dob, "tn") + _dot(kb * b, ds_out, "nn")).astype(dv_dtype)
        dstate[...] = g * ds_out + _dot(qb, a * dob, "tn")

    return pl.pallas_call(
        body,
        name=name,
        grid=(bsz, heads, n_chunks),
        in_specs=[lg_spec, qk_spec, qk_spec, v_spec, s_spec, o_spec],
        out_specs=[qk_spec, qk_spec, o_spec],
        out_shape=[
            jax.ShapeDtypeStruct(q.shape, F32),
            jax.ShapeDtypeStruct(k.shape, F32),
            jax.ShapeDtypeStruct(do.shape, dv_dtype),
        ],
        scratch_shapes=[pltpu.VMEM((dk, dv), F32)],
        compiler_params=_params("parallel", "parallel", "arbitrary"),
    )(lg, q, k, v, states, do)


def _adamw(w, g, m, v, name):
    r, c = w.shape
    tr = _pick(r, (256, 128, 64, 32, 16, 8))

    def body(w_ref, g_ref, m_ref, v_ref, d_ref, nm_ref, nv_ref):
        gv = g_ref[...]
        nm = ADAM_B1 * m_ref[...] + (1.0 - ADAM_B1) * gv
        nv = ADAM_B2 * v_ref[...] + (1.0 - ADAM_B2) * (gv * gv)
        m_hat = nm / (1.0 - ADAM_B1 ** ADAM_STEP)
        v_hat = nv / (1.0 - ADAM_B2 ** ADAM_STEP)
        d_ref[...] = -ADAM_LR * (m_hat / (jnp.sqrt(v_hat) + ADAM_EPS) + ADAM_WD * w_ref[...])
        nm_ref[...] = nm
        nv_ref[...] = nv

    spec = pl.BlockSpec((tr, c), lambda i: (i, 0))
    return pl.pallas_call(
        body,
        name=name,
        grid=(r // tr,),
        in_specs=[spec] * 4,
        out_specs=[spec] * 3,
        out_shape=[jax.ShapeDtypeStruct((r, c), F32)] * 3,
        compiler_params=_params("parallel"),
    )(w, g, m, v)


def _rope_tables(t, half, reps):
    inv = ROPE_BASE ** (-jnp.arange(half, dtype=F32) / half)
    ang = jnp.arange(t, dtype=jnp.int32).astype(F32)[:, None] * inv[None, :]
    return jnp.tile(jnp.cos(ang), (1, reps)), jnp.tile(jnp.sin(ang), (1, reps))


def _padded_len(seq):
    return -(-(N_META + seq) // SEQ_BLOCK) * SEQ_BLOCK


def _embed(meta, x):
    bsz, seq, d = x.shape
    t = _padded_len(seq)
    return jnp.concatenate(
        [jnp.broadcast_to(meta[None], (bsz, N_META, d)), x, jnp.zeros((bsz, t - N_META - seq, d), F32)], axis=1
    ).reshape(bsz * t, d)


def _even_mixer(p, conv_w, conv_b, w_rg_a, b_rg_a, w_rg_x, b_rg_x, lru_lambda, q_norm_g, uq_slot, kv_norm_g,
                ukv_slot, gathered, bsz):
    w_uq_pad, w_ukv_full = gathered
    r = p.shape[0]
    t = r // bsz

    def tile_rows(tab):
        return jnp.tile(tab, (bsz, 1))

    lru_w = w_rg_a.shape[2] * w_rg_a.shape[1]
    q_rank, kv_rank = q_norm_g.shape[1], kv_norm_g.shape[1]
    p_gate, p_rec, p_q, p_kv, p_kpe = _split_cols(
        p, (lru_w, 2 * lru_w, 2 * lru_w + q_rank, 2 * lru_w + q_rank + kv_rank))

    xc = _make_conv("conv")(p_rec.reshape(bsz, t, lru_w), conv_w, conv_b).reshape(r, lru_w)
    ga = _make_group_linear("rg_a")(xc, w_rg_a[0])
    gx = _make_group_linear("rg_x")(xc, w_rg_x[0])
    (a, bb), _ = _make_rowwise(_lru_gates_f, "lru_gates", 3, 0, 3)((ga, gx, xc), (), (b_rg_a, b_rg_x, lru_lambda))
    hh = _make_scan("lru_scan")(a.reshape(bsz, t, lru_w), bb.reshape(bsz, t, lru_w)).reshape(r, lru_w)
    (y_rec,), _ = _make_rowwise(_lru_out_f, "lru_out", 2, 0, 0)((hh, p_gate), (), ())

    (qn,), _ = _make_rowwise(_rmsnorm_f, "q_norm", 1, 0, 1)((p_q,), (), (q_norm_g,))
    (kvn,), _ = _make_rowwise(_rmsnorm_f, "kv_norm", 1, 0, 1)((p_kv,), (), (kv_norm_g,))
    q = _make_slot_linear("ev_uq")(qn, w_uq_pad, uq_slot)
    kv = _make_slot_linear("ev_ukv")(kvn, w_ukv_full, ukv_slot)
    half = MLA_ROPE // 2
    cos, sin = _rope_tables(t, half, 1)
    one, zero = jnp.ones((t, MLA_NOPE), F32), jnp.zeros((t, MLA_NOPE), F32)
    tail = LANES - MLA_NOPE - MLA_ROPE
    c_tab = tile_rows(jnp.concatenate([one, cos, cos, one[:, :tail]], axis=1))
    sm_tab = tile_rows(jnp.concatenate([zero, -sin, zero[:, :half + tail]], axis=1))
    sp_tab = tile_rows(jnp.concatenate([zero, zero[:, :half], sin, zero[:, :tail]], axis=1))
    q = _make_rope_lanes("rope_q")(q, c_tab, sm_tab, sp_tab)
    kpe = _make_rope_lanes("rope_k")(p_kpe, c_tab, sm_tab, sp_tab)
    o = _make_attention("mla")(q.reshape(bsz, t, -1), kv.reshape(bsz, t, -1), kpe.reshape(bsz, t, LANES))
    return jnp.concatenate([y_rec, o.reshape(r, -1)], axis=1)


def _odd_mixer_fwd(p, bsz):
    r, width = p.shape
    t = r // bsz
    qk = width // 6
    dk = qk // RET_HEADS
    cos2, sin2 = _rope_tables(t, dk // 2, 2)
    sin2 = jnp.concatenate([-sin2[:, :dk // 2], sin2[:, dk // 2:]], axis=1)
    rope_args = (_Cols(p, qk, 0), _Cols(p, qk, 1), jnp.tile(cos2, (bsz, 1)), jnp.tile(sin2, (bsz, 1)))
    (rq, rk), _ = _make_rowwise(_rope_ret_f, "rope_ret", 2, 2, 0).fwd_call(*rope_args)
    lg = jnp.log(1.0 - 2.0 ** (-5.0 - jnp.arange(RET_HEADS, dtype=F32)))
    ret_args = (lg, rq.reshape(bsz, t, qk), rk.reshape(bsz, t, qk), p.reshape(bsz, t, width))
    o, states = _ret_fwd(*ret_args, "ret_fwd", dv=2 * dk, v_block0=qk // dk)
    gate_args = (o.reshape(r, 2 * qk), _Cols(p, 2 * qk, 2))
    (y,), _ = _make_rowwise(_ret_out_bf16_f, "ret_out", 2, 0, 0).fwd_call(*gate_args)
    return y, (rope_args, ret_args + (states,), gate_args)


def _odd_mixer_bwd(res, dy):
    rope_args, ret_args, gate_args = res
    bsz, t, qk = ret_args[1].shape
    dk = qk // RET_HEADS
    (do, dg), _ = _make_rowwise(_ret_out_f, "ret_out", 2, 0, 0).bwd_call(
        gate_args, ((dy,), ()), row_dtypes=(F32, BF16))
    drq, drk, dv = _ret_bwd(*ret_args, do.reshape(bsz, t, 2 * qk), "ret_bwd", v_block0=qk // dk, dv_dtype=BF16)
    (dq, dkk), _ = _make_rowwise(_rope_ret_f, "rope_ret", 2, 2, 0).bwd_call(
        rope_args, ((drq.reshape(bsz * t, qk), drk.reshape(bsz * t, qk)), ()), row_dtypes=(BF16, BF16))
    return jnp.concatenate([dq, dkk, dv.reshape(bsz * t, 2 * qk), dg], axis=1)


def _local_loss(h, target):
    bsz, seq, d = target.shape
    t = _padded_len(seq)
    t_real = N_META + seq
    pos = jnp.arange(t, dtype=jnp.int32)
    mask = jnp.tile(((pos >= N_META) & (pos < t_real)).astype(F32)[:, None], (bsz, 1))
    tgt = jnp.concatenate(
        [jnp.zeros((bsz, N_META, d), F32), target, jnp.zeros((bsz, t - t_real, d), F32)], axis=1).reshape(bsz * t, d)
    _, (total,) = _make_rowwise(_loss_f, "loss", 1, 2, 0)((h,), (tgt, mask), ())
    return total[0, 0]


_WEIGHTS = ("meta_tokens", "ev_w_in", "ev_conv_w", "ev_conv_b", "ev_w_rg_a", "ev_b_rg_a", "ev_w_rg_x", "ev_b_rg_x",
            "ev_lru_lambda", "ev_q_norm_g", "ev_w_uq", "ev_kv_norm_g", "ev_w_ukv", "ev_w_out", "od_w_in", "od_w_out",
            "ln_mix_g", "ln_mix_b", "mlp_w1", "mlp_w2", "ln_mlp_g", "ln_mlp_b")


def kernel(x, meta_tokens, ev_w_in, ev_conv_w, ev_conv_b, ev_w_rg_a, ev_b_rg_a, ev_w_rg_x, ev_b_rg_x, ev_lru_lambda, ev_q_norm_g, ev_w_uq, ev_kv_norm_g, ev_w_ukv, ev_w_out, od_w_in, od_w_out, ln_mix_g, ln_mix_b, mlp_w1, mlp_w2, ln_mlp_g, ln_mlp_b, loss_target, m_meta_tokens, m_ev_w_in, m_ev_conv_w, m_ev_conv_b, m_ev_w_rg_a, m_ev_b_rg_a, m_ev_w_rg_x, m_ev_b_rg_x, m_ev_lru_lambda, m_ev_q_norm_g, m_ev_w_uq, m_ev_kv_norm_g, m_ev_w_ukv, m_ev_w_out, m_od_w_in, m_od_w_out, m_ln_mix_g, m_ln_mix_b, m_mlp_w1, m_mlp_w2, m_ln_mlp_g, m_ln_mlp_b, v_meta_tokens, v_ev_w_in, v_ev_conv_w, v_ev_conv_b, v_ev_w_rg_a, v_ev_b_rg_a, v_ev_w_rg_x, v_ev_b_rg_x, v_ev_lru_lambda, v_ev_q_norm_g, v_ev_w_uq, v_ev_kv_norm_g, v_ev_w_ukv, v_ev_w_out, v_od_w_in, v_od_w_out, v_ln_mix_g, v_ln_mix_b, v_mlp_w1, v_mlp_w2, v_ln_mlp_g, v_ln_mlp_b):
    args = locals()
    weights = {n: args[n] for n in _WEIGHTS}
    bsz = x.shape[0]
    my_x, my_y, my_c = _my_place()
    me = 4 * my_x + 2 * my_y + my_c

    big = (("ev_in", ev_w_in[0], True), ("ev_out", ev_w_out[0], False), ("mlp0_w1", mlp_w1[0], True),
           ("mlp0_w2", mlp_w2[0], False), ("od_in", od_w_in[0], True), ("od_out", od_w_out[0], False),
           ("mlp1_w1", mlp_w1[1], True), ("mlp1_w2", mlp_w2[1], False))
    small_sharded = (("meta", meta_tokens, F32), ("conv_w", ev_conv_w[0], F32), ("ev_uq", ev_w_uq[0], BF16),
                     ("ev_ukv", ev_w_ukv[0], BF16))
    to_gather = (tuple((nm, s.astype(dt), True) for nm, s, dt in small_sharded)
                 + tuple((nm, s.astype(BF16), cols) for nm, s, cols in big))
    handles = _gather_start_all([s for _, s, _ in to_gather], "ag_start")
    gathers = {nm: (s, cols, h) for (nm, s, cols), h in zip(to_gather, handles)}
    gather_tokens = (handles[0][4],)

    def full_weight(nm, after):
        shard16, cols, handle = gathers[nm]
        land = _exchange_wait(handle, True, after, "ag_wait_" + nm)
        land = lax.dynamic_update_index_in_dim(land, shard16, me, 0)
        if cols and shard16.shape[1] % LANES == 0:
            return land, True
        return (_unstack_cols(land) if cols else land.reshape(-1, shard16.shape[1])), False

    meta_full, conv_w_full, w_uq_full, w_ukv_full = (
        _unstack_cols(lax.dynamic_update_index_in_dim(
            _exchange_wait(gathers[nm][2], True, gather_tokens[-1], "ag_wait_" + nm), gathers[nm][0], me, 0))
        for nm, _, _ in small_sharded)

    pending = []

    def linear_bwd(nm, x_in, w, dy, cols, unpad=None, **fused):
        w_full, w_stacked = w
        own = None
        if w_stacked:
            stacked = _matmul(x_in, dy, "tn", nm + "_dw", stacked=True)
        else:
            dw = _matmul(x_in, dy, "tn", nm + "_dw")
            dw = dw if unpad is None else unpad(dw)
            n = dw.shape[1] // N_DEV
            if cols:
                stacked = _stack_cols(dw)
                own = lax.dynamic_slice_in_dim(dw, me * n, n, axis=1)
            else:
                stacked = dw.reshape(N_DEV, dw.shape[0] // N_DEV, dw.shape[1])
        handle = _exchange_start(stacked, (_N_PEERS,) + stacked.shape[1:], False, "rs_start_" + nm)
        if own is None:
            own = lax.dynamic_index_in_dim(handle[2], me, 0, keepdims=False)
        pending.append((nm, own, handle))
        return _matmul(dy, w_full, "nt", nm + "_dx", after=(handle[4],), stacked=w_stacked, **fused)

    def linear_fwd(nm, x_in, w, **fused):
        return _matmul(x_in, w[0], "nn", nm + "_fwd", stacked=w[1], **fused)

    def mlp_fwd(h, h16, l):
        w1 = full_weight(f"mlp{l}_w1", h16)
        a16 = linear_fwd(f"mlp{l}_w1", h16, w1, out_relu2=True, out_dtype=BF16)
        w2 = full_weight(f"mlp{l}_w2", a16)
        f = linear_fwd(f"mlp{l}_w2", a16, w2)
        ln_args = (h, f, ln_mlp_g[l:l + 1], ln_mlp_b[l:l + 1])
        return ln_fwd(f"mlp{l}_ln", *ln_args), (h16, w1, a16, w2, ln_args)

    def mlp_bwd(dout, res, l):
        h16, w1, a16, w2, ln_args = res
        dh, df, dg, db = ln_bwd(f"mlp{l}_ln", ln_args, dout)
        du = linear_bwd(f"mlp{l}_w2", a16, w2, df, False, relu2_bwd_of=a16, out_dtype=BF16)
        return (dh, linear_bwd(f"mlp{l}_w1", h16, w1, du, True)), dg, db

    def ln_fwd(nm, h, mix, g, b):
        return _make_rowwise(_ln_res_copy_f, nm, 2, 0, 2).fwd_call(h, mix, g, b)[0]

    def ln_bwd(nm, ln_args, pieces):
        (dh, dmix), (dg, db) = _make_rowwise(_ln_res_f, nm, 2, 0, 2).bwd_call(
            ln_args, ((pieces[0],), ()), more=tuple(pieces[1:]), row_dtypes=(F32, BF16))
        return dh, dmix, dg, db

    h0, vjp_embed = jax.vjp(_embed, meta_full, x)
    n_in = ev_w_in.shape[2] * N_DEV
    kpe0, pad_lo, pad_hi = n_in - MLA_ROPE, MLA_NOPE, LANES - MLA_NOPE - MLA_ROPE
    w_in = full_weight("ev_in", h0)[0]
    zeros_in = jnp.zeros((w_in.shape[0], pad_lo), BF16)
    w_ev_in = (jnp.concatenate([w_in[:, :kpe0], zeros_in, w_in[:, kpe0:], zeros_in[:, :pad_hi]], axis=1), False)

    def unpad_in(dw):
        return jnp.concatenate([dw[:, :kpe0], dw[:, kpe0 + pad_lo:kpe0 + pad_lo + MLA_ROPE]], axis=1)

    p0 = _matmul(h0, w_ev_in[0], "nn", "ev_in_fwd", after=gather_tokens)
    q_rank, d_head = w_uq_full.shape[0], MLA_NOPE + MLA_ROPE
    w_uq_pad = jnp.pad(w_uq_full.reshape(q_rank, MLA_HEADS, d_head), ((0, 0), (0, 0), (0, LANES - d_head)))
    w_uq_pad = w_uq_pad.reshape(q_rank, MLA_HEADS * LANES)
    small = (conv_w_full, ev_conv_b, ev_w_rg_a, ev_b_rg_a, ev_w_rg_x, ev_b_rg_x, ev_lru_lambda, ev_q_norm_g,
             jnp.zeros(w_uq_pad.shape, F32), ev_kv_norm_g, jnp.zeros(w_ukv_full.shape, F32))
    y0, vjp_even = jax.vjp(lambda p, *s: _even_mixer(p, *s, (w_uq_pad, w_ukv_full), bsz), p0, *small)
    w_out = full_weight("ev_out", y0)[0]
    lru_w, d_model = y0.shape[1] - MLA_HEADS * LANES, w_out.shape[1]
    w_att = w_out[lru_w:].reshape(MLA_HEADS, MLA_V, d_model)
    w_att = jnp.concatenate([jnp.zeros((MLA_HEADS, LANES - MLA_V, d_model), BF16), w_att], axis=1)
    w_ev_out = (jnp.concatenate([w_out[:lru_w], w_att.reshape(MLA_HEADS * LANES, d_model)], axis=0), False)

    def unpad_out(dw):
        d_att = dw[lru_w:].reshape(MLA_HEADS, LANES, d_model)[:, LANES - MLA_V:].reshape(MLA_HEADS * MLA_V, d_model)
        return jnp.concatenate([dw[:lru_w], d_att], axis=0)

    mix0 = linear_fwd("ev_out", y0, w_ev_out)
    ln0_args = (h0, mix0, ln_mix_g[0:1], ln_mix_b[0:1])
    h1, h1_16 = ln_fwd("mix0_ln", *ln0_args)
    (h2, h2_16), res_mlp0 = mlp_fwd(h1, h1_16, 0)
    w_od_in = full_weight("od_in", h2_16)
    p1 = linear_fwd("od_in", h2_16, w_od_in)
    y1, res_odd = _odd_mixer_fwd(p1, bsz)
    w_od_out = full_weight("od_out", y1)
    mix1 = linear_fwd("od_out", y1, w_od_out)
    ln1_args = (h2, mix1, ln_mix_g[1:2], ln_mix_b[1:2])
    h3, h3_16 = ln_fwd("mix1_ln", *ln1_args)
    (h4, _), res_mlp1 = mlp_fwd(h3, h3_16, 1)
    loss_local, vjp_loss = jax.vjp(lambda h: _local_loss(h, loss_target), h4)

    dh4 = vjp_loss(jnp.ones((), F32))
    dh3, dg_mlp1, db_mlp1 = mlp_bwd(dh4, res_mlp1, 1)
    dh2, dmix1, dg_mix1, db_mix1 = ln_bwd("mix1_ln", ln1_args, dh3)
    dp1 = _odd_mixer_bwd(res_odd, linear_bwd("od_out", y1, w_od_out, dmix1, False))
    dh2 = (dh2, linear_bwd("od_in", h2_16, w_od_in, dp1, True))
    dh1, dg_mlp0, db_mlp0 = mlp_bwd(dh2, res_mlp0, 0)
    dh0, dmix0, dg_mix0, db_mix0 = ln_bwd("mix0_ln", ln0_args, dh1)
    dp0, *dsmall = vjp_even(linear_bwd("ev_out", y0, w_ev_out, dmix0, False, unpad=unpad_out))
    dh0 = dh0 + linear_bwd("ev_in", h0, w_ev_in, dp0.astype(BF16), True, unpad=unpad_in)
    g_meta_full, grad_x = vjp_embed(dh0)
    (g_conv_w_full, g_conv_b, g_w_rg_a, g_b_rg_a, g_w_rg_x, g_b_rg_x, g_lambda, g_q_norm, g_uq_pad, g_kv_norm,
     g_ukv_full) = dsmall

    for nm, dw in (("ev_uq", g_uq_pad.reshape(q_rank, MLA_HEADS, LANES)[:, :, :d_head].reshape(q_rank, -1)),
                   ("ev_ukv", g_ukv_full)):
        n = dw.shape[1] // N_DEV
        handle = _exchange_start(_stack_cols(dw), (_N_PEERS, dw.shape[0], n), False, "rs_start_" + nm)
        pending.append((nm, lax.dynamic_slice_in_dim(dw, me * n, n, axis=1), handle))

    rep_names = ("ev_conv_b", "ev_w_rg_a", "ev_b_rg_a", "ev_w_rg_x", "ev_b_rg_x", "ev_lru_lambda", "ev_q_norm_g",
                 "ev_kv_norm_g", "ln_mix_g", "ln_mix_b", "ln_mlp_g", "ln_mlp_b")
    rep_local = (g_conv_b, g_w_rg_a, g_b_rg_a, g_w_rg_x, g_b_rg_x, g_lambda, g_q_norm, g_kv_norm,
                 jnp.concatenate([dg_mix0, dg_mix1]), jnp.concatenate([db_mix0, db_mix1]),
                 jnp.concatenate([dg_mlp0, dg_mlp1]), jnp.concatenate([db_mlp0, db_mlp1]), g_conv_w_full, g_meta_full)
    rep_stacked = _pack_rows(rep_local).reshape(N_DEV, -1, LANES)
    rep_rs = _exchange_start(rep_stacked, (_N_PEERS,) + rep_stacked.shape[1:], False, "rep_rs_start", after=(grad_x,))
    rep_own = lax.dynamic_index_in_dim(rep_rs[2], me, 0, keepdims=False)

    after, summed = rep_rs[4], {}
    for nm, own, handle in pending:
        land = _exchange_wait(handle, False, after, "rs_wait_" + nm)
        summed[nm] = after = _sum_own_and_peers(own, land, "rs_sum_" + nm)
    rep_part = _sum_own_and_peers(rep_own, _exchange_wait(rep_rs, False, after, "rep_rs_wait"), "rep_sum")
    rep_ag = _exchange_start(rep_part, (N_DEV,) + rep_part.shape, True, "rep_ag_start")

    grad_w = dict(ev_w_uq=summed["ev_uq"][None], ev_w_ukv=summed["ev_ukv"][None],
                  ev_w_in=summed["ev_in"][None], ev_w_out=summed["ev_out"][None], od_w_in=summed["od_in"][None],
                  od_w_out=summed["od_out"][None], mlp_w1=jnp.stack([summed["mlp0_w1"], summed["mlp1_w1"]]),
                  mlp_w2=jnp.stack([summed["mlp0_w2"], summed["mlp1_w2"]]))

    loss = lax.psum(loss_local, MESH_AXES)
    delta, new_m, new_v = {}, {}, {}

    def adamw(n):
        w, g, m, v = weights[n], grad_w[n], args["m_" + n], args["v_" + n]
        two_d = (-1, w.shape[-1])
        d2, m2, v2 = _adamw(w.reshape(two_d), g.reshape(two_d), m.reshape(two_d), v.reshape(two_d), "adamw_" + n)
        delta[n], new_m[n], new_v[n] = d2.reshape(w.shape), m2.reshape(w.shape), v2.reshape(w.shape)
        return d2

    for n in tuple(grad_w):
        after = adamw(n)
    rep_land = lax.dynamic_update_index_in_dim(_exchange_wait(rep_ag, True, after, "rep_ag_wait"), rep_part, me, 0)
    *rep_total, t_conv_w, t_meta = _unpack_rows(rep_land, rep_local)
    n_conv, n_meta = ev_conv_w.shape[2], meta_tokens.shape[1]
    small_g = dict(zip(rep_names, rep_total))
    small_g.update(meta_tokens=lax.dynamic_slice_in_dim(t_meta, me * n_meta, n_meta, axis=1),
                   ev_conv_w=lax.dynamic_slice_in_dim(t_conv_w, me * n_conv, n_conv, axis=1)[None])
    grad_w.update(small_g)
    for n in small_g:
        adamw(n)
    return (loss, grad_x, *[grad_w[n] for n in _WEIGHTS], *[delta[n] for n in _WEIGHTS],
            *[new_m[n] for n in _WEIGHTS], *[new_v[n] for n in _WEIGHTS])
```

```python
import math

import jax
import jax.numpy as jnp
from jax import lax
from jax.experimental import pallas as pl
from jax.experimental.pallas import tpu as pltpu

F32 = jnp.float32
BF16 = jnp.bfloat16

N_DEV = 8
MESH_AXES = ("x", "y", "c")
LANES = 128
SEQ_BLOCK = 128

N_META = 16
LRU_C = 8.0
MLA_HEADS = 8
MLA_NOPE = 64
MLA_ROPE = 32
MLA_V = 64
RET_HEADS = 4
ROPE_BASE = 10000.0
DEPTH = 2
DN_ALPHA = (2 * DEPTH) ** 0.25
EPS = 1e-5
NEG_INF = -1e30

ADAM_LR = 0.001
ADAM_B1 = 0.9
ADAM_B2 = 0.999
ADAM_EPS = 1e-08
ADAM_WD = 0.01
ADAM_STEP = 10

VMEM_LIMIT = 56 * 1024 * 1024


def _params(*sem):
    return pltpu.CompilerParams(dimension_semantics=sem, vmem_limit_bytes=VMEM_LIMIT)


def _pick(n, cands):
    for c in cands:
        if n % c == 0:
            return c
    return n


def _row_tile(r, width):
    cands = (256, 128, 64, 32, 16, 8) if width <= 1024 else (128, 64, 32, 16, 8)
    return _pick(r, cands)


_DIMS = {"nn": (((1,), (0,)), ((), ())), "nt": (((1,), (1,)), ((), ())), "tn": (((0,), (0,)), ((), ()))}


def _dot(a, b, mode):
    return lax.dot_general(a.astype(BF16), b.astype(BF16), _DIMS[mode], preferred_element_type=F32)


def _matmul(a, b, mode, name, after=(), stacked=False, relu2_bwd_of=None, out_dtype=F32, out_relu2=False):
    if stacked:
        n_blk = b.shape[2] if mode != "tn" else b.shape[1] // N_DEV
    if mode == "nn":
        (m, k), n = a.shape, (N_DEV * n_blk if stacked else b.shape[1])
    elif mode == "nt":
        (m, k), n = a.shape, (b.shape[1] if stacked else b.shape[0])
    else:
        (k, m), n = a.shape, b.shape[1]
    tm = _pick(m, (1088, 1024, 544, 512, 272, 256, 128, 64, 32, 16, 8))
    tn = _pick(n, (1024, 512, 256, 128))
    tk = _pick(k, (1088, 1024, 544, 512, 272, 256, 128))
    kb = 2
    if stacked and mode == "nn":
        tn = n_blk
    if stacked and mode == "tn":
        tn = kb * n_blk
    if stacked and mode == "nt":
        tk = kb * n_blk
    nk = k // tk
    assert out_dtype == F32 or (nk == 1 and not (stacked and mode == "tn")), "narrow results need a single k step"
    assert not out_relu2 or nk == 1, "relu^2 is applied to a finished tile"

    out_spec = pl.BlockSpec((tm, tn), lambda i, j, kk: (i, j))
    out_shape = jax.ShapeDtypeStruct((m, n), out_dtype)
    if mode == "nn":
        a_spec = pl.BlockSpec((tm, tk), lambda i, j, kk: (i, kk))
        b_spec = pl.BlockSpec((tk, tn), lambda i, j, kk: (kk, j))
        if stacked:
            b_spec = pl.BlockSpec((None, tk, tn), lambda i, j, kk: (j, kk, 0))
    elif mode == "nt":
        a_spec = pl.BlockSpec((tm, tk), lambda i, j, kk: (i, kk))
        b_spec = pl.BlockSpec((tn, tk), lambda i, j, kk: (j, kk))
        if stacked:
            b_spec = pl.BlockSpec((kb, tn, n_blk), lambda i, j, kk: (kk, j, 0))
    else:
        a_spec = pl.BlockSpec((tk, tm), lambda i, j, kk: (kk, i))
        b_spec = pl.BlockSpec((tk, tn), lambda i, j, kk: (kk, j))
        if stacked:
            out_spec = pl.BlockSpec((kb, tm, n_blk), lambda i, j, kk: (j, i, 0))
            out_shape = jax.ShapeDtypeStruct((N_DEV, m, n_blk), F32)
    extra = [] if relu2_bwd_of is None else [relu2_bwd_of]
    extra_specs = [pl.BlockSpec((tm, tn), lambda i, j, kk: (i, j))] * len(extra)

    def body(a_ref, b_ref, *rest):
        def relu2_slope():
            return 2.0 * jnp.sqrt(rest[0][...].astype(F32))

        o_ref = rest[-1]
        kk = pl.program_id(2)
        av = a_ref[...]
        if stacked and mode == "nt":
            part = _dot(av[:, :n_blk], b_ref[0], mode)
            for q in range(1, kb):
                part = part + _dot(av[:, q * n_blk:(q + 1) * n_blk], b_ref[q], mode)
        else:
            part = _dot(av, b_ref[...], mode)
        if stacked and mode == "tn":
            part = jnp.stack([part[:, q * n_blk:(q + 1) * n_blk] for q in range(kb)])
        if nk == 1:
            if out_relu2:
                part = jnp.maximum(part, 0.0)
                part = part * part
            if relu2_bwd_of is not None:
                part = part * relu2_slope()
            o_ref[...] = part.astype(out_dtype)
            return

        @pl.when(kk == 0)
        def _():
            o_ref[...] = part

        @pl.when(kk != 0)
        def _():
            o_ref[...] += part

        if relu2_bwd_of is not None:
            @pl.when(kk == nk - 1)
            def _():
                o_ref[...] *= relu2_slope()

    return pl.pallas_call(
        body,
        name=name,
        grid=(m // tm, n // tn, nk),
        in_specs=[a_spec, b_spec] + extra_specs + [pl.BlockSpec(memory_space=pl.ANY)] * len(after),
        out_specs=out_spec,
        out_shape=out_shape,
        compiler_params=_params("parallel", "parallel", "arbitrary"),
    )(a, b, *extra, *after)


def _make_gate_pair(name):
    def call(body, ins, n_out, reduce_rows, tag):
        x = ins[0]
        m, d = x.shape[0], LANES
        g = x.shape[1] // d
        tm = _pick(m, (1088, 1024, 544, 512, 272, 256, 128, 64, 32, 16, 8))
        rows = pl.BlockSpec((tm, d), lambda h, i: (i, h))
        mats = pl.BlockSpec((1, d, d), lambda h, i: (h, 0, 0))
        return pl.pallas_call(
            body,
            name=name + tag,
            grid=(g, m // tm),
            in_specs=[rows if a.ndim == 2 else mats for a in ins],
            out_specs=[mats if reduce_rows else rows] * n_out,
            out_shape=[jax.ShapeDtypeStruct((g, d, d) if reduce_rows else (m, g * d), F32)] * n_out,
            compiler_params=_params("parallel", "arbitrary" if reduce_rows else "parallel"),
        )(*ins)

    def fwd_body(x_ref, wa_ref, wx_ref, a_ref, b_ref):
        xv = x_ref[...]
        a_ref[...] = _dot(xv, wa_ref[0], "nn")
        b_ref[...] = _dot(xv, wx_ref[0], "nn")

    def dx_body(da_ref, db_ref, wa_ref, wx_ref, dx_ref):
        dx_ref[...] = _dot(da_ref[...], wa_ref[0], "nt") + _dot(db_ref[...], wx_ref[0], "nt")

    def dw_body(x_ref, da_ref, db_ref, dwa_ref, dwx_ref):
        xv = x_ref[...]
        pa, pb = _dot(xv, da_ref[...], "tn"), _dot(xv, db_ref[...], "tn")

        @pl.when(pl.program_id(1) == 0)
        def _():
            dwa_ref[0] = pa
            dwx_ref[0] = pb

        @pl.when(pl.program_id(1) != 0)
        def _():
            dwa_ref[0] += pa
            dwx_ref[0] += pb

    @jax.custom_vjp
    def op(x, wa, wx):
        return tuple(call(fwd_body, (x, wa, wx), 2, False, "_fwd"))

    def fwd(x, wa, wx):
        return op(x, wa, wx), (x, wa, wx)

    def bwd(res, cots):
        x, wa, wx = res
        da, db = cots
        (dx,) = call(dx_body, (da, db, wa, wx), 1, False, "_dx")
        dwa, dwx = call(dw_body, (x, da, db), 2, True, "_dw")
        return dx, dwa, dwx

    op.defvjp(fwd, bwd)
    return op


def _my_place():
    return lax.axis_index("x"), lax.axis_index("y"), lax.axis_index("c")


def _stack_cols(full):
    k, n8 = full.shape
    return full.reshape(k, N_DEV, n8 // N_DEV).transpose(1, 0, 2)


def _unstack_cols(stacked):
    j, k, n = stacked.shape
    return stacked.transpose(1, 0, 2).reshape(k, j * n)


def _split_cols(p, cuts):
    bounds = (0,) + tuple(cuts) + (p.shape[1],)

    @jax.custom_vjp
    def op(z):
        return tuple(z[:, lo:hi] for lo, hi in zip(bounds[:-1], bounds[1:]))

    op.defvjp(lambda z: (op(z), None), lambda _, cots: (jnp.concatenate(cots, axis=1),))
    return op(p)


def _make_slot_linear(name):
    @jax.custom_vjp
    def op(x, w_full, slot):
        return _matmul(x, w_full, "nn", name + "_fwd")

    def fwd(x, w_full, slot):
        return op(x, w_full, slot), (x, w_full)

    def bwd(res, dy):
        x, w = res
        return _matmul(dy, w, "nt", name + "_dx"), jnp.zeros_like(w), _matmul(x, dy, "tn", name + "_dw")

    op.defvjp(fwd, bwd)
    return op


def _pack_rows(gs):
    flat = jnp.concatenate([g.reshape(-1) for g in gs])
    n = flat.shape[0]
    rows = -(-n // (256 * LANES)) * 256
    return jnp.pad(flat, (0, rows * LANES - n)).reshape(rows, LANES)


def _unpack_rows(packed, like):
    flat, out, off = packed.reshape(-1), [], 0
    for g in like:
        out.append(flat[off:off + g.size].reshape(g.shape))
        off += g.size
    return out


_HBM = pl.BlockSpec(memory_space=pltpu.HBM)
_SEM = pl.BlockSpec(memory_space=pltpu.SEMAPHORE)
_SIDE_EFFECT = pltpu.SideEffectType.DATAFLOW_SIDE_EFFECTING
_N_PEERS = N_DEV - 1


def _peer(k):
    x, y, c = _my_place()
    return x ^ ((k >> 2) & 1), y ^ ((k >> 1) & 1), c ^ (k & 1)


def _exchange_start(src, land_shape, gather, name, after=()):
    def body(src_ref, land_ref, *rest):
        send_sems, recv_sems, src_thru, land_thru, token = rest[len(after):]
        x, y, c = _my_place()
        me = 4 * x + 2 * y + c
        for k in range(1, N_DEV):
            px, py, pc = _peer(k)
            pltpu.make_async_remote_copy(
                src_ref=src_ref if gather else src_ref.at[4 * px + 2 * py + pc],
                dst_ref=land_ref.at[me] if gather else land_ref.at[k - 1],
                send_sem=send_sems.at[k - 1],
                recv_sem=recv_sems.at[k - 1],
                device_id=(px, py, pc),
                device_id_type=pl.DeviceIdType.MESH,
            ).start()
        token[...] = jnp.zeros_like(token)

    return pl.pallas_call(
        body,
        name=name,
        out_shape=(
            pltpu.SemaphoreType.DMA((_N_PEERS,)),
            pltpu.SemaphoreType.DMA((_N_PEERS,)),
            pltpu.HBM(src.shape, src.dtype),
            pltpu.HBM(land_shape, src.dtype),
            jax.ShapeDtypeStruct((8, LANES), F32),
        ),
        in_specs=(_HBM, _HBM) + (pl.BlockSpec(memory_space=pl.ANY),) * len(after),
        out_specs=(_SEM, _SEM, _HBM, _HBM, pl.BlockSpec(memory_space=pltpu.VMEM)),
        input_output_aliases={0: 2, 1: 3},
        compiler_params=pltpu.CompilerParams(has_side_effects=_SIDE_EFFECT),
    )(pltpu.with_memory_space_constraint(src, pltpu.HBM),
      pltpu.with_memory_space_constraint(lax.empty(land_shape, src.dtype), pltpu.HBM), *after)


def _gather_start_all(shards, name):
    n = len(shards)

    def body(*refs):
        srcs, lands = refs[:n], refs[n:2 * n]
        outs = refs[2 * n:]
        send_sems, recv_sems, token = outs[:n], outs[n:2 * n], outs[-1]
        x, y, c = _my_place()
        me = 4 * x + 2 * y + c
        for i in range(n):
            for k in range(1, N_DEV):
                pltpu.make_async_remote_copy(
                    src_ref=srcs[i],
                    dst_ref=lands[i].at[me],
                    send_sem=send_sems[i].at[k - 1],
                    recv_sem=recv_sems[i].at[k - 1],
                    device_id=_peer(k),
                    device_id_type=pl.DeviceIdType.MESH,
                ).start()
        token[...] = jnp.zeros_like(token)

    lands = [(N_DEV,) + s.shape for s in shards]
    sems = tuple(pltpu.SemaphoreType.DMA((_N_PEERS,)) for _ in range(2 * n))
    res = pl.pallas_call(
        body,
        name=name,
        out_shape=sems + tuple(pltpu.HBM(s.shape, s.dtype) for s in shards)
        + tuple(pltpu.HBM(ls, s.dtype) for ls, s in zip(lands, shards)) + (jax.ShapeDtypeStruct((8, LANES), F32),),
        in_specs=(_HBM,) * (2 * n),
        out_specs=(_SEM,) * (2 * n) + (_HBM,) * (2 * n) + (pl.BlockSpec(memory_space=pltpu.VMEM),),
        input_output_aliases={i: 2 * n + i for i in range(2 * n)},
        compiler_params=pltpu.CompilerParams(has_side_effects=_SIDE_EFFECT),
    )(*[pltpu.with_memory_space_constraint(s, pltpu.HBM) for s in shards],
      *[pltpu.with_memory_space_constraint(lax.empty(ls, s.dtype), pltpu.HBM) for ls, s in zip(lands, shards)])
    return [(res[i], res[n + i], res[2 * n + i], res[3 * n + i], res[-1]) for i in range(n)]


def _exchange_wait(handle, gather, after, name):
    send_sems, recv_sems, src_thru, land_thru, _ = handle

    def body(src_ref, land_ref, send_sems, recv_sems, after_ref, src_dead, got_ref):
        for k in range(1, N_DEV):
            cp = pltpu.make_async_remote_copy(
                src_ref=src_ref if gather else src_ref.at[k],
                dst_ref=land_ref.at[k - 1],
                send_sem=send_sems.at[k - 1],
                recv_sem=recv_sems.at[k - 1],
                device_id=_peer(k),
                device_id_type=pl.DeviceIdType.MESH,
            )
            cp.wait_send()
            cp.wait_recv()

    return pl.pallas_call(
        body,
        name=name,
        out_shape=(pltpu.HBM(src_thru.shape, src_thru.dtype), pltpu.HBM(land_thru.shape, land_thru.dtype)),
        in_specs=(_HBM, _HBM, _SEM, _SEM, pl.BlockSpec(memory_space=pl.ANY)),
        out_specs=(_HBM, _HBM),
        input_output_aliases={0: 0, 1: 1},
        compiler_params=pltpu.CompilerParams(has_side_effects=_SIDE_EFFECT),
    )(src_thru, land_thru, send_sems, recv_sems, after)[1]


def _sum_own_and_peers(own, land, name):
    r, c = own.shape
    tr = _pick(r, (256, 128, 64, 32, 16, 8))

    def body(o_ref, l_ref, out_ref):
        s = [l_ref[j] for j in range(_N_PEERS)]
        out_ref[...] = ((o_ref[...] + s[0]) + (s[1] + s[2])) + ((s[3] + s[4]) + (s[5] + s[6]))

    return pl.pallas_call(
        body,
        name=name,
        grid=(r // tr,),
        in_specs=[pl.BlockSpec((tr, c), lambda i: (i, 0)), pl.BlockSpec((_N_PEERS, tr, c), lambda i: (0, i, 0))],
        out_specs=pl.BlockSpec((tr, c), lambda i: (i, 0)),
        out_shape=jax.ShapeDtypeStruct((r, c), own.dtype),
        compiler_params=_params("parallel"),
    )(own, land)


class _Cols:
    def __init__(self, array, width, block):
        self.array, self.width, self.block = array, width, block
        self.shape, self.dtype = (array.shape[0], width), array.dtype


def _base(a):
    return a.array if isinstance(a, _Cols) else a


def _col_block(a):
    return a.block if isinstance(a, _Cols) else 0


def _make_rowwise(f, name, n_rows, n_tabs, n_pars):
    n_in = n_rows + n_tabs + n_pars

    def specs(args, tm):
        blocked = [pl.BlockSpec((tm, a.shape[1]), lambda i, blk=_col_block(a): (i, blk)) for a in args[: n_rows + n_tabs]]
        whole = [pl.BlockSpec(a.shape, lambda i: (0, 0)) for a in args[n_rows + n_tabs:]]
        return blocked + whole

    def out_struct(args, tm):
        blk = [jax.ShapeDtypeStruct((tm, a.shape[1]), a.dtype) for a in args[: n_rows + n_tabs]]
        blk += [jax.ShapeDtypeStruct(a.shape, a.dtype) for a in args[n_rows + n_tabs:]]
        return jax.eval_shape(f, *blk)

    def fwd_call(*args):
        r = args[0].shape[0]
        tm = _row_tile(r, max(a.shape[1] for a in args[:n_rows]))
        ro, so = out_struct(args, tm)

        def body(*refs):
            vals = [x[...] for x in refs[:n_in]]
            outs = refs[n_in:]
            rv, sv = f(*vals)
            for o, v in zip(outs[: len(ro)], rv):
                o[...] = v
            for o, v in zip(outs[len(ro):], sv):
                @pl.when(pl.program_id(0) == 0)
                def _(o=o, v=v):
                    o[...] = v

                @pl.when(pl.program_id(0) != 0)
                def _(o=o, v=v):
                    o[...] += v

        out_shape = [jax.ShapeDtypeStruct((r, s.shape[1]), s.dtype) for s in ro]
        out_shape += [jax.ShapeDtypeStruct(s.shape, s.dtype) for s in so]
        out_specs = [pl.BlockSpec((tm, s.shape[1]), lambda i: (i, 0)) for s in ro]
        out_specs += [pl.BlockSpec(s.shape, lambda i: (0, 0)) for s in so]
        res = pl.pallas_call(
            body,
            name=name + "_fwd",
            grid=(r // tm,),
            in_specs=specs(args, tm),
            out_specs=out_specs,
            out_shape=out_shape,
            compiler_params=_params("arbitrary" if so else "parallel"),
        )(*[_base(a) for a in args])
        return tuple(res[: len(ro)]), tuple(res[len(ro):])

    def bwd_call(args, cots, more=(), row_dtypes=None):
        r = args[0].shape[0]
        tm = _row_tile(r, max(a.shape[1] for a in args[:n_rows]))
        ro, so = out_struct(args, tm)
        crow, csum = cots
        rows, tabs, pars = args[:n_rows], args[n_rows:n_rows + n_tabs], args[n_rows + n_tabs:]
        n_c = len(crow) + len(csum)

        def body(*refs):
            vals = [x[...] for x in refs[:n_in]]
            cv = [x[...] for x in refs[n_in:n_in + n_c]]
            for x in refs[n_in + n_c:n_in + n_c + len(more)]:
                cv[0] = cv[0] + x[...]
            outs = refs[n_in + n_c + len(more):]
            tv = vals[n_rows:n_rows + n_tabs]

            def g(*dargs):
                return f(*dargs[:n_rows], *tv, *dargs[n_rows:])

            _, vjp = jax.vjp(g, *vals[:n_rows], *vals[n_rows + n_tabs:])
            d = vjp((tuple(cv[: len(crow)]), tuple(cv[len(crow):])))
            for o, v in zip(outs[:n_rows], d[:n_rows]):
                o[...] = v.astype(o.dtype)
            for o, v in zip(outs[n_rows:], d[n_rows:]):
                @pl.when(pl.program_id(0) == 0)
                def _(o=o, v=v):
                    o[...] = v

                @pl.when(pl.program_id(0) != 0)
                def _(o=o, v=v):
                    o[...] += v

        in_specs = specs(args, tm)
        in_specs += [pl.BlockSpec((tm, c.shape[1]), lambda i: (i, 0)) for c in crow]
        in_specs += [pl.BlockSpec(c.shape, lambda i: (0, 0)) for c in csum]
        in_specs += [pl.BlockSpec((tm, c.shape[1]), lambda i: (i, 0)) for c in more]
        out_shape = [jax.ShapeDtypeStruct(a.shape, dt) for a, dt in zip(rows, row_dtypes or [a.dtype for a in rows])]
        out_shape += [jax.ShapeDtypeStruct(a.shape, a.dtype) for a in pars]
        out_specs = [pl.BlockSpec((tm, a.shape[1]), lambda i: (i, 0)) for a in rows]
        out_specs += [pl.BlockSpec(a.shape, lambda i: (0, 0)) for a in pars]
        res = pl.pallas_call(
            body,
            name=name + "_bwd",
            grid=(r // tm,),
            in_specs=in_specs,
            out_specs=out_specs,
            out_shape=out_shape,
            compiler_params=_params("arbitrary" if pars else "parallel"),
        )(*[_base(a) for a in args], *crow, *csum, *more)
        return tuple(res[:n_rows]), tuple(res[n_rows:])

    @jax.custom_vjp
    def op(rows, tabs, pars):
        return fwd_call(*rows, *tabs, *pars)

    op.fwd_call, op.bwd_call = fwd_call, bwd_call

    def fwd(rows, tabs, pars):
        return fwd_call(*rows, *tabs, *pars), (rows, tabs, pars)

    def bwd(res, cots):
        rows, tabs, pars = res
        drows, dpars = bwd_call(tuple(rows) + tuple(tabs) + tuple(pars), cots)
        return drows, tuple(jnp.zeros_like(t) for t in tabs), dpars

    op.defvjp(fwd, bwd)
    return op


def _sigmoid(x):
    return 0.5 * (jnp.tanh(0.5 * x) + 1.0)


@jax.custom_jvp
def _softplus(x):
    e = jnp.exp(-jnp.abs(x))
    u = 1.0 + e
    log1p_e = jnp.where(u == 1.0, e, e * jnp.log(u) / jnp.where(u == 1.0, 1.0, u - 1.0))
    return jnp.maximum(x, 0.0) + log1p_e


@_softplus.defjvp
def _softplus_jvp(primals, tangents):
    (x,), (t,) = primals, tangents
    return _softplus(x), t * _sigmoid(x)


def _gelu(x):
    return 0.5 * x * (1.0 + jnp.tanh(math.sqrt(2.0 / math.pi) * (x + 0.044715 * (x * x * x))))


def _ln_res_f(h, mix, g, b):
    z = DN_ALPHA * h + mix
    mu = jnp.mean(z, axis=-1, keepdims=True)
    zc = z - mu
    var = jnp.mean(zc * zc, axis=-1, keepdims=True)
    return (zc * lax.rsqrt(var + EPS) * g + b,), ()


def _ln_res_copy_f(h, mix, g, b):
    (out,), _ = _ln_res_f(h, mix, g, b)
    return (out, out.astype(BF16)), ()


def _rmsnorm_f(x, g):
    return (x * lax.rsqrt(jnp.mean(x * x, axis=-1, keepdims=True) + EPS) * g,), ()


def _lru_gates_f(ga, gx, xc, b_a, b_x, lam):
    r = _sigmoid(ga + b_a)
    i = _sigmoid(gx + b_x)
    log_a = -LRU_C * r * _softplus(-lam)
    a = jnp.exp(log_a)
    one_minus_a2 = jnp.tanh(-log_a) * (jnp.exp(2.0 * log_a) + 1.0)
    return (a, jnp.sqrt(one_minus_a2) * (i * xc)), ()


def _lru_out_f(hh, p_gate):
    return (hh * _gelu(p_gate),), ()


def _rope_ret_f(q, k, cos2, sin2):
    d = cos2.shape[1]
    half = d // 2
    k_scale = d ** -0.5

    def rope(x):
        outs = []
        for h in range(x.shape[1] // d):
            xh = x[:, h * d:(h + 1) * d]
            rot = jnp.concatenate([xh[:, half:], xh[:, :half]], axis=1)
            outs.append(xh * cos2 + rot * sin2)
        return jnp.concatenate(outs, axis=1)

    return (rope(q), rope(k) * k_scale), ()


def _ret_out_f(o, g):
    d = o.shape[1] // RET_HEADS
    outs = []
    for h in range(RET_HEADS):
        oh = o[:, h * d:(h + 1) * d]
        outs.append(oh * lax.rsqrt(jnp.mean(oh * oh, axis=-1, keepdims=True) + EPS))
    y = jnp.concatenate(outs, axis=1)
    return (g * _sigmoid(g) * y,), ()


def _ret_out_bf16_f(o, g):
    (y,), _ = _ret_out_f(o, g)
    return (y.astype(BF16),), ()


def _loss_f(y, t, mask):
    e = (y - t) * mask
    per_row = jnp.sum(e * e, axis=-1, keepdims=True) * (0.5 / y.shape[1])
    total = jnp.sum(per_row, axis=0, keepdims=True)
    return (), (jnp.broadcast_to(total, (1, LANES)),)


def _shift_down(x, s):
    if s == 0:
        return x
    t = x.shape[0]
    row = lax.broadcasted_iota(jnp.int32, x.shape, 0)
    return jnp.where(row >= s, pltpu.roll(x, s, 0), 0.0)


def _shift_up(x, s):
    if s == 0:
        return x
    t = x.shape[0]
    row = lax.broadcasted_iota(jnp.int32, x.shape, 0)
    return jnp.where(row < t - s, pltpu.roll(x, t - s, 0), 0.0)


def _conv_fwd(x, w, b, name):
    bsz, t, c = x.shape
    width = w.shape[0]

    def body(x_ref, w_ref, b_ref, y_ref):
        xv = x_ref[0]
        acc = jnp.broadcast_to(b_ref[...], xv.shape)
        for k in range(width):
            acc = acc + w_ref[k:k + 1, :] * _shift_down(xv, width - 1 - k)
        y_ref[0] = acc

    return pl.pallas_call(
        body,
        name=name,
        grid=(bsz, c // LANES),
        in_specs=[
            pl.BlockSpec((1, t, LANES), lambda i, j: (i, 0, j)),
            pl.BlockSpec((width, LANES), lambda i, j: (0, j)),
            pl.BlockSpec((1, LANES), lambda i, j: (0, j)),
        ],
        out_specs=pl.BlockSpec((1, t, LANES), lambda i, j: (i, 0, j)),
        out_shape=jax.ShapeDtypeStruct(x.shape, F32),
        compiler_params=_params("parallel", "parallel"),
    )(x, w, b)


def _conv_bwd(x, w, dy, name):
    bsz, t, c = x.shape
    width = w.shape[0]

    def body(x_ref, w_ref, dy_ref, dx_ref, dw_ref, db_ref):
        xv, g = x_ref[0], dy_ref[0]
        dx = jnp.zeros_like(xv)
        dws = []
        for k in range(width):
            s = width - 1 - k
            dx = dx + w_ref[k:k + 1, :] * _shift_up(g, s)
            dws.append(jnp.sum(g * _shift_down(xv, s), axis=0, keepdims=True))
        dx_ref[0] = dx
        dw = jnp.concatenate(dws, axis=0)
        db = jnp.sum(g, axis=0, keepdims=True)

        @pl.when(pl.program_id(1) == 0)
        def _():
            dw_ref[...] = dw
            db_ref[...] = db

        @pl.when(pl.program_id(1) != 0)
        def _():
            dw_ref[...] += dw
            db_ref[...] += db

    return pl.pallas_call(
        body,
        name=name,
        grid=(c // LANES, bsz),
        in_specs=[
            pl.BlockSpec((1, t, LANES), lambda j, i: (i, 0, j)),
            pl.BlockSpec((width, LANES), lambda j, i: (0, j)),
            pl.BlockSpec((1, t, LANES), lambda j, i: (i, 0, j)),
        ],
        out_specs=[
            pl.BlockSpec((1, t, LANES), lambda j, i: (i, 0, j)),
            pl.BlockSpec((width, LANES), lambda j, i: (0, j)),
            pl.BlockSpec((1, LANES), lambda j, i: (0, j)),
        ],
        out_shape=[
            jax.ShapeDtypeStruct(x.shape, F32),
            jax.ShapeDtypeStruct(w.shape, F32),
            jax.ShapeDtypeStruct((1, c), F32),
        ],
        compiler_params=_params("parallel", "arbitrary"),
    )(x, w, dy)


def _make_conv(name):
    @jax.custom_vjp
    def op(x, w, b):
        return _conv_fwd(x, w, b, name + "_fwd")

    def fwd(x, w, b):
        return op(x, w, b), (x, w)

    def bwd(res, dy):
        x, w = res
        return tuple(_conv_bwd(x, w, dy, name + "_bwd"))

    op.defvjp(fwd, bwd)
    return op


_SCAN_ROWS = 8


def _scan_fwd(a, b, name):
    bsz, t, c = a.shape
    cw = _pick(c, (4 * LANES, 2 * LANES, LANES))

    def body(a_ref, b_ref, h_ref):
        row = lax.broadcasted_iota(jnp.int32, (_SCAN_ROWS, cw), 0)

        def step(i, carry):
            r0 = pl.multiple_of(i * _SCAN_ROWS, _SCAN_ROWS)
            av, bv = a_ref[0, pl.ds(r0, _SCAN_ROWS), :], b_ref[0, pl.ds(r0, _SCAN_ROWS), :]
            for s in (1, 2, 4):
                a_sh = jnp.where(row >= s, pltpu.roll(av, s, 0), 1.0)
                b_sh = jnp.where(row >= s, pltpu.roll(bv, s, 0), 0.0)
                bv = av * b_sh + bv
                av = av * a_sh
            hv = bv + av * carry
            h_ref[0, pl.ds(r0, _SCAN_ROWS), :] = hv
            return hv[_SCAN_ROWS - 1:, :]

        lax.fori_loop(0, t // _SCAN_ROWS, step, jnp.zeros((1, cw), F32), unroll=2)

    spec = pl.BlockSpec((1, t, cw), lambda i, j: (i, 0, j))
    return pl.pallas_call(
        body,
        name=name,
        grid=(bsz, c // cw),
        in_specs=[spec, spec],
        out_specs=spec,
        out_shape=jax.ShapeDtypeStruct(a.shape, F32),
        compiler_params=_params("parallel", "parallel"),
    )(a, b)


def _scan_bwd(a, h, g, name):
    bsz, t, c = a.shape
    cw = _pick(c, (2 * LANES, LANES))

    def body(a_ref, h_ref, g_ref, da_ref, db_ref):
        rows = _SCAN_ROWS
        row = lax.broadcasted_iota(jnp.int32, (rows, cw), 0)
        n_tiles = t // rows

        def step(n, carry):
            lam_next, a_next = carry
            i = n_tiles - 1 - n
            r0 = pl.multiple_of(i * rows, rows)
            rp = pl.multiple_of(jnp.maximum(i - 1, 0) * rows, rows)
            av, gv, hv = a_ref[0, pl.ds(r0, rows), :], g_ref[0, pl.ds(r0, rows), :], h_ref[0, pl.ds(r0, rows), :]
            h_before = jnp.where(i > 0, h_ref[0, pl.ds(rp, rows), :][rows - 1:, :], 0.0)
            cv = jnp.where(row < rows - 1, pltpu.roll(av, rows - 1, 0), a_next)
            for s in (1, 2, 4):
                c_sh = jnp.where(row < rows - s, pltpu.roll(cv, rows - s, 0), 1.0)
                g_sh = jnp.where(row < rows - s, pltpu.roll(gv, rows - s, 0), 0.0)
                gv = cv * g_sh + gv
                cv = cv * c_sh
            lam = gv + cv * lam_next
            db_ref[0, pl.ds(r0, rows), :] = lam
            da_ref[0, pl.ds(r0, rows), :] = lam * jnp.where(row >= 1, pltpu.roll(hv, 1, 0), h_before)
            return lam[:1, :], av[:1, :]

        zero = jnp.zeros((1, cw), F32)
        lax.fori_loop(0, n_tiles, step, (zero, zero), unroll=2)

    spec = pl.BlockSpec((1, t, cw), lambda i, j: (i, 0, j))
    return pl.pallas_call(
        body,
        name=name,
        grid=(bsz, c // cw),
        in_specs=[spec, spec, spec],
        out_specs=[spec, spec],
        out_shape=[jax.ShapeDtypeStruct(a.shape, F32)] * 2,
        compiler_params=_params("parallel", "parallel"),
    )(a, h, g)


def _make_scan(name):
    @jax.custom_vjp
    def op(a, b):
        return _scan_fwd(a, b, name + "_fwd")

    def fwd(a, b):
        h = op(a, b)
        return h, (a, h)

    def bwd(res, g):
        a, h = res
        da, db = _scan_bwd(a, h, g, name + "_bwd")
        return da, db

    op.defvjp(fwd, bwd)
    return op


def _query_blocks(t):
    blocks, start = [], 0
    while start < t:
        rows = 2 * SEQ_BLOCK if start + 2 * SEQ_BLOCK <= t else SEQ_BLOCK
        blocks.append((start, rows))
        start += rows
    return blocks


def _attn_exp(q, k, start, scale):
    tq, tk = q.shape[0], k.shape[0]
    s = _dot(q, k, "nt") * scale
    qpos = start + lax.broadcasted_iota(jnp.int32, (tq, tk), 0)
    kpos = lax.broadcasted_iota(jnp.int32, (tq, tk), 1)
    s = jnp.where(kpos <= qpos, s, NEG_INF)
    e = jnp.exp(s - jnp.max(s, axis=-1, keepdims=True))
    return e, 1.0 / jnp.sum(e, axis=-1, keepdims=True)


_MLA_SCALE = (MLA_NOPE + MLA_ROPE) ** -0.5


def _attn_specs(t):
    head = pl.BlockSpec((1, t, LANES), lambda b, h: (b, 0, h))
    shared = pl.BlockSpec((1, t, LANES), lambda b, h: (b, 0, 0))
    return head, shared


def _attn_fwd(q, kv, kpe, name):
    bsz, t, hl = q.shape
    head, shared = _attn_specs(t)

    def body(q_ref, kv_ref, kpe_ref, o_ref, k_s, v_s):
        lane = lax.broadcasted_iota(jnp.int32, (t, LANES), 1)
        kvh = kv_ref[0]
        k_s[...] = jnp.where(lane < MLA_NOPE, kvh, kpe_ref[0]).astype(BF16)
        v_s[...] = kvh.astype(BF16)
        for start, rows in _query_blocks(t):
            n = start + rows
            e, inv_l = _attn_exp(q_ref[0, start:n, :], k_s[:n, :], start, _MLA_SCALE)
            o_ref[0, start:n, :] = _dot(e, v_s[:n, :], "nn") * inv_l

    return pl.pallas_call(
        body,
        name=name,
        grid=(bsz, hl // LANES),
        in_specs=[head, head, shared],
        out_specs=head,
        out_shape=jax.ShapeDtypeStruct(q.shape, F32),
        scratch_shapes=[pltpu.VMEM((t, LANES), BF16), pltpu.VMEM((t, LANES), BF16)],
        compiler_params=_params("parallel", "parallel"),
    )(q, kv, kpe)


def _attn_bwd(q, kv, kpe, do, name):
    bsz, t, hl = q.shape
    head, shared = _attn_specs(t)

    def body(q_ref, kv_ref, kpe_ref, do_ref, dq_ref, dkv_ref, dkpe_ref, k_s, v_s, dk_s, dv_s):
        lane = lax.broadcasted_iota(jnp.int32, (t, LANES), 1)
        kvh = kv_ref[0]
        k_s[...] = jnp.where(lane < MLA_NOPE, kvh, kpe_ref[0]).astype(BF16)
        v_s[...] = kvh.astype(BF16)
        for start, rows in reversed(_query_blocks(t)):
            n = start + rows
            qb = q_ref[0, start:n, :]
            dob = jnp.where(lane[:rows] >= MLA_NOPE, do_ref[0, start:n, :], 0.0)
            kk, vv = k_s[:n, :], v_s[:n, :]
            e, inv_l = _attn_exp(qb, kk, start, _MLA_SCALE)
            p = e * inv_l
            dp = _dot(dob, vv, "nt")
            ds = p * (dp - jnp.sum(dp * p, axis=-1, keepdims=True)) * _MLA_SCALE
            dq_ref[0, start:n, :] = _dot(ds, kk, "nn")
            if n == t:
                dk_s[...] = _dot(ds, qb, "tn")
                dv_s[...] = _dot(p, dob, "tn")
            else:
                dk_s[:n, :] += _dot(ds, qb, "tn")
                dv_s[:n, :] += _dot(p, dob, "tn")
        dk = dk_s[...]
        dkv_ref[0] = jnp.where(lane < MLA_NOPE, dk, dv_s[...])
        dkpe = jnp.where(lane >= MLA_NOPE, dk, 0.0)

        @pl.when(pl.program_id(1) == 0)
        def _():
            dkpe_ref[0] = dkpe

        @pl.when(pl.program_id(1) != 0)
        def _():
            dkpe_ref[0] += dkpe

    return pl.pallas_call(
        body,
        name=name,
        grid=(bsz, hl // LANES),
        in_specs=[head, head, shared, head],
        out_specs=[head, head, shared],
        out_shape=[
            jax.ShapeDtypeStruct(q.shape, F32),
            jax.ShapeDtypeStruct(kv.shape, F32),
            jax.ShapeDtypeStruct(kpe.shape, F32),
        ],
        scratch_shapes=[pltpu.VMEM((t, LANES), BF16), pltpu.VMEM((t, LANES), BF16),
                        pltpu.VMEM((t, LANES), F32), pltpu.VMEM((t, LANES), F32)],
        compiler_params=_params("parallel", "arbitrary"),
    )(q, kv, kpe, do)


def _make_attention(name):
    @jax.custom_vjp
    def op(q, kv, kpe):
        return _attn_fwd(q, kv, kpe, name + "_fwd")

    def fwd(q, kv, kpe):
        return op(q, kv, kpe), (q, kv, kpe)

    def bwd(res, do):
        return tuple(_attn_bwd(*res, do, name + "_bwd"))

    op.defvjp(fwd, bwd)
    return op


_ROPE_SHIFT = MLA_ROPE // 2


def _rope_lanes_call(x, c, sm, sp, transpose, name):
    r, width = x.shape
    tm = _row_tile(r, width)

    def body(x_ref, c_ref, sm_ref, sp_ref, y_ref):
        cv, smv, spv = c_ref[...], sm_ref[...], sp_ref[...]
        for b in range(width // LANES):
            xb = x_ref[:, b * LANES:(b + 1) * LANES]
            if transpose:
                yb = xb * cv + pltpu.roll(xb * smv, _ROPE_SHIFT, 1) + pltpu.roll(xb * spv, LANES - _ROPE_SHIFT, 1)
            else:
                yb = xb * cv + pltpu.roll(xb, LANES - _ROPE_SHIFT, 1) * smv + pltpu.roll(xb, _ROPE_SHIFT, 1) * spv
            y_ref[:, b * LANES:(b + 1) * LANES] = yb

    tab = pl.BlockSpec((tm, LANES), lambda i: (i, 0))
    blk = pl.BlockSpec((tm, width), lambda i: (i, 0))
    return pl.pallas_call(
        body,
        name=name,
        grid=(r // tm,),
        in_specs=[blk, tab, tab, tab],
        out_specs=blk,
        out_shape=jax.ShapeDtypeStruct(x.shape, F32),
        compiler_params=_params("parallel"),
    )(x, c, sm, sp)


def _make_rope_lanes(name):
    @jax.custom_vjp
    def op(x, c, sm, sp):
        return _rope_lanes_call(x, c, sm, sp, False, name + "_fwd")

    def fwd(x, c, sm, sp):
        return op(x, c, sm, sp), (c, sm, sp)

    def bwd(res, dy):
        c, sm, sp = res
        return _rope_lanes_call(dy, c, sm, sp, True, name + "_bwd"), jnp.zeros_like(c), jnp.zeros_like(sm), jnp.zeros_like(sp)

    op.defvjp(fwd, bwd)
    return op


def _ret_chunk_rows(t):
    return t // 4 if t % 32 == 0 else SEQ_BLOCK


def _ret_decays(c, log_gamma):
    row = lax.broadcasted_iota(jnp.int32, (c, 1), 0)
    col = lax.broadcasted_iota(jnp.int32, (1, c), 1)
    rowf = row.astype(F32)
    d = jnp.where(row >= col, jnp.exp(log_gamma * rowf) * jnp.exp(-log_gamma * col.astype(F32)), 0.0)
    return d, jnp.exp(log_gamma * (rowf + 1.0)), jnp.exp(log_gamma * (c - 1.0 - rowf)), jnp.exp(log_gamma * c)


def _ret_specs(c, dk, dv, v_block0, n_chunks, reverse):
    pos = (lambda i: n_chunks - 1 - i) if reverse else (lambda i: i)
    return (
        pl.BlockSpec(memory_space=pltpu.SMEM),
        pl.BlockSpec((1, c, dk), lambda b, h, i: (b, pos(i), h)),
        pl.BlockSpec((1, c, dv), lambda b, h, i: (b, pos(i), h + v_block0)),
        pl.BlockSpec((1, c, dv), lambda b, h, i: (b, pos(i), h)),
        pl.BlockSpec((1, 1, dk, dv), lambda b, h, i: (b, h * n_chunks + pos(i), 0, 0)),
    )


def _ret_fwd(lg, q, k, v, name, dv=None, v_block0=0):
    bsz, t, hdk = q.shape
    heads = lg.shape[0]
    dk, dv = hdk // heads, dv or v.shape[2] // heads
    c = _ret_chunk_rows(t)
    n_chunks = t // c
    lg_spec, qk_spec, v_spec, o_spec, s_spec = _ret_specs(c, dk, dv, v_block0, n_chunks, False)

    def body(lg_ref, q_ref, k_ref, v_ref, o_ref, s_ref, state):
        @pl.when(pl.program_id(2) == 0)
        def _():
            state[...] = jnp.zeros((dk, dv), F32)

        d, a, b, g = _ret_decays(c, lg_ref[pl.program_id(1)])
        qb, kb, vb, s_in = q_ref[0], k_ref[0], v_ref[0], state[...]
        s_ref[0, 0] = s_in
        o_ref[0] = _dot(_dot(qb, kb, "nt") * d, vb, "nn") + a * _dot(qb, s_in, "nn")
        state[...] = g * s_in + _dot(kb * b, vb, "tn")

    return pl.pallas_call(
        body,
        name=name,
        grid=(bsz, heads, n_chunks),
        in_specs=[lg_spec, qk_spec, qk_spec, v_spec],
        out_specs=[o_spec, s_spec],
        out_shape=[jax.ShapeDtypeStruct((bsz, t, heads * dv), F32),
                   jax.ShapeDtypeStruct((bsz, heads * n_chunks, dk, dv), F32)],
        scratch_shapes=[pltpu.VMEM((dk, dv), F32)],
        compiler_params=_params("parallel", "parallel", "arbitrary"),
    )(lg, q, k, v)


def _ret_bwd(lg, q, k, v, states, do, name, v_block0=0, dv_dtype=F32):
    bsz, t, hdk = q.shape
    heads = lg.shape[0]
    dk, dv = hdk // heads, do.shape[2] // heads
    c = _ret_chunk_rows(t)
    n_chunks = t // c
    lg_spec, qk_spec, v_spec, o_spec, s_spec = _ret_specs(c, dk, dv, v_block0, n_chunks, True)

    def body(lg_ref, q_ref, k_ref, v_ref, s_ref, do_ref, dq_ref, dk_ref, dv_ref, dstate):
        @pl.when(pl.program_id(2) == 0)
        def _():
            dstate[...] = jnp.zeros((dk, dv), F32)

        d, a, b, g = _ret_decays(c, lg_ref[pl.program_id(1)])
        qb, kb, vb, dob, s_in, ds_out = q_ref[0], k_ref[0], v_ref[0], do_ref[0], s_ref[0, 0], dstate[...]
        scores = _dot(qb, kb, "nt") * d
        dscores = _dot(dob, vb, "nt") * d
        dq_ref[0] = _dot(dscores, kb, "nn") + a * _dot(dob, s_in, "nt")
        dk_ref[0] = _dot(dscores, qb, "tn") + b * _dot(vb, ds_out, "nt")
        dv_ref[0] = (_dot(scores, dob, "tn") + _dot(kb * b, ds_out, "nn")).astype(dv_dtype)
        dstate[...] = g * ds_out + _dot(qb, a * dob, "tn")

    return pl.pallas_call(
        body,
        name=name,
        grid=(bsz, heads, n_chunks),
        in_specs=[lg_spec, qk_spec, qk_spec, v_spec, s_spec, o_spec],
        out_specs=[qk_spec, qk_spec, o_spec],
        out_shape=[
            jax.ShapeDtypeStruct(q.shape, F32),
            jax.ShapeDtypeStruct(k.shape, F32),
            jax.ShapeDtypeStruct(do.shape, dv_dtype),
        ],
        scratch_shapes=[pltpu.VMEM((dk, dv), F32)],
        compiler_params=_params("parallel", "parallel", "arbitrary"),
    )(lg, q, k, v, states, do)


def _adamw(w, g, m, v, name):
    r, c = w.shape
    tr = _pick(r, (256, 128, 64, 32, 16, 8))

    def body(w_ref, g_ref, m_ref, v_ref, d_ref, nm_ref, nv_ref):
        gv = g_ref[...]
        nm = ADAM_B1 * m_ref[...] + (1.0 - ADAM_B1) * gv
        nv = ADAM_B2 * v_ref[...] + (1.0 - ADAM_B2) * (gv * gv)
        m_hat = nm / (1.0 - ADAM_B1 ** ADAM_STEP)
        v_hat = nv / (1.0 - ADAM_B2 ** ADAM_STEP)
        d_ref[...] = -ADAM_LR * (m_hat / (jnp.sqrt(v_hat) + ADAM_EPS) + ADAM_WD * w_ref[...])
        nm_ref[...] = nm
        nv_ref[...] = nv

    spec = pl.BlockSpec((tr, c), lambda i: (i, 0))
    return pl.pallas_call(
        body,
        name=name,
        grid=(r // tr,),
        in_specs=[spec] * 4,
        out_specs=[spec] * 3,
        out_shape=[jax.ShapeDtypeStruct((r, c), F32)] * 3,
        compiler_params=_params("parallel"),
    )(w, g, m, v)


def _rope_tables(t, half, reps):
    inv = ROPE_BASE ** (-jnp.arange(half, dtype=F32) / half)
    ang = jnp.arange(t, dtype=jnp.int32).astype(F32)[:, None] * inv[None, :]
    return jnp.tile(jnp.cos(ang), (1, reps)), jnp.tile(jnp.sin(ang), (1, reps))


def _padded_len(seq):
    return -(-(N_META + seq) // SEQ_BLOCK) * SEQ_BLOCK


def _embed(meta, x):
    bsz, seq, d = x.shape
    t = _padded_len(seq)
    return jnp.concatenate(
        [jnp.broadcast_to(meta[None], (bsz, N_META, d)), x, jnp.zeros((bsz, t - N_META - seq, d), F32)], axis=1
    ).reshape(bsz * t, d)


def _even_mixer(p, conv_w, conv_b, w_rg_a, b_rg_a, w_rg_x, b_rg_x, lru_lambda, q_norm_g, uq_slot, kv_norm_g,
                ukv_slot, gathered, bsz):
    w_uq_pad, w_ukv_full = gathered
    r = p.shape[0]
    t = r // bsz

    def tile_rows(tab):
        return jnp.tile(tab, (bsz, 1))

    lru_w = w_rg_a.shape[2] * w_rg_a.shape[1]
    q_rank, kv_rank = q_norm_g.shape[1], kv_norm_g.shape[1]
    p_gate, p_rec, p_q, p_kv, p_kpe = _split_cols(
        p, (lru_w, 2 * lru_w, 2 * lru_w + q_rank, 2 * lru_w + q_rank + kv_rank))

    xc = _make_conv("conv")(p_rec.reshape(bsz, t, lru_w), conv_w, conv_b).reshape(r, lru_w)
    ga, gx = _make_gate_pair("rg")(xc, w_rg_a[0], w_rg_x[0])
    (a, bb), _ = _make_rowwise(_lru_gates_f, "lru_gates", 3, 0, 3)((ga, gx, xc), (), (b_rg_a, b_rg_x, lru_lambda))
    hh = _make_scan("lru_scan")(a.reshape(bsz, t, lru_w), bb.reshape(bsz, t, lru_w)).reshape(r, lru_w)
    (y_rec,), _ = _make_rowwise(_lru_out_f, "lru_out", 2, 0, 0)((hh, p_gate), (), ())

    (qn,), _ = _make_rowwise(_rmsnorm_f, "q_norm", 1, 0, 1)((p_q,), (), (q_norm_g,))
    (kvn,), _ = _make_rowwise(_rmsnorm_f, "kv_norm", 1, 0, 1)((p_kv,), (), (kv_norm_g,))
    q = _make_slot_linear("ev_uq")(qn, w_uq_pad, uq_slot)
    kv = _make_slot_linear("ev_ukv")(kvn, w_ukv_full, ukv_slot)
    half = MLA_ROPE // 2
    cos, sin = _rope_tables(t, half, 1)
    one, zero = jnp.ones((t, MLA_NOPE), F32), jnp.zeros((t, MLA_NOPE), F32)
    tail = LANES - MLA_NOPE - MLA_ROPE
    c_tab = tile_rows(jnp.concatenate([one, cos, cos, one[:, :tail]], axis=1))
    sm_tab = tile_rows(jnp.concatenate([zero, -sin, zero[:, :half + tail]], axis=1))
    sp_tab = tile_rows(jnp.concatenate([zero, zero[:, :half], sin, zero[:, :tail]], axis=1))
    q = _make_rope_lanes("rope_q")(q, c_tab, sm_tab, sp_tab)
    kpe = _make_rope_lanes("rope_k")(p_kpe, c_tab, sm_tab, sp_tab)
    o = _make_attention("mla")(q.reshape(bsz, t, -1), kv.reshape(bsz, t, -1), kpe.reshape(bsz, t, LANES))
    return jnp.concatenate([y_rec, o.reshape(r, -1)], axis=1)


def _odd_mixer_fwd(p, bsz):
    r, width = p.shape
    t = r // bsz
    qk = width // 6
    dk = qk // RET_HEADS
    cos2, sin2 = _rope_tables(t, dk // 2, 2)
    sin2 = jnp.concatenate([-sin2[:, :dk // 2], sin2[:, dk // 2:]], axis=1)
    rope_args = (_Cols(p, qk, 0), _Cols(p, qk, 1), jnp.tile(cos2, (bsz, 1)), jnp.tile(sin2, (bsz, 1)))
    (rq, rk), _ = _make_rowwise(_rope_ret_f, "rope_ret", 2, 2, 0).fwd_call(*rope_args)
    lg = jnp.log(1.0 - 2.0 ** (-5.0 - jnp.arange(RET_HEADS, dtype=F32)))
    ret_args = (lg, rq.reshape(bsz, t, qk), rk.reshape(bsz, t, qk), p.reshape(bsz, t, width))
    o, states = _ret_fwd(*ret_args, "ret_fwd", dv=2 * dk, v_block0=qk // dk)
    gate_args = (o.reshape(r, 2 * qk), _Cols(p, 2 * qk, 2))
    (y,), _ = _make_rowwise(_ret_out_bf16_f, "ret_out", 2, 0, 0).fwd_call(*gate_args)
    return y, (rope_args, ret_args + (states,), gate_args)


def _odd_mixer_bwd(res, dy):
    rope_args, ret_args, gate_args = res
    bsz, t, qk = ret_args[1].shape
    dk = qk // RET_HEADS
    (do, dg), _ = _make_rowwise(_ret_out_f, "ret_out", 2, 0, 0).bwd_call(
        gate_args, ((dy,), ()), row_dtypes=(F32, BF16))
    drq, drk, dv = _ret_bwd(*ret_args, do.reshape(bsz, t, 2 * qk), "ret_bwd", v_block0=qk // dk, dv_dtype=BF16)
    (dq, dkk), _ = _make_rowwise(_rope_ret_f, "rope_ret", 2, 2, 0).bwd_call(
        rope_args, ((drq.reshape(bsz * t, qk), drk.reshape(bsz * t, qk)), ()), row_dtypes=(BF16, BF16))
    return jnp.concatenate([dq, dkk, dv.reshape(bsz * t, 2 * qk), dg], axis=1)


def _local_loss(h, target):
    bsz, seq, d = target.shape
    t = _padded_len(seq)
    t_real = N_META + seq
    pos = jnp.arange(t, dtype=jnp.int32)
    mask = jnp.tile(((pos >= N_META) & (pos < t_real)).astype(F32)[:, None], (bsz, 1))
    tgt = jnp.concatenate(
        [jnp.zeros((bsz, N_META, d), F32), target, jnp.zeros((bsz, t - t_real, d), F32)], axis=1).reshape(bsz * t, d)
    _, (total,) = _make_rowwise(_loss_f, "loss", 1, 2, 0)((h,), (tgt, mask), ())
    return total[0, 0]


_WEIGHTS = ("meta_tokens", "ev_w_in", "ev_conv_w", "ev_conv_b", "ev_w_rg_a", "ev_b_rg_a", "ev_w_rg_x", "ev_b_rg_x",
            "ev_lru_lambda", "ev_q_norm_g", "ev_w_uq", "ev_kv_norm_g", "ev_w_ukv", "ev_w_out", "od_w_in", "od_w_out",
            "ln_mix_g", "ln_mix_b", "mlp_w1", "mlp_w2", "ln_mlp_g", "ln_mlp_b")


def kernel(x, meta_tokens, ev_w_in, ev_conv_w, ev_conv_b, ev_w_rg_a, ev_b_rg_a, ev_w_rg_x, ev_b_rg_x, ev_lru_lambda, ev_q_norm_g, ev_w_uq, ev_kv_norm_g, ev_w_ukv, ev_w_out, od_w_in, od_w_out, ln_mix_g, ln_mix_b, mlp_w1, mlp_w2, ln_mlp_g, ln_mlp_b, loss_target, m_meta_tokens, m_ev_w_in, m_ev_conv_w, m_ev_conv_b, m_ev_w_rg_a, m_ev_b_rg_a, m_ev_w_rg_x, m_ev_b_rg_x, m_ev_lru_lambda, m_ev_q_norm_g, m_ev_w_uq, m_ev_kv_norm_g, m_ev_w_ukv, m_ev_w_out, m_od_w_in, m_od_w_out, m_ln_mix_g, m_ln_mix_b, m_mlp_w1, m_mlp_w2, m_ln_mlp_g, m_ln_mlp_b, v_meta_tokens, v_ev_w_in, v_ev_conv_w, v_ev_conv_b, v_ev_w_rg_a, v_ev_b_rg_a, v_ev_w_rg_x, v_ev_b_rg_x, v_ev_lru_lambda, v_ev_q_norm_g, v_ev_w_uq, v_ev_kv_norm_g, v_ev_w_ukv, v_ev_w_out, v_od_w_in, v_od_w_out, v_ln_mix_g, v_ln_mix_b, v_mlp_w1, v_mlp_w2, v_ln_mlp_g, v_ln_mlp_b):
    args = locals()
    weights = {n: args[n] for n in _WEIGHTS}
    bsz = x.shape[0]
    my_x, my_y, my_c = _my_place()
    me = 4 * my_x + 2 * my_y + my_c

    big = (("ev_in", ev_w_in[0], True), ("ev_out", ev_w_out[0], False), ("mlp0_w1", mlp_w1[0], True),
           ("mlp0_w2", mlp_w2[0], False), ("od_in", od_w_in[0], True), ("od_out", od_w_out[0], False),
           ("mlp1_w1", mlp_w1[1], True), ("mlp1_w2", mlp_w2[1], False))
    small_sharded = (("meta", meta_tokens, F32), ("conv_w", ev_conv_w[0], F32), ("ev_uq", ev_w_uq[0], BF16),
                     ("ev_ukv", ev_w_ukv[0], BF16))
    to_gather = (tuple((nm, s.astype(dt), True) for nm, s, dt in small_sharded)
                 + tuple((nm, s.astype(BF16), cols) for nm, s, cols in big))
    handles = _gather_start_all([s for _, s, _ in to_gather], "ag_start")
    gathers = {nm: (s, cols, h) for (nm, s, cols), h in zip(to_gather, handles)}
    gather_tokens = (handles[0][4],)

    def full_weight(nm, after):
        shard16, cols, handle = gathers[nm]
        land = _exchange_wait(handle, True, after, "ag_wait_" + nm)
        land = lax.dynamic_update_index_in_dim(land, shard16, me, 0)
        if cols and shard16.shape[1] % LANES == 0:
            return land, True
        return (_unstack_cols(land) if cols else land.reshape(-1, shard16.shape[1])), False

    meta_full, conv_w_full, w_uq_full, w_ukv_full = (
        _unstack_cols(lax.dynamic_update_index_in_dim(
            _exchange_wait(gathers[nm][2], True, gather_tokens[-1], "ag_wait_" + nm), gathers[nm][0], me, 0))
        for nm, _, _ in small_sharded)

    pending = []

    def linear_bwd(nm, x_in, w, dy, cols, unpad=None, **fused):
        w_full, w_stacked = w
        own = None
        if w_stacked:
            stacked = _matmul(x_in, dy, "tn", nm + "_dw", stacked=True)
        else:
            dw = _matmul(x_in, dy, "tn", nm + "_dw")
            dw = dw if unpad is None else unpad(dw)
            n = dw.shape[1] // N_DEV
            if cols:
                stacked = _stack_cols(dw)
                own = lax.dynamic_slice_in_dim(dw, me * n, n, axis=1)
            else:
                stacked = dw.reshape(N_DEV, dw.shape[0] // N_DEV, dw.shape[1])
        handle = _exchange_start(stacked, (_N_PEERS,) + stacked.shape[1:], False, "rs_start_" + nm)
        if own is None:
            own = lax.dynamic_index_in_dim(handle[2], me, 0, keepdims=False)
        pending.append((nm, own, handle))
        return _matmul(dy, w_full, "nt", nm + "_dx", after=(handle[4],), stacked=w_stacked, **fused)

    def linear_fwd(nm, x_in, w, **fused):
        return _matmul(x_in, w[0], "nn", nm + "_fwd", stacked=w[1], **fused)

    def mlp_fwd(h, h16, l):
        w1 = full_weight(f"mlp{l}_w1", h16)
        a16 = linear_fwd(f"mlp{l}_w1", h16, w1, out_relu2=True, out_dtype=BF16)
        w2 = full_weight(f"mlp{l}_w2", a16)
        f = linear_fwd(f"mlp{l}_w2", a16, w2)
        ln_args = (h, f, ln_mlp_g[l:l + 1], ln_mlp_b[l:l + 1])
        return ln_fwd(f"mlp{l}_ln", *ln_args), (h16, w1, a16, w2, ln_args)

    def mlp_bwd(dout, res, l):
        h16, w1, a16, w2, ln_args = res
        dh, df, dg, db = ln_bwd(f"mlp{l}_ln", ln_args, dout)
        du = linear_bwd(f"mlp{l}_w2", a16, w2, df, False, relu2_bwd_of=a16, out_dtype=BF16)
        return (dh, linear_bwd(f"mlp{l}_w1", h16, w1, du, True)), dg, db

    def ln_fwd(nm, h, mix, g, b):
        return _make_rowwise(_ln_res_copy_f, nm, 2, 0, 2).fwd_call(h, mix, g, b)[0]

    def ln_bwd(nm, ln_args, pieces):
        (dh, dmix), (dg, db) = _make_rowwise(_ln_res_f, nm, 2, 0, 2).bwd_call(
            ln_args, ((pieces[0],), ()), more=tuple(pieces[1:]), row_dtypes=(F32, BF16))
        return dh, dmix, dg, db

    h0, vjp_embed = jax.vjp(_embed, meta_full, x)
    n_in = ev_w_in.shape[2] * N_DEV
    kpe0, pad_lo, pad_hi = n_in - MLA_ROPE, MLA_NOPE, LANES - MLA_NOPE - MLA_ROPE
    w_in = full_weight("ev_in", h0)[0]
    zeros_in = jnp.zeros((w_in.shape[0], pad_lo), BF16)
    w_ev_in = (jnp.concatenate([w_in[:, :kpe0], zeros_in, w_in[:, kpe0:], zeros_in[:, :pad_hi]], axis=1), False)

    def unpad_in(dw):
        return jnp.concatenate([dw[:, :kpe0], dw[:, kpe0 + pad_lo:kpe0 + pad_lo + MLA_ROPE]], axis=1)

    p0 = _matmul(h0, w_ev_in[0], "nn", "ev_in_fwd", after=gather_tokens)
    q_rank, d_head = w_uq_full.shape[0], MLA_NOPE + MLA_ROPE
    w_uq_pad = jnp.pad(w_uq_full.reshape(q_rank, MLA_HEADS, d_head), ((0, 0), (0, 0), (0, LANES - d_head)))
    w_uq_pad = w_uq_pad.reshape(q_rank, MLA_HEADS * LANES)
    small = (conv_w_full, ev_conv_b, ev_w_rg_a, ev_b_rg_a, ev_w_rg_x, ev_b_rg_x, ev_lru_lambda, ev_q_norm_g,
             jnp.zeros(w_uq_pad.shape, F32), ev_kv_norm_g, jnp.zeros(w_ukv_full.shape, F32))
    y0, vjp_even = jax.vjp(lambda p, *s: _even_mixer(p, *s, (w_uq_pad, w_ukv_full), bsz), p0, *small)
    w_out = full_weight("ev_out", y0)[0]
    lru_w, d_model = y0.shape[1] - MLA_HEADS * LANES, w_out.shape[1]
    w_att = w_out[lru_w:].reshape(MLA_HEADS, MLA_V, d_model)
    w_att = jnp.concatenate([jnp.zeros((MLA_HEADS, LANES - MLA_V, d_model), BF16), w_att], axis=1)
    w_ev_out = (jnp.concatenate([w_out[:lru_w], w_att.reshape(MLA_HEADS * LANES, d_model)], axis=0), False)

    def unpad_out(dw):
        d_att = dw[lru_w:].reshape(MLA_HEADS, LANES, d_model)[:, LANES - MLA_V:].reshape(MLA_HEADS * MLA_V, d_model)
        return jnp.concatenate([dw[:lru_w], d_att], axis=0)

    mix0 = linear_fwd("ev_out", y0, w_ev_out)
    ln0_args = (h0, mix0, ln_mix_g[0:1], ln_mix_b[0:1])
    h1, h1_16 = ln_fwd("mix0_ln", *ln0_args)
    (h2, h2_16), res_mlp0 = mlp_fwd(h1, h1_16, 0)
    w_od_in = full_weight("od_in", h2_16)
    p1 = linear_fwd("od_in", h2_16, w_od_in)
    y1, res_odd = _odd_mixer_fwd(p1, bsz)
    w_od_out = full_weight("od_out", y1)
    mix1 = linear_fwd("od_out", y1, w_od_out)
    ln1_args = (h2, mix1, ln_mix_g[1:2], ln_mix_b[1:2])
    h3, h3_16 = ln_fwd("mix1_ln", *ln1_args)
    (h4, _), res_mlp1 = mlp_fwd(h3, h3_16, 1)
    loss_local, vjp_loss = jax.vjp(lambda h: _local_loss(h, loss_target), h4)

    dh4 = vjp_loss(jnp.ones((), F32))
    dh3, dg_mlp1, db_mlp1 = mlp_bwd(dh4, res_mlp1, 1)
    dh2, dmix1, dg_mix1, db_mix1 = ln_bwd("mix1_ln", ln1_args, dh3)
    dp1 = _odd_mixer_bwd(res_odd, linear_bwd("od_out", y1, w_od_out, dmix1, False))
    dh2 = (dh2, linear_bwd("od_in", h2_16, w_od_in, dp1, True))
    dh1, dg_mlp0, db_mlp0 = mlp_bwd(dh2, res_mlp0, 0)
    dh0, dmix0, dg_mix0, db_mix0 = ln_bwd("mix0_ln", ln0_args, dh1)
    dp0, *dsmall = vjp_even(linear_bwd("ev_out", y0, w_ev_out, dmix0, False, unpad=unpad_out))
    dh0 = dh0 + linear_bwd("ev_in", h0, w_ev_in, dp0.astype(BF16), True, unpad=unpad_in)
    g_meta_full, grad_x = vjp_embed(dh0)
    (g_conv_w_full, g_conv_b, g_w_rg_a, g_b_rg_a, g_w_rg_x, g_b_rg_x, g_lambda, g_q_norm, g_uq_pad, g_kv_norm,
     g_ukv_full) = dsmall

    for nm, dw in (("ev_uq", g_uq_pad.reshape(q_rank, MLA_HEADS, LANES)[:, :, :d_head].reshape(q_rank, -1)),
                   ("ev_ukv", g_ukv_full)):
        n = dw.shape[1] // N_DEV
        handle = _exchange_start(_stack_cols(dw), (_N_PEERS, dw.shape[0], n), False, "rs_start_" + nm)
        pending.append((nm, lax.dynamic_slice_in_dim(dw, me * n, n, axis=1), handle))

    rep_names = ("ev_conv_b", "ev_w_rg_a", "ev_b_rg_a", "ev_w_rg_x", "ev_b_rg_x", "ev_lru_lambda", "ev_q_norm_g",
                 "ev_kv_norm_g", "ln_mix_g", "ln_mix_b", "ln_mlp_g", "ln_mlp_b")
    rep_local = (g_conv_b, g_w_rg_a, g_b_rg_a, g_w_rg_x, g_b_rg_x, g_lambda, g_q_norm, g_kv_norm,
                 jnp.concatenate([dg_mix0, dg_mix1]), jnp.concatenate([db_mix0, db_mix1]),
                 jnp.concatenate([dg_mlp0, dg_mlp1]), jnp.concatenate([db_mlp0, db_mlp1]), g_conv_w_full, g_meta_full)
    rep_stacked = _pack_rows(rep_local).reshape(N_DEV, -1, LANES)
    rep_rs = _exchange_start(rep_stacked, (_N_PEERS,) + rep_stacked.shape[1:], False, "rep_rs_start", after=(grad_x,))
    rep_own = lax.dynamic_index_in_dim(rep_rs[2], me, 0, keepdims=False)

    after, summed = rep_rs[4], {}
    for nm, own, handle in pending:
        land = _exchange_wait(handle, False, after, "rs_wait_" + nm)
        summed[nm] = after = _sum_own_and_peers(own, land, "rs_sum_" + nm)
    rep_part = _sum_own_and_peers(rep_own, _exchange_wait(rep_rs, False, after, "rep_rs_wait"), "rep_sum")
    rep_ag = _exchange_start(rep_part, (N_DEV,) + rep_part.shape, True, "rep_ag_start")

    grad_w = dict(ev_w_uq=summed["ev_uq"][None], ev_w_ukv=summed["ev_ukv"][None],
                  ev_w_in=summed["ev_in"][None], ev_w_out=summed["ev_out"][None], od_w_in=summed["od_in"][None],
                  od_w_out=summed["od_out"][None], mlp_w1=jnp.stack([summed["mlp0_w1"], summed["mlp1_w1"]]),
                  mlp_w2=jnp.stack([summed["mlp0_w2"], summed["mlp1_w2"]]))

    loss = lax.psum(loss_local, MESH_AXES)
    delta, new_m, new_v = {}, {}, {}

    def adamw(n):
        w, g, m, v = weights[n], grad_w[n], args["m_" + n], args["v_" + n]
        two_d = (-1, w.shape[-1])
        d2, m2, v2 = _adamw(w.reshape(two_d), g.reshape(two_d), m.reshape(two_d), v.reshape(two_d), "adamw_" + n)
        delta[n], new_m[n], new_v[n] = d2.reshape(w.shape), m2.reshape(w.shape), v2.reshape(w.shape)
        return d2

    for n in tuple(grad_w):
        after = adamw(n)
    rep_land = lax.dynamic_update_index_in_dim(_exchange_wait(rep_ag, True, after, "rep_ag_wait"), rep_part, me, 0)
    *rep_total, t_conv_w, t_meta = _unpack_rows(rep_land, rep_local)
    n_conv, n_meta = ev_conv_w.shape[2], meta_tokens.shape[1]
    small_g = dict(zip(rep_names, rep_total))
    small_g.update(meta_tokens=lax.dynamic_slice_in_dim(t_meta, me * n_meta, n_meta, axis=1),
                   ev_conv_w=lax.dynamic_slice_in_dim(t_conv_w, me * n_conv, n_conv, axis=1)[None])
    grad_w.update(small_g)
    for n in small_g:
        adamw(n)
    return (loss, grad_x, *[grad_w[n] for n in _WEIGHTS], *[delta[n] for n in _WEIGHTS],
            *[new_m[n] for n in _WEIGHTS], *[new_v[n] for n in _WEIGHTS])
```

```python
import math

import jax
import jax.numpy as jnp
from jax import lax
from jax.experimental import pallas as pl
from jax.experimental.pallas import tpu as pltpu

F32 = jnp.float32
BF16 = jnp.bfloat16

N_DEV = 8
MESH_AXES = ("x", "y", "c")
LANES = 128
SEQ_BLOCK = 128

N_META = 16
LRU_C = 8.0
MLA_HEADS = 8
MLA_NOPE = 64
MLA_ROPE = 32
MLA_V = 64
RET_HEADS = 4
ROPE_BASE = 10000.0
DEPTH = 2
DN_ALPHA = (2 * DEPTH) ** 0.25
EPS = 1e-5
NEG_INF = -1e30

ADAM_LR = 0.001
ADAM_B1 = 0.9
ADAM_B2 = 0.999
ADAM_EPS = 1e-08
ADAM_WD = 0.01
ADAM_STEP = 10

VMEM_LIMIT = 56 * 1024 * 1024


def _params(*sem):
    return pltpu.CompilerParams(dimension_semantics=sem, vmem_limit_bytes=VMEM_LIMIT)


def _pick(n, cands):
    for c in cands:
        if n % c == 0:
            return c
    return n


def _row_tile(r, width):
    cands = (256, 128, 64, 32, 16, 8) if width <= 1024 else (128, 64, 32, 16, 8)
    return _pick(r, cands)


_DIMS = {"nn": (((1,), (0,)), ((), ())), "nt": (((1,), (1,)), ((), ())), "tn": (((0,), (0,)), ((), ()))}


def _dot(a, b, mode):
    return lax.dot_general(a.astype(BF16), b.astype(BF16), _DIMS[mode], preferred_element_type=F32)


def _matmul(a, b, mode, name, after=(), stacked=False, relu2_bwd_of=None, out_dtype=F32, out_relu2=False):
    if stacked:
        n_blk = b.shape[2] if mode != "tn" else b.shape[1] // N_DEV
    if mode == "nn":
        (m, k), n = a.shape, (N_DEV * n_blk if stacked else b.shape[1])
    elif mode == "nt":
        (m, k), n = a.shape, (b.shape[1] if stacked else b.shape[0])
    else:
        (k, m), n = a.shape, b.shape[1]
    tm = _pick(m, (2176, 1088, 1024, 544, 512, 272, 256, 128, 64, 32, 16, 8))
    tn = _pick(n, (1024, 512, 256, 128))
    tk = _pick(k, (1088, 1024, 544, 512, 272, 256, 128))
    kb = 2
    if stacked and mode == "nn":
        tn = n_blk
    if stacked and mode == "tn":
        tn = kb * n_blk
    if stacked and mode == "nt":
        tk = kb * n_blk
    nk = k // tk
    assert out_dtype == F32 or (nk == 1 and not (stacked and mode == "tn")), "narrow results need a single k step"
    assert not out_relu2 or nk == 1, "relu^2 is applied to a finished tile"

    out_spec = pl.BlockSpec((tm, tn), lambda i, j, kk: (i, j))
    out_shape = jax.ShapeDtypeStruct((m, n), out_dtype)
    if mode == "nn":
        a_spec = pl.BlockSpec((tm, tk), lambda i, j, kk: (i, kk))
        b_spec = pl.BlockSpec((tk, tn), lambda i, j, kk: (kk, j))
        if stacked:
            b_spec = pl.BlockSpec((None, tk, tn), lambda i, j, kk: (j, kk, 0))
    elif mode == "nt":
        a_spec = pl.BlockSpec((tm, tk), lambda i, j, kk: (i, kk))
        b_spec = pl.BlockSpec((tn, tk), lambda i, j, kk: (j, kk))
        if stacked:
            b_spec = pl.BlockSpec((kb, tn, n_blk), lambda i, j, kk: (kk, j, 0))
    else:
        a_spec = pl.BlockSpec((tk, tm), lambda i, j, kk: (kk, i))
        b_spec = pl.BlockSpec((tk, tn), lambda i, j, kk: (kk, j))
        if stacked:
            out_spec = pl.BlockSpec((kb, tm, n_blk), lambda i, j, kk: (j, i, 0))
            out_shape = jax.ShapeDtypeStruct((N_DEV, m, n_blk), F32)
    extra = [] if relu2_bwd_of is None else [relu2_bwd_of]
    extra_specs = [pl.BlockSpec((tm, tn), lambda i, j, kk: (i, j))] * len(extra)

    def body(a_ref, b_ref, *rest):
        def relu2_slope():
            return 2.0 * jnp.sqrt(rest[0][...].astype(F32))

        o_ref = rest[-1]
        kk = pl.program_id(2)
        av = a_ref[...]
        if stacked and mode == "nt":
            part = _dot(av[:, :n_blk], b_ref[0], mode)
            for q in range(1, kb):
                part = part + _dot(av[:, q * n_blk:(q + 1) * n_blk], b_ref[q], mode)
        else:
            part = _dot(av, b_ref[...], mode)
        if stacked and mode == "tn":
            part = jnp.stack([part[:, q * n_blk:(q + 1) * n_blk] for q in range(kb)])
        if nk == 1:
            if out_relu2:
                part = jnp.maximum(part, 0.0)
                part = part * part
            if relu2_bwd_of is not None:
                part = part * relu2_slope()
            o_ref[...] = part.astype(out_dtype)
            return

        @pl.when(kk == 0)
        def _():
            o_ref[...] = part

        @pl.when(kk != 0)
        def _():
            o_ref[...] += part

        if relu2_bwd_of is not None:
            @pl.when(kk == nk - 1)
            def _():
                o_ref[...] *= relu2_slope()

    return pl.pallas_call(
        body,
        name=name,
        grid=(m // tm, n // tn, nk),
        in_specs=[a_spec, b_spec] + extra_specs + [pl.BlockSpec(memory_space=pl.ANY)] * len(after),
        out_specs=out_spec,
        out_shape=out_shape,
        compiler_params=_params("parallel", "parallel", "arbitrary"),
    )(a, b, *extra, *after)


def _make_gate_pair(name):
    def call(body, ins, n_out, reduce_rows, tag):
        x = ins[0]
        m, d = x.shape[0], LANES
        g = x.shape[1] // d
        tm = _pick(m, (1088, 1024, 544, 512, 272, 256, 128, 64, 32, 16, 8))
        rows = pl.BlockSpec((tm, d), lambda h, i: (i, h))
        mats = pl.BlockSpec((1, d, d), lambda h, i: (h, 0, 0))
        return pl.pallas_call(
            body,
            name=name + tag,
            grid=(g, m // tm),
            in_specs=[rows if a.ndim == 2 else mats for a in ins],
            out_specs=[mats if reduce_rows else rows] * n_out,
            out_shape=[jax.ShapeDtypeStruct((g, d, d) if reduce_rows else (m, g * d), F32)] * n_out,
            compiler_params=_params("parallel", "arbitrary" if reduce_rows else "parallel"),
        )(*ins)

    def fwd_body(x_ref, wa_ref, wx_ref, a_ref, b_ref):
        xv = x_ref[...]
        a_ref[...] = _dot(xv, wa_ref[0], "nn")
        b_ref[...] = _dot(xv, wx_ref[0], "nn")

    def dx_body(da_ref, db_ref, wa_ref, wx_ref, dx_ref):
        dx_ref[...] = _dot(da_ref[...], wa_ref[0], "nt") + _dot(db_ref[...], wx_ref[0], "nt")

    def dw_body(x_ref, da_ref, db_ref, dwa_ref, dwx_ref):
        xv = x_ref[...]
        pa, pb = _dot(xv, da_ref[...], "tn"), _dot(xv, db_ref[...], "tn")

        @pl.when(pl.program_id(1) == 0)
        def _():
            dwa_ref[0] = pa
            dwx_ref[0] = pb

        @pl.when(pl.program_id(1) != 0)
        def _():
            dwa_ref[0] += pa
            dwx_ref[0] += pb

    @jax.custom_vjp
    def op(x, wa, wx):
        return tuple(call(fwd_body, (x, wa, wx), 2, False, "_fwd"))

    def fwd(x, wa, wx):
        return op(x, wa, wx), (x, wa, wx)

    def bwd(res, cots):
        x, wa, wx = res
        da, db = cots
        (dx,) = call(dx_body, (da, db, wa, wx), 1, False, "_dx")
        dwa, dwx = call(dw_body, (x, da, db), 2, True, "_dw")
        return dx, dwa, dwx

    op.defvjp(fwd, bwd)
    return op


def _my_place():
    return lax.axis_index("x"), lax.axis_index("y"), lax.axis_index("c")


def _stack_cols(full):
    k, n8 = full.shape
    return full.reshape(k, N_DEV, n8 // N_DEV).transpose(1, 0, 2)


def _unstack_cols(stacked):
    j, k, n = stacked.shape
    return stacked.transpose(1, 0, 2).reshape(k, j * n)


def _split_cols(p, cuts):
    bounds = (0,) + tuple(cuts) + (p.shape[1],)

    @jax.custom_vjp
    def op(z):
        return tuple(z[:, lo:hi] for lo, hi in zip(bounds[:-1], bounds[1:]))

    op.defvjp(lambda z: (op(z), None), lambda _, cots: (jnp.concatenate(cots, axis=1),))
    return op(p)


def _make_slot_linear(name):
    @jax.custom_vjp
    def op(x, w_full, slot):
        return _matmul(x, w_full, "nn", name + "_fwd")

    def fwd(x, w_full, slot):
        return op(x, w_full, slot), (x, w_full)

    def bwd(res, dy):
        x, w = res
        return _matmul(dy, w, "nt", name + "_dx"), jnp.zeros_like(w), _matmul(x, dy, "tn", name + "_dw")

    op.defvjp(fwd, bwd)
    return op


def _pack_rows(gs):
    flat = jnp.concatenate([g.reshape(-1) for g in gs])
    n = flat.shape[0]
    rows = -(-n // (256 * LANES)) * 256
    return jnp.pad(flat, (0, rows * LANES - n)).reshape(rows, LANES)


def _unpack_rows(packed, like):
    flat, out, off = packed.reshape(-1), [], 0
    for g in like:
        out.append(flat[off:off + g.size].reshape(g.shape))
        off += g.size
    return out


_HBM = pl.BlockSpec(memory_space=pltpu.HBM)
_SEM = pl.BlockSpec(memory_space=pltpu.SEMAPHORE)
_SIDE_EFFECT = pltpu.SideEffectType.DATAFLOW_SIDE_EFFECTING
_N_PEERS = N_DEV - 1


def _peer(k):
    x, y, c = _my_place()
    return x ^ ((k >> 2) & 1), y ^ ((k >> 1) & 1), c ^ (k & 1)


def _exchange_start(src, land_shape, gather, name, after=()):
    def body(src_ref, land_ref, *rest):
        send_sems, recv_sems, src_thru, land_thru, token = rest[len(after):]
        x, y, c = _my_place()
        me = 4 * x + 2 * y + c
        for k in range(1, N_DEV):
            px, py, pc = _peer(k)
            pltpu.make_async_remote_copy(
                src_ref=src_ref if gather else src_ref.at[4 * px + 2 * py + pc],
                dst_ref=land_ref.at[me] if gather else land_ref.at[k - 1],
                send_sem=send_sems.at[k - 1],
                recv_sem=recv_sems.at[k - 1],
                device_id=(px, py, pc),
                device_id_type=pl.DeviceIdType.MESH,
            ).start()
        token[...] = jnp.zeros_like(token)

    return pl.pallas_call(
        body,
        name=name,
        out_shape=(
            pltpu.SemaphoreType.DMA((_N_PEERS,)),
            pltpu.SemaphoreType.DMA((_N_PEERS,)),
            pltpu.HBM(src.shape, src.dtype),
            pltpu.HBM(land_shape, src.dtype),
            jax.ShapeDtypeStruct((8, LANES), F32),
        ),
        in_specs=(_HBM, _HBM) + (pl.BlockSpec(memory_space=pl.ANY),) * len(after),
        out_specs=(_SEM, _SEM, _HBM, _HBM, pl.BlockSpec(memory_space=pltpu.VMEM)),
        input_output_aliases={0: 2, 1: 3},
        compiler_params=pltpu.CompilerParams(has_side_effects=_SIDE_EFFECT),
    )(pltpu.with_memory_space_constraint(src, pltpu.HBM),
      pltpu.with_memory_space_constraint(lax.empty(land_shape, src.dtype), pltpu.HBM), *after)


def _gather_start_all(shards, name):
    n = len(shards)

    def body(*refs):
        srcs, lands = refs[:n], refs[n:2 * n]
        outs = refs[2 * n:]
        send_sems, recv_sems, token = outs[:n], outs[n:2 * n], outs[-1]
        x, y, c = _my_place()
        me = 4 * x + 2 * y + c
        for i in range(n):
            for k in range(1, N_DEV):
                pltpu.make_async_remote_copy(
                    src_ref=srcs[i],
                    dst_ref=lands[i].at[me],
                    send_sem=send_sems[i].at[k - 1],
                    recv_sem=recv_sems[i].at[k - 1],
                    device_id=_peer(k),
                    device_id_type=pl.DeviceIdType.MESH,
                ).start()
        token[...] = jnp.zeros_like(token)

    lands = [(N_DEV,) + s.shape for s in shards]
    sems = tuple(pltpu.SemaphoreType.DMA((_N_PEERS,)) for _ in range(2 * n))
    res = pl.pallas_call(
        body,
        name=name,
        out_shape=sems + tuple(pltpu.HBM(s.shape, s.dtype) for s in shards)
        + tuple(pltpu.HBM(ls, s.dtype) for ls, s in zip(lands, shards)) + (jax.ShapeDtypeStruct((8, LANES), F32),),
        in_specs=(_HBM,) * (2 * n),
        out_specs=(_SEM,) * (2 * n) + (_HBM,) * (2 * n) + (pl.BlockSpec(memory_space=pltpu.VMEM),),
        input_output_aliases={i: 2 * n + i for i in range(2 * n)},
        compiler_params=pltpu.CompilerParams(has_side_effects=_SIDE_EFFECT),
    )(*[pltpu.with_memory_space_constraint(s, pltpu.HBM) for s in shards],
      *[pltpu.with_memory_space_constraint(lax.empty(ls, s.dtype), pltpu.HBM) for ls, s in zip(lands, shards)])
    return [(res[i], res[n + i], res[2 * n + i], res[3 * n + i], res[-1]) for i in range(n)]


def _exchange_wait(handle, gather, after, name):
    send_sems, recv_sems, src_thru, land_thru, _ = handle

    def body(src_ref, land_ref, send_sems, recv_sems, after_ref, src_dead, got_ref):
        for k in range(1, N_DEV):
            cp = pltpu.make_async_remote_copy(
                src_ref=src_ref if gather else src_ref.at[k],
                dst_ref=land_ref.at[k - 1],
                send_sem=send_sems.at[k - 1],
                recv_sem=recv_sems.at[k - 1],
                device_id=_peer(k),
                device_id_type=pl.DeviceIdType.MESH,
            )
            cp.wait_send()
            cp.wait_recv()

    return pl.pallas_call(
        body,
        name=name,
        out_shape=(pltpu.HBM(src_thru.shape, src_thru.dtype), pltpu.HBM(land_thru.shape, land_thru.dtype)),
        in_specs=(_HBM, _HBM, _SEM, _SEM, pl.BlockSpec(memory_space=pl.ANY)),
        out_specs=(_HBM, _HBM),
        input_output_aliases={0: 0, 1: 1},
        compiler_params=pltpu.CompilerParams(has_side_effects=_SIDE_EFFECT),
    )(src_thru, land_thru, send_sems, recv_sems, after)[1]


def _sum_own_and_peers(own, land, name):
    r, c = own.shape
    tr = _pick(r, (256, 128, 64, 32, 16, 8))

    def body(o_ref, l_ref, out_ref):
        s = [l_ref[j] for j in range(_N_PEERS)]
        out_ref[...] = ((o_ref[...] + s[0]) + (s[1] + s[2])) + ((s[3] + s[4]) + (s[5] + s[6]))

    return pl.pallas_call(
        body,
        name=name,
        grid=(r // tr,),
        in_specs=[pl.BlockSpec((tr, c), lambda i: (i, 0)), pl.BlockSpec((_N_PEERS, tr, c), lambda i: (0, i, 0))],
        out_specs=pl.BlockSpec((tr, c), lambda i: (i, 0)),
        out_shape=jax.ShapeDtypeStruct((r, c), own.dtype),
        compiler_params=_params("parallel"),
    )(own, land)


class _Cols:
    def __init__(self, array, width, block):
        self.array, self.width, self.block = array, width, block
        self.shape, self.dtype = (array.shape[0], width), array.dtype


def _base(a):
    return a.array if isinstance(a, _Cols) else a


def _col_block(a):
    return a.block if isinstance(a, _Cols) else 0


def _make_rowwise(f, name, n_rows, n_tabs, n_pars):
    n_in = n_rows + n_tabs + n_pars

    def specs(args, tm):
        blocked = [pl.BlockSpec((tm, a.shape[1]), lambda i, blk=_col_block(a): (i, blk)) for a in args[: n_rows + n_tabs]]
        whole = [pl.BlockSpec(a.shape, lambda i: (0, 0)) for a in args[n_rows + n_tabs:]]
        return blocked + whole

    def out_struct(args, tm):
        blk = [jax.ShapeDtypeStruct((tm, a.shape[1]), a.dtype) for a in args[: n_rows + n_tabs]]
        blk += [jax.ShapeDtypeStruct(a.shape, a.dtype) for a in args[n_rows + n_tabs:]]
        return jax.eval_shape(f, *blk)

    def fwd_call(*args):
        r = args[0].shape[0]
        tm = _row_tile(r, max(a.shape[1] for a in args[:n_rows]))
        ro, so = out_struct(args, tm)

        def body(*refs):
            vals = [x[...] for x in refs[:n_in]]
            outs = refs[n_in:]
            rv, sv = f(*vals)
            for o, v in zip(outs[: len(ro)], rv):
                o[...] = v
            for o, v in zip(outs[len(ro):], sv):
                @pl.when(pl.program_id(0) == 0)
                def _(o=o, v=v):
                    o[...] = v

                @pl.when(pl.program_id(0) != 0)
                def _(o=o, v=v):
                    o[...] += v

        out_shape = [jax.ShapeDtypeStruct((r, s.shape[1]), s.dtype) for s in ro]
        out_shape += [jax.ShapeDtypeStruct(s.shape, s.dtype) for s in so]
        out_specs = [pl.BlockSpec((tm, s.shape[1]), lambda i: (i, 0)) for s in ro]
        out_specs += [pl.BlockSpec(s.shape, lambda i: (0, 0)) for s in so]
        res = pl.pallas_call(
            body,
            name=name + "_fwd",
            grid=(r // tm,),
            in_specs=specs(args, tm),
            out_specs=out_specs,
            out_shape=out_shape,
            compiler_params=_params("arbitrary" if so else "parallel"),
        )(*[_base(a) for a in args])
        return tuple(res[: len(ro)]), tuple(res[len(ro):])

    def bwd_call(args, cots, more=(), row_dtypes=None):
        r = args[0].shape[0]
        tm = _row_tile(r, max(a.shape[1] for a in args[:n_rows]))
        ro, so = out_struct(args, tm)
        crow, csum = cots
        rows, tabs, pars = args[:n_rows], args[n_rows:n_rows + n_tabs], args[n_rows + n_tabs:]
        n_c = len(crow) + len(csum)

        def body(*refs):
            vals = [x[...] for x in refs[:n_in]]
            cv = [x[...] for x in refs[n_in:n_in + n_c]]
            for x in refs[n_in + n_c:n_in + n_c + len(more)]:
                cv[0] = cv[0] + x[...]
            outs = refs[n_in + n_c + len(more):]
            tv = vals[n_rows:n_rows + n_tabs]

            def g(*dargs):
                return f(*dargs[:n_rows], *tv, *dargs[n_rows:])

            _, vjp = jax.vjp(g, *vals[:n_rows], *vals[n_rows + n_tabs:])
            d = vjp((tuple(cv[: len(crow)]), tuple(cv[len(crow):])))
            for o, v in zip(outs[:n_rows], d[:n_rows]):
                o[...] = v.astype(o.dtype)
            for o, v in zip(outs[n_rows:], d[n_rows:]):
                @pl.when(pl.program_id(0) == 0)
                def _(o=o, v=v):
                    o[...] = v

                @pl.when(pl.program_id(0) != 0)
                def _(o=o, v=v):
                    o[...] += v

        in_specs = specs(args, tm)
        in_specs += [pl.BlockSpec((tm, c.shape[1]), lambda i: (i, 0)) for c in crow]
        in_specs += [pl.BlockSpec(c.shape, lambda i: (0, 0)) for c in csum]
        in_specs += [pl.BlockSpec((tm, c.shape[1]), lambda i: (i, 0)) for c in more]
        out_shape = [jax.ShapeDtypeStruct(a.shape, dt) for a, dt in zip(rows, row_dtypes or [a.dtype for a in rows])]
        out_shape += [jax.ShapeDtypeStruct(a.shape, a.dtype) for a in pars]
        out_specs = [pl.BlockSpec((tm, a.shape[1]), lambda i: (i, 0)) for a in rows]
        out_specs += [pl.BlockSpec(a.shape, lambda i: (0, 0)) for a in pars]
        res = pl.pallas_call(
            body,
            name=name + "_bwd",
            grid=(r // tm,),
            in_specs=in_specs,
            out_specs=out_specs,
            out_shape=out_shape,
            compiler_params=_params("arbitrary" if pars else "parallel"),
        )(*[_base(a) for a in args], *crow, *csum, *more)
        return tuple(res[:n_rows]), tuple(res[n_rows:])

    @jax.custom_vjp
    def op(rows, tabs, pars):
        return fwd_call(*rows, *tabs, *pars)

    op.fwd_call, op.bwd_call = fwd_call, bwd_call

    def fwd(rows, tabs, pars):
        return fwd_call(*rows, *tabs, *pars), (rows, tabs, pars)

    def bwd(res, cots):
        rows, tabs, pars = res
        drows, dpars = bwd_call(tuple(rows) + tuple(tabs) + tuple(pars), cots)
        return drows, tuple(jnp.zeros_like(t) for t in tabs), dpars

    op.defvjp(fwd, bwd)
    return op


def _sigmoid(x):
    return 0.5 * (jnp.tanh(0.5 * x) + 1.0)


@jax.custom_jvp
def _softplus(x):
    e = jnp.exp(-jnp.abs(x))
    u = 1.0 + e
    log1p_e = jnp.where(u == 1.0, e, e * jnp.log(u) / jnp.where(u == 1.0, 1.0, u - 1.0))
    return jnp.maximum(x, 0.0) + log1p_e


@_softplus.defjvp
def _softplus_jvp(primals, tangents):
    (x,), (t,) = primals, tangents
    return _softplus(x), t * _sigmoid(x)


def _gelu(x):
    return 0.5 * x * (1.0 + jnp.tanh(math.sqrt(2.0 / math.pi) * (x + 0.044715 * (x * x * x))))


def _ln_res_f(h, mix, g, b):
    z = DN_ALPHA * h + mix
    mu = jnp.mean(z, axis=-1, keepdims=True)
    zc = z - mu
    var = jnp.mean(zc * zc, axis=-1, keepdims=True)
    return (zc * lax.rsqrt(var + EPS) * g + b,), ()


def _ln_res_copy_f(h, mix, g, b):
    (out,), _ = _ln_res_f(h, mix, g, b)
    return (out, out.astype(BF16)), ()


def _rmsnorm_f(x, g):
    return (x * lax.rsqrt(jnp.mean(x * x, axis=-1, keepdims=True) + EPS) * g,), ()


def _lru_gates_f(ga, gx, xc, b_a, b_x, lam):
    r = _sigmoid(ga + b_a)
    i = _sigmoid(gx + b_x)
    log_a = -LRU_C * r * _softplus(-lam)
    a = jnp.exp(log_a)
    one_minus_a2 = jnp.tanh(-log_a) * (jnp.exp(2.0 * log_a) + 1.0)
    return (a, jnp.sqrt(one_minus_a2) * (i * xc)), ()


def _lru_out_f(hh, p_gate):
    return (hh * _gelu(p_gate),), ()


def _rope_ret_f(q, k, cos2, sin2):
    d = cos2.shape[1]
    half = d // 2
    k_scale = d ** -0.5

    def rope(x):
        outs = []
        for h in range(x.shape[1] // d):
            xh = x[:, h * d:(h + 1) * d]
            rot = jnp.concatenate([xh[:, half:], xh[:, :half]], axis=1)
            outs.append(xh * cos2 + rot * sin2)
        return jnp.concatenate(outs, axis=1)

    return (rope(q), rope(k) * k_scale), ()


def _ret_out_f(o, g):
    d = o.shape[1] // RET_HEADS
    outs = []
    for h in range(RET_HEADS):
        oh = o[:, h * d:(h + 1) * d]
        outs.append(oh * lax.rsqrt(jnp.mean(oh * oh, axis=-1, keepdims=True) + EPS))
    y = jnp.concatenate(outs, axis=1)
    return (g * _sigmoid(g) * y,), ()


def _ret_out_bf16_f(o, g):
    (y,), _ = _ret_out_f(o, g)
    return (y.astype(BF16),), ()


def _loss_f(y, t, mask):
    e = (y - t) * mask
    per_row = jnp.sum(e * e, axis=-1, keepdims=True) * (0.5 / y.shape[1])
    total = jnp.sum(per_row, axis=0, keepdims=True)
    return (), (jnp.broadcast_to(total, (1, LANES)),)


def _shift_down(x, s):
    if s == 0:
        return x
    t = x.shape[0]
    row = lax.broadcasted_iota(jnp.int32, x.shape, 0)
    return jnp.where(row >= s, pltpu.roll(x, s, 0), 0.0)


def _shift_up(x, s):
    if s == 0:
        return x
    t = x.shape[0]
    row = lax.broadcasted_iota(jnp.int32, x.shape, 0)
    return jnp.where(row < t - s, pltpu.roll(x, t - s, 0), 0.0)


def _conv_fwd(x, w, b, name):
    bsz, t, c = x.shape
    width = w.shape[0]

    def body(x_ref, w_ref, b_ref, y_ref):
        xv = x_ref[0]
        acc = jnp.broadcast_to(b_ref[...], xv.shape)
        for k in range(width):
            acc = acc + w_ref[k:k + 1, :] * _shift_down(xv, width - 1 - k)
        y_ref[0] = acc

    return pl.pallas_call(
        body,
        name=name,
        grid=(bsz, c // LANES),
        in_specs=[
            pl.BlockSpec((1, t, LANES), lambda i, j: (i, 0, j)),
            pl.BlockSpec((width, LANES), lambda i, j: (0, j)),
            pl.BlockSpec((1, LANES), lambda i, j: (0, j)),
        ],
        out_specs=pl.BlockSpec((1, t, LANES), lambda i, j: (i, 0, j)),
        out_shape=jax.ShapeDtypeStruct(x.shape, F32),
        compiler_params=_params("parallel", "parallel"),
    )(x, w, b)


def _conv_bwd(x, w, dy, name):
    bsz, t, c = x.shape
    width = w.shape[0]

    def body(x_ref, w_ref, dy_ref, dx_ref, dw_ref, db_ref):
        xv, g = x_ref[0], dy_ref[0]
        dx = jnp.zeros_like(xv)
        dws = []
        for k in range(width):
            s = width - 1 - k
            dx = dx + w_ref[k:k + 1, :] * _shift_up(g, s)
            dws.append(jnp.sum(g * _shift_down(xv, s), axis=0, keepdims=True))
        dx_ref[0] = dx
        dw = jnp.concatenate(dws, axis=0)
        db = jnp.sum(g, axis=0, keepdims=True)

        @pl.when(pl.program_id(1) == 0)
        def _():
            dw_ref[...] = dw
            db_ref[...] = db

        @pl.when(pl.program_id(1) != 0)
        def _():
            dw_ref[...] += dw
            db_ref[...] += db

    return pl.pallas_call(
        body,
        name=name,
        grid=(c // LANES, bsz),
        in_specs=[
            pl.BlockSpec((1, t, LANES), lambda j, i: (i, 0, j)),
            pl.BlockSpec((width, LANES), lambda j, i: (0, j)),
            pl.BlockSpec((1, t, LANES), lambda j, i: (i, 0, j)),
        ],
        out_specs=[
            pl.BlockSpec((1, t, LANES), lambda j, i: (i, 0, j)),
            pl.BlockSpec((width, LANES), lambda j, i: (0, j)),
            pl.BlockSpec((1, LANES), lambda j, i: (0, j)),
        ],
        out_shape=[
            jax.ShapeDtypeStruct(x.shape, F32),
            jax.ShapeDtypeStruct(w.shape, F32),
            jax.ShapeDtypeStruct((1, c), F32),
        ],
        compiler_params=_params("parallel", "arbitrary"),
    )(x, w, dy)


def _make_conv(name):
    @jax.custom_vjp
    def op(x, w, b):
        return _conv_fwd(x, w, b, name + "_fwd")

    def fwd(x, w, b):
        return op(x, w, b), (x, w)

    def bwd(res, dy):
        x, w = res
        return tuple(_conv_bwd(x, w, dy, name + "_bwd"))

    op.defvjp(fwd, bwd)
    return op


_SCAN_ROWS = 8


def _scan_fwd(a, b, name):
    bsz, t, c = a.shape
    cw = _pick(c, (4 * LANES, 2 * LANES, LANES))

    def body(a_ref, b_ref, h_ref):
        row = lax.broadcasted_iota(jnp.int32, (_SCAN_ROWS, cw), 0)

        def step(i, carry):
            r0 = pl.multiple_of(i * _SCAN_ROWS, _SCAN_ROWS)
            av, bv = a_ref[0, pl.ds(r0, _SCAN_ROWS), :], b_ref[0, pl.ds(r0, _SCAN_ROWS), :]
            for s in (1, 2, 4):
                a_sh = jnp.where(row >= s, pltpu.roll(av, s, 0), 1.0)
                b_sh = jnp.where(row >= s, pltpu.roll(bv, s, 0), 0.0)
                bv = av * b_sh + bv
                av = av * a_sh
            hv = bv + av * carry
            h_ref[0, pl.ds(r0, _SCAN_ROWS), :] = hv
            return hv[_SCAN_ROWS - 1:, :]

        lax.fori_loop(0, t // _SCAN_ROWS, step, jnp.zeros((1, cw), F32), unroll=2)

    spec = pl.BlockSpec((1, t, cw), lambda i, j: (i, 0, j))
    return pl.pallas_call(
        body,
        name=name,
        grid=(bsz, c // cw),
        in_specs=[spec, spec],
        out_specs=spec,
        out_shape=jax.ShapeDtypeStruct(a.shape, F32),
        compiler_params=_params("parallel", "parallel"),
    )(a, b)


def _scan_bwd(a, h, g, name):
    bsz, t, c = a.shape
    cw = _pick(c, (2 * LANES, LANES))

    def body(a_ref, h_ref, g_ref, da_ref, db_ref):
        rows = _SCAN_ROWS
        row = lax.broadcasted_iota(jnp.int32, (rows, cw), 0)
        n_tiles = t // rows

        def step(n, carry):
            lam_next, a_next = carry
            i = n_tiles - 1 - n
            r0 = pl.multiple_of(i * rows, rows)
            rp = pl.multiple_of(jnp.maximum(i - 1, 0) * rows, rows)
            av, gv, hv = a_ref[0, pl.ds(r0, rows), :], g_ref[0, pl.ds(r0, rows), :], h_ref[0, pl.ds(r0, rows), :]
            h_before = jnp.where(i > 0, h_ref[0, pl.ds(rp, rows), :][rows - 1:, :], 0.0)
            cv = jnp.where(row < rows - 1, pltpu.roll(av, rows - 1, 0), a_next)
            for s in (1, 2, 4):
                c_sh = jnp.where(row < rows - s, pltpu.roll(cv, rows - s, 0), 1.0)
                g_sh = jnp.where(row < rows - s, pltpu.roll(gv, rows - s, 0), 0.0)
                gv = cv * g_sh + gv
                cv = cv * c_sh
            lam = gv + cv * lam_next
            db_ref[0, pl.ds(r0, rows), :] = lam
            da_ref[0, pl.ds(r0, rows), :] = lam * jnp.where(row >= 1, pltpu.roll(hv, 1, 0), h_before)
            return lam[:1, :], av[:1, :]

        zero = jnp.zeros((1, cw), F32)
        lax.fori_loop(0, n_tiles, step, (zero, zero), unroll=2)

    spec = pl.BlockSpec((1, t, cw), lambda i, j: (i, 0, j))
    return pl.pallas_call(
        body,
        name=name,
        grid=(bsz, c // cw),
        in_specs=[spec, spec, spec],
        out_specs=[spec, spec],
        out_shape=[jax.ShapeDtypeStruct(a.shape, F32)] * 2,
        compiler_params=_params("parallel", "parallel"),
    )(a, h, g)


def _make_scan(name):
    @jax.custom_vjp
    def op(a, b):
        return _scan_fwd(a, b, name + "_fwd")

    def fwd(a, b):
        h = op(a, b)
        return h, (a, h)

    def bwd(res, g):
        a, h = res
        da, db = _scan_bwd(a, h, g, name + "_bwd")
        return da, db

    op.defvjp(fwd, bwd)
    return op


def _query_blocks(t):
    blocks, start = [], 0
    while start < t:
        rows = 2 * SEQ_BLOCK if start + 2 * SEQ_BLOCK <= t else SEQ_BLOCK
        blocks.append((start, rows))
        start += rows
    return blocks


def _attn_exp(q, k, start, scale):
    tq, tk = q.shape[0], k.shape[0]
    s = _dot(q, k, "nt") * scale
    qpos = start + lax.broadcasted_iota(jnp.int32, (tq, tk), 0)
    kpos = lax.broadcasted_iota(jnp.int32, (tq, tk), 1)
    s = jnp.where(kpos <= qpos, s, NEG_INF)
    e = jnp.exp(s - jnp.max(s, axis=-1, keepdims=True))
    return e, 1.0 / jnp.sum(e, axis=-1, keepdims=True)


_MLA_SCALE = (MLA_NOPE + MLA_ROPE) ** -0.5


def _attn_specs(t):
    head = pl.BlockSpec((1, t, LANES), lambda b, h: (b, 0, h))
    shared = pl.BlockSpec((1, t, LANES), lambda b, h: (b, 0, 0))
    return head, shared


def _attn_fwd(q, kv, kpe, name):
    bsz, t, hl = q.shape
    head, shared = _attn_specs(t)

    def body(q_ref, kv_ref, kpe_ref, o_ref, k_s, v_s):
        lane = lax.broadcasted_iota(jnp.int32, (t, LANES), 1)
        kvh = kv_ref[0]
        k_s[...] = jnp.where(lane < MLA_NOPE, kvh, kpe_ref[0]).astype(BF16)
        v_s[...] = kvh.astype(BF16)
        for start, rows in _query_blocks(t):
            n = start + rows
            e, inv_l = _attn_exp(q_ref[0, start:n, :], k_s[:n, :], start, _MLA_SCALE)
            o_ref[0, start:n, :] = _dot(e, v_s[:n, :], "nn") * inv_l

    return pl.pallas_call(
        body,
        name=name,
        grid=(bsz, hl // LANES),
        in_specs=[head, head, shared],
        out_specs=head,
        out_shape=jax.ShapeDtypeStruct(q.shape, F32),
        scratch_shapes=[pltpu.VMEM((t, LANES), BF16), pltpu.VMEM((t, LANES), BF16)],
        compiler_params=_params("parallel", "parallel"),
    )(q, kv, kpe)


def _attn_bwd(q, kv, kpe, do, name):
    bsz, t, hl = q.shape
    head, shared = _attn_specs(t)

    def body(q_ref, kv_ref, kpe_ref, do_ref, dq_ref, dkv_ref, dkpe_ref, k_s, v_s, dk_s, dv_s):
        lane = lax.broadcasted_iota(jnp.int32, (t, LANES), 1)
        kvh = kv_ref[0]
        k_s[...] = jnp.where(lane < MLA_NOPE, kvh, kpe_ref[0]).astype(BF16)
        v_s[...] = kvh.astype(BF16)
        for start, rows in reversed(_query_blocks(t)):
            n = start + rows
            qb = q_ref[0, start:n, :]
            dob = jnp.where(lane[:rows] >= MLA_NOPE, do_ref[0, start:n, :], 0.0)
            kk, vv = k_s[:n, :], v_s[:n, :]
            e, inv_l = _attn_exp(qb, kk, start, _MLA_SCALE)
            p = e * inv_l
            dp = _dot(dob, vv, "nt")
            ds = p * (dp - jnp.sum(dp * p, axis=-1, keepdims=True)) * _MLA_SCALE
            dq_ref[0, start:n, :] = _dot(ds, kk, "nn")
            if n == t:
                dk_s[...] = _dot(ds, qb, "tn")
                dv_s[...] = _dot(p, dob, "tn")
            else:
                dk_s[:n, :] += _dot(ds, qb, "tn")
                dv_s[:n, :] += _dot(p, dob, "tn")
        dk = dk_s[...]
        dkv_ref[0] = jnp.where(lane < MLA_NOPE, dk, dv_s[...])
        dkpe = jnp.where(lane >= MLA_NOPE, dk, 0.0)

        @pl.when(pl.program_id(1) == 0)
        def _():
            dkpe_ref[0] = dkpe

        @pl.when(pl.program_id(1) != 0)
        def _():
            dkpe_ref[0] += dkpe

    return pl.pallas_call(
        body,
        name=name,
        grid=(bsz, hl // LANES),
        in_specs=[head, head, shared, head],
        out_specs=[head, head, shared],
        out_shape=[
            jax.ShapeDtypeStruct(q.shape, F32),
            jax.ShapeDtypeStruct(kv.shape, F32),
            jax.ShapeDtypeStruct(kpe.shape, F32),
        ],
        scratch_shapes=[pltpu.VMEM((t, LANES), BF16), pltpu.VMEM((t, LANES), BF16),
                        pltpu.VMEM((t, LANES), F32), pltpu.VMEM((t, LANES), F32)],
        compiler_params=_params("parallel", "arbitrary"),
    )(q, kv, kpe, do)


def _make_attention(name):
    @jax.custom_vjp
    def op(q, kv, kpe):
        return _attn_fwd(q, kv, kpe, name + "_fwd")

    def fwd(q, kv, kpe):
        return op(q, kv, kpe), (q, kv, kpe)

    def bwd(res, do):
        return tuple(_attn_bwd(*res, do, name + "_bwd"))

    op.defvjp(fwd, bwd)
    return op


_ROPE_SHIFT = MLA_ROPE // 2


def _rope_lanes_call(x, c, sm, sp, transpose, name):
    r, width = x.shape
    tm = _row_tile(r, width)

    def body(x_ref, c_ref, sm_ref, sp_ref, y_ref):
        cv, smv, spv = c_ref[...], sm_ref[...], sp_ref[...]
        for b in range(width // LANES):
            xb = x_ref[:, b * LANES:(b + 1) * LANES]
            if transpose:
                yb = xb * cv + pltpu.roll(xb * smv, _ROPE_SHIFT, 1) + pltpu.roll(xb * spv, LANES - _ROPE_SHIFT, 1)
            else:
                yb = xb * cv + pltpu.roll(xb, LANES - _ROPE_SHIFT, 1) * smv + pltpu.roll(xb, _ROPE_SHIFT, 1) * spv
            y_ref[:, b * LANES:(b + 1) * LANES] = yb

    tab = pl.BlockSpec((tm, LANES), lambda i: (i, 0))
    blk = pl.BlockSpec((tm, width), lambda i: (i, 0))
    return pl.pallas_call(
        body,
        name=name,
        grid=(r // tm,),
        in_specs=[blk, tab, tab, tab],
        out_specs=blk,
        out_shape=jax.ShapeDtypeStruct(x.shape, F32),
        compiler_params=_params("parallel"),
    )(x, c, sm, sp)


def _make_rope_lanes(name):
    @jax.custom_vjp
    def op(x, c, sm, sp):
        return _rope_lanes_call(x, c, sm, sp, False, name + "_fwd")

    def fwd(x, c, sm, sp):
        return op(x, c, sm, sp), (c, sm, sp)

    def bwd(res, dy):
        c, sm, sp = res
        return _rope_lanes_call(dy, c, sm, sp, True, name + "_bwd"), jnp.zeros_like(c), jnp.zeros_like(sm), jnp.zeros_like(sp)

    op.defvjp(fwd, bwd)
    return op


def _ret_chunk_rows(t):
    return t // 4 if t % 32 == 0 else SEQ_BLOCK


def _ret_decays(c, log_gamma):
    row = lax.broadcasted_iota(jnp.int32, (c, 1), 0)
    col = lax.broadcasted_iota(jnp.int32, (1, c), 1)
    rowf = row.astype(F32)
    d = jnp.where(row >= col, jnp.exp(log_gamma * rowf) * jnp.exp(-log_gamma * col.astype(F32)), 0.0)
    return d, jnp.exp(log_gamma * (rowf + 1.0)), jnp.exp(log_gamma * (c - 1.0 - rowf)), jnp.exp(log_gamma * c)


def _ret_specs(c, dk, dv, v_block0, n_chunks, reverse):
    pos = (lambda i: n_chunks - 1 - i) if reverse else (lambda i: i)
    return (
        pl.BlockSpec(memory_space=pltpu.SMEM),
        pl.BlockSpec((1, c, dk), lambda b, h, i: (b, pos(i), h)),
        pl.BlockSpec((1, c, dv), lambda b, h, i: (b, pos(i), h + v_block0)),
        pl.BlockSpec((1, c, dv), lambda b, h, i: (b, pos(i), h)),
        pl.BlockSpec((1, 1, dk, dv), lambda b, h, i: (b, h * n_chunks + pos(i), 0, 0)),
    )


def _ret_fwd(lg, q, k, v, name, dv=None, v_block0=0):
    bsz, t, hdk = q.shape
    heads = lg.shape[0]
    dk, dv = hdk // heads, dv or v.shape[2] // heads
    c = _ret_chunk_rows(t)
    n_chunks = t // c
    lg_spec, qk_spec, v_spec, o_spec, s_spec = _ret_specs(c, dk, dv, v_block0, n_chunks, False)

    def body(lg_ref, q_ref, k_ref, v_ref, o_ref, s_ref, state):
        @pl.when(pl.program_id(2) == 0)
        def _():
            state[...] = jnp.zeros((dk, dv), F32)

        d, a, b, g = _ret_decays(c, lg_ref[pl.program_id(1)])
        qb, kb, vb, s_in = q_ref[0], k_ref[0], v_ref[0], state[...]
        s_ref[0, 0] = s_in
        o_ref[0] = _dot(_dot(qb, kb, "nt") * d, vb, "nn") + a * _dot(qb, s_in, "nn")
        state[...] = g * s_in + _dot(kb * b, vb, "tn")

    return pl.pallas_call(
        body,
        name=name,
        grid=(bsz, heads, n_chunks),
        in_specs=[lg_spec, qk_spec, qk_spec, v_spec],
        out_specs=[o_spec, s_spec],
        out_shape=[jax.ShapeDtypeStruct((bsz, t, heads * dv), F32),
                   jax.ShapeDtypeStruct((bsz, heads * n_chunks, dk, dv), F32)],
        scratch_shapes=[pltpu.VMEM((dk, dv), F32)],
        compiler_params=_params("parallel", "parallel", "arbitrary"),
    )(lg, q, k, v)


def _ret_bwd(lg, q, k, v, states, do, name, v_block0=0, dv_dtype=F32):
    bsz, t, hdk = q.shape
    heads = lg.shape[0]
    dk, dv = hdk // heads, do.shape[2] // heads
    c = _ret_chunk_rows(t)
    n_chunks = t // c
    lg_spec, qk_spec, v_spec, o_spec, s_spec = _ret_specs(c, dk, dv, v_block0, n_chunks, True)

    def body(lg_ref, q_ref, k_ref, v_ref, s_ref, do_ref, dq_ref, dk_ref, dv_ref, dstate):
        @pl.when(pl.program_id(2) == 0)
        def _():
            dstate[...] = jnp.zeros((dk, dv), F32)

        d, a, b, g = _ret_decays(c, lg_ref[pl.program_id(1)])
        qb, kb, vb, dob, s_in, ds_out = q_ref[0], k_ref[0], v_ref[0], do_ref[0], s_ref[0, 0], dstate[...]
        scores = _dot(qb, kb, "nt") * d
        dscores = _dot(dob, vb, "nt") * d
        dq_ref[0] = _dot(dscores, kb, "nn") + a * _dot(dob, s_in, "nt")
        dk_ref[0] = _dot(dscores, qb, "tn") + b * _dot(vb, ds_out, "nt")
        dv_ref[0] = (_dot(scores, dob, "tn") + _dot(kb * b, ds_out, "nn")).astype(dv_dtype)
        dstate[...] = g * ds_out + _dot(qb, a * dob, "tn")

    return pl.pallas_call(
        body,
        name=name,
        grid=(bsz, heads, n_chunks),
        in_specs=[lg_spec, qk_spec, qk_spec, v_spec, s_spec, o_spec],
        out_specs=[qk_spec, qk_spec, o_spec],
        out_shape=[
            jax.ShapeDtypeStruct(q.shape, F32),
            jax.ShapeDtypeStruct(k.shape, F32),
            jax.ShapeDtypeStruct(do.shape, dv_dtype),
        ],
        scratch_shapes=[pltpu.VMEM((dk, dv), F32)],
        compiler_params=_params("parallel", "parallel", "arbitrary"),
    )(lg, q, k, v, states, do)


def _adamw(w, g, m, v, name):
    r, c = w.shape
    tr = _pick(r, (256, 128, 64, 32, 16, 8))

    def body(w_ref, g_ref, m_ref, v_ref, d_ref, nm_ref, nv_ref):
        gv = g_ref[...]
        nm = ADAM_B1 * m_ref[...] + (1.0 - ADAM_B1) * gv
        nv = ADAM_B2 * v_ref[...] + (1.0 - ADAM_B2) * (gv * gv)
        m_hat = nm / (1.0 - ADAM_B1 ** ADAM_STEP)
        v_hat = nv / (1.0 - ADAM_B2 ** ADAM_STEP)
        d_ref[...] = -ADAM_LR * (m_hat / (jnp.sqrt(v_hat) + ADAM_EPS) + ADAM_WD * w_ref[...])
        nm_ref[...] = nm
        nv_ref[...] = nv

    spec = pl.BlockSpec((tr, c), lambda i: (i, 0))
    return pl.pallas_call(
        body,
        name=name,
        grid=(r // tr,),
        in_specs=[spec] * 4,
        out_specs=[spec] * 3,
        out_shape=[jax.ShapeDtypeStruct((r, c), F32)] * 3,
        compiler_params=_params("parallel"),
    )(w, g, m, v)


def _rope_tables(t, half, reps):
    inv = ROPE_BASE ** (-jnp.arange(half, dtype=F32) / half)
    ang = jnp.arange(t, dtype=jnp.int32).astype(F32)[:, None] * inv[None, :]
    return jnp.tile(jnp.cos(ang), (1, reps)), jnp.tile(jnp.sin(ang), (1, reps))


def _padded_len(seq):
    return -(-(N_META + seq) // SEQ_BLOCK) * SEQ_BLOCK


def _embed(meta, x):
    bsz, seq, d = x.shape
    t = _padded_len(seq)
    return jnp.concatenate(
        [jnp.broadcast_to(meta[None], (bsz, N_META, d)), x, jnp.zeros((bsz, t - N_META - seq, d), F32)], axis=1
    ).reshape(bsz * t, d)


def _even_mixer(p, conv_w, conv_b, w_rg_a, b_rg_a, w_rg_x, b_rg_x, lru_lambda, q_norm_g, uq_slot, kv_norm_g,
                ukv_slot, gathered, bsz):
    w_uq_pad, w_ukv_full = gathered
    r = p.shape[0]
    t = r // bsz

    def tile_rows(tab):
        return jnp.tile(tab, (bsz, 1))

    lru_w = w_rg_a.shape[2] * w_rg_a.shape[1]
    q_rank, kv_rank = q_norm_g.shape[1], kv_norm_g.shape[1]
    p_gate, p_rec, p_q, p_kv, p_kpe = _split_cols(
        p, (lru_w, 2 * lru_w, 2 * lru_w + q_rank, 2 * lru_w + q_rank + kv_rank))

    xc = _make_conv("conv")(p_rec.reshape(bsz, t, lru_w), conv_w, conv_b).reshape(r, lru_w)
    ga, gx = _make_gate_pair("rg")(xc, w_rg_a[0], w_rg_x[0])
    (a, bb), _ = _make_rowwise(_lru_gates_f, "lru_gates", 3, 0, 3)((ga, gx, xc), (), (b_rg_a, b_rg_x, lru_lambda))
    hh = _make_scan("lru_scan")(a.reshape(bsz, t, lru_w), bb.reshape(bsz, t, lru_w)).reshape(r, lru_w)
    (y_rec,), _ = _make_rowwise(_lru_out_f, "lru_out", 2, 0, 0)((hh, p_gate), (), ())

    (qn,), _ = _make_rowwise(_rmsnorm_f, "q_norm", 1, 0, 1)((p_q,), (), (q_norm_g,))
    (kvn,), _ = _make_rowwise(_rmsnorm_f, "kv_norm", 1, 0, 1)((p_kv,), (), (kv_norm_g,))
    q = _make_slot_linear("ev_uq")(qn, w_uq_pad, uq_slot)
    kv = _make_slot_linear("ev_ukv")(kvn, w_ukv_full, ukv_slot)
    half = MLA_ROPE // 2
    cos, sin = _rope_tables(t, half, 1)
    one, zero = jnp.ones((t, MLA_NOPE), F32), jnp.zeros((t, MLA_NOPE), F32)
    tail = LANES - MLA_NOPE - MLA_ROPE
    c_tab = tile_rows(jnp.concatenate([one, cos, cos, one[:, :tail]], axis=1))
    sm_tab = tile_rows(jnp.concatenate([zero, -sin, zero[:, :half + tail]], axis=1))
    sp_tab = tile_rows(jnp.concatenate([zero, zero[:, :half], sin, zero[:, :tail]], axis=1))
    q = _make_rope_lanes("rope_q")(q, c_tab, sm_tab, sp_tab)
    kpe = _make_rope_lanes("rope_k")(p_kpe, c_tab, sm_tab, sp_tab)
    o = _make_attention("mla")(q.reshape(bsz, t, -1), kv.reshape(bsz, t, -1), kpe.reshape(bsz, t, LANES))
    return jnp.concatenate([y_rec, o.reshape(r, -1)], axis=1)


def _odd_mixer_fwd(p, bsz):
    r, width = p.shape
    t = r // bsz
    qk = width // 6
    dk = qk // RET_HEADS
    cos2, sin2 = _rope_tables(t, dk // 2, 2)
    sin2 = jnp.concatenate([-sin2[:, :dk // 2], sin2[:, dk // 2:]], axis=1)
    rope_args = (_Cols(p, qk, 0), _Cols(p, qk, 1), jnp.tile(cos2, (bsz, 1)), jnp.tile(sin2, (bsz, 1)))
    (rq, rk), _ = _make_rowwise(_rope_ret_f, "rope_ret", 2, 2, 0).fwd_call(*rope_args)
    lg = jnp.log(1.0 - 2.0 ** (-5.0 - jnp.arange(RET_HEADS, dtype=F32)))
    ret_args = (lg, rq.reshape(bsz, t, qk), rk.reshape(bsz, t, qk), p.reshape(bsz, t, width))
    o, states = _ret_fwd(*ret_args, "ret_fwd", dv=2 * dk, v_block0=qk // dk)
    gate_args = (o.reshape(r, 2 * qk), _Cols(p, 2 * qk, 2))
    (y,), _ = _make_rowwise(_ret_out_bf16_f, "ret_out", 2, 0, 0).fwd_call(*gate_args)
    return y, (rope_args, ret_args + (states,), gate_args)


def _odd_mixer_bwd(res, dy):
    rope_args, ret_args, gate_args = res
    bsz, t, qk = ret_args[1].shape
    dk = qk // RET_HEADS
    (do, dg), _ = _make_rowwise(_ret_out_f, "ret_out", 2, 0, 0).bwd_call(
        gate_args, ((dy,), ()), row_dtypes=(F32, BF16))
    drq, drk, dv = _ret_bwd(*ret_args, do.reshape(bsz, t, 2 * qk), "ret_bwd", v_block0=qk // dk, dv_dtype=BF16)
    (dq, dkk), _ = _make_rowwise(_rope_ret_f, "rope_ret", 2, 2, 0).bwd_call(
        rope_args, ((drq.reshape(bsz * t, qk), drk.reshape(bsz * t, qk)), ()), row_dtypes=(BF16, BF16))
    return jnp.concatenate([dq, dkk, dv.reshape(bsz * t, 2 * qk), dg], axis=1)


def _local_loss(h, target):
    bsz, seq, d = target.shape
    t = _padded_len(seq)
    t_real = N_META + seq
    pos = jnp.arange(t, dtype=jnp.int32)
    mask = jnp.tile(((pos >= N_META) & (pos < t_real)).astype(F32)[:, None], (bsz, 1))
    tgt = jnp.concatenate(
        [jnp.zeros((bsz, N_META, d), F32), target, jnp.zeros((bsz, t - t_real, d), F32)], axis=1).reshape(bsz * t, d)
    _, (total,) = _make_rowwise(_loss_f, "loss", 1, 2, 0)((h,), (tgt, mask), ())
    return total[0, 0]


_WEIGHTS = ("meta_tokens", "ev_w_in", "ev_conv_w", "ev_conv_b", "ev_w_rg_a", "ev_b_rg_a", "ev_w_rg_x", "ev_b_rg_x",
            "ev_lru_lambda", "ev_q_norm_g", "ev_w_uq", "ev_kv_norm_g", "ev_w_ukv", "ev_w_out", "od_w_in", "od_w_out",
            "ln_mix_g", "ln_mix_b", "mlp_w1", "mlp_w2", "ln_mlp_g", "ln_mlp_b")


def kernel(x, meta_tokens, ev_w_in, ev_conv_w, ev_conv_b, ev_w_rg_a, ev_b_rg_a, ev_w_rg_x, ev_b_rg_x, ev_lru_lambda, ev_q_norm_g, ev_w_uq, ev_kv_norm_g, ev_w_ukv, ev_w_out, od_w_in, od_w_out, ln_mix_g, ln_mix_b, mlp_w1, mlp_w2, ln_mlp_g, ln_mlp_b, loss_target, m_meta_tokens, m_ev_w_in, m_ev_conv_w, m_ev_conv_b, m_ev_w_rg_a, m_ev_b_rg_a, m_ev_w_rg_x, m_ev_b_rg_x, m_ev_lru_lambda, m_ev_q_norm_g, m_ev_w_uq, m_ev_kv_norm_g, m_ev_w_ukv, m_ev_w_out, m_od_w_in, m_od_w_out, m_ln_mix_g, m_ln_mix_b, m_mlp_w1, m_mlp_w2, m_ln_mlp_g, m_ln_mlp_b, v_meta_tokens, v_ev_w_in, v_ev_conv_w, v_ev_conv_b, v_ev_w_rg_a, v_ev_b_rg_a, v_ev_w_rg_x, v_ev_b_rg_x, v_ev_lru_lambda, v_ev_q_norm_g, v_ev_w_uq, v_ev_kv_norm_g, v_ev_w_ukv, v_ev_w_out, v_od_w_in, v_od_w_out, v_ln_mix_g, v_ln_mix_b, v_mlp_w1, v_mlp_w2, v_ln_mlp_g, v_ln_mlp_b):
    args = locals()
    weights = {n: args[n] for n in _WEIGHTS}
    bsz = x.shape[0]
    my_x, my_y, my_c = _my_place()
    me = 4 * my_x + 2 * my_y + my_c

    big = (("ev_in", ev_w_in[0], True), ("ev_out", ev_w_out[0], False), ("mlp0_w1", mlp_w1[0], True),
           ("mlp0_w2", mlp_w2[0], False), ("od_in", od_w_in[0], True), ("od_out", od_w_out[0], False),
           ("mlp1_w1", mlp_w1[1], True), ("mlp1_w2", mlp_w2[1], False))
    small_sharded = (("meta", meta_tokens, F32), ("conv_w", ev_conv_w[0], F32), ("ev_uq", ev_w_uq[0], BF16),
                     ("ev_ukv", ev_w_ukv[0], BF16))
    to_gather = (tuple((nm, s.astype(dt), True) for nm, s, dt in small_sharded)
                 + tuple((nm, s.astype(BF16), cols) for nm, s, cols in big))
    handles = _gather_start_all([s for _, s, _ in to_gather], "ag_start")
    gathers = {nm: (s, cols, h) for (nm, s, cols), h in zip(to_gather, handles)}
    gather_tokens = (handles[0][4],)

    def full_weight(nm, after):
        shard16, cols, handle = gathers[nm]
        land = _exchange_wait(handle, True, after, "ag_wait_" + nm)
        land = lax.dynamic_update_index_in_dim(land, shard16, me, 0)
        if cols and shard16.shape[1] % LANES == 0:
            return land, True
        return (_unstack_cols(land) if cols else land.reshape(-1, shard16.shape[1])), False

    meta_full, conv_w_full, w_uq_full, w_ukv_full = (
        _unstack_cols(lax.dynamic_update_index_in_dim(
            _exchange_wait(gathers[nm][2], True, gather_tokens[-1], "ag_wait_" + nm), gathers[nm][0], me, 0))
        for nm, _, _ in small_sharded)

    pending = []

    def linear_bwd(nm, x_in, w, dy, cols, unpad=None, **fused):
        w_full, w_stacked = w
        own = None
        if w_stacked:
            stacked = _matmul(x_in, dy, "tn", nm + "_dw", stacked=True)
        else:
            dw = _matmul(x_in, dy, "tn", nm + "_dw")
            dw = dw if unpad is None else unpad(dw)
            n = dw.shape[1] // N_DEV
            if cols:
                stacked = _stack_cols(dw)
                own = lax.dynamic_slice_in_dim(dw, me * n, n, axis=1)
            else:
                stacked = dw.reshape(N_DEV, dw.shape[0] // N_DEV, dw.shape[1])
        handle = _exchange_start(stacked, (_N_PEERS,) + stacked.shape[1:], False, "rs_start_" + nm)
        if own is None:
            own = lax.dynamic_index_in_dim(handle[2], me, 0, keepdims=False)
        pending.append((nm, own, handle))
        return _matmul(dy, w_full, "nt", nm + "_dx", after=(handle[4],), stacked=w_stacked, **fused)

    def linear_fwd(nm, x_in, w, **fused):
        return _matmul(x_in, w[0], "nn", nm + "_fwd", stacked=w[1], **fused)

    def mlp_fwd(h, h16, l):
        w1 = full_weight(f"mlp{l}_w1", h16)
        a16 = linear_fwd(f"mlp{l}_w1", h16, w1, out_relu2=True, out_dtype=BF16)
        w2 = full_weight(f"mlp{l}_w2", a16)
        f = linear_fwd(f"mlp{l}_w2", a16, w2)
        ln_args = (h, f, ln_mlp_g[l:l + 1], ln_mlp_b[l:l + 1])
        return ln_fwd(f"mlp{l}_ln", *ln_args), (h16, w1, a16, w2, ln_args)

    def mlp_bwd(dout, res, l):
        h16, w1, a16, w2, ln_args = res
        dh, df, dg, db = ln_bwd(f"mlp{l}_ln", ln_args, dout)
        du = linear_bwd(f"mlp{l}_w2", a16, w2, df, False, relu2_bwd_of=a16, out_dtype=BF16)
        return (dh, linear_bwd(f"mlp{l}_w1", h16, w1, du, True)), dg, db

    def ln_fwd(nm, h, mix, g, b):
        return _make_rowwise(_ln_res_copy_f, nm, 2, 0, 2).fwd_call(h, mix, g, b)[0]

    def ln_bwd(nm, ln_args, pieces):
        (dh, dmix), (dg, db) = _make_rowwise(_ln_res_f, nm, 2, 0, 2).bwd_call(
            ln_args, ((pieces[0],), ()), more=tuple(pieces[1:]), row_dtypes=(F32, BF16))
        return dh, dmix, dg, db

    h0, vjp_embed = jax.vjp(_embed, meta_full, x)
    n_in = ev_w_in.shape[2] * N_DEV
    kpe0, pad_lo, pad_hi = n_in - MLA_ROPE, MLA_NOPE, LANES - MLA_NOPE - MLA_ROPE
    w_in = full_weight("ev_in", h0)[0]
    zeros_in = jnp.zeros((w_in.shape[0], pad_lo), BF16)
    w_ev_in = (jnp.concatenate([w_in[:, :kpe0], zeros_in, w_in[:, kpe0:], zeros_in[:, :pad_hi]], axis=1), False)

    def unpad_in(dw):
        return jnp.concatenate([dw[:, :kpe0], dw[:, kpe0 + pad_lo:kpe0 + pad_lo + MLA_ROPE]], axis=1)

    p0 = _matmul(h0, w_ev_in[0], "nn", "ev_in_fwd", after=gather_tokens)
    q_rank, d_head = w_uq_full.shape[0], MLA_NOPE + MLA_ROPE
    w_uq_pad = jnp.pad(w_uq_full.reshape(q_rank, MLA_HEADS, d_head), ((0, 0), (0, 0), (0, LANES - d_head)))
    w_uq_pad = w_uq_pad.reshape(q_rank, MLA_HEADS * LANES)
    small = (conv_w_full, ev_conv_b, ev_w_rg_a, ev_b_rg_a, ev_w_rg_x, ev_b_rg_x, ev_lru_lambda, ev_q_norm_g,
             jnp.zeros(w_uq_pad.shape, F32), ev_kv_norm_g, jnp.zeros(w_ukv_full.shape, F32))
    y0, vjp_even = jax.vjp(lambda p, *s: _even_mixer(p, *s, (w_uq_pad, w_ukv_full), bsz), p0, *small)
    w_out = full_weight("ev_out", y0)[0]
    lru_w, d_model = y0.shape[1] - MLA_HEADS * LANES, w_out.shape[1]
    w_att = w_out[lru_w:].reshape(MLA_HEADS, MLA_V, d_model)
    w_att = jnp.concatenate([jnp.zeros((MLA_HEADS, LANES - MLA_V, d_model), BF16), w_att], axis=1)
    w_ev_out = (jnp.concatenate([w_out[:lru_w], w_att.reshape(MLA_HEADS * LANES, d_model)], axis=0), False)

    def unpad_out(dw):
        d_att = dw[lru_w:].reshape(MLA_HEADS, LANES, d_model)[:, LANES - MLA_V:].reshape(MLA_HEADS * MLA_V, d_model)
        return jnp.concatenate([dw[:lru_w], d_att], axis=0)

    mix0 = linear_fwd("ev_out", y0, w_ev_out)
    ln0_args = (h0, mix0, ln_mix_g[0:1], ln_mix_b[0:1])
    h1, h1_16 = ln_fwd("mix0_ln", *ln0_args)
    (h2, h2_16), res_mlp0 = mlp_fwd(h1, h1_16, 0)
    w_od_in = full_weight("od_in", h2_16)
    p1 = linear_fwd("od_in", h2_16, w_od_in)
    y1, res_odd = _odd_mixer_fwd(p1, bsz)
    w_od_out = full_weight("od_out", y1)
    mix1 = linear_fwd("od_out", y1, w_od_out)
    ln1_args = (h2, mix1, ln_mix_g[1:2], ln_mix_b[1:2])
    h3, h3_16 = ln_fwd("mix1_ln", *ln1_args)
    (h4, _), res_mlp1 = mlp_fwd(h3, h3_16, 1)
    loss_local, vjp_loss = jax.vjp(lambda h: _local_loss(h, loss_target), h4)

    dh4 = vjp_loss(jnp.ones((), F32))
    dh3, dg_mlp1, db_mlp1 = mlp_bwd(dh4, res_mlp1, 1)
    dh2, dmix1, dg_mix1, db_mix1 = ln_bwd("mix1_ln", ln1_args, dh3)
    dp1 = _odd_mixer_bwd(res_odd, linear_bwd("od_out", y1, w_od_out, dmix1, False))
    dh2 = (dh2, linear_bwd("od_in", h2_16, w_od_in, dp1, True))
    dh1, dg_mlp0, db_mlp0 = mlp_bwd(dh2, res_mlp0, 0)
    dh0, dmix0, dg_mix0, db_mix0 = ln_bwd("mix0_ln", ln0_args, dh1)
    dp0, *dsmall = vjp_even(linear_bwd("ev_out", y0, w_ev_out, dmix0, False, unpad=unpad_out))
    dh0 = dh0 + linear_bwd("ev_in", h0, w_ev_in, dp0.astype(BF16), True, unpad=unpad_in)
    g_meta_full, grad_x = vjp_embed(dh0)
    (g_conv_w_full, g_conv_b, g_w_rg_a, g_b_rg_a, g_w_rg_x, g_b_rg_x, g_lambda, g_q_norm, g_uq_pad, g_kv_norm,
     g_ukv_full) = dsmall

    for nm, dw in (("ev_uq", g_uq_pad.reshape(q_rank, MLA_HEADS, LANES)[:, :, :d_head].reshape(q_rank, -1)),
                   ("ev_ukv", g_ukv_full)):
        n = dw.shape[1] // N_DEV
        handle = _exchange_start(_stack_cols(dw), (_N_PEERS, dw.shape[0], n), False, "rs_start_" + nm)
        pending.append((nm, lax.dynamic_slice_in_dim(dw, me * n, n, axis=1), handle))

    rep_names = ("ev_conv_b", "ev_w_rg_a", "ev_b_rg_a", "ev_w_rg_x", "ev_b_rg_x", "ev_lru_lambda", "ev_q_norm_g",
                 "ev_kv_norm_g", "ln_mix_g", "ln_mix_b", "ln_mlp_g", "ln_mlp_b")
    rep_local = (g_conv_b, g_w_rg_a, g_b_rg_a, g_w_rg_x, g_b_rg_x, g_lambda, g_q_norm, g_kv_norm,
                 jnp.concatenate([dg_mix0, dg_mix1]), jnp.concatenate([db_mix0, db_mix1]),
                 jnp.concatenate([dg_mlp0, dg_mlp1]), jnp.concatenate([db_mlp0, db_mlp1]), g_conv_w_full, g_meta_full)
    rep_stacked = _pack_rows(rep_local).reshape(N_DEV, -1, LANES)
    rep_rs = _exchange_start(rep_stacked, (_N_PEERS,) + rep_stacked.shape[1:], False, "rep_rs_start", after=(grad_x,))
    rep_own = lax.dynamic_index_in_dim(rep_rs[2], me, 0, keepdims=False)

    after, summed = rep_rs[4], {}
    for nm, own, handle in pending:
        land = _exchange_wait(handle, False, after, "rs_wait_" + nm)
        summed[nm] = after = _sum_own_and_peers(own, land, "rs_sum_" + nm)
    rep_part = _sum_own_and_peers(rep_own, _exchange_wait(rep_rs, False, after, "rep_rs_wait"), "rep_sum")
    rep_ag = _exchange_start(rep_part, (N_DEV,) + rep_part.shape, True, "rep_ag_start")

    grad_w = dict(ev_w_uq=summed["ev_uq"][None], ev_w_ukv=summed["ev_ukv"][None],
                  ev_w_in=summed["ev_in"][None], ev_w_out=summed["ev_out"][None], od_w_in=summed["od_in"][None],
                  od_w_out=summed["od_out"][None], mlp_w1=jnp.stack([summed["mlp0_w1"], summed["mlp1_w1"]]),
                  mlp_w2=jnp.stack([summed["mlp0_w2"], summed["mlp1_w2"]]))

    loss = lax.psum(loss_local, MESH_AXES)
    delta, new_m, new_v = {}, {}, {}

    def adamw(n):
        w, g, m, v = weights[n], grad_w[n], args["m_" + n], args["v_" + n]
        two_d = (-1, w.shape[-1])
        d2, m2, v2 = _adamw(w.reshape(two_d), g.reshape(two_d), m.reshape(two_d), v.reshape(two_d), "adamw_" + n)
        delta[n], new_m[n], new_v[n] = d2.reshape(w.shape), m2.reshape(w.shape), v2.reshape(w.shape)
        return d2

    for n in tuple(grad_w):
        after = adamw(n)
    rep_land = lax.dynamic_update_index_in_dim(_exchange_wait(rep_ag, True, after, "rep_ag_wait"), rep_part, me, 0)
    *rep_total, t_conv_w, t_meta = _unpack_rows(rep_land, rep_local)
    n_conv, n_meta = ev_conv_w.shape[2], meta_tokens.shape[1]
    small_g = dict(zip(rep_names, rep_total))
    small_g.update(meta_tokens=lax.dynamic_slice_in_dim(t_meta, me * n_meta, n_meta, axis=1),
                   ev_conv_w=lax.dynamic_slice_in_dim(t_conv_w, me * n_conv, n_conv, axis=1)[None])
    grad_w.update(small_g)
    for n in small_g:
        adamw(n)
    return (loss, grad_x, *[grad_w[n] for n in _WEIGHTS], *[delta[n] for n in _WEIGHTS],
            *[new_m[n] for n in _WEIGHTS], *[new_v[n] for n in _WEIGHTS])
```

```python
import math

import jax
import jax.numpy as jnp
from jax import lax
from jax.experimental import pallas as pl
from jax.experimental.pallas import tpu as pltpu

F32 = jnp.float32
BF16 = jnp.bfloat16

N_DEV = 8
MESH_AXES = ("x", "y", "c")
LANES = 128
SEQ_BLOCK = 128

N_META = 16
LRU_C = 8.0
MLA_HEADS = 8
MLA_NOPE = 64
MLA_ROPE = 32
MLA_V = 64
RET_HEADS = 4
ROPE_BASE = 10000.0
DEPTH = 2
DN_ALPHA = (2 * DEPTH) ** 0.25
EPS = 1e-5
NEG_INF = -1e30

ADAM_LR = 0.001
ADAM_B1 = 0.9
ADAM_B2 = 0.999
ADAM_EPS = 1e-08
ADAM_WD = 0.01
ADAM_STEP = 10

VMEM_LIMIT = 56 * 1024 * 1024


def _params(*sem):
    return pltpu.CompilerParams(dimension_semantics=sem, vmem_limit_bytes=VMEM_LIMIT)


def _pick(n, cands):
    for c in cands:
        if n % c == 0:
            return c
    return n


def _row_tile(r, width):
    cands = (256, 128, 64, 32, 16, 8) if width <= 1024 else (128, 64, 32, 16, 8)
    return _pick(r, cands)


_DIMS = {"nn": (((1,), (0,)), ((), ())), "nt": (((1,), (1,)), ((), ())), "tn": (((0,), (0,)), ((), ()))}


def _dot(a, b, mode):
    return lax.dot_general(a.astype(BF16), b.astype(BF16), _DIMS[mode], preferred_element_type=F32)


def _matmul(a, b, mode, name, after=(), stacked=False, relu2_bwd_of=None, out_dtype=F32, out_relu2=False):
    if stacked:
        n_blk = b.shape[2] if mode != "tn" else b.shape[1] // N_DEV
    if mode == "nn":
        (m, k), n = a.shape, (N_DEV * n_blk if stacked else b.shape[1])
    elif mode == "nt":
        (m, k), n = a.shape, (b.shape[1] if stacked else b.shape[0])
    else:
        (k, m), n = a.shape, b.shape[1]
    tm = _pick(m, (2176, 1088, 1024, 544, 512, 272, 256, 128, 64, 32, 16, 8))
    tn = _pick(n, (1024, 512, 256, 128))
    tk = _pick(k, (2176, 1088, 1024, 544, 512, 272, 256, 128))
    kb = 2
    if stacked and mode == "nn":
        tn = n_blk
    if stacked and mode == "tn":
        tn = kb * n_blk
    if stacked and mode == "nt":
        tk = kb * n_blk
    nk = k // tk
    assert out_dtype == F32 or (nk == 1 and not (stacked and mode == "tn")), "narrow results need a single k step"
    assert not out_relu2 or nk == 1, "relu^2 is applied to a finished tile"

    out_spec = pl.BlockSpec((tm, tn), lambda i, j, kk: (i, j))
    out_shape = jax.ShapeDtypeStruct((m, n), out_dtype)
    if mode == "nn":
        a_spec = pl.BlockSpec((tm, tk), lambda i, j, kk: (i, kk))
        b_spec = pl.BlockSpec((tk, tn), lambda i, j, kk: (kk, j))
        if stacked:
            b_spec = pl.BlockSpec((None, tk, tn), lambda i, j, kk: (j, kk, 0))
    elif mode == "nt":
        a_spec = pl.BlockSpec((tm, tk), lambda i, j, kk: (i, kk))
        b_spec = pl.BlockSpec((tn, tk), lambda i, j, kk: (j, kk))
        if stacked:
            b_spec = pl.BlockSpec((kb, tn, n_blk), lambda i, j, kk: (kk, j, 0))
    else:
        a_spec = pl.BlockSpec((tk, tm), lambda i, j, kk: (kk, i))
        b_spec = pl.BlockSpec((tk, tn), lambda i, j, kk: (kk, j))
        if stacked:
            out_spec = pl.BlockSpec((kb, tm, n_blk), lambda i, j, kk: (j, i, 0))
            out_shape = jax.ShapeDtypeStruct((N_DEV, m, n_blk), F32)
    extra = [] if relu2_bwd_of is None else [relu2_bwd_of]
    extra_specs = [pl.BlockSpec((tm, tn), lambda i, j, kk: (i, j))] * len(extra)

    def body(a_ref, b_ref, *rest):
        def relu2_slope():
            return 2.0 * jnp.sqrt(rest[0][...].astype(F32))

        o_ref = rest[-1]
        kk = pl.program_id(2)
        av = a_ref[...]
        if stacked and mode == "nt":
            part = _dot(av[:, :n_blk], b_ref[0], mode)
            for q in range(1, kb):
                part = part + _dot(av[:, q * n_blk:(q + 1) * n_blk], b_ref[q], mode)
        else:
            part = _dot(av, b_ref[...], mode)
        if stacked and mode == "tn":
            part = jnp.stack([part[:, q * n_blk:(q + 1) * n_blk] for q in range(kb)])
        if nk == 1:
            if out_relu2:
                part = jnp.maximum(part, 0.0)
                part = part * part
            if relu2_bwd_of is not None:
                part = part * relu2_slope()
            o_ref[...] = part.astype(out_dtype)
            return

        @pl.when(kk == 0)
        def _():
            o_ref[...] = part

        @pl.when(kk != 0)
        def _():
            o_ref[...] += part

        if relu2_bwd_of is not None:
            @pl.when(kk == nk - 1)
            def _():
                o_ref[...] *= relu2_slope()

    return pl.pallas_call(
        body,
        name=name,
        grid=(m // tm, n // tn, nk),
        in_specs=[a_spec, b_spec] + extra_specs + [pl.BlockSpec(memory_space=pl.ANY)] * len(after),
        out_specs=out_spec,
        out_shape=out_shape,
        compiler_params=_params("parallel", "parallel", "arbitrary"),
    )(a, b, *extra, *after)


def _make_gate_pair(name):
    def call(body, ins, n_out, reduce_rows, tag):
        x = ins[0]
        m, d = x.shape[0], LANES
        g = x.shape[1] // d
        tm = _pick(m, (1088, 1024, 544, 512, 272, 256, 128, 64, 32, 16, 8))
        rows = pl.BlockSpec((tm, d), lambda h, i: (i, h))
        mats = pl.BlockSpec((1, d, d), lambda h, i: (h, 0, 0))
        return pl.pallas_call(
            body,
            name=name + tag,
            grid=(g, m // tm),
            in_specs=[rows if a.ndim == 2 else mats for a in ins],
            out_specs=[mats if reduce_rows else rows] * n_out,
            out_shape=[jax.ShapeDtypeStruct((g, d, d) if reduce_rows else (m, g * d), F32)] * n_out,
            compiler_params=_params("parallel", "arbitrary" if reduce_rows else "parallel"),
        )(*ins)

    def fwd_body(x_ref, wa_ref, wx_ref, a_ref, b_ref):
        xv = x_ref[...]
        a_ref[...] = _dot(xv, wa_ref[0], "nn")
        b_ref[...] = _dot(xv, wx_ref[0], "nn")

    def dx_body(da_ref, db_ref, wa_ref, wx_ref, dx_ref):
        dx_ref[...] = _dot(da_ref[...], wa_ref[0], "nt") + _dot(db_ref[...], wx_ref[0], "nt")

    def dw_body(x_ref, da_ref, db_ref, dwa_ref, dwx_ref):
        xv = x_ref[...]
        pa, pb = _dot(xv, da_ref[...], "tn"), _dot(xv, db_ref[...], "tn")

        @pl.when(pl.program_id(1) == 0)
        def _():
            dwa_ref[0] = pa
            dwx_ref[0] = pb

        @pl.when(pl.program_id(1) != 0)
        def _():
            dwa_ref[0] += pa
            dwx_ref[0] += pb

    @jax.custom_vjp
    def op(x, wa, wx):
        return tuple(call(fwd_body, (x, wa, wx), 2, False, "_fwd"))

    def fwd(x, wa, wx):
        return op(x, wa, wx), (x, wa, wx)

    def bwd(res, cots):
        x, wa, wx = res
        da, db = cots
        (dx,) = call(dx_body, (da, db, wa, wx), 1, False, "_dx")
        dwa, dwx = call(dw_body, (x, da, db), 2, True, "_dw")
        return dx, dwa, dwx

    op.defvjp(fwd, bwd)
    return op


def _my_place():
    return lax.axis_index("x"), lax.axis_index("y"), lax.axis_index("c")


def _stack_cols(full):
    k, n8 = full.shape
    return full.reshape(k, N_DEV, n8 // N_DEV).transpose(1, 0, 2)


def _unstack_cols(stacked):
    j, k, n = stacked.shape
    return stacked.transpose(1, 0, 2).reshape(k, j * n)


def _split_cols(p, cuts):
    bounds = (0,) + tuple(cuts) + (p.shape[1],)

    @jax.custom_vjp
    def op(z):
        return tuple(z[:, lo:hi] for lo, hi in zip(bounds[:-1], bounds[1:]))

    op.defvjp(lambda z: (op(z), None), lambda _, cots: (jnp.concatenate(cots, axis=1),))
    return op(p)


def _make_slot_linear(name):
    @jax.custom_vjp
    def op(x, w_full, slot):
        return _matmul(x, w_full, "nn", name + "_fwd")

    def fwd(x, w_full, slot):
        return op(x, w_full, slot), (x, w_full)

    def bwd(res, dy):
        x, w = res
        return _matmul(dy, w, "nt", name + "_dx"), jnp.zeros_like(w), _matmul(x, dy, "tn", name + "_dw")

    op.defvjp(fwd, bwd)
    return op


def _pack_rows(gs):
    flat = jnp.concatenate([g.reshape(-1) for g in gs])
    n = flat.shape[0]
    rows = -(-n // (256 * LANES)) * 256
    return jnp.pad(flat, (0, rows * LANES - n)).reshape(rows, LANES)


def _unpack_rows(packed, like):
    flat, out, off = packed.reshape(-1), [], 0
    for g in like:
        out.append(flat[off:off + g.size].reshape(g.shape))
        off += g.size
    return out


_HBM = pl.BlockSpec(memory_space=pltpu.HBM)
_SEM = pl.BlockSpec(memory_space=pltpu.SEMAPHORE)
_SIDE_EFFECT = pltpu.SideEffectType.DATAFLOW_SIDE_EFFECTING
_N_PEERS = N_DEV - 1


def _peer(k):
    x, y, c = _my_place()
    return x ^ ((k >> 2) & 1), y ^ ((k >> 1) & 1), c ^ (k & 1)


def _exchange_start(src, land_shape, gather, name, after=()):
    def body(src_ref, land_ref, *rest):
        send_sems, recv_sems, src_thru, land_thru, token = rest[len(after):]
        x, y, c = _my_place()
        me = 4 * x + 2 * y + c
        for k in range(1, N_DEV):
            px, py, pc = _peer(k)
            pltpu.make_async_remote_copy(
                src_ref=src_ref if gather else src_ref.at[4 * px + 2 * py + pc],
                dst_ref=land_ref.at[me] if gather else land_ref.at[k - 1],
                send_sem=send_sems.at[k - 1],
                recv_sem=recv_sems.at[k - 1],
                device_id=(px, py, pc),
                device_id_type=pl.DeviceIdType.MESH,
            ).start()
        token[...] = jnp.zeros_like(token)

    return pl.pallas_call(
        body,
        name=name,
        out_shape=(
            pltpu.SemaphoreType.DMA((_N_PEERS,)),
            pltpu.SemaphoreType.DMA((_N_PEERS,)),
            pltpu.HBM(src.shape, src.dtype),
            pltpu.HBM(land_shape, src.dtype),
            jax.ShapeDtypeStruct((8, LANES), F32),
        ),
        in_specs=(_HBM, _HBM) + (pl.BlockSpec(memory_space=pl.ANY),) * len(after),
        out_specs=(_SEM, _SEM, _HBM, _HBM, pl.BlockSpec(memory_space=pltpu.VMEM)),
        input_output_aliases={0: 2, 1: 3},
        compiler_params=pltpu.CompilerParams(has_side_effects=_SIDE_EFFECT),
    )(pltpu.with_memory_space_constraint(src, pltpu.HBM),
      pltpu.with_memory_space_constraint(lax.empty(land_shape, src.dtype), pltpu.HBM), *after)


def _gather_start_all(shards, name):
    n = len(shards)

    def body(*refs):
        srcs, lands = refs[:n], refs[n:2 * n]
        outs = refs[2 * n:]
        send_sems, recv_sems, token = outs[:n], outs[n:2 * n], outs[-1]
        x, y, c = _my_place()
        me = 4 * x + 2 * y + c
        for i in range(n):
            for k in range(1, N_DEV):
                pltpu.make_async_remote_copy(
                    src_ref=srcs[i],
                    dst_ref=lands[i].at[me],
                    send_sem=send_sems[i].at[k - 1],
                    recv_sem=recv_sems[i].at[k - 1],
                    device_id=_peer(k),
                    device_id_type=pl.DeviceIdType.MESH,
                ).start()
        token[...] = jnp.zeros_like(token)

    lands = [(N_DEV,) + s.shape for s in shards]
    sems = tuple(pltpu.SemaphoreType.DMA((_N_PEERS,)) for _ in range(2 * n))
    res = pl.pallas_call(
        body,
        name=name,
        out_shape=sems + tuple(pltpu.HBM(s.shape, s.dtype) for s in shards)
        + tuple(pltpu.HBM(ls, s.dtype) for ls, s in zip(lands, shards)) + (jax.ShapeDtypeStruct((8, LANES), F32),),
        in_specs=(_HBM,) * (2 * n),
        out_specs=(_SEM,) * (2 * n) + (_HBM,) * (2 * n) + (pl.BlockSpec(memory_space=pltpu.VMEM),),
        input_output_aliases={i: 2 * n + i for i in range(2 * n)},
        compiler_params=pltpu.CompilerParams(has_side_effects=_SIDE_EFFECT),
    )(*[pltpu.with_memory_space_constraint(s, pltpu.HBM) for s in shards],
      *[pltpu.with_memory_space_constraint(lax.empty(ls, s.dtype), pltpu.HBM) for ls, s in zip(lands, shards)])
    return [(res[i], res[n + i], res[2 * n + i], res[3 * n + i], res[-1]) for i in range(n)]


def _exchange_wait(handle, gather, after, name):
    send_sems, recv_sems, src_thru, land_thru, _ = handle

    def body(src_ref, land_ref, send_sems, recv_sems, after_ref, src_dead, got_ref):
        for k in range(1, N_DEV):
            cp = pltpu.make_async_remote_copy(
                src_ref=src_ref if gather else src_ref.at[k],
                dst_ref=land_ref.at[k - 1],
                send_sem=send_sems.at[k - 1],
                recv_sem=recv_sems.at[k - 1],
                device_id=_peer(k),
                device_id_type=pl.DeviceIdType.MESH,
            )
            cp.wait_send()
            cp.wait_recv()

    return pl.pallas_call(
        body,
        name=name,
        out_shape=(pltpu.HBM(src_thru.shape, src_thru.dtype), pltpu.HBM(land_thru.shape, land_thru.dtype)),
        in_specs=(_HBM, _HBM, _SEM, _SEM, pl.BlockSpec(memory_space=pl.ANY)),
        out_specs=(_HBM, _HBM),
        input_output_aliases={0: 0, 1: 1},
        compiler_params=pltpu.CompilerParams(has_side_effects=_SIDE_EFFECT),
    )(src_thru, land_thru, send_sems, recv_sems, after)[1]


def _sum_own_and_peers(own, land, name):
    r, c = own.shape
    tr = _pick(r, (256, 128, 64, 32, 16, 8))

    def body(o_ref, l_ref, out_ref):
        s = [l_ref[j] for j in range(_N_PEERS)]
        out_ref[...] = ((o_ref[...] + s[0]) + (s[1] + s[2])) + ((s[3] + s[4]) + (s[5] + s[6]))

    return pl.pallas_call(
        body,
        name=name,
        grid=(r // tr,),
        in_specs=[pl.BlockSpec((tr, c), lambda i: (i, 0)), pl.BlockSpec((_N_PEERS, tr, c), lambda i: (0, i, 0))],
        out_specs=pl.BlockSpec((tr, c), lambda i: (i, 0)),
        out_shape=jax.ShapeDtypeStruct((r, c), own.dtype),
        compiler_params=_params("parallel"),
    )(own, land)


class _Cols:
    def __init__(self, array, width, block):
        self.array, self.width, self.block = array, width, block
        self.shape, self.dtype = (array.shape[0], width), array.dtype


def _base(a):
    return a.array if isinstance(a, _Cols) else a


def _col_block(a):
    return a.block if isinstance(a, _Cols) else 0


def _make_rowwise(f, name, n_rows, n_tabs, n_pars):
    n_in = n_rows + n_tabs + n_pars

    def specs(args, tm):
        blocked = [pl.BlockSpec((tm, a.shape[1]), lambda i, blk=_col_block(a): (i, blk)) for a in args[: n_rows + n_tabs]]
        whole = [pl.BlockSpec(a.shape, lambda i: (0, 0)) for a in args[n_rows + n_tabs:]]
        return blocked + whole

    def out_struct(args, tm):
        blk = [jax.ShapeDtypeStruct((tm, a.shape[1]), a.dtype) for a in args[: n_rows + n_tabs]]
        blk += [jax.ShapeDtypeStruct(a.shape, a.dtype) for a in args[n_rows + n_tabs:]]
        return jax.eval_shape(f, *blk)

    def fwd_call(*args):
        r = args[0].shape[0]
        tm = _row_tile(r, max(a.shape[1] for a in args[:n_rows]))
        ro, so = out_struct(args, tm)

        def body(*refs):
            vals = [x[...] for x in refs[:n_in]]
            outs = refs[n_in:]
            rv, sv = f(*vals)
            for o, v in zip(outs[: len(ro)], rv):
                o[...] = v
            for o, v in zip(outs[len(ro):], sv):
                @pl.when(pl.program_id(0) == 0)
                def _(o=o, v=v):
                    o[...] = v

                @pl.when(pl.program_id(0) != 0)
                def _(o=o, v=v):
                    o[...] += v

        out_shape = [jax.ShapeDtypeStruct((r, s.shape[1]), s.dtype) for s in ro]
        out_shape += [jax.ShapeDtypeStruct(s.shape, s.dtype) for s in so]
        out_specs = [pl.BlockSpec((tm, s.shape[1]), lambda i: (i, 0)) for s in ro]
        out_specs += [pl.BlockSpec(s.shape, lambda i: (0, 0)) for s in so]
        res = pl.pallas_call(
            body,
            name=name + "_fwd",
            grid=(r // tm,),
            in_specs=specs(args, tm),
            out_specs=out_specs,
            out_shape=out_shape,
            compiler_params=_params("arbitrary" if so else "parallel"),
        )(*[_base(a) for a in args])
        return tuple(res[: len(ro)]), tuple(res[len(ro):])

    def bwd_call(args, cots, more=(), row_dtypes=None):
        r = args[0].shape[0]
        tm = _row_tile(r, max(a.shape[1] for a in args[:n_rows]))
        ro, so = out_struct(args, tm)
        crow, csum = cots
        rows, tabs, pars = args[:n_rows], args[n_rows:n_rows + n_tabs], args[n_rows + n_tabs:]
        n_c = len(crow) + len(csum)

        def body(*refs):
            vals = [x[...] for x in refs[:n_in]]
            cv = [x[...] for x in refs[n_in:n_in + n_c]]
            for x in refs[n_in + n_c:n_in + n_c + len(more)]:
                cv[0] = cv[0] + x[...]
            outs = refs[n_in + n_c + len(more):]
            tv = vals[n_rows:n_rows + n_tabs]

            def g(*dargs):
                return f(*dargs[:n_rows], *tv, *dargs[n_rows:])

            _, vjp = jax.vjp(g, *vals[:n_rows], *vals[n_rows + n_tabs:])
            d = vjp((tuple(cv[: len(crow)]), tuple(cv[len(crow):])))
            for o, v in zip(outs[:n_rows], d[:n_rows]):
                o[...] = v.astype(o.dtype)
            for o, v in zip(outs[n_rows:], d[n_rows:]):
                @pl.when(pl.program_id(0) == 0)
                def _(o=o, v=v):
                    o[...] = v

                @pl.when(pl.program_id(0) != 0)
                def _(o=o, v=v):
                    o[...] += v

        in_specs = specs(args, tm)
        in_specs += [pl.BlockSpec((tm, c.shape[1]), lambda i: (i, 0)) for c in crow]
        in_specs += [pl.BlockSpec(c.shape, lambda i: (0, 0)) for c in csum]
        in_specs += [pl.BlockSpec((tm, c.shape[1]), lambda i: (i, 0)) for c in more]
        out_shape = [jax.ShapeDtypeStruct(a.shape, dt) for a, dt in zip(rows, row_dtypes or [a.dtype for a in rows])]
        out_shape += [jax.ShapeDtypeStruct(a.shape, a.dtype) for a in pars]
        out_specs = [pl.BlockSpec((tm, a.shape[1]), lambda i: (i, 0)) for a in rows]
        out_specs += [pl.BlockSpec(a.shape, lambda i: (0, 0)) for a in pars]
        res = pl.pallas_call(
            body,
            name=name + "_bwd",
            grid=(r // tm,),
            in_specs=in_specs,
            out_specs=out_specs,
            out_shape=out_shape,
            compiler_params=_params("arbitrary" if pars else "parallel"),
        )(*[_base(a) for a in args], *crow, *csum, *more)
        return tuple(res[:n_rows]), tuple(res[n_rows:])

    @jax.custom_vjp
    def op(rows, tabs, pars):
        return fwd_call(*rows, *tabs, *pars)

    op.fwd_call, op.bwd_call = fwd_call, bwd_call

    def fwd(rows, tabs, pars):
        return fwd_call(*rows, *tabs, *pars), (rows, tabs, pars)

    def bwd(res, cots):
        rows, tabs, pars = res
        drows, dpars = bwd_call(tuple(rows) + tuple(tabs) + tuple(pars), cots)
        return drows, tuple(jnp.zeros_like(t) for t in tabs), dpars

    op.defvjp(fwd, bwd)
    return op


def _sigmoid(x):
    return 0.5 * (jnp.tanh(0.5 * x) + 1.0)


@jax.custom_jvp
def _softplus(x):
    e = jnp.exp(-jnp.abs(x))
    u = 1.0 + e
    log1p_e = jnp.where(u == 1.0, e, e * jnp.log(u) / jnp.where(u == 1.0, 1.0, u - 1.0))
    return jnp.maximum(x, 0.0) + log1p_e


@_softplus.defjvp
def _softplus_jvp(primals, tangents):
    (x,), (t,) = primals, tangents
    return _softplus(x), t * _sigmoid(x)


def _gelu(x):
    return 0.5 * x * (1.0 + jnp.tanh(math.sqrt(2.0 / math.pi) * (x + 0.044715 * (x * x * x))))


def _ln_res_f(h, mix, g, b):
    z = DN_ALPHA * h + mix
    mu = jnp.mean(z, axis=-1, keepdims=True)
    zc = z - mu
    var = jnp.mean(zc * zc, axis=-1, keepdims=True)
    return (zc * lax.rsqrt(var + EPS) * g + b,), ()


def _ln_res_copy_f(h, mix, g, b):
    (out,), _ = _ln_res_f(h, mix, g, b)
    return (out, out.astype(BF16)), ()


def _rmsnorm_f(x, g):
    return (x * lax.rsqrt(jnp.mean(x * x, axis=-1, keepdims=True) + EPS) * g,), ()


def _lru_gates_f(ga, gx, xc, b_a, b_x, lam):
    r = _sigmoid(ga + b_a)
    i = _sigmoid(gx + b_x)
    log_a = -LRU_C * r * _softplus(-lam)
    a = jnp.exp(log_a)
    one_minus_a2 = jnp.tanh(-log_a) * (jnp.exp(2.0 * log_a) + 1.0)
    return (a, jnp.sqrt(one_minus_a2) * (i * xc)), ()


def _lru_out_f(hh, p_gate):
    return (hh * _gelu(p_gate),), ()


def _rope_ret_f(q, k, cos2, sin2):
    d = cos2.shape[1]
    half = d // 2
    k_scale = d ** -0.5

    def rope(x):
        outs = []
        for h in range(x.shape[1] // d):
            xh = x[:, h * d:(h + 1) * d]
            rot = jnp.concatenate([xh[:, half:], xh[:, :half]], axis=1)
            outs.append(xh * cos2 + rot * sin2)
        return jnp.concatenate(outs, axis=1)

    return (rope(q), rope(k) * k_scale), ()


def _ret_out_f(o, g):
    d = o.shape[1] // RET_HEADS
    outs = []
    for h in range(RET_HEADS):
        oh = o[:, h * d:(h + 1) * d]
        outs.append(oh * lax.rsqrt(jnp.mean(oh * oh, axis=-1, keepdims=True) + EPS))
    y = jnp.concatenate(outs, axis=1)
    return (g * _sigmoid(g) * y,), ()


def _ret_out_bf16_f(o, g):
    (y,), _ = _ret_out_f(o, g)
    return (y.astype(BF16),), ()


def _loss_f(y, t, mask):
    e = (y - t) * mask
    per_row = jnp.sum(e * e, axis=-1, keepdims=True) * (0.5 / y.shape[1])
    total = jnp.sum(per_row, axis=0, keepdims=True)
    return (), (jnp.broadcast_to(total, (1, LANES)),)


def _shift_down(x, s):
    if s == 0:
        return x
    t = x.shape[0]
    row = lax.broadcasted_iota(jnp.int32, x.shape, 0)
    return jnp.where(row >= s, pltpu.roll(x, s, 0), 0.0)


def _shift_up(x, s):
    if s == 0:
        return x
    t = x.shape[0]
    row = lax.broadcasted_iota(jnp.int32, x.shape, 0)
    return jnp.where(row < t - s, pltpu.roll(x, t - s, 0), 0.0)


def _conv_fwd(x, w, b, name):
    bsz, t, c = x.shape
    width = w.shape[0]

    def body(x_ref, w_ref, b_ref, y_ref):
        xv = x_ref[0]
        acc = jnp.broadcast_to(b_ref[...], xv.shape)
        for k in range(width):
            acc = acc + w_ref[k:k + 1, :] * _shift_down(xv, width - 1 - k)
        y_ref[0] = acc

    return pl.pallas_call(
        body,
        name=name,
        grid=(bsz, c // LANES),
        in_specs=[
            pl.BlockSpec((1, t, LANES), lambda i, j: (i, 0, j)),
            pl.BlockSpec((width, LANES), lambda i, j: (0, j)),
            pl.BlockSpec((1, LANES), lambda i, j: (0, j)),
        ],
        out_specs=pl.BlockSpec((1, t, LANES), lambda i, j: (i, 0, j)),
        out_shape=jax.ShapeDtypeStruct(x.shape, F32),
        compiler_params=_params("parallel", "parallel"),
    )(x, w, b)


def _conv_bwd(x, w, dy, name):
    bsz, t, c = x.shape
    width = w.shape[0]

    def body(x_ref, w_ref, dy_ref, dx_ref, dw_ref, db_ref):
        xv, g = x_ref[0], dy_ref[0]
        dx = jnp.zeros_like(xv)
        dws = []
        for k in range(width):
            s = width - 1 - k
            dx = dx + w_ref[k:k + 1, :] * _shift_up(g, s)
            dws.append(jnp.sum(g * _shift_down(xv, s), axis=0, keepdims=True))
        dx_ref[0] = dx
        dw = jnp.concatenate(dws, axis=0)
        db = jnp.sum(g, axis=0, keepdims=True)

        @pl.when(pl.program_id(1) == 0)
        def _():
            dw_ref[...] = dw
            db_ref[...] = db

        @pl.when(pl.program_id(1) != 0)
        def _():
            dw_ref[...] += dw
            db_ref[...] += db

    return pl.pallas_call(
        body,
        name=name,
        grid=(c // LANES, bsz),
        in_specs=[
            pl.BlockSpec((1, t, LANES), lambda j, i: (i, 0, j)),
            pl.BlockSpec((width, LANES), lambda j, i: (0, j)),
            pl.BlockSpec((1, t, LANES), lambda j, i: (i, 0, j)),
        ],
        out_specs=[
            pl.BlockSpec((1, t, LANES), lambda j, i: (i, 0, j)),
            pl.BlockSpec((width, LANES), lambda j, i: (0, j)),
            pl.BlockSpec((1, LANES), lambda j, i: (0, j)),
        ],
        out_shape=[
            jax.ShapeDtypeStruct(x.shape, F32),
            jax.ShapeDtypeStruct(w.shape, F32),
            jax.ShapeDtypeStruct((1, c), F32),
        ],
        compiler_params=_params("parallel", "arbitrary"),
    )(x, w, dy)


def _make_conv(name):
    @jax.custom_vjp
    def op(x, w, b):
        return _conv_fwd(x, w, b, name + "_fwd")

    def fwd(x, w, b):
        return op(x, w, b), (x, w)

    def bwd(res, dy):
        x, w = res
        return tuple(_conv_bwd(x, w, dy, name + "_bwd"))

    op.defvjp(fwd, bwd)
    return op


_SCAN_ROWS = 8


def _scan_fwd(a, b, name):
    bsz, t, c = a.shape
    cw = _pick(c, (4 * LANES, 2 * LANES, LANES))

    def body(a_ref, b_ref, h_ref):
        row = lax.broadcasted_iota(jnp.int32, (_SCAN_ROWS, cw), 0)

        def step(i, carry):
            r0 = pl.multiple_of(i * _SCAN_ROWS, _SCAN_ROWS)
            av, bv = a_ref[0, pl.ds(r0, _SCAN_ROWS), :], b_ref[0, pl.ds(r0, _SCAN_ROWS), :]
            for s in (1, 2, 4):
                a_sh = jnp.where(row >= s, pltpu.roll(av, s, 0), 1.0)
                b_sh = jnp.where(row >= s, pltpu.roll(bv, s, 0), 0.0)
                bv = av * b_sh + bv
                av = av * a_sh
            hv = bv + av * carry
            h_ref[0, pl.ds(r0, _SCAN_ROWS), :] = hv
            return hv[_SCAN_ROWS - 1:, :]

        lax.fori_loop(0, t // _SCAN_ROWS, step, jnp.zeros((1, cw), F32), unroll=2)

    spec = pl.BlockSpec((1, t, cw), lambda i, j: (i, 0, j))
    return pl.pallas_call(
        body,
        name=name,
        grid=(bsz, c // cw),
        in_specs=[spec, spec],
        out_specs=spec,
        out_shape=jax.ShapeDtypeStruct(a.shape, F32),
        compiler_params=_params("parallel", "parallel"),
    )(a, b)


def _scan_bwd(a, h, g, name):
    bsz, t, c = a.shape
    cw = _pick(c, (2 * LANES, LANES))

    def body(a_ref, h_ref, g_ref, da_ref, db_ref):
        rows = _SCAN_ROWS
        row = lax.broadcasted_iota(jnp.int32, (rows, cw), 0)
        n_tiles = t // rows

        def step(n, carry):
            lam_next, a_next = carry
            i = n_tiles - 1 - n
            r0 = pl.multiple_of(i * rows, rows)
            rp = pl.multiple_of(jnp.maximum(i - 1, 0) * rows, rows)
            av, gv, hv = a_ref[0, pl.ds(r0, rows), :], g_ref[0, pl.ds(r0, rows), :], h_ref[0, pl.ds(r0, rows), :]
            h_before = jnp.where(i > 0, h_ref[0, pl.ds(rp, rows), :][rows - 1:, :], 0.0)
            cv = jnp.where(row < rows - 1, pltpu.roll(av, rows - 1, 0), a_next)
            for s in (1, 2, 4):
                c_sh = jnp.where(row < rows - s, pltpu.roll(cv, rows - s, 0), 1.0)
                g_sh = jnp.where(row < rows - s, pltpu.roll(gv, rows - s, 0), 0.0)
                gv = cv * g_sh + gv
                cv = cv * c_sh
            lam = gv + cv * lam_next
            db_ref[0, pl.ds(r0, rows), :] = lam
            da_ref[0, pl.ds(r0, rows), :] = lam * jnp.where(row >= 1, pltpu.roll(hv, 1, 0), h_before)
            return lam[:1, :], av[:1, :]

        zero = jnp.zeros((1, cw), F32)
        lax.fori_loop(0, n_tiles, step, (zero, zero), unroll=2)

    spec = pl.BlockSpec((1, t, cw), lambda i, j: (i, 0, j))
    return pl.pallas_call(
        body,
        name=name,
        grid=(bsz, c // cw),
        in_specs=[spec, spec, spec],
        out_specs=[spec, spec],
        out_shape=[jax.ShapeDtypeStruct(a.shape, F32)] * 2,
        compiler_params=_params("parallel", "parallel"),
    )(a, h, g)


def _make_scan(name):
    @jax.custom_vjp
    def op(a, b):
        return _scan_fwd(a, b, name + "_fwd")

    def fwd(a, b):
        h = op(a, b)
        return h, (a, h)

    def bwd(res, g):
        a, h = res
        da, db = _scan_bwd(a, h, g, name + "_bwd")
        return da, db

    op.defvjp(fwd, bwd)
    return op


def _query_blocks(t):
    blocks, start = [], 0
    while start < t:
        rows = 2 * SEQ_BLOCK if start + 2 * SEQ_BLOCK <= t else SEQ_BLOCK
        blocks.append((start, rows))
        start += rows
    return blocks


def _attn_exp(q, k, start, scale):
    tq, tk = q.shape[0], k.shape[0]
    s = _dot(q, k, "nt") * scale
    qpos = start + lax.broadcasted_iota(jnp.int32, (tq, tk), 0)
    kpos = lax.broadcasted_iota(jnp.int32, (tq, tk), 1)
    s = jnp.where(kpos <= qpos, s, NEG_INF)
    e = jnp.exp(s - jnp.max(s, axis=-1, keepdims=True))
    return e, 1.0 / jnp.sum(e, axis=-1, keepdims=True)


_MLA_SCALE = (MLA_NOPE + MLA_ROPE) ** -0.5


def _attn_specs(t):
    head = pl.BlockSpec((1, t, LANES), lambda b, h: (b, 0, h))
    shared = pl.BlockSpec((1, t, LANES), lambda b, h: (b, 0, 0))
    return head, shared


def _attn_fwd(q, kv, kpe, name):
    bsz, t, hl = q.shape
    head, shared = _attn_specs(t)

    def body(q_ref, kv_ref, kpe_ref, o_ref, k_s, v_s):
        lane = lax.broadcasted_iota(jnp.int32, (t, LANES), 1)
        kvh = kv_ref[0]
        k_s[...] = jnp.where(lane < MLA_NOPE, kvh, kpe_ref[0]).astype(BF16)
        v_s[...] = kvh.astype(BF16)
        for start, rows in _query_blocks(t):
            n = start + rows
            e, inv_l = _attn_exp(q_ref[0, start:n, :], k_s[:n, :], start, _MLA_SCALE)
            o_ref[0, start:n, :] = _dot(e, v_s[:n, :], "nn") * inv_l

    return pl.pallas_call(
        body,
        name=name,
        grid=(bsz, hl // LANES),
        in_specs=[head, head, shared],
        out_specs=head,
        out_shape=jax.ShapeDtypeStruct(q.shape, F32),
        scratch_shapes=[pltpu.VMEM((t, LANES), BF16), pltpu.VMEM((t, LANES), BF16)],
        compiler_params=_params("parallel", "parallel"),
    )(q, kv, kpe)


def _attn_bwd(q, kv, kpe, do, name):
    bsz, t, hl = q.shape
    head, shared = _attn_specs(t)

    def body(q_ref, kv_ref, kpe_ref, do_ref, dq_ref, dkv_ref, dkpe_ref, k_s, v_s, dk_s, dv_s):
        lane = lax.broadcasted_iota(jnp.int32, (t, LANES), 1)
        kvh = kv_ref[0]
        k_s[...] = jnp.where(lane < MLA_NOPE, kvh, kpe_ref[0]).astype(BF16)
        v_s[...] = kvh.astype(BF16)
        for start, rows in reversed(_query_blocks(t)):
            n = start + rows
            qb = q_ref[0, start:n, :]
            dob = jnp.where(lane[:rows] >= MLA_NOPE, do_ref[0, start:n, :], 0.0)
            kk, vv = k_s[:n, :], v_s[:n, :]
            e, inv_l = _attn_exp(qb, kk, start, _MLA_SCALE)
            p = e * inv_l
            dp = _dot(dob, vv, "nt")
            ds = p * (dp - jnp.sum(dp * p, axis=-1, keepdims=True)) * _MLA_SCALE
            dq_ref[0, start:n, :] = _dot(ds, kk, "nn")
            if n == t:
                dk_s[...] = _dot(ds, qb, "tn")
                dv_s[...] = _dot(p, dob, "tn")
            else:
                dk_s[:n, :] += _dot(ds, qb, "tn")
                dv_s[:n, :] += _dot(p, dob, "tn")
        dk = dk_s[...]
        dkv_ref[0] = jnp.where(lane < MLA_NOPE, dk, dv_s[...])
        dkpe = jnp.where(lane >= MLA_NOPE, dk, 0.0)

        @pl.when(pl.program_id(1) == 0)
        def _():
            dkpe_ref[0] = dkpe

        @pl.when(pl.program_id(1) != 0)
        def _():
            dkpe_ref[0] += dkpe

    return pl.pallas_call(
        body,
        name=name,
        grid=(bsz, hl // LANES),
        in_specs=[head, head, shared, head],
        out_specs=[head, head, shared],
        out_shape=[
            jax.ShapeDtypeStruct(q.shape, F32),
            jax.ShapeDtypeStruct(kv.shape, F32),
            jax.ShapeDtypeStruct(kpe.shape, F32),
        ],
        scratch_shapes=[pltpu.VMEM((t, LANES), BF16), pltpu.VMEM((t, LANES), BF16),
                        pltpu.VMEM((t, LANES), F32), pltpu.VMEM((t, LANES), F32)],
        compiler_params=_params("parallel", "arbitrary"),
    )(q, kv, kpe, do)


def _make_attention(name):
    @jax.custom_vjp
    def op(q, kv, kpe):
        return _attn_fwd(q, kv, kpe, name + "_fwd")

    def fwd(q, kv, kpe):
        return op(q, kv, kpe), (q, kv, kpe)

    def bwd(res, do):
        return tuple(_attn_bwd(*res, do, name + "_bwd"))

    op.defvjp(fwd, bwd)
    return op


_ROPE_SHIFT = MLA_ROPE // 2


def _rope_lanes_call(x, c, sm, sp, transpose, name):
    r, width = x.shape
    tm = _row_tile(r, width)

    def body(x_ref, c_ref, sm_ref, sp_ref, y_ref):
        cv, smv, spv = c_ref[...], sm_ref[...], sp_ref[...]
        for b in range(width // LANES):
            xb = x_ref[:, b * LANES:(b + 1) * LANES]
            if transpose:
                yb = xb * cv + pltpu.roll(xb * smv, _ROPE_SHIFT, 1) + pltpu.roll(xb * spv, LANES - _ROPE_SHIFT, 1)
            else:
                yb = xb * cv + pltpu.roll(xb, LANES - _ROPE_SHIFT, 1) * smv + pltpu.roll(xb, _ROPE_SHIFT, 1) * spv
            y_ref[:, b * LANES:(b + 1) * LANES] = yb

    tab = pl.BlockSpec((tm, LANES), lambda i: (i, 0))
    blk = pl.BlockSpec((tm, width), lambda i: (i, 0))
    return pl.pallas_call(
        body,
        name=name,
        grid=(r // tm,),
        in_specs=[blk, tab, tab, tab],
        out_specs=blk,
        out_shape=jax.ShapeDtypeStruct(x.shape, F32),
        compiler_params=_params("parallel"),
    )(x, c, sm, sp)


def _make_rope_lanes(name):
    @jax.custom_vjp
    def op(x, c, sm, sp):
        return _rope_lanes_call(x, c, sm, sp, False, name + "_fwd")

    def fwd(x, c, sm, sp):
        return op(x, c, sm, sp), (c, sm, sp)

    def bwd(res, dy):
        c, sm, sp = res
        return _rope_lanes_call(dy, c, sm, sp, True, name + "_bwd"), jnp.zeros_like(c), jnp.zeros_like(sm), jnp.zeros_like(sp)

    op.defvjp(fwd, bwd)
    return op


def _ret_chunk_rows(t):
    return t // 4 if t % 32 == 0 else SEQ_BLOCK


def _ret_decays(c, log_gamma):
    row = lax.broadcasted_iota(jnp.int32, (c, 1), 0)
    col = lax.broadcasted_iota(jnp.int32, (1, c), 1)
    rowf = row.astype(F32)
    d = jnp.where(row >= col, jnp.exp(log_gamma * rowf) * jnp.exp(-log_gamma * col.astype(F32)), 0.0)
    return d, jnp.exp(log_gamma * (rowf + 1.0)), jnp.exp(log_gamma * (c - 1.0 - rowf)), jnp.exp(log_gamma * c)


def _ret_specs(c, dk, dv, v_block0, n_chunks, reverse):
    pos = (lambda i: n_chunks - 1 - i) if reverse else (lambda i: i)
    return (
        pl.BlockSpec(memory_space=pltpu.SMEM),
        pl.BlockSpec((1, c, dk), lambda b, h, i: (b, pos(i), h)),
        pl.BlockSpec((1, c, dv), lambda b, h, i: (b, pos(i), h + v_block0)),
        pl.BlockSpec((1, c, dv), lambda b, h, i: (b, pos(i), h)),
        pl.BlockSpec((1, 1, dk, dv), lambda b, h, i: (b, h * n_chunks + pos(i), 0, 0)),
    )


def _ret_fwd(lg, q, k, v, name, dv=None, v_block0=0):
    bsz, t, hdk = q.shape
    heads = lg.shape[0]
    dk, dv = hdk // heads, dv or v.shape[2] // heads
    c = _ret_chunk_rows(t)
    n_chunks = t // c
    lg_spec, qk_spec, v_spec, o_spec, s_spec = _ret_specs(c, dk, dv, v_block0, n_chunks, False)

    def body(lg_ref, q_ref, k_ref, v_ref, o_ref, s_ref, state):
        @pl.when(pl.program_id(2) == 0)
        def _():
            state[...] = jnp.zeros((dk, dv), F32)

        d, a, b, g = _ret_decays(c, lg_ref[pl.program_id(1)])
        qb, kb, vb, s_in = q_ref[0], k_ref[0], v_ref[0], state[...]
        s_ref[0, 0] = s_in
        o_ref[0] = _dot(_dot(qb, kb, "nt") * d, vb, "nn") + a * _dot(qb, s_in, "nn")
        state[...] = g * s_in + _dot(kb * b, vb, "tn")

    return pl.pallas_call(
        body,
        name=name,
        grid=(bsz, heads, n_chunks),
        in_specs=[lg_spec, qk_spec, qk_spec, v_spec],
        out_specs=[o_spec, s_spec],
        out_shape=[jax.ShapeDtypeStruct((bsz, t, heads * dv), F32),
                   jax.ShapeDtypeStruct((bsz, heads * n_chunks, dk, dv), F32)],
        scratch_shapes=[pltpu.VMEM((dk, dv), F32)],
        compiler_params=_params("parallel", "parallel", "arbitrary"),
    )(lg, q, k, v)


def _ret_bwd(lg, q, k, v, states, do, name, v_block0=0, dv_dtype=F32):
    bsz, t, hdk = q.shape
    heads = lg.shape[0]
    dk, dv = hdk // heads, do.shape[2] // heads
    c = _ret_chunk_rows(t)
    n_chunks = t // c
    lg_spec, qk_spec, v_spec, o_spec, s_spec = _ret_specs(c, dk, dv, v_block0, n_chunks, True)

    def body(lg_ref, q_ref, k_ref, v_ref, s_ref, do_ref, dq_ref, dk_ref, dv_ref, dstate):
        @pl.when(pl.program_id(2) == 0)
        def _():
            dstate[...] = jnp.zeros((dk, dv), F32)

        d, a, b, g = _ret_decays(c, lg_ref[pl.program_id(1)])
        qb, kb, vb, dob, s_in, ds_out = q_ref[0], k_ref[0], v_ref[0], do_ref[0], s_ref[0, 0], dstate[...]
        scores = _dot(qb, kb, "nt") * d
        dscores = _dot(dob, vb, "nt") * d
        dq_ref[0] = _dot(dscores, kb, "nn") + a * _dot(dob, s_in, "nt")
        dk_ref[0] = _dot(dscores, qb, "tn") + b * _dot(vb, ds_out, "nt")
        dv_ref[0] = (_dot(scores, dob, "tn") + _dot(kb * b, ds_out, "nn")).astype(dv_dtype)
        dstate[...] = g * ds_out + _dot(qb, a * dob, "tn")

    return pl.pallas_call(
        body,
        name=name,
        grid=(bsz, heads, n_chunks),
        in_specs=[lg_spec, qk_spec, qk_spec, v_spec, s_spec, o_spec],
        out_specs=[qk_spec, qk_spec, o_spec],
        out_shape=[
            jax.ShapeDtypeStruct(q.shape, F32),
            jax.ShapeDtypeStruct(k.shape, F32),
            jax.ShapeDtypeStruct(do.shape, dv_dtype),
        ],
        scratch_shapes=[pltpu.VMEM((dk, dv), F32)],
        compiler_params=_params("parallel", "parallel", "arbitrary"),
    )(lg, q, k, v, states, do)


def _adamw(w, g, m, v, name):
    r, c = w.shape
    tr = _pick(r, (256, 128, 64, 32, 16, 8))

    def body(w_ref, g_ref, m_ref, v_ref, d_ref, nm_ref, nv_ref):
        gv = g_ref[...]
        nm = ADAM_B1 * m_ref[...] + (1.0 - ADAM_B1) * gv
        nv = ADAM_B2 * v_ref[...] + (1.0 - ADAM_B2) * (gv * gv)
        m_hat = nm / (1.0 - ADAM_B1 ** ADAM_STEP)
        v_hat = nv / (1.0 - ADAM_B2 ** ADAM_STEP)
        d_ref[...] = -ADAM_LR * (m_hat / (jnp.sqrt(v_hat) + ADAM_EPS) + ADAM_WD * w_ref[...])
        nm_ref[...] = nm
        nv_ref[...] = nv

    spec = pl.BlockSpec((tr, c), lambda i: (i, 0))
    return pl.pallas_call(
        body,
        name=name,
        grid=(r // tr,),
        in_specs=[spec] * 4,
        out_specs=[spec] * 3,
        out_shape=[jax.ShapeDtypeStruct((r, c), F32)] * 3,
        compiler_params=_params("parallel"),
    )(w, g, m, v)


def _rope_tables(t, half, reps):
    inv = ROPE_BASE ** (-jnp.arange(half, dtype=F32) / half)
    ang = jnp.arange(t, dtype=jnp.int32).astype(F32)[:, None] * inv[None, :]
    return jnp.tile(jnp.cos(ang), (1, reps)), jnp.tile(jnp.sin(ang), (1, reps))


def _padded_len(seq):
    return -(-(N_META + seq) // SEQ_BLOCK) * SEQ_BLOCK


def _embed(meta, x):
    bsz, seq, d = x.shape
    t = _padded_len(seq)
    return jnp.concatenate(
        [jnp.broadcast_to(meta[None], (bsz, N_META, d)), x, jnp.zeros((bsz, t - N_META - seq, d), F32)], axis=1
    ).reshape(bsz * t, d)


def _even_mixer(p, conv_w, conv_b, w_rg_a, b_rg_a, w_rg_x, b_rg_x, lru_lambda, q_norm_g, uq_slot, kv_norm_g,
                ukv_slot, gathered, bsz):
    w_uq_pad, w_ukv_full = gathered
    r = p.shape[0]
    t = r // bsz

    def tile_rows(tab):
        return jnp.tile(tab, (bsz, 1))

    lru_w = w_rg_a.shape[2] * w_rg_a.shape[1]
    q_rank, kv_rank = q_norm_g.shape[1], kv_norm_g.shape[1]
    p_gate, p_rec, p_q, p_kv, p_kpe = _split_cols(
        p, (lru_w, 2 * lru_w, 2 * lru_w + q_rank, 2 * lru_w + q_rank + kv_rank))

    xc = _make_conv("conv")(p_rec.reshape(bsz, t, lru_w), conv_w, conv_b).reshape(r, lru_w)
    ga, gx = _make_gate_pair("rg")(xc, w_rg_a[0], w_rg_x[0])
    (a, bb), _ = _make_rowwise(_lru_gates_f, "lru_gates", 3, 0, 3)((ga, gx, xc), (), (b_rg_a, b_rg_x, lru_lambda))
    hh = _make_scan("lru_scan")(a.reshape(bsz, t, lru_w), bb.reshape(bsz, t, lru_w)).reshape(r, lru_w)
    (y_rec,), _ = _make_rowwise(_lru_out_f, "lru_out", 2, 0, 0)((hh, p_gate), (), ())

    (qn,), _ = _make_rowwise(_rmsnorm_f, "q_norm", 1, 0, 1)((p_q,), (), (q_norm_g,))
    (kvn,), _ = _make_rowwise(_rmsnorm_f, "kv_norm", 1, 0, 1)((p_kv,), (), (kv_norm_g,))
    q = _make_slot_linear("ev_uq")(qn, w_uq_pad, uq_slot)
    kv = _make_slot_linear("ev_ukv")(kvn, w_ukv_full, ukv_slot)
    half = MLA_ROPE // 2
    cos, sin = _rope_tables(t, half, 1)
    one, zero = jnp.ones((t, MLA_NOPE), F32), jnp.zeros((t, MLA_NOPE), F32)
    tail = LANES - MLA_NOPE - MLA_ROPE
    c_tab = tile_rows(jnp.concatenate([one, cos, cos, one[:, :tail]], axis=1))
    sm_tab = tile_rows(jnp.concatenate([zero, -sin, zero[:, :half + tail]], axis=1))
    sp_tab = tile_rows(jnp.concatenate([zero, zero[:, :half], sin, zero[:, :tail]], axis=1))
    q = _make_rope_lanes("rope_q")(q, c_tab, sm_tab, sp_tab)
    kpe = _make_rope_lanes("rope_k")(p_kpe, c_tab, sm_tab, sp_tab)
    o = _make_attention("mla")(q.reshape(bsz, t, -1), kv.reshape(bsz, t, -1), kpe.reshape(bsz, t, LANES))
    return jnp.concatenate([y_rec, o.reshape(r, -1)], axis=1)


def _odd_mixer_fwd(p, bsz):
    r, width = p.shape
    t = r // bsz
    qk = width // 6
    dk = qk // RET_HEADS
    cos2, sin2 = _rope_tables(t, dk // 2, 2)
    sin2 = jnp.concatenate([-sin2[:, :dk // 2], sin2[:, dk // 2:]], axis=1)
    rope_args = (_Cols(p, qk, 0), _Cols(p, qk, 1), jnp.tile(cos2, (bsz, 1)), jnp.tile(sin2, (bsz, 1)))
    (rq, rk), _ = _make_rowwise(_rope_ret_f, "rope_ret", 2, 2, 0).fwd_call(*rope_args)
    lg = jnp.log(1.0 - 2.0 ** (-5.0 - jnp.arange(RET_HEADS, dtype=F32)))
    ret_args = (lg, rq.reshape(bsz, t, qk), rk.reshape(bsz, t, qk), p.reshape(bsz, t, width))
    o, states = _ret_fwd(*ret_args, "ret_fwd", dv=2 * dk, v_block0=qk // dk)
    gate_args = (o.reshape(r, 2 * qk), _Cols(p, 2 * qk, 2))
    (y,), _ = _make_rowwise(_ret_out_bf16_f, "ret_out", 2, 0, 0).fwd_call(*gate_args)
    return y, (rope_args, ret_args + (states,), gate_args)


def _odd_mixer_bwd(res, dy):
    rope_args, ret_args, gate_args = res
    bsz, t, qk = ret_args[1].shape
    dk = qk // RET_HEADS
    (do, dg), _ = _make_rowwise(_ret_out_f, "ret_out", 2, 0, 0).bwd_call(
        gate_args, ((dy,), ()), row_dtypes=(F32, BF16))
    drq, drk, dv = _ret_bwd(*ret_args, do.reshape(bsz, t, 2 * qk), "ret_bwd", v_block0=qk // dk, dv_dtype=BF16)
    (dq, dkk), _ = _make_rowwise(_rope_ret_f, "rope_ret", 2, 2, 0).bwd_call(
        rope_args, ((drq.reshape(bsz * t, qk), drk.reshape(bsz * t, qk)), ()), row_dtypes=(BF16, BF16))
    return jnp.concatenate([dq, dkk, dv.reshape(bsz * t, 2 * qk), dg], axis=1)


def _local_loss(h, target):
    bsz, seq, d = target.shape
    t = _padded_len(seq)
    t_real = N_META + seq
    pos = jnp.arange(t, dtype=jnp.int32)
    mask = jnp.tile(((pos >= N_META) & (pos < t_real)).astype(F32)[:, None], (bsz, 1))
    tgt = jnp.concatenate(
        [jnp.zeros((bsz, N_META, d), F32), target, jnp.zeros((bsz, t - t_real, d), F32)], axis=1).reshape(bsz * t, d)
    _, (total,) = _make_rowwise(_loss_f, "loss", 1, 2, 0)((h,), (tgt, mask), ())
    return total[0, 0]


_WEIGHTS = ("meta_tokens", "ev_w_in", "ev_conv_w", "ev_conv_b", "ev_w_rg_a", "ev_b_rg_a", "ev_w_rg_x", "ev_b_rg_x",
            "ev_lru_lambda", "ev_q_norm_g", "ev_w_uq", "ev_kv_norm_g", "ev_w_ukv", "ev_w_out", "od_w_in", "od_w_out",
            "ln_mix_g", "ln_mix_b", "mlp_w1", "mlp_w2", "ln_mlp_g", "ln_mlp_b")


def kernel(x, meta_tokens, ev_w_in, ev_conv_w, ev_conv_b, ev_w_rg_a, ev_b_rg_a, ev_w_rg_x, ev_b_rg_x, ev_lru_lambda, ev_q_norm_g, ev_w_uq, ev_kv_norm_g, ev_w_ukv, ev_w_out, od_w_in, od_w_out, ln_mix_g, ln_mix_b, mlp_w1, mlp_w2, ln_mlp_g, ln_mlp_b, loss_target, m_meta_tokens, m_ev_w_in, m_ev_conv_w, m_ev_conv_b, m_ev_w_rg_a, m_ev_b_rg_a, m_ev_w_rg_x, m_ev_b_rg_x, m_ev_lru_lambda, m_ev_q_norm_g, m_ev_w_uq, m_ev_kv_norm_g, m_ev_w_ukv, m_ev_w_out, m_od_w_in, m_od_w_out, m_ln_mix_g, m_ln_mix_b, m_mlp_w1, m_mlp_w2, m_ln_mlp_g, m_ln_mlp_b, v_meta_tokens, v_ev_w_in, v_ev_conv_w, v_ev_conv_b, v_ev_w_rg_a, v_ev_b_rg_a, v_ev_w_rg_x, v_ev_b_rg_x, v_ev_lru_lambda, v_ev_q_norm_g, v_ev_w_uq, v_ev_kv_norm_g, v_ev_w_ukv, v_ev_w_out, v_od_w_in, v_od_w_out, v_ln_mix_g, v_ln_mix_b, v_mlp_w1, v_mlp_w2, v_ln_mlp_g, v_ln_mlp_b):
    args = locals()
    weights = {n: args[n] for n in _WEIGHTS}
    bsz = x.shape[0]
    my_x, my_y, my_c = _my_place()
    me = 4 * my_x + 2 * my_y + my_c

    big = (("ev_in", ev_w_in[0], True), ("ev_out", ev_w_out[0], False), ("mlp0_w1", mlp_w1[0], True),
           ("mlp0_w2", mlp_w2[0], False), ("od_in", od_w_in[0], True), ("od_out", od_w_out[0], False),
           ("mlp1_w1", mlp_w1[1], True), ("mlp1_w2", mlp_w2[1], False))
    small_sharded = (("meta", meta_tokens, F32), ("conv_w", ev_conv_w[0], F32), ("ev_uq", ev_w_uq[0], BF16),
                     ("ev_ukv", ev_w_ukv[0], BF16))
    to_gather = (tuple((nm, s.astype(dt), True) for nm, s, dt in small_sharded)
                 + tuple((nm, s.astype(BF16), cols) for nm, s, cols in big))
    handles = _gather_start_all([s for _, s, _ in to_gather], "ag_start")
    gathers = {nm: (s, cols, h) for (nm, s, cols), h in zip(to_gather, handles)}
    gather_tokens = (handles[0][4],)

    def full_weight(nm, after):
        shard16, cols, handle = gathers[nm]
        land = _exchange_wait(handle, True, after, "ag_wait_" + nm)
        land = lax.dynamic_update_index_in_dim(land, shard16, me, 0)
        if cols and shard16.shape[1] % LANES == 0:
            return land, True
        return (_unstack_cols(land) if cols else land.reshape(-1, shard16.shape[1])), False

    meta_full, conv_w_full, w_uq_full, w_ukv_full = (
        _unstack_cols(lax.dynamic_update_index_in_dim(
            _exchange_wait(gathers[nm][2], True, gather_tokens[-1], "ag_wait_" + nm), gathers[nm][0], me, 0))
        for nm, _, _ in small_sharded)

    pending = []

    def linear_bwd(nm, x_in, w, dy, cols, unpad=None, **fused):
        w_full, w_stacked = w
        own = None
        if w_stacked:
            stacked = _matmul(x_in, dy, "tn", nm + "_dw", stacked=True)
        else:
            dw = _matmul(x_in, dy, "tn", nm + "_dw")
            dw = dw if unpad is None else unpad(dw)
            n = dw.shape[1] // N_DEV
            if cols:
                stacked = _stack_cols(dw)
                own = lax.dynamic_slice_in_dim(dw, me * n, n, axis=1)
            else:
                stacked = dw.reshape(N_DEV, dw.shape[0] // N_DEV, dw.shape[1])
        handle = _exchange_start(stacked, (_N_PEERS,) + stacked.shape[1:], False, "rs_start_" + nm)
        if own is None:
            own = lax.dynamic_index_in_dim(handle[2], me, 0, keepdims=False)
        pending.append((nm, own, handle))
        return _matmul(dy, w_full, "nt", nm + "_dx", after=(handle[4],), stacked=w_stacked, **fused)

    def linear_fwd(nm, x_in, w, **fused):
        return _matmul(x_in, w[0], "nn", nm + "_fwd", stacked=w[1], **fused)

    def mlp_fwd(h, h16, l):
        w1 = full_weight(f"mlp{l}_w1", h16)
        a16 = linear_fwd(f"mlp{l}_w1", h16, w1, out_relu2=True, out_dtype=BF16)
        w2 = full_weight(f"mlp{l}_w2", a16)
        f = linear_fwd(f"mlp{l}_w2", a16, w2)
        ln_args = (h, f, ln_mlp_g[l:l + 1], ln_mlp_b[l:l + 1])
        return ln_fwd(f"mlp{l}_ln", *ln_args), (h16, w1, a16, w2, ln_args)

    def mlp_bwd(dout, res, l):
        h16, w1, a16, w2, ln_args = res
        dh, df, dg, db = ln_bwd(f"mlp{l}_ln", ln_args, dout)
        du = linear_bwd(f"mlp{l}_w2", a16, w2, df, False, relu2_bwd_of=a16, out_dtype=BF16)
        return (dh, linear_bwd(f"mlp{l}_w1", h16, w1, du, True)), dg, db

    def ln_fwd(nm, h, mix, g, b):
        return _make_rowwise(_ln_res_copy_f, nm, 2, 0, 2).fwd_call(h, mix, g, b)[0]

    def ln_bwd(nm, ln_args, pieces):
        (dh, dmix), (dg, db) = _make_rowwise(_ln_res_f, nm, 2, 0, 2).bwd_call(
            ln_args, ((pieces[0],), ()), more=tuple(pieces[1:]), row_dtypes=(F32, BF16))
        return dh, dmix, dg, db

    h0, vjp_embed = jax.vjp(_embed, meta_full, x)
    n_in = ev_w_in.shape[2] * N_DEV
    kpe0, pad_lo, pad_hi = n_in - MLA_ROPE, MLA_NOPE, LANES - MLA_NOPE - MLA_ROPE
    w_in = full_weight("ev_in", h0)[0]
    zeros_in = jnp.zeros((w_in.shape[0], pad_lo), BF16)
    w_ev_in = (jnp.concatenate([w_in[:, :kpe0], zeros_in, w_in[:, kpe0:], zeros_in[:, :pad_hi]], axis=1), False)

    def unpad_in(dw):
        return jnp.concatenate([dw[:, :kpe0], dw[:, kpe0 + pad_lo:kpe0 + pad_lo + MLA_ROPE]], axis=1)

    p0 = _matmul(h0, w_ev_in[0], "nn", "ev_in_fwd", after=gather_tokens)
    q_rank, d_head = w_uq_full.shape[0], MLA_NOPE + MLA_ROPE
    w_uq_pad = jnp.pad(w_uq_full.reshape(q_rank, MLA_HEADS, d_head), ((0, 0), (0, 0), (0, LANES - d_head)))
    w_uq_pad = w_uq_pad.reshape(q_rank, MLA_HEADS * LANES)
    small = (conv_w_full, ev_conv_b, ev_w_rg_a, ev_b_rg_a, ev_w_rg_x, ev_b_rg_x, ev_lru_lambda, ev_q_norm_g,
             jnp.zeros(w_uq_pad.shape, F32), ev_kv_norm_g, jnp.zeros(w_ukv_full.shape, F32))
    y0, vjp_even = jax.vjp(lambda p, *s: _even_mixer(p, *s, (w_uq_pad, w_ukv_full), bsz), p0, *small)
    w_out = full_weight("ev_out", y0)[0]
    lru_w, d_model = y0.shape[1] - MLA_HEADS * LANES, w_out.shape[1]
    w_att = w_out[lru_w:].reshape(MLA_HEADS, MLA_V, d_model)
    w_att = jnp.concatenate([jnp.zeros((MLA_HEADS, LANES - MLA_V, d_model), BF16), w_att], axis=1)
    w_ev_out = (jnp.concatenate([w_out[:lru_w], w_att.reshape(MLA_HEADS * LANES, d_model)], axis=0), False)

    def unpad_out(dw):
        d_att = dw[lru_w:].reshape(MLA_HEADS, LANES, d_model)[:, LANES - MLA_V:].reshape(MLA_HEADS * MLA_V, d_model)
        return jnp.concatenate([dw[:lru_w], d_att], axis=0)

    mix0 = linear_fwd("ev_out", y0, w_ev_out)
    ln0_args = (h0, mix0, ln_mix_g[0:1], ln_mix_b[0:1])
    h1, h1_16 = ln_fwd("mix0_ln", *ln0_args)
    (h2, h2_16), res_mlp0 = mlp_fwd(h1, h1_16, 0)
    w_od_in = full_weight("od_in", h2_16)
    p1 = linear_fwd("od_in", h2_16, w_od_in)
    y1, res_odd = _odd_mixer_fwd(p1, bsz)
    w_od_out = full_weight("od_out", y1)
    mix1 = linear_fwd("od_out", y1, w_od_out)
    ln1_args = (h2, mix1, ln_mix_g[1:2], ln_mix_b[1:2])
    h3, h3_16 = ln_fwd("mix1_ln", *ln1_args)
    (h4, _), res_mlp1 = mlp_fwd(h3, h3_16, 1)
    loss_local, vjp_loss = jax.vjp(lambda h: _local_loss(h, loss_target), h4)

    dh4 = vjp_loss(jnp.ones((), F32))
    dh3, dg_mlp1, db_mlp1 = mlp_bwd(dh4, res_mlp1, 1)
    dh2, dmix1, dg_mix1, db_mix1 = ln_bwd("mix1_ln", ln1_args, dh3)
    dp1 = _odd_mixer_bwd(res_odd, linear_bwd("od_out", y1, w_od_out, dmix1, False))
    dh2 = (dh2, linear_bwd("od_in", h2_16, w_od_in, dp1, True))
    dh1, dg_mlp0, db_mlp0 = mlp_bwd(dh2, res_mlp0, 0)
    dh0, dmix0, dg_mix0, db_mix0 = ln_bwd("mix0_ln", ln0_args, dh1)
    dp0, *dsmall = vjp_even(linear_bwd("ev_out", y0, w_ev_out, dmix0, False, unpad=unpad_out))
    dh0 = dh0 + linear_bwd("ev_in", h0, w_ev_in, dp0.astype(BF16), True, unpad=unpad_in)
    g_meta_full, grad_x = vjp_embed(dh0)
    (g_conv_w_full, g_conv_b, g_w_rg_a, g_b_rg_a, g_w_rg_x, g_b_rg_x, g_lambda, g_q_norm, g_uq_pad, g_kv_norm,
     g_ukv_full) = dsmall

    for nm, dw in (("ev_uq", g_uq_pad.reshape(q_rank, MLA_HEADS, LANES)[:, :, :d_head].reshape(q_rank, -1)),
                   ("ev_ukv", g_ukv_full)):
        n = dw.shape[1] // N_DEV
        handle = _exchange_start(_stack_cols(dw), (_N_PEERS, dw.shape[0], n), False, "rs_start_" + nm)
        pending.append((nm, lax.dynamic_slice_in_dim(dw, me * n, n, axis=1), handle))

    rep_names = ("ev_conv_b", "ev_w_rg_a", "ev_b_rg_a", "ev_w_rg_x", "ev_b_rg_x", "ev_lru_lambda", "ev_q_norm_g",
                 "ev_kv_norm_g", "ln_mix_g", "ln_mix_b", "ln_mlp_g", "ln_mlp_b")
    rep_local = (g_conv_b, g_w_rg_a, g_b_rg_a, g_w_rg_x, g_b_rg_x, g_lambda, g_q_norm, g_kv_norm,
                 jnp.concatenate([dg_mix0, dg_mix1]), jnp.concatenate([db_mix0, db_mix1]),
                 jnp.concatenate([dg_mlp0, dg_mlp1]), jnp.concatenate([db_mlp0, db_mlp1]), g_conv_w_full, g_meta_full)
    rep_stacked = _pack_rows(rep_local).reshape(N_DEV, -1, LANES)
    rep_rs = _exchange_start(rep_stacked, (_N_PEERS,) + rep_stacked.shape[1:], False, "rep_rs_start", after=(grad_x,))
    rep_own = lax.dynamic_index_in_dim(rep_rs[2], me, 0, keepdims=False)

    after, summed = rep_rs[4], {}
    for nm, own, handle in pending:
        land = _exchange_wait(handle, False, after, "rs_wait_" + nm)
        summed[nm] = after = _sum_own_and_peers(own, land, "rs_sum_" + nm)
    rep_part = _sum_own_and_peers(rep_own, _exchange_wait(rep_rs, False, after, "rep_rs_wait"), "rep_sum")
    rep_ag = _exchange_start(rep_part, (N_DEV,) + rep_part.shape, True, "rep_ag_start")

    grad_w = dict(ev_w_uq=summed["ev_uq"][None], ev_w_ukv=summed["ev_ukv"][None],
                  ev_w_in=summed["ev_in"][None], ev_w_out=summed["ev_out"][None], od_w_in=summed["od_in"][None],
                  od_w_out=summed["od_out"][None], mlp_w1=jnp.stack([summed["mlp0_w1"], summed["mlp1_w1"]]),
                  mlp_w2=jnp.stack([summed["mlp0_w2"], summed["mlp1_w2"]]))

    loss = lax.psum(loss_local, MESH_AXES)
    delta, new_m, new_v = {}, {}, {}

    def adamw(n):
        w, g, m, v = weights[n], grad_w[n], args["m_" + n], args["v_" + n]
        two_d = (-1, w.shape[-1])
        d2, m2, v2 = _adamw(w.reshape(two_d), g.reshape(two_d), m.reshape(two_d), v.reshape(two_d), "adamw_" + n)
        delta[n], new_m[n], new_v[n] = d2.reshape(w.shape), m2.reshape(w.shape), v2.reshape(w.shape)
        return d2

    for n in tuple(grad_w):
        after = adamw(n)
    rep_land = lax.dynamic_update_index_in_dim(_exchange_wait(rep_ag, True, after, "rep_ag_wait"), rep_part, me, 0)
    *rep_total, t_conv_w, t_meta = _unpack_rows(rep_land, rep_local)
    n_conv, n_meta = ev_conv_w.shape[2], meta_tokens.shape[1]
    small_g = dict(zip(rep_names, rep_total))
    small_g.update(meta_tokens=lax.dynamic_slice_in_dim(t_meta, me * n_meta, n_meta, axis=1),
                   ev_conv_w=lax.dynamic_slice_in_dim(t_conv_w, me * n_conv, n_conv, axis=1)[None])
    grad_w.update(small_g)
    for n in small_g:
        adamw(n)
    return (loss, grad_x, *[grad_w[n] for n in _WEIGHTS], *[delta[n] for n in _WEIGHTS],
            *[new_m[n] for n in _WEIGHTS], *[new_v[n] for n in _WEIGHTS])
```

```python
import math

import jax
import jax.numpy as jnp
from jax import lax
from jax.experimental import pallas as pl
from jax.experimental.pallas import tpu as pltpu

F32 = jnp.float32
BF16 = jnp.bfloat16

N_DEV = 8
MESH_AXES = ("x", "y", "c")
LANES = 128
SEQ_BLOCK = 128

N_META = 16
LRU_C = 8.0
MLA_HEADS = 8
MLA_NOPE = 64
MLA_ROPE = 32
MLA_V = 64
RET_HEADS = 4
ROPE_BASE = 10000.0
DEPTH = 2
DN_ALPHA = (2 * DEPTH) ** 0.25
EPS = 1e-5
NEG_INF = -1e30

ADAM_LR = 0.001
ADAM_B1 = 0.9
ADAM_B2 = 0.999
ADAM_EPS = 1e-08
ADAM_WD = 0.01
ADAM_STEP = 10

VMEM_LIMIT = 56 * 1024 * 1024
TN_OPERAND_BYTES = 36 * 1024 * 1024


def _params(*sem):
    return pltpu.CompilerParams(dimension_semantics=sem, vmem_limit_bytes=VMEM_LIMIT)


def _pick(n, cands):
    for c in cands:
        if n % c == 0:
            return c
    return n


def _row_tile(r, width):
    cands = (256, 128, 64, 32, 16, 8) if width <= 1024 else (128, 64, 32, 16, 8)
    return _pick(r, cands)


_DIMS = {"nn": (((1,), (0,)), ((), ())), "nt": (((1,), (1,)), ((), ())), "tn": (((0,), (0,)), ((), ()))}


def _dot(a, b, mode):
    return lax.dot_general(a.astype(BF16), b.astype(BF16), _DIMS[mode], preferred_element_type=F32)


def _matmul(a, b, mode, name, after=(), stacked=False, relu2_bwd_of=None, out_dtype=F32, out_relu2=False):
    if stacked:
        n_blk = b.shape[2] if mode != "tn" else b.shape[1] // N_DEV
    if mode == "nn":
        (m, k), n = a.shape, (N_DEV * n_blk if stacked else b.shape[1])
    elif mode == "nt":
        (m, k), n = a.shape, (b.shape[1] if stacked else b.shape[0])
    else:
        (k, m), n = a.shape, b.shape[1]
    tm = _pick(m, (2176, 1088, 1024, 544, 512, 272, 256, 128, 64, 32, 16, 8))
    tn = _pick(n, (1024, 512, 256, 128))
    tk = _pick(k, (2176, 1088, 1024, 544, 512, 272, 256, 128))
    kb = 2
    if stacked and mode == "nn":
        tn = n_blk
    if stacked and mode == "tn":
        tn = kb * n_blk
    if stacked and mode == "nt":
        tk = kb * n_blk
    if mode == "tn" and 2 * k * (tm * a.dtype.itemsize + tn * b.dtype.itemsize) <= TN_OPERAND_BYTES:
        tk = k
    nk = k // tk
    assert out_dtype == F32 or (nk == 1 and not (stacked and mode == "tn")), "narrow results need a single k step"
    assert not out_relu2 or nk == 1, "relu^2 is applied to a finished tile"

    out_spec = pl.BlockSpec((tm, tn), lambda i, j, kk: (i, j))
    out_shape = jax.ShapeDtypeStruct((m, n), out_dtype)
    if mode == "nn":
        a_spec = pl.BlockSpec((tm, tk), lambda i, j, kk: (i, kk))
        b_spec = pl.BlockSpec((tk, tn), lambda i, j, kk: (kk, j))
        if stacked:
            b_spec = pl.BlockSpec((None, tk, tn), lambda i, j, kk: (j, kk, 0))
    elif mode == "nt":
        a_spec = pl.BlockSpec((tm, tk), lambda i, j, kk: (i, kk))
        b_spec = pl.BlockSpec((tn, tk), lambda i, j, kk: (j, kk))
        if stacked:
            b_spec = pl.BlockSpec((kb, tn, n_blk), lambda i, j, kk: (kk, j, 0))
    else:
        a_spec = pl.BlockSpec((tk, tm), lambda i, j, kk: (kk, i))
        b_spec = pl.BlockSpec((tk, tn), lambda i, j, kk: (kk, j))
        if stacked:
            out_spec = pl.BlockSpec((kb, tm, n_blk), lambda i, j, kk: (j, i, 0))
            out_shape = jax.ShapeDtypeStruct((N_DEV, m, n_blk), F32)
    extra = [] if relu2_bwd_of is None else [relu2_bwd_of]
    extra_specs = [pl.BlockSpec((tm, tn), lambda i, j, kk: (i, j))] * len(extra)

    def body(a_ref, b_ref, *rest):
        def relu2_slope():
            return 2.0 * jnp.sqrt(rest[0][...].astype(F32))

        o_ref = rest[-1]
        kk = pl.program_id(2)
        av = a_ref[...]
        if stacked and mode == "nt":
            part = _dot(av[:, :n_blk], b_ref[0], mode)
            for q in range(1, kb):
                part = part + _dot(av[:, q * n_blk:(q + 1) * n_blk], b_ref[q], mode)
        else:
            part = _dot(av, b_ref[...], mode)
        if stacked and mode == "tn":
            part = jnp.stack([part[:, q * n_blk:(q + 1) * n_blk] for q in range(kb)])
        if nk == 1:
            if out_relu2:
                part = jnp.maximum(part, 0.0)
                part = part * part
            if relu2_bwd_of is not None:
                part = part * relu2_slope()
            o_ref[...] = part.astype(out_dtype)
            return

        @pl.when(kk == 0)
        def _():
            o_ref[...] = part

        @pl.when(kk != 0)
        def _():
            o_ref[...] += part

        if relu2_bwd_of is not None:
            @pl.when(kk == nk - 1)
            def _():
                o_ref[...] *= relu2_slope()

    return pl.pallas_call(
        body,
        name=name,
        grid=(m // tm, n // tn, nk),
        in_specs=[a_spec, b_spec] + extra_specs + [pl.BlockSpec(memory_space=pl.ANY)] * len(after),
        out_specs=out_spec,
        out_shape=out_shape,
        compiler_params=_params("parallel", "parallel", "arbitrary"),
    )(a, b, *extra, *after)


def _make_gate_pair(name):
    def call(body, ins, n_out, reduce_rows, tag):
        x = ins[0]
        m, d = x.shape[0], LANES
        g = x.shape[1] // d
        tm = _pick(m, (1088, 1024, 544, 512, 272, 256, 128, 64, 32, 16, 8))
        rows = pl.BlockSpec((tm, d), lambda h, i: (i, h))
        mats = pl.BlockSpec((1, d, d), lambda h, i: (h, 0, 0))
        return pl.pallas_call(
            body,
            name=name + tag,
            grid=(g, m // tm),
            in_specs=[rows if a.ndim == 2 else mats for a in ins],
            out_specs=[mats if reduce_rows else rows] * n_out,
            out_shape=[jax.ShapeDtypeStruct((g, d, d) if reduce_rows else (m, g * d), F32)] * n_out,
            compiler_params=_params("parallel", "arbitrary" if reduce_rows else "parallel"),
        )(*ins)

    def fwd_body(x_ref, wa_ref, wx_ref, a_ref, b_ref):
        xv = x_ref[...]
        a_ref[...] = _dot(xv, wa_ref[0], "nn")
        b_ref[...] = _dot(xv, wx_ref[0], "nn")

    def dx_body(da_ref, db_ref, wa_ref, wx_ref, dx_ref):
        dx_ref[...] = _dot(da_ref[...], wa_ref[0], "nt") + _dot(db_ref[...], wx_ref[0], "nt")

    def dw_body(x_ref, da_ref, db_ref, dwa_ref, dwx_ref):
        xv = x_ref[...]
        pa, pb = _dot(xv, da_ref[...], "tn"), _dot(xv, db_ref[...], "tn")

        @pl.when(pl.program_id(1) == 0)
        def _():
            dwa_ref[0] = pa
            dwx_ref[0] = pb

        @pl.when(pl.program_id(1) != 0)
        def _():
            dwa_ref[0] += pa
            dwx_ref[0] += pb

    @jax.custom_vjp
    def op(x, wa, wx):
        return tuple(call(fwd_body, (x, wa, wx), 2, False, "_fwd"))

    def fwd(x, wa, wx):
        return op(x, wa, wx), (x, wa, wx)

    def bwd(res, cots):
        x, wa, wx = res
        da, db = cots
        (dx,) = call(dx_body, (da, db, wa, wx), 1, False, "_dx")
        dwa, dwx = call(dw_body, (x, da, db), 2, True, "_dw")
        return dx, dwa, dwx

    op.defvjp(fwd, bwd)
    return op


def _my_place():
    return lax.axis_index("x"), lax.axis_index("y"), lax.axis_index("c")


def _stack_cols(full):
    k, n8 = full.shape
    return full.reshape(k, N_DEV, n8 // N_DEV).transpose(1, 0, 2)


def _unstack_cols(stacked):
    j, k, n = stacked.shape
    return stacked.transpose(1, 0, 2).reshape(k, j * n)


def _split_cols(p, cuts):
    bounds = (0,) + tuple(cuts) + (p.shape[1],)

    @jax.custom_vjp
    def op(z):
        return tuple(z[:, lo:hi] for lo, hi in zip(bounds[:-1], bounds[1:]))

    op.defvjp(lambda z: (op(z), None), lambda _, cots: (jnp.concatenate(cots, axis=1),))
    return op(p)


def _make_slot_linear(name):
    @jax.custom_vjp
    def op(x, w_full, slot):
        return _matmul(x, w_full, "nn", name + "_fwd")

    def fwd(x, w_full, slot):
        return op(x, w_full, slot), (x, w_full)

    def bwd(res, dy):
        x, w = res
        return _matmul(dy, w, "nt", name + "_dx"), jnp.zeros_like(w), _matmul(x, dy, "tn", name + "_dw")

    op.defvjp(fwd, bwd)
    return op


def _pack_rows(gs):
    flat = jnp.concatenate([g.reshape(-1) for g in gs])
    n = flat.shape[0]
    rows = -(-n // (256 * LANES)) * 256
    return jnp.pad(flat, (0, rows * LANES - n)).reshape(rows, LANES)


def _unpack_rows(packed, like):
    flat, out, off = packed.reshape(-1), [], 0
    for g in like:
        out.append(flat[off:off + g.size].reshape(g.shape))
        off += g.size
    return out


_HBM = pl.BlockSpec(memory_space=pltpu.HBM)
_SEM = pl.BlockSpec(memory_space=pltpu.SEMAPHORE)
_SIDE_EFFECT = pltpu.SideEffectType.DATAFLOW_SIDE_EFFECTING
_N_PEERS = N_DEV - 1


def _peer(k):
    x, y, c = _my_place()
    return x ^ ((k >> 2) & 1), y ^ ((k >> 1) & 1), c ^ (k & 1)


def _exchange_start(src, land_shape, gather, name, after=()):
    def body(src_ref, land_ref, *rest):
        send_sems, recv_sems, src_thru, land_thru, token = rest[len(after):]
        x, y, c = _my_place()
        me = 4 * x + 2 * y + c
        for k in range(1, N_DEV):
            px, py, pc = _peer(k)
            pltpu.make_async_remote_copy(
                src_ref=src_ref if gather else src_ref.at[4 * px + 2 * py + pc],
                dst_ref=land_ref.at[me] if gather else land_ref.at[k - 1],
                send_sem=send_sems.at[k - 1],
                recv_sem=recv_sems.at[k - 1],
                device_id=(px, py, pc),
                device_id_type=pl.DeviceIdType.MESH,
            ).start()
        token[...] = jnp.zeros_like(token)

    return pl.pallas_call(
        body,
        name=name,
        out_shape=(
            pltpu.SemaphoreType.DMA((_N_PEERS,)),
            pltpu.SemaphoreType.DMA((_N_PEERS,)),
            pltpu.HBM(src.shape, src.dtype),
            pltpu.HBM(land_shape, src.dtype),
            jax.ShapeDtypeStruct((8, LANES), F32),
        ),
        in_specs=(_HBM, _HBM) + (pl.BlockSpec(memory_space=pl.ANY),) * len(after),
        out_specs=(_SEM, _SEM, _HBM, _HBM, pl.BlockSpec(memory_space=pltpu.VMEM)),
        input_output_aliases={0: 2, 1: 3},
        compiler_params=pltpu.CompilerParams(has_side_effects=_SIDE_EFFECT),
    )(pltpu.with_memory_space_constraint(src, pltpu.HBM),
      pltpu.with_memory_space_constraint(lax.empty(land_shape, src.dtype), pltpu.HBM), *after)


def _gather_start_all(shards, name):
    n = len(shards)

    def body(*refs):
        srcs, lands = refs[:n], refs[n:2 * n]
        outs = refs[2 * n:]
        send_sems, recv_sems, token = outs[:n], outs[n:2 * n], outs[-1]
        x, y, c = _my_place()
        me = 4 * x + 2 * y + c
        for i in range(n):
            for k in range(1, N_DEV):
                pltpu.make_async_remote_copy(
                    src_ref=srcs[i],
                    dst_ref=lands[i].at[me],
                    send_sem=send_sems[i].at[k - 1],
                    recv_sem=recv_sems[i].at[k - 1],
                    device_id=_peer(k),
                    device_id_type=pl.DeviceIdType.MESH,
                ).start()
        token[...] = jnp.zeros_like(token)

    lands = [(N_DEV,) + s.shape for s in shards]
    sems = tuple(pltpu.SemaphoreType.DMA((_N_PEERS,)) for _ in range(2 * n))
    res = pl.pallas_call(
        body,
        name=name,
        out_shape=sems + tuple(pltpu.HBM(s.shape, s.dtype) for s in shards)
        + tuple(pltpu.HBM(ls, s.dtype) for ls, s in zip(lands, shards)) + (jax.ShapeDtypeStruct((8, LANES), F32),),
        in_specs=(_HBM,) * (2 * n),
        out_specs=(_SEM,) * (2 * n) + (_HBM,) * (2 * n) + (pl.BlockSpec(memory_space=pltpu.VMEM),),
        input_output_aliases={i: 2 * n + i for i in range(2 * n)},
        compiler_params=pltpu.CompilerParams(has_side_effects=_SIDE_EFFECT),
    )(*[pltpu.with_memory_space_constraint(s, pltpu.HBM) for s in shards],
      *[pltpu.with_memory_space_constraint(lax.empty(ls, s.dtype), pltpu.HBM) for ls, s in zip(lands, shards)])
    return [(res[i], res[n + i], res[2 * n + i], res[3 * n + i], res[-1]) for i in range(n)]


def _exchange_wait(handle, gather, after, name):
    send_sems, recv_sems, src_thru, land_thru, _ = handle

    def body(src_ref, land_ref, send_sems, recv_sems, after_ref, src_dead, got_ref):
        for k in range(1, N_DEV):
            cp = pltpu.make_async_remote_copy(
                src_ref=src_ref if gather else src_ref.at[k],
                dst_ref=land_ref.at[k - 1],
                send_sem=send_sems.at[k - 1],
                recv_sem=recv_sems.at[k - 1],
                device_id=_peer(k),
                device_id_type=pl.DeviceIdType.MESH,
            )
            cp.wait_send()
            cp.wait_recv()

    return pl.pallas_call(
        body,
        name=name,
        out_shape=(pltpu.HBM(src_thru.shape, src_thru.dtype), pltpu.HBM(land_thru.shape, land_thru.dtype)),
        in_specs=(_HBM, _HBM, _SEM, _SEM, pl.BlockSpec(memory_space=pl.ANY)),
        out_specs=(_HBM, _HBM),
        input_output_aliases={0: 0, 1: 1},
        compiler_params=pltpu.CompilerParams(has_side_effects=_SIDE_EFFECT),
    )(src_thru, land_thru, send_sems, recv_sems, after)[1]


def _sum_own_and_peers(own, land, name):
    r, c = own.shape
    tr = _pick(r, (256, 128, 64, 32, 16, 8))

    def body(o_ref, l_ref, out_ref):
        s = [l_ref[j] for j in range(_N_PEERS)]
        out_ref[...] = ((o_ref[...] + s[0]) + (s[1] + s[2])) + ((s[3] + s[4]) + (s[5] + s[6]))

    return pl.pallas_call(
        body,
        name=name,
        grid=(r // tr,),
        in_specs=[pl.BlockSpec((tr, c), lambda i: (i, 0)), pl.BlockSpec((_N_PEERS, tr, c), lambda i: (0, i, 0))],
        out_specs=pl.BlockSpec((tr, c), lambda i: (i, 0)),
        out_shape=jax.ShapeDtypeStruct((r, c), own.dtype),
        compiler_params=_params("parallel"),
    )(own, land)


class _Cols:
    def __init__(self, array, width, block):
        self.array, self.width, self.block = array, width, block
        self.shape, self.dtype = (array.shape[0], width), array.dtype


def _base(a):
    return a.array if isinstance(a, _Cols) else a


def _col_block(a):
    return a.block if isinstance(a, _Cols) else 0


def _make_rowwise(f, name, n_rows, n_tabs, n_pars):
    n_in = n_rows + n_tabs + n_pars

    def specs(args, tm):
        blocked = [pl.BlockSpec((tm, a.shape[1]), lambda i, blk=_col_block(a): (i, blk)) for a in args[: n_rows + n_tabs]]
        whole = [pl.BlockSpec(a.shape, lambda i: (0, 0)) for a in args[n_rows + n_tabs:]]
        return blocked + whole

    def out_struct(args, tm):
        blk = [jax.ShapeDtypeStruct((tm, a.shape[1]), a.dtype) for a in args[: n_rows + n_tabs]]
        blk += [jax.ShapeDtypeStruct(a.shape, a.dtype) for a in args[n_rows + n_tabs:]]
        return jax.eval_shape(f, *blk)

    def fwd_call(*args):
        r = args[0].shape[0]
        tm = _row_tile(r, max(a.shape[1] for a in args[:n_rows]))
        ro, so = out_struct(args, tm)

        def body(*refs):
            vals = [x[...] for x in refs[:n_in]]
            outs = refs[n_in:]
            rv, sv = f(*vals)
            for o, v in zip(outs[: len(ro)], rv):
                o[...] = v
            for o, v in zip(outs[len(ro):], sv):
                @pl.when(pl.program_id(0) == 0)
                def _(o=o, v=v):
                    o[...] = v

                @pl.when(pl.program_id(0) != 0)
                def _(o=o, v=v):
                    o[...] += v

        out_shape = [jax.ShapeDtypeStruct((r, s.shape[1]), s.dtype) for s in ro]
        out_shape += [jax.ShapeDtypeStruct(s.shape, s.dtype) for s in so]
        out_specs = [pl.BlockSpec((tm, s.shape[1]), lambda i: (i, 0)) for s in ro]
        out_specs += [pl.BlockSpec(s.shape, lambda i: (0, 0)) for s in so]
        res = pl.pallas_call(
            body,
            name=name + "_fwd",
            grid=(r // tm,),
            in_specs=specs(args, tm),
            out_specs=out_specs,
            out_shape=out_shape,
            compiler_params=_params("arbitrary" if so else "parallel"),
        )(*[_base(a) for a in args])
        return tuple(res[: len(ro)]), tuple(res[len(ro):])

    def bwd_call(args, cots, more=(), row_dtypes=None):
        r = args[0].shape[0]
        tm = _row_tile(r, max(a.shape[1] for a in args[:n_rows]))
        ro, so = out_struct(args, tm)
        crow, csum = cots
        rows, tabs, pars = args[:n_rows], args[n_rows:n_rows + n_tabs], args[n_rows + n_tabs:]
        n_c = len(crow) + len(csum)

        def body(*refs):
            vals = [x[...] for x in refs[:n_in]]
            cv = [x[...] for x in refs[n_in:n_in + n_c]]
            for x in refs[n_in + n_c:n_in + n_c + len(more)]:
                cv[0] = cv[0] + x[...]
            outs = refs[n_in + n_c + len(more):]
            tv = vals[n_rows:n_rows + n_tabs]

            def g(*dargs):
                return f(*dargs[:n_rows], *tv, *dargs[n_rows:])

            _, vjp = jax.vjp(g, *vals[:n_rows], *vals[n_rows + n_tabs:])
            d = vjp((tuple(cv[: len(crow)]), tuple(cv[len(crow):])))
            for o, v in zip(outs[:n_rows], d[:n_rows]):
                o[...] = v.astype(o.dtype)
            for o, v in zip(outs[n_rows:], d[n_rows:]):
                @pl.when(pl.program_id(0) == 0)
                def _(o=o, v=v):
                    o[...] = v

                @pl.when(pl.program_id(0) != 0)
                def _(o=o, v=v):
                    o[...] += v

        in_specs = specs(args, tm)
        in_specs += [pl.BlockSpec((tm, c.shape[1]), lambda i: (i, 0)) for c in crow]
        in_specs += [pl.BlockSpec(c.shape, lambda i: (0, 0)) for c in csum]
        in_specs += [pl.BlockSpec((tm, c.shape[1]), lambda i: (i, 0)) for c in more]
        out_shape = [jax.ShapeDtypeStruct(a.shape, dt) for a, dt in zip(rows, row_dtypes or [a.dtype for a in rows])]
        out_shape += [jax.ShapeDtypeStruct(a.shape, a.dtype) for a in pars]
        out_specs = [pl.BlockSpec((tm, a.shape[1]), lambda i: (i, 0)) for a in rows]
        out_specs += [pl.BlockSpec(a.shape, lambda i: (0, 0)) for a in pars]
        res = pl.pallas_call(
            body,
            name=name + "_bwd",
            grid=(r // tm,),
            in_specs=in_specs,
            out_specs=out_specs,
            out_shape=out_shape,
            compiler_params=_params("arbitrary" if pars else "parallel"),
        )(*[_base(a) for a in args], *crow, *csum, *more)
        return tuple(res[:n_rows]), tuple(res[n_rows:])

    @jax.custom_vjp
    def op(rows, tabs, pars):
        return fwd_call(*rows, *tabs, *pars)

    op.fwd_call, op.bwd_call = fwd_call, bwd_call

    def fwd(rows, tabs, pars):
        return fwd_call(*rows, *tabs, *pars), (rows, tabs, pars)

    def bwd(res, cots):
        rows, tabs, pars = res
        drows, dpars = bwd_call(tuple(rows) + tuple(tabs) + tuple(pars), cots)
        return drows, tuple(jnp.zeros_like(t) for t in tabs), dpars

    op.defvjp(fwd, bwd)
    return op


def _sigmoid(x):
    return 0.5 * (jnp.tanh(0.5 * x) + 1.0)


@jax.custom_jvp
def _softplus(x):
    e = jnp.exp(-jnp.abs(x))
    u = 1.0 + e
    log1p_e = jnp.where(u == 1.0, e, e * jnp.log(u) / jnp.where(u == 1.0, 1.0, u - 1.0))
    return jnp.maximum(x, 0.0) + log1p_e


@_softplus.defjvp
def _softplus_jvp(primals, tangents):
    (x,), (t,) = primals, tangents
    return _softplus(x), t * _sigmoid(x)


def _gelu(x):
    return 0.5 * x * (1.0 + jnp.tanh(math.sqrt(2.0 / math.pi) * (x + 0.044715 * (x * x * x))))


def _ln_res_f(h, mix, g, b):
    z = DN_ALPHA * h + mix
    mu = jnp.mean(z, axis=-1, keepdims=True)
    zc = z - mu
    var = jnp.mean(zc * zc, axis=-1, keepdims=True)
    return (zc * lax.rsqrt(var + EPS) * g + b,), ()


def _ln_res_copy_f(h, mix, g, b):
    (out,), _ = _ln_res_f(h, mix, g, b)
    return (out, out.astype(BF16)), ()


def _rmsnorm_f(x, g):
    return (x * lax.rsqrt(jnp.mean(x * x, axis=-1, keepdims=True) + EPS) * g,), ()


def _lru_gates_f(ga, gx, xc, b_a, b_x, lam):
    r = _sigmoid(ga + b_a)
    i = _sigmoid(gx + b_x)
    log_a = -LRU_C * r * _softplus(-lam)
    a = jnp.exp(log_a)
    one_minus_a2 = jnp.tanh(-log_a) * (jnp.exp(2.0 * log_a) + 1.0)
    return (a, jnp.sqrt(one_minus_a2) * (i * xc)), ()


def _lru_out_f(hh, p_gate):
    return (hh * _gelu(p_gate),), ()


def _rope_ret_f(q, k, cos2, sin2):
    d = cos2.shape[1]
    half = d // 2
    k_scale = d ** -0.5

    def rope(x):
        outs = []
        for h in range(x.shape[1] // d):
            xh = x[:, h * d:(h + 1) * d]
            rot = jnp.concatenate([xh[:, half:], xh[:, :half]], axis=1)
            outs.append(xh * cos2 + rot * sin2)
        return jnp.concatenate(outs, axis=1)

    return (rope(q), rope(k) * k_scale), ()


def _ret_out_f(o, g):
    d = o.shape[1] // RET_HEADS
    outs = []
    for h in range(RET_HEADS):
        oh = o[:, h * d:(h + 1) * d]
        outs.append(oh * lax.rsqrt(jnp.mean(oh * oh, axis=-1, keepdims=True) + EPS))
    y = jnp.concatenate(outs, axis=1)
    return (g * _sigmoid(g) * y,), ()


def _ret_out_bf16_f(o, g):
    (y,), _ = _ret_out_f(o, g)
    return (y.astype(BF16),), ()


def _loss_f(y, t, mask):
    e = (y - t) * mask
    per_row = jnp.sum(e * e, axis=-1, keepdims=True) * (0.5 / y.shape[1])
    total = jnp.sum(per_row, axis=0, keepdims=True)
    return (), (jnp.broadcast_to(total, (1, LANES)),)


def _shift_down(x, s):
    if s == 0:
        return x
    t = x.shape[0]
    row = lax.broadcasted_iota(jnp.int32, x.shape, 0)
    return jnp.where(row >= s, pltpu.roll(x, s, 0), 0.0)


def _shift_up(x, s):
    if s == 0:
        return x
    t = x.shape[0]
    row = lax.broadcasted_iota(jnp.int32, x.shape, 0)
    return jnp.where(row < t - s, pltpu.roll(x, t - s, 0), 0.0)


def _conv_fwd(x, w, b, name):
    bsz, t, c = x.shape
    width = w.shape[0]

    def body(x_ref, w_ref, b_ref, y_ref):
        xv = x_ref[0]
        acc = jnp.broadcast_to(b_ref[...], xv.shape)
        for k in range(width):
            acc = acc + w_ref[k:k + 1, :] * _shift_down(xv, width - 1 - k)
        y_ref[0] = acc

    return pl.pallas_call(
        body,
        name=name,
        grid=(bsz, c // LANES),
        in_specs=[
            pl.BlockSpec((1, t, LANES), lambda i, j: (i, 0, j)),
            pl.BlockSpec((width, LANES), lambda i, j: (0, j)),
            pl.BlockSpec((1, LANES), lambda i, j: (0, j)),
        ],
        out_specs=pl.BlockSpec((1, t, LANES), lambda i, j: (i, 0, j)),
        out_shape=jax.ShapeDtypeStruct(x.shape, F32),
        compiler_params=_params("parallel", "parallel"),
    )(x, w, b)


def _conv_bwd(x, w, dy, name):
    bsz, t, c = x.shape
    width = w.shape[0]

    def body(x_ref, w_ref, dy_ref, dx_ref, dw_ref, db_ref):
        xv, g = x_ref[0], dy_ref[0]
        dx = jnp.zeros_like(xv)
        dws = []
        for k in range(width):
            s = width - 1 - k
            dx = dx + w_ref[k:k + 1, :] * _shift_up(g, s)
            dws.append(jnp.sum(g * _shift_down(xv, s), axis=0, keepdims=True))
        dx_ref[0] = dx
        dw = jnp.concatenate(dws, axis=0)
        db = jnp.sum(g, axis=0, keepdims=True)

        @pl.when(pl.program_id(1) == 0)
        def _():
            dw_ref[...] = dw
            db_ref[...] = db

        @pl.when(pl.program_id(1) != 0)
        def _():
            dw_ref[...] += dw
            db_ref[...] += db

    return pl.pallas_call(
        body,
        name=name,
        grid=(c // LANES, bsz),
        in_specs=[
            pl.BlockSpec((1, t, LANES), lambda j, i: (i, 0, j)),
            pl.BlockSpec((width, LANES), lambda j, i: (0, j)),
            pl.BlockSpec((1, t, LANES), lambda j, i: (i, 0, j)),
        ],
        out_specs=[
            pl.BlockSpec((1, t, LANES), lambda j, i: (i, 0, j)),
            pl.BlockSpec((width, LANES), lambda j, i: (0, j)),
            pl.BlockSpec((1, LANES), lambda j, i: (0, j)),
        ],
        out_shape=[
            jax.ShapeDtypeStruct(x.shape, F32),
            jax.ShapeDtypeStruct(w.shape, F32),
            jax.ShapeDtypeStruct((1, c), F32),
        ],
        compiler_params=_params("parallel", "arbitrary"),
    )(x, w, dy)


def _make_conv(name):
    @jax.custom_vjp
    def op(x, w, b):
        return _conv_fwd(x, w, b, name + "_fwd")

    def fwd(x, w, b):
        return op(x, w, b), (x, w)

    def bwd(res, dy):
        x, w = res
        return tuple(_conv_bwd(x, w, dy, name + "_bwd"))

    op.defvjp(fwd, bwd)
    return op


_SCAN_ROWS = 8


def _scan_fwd(a, b, name):
    bsz, t, c = a.shape
    cw = _pick(c, (4 * LANES, 2 * LANES, LANES))

    def body(a_ref, b_ref, h_ref):
        row = lax.broadcasted_iota(jnp.int32, (_SCAN_ROWS, cw), 0)

        def step(i, carry):
            r0 = pl.multiple_of(i * _SCAN_ROWS, _SCAN_ROWS)
            av, bv = a_ref[0, pl.ds(r0, _SCAN_ROWS), :], b_ref[0, pl.ds(r0, _SCAN_ROWS), :]
            for s in (1, 2, 4):
                a_sh = jnp.where(row >= s, pltpu.roll(av, s, 0), 1.0)
                b_sh = jnp.where(row >= s, pltpu.roll(bv, s, 0), 0.0)
                bv = av * b_sh + bv
                av = av * a_sh
            hv = bv + av * carry
            h_ref[0, pl.ds(r0, _SCAN_ROWS), :] = hv
            return hv[_SCAN_ROWS - 1:, :]

        lax.fori_loop(0, t // _SCAN_ROWS, step, jnp.zeros((1, cw), F32), unroll=2)

    spec = pl.BlockSpec((1, t, cw), lambda i, j: (i, 0, j))
    return pl.pallas_call(
        body,
        name=name,
        grid=(bsz, c // cw),
        in_specs=[spec, spec],
        out_specs=spec,
        out_shape=jax.ShapeDtypeStruct(a.shape, F32),
        compiler_params=_params("parallel", "parallel"),
    )(a, b)


def _scan_bwd(a, h, g, name):
    bsz, t, c = a.shape
    cw = _pick(c, (2 * LANES, LANES))

    def body(a_ref, h_ref, g_ref, da_ref, db_ref):
        rows = _SCAN_ROWS
        row = lax.broadcasted_iota(jnp.int32, (rows, cw), 0)
        n_tiles = t // rows

        def step(n, carry):
            lam_next, a_next = carry
            i = n_tiles - 1 - n
            r0 = pl.multiple_of(i * rows, rows)
            rp = pl.multiple_of(jnp.maximum(i - 1, 0) * rows, rows)
            av, gv, hv = a_ref[0, pl.ds(r0, rows), :], g_ref[0, pl.ds(r0, rows), :], h_ref[0, pl.ds(r0, rows), :]
            h_before = jnp.where(i > 0, h_ref[0, pl.ds(rp, rows), :][rows - 1:, :], 0.0)
            cv = jnp.where(row < rows - 1, pltpu.roll(av, rows - 1, 0), a_next)
            for s in (1, 2, 4):
                c_sh = jnp.where(row < rows - s, pltpu.roll(cv, rows - s, 0), 1.0)
                g_sh = jnp.where(row < rows - s, pltpu.roll(gv, rows - s, 0), 0.0)
                gv = cv * g_sh + gv
                cv = cv * c_sh
            lam = gv + cv * lam_next
            db_ref[0, pl.ds(r0, rows), :] = lam
            da_ref[0, pl.ds(r0, rows), :] = lam * jnp.where(row >= 1, pltpu.roll(hv, 1, 0), h_before)
            return lam[:1, :], av[:1, :]

        zero = jnp.zeros((1, cw), F32)
        lax.fori_loop(0, n_tiles, step, (zero, zero), unroll=2)

    spec = pl.BlockSpec((1, t, cw), lambda i, j: (i, 0, j))
    return pl.pallas_call(
        body,
        name=name,
        grid=(bsz, c // cw),
        in_specs=[spec, spec, spec],
        out_specs=[spec, spec],
        out_shape=[jax.ShapeDtypeStruct(a.shape, F32)] * 2,
        compiler_params=_params("parallel", "parallel"),
    )(a, h, g)


def _make_scan(name):
    @jax.custom_vjp
    def op(a, b):
        return _scan_fwd(a, b, name + "_fwd")

    def fwd(a, b):
        h = op(a, b)
        return h, (a, h)

    def bwd(res, g):
        a, h = res
        da, db = _scan_bwd(a, h, g, name + "_bwd")
        return da, db

    op.defvjp(fwd, bwd)
    return op


def _query_blocks(t):
    blocks, start = [], 0
    while start < t:
        rows = 2 * SEQ_BLOCK if start + 2 * SEQ_BLOCK <= t else SEQ_BLOCK
        blocks.append((start, rows))
        start += rows
    return blocks


def _attn_exp(q, k, start, scale):
    tq, tk = q.shape[0], k.shape[0]
    s = _dot(q, k, "nt") * scale
    qpos = start + lax.broadcasted_iota(jnp.int32, (tq, tk), 0)
    kpos = lax.broadcasted_iota(jnp.int32, (tq, tk), 1)
    s = jnp.where(kpos <= qpos, s, NEG_INF)
    e = jnp.exp(s - jnp.max(s, axis=-1, keepdims=True))
    return e, 1.0 / jnp.sum(e, axis=-1, keepdims=True)


_MLA_SCALE = (MLA_NOPE + MLA_ROPE) ** -0.5


def _attn_specs(t):
    head = pl.BlockSpec((1, t, LANES), lambda b, h: (b, 0, h))
    shared = pl.BlockSpec((1, t, LANES), lambda b, h: (b, 0, 0))
    return head, shared


def _attn_fwd(q, kv, kpe, name):
    bsz, t, hl = q.shape
    head, shared = _attn_specs(t)

    def body(q_ref, kv_ref, kpe_ref, o_ref, k_s, v_s):
        lane = lax.broadcasted_iota(jnp.int32, (t, LANES), 1)
        kvh = kv_ref[0]
        k_s[...] = jnp.where(lane < MLA_NOPE, kvh, kpe_ref[0]).astype(BF16)
        v_s[...] = kvh.astype(BF16)
        for start, rows in _query_blocks(t):
            n = start + rows
            e, inv_l = _attn_exp(q_ref[0, start:n, :], k_s[:n, :], start, _MLA_SCALE)
            o_ref[0, start:n, :] = _dot(e, v_s[:n, :], "nn") * inv_l

    return pl.pallas_call(
        body,
        name=name,
        grid=(bsz, hl // LANES),
        in_specs=[head, head, shared],
        out_specs=head,
        out_shape=jax.ShapeDtypeStruct(q.shape, F32),
        scratch_shapes=[pltpu.VMEM((t, LANES), BF16), pltpu.VMEM((t, LANES), BF16)],
        compiler_params=_params("parallel", "parallel"),
    )(q, kv, kpe)


def _attn_bwd(q, kv, kpe, do, name):
    bsz, t, hl = q.shape
    head, shared = _attn_specs(t)

    def body(q_ref, kv_ref, kpe_ref, do_ref, dq_ref, dkv_ref, dkpe_ref, k_s, v_s, dk_s, dv_s):
        lane = lax.broadcasted_iota(jnp.int32, (t, LANES), 1)
        kvh = kv_ref[0]
        k_s[...] = jnp.where(lane < MLA_NOPE, kvh, kpe_ref[0]).astype(BF16)
        v_s[...] = kvh.astype(BF16)
        for start, rows in reversed(_query_blocks(t)):
            n = start + rows
            qb = q_ref[0, start:n, :]
            dob = jnp.where(lane[:rows] >= MLA_NOPE, do_ref[0, start:n, :], 0.0)
            kk, vv = k_s[:n, :], v_s[:n, :]
            e, inv_l = _attn_exp(qb, kk, start, _MLA_SCALE)
            p = e * inv_l
            dp = _dot(dob, vv, "nt")
            ds = p * (dp - jnp.sum(dp * p, axis=-1, keepdims=True)) * _MLA_SCALE
            dq_ref[0, start:n, :] = _dot(ds, kk, "nn")
            if n == t:
                dk_s[...] = _dot(ds, qb, "tn")
                dv_s[...] = _dot(p, dob, "tn")
            else:
                dk_s[:n, :] += _dot(ds, qb, "tn")
                dv_s[:n, :] += _dot(p, dob, "tn")
        dk = dk_s[...]
        dkv_ref[0] = jnp.where(lane < MLA_NOPE, dk, dv_s[...])
        dkpe = jnp.where(lane >= MLA_NOPE, dk, 0.0)

        @pl.when(pl.program_id(1) == 0)
        def _():
            dkpe_ref[0] = dkpe

        @pl.when(pl.program_id(1) != 0)
        def _():
            dkpe_ref[0] += dkpe

    return pl.pallas_call(
        body,
        name=name,
        grid=(bsz, hl // LANES),
        in_specs=[head, head, shared, head],
        out_specs=[head, head, shared],
        out_shape=[
            jax.ShapeDtypeStruct(q.shape, F32),
            jax.ShapeDtypeStruct(kv.shape, F32),
            jax.ShapeDtypeStruct(kpe.shape, F32),
        ],
        scratch_shapes=[pltpu.VMEM((t, LANES), BF16), pltpu.VMEM((t, LANES), BF16),
                        pltpu.VMEM((t, LANES), F32), pltpu.VMEM((t, LANES), F32)],
        compiler_params=_params("parallel", "arbitrary"),
    )(q, kv, kpe, do)


def _make_attention(name):
    @jax.custom_vjp
    def op(q, kv, kpe):
        return _attn_fwd(q, kv, kpe, name + "_fwd")

    def fwd(q, kv, kpe):
        return op(q, kv, kpe), (q, kv, kpe)

    def bwd(res, do):
        return tuple(_attn_bwd(*res, do, name + "_bwd"))

    op.defvjp(fwd, bwd)
    return op


_ROPE_SHIFT = MLA_ROPE // 2


def _rope_lanes_call(x, c, sm, sp, transpose, name):
    r, width = x.shape
    tm = _row_tile(r, width)

    def body(x_ref, c_ref, sm_ref, sp_ref, y_ref):
        cv, smv, spv = c_ref[...], sm_ref[...], sp_ref[...]
        for b in range(width // LANES):
            xb = x_ref[:, b * LANES:(b + 1) * LANES]
            if transpose:
                yb = xb * cv + pltpu.roll(xb * smv, _ROPE_SHIFT, 1) + pltpu.roll(xb * spv, LANES - _ROPE_SHIFT, 1)
            else:
                yb = xb * cv + pltpu.roll(xb, LANES - _ROPE_SHIFT, 1) * smv + pltpu.roll(xb, _ROPE_SHIFT, 1) * spv
            y_ref[:, b * LANES:(b + 1) * LANES] = yb

    tab = pl.BlockSpec((tm, LANES), lambda i: (i, 0))
    blk = pl.BlockSpec((tm, width), lambda i: (i, 0))
    return pl.pallas_call(
        body,
        name=name,
        grid=(r // tm,),
        in_specs=[blk, tab, tab, tab],
        out_specs=blk,
        out_shape=jax.ShapeDtypeStruct(x.shape, F32),
        compiler_params=_params("parallel"),
    )(x, c, sm, sp)


def _make_rope_lanes(name):
    @jax.custom_vjp
    def op(x, c, sm, sp):
        return _rope_lanes_call(x, c, sm, sp, False, name + "_fwd")

    def fwd(x, c, sm, sp):
        return op(x, c, sm, sp), (c, sm, sp)

    def bwd(res, dy):
        c, sm, sp = res
        return _rope_lanes_call(dy, c, sm, sp, True, name + "_bwd"), jnp.zeros_like(c), jnp.zeros_like(sm), jnp.zeros_like(sp)

    op.defvjp(fwd, bwd)
    return op


def _ret_chunk_rows(t):
    return t // 4 if t % 32 == 0 else SEQ_BLOCK


def _ret_decays(c, log_gamma):
    row = lax.broadcasted_iota(jnp.int32, (c, 1), 0)
    col = lax.broadcasted_iota(jnp.int32, (1, c), 1)
    rowf = row.astype(F32)
    d = jnp.where(row >= col, jnp.exp(log_gamma * rowf) * jnp.exp(-log_gamma * col.astype(F32)), 0.0)
    return d, jnp.exp(log_gamma * (rowf + 1.0)), jnp.exp(log_gamma * (c - 1.0 - rowf)), jnp.exp(log_gamma * c)


def _ret_specs(c, dk, dv, v_block0, n_chunks, reverse):
    pos = (lambda i: n_chunks - 1 - i) if reverse else (lambda i: i)
    return (
        pl.BlockSpec(memory_space=pltpu.SMEM),
        pl.BlockSpec((1, c, dk), lambda b, h, i: (b, pos(i), h)),
        pl.BlockSpec((1, c, dv), lambda b, h, i: (b, pos(i), h + v_block0)),
        pl.BlockSpec((1, c, dv), lambda b, h, i: (b, pos(i), h)),
        pl.BlockSpec((1, 1, dk, dv), lambda b, h, i: (b, h * n_chunks + pos(i), 0, 0)),
    )


def _ret_fwd(lg, q, k, v, name, dv=None, v_block0=0):
    bsz, t, hdk = q.shape
    heads = lg.shape[0]
    dk, dv = hdk // heads, dv or v.shape[2] // heads
    c = _ret_chunk_rows(t)
    n_chunks = t // c
    lg_spec, qk_spec, v_spec, o_spec, s_spec = _ret_specs(c, dk, dv, v_block0, n_chunks, False)

    def body(lg_ref, q_ref, k_ref, v_ref, o_ref, s_ref, state):
        @pl.when(pl.program_id(2) == 0)
        def _():
            state[...] = jnp.zeros((dk, dv), F32)

        d, a, b, g = _ret_decays(c, lg_ref[pl.program_id(1)])
        qb, kb, vb, s_in = q_ref[0], k_ref[0], v_ref[0], state[...]
        s_ref[0, 0] = s_in
        o_ref[0] = _dot(_dot(qb, kb, "nt") * d, vb, "nn") + a * _dot(qb, s_in, "nn")
        state[...] = g * s_in + _dot(kb * b, vb, "tn")

    return pl.pallas_call(
        body,
        name=name,
        grid=(bsz, heads, n_chunks),
        in_specs=[lg_spec, qk_spec, qk_spec, v_spec],
        out_specs=[o_spec, s_spec],
        out_shape=[jax.ShapeDtypeStruct((bsz, t, heads * dv), F32),
                   jax.ShapeDtypeStruct((bsz, heads * n_chunks, dk, dv), F32)],
        scratch_shapes=[pltpu.VMEM((dk, dv), F32)],
        compiler_params=_params("parallel", "parallel", "arbitrary"),
    )(lg, q, k, v)


def _ret_bwd(lg, q, k, v, states, do, name, v_block0=0, dv_dtype=F32):
    bsz, t, hdk = q.shape
    heads = lg.shape[0]
    dk, dv = hdk // heads, do.shape[2] // heads
    c = _ret_chunk_rows(t)
    n_chunks = t // c
    lg_spec, qk_spec, v_spec, o_spec, s_spec = _ret_specs(c, dk, dv, v_block0, n_chunks, True)

    def body(lg_ref, q_ref, k_ref, v_ref, s_ref, do_ref, dq_ref, dk_ref, dv_ref, dstate):
        @pl.when(pl.program_id(2) == 0)
        def _():
            dstate[...] = jnp.zeros((dk, dv), F32)

        d, a, b, g = _ret_decays(c, lg_ref[pl.program_id(1)])
        qb, kb, vb, dob, s_in, ds_out = q_ref[0], k_ref[0], v_ref[0], do_ref[0], s_ref[0, 0], dstate[...]
        scores = _dot(qb, kb, "nt") * d
        dscores = _dot(dob, vb, "nt") * d
        dq_ref[0] = _dot(dscores, kb, "nn") + a * _dot(dob, s_in, "nt")
        dk_ref[0] = _dot(dscores, qb, "tn") + b * _dot(vb, ds_out, "nt")
        dv_ref[0] = (_dot(scores, dob, "tn") + _dot(kb * b, ds_out, "nn")).astype(dv_dtype)
        dstate[...] = g * ds_out + _dot(qb, a * dob, "tn")

    return pl.pallas_call(
        body,
        name=name,
        grid=(bsz, heads, n_chunks),
        in_specs=[lg_spec, qk_spec, qk_spec, v_spec, s_spec, o_spec],
        out_specs=[qk_spec, qk_spec, o_spec],
        out_shape=[
            jax.ShapeDtypeStruct(q.shape, F32),
            jax.ShapeDtypeStruct(k.shape, F32),
            jax.ShapeDtypeStruct(do.shape, dv_dtype),
        ],
        scratch_shapes=[pltpu.VMEM((dk, dv), F32)],
        compiler_params=_params("parallel", "parallel", "arbitrary"),
    )(lg, q, k, v, states, do)


def _adamw(w, g, m, v, name):
    r, c = w.shape
    tr = _pick(r, (256, 128, 64, 32, 16, 8))

    def body(w_ref, g_ref, m_ref, v_ref, d_ref, nm_ref, nv_ref):
        gv = g_ref[...]
        nm = ADAM_B1 * m_ref[...] + (1.0 - ADAM_B1) * gv
        nv = ADAM_B2 * v_ref[...] + (1.0 - ADAM_B2) * (gv * gv)
        m_hat = nm / (1.0 - ADAM_B1 ** ADAM_STEP)
        v_hat = nv / (1.0 - ADAM_B2 ** ADAM_STEP)
        d_ref[...] = -ADAM_LR * (m_hat / (jnp.sqrt(v_hat) + ADAM_EPS) + ADAM_WD * w_ref[...])
        nm_ref[...] = nm
        nv_ref[...] = nv

    spec = pl.BlockSpec((tr, c), lambda i: (i, 0))
    return pl.pallas_call(
        body,
        name=name,
        grid=(r // tr,),
        in_specs=[spec] * 4,
        out_specs=[spec] * 3,
        out_shape=[jax.ShapeDtypeStruct((r, c), F32)] * 3,
        compiler_params=_params("parallel"),
    )(w, g, m, v)


def _rope_tables(t, half, reps):
    inv = ROPE_BASE ** (-jnp.arange(half, dtype=F32) / half)
    ang = jnp.arange(t, dtype=jnp.int32).astype(F32)[:, None] * inv[None, :]
    return jnp.tile(jnp.cos(ang), (1, reps)), jnp.tile(jnp.sin(ang), (1, reps))


def _padded_len(seq):
    return -(-(N_META + seq) // SEQ_BLOCK) * SEQ_BLOCK


def _embed(meta, x):
    bsz, seq, d = x.shape
    t = _padded_len(seq)
    return jnp.concatenate(
        [jnp.broadcast_to(meta[None], (bsz, N_META, d)), x, jnp.zeros((bsz, t - N_META - seq, d), F32)], axis=1
    ).reshape(bsz * t, d)


def _even_mixer(p, conv_w, conv_b, w_rg_a, b_rg_a, w_rg_x, b_rg_x, lru_lambda, q_norm_g, uq_slot, kv_norm_g,
                ukv_slot, gathered, bsz):
    w_uq_pad, w_ukv_full = gathered
    r = p.shape[0]
    t = r // bsz

    def tile_rows(tab):
        return jnp.tile(tab, (bsz, 1))

    lru_w = w_rg_a.shape[2] * w_rg_a.shape[1]
    q_rank, kv_rank = q_norm_g.shape[1], kv_norm_g.shape[1]
    p_gate, p_rec, p_q, p_kv, p_kpe = _split_cols(
        p, (lru_w, 2 * lru_w, 2 * lru_w + q_rank, 2 * lru_w + q_rank + kv_rank))

    xc = _make_conv("conv")(p_rec.reshape(bsz, t, lru_w), conv_w, conv_b).reshape(r, lru_w)
    ga, gx = _make_gate_pair("rg")(xc, w_rg_a[0], w_rg_x[0])
    (a, bb), _ = _make_rowwise(_lru_gates_f, "lru_gates", 3, 0, 3)((ga, gx, xc), (), (b_rg_a, b_rg_x, lru_lambda))
    hh = _make_scan("lru_scan")(a.reshape(bsz, t, lru_w), bb.reshape(bsz, t, lru_w)).reshape(r, lru_w)
    (y_rec,), _ = _make_rowwise(_lru_out_f, "lru_out", 2, 0, 0)((hh, p_gate), (), ())

    (qn,), _ = _make_rowwise(_rmsnorm_f, "q_norm", 1, 0, 1)((p_q,), (), (q_norm_g,))
    (kvn,), _ = _make_rowwise(_rmsnorm_f, "kv_norm", 1, 0, 1)((p_kv,), (), (kv_norm_g,))
    q = _make_slot_linear("ev_uq")(qn, w_uq_pad, uq_slot)
    kv = _make_slot_linear("ev_ukv")(kvn, w_ukv_full, ukv_slot)
    half = MLA_ROPE // 2
    cos, sin = _rope_tables(t, half, 1)
    one, zero = jnp.ones((t, MLA_NOPE), F32), jnp.zeros((t, MLA_NOPE), F32)
    tail = LANES - MLA_NOPE - MLA_ROPE
    c_tab = tile_rows(jnp.concatenate([one, cos, cos, one[:, :tail]], axis=1))
    sm_tab = tile_rows(jnp.concatenate([zero, -sin, zero[:, :half + tail]], axis=1))
    sp_tab = tile_rows(jnp.concatenate([zero, zero[:, :half], sin, zero[:, :tail]], axis=1))
    q = _make_rope_lanes("rope_q")(q, c_tab, sm_tab, sp_tab)
    kpe = _make_rope_lanes("rope_k")(p_kpe, c_tab, sm_tab, sp_tab)
    o = _make_attention("mla")(q.reshape(bsz, t, -1), kv.reshape(bsz, t, -1), kpe.reshape(bsz, t, LANES))
    return jnp.concatenate([y_rec, o.reshape(r, -1)], axis=1)


def _odd_mixer_fwd(p, bsz):
    r, width = p.shape
    t = r // bsz
    qk = width // 6
    dk = qk // RET_HEADS
    cos2, sin2 = _rope_tables(t, dk // 2, 2)
    sin2 = jnp.concatenate([-sin2[:, :dk // 2], sin2[:, dk // 2:]], axis=1)
    rope_args = (_Cols(p, qk, 0), _Cols(p, qk, 1), jnp.tile(cos2, (bsz, 1)), jnp.tile(sin2, (bsz, 1)))
    (rq, rk), _ = _make_rowwise(_rope_ret_f, "rope_ret", 2, 2, 0).fwd_call(*rope_args)
    lg = jnp.log(1.0 - 2.0 ** (-5.0 - jnp.arange(RET_HEADS, dtype=F32)))
    ret_args = (lg, rq.reshape(bsz, t, qk), rk.reshape(bsz, t, qk), p.reshape(bsz, t, width))
    o, states = _ret_fwd(*ret_args, "ret_fwd", dv=2 * dk, v_block0=qk // dk)
    gate_args = (o.reshape(r, 2 * qk), _Cols(p, 2 * qk, 2))
    (y,), _ = _make_rowwise(_ret_out_bf16_f, "ret_out", 2, 0, 0).fwd_call(*gate_args)
    return y, (rope_args, ret_args + (states,), gate_args)


def _odd_mixer_bwd(res, dy):
    rope_args, ret_args, gate_args = res
    bsz, t, qk = ret_args[1].shape
    dk = qk // RET_HEADS
    (do, dg), _ = _make_rowwise(_ret_out_f, "ret_out", 2, 0, 0).bwd_call(
        gate_args, ((dy,), ()), row_dtypes=(F32, BF16))
    drq, drk, dv = _ret_bwd(*ret_args, do.reshape(bsz, t, 2 * qk), "ret_bwd", v_block0=qk // dk, dv_dtype=BF16)
    (dq, dkk), _ = _make_rowwise(_rope_ret_f, "rope_ret", 2, 2, 0).bwd_call(
        rope_args, ((drq.reshape(bsz * t, qk), drk.reshape(bsz * t, qk)), ()), row_dtypes=(BF16, BF16))
    return jnp.concatenate([dq, dkk, dv.reshape(bsz * t, 2 * qk), dg], axis=1)


def _local_loss(h, target):
    bsz, seq, d = target.shape
    t = _padded_len(seq)
    t_real = N_META + seq
    pos = jnp.arange(t, dtype=jnp.int32)
    mask = jnp.tile(((pos >= N_META) & (pos < t_real)).astype(F32)[:, None], (bsz, 1))
    tgt = jnp.concatenate(
        [jnp.zeros((bsz, N_META, d), F32), target, jnp.zeros((bsz, t - t_real, d), F32)], axis=1).reshape(bsz * t, d)
    _, (total,) = _make_rowwise(_loss_f, "loss", 1, 2, 0)((h,), (tgt, mask), ())
    return total[0, 0]


_WEIGHTS = ("meta_tokens", "ev_w_in", "ev_conv_w", "ev_conv_b", "ev_w_rg_a", "ev_b_rg_a", "ev_w_rg_x", "ev_b_rg_x",
            "ev_lru_lambda", "ev_q_norm_g", "ev_w_uq", "ev_kv_norm_g", "ev_w_ukv", "ev_w_out", "od_w_in", "od_w_out",
            "ln_mix_g", "ln_mix_b", "mlp_w1", "mlp_w2", "ln_mlp_g", "ln_mlp_b")


def kernel(x, meta_tokens, ev_w_in, ev_conv_w, ev_conv_b, ev_w_rg_a, ev_b_rg_a, ev_w_rg_x, ev_b_rg_x, ev_lru_lambda, ev_q_norm_g, ev_w_uq, ev_kv_norm_g, ev_w_ukv, ev_w_out, od_w_in, od_w_out, ln_mix_g, ln_mix_b, mlp_w1, mlp_w2, ln_mlp_g, ln_mlp_b, loss_target, m_meta_tokens, m_ev_w_in, m_ev_conv_w, m_ev_conv_b, m_ev_w_rg_a, m_ev_b_rg_a, m_ev_w_rg_x, m_ev_b_rg_x, m_ev_lru_lambda, m_ev_q_norm_g, m_ev_w_uq, m_ev_kv_norm_g, m_ev_w_ukv, m_ev_w_out, m_od_w_in, m_od_w_out, m_ln_mix_g, m_ln_mix_b, m_mlp_w1, m_mlp_w2, m_ln_mlp_g, m_ln_mlp_b, v_meta_tokens, v_ev_w_in, v_ev_conv_w, v_ev_conv_b, v_ev_w_rg_a, v_ev_b_rg_a, v_ev_w_rg_x, v_ev_b_rg_x, v_ev_lru_lambda, v_ev_q_norm_g, v_ev_w_uq, v_ev_kv_norm_g, v_ev_w_ukv, v_ev_w_out, v_od_w_in, v_od_w_out, v_ln_mix_g, v_ln_mix_b, v_mlp_w1, v_mlp_w2, v_ln_mlp_g, v_ln_mlp_b):
    args = locals()
    weights = {n: args[n] for n in _WEIGHTS}
    bsz = x.shape[0]
    my_x, my_y, my_c = _my_place()
    me = 4 * my_x + 2 * my_y + my_c

    big = (("ev_in", ev_w_in[0], True), ("ev_out", ev_w_out[0], False), ("mlp0_w1", mlp_w1[0], True),
           ("mlp0_w2", mlp_w2[0], False), ("od_in", od_w_in[0], True), ("od_out", od_w_out[0], False),
           ("mlp1_w1", mlp_w1[1], True), ("mlp1_w2", mlp_w2[1], False))
    small_sharded = (("meta", meta_tokens, F32), ("conv_w", ev_conv_w[0], F32), ("ev_uq", ev_w_uq[0], BF16),
                     ("ev_ukv", ev_w_ukv[0], BF16))
    to_gather = (tuple((nm, s.astype(dt), True) for nm, s, dt in small_sharded)
                 + tuple((nm, s.astype(BF16), cols) for nm, s, cols in big))
    handles = _gather_start_all([s for _, s, _ in to_gather], "ag_start")
    gathers = {nm: (s, cols, h) for (nm, s, cols), h in zip(to_gather, handles)}
    gather_tokens = (handles[0][4],)

    def full_weight(nm, after):
        shard16, cols, handle = gathers[nm]
        land = _exchange_wait(handle, True, after, "ag_wait_" + nm)
        land = lax.dynamic_update_index_in_dim(land, shard16, me, 0)
        if cols and shard16.shape[1] % LANES == 0:
            return land, True
        return (_unstack_cols(land) if cols else land.reshape(-1, shard16.shape[1])), False

    meta_full, conv_w_full, w_uq_full, w_ukv_full = (
        _unstack_cols(lax.dynamic_update_index_in_dim(
            _exchange_wait(gathers[nm][2], True, gather_tokens[-1], "ag_wait_" + nm), gathers[nm][0], me, 0))
        for nm, _, _ in small_sharded)

    pending = []

    def linear_bwd(nm, x_in, w, dy, cols, unpad=None, **fused):
        w_full, w_stacked = w
        own = None
        if w_stacked:
            stacked = _matmul(x_in, dy, "tn", nm + "_dw", stacked=True)
        else:
            dw = _matmul(x_in, dy, "tn", nm + "_dw")
            dw = dw if unpad is None else unpad(dw)
            n = dw.shape[1] // N_DEV
            if cols:
                stacked = _stack_cols(dw)
                own = lax.dynamic_slice_in_dim(dw, me * n, n, axis=1)
            else:
                stacked = dw.reshape(N_DEV, dw.shape[0] // N_DEV, dw.shape[1])
        handle = _exchange_start(stacked, (_N_PEERS,) + stacked.shape[1:], False, "rs_start_" + nm)
        if own is None:
            own = lax.dynamic_index_in_dim(handle[2], me, 0, keepdims=False)
        pending.append((nm, own, handle))
        return _matmul(dy, w_full, "nt", nm + "_dx", after=(handle[4],), stacked=w_stacked, **fused)

    def linear_fwd(nm, x_in, w, **fused):
        return _matmul(x_in, w[0], "nn", nm + "_fwd", stacked=w[1], **fused)

    def mlp_fwd(h, h16, l):
        w1 = full_weight(f"mlp{l}_w1", h16)
        a16 = linear_fwd(f"mlp{l}_w1", h16, w1, out_relu2=True, out_dtype=BF16)
        w2 = full_weight(f"mlp{l}_w2", a16)
        f = linear_fwd(f"mlp{l}_w2", a16, w2)
        ln_args = (h, f, ln_mlp_g[l:l + 1], ln_mlp_b[l:l + 1])
        return ln_fwd(f"mlp{l}_ln", *ln_args), (h16, w1, a16, w2, ln_args)

    def mlp_bwd(dout, res, l):
        h16, w1, a16, w2, ln_args = res
        dh, df, dg, db = ln_bwd(f"mlp{l}_ln", ln_args, dout)
        du = linear_bwd(f"mlp{l}_w2", a16, w2, df, False, relu2_bwd_of=a16, out_dtype=BF16)
        return (dh, linear_bwd(f"mlp{l}_w1", h16, w1, du, True)), dg, db

    def ln_fwd(nm, h, mix, g, b):
        return _make_rowwise(_ln_res_copy_f, nm, 2, 0, 2).fwd_call(h, mix, g, b)[0]

    def ln_bwd(nm, ln_args, pieces):
        (dh, dmix), (dg, db) = _make_rowwise(_ln_res_f, nm, 2, 0, 2).bwd_call(
            ln_args, ((pieces[0],), ()), more=tuple(pieces[1:]), row_dtypes=(F32, BF16))
        return dh, dmix, dg, db

    h0, vjp_embed = jax.vjp(_embed, meta_full, x)
    n_in = ev_w_in.shape[2] * N_DEV
    kpe0, pad_lo, pad_hi = n_in - MLA_ROPE, MLA_NOPE, LANES - MLA_NOPE - MLA_ROPE
    w_in = full_weight("ev_in", h0)[0]
    zeros_in = jnp.zeros((w_in.shape[0], pad_lo), BF16)
    w_ev_in = (jnp.concatenate([w_in[:, :kpe0], zeros_in, w_in[:, kpe0:], zeros_in[:, :pad_hi]], axis=1), False)

    def unpad_in(dw):
        return jnp.concatenate([dw[:, :kpe0], dw[:, kpe0 + pad_lo:kpe0 + pad_lo + MLA_ROPE]], axis=1)

    p0 = _matmul(h0, w_ev_in[0], "nn", "ev_in_fwd", after=gather_tokens)
    q_rank, d_head = w_uq_full.shape[0], MLA_NOPE + MLA_ROPE
    w_uq_pad = jnp.pad(w_uq_full.reshape(q_rank, MLA_HEADS, d_head), ((0, 0), (0, 0), (0, LANES - d_head)))
    w_uq_pad = w_uq_pad.reshape(q_rank, MLA_HEADS * LANES)
    small = (conv_w_full, ev_conv_b, ev_w_rg_a, ev_b_rg_a, ev_w_rg_x, ev_b_rg_x, ev_lru_lambda, ev_q_norm_g,
             jnp.zeros(w_uq_pad.shape, F32), ev_kv_norm_g, jnp.zeros(w_ukv_full.shape, F32))
    y0, vjp_even = jax.vjp(lambda p, *s: _even_mixer(p, *s, (w_uq_pad, w_ukv_full), bsz), p0, *small)
    w_out = full_weight("ev_out", y0)[0]
    lru_w, d_model = y0.shape[1] - MLA_HEADS * LANES, w_out.shape[1]
    w_att = w_out[lru_w:].reshape(MLA_HEADS, MLA_V, d_model)
    w_att = jnp.concatenate([jnp.zeros((MLA_HEADS, LANES - MLA_V, d_model), BF16), w_att], axis=1)
    w_ev_out = (jnp.concatenate([w_out[:lru_w], w_att.reshape(MLA_HEADS * LANES, d_model)], axis=0), False)

    def unpad_out(dw):
        d_att = dw[lru_w:].reshape(MLA_HEADS, LANES, d_model)[:, LANES - MLA_V:].reshape(MLA_HEADS * MLA_V, d_model)
        return jnp.concatenate([dw[:lru_w], d_att], axis=0)

    mix0 = linear_fwd("ev_out", y0, w_ev_out)
    ln0_args = (h0, mix0, ln_mix_g[0:1], ln_mix_b[0:1])
    h1, h1_16 = ln_fwd("mix0_ln", *ln0_args)
    (h2, h2_16), res_mlp0 = mlp_fwd(h1, h1_16, 0)
    w_od_in = full_weight("od_in", h2_16)
    p1 = linear_fwd("od_in", h2_16, w_od_in)
    y1, res_odd = _odd_mixer_fwd(p1, bsz)
    w_od_out = full_weight("od_out", y1)
    mix1 = linear_fwd("od_out", y1, w_od_out)
    ln1_args = (h2, mix1, ln_mix_g[1:2], ln_mix_b[1:2])
    h3, h3_16 = ln_fwd("mix1_ln", *ln1_args)
    (h4, _), res_mlp1 = mlp_fwd(h3, h3_16, 1)
    loss_local, vjp_loss = jax.vjp(lambda h: _local_loss(h, loss_target), h4)

    dh4 = vjp_loss(jnp.ones((), F32))
    dh3, dg_mlp1, db_mlp1 = mlp_bwd(dh4, res_mlp1, 1)
    dh2, dmix1, dg_mix1, db_mix1 = ln_bwd("mix1_ln", ln1_args, dh3)
    dp1 = _odd_mixer_bwd(res_odd, linear_bwd("od_out", y1, w_od_out, dmix1, False))
    dh2 = (dh2, linear_bwd("od_in", h2_16, w_od_in, dp1, True))
    dh1, dg_mlp0, db_mlp0 = mlp_bwd(dh2, res_mlp0, 0)
    dh0, dmix0, dg_mix0, db_mix0 = ln_bwd("mix0_ln", ln0_args, dh1)
    dp0, *dsmall = vjp_even(linear_bwd("ev_out", y0, w_ev_out, dmix0, False, unpad=unpad_out))
    dh0 = dh0 + linear_bwd("ev_in", h0, w_ev_in, dp0.astype(BF16), True, unpad=unpad_in)
    g_meta_full, grad_x = vjp_embed(dh0)
    (g_conv_w_full, g_conv_b, g_w_rg_a, g_b_rg_a, g_w_rg_x, g_b_rg_x, g_lambda, g_q_norm, g_uq_pad, g_kv_norm,
     g_ukv_full) = dsmall

    for nm, dw in (("ev_uq", g_uq_pad.reshape(q_rank, MLA_HEADS, LANES)[:, :, :d_head].reshape(q_rank, -1)),
                   ("ev_ukv", g_ukv_full)):
        n = dw.shape[1] // N_DEV
        handle = _exchange_start(_stack_cols(dw), (_N_PEERS, dw.shape[0], n), False, "rs_start_" + nm)
        pending.append((nm, lax.dynamic_slice_in_dim(dw, me * n, n, axis=1), handle))

    rep_names = ("ev_conv_b", "ev_w_rg_a", "ev_b_rg_a", "ev_w_rg_x", "ev_b_rg_x", "ev_lru_lambda", "ev_q_norm_g",
                 "ev_kv_norm_g", "ln_mix_g", "ln_mix_b", "ln_mlp_g", "ln_mlp_b")
    rep_local = (g_conv_b, g_w_rg_a, g_b_rg_a, g_w_rg_x, g_b_rg_x, g_lambda, g_q_norm, g_kv_norm,
                 jnp.concatenate([dg_mix0, dg_mix1]), jnp.concatenate([db_mix0, db_mix1]),
                 jnp.concatenate([dg_mlp0, dg_mlp1]), jnp.concatenate([db_mlp0, db_mlp1]), g_conv_w_full, g_meta_full)
    rep_stacked = _pack_rows(rep_local).reshape(N_DEV, -1, LANES)
    rep_rs = _exchange_start(rep_stacked, (_N_PEERS,) + rep_stacked.shape[1:], False, "rep_rs_start", after=(grad_x,))
    rep_own = lax.dynamic_index_in_dim(rep_rs[2], me, 0, keepdims=False)

    after, summed = rep_rs[4], {}
    for nm, own, handle in pending:
        land = _exchange_wait(handle, False, after, "rs_wait_" + nm)
        summed[nm] = after = _sum_own_and_peers(own, land, "rs_sum_" + nm)
    rep_part = _sum_own_and_peers(rep_own, _exchange_wait(rep_rs, False, after, "rep_rs_wait"), "rep_sum")
    rep_ag = _exchange_start(rep_part, (N_DEV,) + rep_part.shape, True, "rep_ag_start")

    grad_w = dict(ev_w_uq=summed["ev_uq"][None], ev_w_ukv=summed["ev_ukv"][None],
                  ev_w_in=summed["ev_in"][None], ev_w_out=summed["ev_out"][None], od_w_in=summed["od_in"][None],
                  od_w_out=summed["od_out"][None], mlp_w1=jnp.stack([summed["mlp0_w1"], summed["mlp1_w1"]]),
                  mlp_w2=jnp.stack([summed["mlp0_w2"], summed["mlp1_w2"]]))

    loss = lax.psum(loss_local, MESH_AXES)
    delta, new_m, new_v = {}, {}, {}

    def adamw(n):
        w, g, m, v = weights[n], grad_w[n], args["m_" + n], args["v_" + n]
        two_d = (-1, w.shape[-1])
        d2, m2, v2 = _adamw(w.reshape(two_d), g.reshape(two_d), m.reshape(two_d), v.reshape(two_d), "adamw_" + n)
        delta[n], new_m[n], new_v[n] = d2.reshape(w.shape), m2.reshape(w.shape), v2.reshape(w.shape)
        return d2

    for n in tuple(grad_w):
        after = adamw(n)
    rep_land = lax.dynamic_update_index_in_dim(_exchange_wait(rep_ag, True, after, "rep_ag_wait"), rep_part, me, 0)
    *rep_total, t_conv_w, t_meta = _unpack_rows(rep_land, rep_local)
    n_conv, n_meta = ev_conv_w.shape[2], meta_tokens.shape[1]
    small_g = dict(zip(rep_names, rep_total))
    small_g.update(meta_tokens=lax.dynamic_slice_in_dim(t_meta, me * n_meta, n_meta, axis=1),
                   ev_conv_w=lax.dynamic_slice_in_dim(t_conv_w, me * n_conv, n_conv, axis=1)[None])
    grad_w.update(small_g)
    for n in small_g:
        adamw(n)
    return (loss, grad_x, *[grad_w[n] for n in _WEIGHTS], *[delta[n] for n in _WEIGHTS],
            *[new_m[n] for n in _WEIGHTS], *[new_v[n] for n in _WEIGHTS])
```

```python
import math

import jax
import jax.numpy as jnp
from jax import lax
from jax.experimental import pallas as pl
from jax.experimental.pallas import tpu as pltpu

F32 = jnp.float32
BF16 = jnp.bfloat16

N_DEV = 8
MESH_AXES = ("x", "y", "c")
LANES = 128
SEQ_BLOCK = 128

N_META = 16
LRU_C = 8.0
MLA_HEADS = 8
MLA_NOPE = 64
MLA_ROPE = 32
MLA_V = 64
RET_HEADS = 4
ROPE_BASE = 10000.0
DEPTH = 2
DN_ALPHA = (2 * DEPTH) ** 0.25
EPS = 1e-5
NEG_INF = -1e30

ADAM_LR = 0.001
ADAM_B1 = 0.9
ADAM_B2 = 0.999
ADAM_EPS = 1e-08
ADAM_WD = 0.01
ADAM_STEP = 10

VMEM_LIMIT = 56 * 1024 * 1024
TN_OPERAND_BYTES = 36 * 1024 * 1024


def _params(*sem):
    return pltpu.CompilerParams(dimension_semantics=sem, vmem_limit_bytes=VMEM_LIMIT)


def _pick(n, cands):
    for c in cands:
        if n % c == 0:
            return c
    return n


def _row_tile(r, width):
    cands = (544, 272, 256, 128, 64, 32, 16, 8) if width <= 1024 else (272, 136, 128, 64, 32, 16, 8)
    return _pick(r, cands)


_DIMS = {"nn": (((1,), (0,)), ((), ())), "nt": (((1,), (1,)), ((), ())), "tn": (((0,), (0,)), ((), ()))}


def _dot(a, b, mode):
    return lax.dot_general(a.astype(BF16), b.astype(BF16), _DIMS[mode], preferred_element_type=F32)


def _matmul(a, b, mode, name, after=(), stacked=False, relu2_bwd_of=None, out_dtype=F32, out_relu2=False):
    if stacked:
        n_blk = b.shape[2] if mode != "tn" else b.shape[1] // N_DEV
    if mode == "nn":
        (m, k), n = a.shape, (N_DEV * n_blk if stacked else b.shape[1])
    elif mode == "nt":
        (m, k), n = a.shape, (b.shape[1] if stacked else b.shape[0])
    else:
        (k, m), n = a.shape, b.shape[1]
    tm = _pick(m, (2176, 1088, 1024, 544, 512, 272, 256, 128, 64, 32, 16, 8))
    tn = _pick(n, (1024, 512, 256, 128))
    tk = _pick(k, (2176, 1088, 1024, 544, 512, 272, 256, 128))
    kb = 2
    if stacked and mode == "nn":
        tn = n_blk
    if stacked and mode == "tn":
        tn = kb * n_blk
    if stacked and mode == "nt":
        tk = kb * n_blk
    if mode == "tn" and 2 * k * (tm * a.dtype.itemsize + tn * b.dtype.itemsize) <= TN_OPERAND_BYTES:
        tk = k
    nk = k // tk
    assert out_dtype == F32 or (nk == 1 and not (stacked and mode == "tn")), "narrow results need a single k step"
    assert not out_relu2 or nk == 1, "relu^2 is applied to a finished tile"

    out_spec = pl.BlockSpec((tm, tn), lambda i, j, kk: (i, j))
    out_shape = jax.ShapeDtypeStruct((m, n), out_dtype)
    if mode == "nn":
        a_spec = pl.BlockSpec((tm, tk), lambda i, j, kk: (i, kk))
        b_spec = pl.BlockSpec((tk, tn), lambda i, j, kk: (kk, j))
        if stacked:
            b_spec = pl.BlockSpec((None, tk, tn), lambda i, j, kk: (j, kk, 0))
    elif mode == "nt":
        a_spec = pl.BlockSpec((tm, tk), lambda i, j, kk: (i, kk))
        b_spec = pl.BlockSpec((tn, tk), lambda i, j, kk: (j, kk))
        if stacked:
            b_spec = pl.BlockSpec((kb, tn, n_blk), lambda i, j, kk: (kk, j, 0))
    else:
        a_spec = pl.BlockSpec((tk, tm), lambda i, j, kk: (kk, i))
        b_spec = pl.BlockSpec((tk, tn), lambda i, j, kk: (kk, j))
        if stacked:
            out_spec = pl.BlockSpec((kb, tm, n_blk), lambda i, j, kk: (j, i, 0))
            out_shape = jax.ShapeDtypeStruct((N_DEV, m, n_blk), F32)
    extra = [] if relu2_bwd_of is None else [relu2_bwd_of]
    extra_specs = [pl.BlockSpec((tm, tn), lambda i, j, kk: (i, j))] * len(extra)

    def body(a_ref, b_ref, *rest):
        def relu2_slope():
            return 2.0 * jnp.sqrt(rest[0][...].astype(F32))

        o_ref = rest[-1]
        kk = pl.program_id(2)
        av = a_ref[...]
        if stacked and mode == "nt":
            part = _dot(av[:, :n_blk], b_ref[0], mode)
            for q in range(1, kb):
                part = part + _dot(av[:, q * n_blk:(q + 1) * n_blk], b_ref[q], mode)
        else:
            part = _dot(av, b_ref[...], mode)
        if stacked and mode == "tn":
            part = jnp.stack([part[:, q * n_blk:(q + 1) * n_blk] for q in range(kb)])
        if nk == 1:
            if out_relu2:
                part = jnp.maximum(part, 0.0)
                part = part * part
            if relu2_bwd_of is not None:
                part = part * relu2_slope()
            o_ref[...] = part.astype(out_dtype)
            return

        @pl.when(kk == 0)
        def _():
            o_ref[...] = part

        @pl.when(kk != 0)
        def _():
            o_ref[...] += part

        if relu2_bwd_of is not None:
            @pl.when(kk == nk - 1)
            def _():
                o_ref[...] *= relu2_slope()

    return pl.pallas_call(
        body,
        name=name,
        grid=(m // tm, n // tn, nk),
        in_specs=[a_spec, b_spec] + extra_specs + [pl.BlockSpec(memory_space=pl.ANY)] * len(after),
        out_specs=out_spec,
        out_shape=out_shape,
        compiler_params=_params("parallel", "parallel", "arbitrary"),
    )(a, b, *extra, *after)


def _make_gate_pair(name):
    def call(body, ins, n_out, reduce_rows, tag):
        x = ins[0]
        m, d = x.shape[0], LANES
        g = x.shape[1] // d
        tm = _pick(m, (1088, 1024, 544, 512, 272, 256, 128, 64, 32, 16, 8))
        rows = pl.BlockSpec((tm, d), lambda h, i: (i, h))
        mats = pl.BlockSpec((1, d, d), lambda h, i: (h, 0, 0))
        return pl.pallas_call(
            body,
            name=name + tag,
            grid=(g, m // tm),
            in_specs=[rows if a.ndim == 2 else mats for a in ins],
            out_specs=[mats if reduce_rows else rows] * n_out,
            out_shape=[jax.ShapeDtypeStruct((g, d, d) if reduce_rows else (m, g * d), F32)] * n_out,
            compiler_params=_params("parallel", "arbitrary" if reduce_rows else "parallel"),
        )(*ins)

    def fwd_body(x_ref, wa_ref, wx_ref, a_ref, b_ref):
        xv = x_ref[...]
        a_ref[...] = _dot(xv, wa_ref[0], "nn")
        b_ref[...] = _dot(xv, wx_ref[0], "nn")

    def dx_body(da_ref, db_ref, wa_ref, wx_ref, dx_ref):
        dx_ref[...] = _dot(da_ref[...], wa_ref[0], "nt") + _dot(db_ref[...], wx_ref[0], "nt")

    def dw_body(x_ref, da_ref, db_ref, dwa_ref, dwx_ref):
        xv = x_ref[...]
        pa, pb = _dot(xv, da_ref[...], "tn"), _dot(xv, db_ref[...], "tn")

        @pl.when(pl.program_id(1) == 0)
        def _():
            dwa_ref[0] = pa
            dwx_ref[0] = pb

        @pl.when(pl.program_id(1) != 0)
        def _():
            dwa_ref[0] += pa
            dwx_ref[0] += pb

    @jax.custom_vjp
    def op(x, wa, wx):
        return tuple(call(fwd_body, (x, wa, wx), 2, False, "_fwd"))

    def fwd(x, wa, wx):
        return op(x, wa, wx), (x, wa, wx)

    def bwd(res, cots):
        x, wa, wx = res
        da, db = cots
        (dx,) = call(dx_body, (da, db, wa, wx), 1, False, "_dx")
        dwa, dwx = call(dw_body, (x, da, db), 2, True, "_dw")
        return dx, dwa, dwx

    op.defvjp(fwd, bwd)
    return op


def _my_place():
    return lax.axis_index("x"), lax.axis_index("y"), lax.axis_index("c")


def _stack_cols(full):
    k, n8 = full.shape
    return full.reshape(k, N_DEV, n8 // N_DEV).transpose(1, 0, 2)


def _unstack_cols(stacked):
    j, k, n = stacked.shape
    return stacked.transpose(1, 0, 2).reshape(k, j * n)


def _split_cols(p, cuts):
    bounds = (0,) + tuple(cuts) + (p.shape[1],)

    @jax.custom_vjp
    def op(z):
        return tuple(z[:, lo:hi] for lo, hi in zip(bounds[:-1], bounds[1:]))

    op.defvjp(lambda z: (op(z), None), lambda _, cots: (jnp.concatenate(cots, axis=1),))
    return op(p)


def _make_slot_linear(name):
    @jax.custom_vjp
    def op(x, w_full, slot):
        return _matmul(x, w_full, "nn", name + "_fwd")

    def fwd(x, w_full, slot):
        return op(x, w_full, slot), (x, w_full)

    def bwd(res, dy):
        x, w = res
        return _matmul(dy, w, "nt", name + "_dx"), jnp.zeros_like(w), _matmul(x, dy, "tn", name + "_dw")

    op.defvjp(fwd, bwd)
    return op


def _pack_rows(gs):
    flat = jnp.concatenate([g.reshape(-1) for g in gs])
    n = flat.shape[0]
    rows = -(-n // (256 * LANES)) * 256
    return jnp.pad(flat, (0, rows * LANES - n)).reshape(rows, LANES)


def _unpack_rows(packed, like):
    flat, out, off = packed.reshape(-1), [], 0
    for g in like:
        out.append(flat[off:off + g.size].reshape(g.shape))
        off += g.size
    return out


_HBM = pl.BlockSpec(memory_space=pltpu.HBM)
_SEM = pl.BlockSpec(memory_space=pltpu.SEMAPHORE)
_SIDE_EFFECT = pltpu.SideEffectType.DATAFLOW_SIDE_EFFECTING
_N_PEERS = N_DEV - 1


def _peer(k):
    x, y, c = _my_place()
    return x ^ ((k >> 2) & 1), y ^ ((k >> 1) & 1), c ^ (k & 1)


def _exchange_start(src, land_shape, gather, name, after=()):
    def body(src_ref, land_ref, *rest):
        send_sems, recv_sems, src_thru, land_thru, token = rest[len(after):]
        x, y, c = _my_place()
        me = 4 * x + 2 * y + c
        for k in range(1, N_DEV):
            px, py, pc = _peer(k)
            pltpu.make_async_remote_copy(
                src_ref=src_ref if gather else src_ref.at[4 * px + 2 * py + pc],
                dst_ref=land_ref.at[me] if gather else land_ref.at[k - 1],
                send_sem=send_sems.at[k - 1],
                recv_sem=recv_sems.at[k - 1],
                device_id=(px, py, pc),
                device_id_type=pl.DeviceIdType.MESH,
            ).start()
        token[...] = jnp.zeros_like(token)

    return pl.pallas_call(
        body,
        name=name,
        out_shape=(
            pltpu.SemaphoreType.DMA((_N_PEERS,)),
            pltpu.SemaphoreType.DMA((_N_PEERS,)),
            pltpu.HBM(src.shape, src.dtype),
            pltpu.HBM(land_shape, src.dtype),
            jax.ShapeDtypeStruct((8, LANES), F32),
        ),
        in_specs=(_HBM, _HBM) + (pl.BlockSpec(memory_space=pl.ANY),) * len(after),
        out_specs=(_SEM, _SEM, _HBM, _HBM, pl.BlockSpec(memory_space=pltpu.VMEM)),
        input_output_aliases={0: 2, 1: 3},
        compiler_params=pltpu.CompilerParams(has_side_effects=_SIDE_EFFECT),
    )(pltpu.with_memory_space_constraint(src, pltpu.HBM),
      pltpu.with_memory_space_constraint(lax.empty(land_shape, src.dtype), pltpu.HBM), *after)


def _gather_start_all(shards, name):
    n = len(shards)

    def body(*refs):
        srcs, lands = refs[:n], refs[n:2 * n]
        outs = refs[2 * n:]
        send_sems, recv_sems, token = outs[:n], outs[n:2 * n], outs[-1]
        x, y, c = _my_place()
        me = 4 * x + 2 * y + c
        for i in range(n):
            for k in range(1, N_DEV):
                pltpu.make_async_remote_copy(
                    src_ref=srcs[i],
                    dst_ref=lands[i].at[me],
                    send_sem=send_sems[i].at[k - 1],
                    recv_sem=recv_sems[i].at[k - 1],
                    device_id=_peer(k),
                    device_id_type=pl.DeviceIdType.MESH,
                ).start()
        token[...] = jnp.zeros_like(token)

    lands = [(N_DEV,) + s.shape for s in shards]
    sems = tuple(pltpu.SemaphoreType.DMA((_N_PEERS,)) for _ in range(2 * n))
    res = pl.pallas_call(
        body,
        name=name,
        out_shape=sems + tuple(pltpu.HBM(s.shape, s.dtype) for s in shards)
        + tuple(pltpu.HBM(ls, s.dtype) for ls, s in zip(lands, shards)) + (jax.ShapeDtypeStruct((8, LANES), F32),),
        in_specs=(_HBM,) * (2 * n),
        out_specs=(_SEM,) * (2 * n) + (_HBM,) * (2 * n) + (pl.BlockSpec(memory_space=pltpu.VMEM),),
        input_output_aliases={i: 2 * n + i for i in range(2 * n)},
        compiler_params=pltpu.CompilerParams(has_side_effects=_SIDE_EFFECT),
    )(*[pltpu.with_memory_space_constraint(s, pltpu.HBM) for s in shards],
      *[pltpu.with_memory_space_constraint(lax.empty(ls, s.dtype), pltpu.HBM) for ls, s in zip(lands, shards)])
    return [(res[i], res[n + i], res[2 * n + i], res[3 * n + i], res[-1]) for i in range(n)]


def _exchange_wait(handle, gather, after, name):
    send_sems, recv_sems, src_thru, land_thru, _ = handle

    def body(src_ref, land_ref, send_sems, recv_sems, after_ref, src_dead, got_ref):
        for k in range(1, N_DEV):
            cp = pltpu.make_async_remote_copy(
                src_ref=src_ref if gather else src_ref.at[k],
                dst_ref=land_ref.at[k - 1],
                send_sem=send_sems.at[k - 1],
                recv_sem=recv_sems.at[k - 1],
                device_id=_peer(k),
                device_id_type=pl.DeviceIdType.MESH,
            )
            cp.wait_send()
            cp.wait_recv()

    return pl.pallas_call(
        body,
        name=name,
        out_shape=(pltpu.HBM(src_thru.shape, src_thru.dtype), pltpu.HBM(land_thru.shape, land_thru.dtype)),
        in_specs=(_HBM, _HBM, _SEM, _SEM, pl.BlockSpec(memory_space=pl.ANY)),
        out_specs=(_HBM, _HBM),
        input_output_aliases={0: 0, 1: 1},
        compiler_params=pltpu.CompilerParams(has_side_effects=_SIDE_EFFECT),
    )(src_thru, land_thru, send_sems, recv_sems, after)[1]


def _sum_own_and_peers(own, land, name):
    r, c = own.shape
    tr = _pick(r, (256, 128, 64, 32, 16, 8))

    def body(o_ref, l_ref, out_ref):
        s = [l_ref[j] for j in range(_N_PEERS)]
        out_ref[...] = ((o_ref[...] + s[0]) + (s[1] + s[2])) + ((s[3] + s[4]) + (s[5] + s[6]))

    return pl.pallas_call(
        body,
        name=name,
        grid=(r // tr,),
        in_specs=[pl.BlockSpec((tr, c), lambda i: (i, 0)), pl.BlockSpec((_N_PEERS, tr, c), lambda i: (0, i, 0))],
        out_specs=pl.BlockSpec((tr, c), lambda i: (i, 0)),
        out_shape=jax.ShapeDtypeStruct((r, c), own.dtype),
        compiler_params=_params("parallel"),
    )(own, land)


class _Cols:
    def __init__(self, array, width, block):
        self.array, self.width, self.block = array, width, block
        self.shape, self.dtype = (array.shape[0], width), array.dtype


def _base(a):
    return a.array if isinstance(a, _Cols) else a


def _col_block(a):
    return a.block if isinstance(a, _Cols) else 0


def _make_rowwise(f, name, n_rows, n_tabs, n_pars):
    n_in = n_rows + n_tabs + n_pars

    def specs(args, tm):
        blocked = [pl.BlockSpec((tm, a.shape[1]), lambda i, blk=_col_block(a): (i, blk)) for a in args[: n_rows + n_tabs]]
        whole = [pl.BlockSpec(a.shape, lambda i: (0, 0)) for a in args[n_rows + n_tabs:]]
        return blocked + whole

    def out_struct(args, tm):
        blk = [jax.ShapeDtypeStruct((tm, a.shape[1]), a.dtype) for a in args[: n_rows + n_tabs]]
        blk += [jax.ShapeDtypeStruct(a.shape, a.dtype) for a in args[n_rows + n_tabs:]]
        return jax.eval_shape(f, *blk)

    def fwd_call(*args):
        r = args[0].shape[0]
        tm = _row_tile(r, max(a.shape[1] for a in args[:n_rows]))
        ro, so = out_struct(args, tm)

        def body(*refs):
            vals = [x[...] for x in refs[:n_in]]
            outs = refs[n_in:]
            rv, sv = f(*vals)
            for o, v in zip(outs[: len(ro)], rv):
                o[...] = v
            for o, v in zip(outs[len(ro):], sv):
                @pl.when(pl.program_id(0) == 0)
                def _(o=o, v=v):
                    o[...] = v

                @pl.when(pl.program_id(0) != 0)
                def _(o=o, v=v):
                    o[...] += v

        out_shape = [jax.ShapeDtypeStruct((r, s.shape[1]), s.dtype) for s in ro]
        out_shape += [jax.ShapeDtypeStruct(s.shape, s.dtype) for s in so]
        out_specs = [pl.BlockSpec((tm, s.shape[1]), lambda i: (i, 0)) for s in ro]
        out_specs += [pl.BlockSpec(s.shape, lambda i: (0, 0)) for s in so]
        res = pl.pallas_call(
            body,
            name=name + "_fwd",
            grid=(r // tm,),
            in_specs=specs(args, tm),
            out_specs=out_specs,
            out_shape=out_shape,
            compiler_params=_params("arbitrary" if so else "parallel"),
        )(*[_base(a) for a in args])
        return tuple(res[: len(ro)]), tuple(res[len(ro):])

    def bwd_call(args, cots, more=(), row_dtypes=None):
        r = args[0].shape[0]
        tm = _row_tile(r, max(a.shape[1] for a in args[:n_rows]))
        ro, so = out_struct(args, tm)
        crow, csum = cots
        rows, tabs, pars = args[:n_rows], args[n_rows:n_rows + n_tabs], args[n_rows + n_tabs:]
        n_c = len(crow) + len(csum)

        def body(*refs):
            vals = [x[...] for x in refs[:n_in]]
            cv = [x[...] for x in refs[n_in:n_in + n_c]]
            for x in refs[n_in + n_c:n_in + n_c + len(more)]:
                cv[0] = cv[0] + x[...]
            outs = refs[n_in + n_c + len(more):]
            tv = vals[n_rows:n_rows + n_tabs]

            def g(*dargs):
                return f(*dargs[:n_rows], *tv, *dargs[n_rows:])

            _, vjp = jax.vjp(g, *vals[:n_rows], *vals[n_rows + n_tabs:])
            d = vjp((tuple(cv[: len(crow)]), tuple(cv[len(crow):])))
            for o, v in zip(outs[:n_rows], d[:n_rows]):
                o[...] = v.astype(o.dtype)
            for o, v in zip(outs[n_rows:], d[n_rows:]):
                @pl.when(pl.program_id(0) == 0)
                def _(o=o, v=v):
                    o[...] = v

                @pl.when(pl.program_id(0) != 0)
                def _(o=o, v=v):
                    o[...] += v

        in_specs = specs(args, tm)
        in_specs += [pl.BlockSpec((tm, c.shape[1]), lambda i: (i, 0)) for c in crow]
        in_specs += [pl.BlockSpec(c.shape, lambda i: (0, 0)) for c in csum]
        in_specs += [pl.BlockSpec((tm, c.shape[1]), lambda i: (i, 0)) for c in more]
        out_shape = [jax.ShapeDtypeStruct(a.shape, dt) for a, dt in zip(rows, row_dtypes or [a.dtype for a in rows])]
        out_shape += [jax.ShapeDtypeStruct(a.shape, a.dtype) for a in pars]
        out_specs = [pl.BlockSpec((tm, a.shape[1]), lambda i: (i, 0)) for a in rows]
        out_specs += [pl.BlockSpec(a.shape, lambda i: (0, 0)) for a in pars]
        res = pl.pallas_call(
            body,
            name=name + "_bwd",
            grid=(r // tm,),
            in_specs=in_specs,
            out_specs=out_specs,
            out_shape=out_shape,
            compiler_params=_params("arbitrary" if pars else "parallel"),
        )(*[_base(a) for a in args], *crow, *csum, *more)
        return tuple(res[:n_rows]), tuple(res[n_rows:])

    @jax.custom_vjp
    def op(rows, tabs, pars):
        return fwd_call(*rows, *tabs, *pars)

    op.fwd_call, op.bwd_call = fwd_call, bwd_call

    def fwd(rows, tabs, pars):
        return fwd_call(*rows, *tabs, *pars), (rows, tabs, pars)

    def bwd(res, cots):
        rows, tabs, pars = res
        drows, dpars = bwd_call(tuple(rows) + tuple(tabs) + tuple(pars), cots)
        return drows, tuple(jnp.zeros_like(t) for t in tabs), dpars

    op.defvjp(fwd, bwd)
    return op


def _sigmoid(x):
    return 0.5 * (jnp.tanh(0.5 * x) + 1.0)


@jax.custom_jvp
def _softplus(x):
    e = jnp.exp(-jnp.abs(x))
    u = 1.0 + e
    log1p_e = jnp.where(u == 1.0, e, e * jnp.log(u) / jnp.where(u == 1.0, 1.0, u - 1.0))
    return jnp.maximum(x, 0.0) + log1p_e


@_softplus.defjvp
def _softplus_jvp(primals, tangents):
    (x,), (t,) = primals, tangents
    return _softplus(x), t * _sigmoid(x)


def _gelu(x):
    return 0.5 * x * (1.0 + jnp.tanh(math.sqrt(2.0 / math.pi) * (x + 0.044715 * (x * x * x))))


def _ln_res_f(h, mix, g, b):
    z = DN_ALPHA * h + mix
    mu = jnp.mean(z, axis=-1, keepdims=True)
    zc = z - mu
    var = jnp.mean(zc * zc, axis=-1, keepdims=True)
    return (zc * lax.rsqrt(var + EPS) * g + b,), ()


def _ln_res_copy_f(h, mix, g, b):
    (out,), _ = _ln_res_f(h, mix, g, b)
    return (out, out.astype(BF16)), ()


def _rmsnorm_f(x, g):
    return (x * lax.rsqrt(jnp.mean(x * x, axis=-1, keepdims=True) + EPS) * g,), ()


def _lru_gates_f(ga, gx, xc, b_a, b_x, lam):
    r = _sigmoid(ga + b_a)
    i = _sigmoid(gx + b_x)
    log_a = -LRU_C * r * _softplus(-lam)
    a = jnp.exp(log_a)
    one_minus_a2 = jnp.tanh(-log_a) * (jnp.exp(2.0 * log_a) + 1.0)
    return (a, jnp.sqrt(one_minus_a2) * (i * xc)), ()


def _lru_out_f(hh, p_gate):
    return (hh * _gelu(p_gate),), ()


def _rope_ret_f(q, k, cos2, sin2):
    d = cos2.shape[1]
    half = d // 2
    k_scale = d ** -0.5

    def rope(x):
        outs = []
        for h in range(x.shape[1] // d):
            xh = x[:, h * d:(h + 1) * d]
            rot = jnp.concatenate([xh[:, half:], xh[:, :half]], axis=1)
            outs.append(xh * cos2 + rot * sin2)
        return jnp.concatenate(outs, axis=1)

    return (rope(q), rope(k) * k_scale), ()


def _ret_out_f(o, g):
    d = o.shape[1] // RET_HEADS
    outs = []
    for h in range(RET_HEADS):
        oh = o[:, h * d:(h + 1) * d]
        outs.append(oh * lax.rsqrt(jnp.mean(oh * oh, axis=-1, keepdims=True) + EPS))
    y = jnp.concatenate(outs, axis=1)
    return (g * _sigmoid(g) * y,), ()


def _ret_out_bf16_f(o, g):
    (y,), _ = _ret_out_f(o, g)
    return (y.astype(BF16),), ()


def _loss_f(y, t, mask):
    e = (y - t) * mask
    per_row = jnp.sum(e * e, axis=-1, keepdims=True) * (0.5 / y.shape[1])
    total = jnp.sum(per_row, axis=0, keepdims=True)
    return (), (jnp.broadcast_to(total, (1, LANES)),)


def _shift_down(x, s):
    if s == 0:
        return x
    t = x.shape[0]
    row = lax.broadcasted_iota(jnp.int32, x.shape, 0)
    return jnp.where(row >= s, pltpu.roll(x, s, 0), 0.0)


def _shift_up(x, s):
    if s == 0:
        return x
    t = x.shape[0]
    row = lax.broadcasted_iota(jnp.int32, x.shape, 0)
    return jnp.where(row < t - s, pltpu.roll(x, t - s, 0), 0.0)


def _conv_fwd(x, w, b, name):
    bsz, t, c = x.shape
    width = w.shape[0]

    def body(x_ref, w_ref, b_ref, y_ref):
        xv = x_ref[0]
        acc = jnp.broadcast_to(b_ref[...], xv.shape)
        for k in range(width):
            acc = acc + w_ref[k:k + 1, :] * _shift_down(xv, width - 1 - k)
        y_ref[0] = acc

    return pl.pallas_call(
        body,
        name=name,
        grid=(bsz, c // LANES),
        in_specs=[
            pl.BlockSpec((1, t, LANES), lambda i, j: (i, 0, j)),
            pl.BlockSpec((width, LANES), lambda i, j: (0, j)),
            pl.BlockSpec((1, LANES), lambda i, j: (0, j)),
        ],
        out_specs=pl.BlockSpec((1, t, LANES), lambda i, j: (i, 0, j)),
        out_shape=jax.ShapeDtypeStruct(x.shape, F32),
        compiler_params=_params("parallel", "parallel"),
    )(x, w, b)


def _conv_bwd(x, w, dy, name):
    bsz, t, c = x.shape
    width = w.shape[0]

    def body(x_ref, w_ref, dy_ref, dx_ref, dw_ref, db_ref):
        xv, g = x_ref[0], dy_ref[0]
        dx = jnp.zeros_like(xv)
        dws = []
        for k in range(width):
            s = width - 1 - k
            dx = dx + w_ref[k:k + 1, :] * _shift_up(g, s)
            dws.append(jnp.sum(g * _shift_down(xv, s), axis=0, keepdims=True))
        dx_ref[0] = dx
        dw = jnp.concatenate(dws, axis=0)
        db = jnp.sum(g, axis=0, keepdims=True)

        @pl.when(pl.program_id(1) == 0)
        def _():
            dw_ref[...] = dw
            db_ref[...] = db

        @pl.when(pl.program_id(1) != 0)
        def _():
            dw_ref[...] += dw
            db_ref[...] += db

    return pl.pallas_call(
        body,
        name=name,
        grid=(c // LANES, bsz),
        in_specs=[
            pl.BlockSpec((1, t, LANES), lambda j, i: (i, 0, j)),
            pl.BlockSpec((width, LANES), lambda j, i: (0, j)),
            pl.BlockSpec((1, t, LANES), lambda j, i: (i, 0, j)),
        ],
        out_specs=[
            pl.BlockSpec((1, t, LANES), lambda j, i: (i, 0, j)),
            pl.BlockSpec((width, LANES), lambda j, i: (0, j)),
            pl.BlockSpec((1, LANES), lambda j, i: (0, j)),
        ],
        out_shape=[
            jax.ShapeDtypeStruct(x.shape, F32),
            jax.ShapeDtypeStruct(w.shape, F32),
            jax.ShapeDtypeStruct((1, c), F32),
        ],
        compiler_params=_params("parallel", "arbitrary"),
    )(x, w, dy)


def _make_conv(name):
    @jax.custom_vjp
    def op(x, w, b):
        return _conv_fwd(x, w, b, name + "_fwd")

    def fwd(x, w, b):
        return op(x, w, b), (x, w)

    def bwd(res, dy):
        x, w = res
        return tuple(_conv_bwd(x, w, dy, name + "_bwd"))

    op.defvjp(fwd, bwd)
    return op


_SCAN_ROWS = 8


def _scan_fwd(a, b, name):
    bsz, t, c = a.shape
    cw = _pick(c, (4 * LANES, 2 * LANES, LANES))

    def body(a_ref, b_ref, h_ref):
        row = lax.broadcasted_iota(jnp.int32, (_SCAN_ROWS, cw), 0)

        def step(i, carry):
            r0 = pl.multiple_of(i * _SCAN_ROWS, _SCAN_ROWS)
            av, bv = a_ref[0, pl.ds(r0, _SCAN_ROWS), :], b_ref[0, pl.ds(r0, _SCAN_ROWS), :]
            for s in (1, 2, 4):
                a_sh = jnp.where(row >= s, pltpu.roll(av, s, 0), 1.0)
                b_sh = jnp.where(row >= s, pltpu.roll(bv, s, 0), 0.0)
                bv = av * b_sh + bv
                av = av * a_sh
            hv = bv + av * carry
            h_ref[0, pl.ds(r0, _SCAN_ROWS), :] = hv
            return hv[_SCAN_ROWS - 1:, :]

        lax.fori_loop(0, t // _SCAN_ROWS, step, jnp.zeros((1, cw), F32), unroll=2)

    spec = pl.BlockSpec((1, t, cw), lambda i, j: (i, 0, j))
    return pl.pallas_call(
        body,
        name=name,
        grid=(bsz, c // cw),
        in_specs=[spec, spec],
        out_specs=spec,
        out_shape=jax.ShapeDtypeStruct(a.shape, F32),
        compiler_params=_params("parallel", "parallel"),
    )(a, b)


def _scan_bwd(a, h, g, name):
    bsz, t, c = a.shape
    cw = _pick(c, (2 * LANES, LANES))

    def body(a_ref, h_ref, g_ref, da_ref, db_ref):
        rows = _SCAN_ROWS
        row = lax.broadcasted_iota(jnp.int32, (rows, cw), 0)
        n_tiles = t // rows

        def step(n, carry):
            lam_next, a_next = carry
            i = n_tiles - 1 - n
            r0 = pl.multiple_of(i * rows, rows)
            rp = pl.multiple_of(jnp.maximum(i - 1, 0) * rows, rows)
            av, gv, hv = a_ref[0, pl.ds(r0, rows), :], g_ref[0, pl.ds(r0, rows), :], h_ref[0, pl.ds(r0, rows), :]
            h_before = jnp.where(i > 0, h_ref[0, pl.ds(rp, rows), :][rows - 1:, :], 0.0)
            cv = jnp.where(row < rows - 1, pltpu.roll(av, rows - 1, 0), a_next)
            for s in (1, 2, 4):
                c_sh = jnp.where(row < rows - s, pltpu.roll(cv, rows - s, 0), 1.0)
                g_sh = jnp.where(row < rows - s, pltpu.roll(gv, rows - s, 0), 0.0)
                gv = cv * g_sh + gv
                cv = cv * c_sh
            lam = gv + cv * lam_next
            db_ref[0, pl.ds(r0, rows), :] = lam
            da_ref[0, pl.ds(r0, rows), :] = lam * jnp.where(row >= 1, pltpu.roll(hv, 1, 0), h_before)
            return lam[:1, :], av[:1, :]

        zero = jnp.zeros((1, cw), F32)
        lax.fori_loop(0, n_tiles, step, (zero, zero), unroll=2)

    spec = pl.BlockSpec((1, t, cw), lambda i, j: (i, 0, j))
    return pl.pallas_call(
        body,
        name=name,
        grid=(bsz, c // cw),
        in_specs=[spec, spec, spec],
        out_specs=[spec, spec],
        out_shape=[jax.ShapeDtypeStruct(a.shape, F32)] * 2,
        compiler_params=_params("parallel", "parallel"),
    )(a, h, g)


def _make_scan(name):
    @jax.custom_vjp
    def op(a, b):
        return _scan_fwd(a, b, name + "_fwd")

    def fwd(a, b):
        h = op(a, b)
        return h, (a, h)

    def bwd(res, g):
        a, h = res
        da, db = _scan_bwd(a, h, g, name + "_bwd")
        return da, db

    op.defvjp(fwd, bwd)
    return op


def _query_blocks(t):
    blocks, start = [], 0
    while start < t:
        rows = 2 * SEQ_BLOCK if start + 2 * SEQ_BLOCK <= t else SEQ_BLOCK
        blocks.append((start, rows))
        start += rows
    return blocks


def _attn_exp(q, k, start, scale):
    tq, tk = q.shape[0], k.shape[0]
    s = _dot(q, k, "nt") * scale
    qpos = start + lax.broadcasted_iota(jnp.int32, (tq, tk), 0)
    kpos = lax.broadcasted_iota(jnp.int32, (tq, tk), 1)
    s = jnp.where(kpos <= qpos, s, NEG_INF)
    e = jnp.exp(s - jnp.max(s, axis=-1, keepdims=True))
    return e, 1.0 / jnp.sum(e, axis=-1, keepdims=True)


_MLA_SCALE = (MLA_NOPE + MLA_ROPE) ** -0.5


def _attn_specs(t):
    head = pl.BlockSpec((1, t, LANES), lambda b, h: (b, 0, h))
    shared = pl.BlockSpec((1, t, LANES), lambda b, h: (b, 0, 0))
    return head, shared


def _attn_fwd(q, kv, kpe, name):
    bsz, t, hl = q.shape
    head, shared = _attn_specs(t)

    def body(q_ref, kv_ref, kpe_ref, o_ref, k_s, v_s):
        lane = lax.broadcasted_iota(jnp.int32, (t, LANES), 1)
        kvh = kv_ref[0]
        k_s[...] = jnp.where(lane < MLA_NOPE, kvh, kpe_ref[0]).astype(BF16)
        v_s[...] = kvh.astype(BF16)
        for start, rows in _query_blocks(t):
            n = start + rows
            e, inv_l = _attn_exp(q_ref[0, start:n, :], k_s[:n, :], start, _MLA_SCALE)
            o_ref[0, start:n, :] = _dot(e, v_s[:n, :], "nn") * inv_l

    return pl.pallas_call(
        body,
        name=name,
        grid=(bsz, hl // LANES),
        in_specs=[head, head, shared],
        out_specs=head,
        out_shape=jax.ShapeDtypeStruct(q.shape, F32),
        scratch_shapes=[pltpu.VMEM((t, LANES), BF16), pltpu.VMEM((t, LANES), BF16)],
        compiler_params=_params("parallel", "parallel"),
    )(q, kv, kpe)


def _attn_bwd(q, kv, kpe, do, name):
    bsz, t, hl = q.shape
    head, shared = _attn_specs(t)

    def body(q_ref, kv_ref, kpe_ref, do_ref, dq_ref, dkv_ref, dkpe_ref, k_s, v_s, dk_s, dv_s):
        lane = lax.broadcasted_iota(jnp.int32, (t, LANES), 1)
        kvh = kv_ref[0]
        k_s[...] = jnp.where(lane < MLA_NOPE, kvh, kpe_ref[0]).astype(BF16)
        v_s[...] = kvh.astype(BF16)
        for start, rows in reversed(_query_blocks(t)):
            n = start + rows
            qb = q_ref[0, start:n, :]
            dob = jnp.where(lane[:rows] >= MLA_NOPE, do_ref[0, start:n, :], 0.0)
            kk, vv = k_s[:n, :], v_s[:n, :]
            e, inv_l = _attn_exp(qb, kk, start, _MLA_SCALE)
            p = e * inv_l
            dp = _dot(dob, vv, "nt")
            ds = p * (dp - jnp.sum(dp * p, axis=-1, keepdims=True)) * _MLA_SCALE
            dq_ref[0, start:n, :] = _dot(ds, kk, "nn")
            if n == t:
                dk_s[...] = _dot(ds, qb, "tn")
                dv_s[...] = _dot(p, dob, "tn")
            else:
                dk_s[:n, :] += _dot(ds, qb, "tn")
                dv_s[:n, :] += _dot(p, dob, "tn")
        dk = dk_s[...]
        dkv_ref[0] = jnp.where(lane < MLA_NOPE, dk, dv_s[...])
        dkpe = jnp.where(lane >= MLA_NOPE, dk, 0.0)

        @pl.when(pl.program_id(1) == 0)
        def _():
            dkpe_ref[0] = dkpe

        @pl.when(pl.program_id(1) != 0)
        def _():
            dkpe_ref[0] += dkpe

    return pl.pallas_call(
        body,
        name=name,
        grid=(bsz, hl // LANES),
        in_specs=[head, head, shared, head],
        out_specs=[head, head, shared],
        out_shape=[
            jax.ShapeDtypeStruct(q.shape, F32),
            jax.ShapeDtypeStruct(kv.shape, F32),
            jax.ShapeDtypeStruct(kpe.shape, F32),
        ],
        scratch_shapes=[pltpu.VMEM((t, LANES), BF16), pltpu.VMEM((t, LANES), BF16),
                        pltpu.VMEM((t, LANES), F32), pltpu.VMEM((t, LANES), F32)],
        compiler_params=_params("parallel", "arbitrary"),
    )(q, kv, kpe, do)


def _make_attention(name):
    @jax.custom_vjp
    def op(q, kv, kpe):
        return _attn_fwd(q, kv, kpe, name + "_fwd")

    def fwd(q, kv, kpe):
        return op(q, kv, kpe), (q, kv, kpe)

    def bwd(res, do):
        return tuple(_attn_bwd(*res, do, name + "_bwd"))

    op.defvjp(fwd, bwd)
    return op


_ROPE_SHIFT = MLA_ROPE // 2


def _rope_lanes_call(x, c, sm, sp, transpose, name):
    r, width = x.shape
    tm = _row_tile(r, width)

    def body(x_ref, c_ref, sm_ref, sp_ref, y_ref):
        cv, smv, spv = c_ref[...], sm_ref[...], sp_ref[...]
        for b in range(width // LANES):
            xb = x_ref[:, b * LANES:(b + 1) * LANES]
            if transpose:
                yb = xb * cv + pltpu.roll(xb * smv, _ROPE_SHIFT, 1) + pltpu.roll(xb * spv, LANES - _ROPE_SHIFT, 1)
            else:
                yb = xb * cv + pltpu.roll(xb, LANES - _ROPE_SHIFT, 1) * smv + pltpu.roll(xb, _ROPE_SHIFT, 1) * spv
            y_ref[:, b * LANES:(b + 1) * LANES] = yb

    tab = pl.BlockSpec((tm, LANES), lambda i: (i, 0))
    blk = pl.BlockSpec((tm, width), lambda i: (i, 0))
    return pl.pallas_call(
        body,
        name=name,
        grid=(r // tm,),
        in_specs=[blk, tab, tab, tab],
        out_specs=blk,
        out_shape=jax.ShapeDtypeStruct(x.shape, F32),
        compiler_params=_params("parallel"),
    )(x, c, sm, sp)


def _make_rope_lanes(name):
    @jax.custom_vjp
    def op(x, c, sm, sp):
        return _rope_lanes_call(x, c, sm, sp, False, name + "_fwd")

    def fwd(x, c, sm, sp):
        return op(x, c, sm, sp), (c, sm, sp)

    def bwd(res, dy):
        c, sm, sp = res
        return _rope_lanes_call(dy, c, sm, sp, True, name + "_bwd"), jnp.zeros_like(c), jnp.zeros_like(sm), jnp.zeros_like(sp)

    op.defvjp(fwd, bwd)
    return op


def _ret_chunk_rows(t):
    return t // 4 if t % 32 == 0 else SEQ_BLOCK


def _ret_decays(c, log_gamma):
    row = lax.broadcasted_iota(jnp.int32, (c, 1), 0)
    col = lax.broadcasted_iota(jnp.int32, (1, c), 1)
    rowf = row.astype(F32)
    d = jnp.where(row >= col, jnp.exp(log_gamma * rowf) * jnp.exp(-log_gamma * col.astype(F32)), 0.0)
    return d, jnp.exp(log_gamma * (rowf + 1.0)), jnp.exp(log_gamma * (c - 1.0 - rowf)), jnp.exp(log_gamma * c)


def _ret_specs(c, dk, dv, v_block0, n_chunks, reverse):
    pos = (lambda i: n_chunks - 1 - i) if reverse else (lambda i: i)
    return (
        pl.BlockSpec(memory_space=pltpu.SMEM),
        pl.BlockSpec((1, c, dk), lambda b, h, i: (b, pos(i), h)),
        pl.BlockSpec((1, c, dv), lambda b, h, i: (b, pos(i), h + v_block0)),
        pl.BlockSpec((1, c, dv), lambda b, h, i: (b, pos(i), h)),
        pl.BlockSpec((1, 1, dk, dv), lambda b, h, i: (b, h * n_chunks + pos(i), 0, 0)),
    )


def _ret_fwd(lg, q, k, v, name, dv=None, v_block0=0):
    bsz, t, hdk = q.shape
    heads = lg.shape[0]
    dk, dv = hdk // heads, dv or v.shape[2] // heads
    c = _ret_chunk_rows(t)
    n_chunks = t // c
    lg_spec, qk_spec, v_spec, o_spec, s_spec = _ret_specs(c, dk, dv, v_block0, n_chunks, False)

    def body(lg_ref, q_ref, k_ref, v_ref, o_ref, s_ref, state):
        @pl.when(pl.program_id(2) == 0)
        def _():
            state[...] = jnp.zeros((dk, dv), F32)

        d, a, b, g = _ret_decays(c, lg_ref[pl.program_id(1)])
        qb, kb, vb, s_in = q_ref[0], k_ref[0], v_ref[0], state[...]
        s_ref[0, 0] = s_in
        o_ref[0] = _dot(_dot(qb, kb, "nt") * d, vb, "nn") + a * _dot(qb, s_in, "nn")
        state[...] = g * s_in + _dot(kb * b, vb, "tn")

    return pl.pallas_call(
        body,
        name=name,
        grid=(bsz, heads, n_chunks),
        in_specs=[lg_spec, qk_spec, qk_spec, v_spec],
        out_specs=[o_spec, s_spec],
        out_shape=[jax.ShapeDtypeStruct((bsz, t, heads * dv), F32),
                   jax.ShapeDtypeStruct((bsz, heads * n_chunks, dk, dv), F32)],
        scratch_shapes=[pltpu.VMEM((dk, dv), F32)],
        compiler_params=_params("parallel", "parallel", "arbitrary"),
    )(lg, q, k, v)


def _ret_bwd(lg, q, k, v, states, do, name, v_block0=0, dv_dtype=F32):
    bsz, t, hdk = q.shape
    heads = lg.shape[0]
    dk, dv = hdk // heads, do.shape[2] // heads
    c = _ret_chunk_rows(t)
    n_chunks = t // c
    lg_spec, qk_spec, v_spec, o_spec, s_spec = _ret_specs(c, dk, dv, v_block0, n_chunks, True)

    def body(lg_ref, q_ref, k_ref, v_ref, s_ref, do_ref, dq_ref, dk_ref, dv_ref, dstate):
        @pl.when(pl.program_id(2) == 0)
        def _():
            dstate[...] = jnp.zeros((dk, dv), F32)

        d, a, b, g = _ret_decays(c, lg_ref[pl.program_id(1)])
        qb, kb, vb, dob, s_in, ds_out = q_ref[0], k_ref[0], v_ref[0], do_ref[0], s_ref[0, 0], dstate[...]
        scores = _dot(qb, kb, "nt") * d
        dscores = _dot(dob, vb, "nt") * d
        dq_ref[0] = _dot(dscores, kb, "nn") + a * _dot(dob, s_in, "nt")
        dk_ref[0] = _dot(dscores, qb, "tn") + b * _dot(vb, ds_out, "nt")
        dv_ref[0] = (_dot(scores, dob, "tn") + _dot(kb * b, ds_out, "nn")).astype(dv_dtype)
        dstate[...] = g * ds_out + _dot(qb, a * dob, "tn")

    return pl.pallas_call(
        body,
        name=name,
        grid=(bsz, heads, n_chunks),
        in_specs=[lg_spec, qk_spec, qk_spec, v_spec, s_spec, o_spec],
        out_specs=[qk_spec, qk_spec, o_spec],
        out_shape=[
            jax.ShapeDtypeStruct(q.shape, F32),
            jax.ShapeDtypeStruct(k.shape, F32),
            jax.ShapeDtypeStruct(do.shape, dv_dtype),
        ],
        scratch_shapes=[pltpu.VMEM((dk, dv), F32)],
        compiler_params=_params("parallel", "parallel", "arbitrary"),
    )(lg, q, k, v, states, do)


def _adamw(w, g, m, v, name):
    r, c = w.shape
    tr = _pick(r, (256, 128, 64, 32, 16, 8))

    def body(w_ref, g_ref, m_ref, v_ref, d_ref, nm_ref, nv_ref):
        gv = g_ref[...]
        nm = ADAM_B1 * m_ref[...] + (1.0 - ADAM_B1) * gv
        nv = ADAM_B2 * v_ref[...] + (1.0 - ADAM_B2) * (gv * gv)
        m_hat = nm / (1.0 - ADAM_B1 ** ADAM_STEP)
        v_hat = nv / (1.0 - ADAM_B2 ** ADAM_STEP)
        d_ref[...] = -ADAM_LR * (m_hat / (jnp.sqrt(v_hat) + ADAM_EPS) + ADAM_WD * w_ref[...])
        nm_ref[...] = nm
        nv_ref[...] = nv

    spec = pl.BlockSpec((tr, c), lambda i: (i, 0))
    return pl.pallas_call(
        body,
        name=name,
        grid=(r // tr,),
        in_specs=[spec] * 4,
        out_specs=[spec] * 3,
        out_shape=[jax.ShapeDtypeStruct((r, c), F32)] * 3,
        compiler_params=_params("parallel"),
    )(w, g, m, v)


def _rope_tables(t, half, reps):
    inv = ROPE_BASE ** (-jnp.arange(half, dtype=F32) / half)
    ang = jnp.arange(t, dtype=jnp.int32).astype(F32)[:, None] * inv[None, :]
    return jnp.tile(jnp.cos(ang), (1, reps)), jnp.tile(jnp.sin(ang), (1, reps))


def _padded_len(seq):
    return -(-(N_META + seq) // SEQ_BLOCK) * SEQ_BLOCK


def _embed(meta, x):
    bsz, seq, d = x.shape
    t = _padded_len(seq)
    return jnp.concatenate(
        [jnp.broadcast_to(meta[None], (bsz, N_META, d)), x, jnp.zeros((bsz, t - N_META - seq, d), F32)], axis=1
    ).reshape(bsz * t, d)


def _even_mixer(p, conv_w, conv_b, w_rg_a, b_rg_a, w_rg_x, b_rg_x, lru_lambda, q_norm_g, uq_slot, kv_norm_g,
                ukv_slot, gathered, bsz):
    w_uq_pad, w_ukv_full = gathered
    r = p.shape[0]
    t = r // bsz

    def tile_rows(tab):
        return jnp.tile(tab, (bsz, 1))

    lru_w = w_rg_a.shape[2] * w_rg_a.shape[1]
    q_rank, kv_rank = q_norm_g.shape[1], kv_norm_g.shape[1]
    p_gate, p_rec, p_q, p_kv, p_kpe = _split_cols(
        p, (lru_w, 2 * lru_w, 2 * lru_w + q_rank, 2 * lru_w + q_rank + kv_rank))

    xc = _make_conv("conv")(p_rec.reshape(bsz, t, lru_w), conv_w, conv_b).reshape(r, lru_w)
    ga, gx = _make_gate_pair("rg")(xc, w_rg_a[0], w_rg_x[0])
    (a, bb), _ = _make_rowwise(_lru_gates_f, "lru_gates", 3, 0, 3)((ga, gx, xc), (), (b_rg_a, b_rg_x, lru_lambda))
    hh = _make_scan("lru_scan")(a.reshape(bsz, t, lru_w), bb.reshape(bsz, t, lru_w)).reshape(r, lru_w)
    (y_rec,), _ = _make_rowwise(_lru_out_f, "lru_out", 2, 0, 0)((hh, p_gate), (), ())

    (qn,), _ = _make_rowwise(_rmsnorm_f, "q_norm", 1, 0, 1)((p_q,), (), (q_norm_g,))
    (kvn,), _ = _make_rowwise(_rmsnorm_f, "kv_norm", 1, 0, 1)((p_kv,), (), (kv_norm_g,))
    q = _make_slot_linear("ev_uq")(qn, w_uq_pad, uq_slot)
    kv = _make_slot_linear("ev_ukv")(kvn, w_ukv_full, ukv_slot)
    half = MLA_ROPE // 2
    cos, sin = _rope_tables(t, half, 1)
    one, zero = jnp.ones((t, MLA_NOPE), F32), jnp.zeros((t, MLA_NOPE), F32)
    tail = LANES - MLA_NOPE - MLA_ROPE
    c_tab = tile_rows(jnp.concatenate([one, cos, cos, one[:, :tail]], axis=1))
    sm_tab = tile_rows(jnp.concatenate([zero, -sin, zero[:, :half + tail]], axis=1))
    sp_tab = tile_rows(jnp.concatenate([zero, zero[:, :half], sin, zero[:, :tail]], axis=1))
    q = _make_rope_lanes("rope_q")(q, c_tab, sm_tab, sp_tab)
    kpe = _make_rope_lanes("rope_k")(p_kpe, c_tab, sm_tab, sp_tab)
    o = _make_attention("mla")(q.reshape(bsz, t, -1), kv.reshape(bsz, t, -1), kpe.reshape(bsz, t, LANES))
    return jnp.concatenate([y_rec, o.reshape(r, -1)], axis=1)


def _odd_mixer_fwd(p, bsz):
    r, width = p.shape
    t = r // bsz
    qk = width // 6
    dk = qk // RET_HEADS
    cos2, sin2 = _rope_tables(t, dk // 2, 2)
    sin2 = jnp.concatenate([-sin2[:, :dk // 2], sin2[:, dk // 2:]], axis=1)
    rope_args = (_Cols(p, qk, 0), _Cols(p, qk, 1), jnp.tile(cos2, (bsz, 1)), jnp.tile(sin2, (bsz, 1)))
    (rq, rk), _ = _make_rowwise(_rope_ret_f, "rope_ret", 2, 2, 0).fwd_call(*rope_args)
    lg = jnp.log(1.0 - 2.0 ** (-5.0 - jnp.arange(RET_HEADS, dtype=F32)))
    ret_args = (lg, rq.reshape(bsz, t, qk), rk.reshape(bsz, t, qk), p.reshape(bsz, t, width))
    o, states = _ret_fwd(*ret_args, "ret_fwd", dv=2 * dk, v_block0=qk // dk)
    gate_args = (o.reshape(r, 2 * qk), _Cols(p, 2 * qk, 2))
    (y,), _ = _make_rowwise(_ret_out_bf16_f, "ret_out", 2, 0, 0).fwd_call(*gate_args)
    return y, (rope_args, ret_args + (states,), gate_args)


def _odd_mixer_bwd(res, dy):
    rope_args, ret_args, gate_args = res
    bsz, t, qk = ret_args[1].shape
    dk = qk // RET_HEADS
    (do, dg), _ = _make_rowwise(_ret_out_f, "ret_out", 2, 0, 0).bwd_call(
        gate_args, ((dy,), ()), row_dtypes=(F32, BF16))
    drq, drk, dv = _ret_bwd(*ret_args, do.reshape(bsz, t, 2 * qk), "ret_bwd", v_block0=qk // dk, dv_dtype=BF16)
    (dq, dkk), _ = _make_rowwise(_rope_ret_f, "rope_ret", 2, 2, 0).bwd_call(
        rope_args, ((drq.reshape(bsz * t, qk), drk.reshape(bsz * t, qk)), ()), row_dtypes=(BF16, BF16))
    return jnp.concatenate([dq, dkk, dv.reshape(bsz * t, 2 * qk), dg], axis=1)


def _local_loss(h, target):
    bsz, seq, d = target.shape
    t = _padded_len(seq)
    t_real = N_META + seq
    pos = jnp.arange(t, dtype=jnp.int32)
    mask = jnp.tile(((pos >= N_META) & (pos < t_real)).astype(F32)[:, None], (bsz, 1))
    tgt = jnp.concatenate(
        [jnp.zeros((bsz, N_META, d), F32), target, jnp.zeros((bsz, t - t_real, d), F32)], axis=1).reshape(bsz * t, d)
    _, (total,) = _make_rowwise(_loss_f, "loss", 1, 2, 0)((h,), (tgt, mask), ())
    return total[0, 0]


_WEIGHTS = ("meta_tokens", "ev_w_in", "ev_conv_w", "ev_conv_b", "ev_w_rg_a", "ev_b_rg_a", "ev_w_rg_x", "ev_b_rg_x",
            "ev_lru_lambda", "ev_q_norm_g", "ev_w_uq", "ev_kv_norm_g", "ev_w_ukv", "ev_w_out", "od_w_in", "od_w_out",
            "ln_mix_g", "ln_mix_b", "mlp_w1", "mlp_w2", "ln_mlp_g", "ln_mlp_b")


def kernel(x, meta_tokens, ev_w_in, ev_conv_w, ev_conv_b, ev_w_rg_a, ev_b_rg_a, ev_w_rg_x, ev_b_rg_x, ev_lru_lambda, ev_q_norm_g, ev_w_uq, ev_kv_norm_g, ev_w_ukv, ev_w_out, od_w_in, od_w_out, ln_mix_g, ln_mix_b, mlp_w1, mlp_w2, ln_mlp_g, ln_mlp_b, loss_target, m_meta_tokens, m_ev_w_in, m_ev_conv_w, m_ev_conv_b, m_ev_w_rg_a, m_ev_b_rg_a, m_ev_w_rg_x, m_ev_b_rg_x, m_ev_lru_lambda, m_ev_q_norm_g, m_ev_w_uq, m_ev_kv_norm_g, m_ev_w_ukv, m_ev_w_out, m_od_w_in, m_od_w_out, m_ln_mix_g, m_ln_mix_b, m_mlp_w1, m_mlp_w2, m_ln_mlp_g, m_ln_mlp_b, v_meta_tokens, v_ev_w_in, v_ev_conv_w, v_ev_conv_b, v_ev_w_rg_a, v_ev_b_rg_a, v_ev_w_rg_x, v_ev_b_rg_x, v_ev_lru_lambda, v_ev_q_norm_g, v_ev_w_uq, v_ev_kv_norm_g, v_ev_w_ukv, v_ev_w_out, v_od_w_in, v_od_w_out, v_ln_mix_g, v_ln_mix_b, v_mlp_w1, v_mlp_w2, v_ln_mlp_g, v_ln_mlp_b):
    args = locals()
    weights = {n: args[n] for n in _WEIGHTS}
    bsz = x.shape[0]
    my_x, my_y, my_c = _my_place()
    me = 4 * my_x + 2 * my_y + my_c

    big = (("ev_in", ev_w_in[0], True), ("ev_out", ev_w_out[0], False), ("mlp0_w1", mlp_w1[0], True),
           ("mlp0_w2", mlp_w2[0], False), ("od_in", od_w_in[0], True), ("od_out", od_w_out[0], False),
           ("mlp1_w1", mlp_w1[1], True), ("mlp1_w2", mlp_w2[1], False))
    small_sharded = (("meta", meta_tokens, F32), ("conv_w", ev_conv_w[0], F32), ("ev_uq", ev_w_uq[0], BF16),
                     ("ev_ukv", ev_w_ukv[0], BF16))
    to_gather = (tuple((nm, s.astype(dt), True) for nm, s, dt in small_sharded)
                 + tuple((nm, s.astype(BF16), cols) for nm, s, cols in big))
    handles = _gather_start_all([s for _, s, _ in to_gather], "ag_start")
    gathers = {nm: (s, cols, h) for (nm, s, cols), h in zip(to_gather, handles)}
    gather_tokens = (handles[0][4],)

    def full_weight(nm, after):
        shard16, cols, handle = gathers[nm]
        land = _exchange_wait(handle, True, after, "ag_wait_" + nm)
        land = lax.dynamic_update_index_in_dim(land, shard16, me, 0)
        if cols and shard16.shape[1] % LANES == 0:
            return land, True
        return (_unstack_cols(land) if cols else land.reshape(-1, shard16.shape[1])), False

    meta_full, conv_w_full, w_uq_full, w_ukv_full = (
        _unstack_cols(lax.dynamic_update_index_in_dim(
            _exchange_wait(gathers[nm][2], True, gather_tokens[-1], "ag_wait_" + nm), gathers[nm][0], me, 0))
        for nm, _, _ in small_sharded)

    pending = []

    def linear_bwd(nm, x_in, w, dy, cols, unpad=None, **fused):
        w_full, w_stacked = w
        own = None
        if w_stacked:
            stacked = _matmul(x_in, dy, "tn", nm + "_dw", stacked=True)
        else:
            dw = _matmul(x_in, dy, "tn", nm + "_dw")
            dw = dw if unpad is None else unpad(dw)
            n = dw.shape[1] // N_DEV
            if cols:
                stacked = _stack_cols(dw)
                own = lax.dynamic_slice_in_dim(dw, me * n, n, axis=1)
            else:
                stacked = dw.reshape(N_DEV, dw.shape[0] // N_DEV, dw.shape[1])
        handle = _exchange_start(stacked, (_N_PEERS,) + stacked.shape[1:], False, "rs_start_" + nm)
        if own is None:
            own = lax.dynamic_index_in_dim(handle[2], me, 0, keepdims=False)
        pending.append((nm, own, handle))
        return _matmul(dy, w_full, "nt", nm + "_dx", after=(handle[4],), stacked=w_stacked, **fused)

    def linear_fwd(nm, x_in, w, **fused):
        return _matmul(x_in, w[0], "nn", nm + "_fwd", stacked=w[1], **fused)

    def mlp_fwd(h, h16, l):
        w1 = full_weight(f"mlp{l}_w1", h16)
        a16 = linear_fwd(f"mlp{l}_w1", h16, w1, out_relu2=True, out_dtype=BF16)
        w2 = full_weight(f"mlp{l}_w2", a16)
        f = linear_fwd(f"mlp{l}_w2", a16, w2)
        ln_args = (h, f, ln_mlp_g[l:l + 1], ln_mlp_b[l:l + 1])
        return ln_fwd(f"mlp{l}_ln", *ln_args), (h16, w1, a16, w2, ln_args)

    def mlp_bwd(dout, res, l):
        h16, w1, a16, w2, ln_args = res
        dh, df, dg, db = ln_bwd(f"mlp{l}_ln", ln_args, dout)
        du = linear_bwd(f"mlp{l}_w2", a16, w2, df, False, relu2_bwd_of=a16, out_dtype=BF16)
        return (dh, linear_bwd(f"mlp{l}_w1", h16, w1, du, True)), dg, db

    def ln_fwd(nm, h, mix, g, b):
        return _make_rowwise(_ln_res_copy_f, nm, 2, 0, 2).fwd_call(h, mix, g, b)[0]

    def ln_bwd(nm, ln_args, pieces):
        (dh, dmix), (dg, db) = _make_rowwise(_ln_res_f, nm, 2, 0, 2).bwd_call(
            ln_args, ((pieces[0],), ()), more=tuple(pieces[1:]), row_dtypes=(F32, BF16))
        return dh, dmix, dg, db

    h0, vjp_embed = jax.vjp(_embed, meta_full, x)
    n_in = ev_w_in.shape[2] * N_DEV
    kpe0, pad_lo, pad_hi = n_in - MLA_ROPE, MLA_NOPE, LANES - MLA_NOPE - MLA_ROPE
    w_in = full_weight("ev_in", h0)[0]
    zeros_in = jnp.zeros((w_in.shape[0], pad_lo), BF16)
    w_ev_in = (jnp.concatenate([w_in[:, :kpe0], zeros_in, w_in[:, kpe0:], zeros_in[:, :pad_hi]], axis=1), False)

    def unpad_in(dw):
        return jnp.concatenate([dw[:, :kpe0], dw[:, kpe0 + pad_lo:kpe0 + pad_lo + MLA_ROPE]], axis=1)

    p0 = _matmul(h0, w_ev_in[0], "nn", "ev_in_fwd", after=gather_tokens)
    q_rank, d_head = w_uq_full.shape[0], MLA_NOPE + MLA_ROPE
    w_uq_pad = jnp.pad(w_uq_full.reshape(q_rank, MLA_HEADS, d_head), ((0, 0), (0, 0), (0, LANES - d_head)))
    w_uq_pad = w_uq_pad.reshape(q_rank, MLA_HEADS * LANES)
    small = (conv_w_full, ev_conv_b, ev_w_rg_a, ev_b_rg_a, ev_w_rg_x, ev_b_rg_x, ev_lru_lambda, ev_q_norm_g,
             jnp.zeros(w_uq_pad.shape, F32), ev_kv_norm_g, jnp.zeros(w_ukv_full.shape, F32))
    y0, vjp_even = jax.vjp(lambda p, *s: _even_mixer(p, *s, (w_uq_pad, w_ukv_full), bsz), p0, *small)
    w_out = full_weight("ev_out", y0)[0]
    lru_w, d_model = y0.shape[1] - MLA_HEADS * LANES, w_out.shape[1]
    w_att = w_out[lru_w:].reshape(MLA_HEADS, MLA_V, d_model)
    w_att = jnp.concatenate([jnp.zeros((MLA_HEADS, LANES - MLA_V, d_model), BF16), w_att], axis=1)
    w_ev_out = (jnp.concatenate([w_out[:lru_w], w_att.reshape(MLA_HEADS * LANES, d_model)], axis=0), False)

    def unpad_out(dw):
        d_att = dw[lru_w:].reshape(MLA_HEADS, LANES, d_model)[:, LANES - MLA_V:].reshape(MLA_HEADS * MLA_V, d_model)
        return jnp.concatenate([dw[:lru_w], d_att], axis=0)

    mix0 = linear_fwd("ev_out", y0, w_ev_out)
    ln0_args = (h0, mix0, ln_mix_g[0:1], ln_mix_b[0:1])
    h1, h1_16 = ln_fwd("mix0_ln", *ln0_args)
    (h2, h2_16), res_mlp0 = mlp_fwd(h1, h1_16, 0)
    w_od_in = full_weight("od_in", h2_16)
    p1 = linear_fwd("od_in", h2_16, w_od_in)
    y1, res_odd = _odd_mixer_fwd(p1, bsz)
    w_od_out = full_weight("od_out", y1)
    mix1 = linear_fwd("od_out", y1, w_od_out)
    ln1_args = (h2, mix1, ln_mix_g[1:2], ln_mix_b[1:2])
    h3, h3_16 = ln_fwd("mix1_ln", *ln1_args)
    (h4, _), res_mlp1 = mlp_fwd(h3, h3_16, 1)
    loss_local, vjp_loss = jax.vjp(lambda h: _local_loss(h, loss_target), h4)

    dh4 = vjp_loss(jnp.ones((), F32))
    dh3, dg_mlp1, db_mlp1 = mlp_bwd(dh4, res_mlp1, 1)
    dh2, dmix1, dg_mix1, db_mix1 = ln_bwd("mix1_ln", ln1_args, dh3)
    dp1 = _odd_mixer_bwd(res_odd, linear_bwd("od_out", y1, w_od_out, dmix1, False))
    dh2 = (dh2, linear_bwd("od_in", h2_16, w_od_in, dp1, True))
    dh1, dg_mlp0, db_mlp0 = mlp_bwd(dh2, res_mlp0, 0)
    dh0, dmix0, dg_mix0, db_mix0 = ln_bwd("mix0_ln", ln0_args, dh1)
    dp0, *dsmall = vjp_even(linear_bwd("ev_out", y0, w_ev_out, dmix0, False, unpad=unpad_out))
    dh0 = dh0 + linear_bwd("ev_in", h0, w_ev_in, dp0.astype(BF16), True, unpad=unpad_in)
    g_meta_full, grad_x = vjp_embed(dh0)
    (g_conv_w_full, g_conv_b, g_w_rg_a, g_b_rg_a, g_w_rg_x, g_b_rg_x, g_lambda, g_q_norm, g_uq_pad, g_kv_norm,
     g_ukv_full) = dsmall

    for nm, dw in (("ev_uq", g_uq_pad.reshape(q_rank, MLA_HEADS, LANES)[:, :, :d_head].reshape(q_rank, -1)),
                   ("ev_ukv", g_ukv_full)):
        n = dw.shape[1] // N_DEV
        handle = _exchange_start(_stack_cols(dw), (_N_PEERS, dw.shape[0], n), False, "rs_start_" + nm)
        pending.append((nm, lax.dynamic_slice_in_dim(dw, me * n, n, axis=1), handle))

    rep_names = ("ev_conv_b", "ev_w_rg_a", "ev_b_rg_a", "ev_w_rg_x", "ev_b_rg_x", "ev_lru_lambda", "ev_q_norm_g",
                 "ev_kv_norm_g", "ln_mix_g", "ln_mix_b", "ln_mlp_g", "ln_mlp_b")
    rep_local = (g_conv_b, g_w_rg_a, g_b_rg_a, g_w_rg_x, g_b_rg_x, g_lambda, g_q_norm, g_kv_norm,
                 jnp.concatenate([dg_mix0, dg_mix1]), jnp.concatenate([db_mix0, db_mix1]),
                 jnp.concatenate([dg_mlp0, dg_mlp1]), jnp.concatenate([db_mlp0, db_mlp1]), g_conv_w_full, g_meta_full)
    rep_stacked = _pack_rows(rep_local).reshape(N_DEV, -1, LANES)
    rep_rs = _exchange_start(rep_stacked, (_N_PEERS,) + rep_stacked.shape[1:], False, "rep_rs_start", after=(grad_x,))
    rep_own = lax.dynamic_index_in_dim(rep_rs[2], me, 0, keepdims=False)

    after, summed = rep_rs[4], {}
    for nm, own, handle in pending:
        land = _exchange_wait(handle, False, after, "rs_wait_" + nm)
        summed[nm] = after = _sum_own_and_peers(own, land, "rs_sum_" + nm)
    rep_part = _sum_own_and_peers(rep_own, _exchange_wait(rep_rs, False, after, "rep_rs_wait"), "rep_sum")
    rep_ag = _exchange_start(rep_part, (N_DEV,) + rep_part.shape, True, "rep_ag_start")

    grad_w = dict(ev_w_uq=summed["ev_uq"][None], ev_w_ukv=summed["ev_ukv"][None],
                  ev_w_in=summed["ev_in"][None], ev_w_out=summed["ev_out"][None], od_w_in=summed["od_in"][None],
                  od_w_out=summed["od_out"][None], mlp_w1=jnp.stack([summed["mlp0_w1"], summed["mlp1_w1"]]),
                  mlp_w2=jnp.stack([summed["mlp0_w2"], summed["mlp1_w2"]]))

    loss = lax.psum(loss_local, MESH_AXES)
    delta, new_m, new_v = {}, {}, {}

    def adamw(n):
        w, g, m, v = weights[n], grad_w[n], args["m_" + n], args["v_" + n]
        two_d = (-1, w.shape[-1])
        d2, m2, v2 = _adamw(w.reshape(two_d), g.reshape(two_d), m.reshape(two_d), v.reshape(two_d), "adamw_" + n)
        delta[n], new_m[n], new_v[n] = d2.reshape(w.shape), m2.reshape(w.shape), v2.reshape(w.shape)
        return d2

    for n in tuple(grad_w):
        after = adamw(n)
    rep_land = lax.dynamic_update_index_in_dim(_exchange_wait(rep_ag, True, after, "rep_ag_wait"), rep_part, me, 0)
    *rep_total, t_conv_w, t_meta = _unpack_rows(rep_land, rep_local)
    n_conv, n_meta = ev_conv_w.shape[2], meta_tokens.shape[1]
    small_g = dict(zip(rep_names, rep_total))
    small_g.update(meta_tokens=lax.dynamic_slice_in_dim(t_meta, me * n_meta, n_meta, axis=1),
                   ev_conv_w=lax.dynamic_slice_in_dim(t_conv_w, me * n_conv, n_conv, axis=1)[None])
    grad_w.update(small_g)
    for n in small_g:
        adamw(n)
    return (loss, grad_x, *[grad_w[n] for n in _WEIGHTS], *[delta[n] for n in _WEIGHTS],
            *[new_m[n] for n in _WEIGHTS], *[new_v[n] for n in _WEIGHTS])
```

```python
import math

import jax
import jax.numpy as jnp
from jax import lax
from jax.experimental import pallas as pl
from jax.experimental.pallas import tpu as pltpu

F32 = jnp.float32
BF16 = jnp.bfloat16

N_DEV = 8
MESH_AXES = ("x", "y", "c")
LANES = 128
SEQ_BLOCK = 128

N_META = 16
LRU_C = 8.0
MLA_HEADS = 8
MLA_NOPE = 64
MLA_ROPE = 32
MLA_V = 64
RET_HEADS = 4
ROPE_BASE = 10000.0
DEPTH = 2
DN_ALPHA = (2 * DEPTH) ** 0.25
EPS = 1e-5
NEG_INF = -1e30

ADAM_LR = 0.001
ADAM_B1 = 0.9
ADAM_B2 = 0.999
ADAM_EPS = 1e-08
ADAM_WD = 0.01
ADAM_STEP = 10

VMEM_LIMIT = 56 * 1024 * 1024
TN_OPERAND_BYTES = 36 * 1024 * 1024


def _params(*sem):
    return pltpu.CompilerParams(dimension_semantics=sem, vmem_limit_bytes=VMEM_LIMIT)


def _pick(n, cands):
    for c in cands:
        if n % c == 0:
            return c
    return n


ROW_BLOCK_BYTES = 28 * 1024 * 1024


def _row_tile(r, widths):
    lanes = sum(max(w, LANES) for w in widths)
    for c in (1088, 544, 272, 256, 128, 64, 32, 16, 8):
        if r % c == 0 and 2 * 4 * lanes * c <= ROW_BLOCK_BYTES:
            return c
    return r


_DIMS = {"nn": (((1,), (0,)), ((), ())), "nt": (((1,), (1,)), ((), ())), "tn": (((0,), (0,)), ((), ()))}


def _dot(a, b, mode):
    return lax.dot_general(a.astype(BF16), b.astype(BF16), _DIMS[mode], preferred_element_type=F32)


def _matmul(a, b, mode, name, after=(), stacked=False, relu2_bwd_of=None, out_dtype=F32, out_relu2=False):
    if stacked:
        n_blk = b.shape[2] if mode != "tn" else b.shape[1] // N_DEV
    if mode == "nn":
        (m, k), n = a.shape, (N_DEV * n_blk if stacked else b.shape[1])
    elif mode == "nt":
        (m, k), n = a.shape, (b.shape[1] if stacked else b.shape[0])
    else:
        (k, m), n = a.shape, b.shape[1]
    tm = _pick(m, (2176, 1088, 1024, 544, 512, 272, 256, 128, 64, 32, 16, 8))
    tn = _pick(n, (1024, 512, 256, 128))
    tk = _pick(k, (2176, 1088, 1024, 544, 512, 272, 256, 128))
    kb = 2
    if stacked and mode == "nn":
        tn = n_blk
    if stacked and mode == "tn":
        tn = kb * n_blk
    if stacked and mode == "nt":
        tk = kb * n_blk
    if mode == "tn" and 2 * k * (tm * a.dtype.itemsize + tn * b.dtype.itemsize) <= TN_OPERAND_BYTES:
        tk = k
    nk = k // tk
    assert out_dtype == F32 or (nk == 1 and not (stacked and mode == "tn")), "narrow results need a single k step"
    assert not out_relu2 or nk == 1, "relu^2 is applied to a finished tile"

    out_spec = pl.BlockSpec((tm, tn), lambda i, j, kk: (i, j))
    out_shape = jax.ShapeDtypeStruct((m, n), out_dtype)
    if mode == "nn":
        a_spec = pl.BlockSpec((tm, tk), lambda i, j, kk: (i, kk))
        b_spec = pl.BlockSpec((tk, tn), lambda i, j, kk: (kk, j))
        if stacked:
            b_spec = pl.BlockSpec((None, tk, tn), lambda i, j, kk: (j, kk, 0))
    elif mode == "nt":
        a_spec = pl.BlockSpec((tm, tk), lambda i, j, kk: (i, kk))
        b_spec = pl.BlockSpec((tn, tk), lambda i, j, kk: (j, kk))
        if stacked:
            b_spec = pl.BlockSpec((kb, tn, n_blk), lambda i, j, kk: (kk, j, 0))
    else:
        a_spec = pl.BlockSpec((tk, tm), lambda i, j, kk: (kk, i))
        b_spec = pl.BlockSpec((tk, tn), lambda i, j, kk: (kk, j))
        if stacked:
            out_spec = pl.BlockSpec((kb, tm, n_blk), lambda i, j, kk: (j, i, 0))
            out_shape = jax.ShapeDtypeStruct((N_DEV, m, n_blk), F32)
    extra = [] if relu2_bwd_of is None else [relu2_bwd_of]
    extra_specs = [pl.BlockSpec((tm, tn), lambda i, j, kk: (i, j))] * len(extra)

    def body(a_ref, b_ref, *rest):
        def relu2_slope():
            return 2.0 * jnp.sqrt(rest[0][...].astype(F32))

        o_ref = rest[-1]
        kk = pl.program_id(2)
        av = a_ref[...]
        if stacked and mode == "nt":
            part = _dot(av[:, :n_blk], b_ref[0], mode)
            for q in range(1, kb):
                part = part + _dot(av[:, q * n_blk:(q + 1) * n_blk], b_ref[q], mode)
        else:
            part = _dot(av, b_ref[...], mode)
        if stacked and mode == "tn":
            part = jnp.stack([part[:, q * n_blk:(q + 1) * n_blk] for q in range(kb)])
        if nk == 1:
            if out_relu2:
                part = jnp.maximum(part, 0.0)
                part = part * part
            if relu2_bwd_of is not None:
                part = part * relu2_slope()
            o_ref[...] = part.astype(out_dtype)
            return

        @pl.when(kk == 0)
        def _():
            o_ref[...] = part

        @pl.when(kk != 0)
        def _():
            o_ref[...] += part

        if relu2_bwd_of is not None:
            @pl.when(kk == nk - 1)
            def _():
                o_ref[...] *= relu2_slope()

    return pl.pallas_call(
        body,
        name=name,
        grid=(m // tm, n // tn, nk),
        in_specs=[a_spec, b_spec] + extra_specs + [pl.BlockSpec(memory_space=pl.ANY)] * len(after),
        out_specs=out_spec,
        out_shape=out_shape,
        compiler_params=_params("parallel", "parallel", "arbitrary"),
    )(a, b, *extra, *after)


def _make_gate_pair(name):
    def call(body, ins, n_out, reduce_rows, tag):
        x = ins[0]
        m, d = x.shape[0], LANES
        g = x.shape[1] // d
        tm = _pick(m, (1088, 1024, 544, 512, 272, 256, 128, 64, 32, 16, 8))
        rows = pl.BlockSpec((tm, d), lambda h, i: (i, h))
        mats = pl.BlockSpec((1, d, d), lambda h, i: (h, 0, 0))
        return pl.pallas_call(
            body,
            name=name + tag,
            grid=(g, m // tm),
            in_specs=[rows if a.ndim == 2 else mats for a in ins],
            out_specs=[mats if reduce_rows else rows] * n_out,
            out_shape=[jax.ShapeDtypeStruct((g, d, d) if reduce_rows else (m, g * d), F32)] * n_out,
            compiler_params=_params("parallel", "arbitrary" if reduce_rows else "parallel"),
        )(*ins)

    def fwd_body(x_ref, wa_ref, wx_ref, a_ref, b_ref):
        xv = x_ref[...]
        a_ref[...] = _dot(xv, wa_ref[0], "nn")
        b_ref[...] = _dot(xv, wx_ref[0], "nn")

    def dx_body(da_ref, db_ref, wa_ref, wx_ref, dx_ref):
        dx_ref[...] = _dot(da_ref[...], wa_ref[0], "nt") + _dot(db_ref[...], wx_ref[0], "nt")

    def dw_body(x_ref, da_ref, db_ref, dwa_ref, dwx_ref):
        xv = x_ref[...]
        pa, pb = _dot(xv, da_ref[...], "tn"), _dot(xv, db_ref[...], "tn")

        @pl.when(pl.program_id(1) == 0)
        def _():
            dwa_ref[0] = pa
            dwx_ref[0] = pb

        @pl.when(pl.program_id(1) != 0)
        def _():
            dwa_ref[0] += pa
            dwx_ref[0] += pb

    @jax.custom_vjp
    def op(x, wa, wx):
        return tuple(call(fwd_body, (x, wa, wx), 2, False, "_fwd"))

    def fwd(x, wa, wx):
        return op(x, wa, wx), (x, wa, wx)

    def bwd(res, cots):
        x, wa, wx = res
        da, db = cots
        (dx,) = call(dx_body, (da, db, wa, wx), 1, False, "_dx")
        dwa, dwx = call(dw_body, (x, da, db), 2, True, "_dw")
        return dx, dwa, dwx

    op.defvjp(fwd, bwd)
    return op


def _my_place():
    return lax.axis_index("x"), lax.axis_index("y"), lax.axis_index("c")


def _stack_cols(full):
    k, n8 = full.shape
    return full.reshape(k, N_DEV, n8 // N_DEV).transpose(1, 0, 2)


def _unstack_cols(stacked):
    j, k, n = stacked.shape
    return stacked.transpose(1, 0, 2).reshape(k, j * n)


def _split_cols(p, cuts):
    bounds = (0,) + tuple(cuts) + (p.shape[1],)

    @jax.custom_vjp
    def op(z):
        return tuple(z[:, lo:hi] for lo, hi in zip(bounds[:-1], bounds[1:]))

    op.defvjp(lambda z: (op(z), None), lambda _, cots: (jnp.concatenate(cots, axis=1),))
    return op(p)


def _make_slot_linear(name):
    @jax.custom_vjp
    def op(x, w_full, slot):
        return _matmul(x, w_full, "nn", name + "_fwd")

    def fwd(x, w_full, slot):
        return op(x, w_full, slot), (x, w_full)

    def bwd(res, dy):
        x, w = res
        return _matmul(dy, w, "nt", name + "_dx"), jnp.zeros_like(w), _matmul(x, dy, "tn", name + "_dw")

    op.defvjp(fwd, bwd)
    return op


def _pack_rows(gs):
    flat = jnp.concatenate([g.reshape(-1) for g in gs])
    n = flat.shape[0]
    rows = -(-n // (256 * LANES)) * 256
    return jnp.pad(flat, (0, rows * LANES - n)).reshape(rows, LANES)


def _unpack_rows(packed, like):
    flat, out, off = packed.reshape(-1), [], 0
    for g in like:
        out.append(flat[off:off + g.size].reshape(g.shape))
        off += g.size
    return out


_HBM = pl.BlockSpec(memory_space=pltpu.HBM)
_SEM = pl.BlockSpec(memory_space=pltpu.SEMAPHORE)
_SIDE_EFFECT = pltpu.SideEffectType.DATAFLOW_SIDE_EFFECTING
_N_PEERS = N_DEV - 1


def _peer(k):
    x, y, c = _my_place()
    return x ^ ((k >> 2) & 1), y ^ ((k >> 1) & 1), c ^ (k & 1)


def _exchange_start(src, land_shape, gather, name, after=()):
    def body(src_ref, land_ref, *rest):
        send_sems, recv_sems, src_thru, land_thru, token = rest[len(after):]
        x, y, c = _my_place()
        me = 4 * x + 2 * y + c
        for k in range(1, N_DEV):
            px, py, pc = _peer(k)
            pltpu.make_async_remote_copy(
                src_ref=src_ref if gather else src_ref.at[4 * px + 2 * py + pc],
                dst_ref=land_ref.at[me] if gather else land_ref.at[k - 1],
                send_sem=send_sems.at[k - 1],
                recv_sem=recv_sems.at[k - 1],
                device_id=(px, py, pc),
                device_id_type=pl.DeviceIdType.MESH,
            ).start()
        token[...] = jnp.zeros_like(token)

    return pl.pallas_call(
        body,
        name=name,
        out_shape=(
            pltpu.SemaphoreType.DMA((_N_PEERS,)),
            pltpu.SemaphoreType.DMA((_N_PEERS,)),
            pltpu.HBM(src.shape, src.dtype),
            pltpu.HBM(land_shape, src.dtype),
            jax.ShapeDtypeStruct((8, LANES), F32),
        ),
        in_specs=(_HBM, _HBM) + (pl.BlockSpec(memory_space=pl.ANY),) * len(after),
        out_specs=(_SEM, _SEM, _HBM, _HBM, pl.BlockSpec(memory_space=pltpu.VMEM)),
        input_output_aliases={0: 2, 1: 3},
        compiler_params=pltpu.CompilerParams(has_side_effects=_SIDE_EFFECT),
    )(pltpu.with_memory_space_constraint(src, pltpu.HBM),
      pltpu.with_memory_space_constraint(lax.empty(land_shape, src.dtype), pltpu.HBM), *after)


def _gather_start_all(shards, name):
    n = len(shards)

    def body(*refs):
        srcs, lands = refs[:n], refs[n:2 * n]
        outs = refs[2 * n:]
        send_sems, recv_sems, token = outs[:n], outs[n:2 * n], outs[-1]
        x, y, c = _my_place()
        me = 4 * x + 2 * y + c
        for i in range(n):
            for k in range(1, N_DEV):
                pltpu.make_async_remote_copy(
                    src_ref=srcs[i],
                    dst_ref=lands[i].at[me],
                    send_sem=send_sems[i].at[k - 1],
                    recv_sem=recv_sems[i].at[k - 1],
                    device_id=_peer(k),
                    device_id_type=pl.DeviceIdType.MESH,
                ).start()
        token[...] = jnp.zeros_like(token)

    lands = [(N_DEV,) + s.shape for s in shards]
    sems = tuple(pltpu.SemaphoreType.DMA((_N_PEERS,)) for _ in range(2 * n))
    res = pl.pallas_call(
        body,
        name=name,
        out_shape=sems + tuple(pltpu.HBM(s.shape, s.dtype) for s in shards)
        + tuple(pltpu.HBM(ls, s.dtype) for ls, s in zip(lands, shards)) + (jax.ShapeDtypeStruct((8, LANES), F32),),
        in_specs=(_HBM,) * (2 * n),
        out_specs=(_SEM,) * (2 * n) + (_HBM,) * (2 * n) + (pl.BlockSpec(memory_space=pltpu.VMEM),),
        input_output_aliases={i: 2 * n + i for i in range(2 * n)},
        compiler_params=pltpu.CompilerParams(has_side_effects=_SIDE_EFFECT),
    )(*[pltpu.with_memory_space_constraint(s, pltpu.HBM) for s in shards],
      *[pltpu.with_memory_space_constraint(lax.empty(ls, s.dtype), pltpu.HBM) for ls, s in zip(lands, shards)])
    return [(res[i], res[n + i], res[2 * n + i], res[3 * n + i], res[-1]) for i in range(n)]


def _exchange_wait(handle, gather, after, name):
    send_sems, recv_sems, src_thru, land_thru, _ = handle

    def body(src_ref, land_ref, send_sems, recv_sems, after_ref, src_dead, got_ref):
        for k in range(1, N_DEV):
            cp = pltpu.make_async_remote_copy(
                src_ref=src_ref if gather else src_ref.at[k],
                dst_ref=land_ref.at[k - 1],
                send_sem=send_sems.at[k - 1],
                recv_sem=recv_sems.at[k - 1],
                device_id=_peer(k),
                device_id_type=pl.DeviceIdType.MESH,
            )
            cp.wait_send()
            cp.wait_recv()

    return pl.pallas_call(
        body,
        name=name,
        out_shape=(pltpu.HBM(src_thru.shape, src_thru.dtype), pltpu.HBM(land_thru.shape, land_thru.dtype)),
        in_specs=(_HBM, _HBM, _SEM, _SEM, pl.BlockSpec(memory_space=pl.ANY)),
        out_specs=(_HBM, _HBM),
        input_output_aliases={0: 0, 1: 1},
        compiler_params=pltpu.CompilerParams(has_side_effects=_SIDE_EFFECT),
    )(src_thru, land_thru, send_sems, recv_sems, after)[1]


def _sum_own_and_peers(own, land, name):
    r, c = own.shape
    tr = _pick(r, (256, 128, 64, 32, 16, 8))

    def body(o_ref, l_ref, out_ref):
        s = [l_ref[j] for j in range(_N_PEERS)]
        out_ref[...] = ((o_ref[...] + s[0]) + (s[1] + s[2])) + ((s[3] + s[4]) + (s[5] + s[6]))

    return pl.pallas_call(
        body,
        name=name,
        grid=(r // tr,),
        in_specs=[pl.BlockSpec((tr, c), lambda i: (i, 0)), pl.BlockSpec((_N_PEERS, tr, c), lambda i: (0, i, 0))],
        out_specs=pl.BlockSpec((tr, c), lambda i: (i, 0)),
        out_shape=jax.ShapeDtypeStruct((r, c), own.dtype),
        compiler_params=_params("parallel"),
    )(own, land)


class _Cols:
    def __init__(self, array, width, block):
        self.array, self.width, self.block = array, width, block
        self.shape, self.dtype = (array.shape[0], width), array.dtype


def _base(a):
    return a.array if isinstance(a, _Cols) else a


def _col_block(a):
    return a.block if isinstance(a, _Cols) else 0


def _make_rowwise(f, name, n_rows, n_tabs, n_pars):
    n_in = n_rows + n_tabs + n_pars

    def specs(args, tm):
        blocked = [pl.BlockSpec((tm, a.shape[1]), lambda i, blk=_col_block(a): (i, blk)) for a in args[: n_rows + n_tabs]]
        whole = [pl.BlockSpec(a.shape, lambda i: (0, 0)) for a in args[n_rows + n_tabs:]]
        return blocked + whole

    def out_struct(args, tm):
        blk = [jax.ShapeDtypeStruct((tm, a.shape[1]), a.dtype) for a in args[: n_rows + n_tabs]]
        blk += [jax.ShapeDtypeStruct(a.shape, a.dtype) for a in args[n_rows + n_tabs:]]
        return jax.eval_shape(f, *blk)

    def fwd_call(*args):
        r = args[0].shape[0]
        tm = _row_tile(r, [a.shape[1] for a in args[:n_rows + n_tabs]] + [s.shape[1] for s in out_struct(args, 8)[0]])
        ro, so = out_struct(args, tm)

        def body(*refs):
            vals = [x[...] for x in refs[:n_in]]
            outs = refs[n_in:]
            rv, sv = f(*vals)
            for o, v in zip(outs[: len(ro)], rv):
                o[...] = v
            for o, v in zip(outs[len(ro):], sv):
                @pl.when(pl.program_id(0) == 0)
                def _(o=o, v=v):
                    o[...] = v

                @pl.when(pl.program_id(0) != 0)
                def _(o=o, v=v):
                    o[...] += v

        out_shape = [jax.ShapeDtypeStruct((r, s.shape[1]), s.dtype) for s in ro]
        out_shape += [jax.ShapeDtypeStruct(s.shape, s.dtype) for s in so]
        out_specs = [pl.BlockSpec((tm, s.shape[1]), lambda i: (i, 0)) for s in ro]
        out_specs += [pl.BlockSpec(s.shape, lambda i: (0, 0)) for s in so]
        res = pl.pallas_call(
            body,
            name=name + "_fwd",
            grid=(r // tm,),
            in_specs=specs(args, tm),
            out_specs=out_specs,
            out_shape=out_shape,
            compiler_params=_params("arbitrary" if so else "parallel"),
        )(*[_base(a) for a in args])
        return tuple(res[: len(ro)]), tuple(res[len(ro):])

    def bwd_call(args, cots, more=(), row_dtypes=None):
        r = args[0].shape[0]
        crow, csum = cots
        tm = _row_tile(r, [a.shape[1] for a in tuple(args[:n_rows + n_tabs]) + tuple(crow) + tuple(more)]
                       + [a.shape[1] for a in args[:n_rows]])
        ro, so = out_struct(args, tm)
        rows, tabs, pars = args[:n_rows], args[n_rows:n_rows + n_tabs], args[n_rows + n_tabs:]
        n_c = len(crow) + len(csum)

        def body(*refs):
            vals = [x[...] for x in refs[:n_in]]
            cv = [x[...] for x in refs[n_in:n_in + n_c]]
            for x in refs[n_in + n_c:n_in + n_c + len(more)]:
                cv[0] = cv[0] + x[...]
            outs = refs[n_in + n_c + len(more):]
            tv = vals[n_rows:n_rows + n_tabs]

            def g(*dargs):
                return f(*dargs[:n_rows], *tv, *dargs[n_rows:])

            _, vjp = jax.vjp(g, *vals[:n_rows], *vals[n_rows + n_tabs:])
            d = vjp((tuple(cv[: len(crow)]), tuple(cv[len(crow):])))
            for o, v in zip(outs[:n_rows], d[:n_rows]):
                o[...] = v.astype(o.dtype)
            for o, v in zip(outs[n_rows:], d[n_rows:]):
                @pl.when(pl.program_id(0) == 0)
                def _(o=o, v=v):
                    o[...] = v

                @pl.when(pl.program_id(0) != 0)
                def _(o=o, v=v):
                    o[...] += v

        in_specs = specs(args, tm)
        in_specs += [pl.BlockSpec((tm, c.shape[1]), lambda i: (i, 0)) for c in crow]
        in_specs += [pl.BlockSpec(c.shape, lambda i: (0, 0)) for c in csum]
        in_specs += [pl.BlockSpec((tm, c.shape[1]), lambda i: (i, 0)) for c in more]
        out_shape = [jax.ShapeDtypeStruct(a.shape, dt) for a, dt in zip(rows, row_dtypes or [a.dtype for a in rows])]
        out_shape += [jax.ShapeDtypeStruct(a.shape, a.dtype) for a in pars]
        out_specs = [pl.BlockSpec((tm, a.shape[1]), lambda i: (i, 0)) for a in rows]
        out_specs += [pl.BlockSpec(a.shape, lambda i: (0, 0)) for a in pars]
        res = pl.pallas_call(
            body,
            name=name + "_bwd",
            grid=(r // tm,),
            in_specs=in_specs,
            out_specs=out_specs,
            out_shape=out_shape,
            compiler_params=_params("arbitrary" if pars else "parallel"),
        )(*[_base(a) for a in args], *crow, *csum, *more)
        return tuple(res[:n_rows]), tuple(res[n_rows:])

    @jax.custom_vjp
    def op(rows, tabs, pars):
        return fwd_call(*rows, *tabs, *pars)

    op.fwd_call, op.bwd_call = fwd_call, bwd_call

    def fwd(rows, tabs, pars):
        return fwd_call(*rows, *tabs, *pars), (rows, tabs, pars)

    def bwd(res, cots):
        rows, tabs, pars = res
        drows, dpars = bwd_call(tuple(rows) + tuple(tabs) + tuple(pars), cots)
        return drows, tuple(jnp.zeros_like(t) for t in tabs), dpars

    op.defvjp(fwd, bwd)
    return op


def _sigmoid(x):
    return 0.5 * (jnp.tanh(0.5 * x) + 1.0)


@jax.custom_jvp
def _softplus(x):
    e = jnp.exp(-jnp.abs(x))
    u = 1.0 + e
    log1p_e = jnp.where(u == 1.0, e, e * jnp.log(u) / jnp.where(u == 1.0, 1.0, u - 1.0))
    return jnp.maximum(x, 0.0) + log1p_e


@_softplus.defjvp
def _softplus_jvp(primals, tangents):
    (x,), (t,) = primals, tangents
    return _softplus(x), t * _sigmoid(x)


def _gelu(x):
    return 0.5 * x * (1.0 + jnp.tanh(math.sqrt(2.0 / math.pi) * (x + 0.044715 * (x * x * x))))


def _ln_res_f(h, mix, g, b):
    z = DN_ALPHA * h + mix
    mu = jnp.mean(z, axis=-1, keepdims=True)
    zc = z - mu
    var = jnp.mean(zc * zc, axis=-1, keepdims=True)
    return (zc * lax.rsqrt(var + EPS) * g + b,), ()


def _ln_res_copy_f(h, mix, g, b):
    (out,), _ = _ln_res_f(h, mix, g, b)
    return (out, out.astype(BF16)), ()


def _rmsnorm_f(x, g):
    return (x * lax.rsqrt(jnp.mean(x * x, axis=-1, keepdims=True) + EPS) * g,), ()


def _lru_gates_f(ga, gx, xc, b_a, b_x, lam):
    r = _sigmoid(ga + b_a)
    i = _sigmoid(gx + b_x)
    log_a = -LRU_C * r * _softplus(-lam)
    a = jnp.exp(log_a)
    one_minus_a2 = jnp.tanh(-log_a) * (jnp.exp(2.0 * log_a) + 1.0)
    return (a, jnp.sqrt(one_minus_a2) * (i * xc)), ()


def _lru_out_f(hh, p_gate):
    return (hh * _gelu(p_gate),), ()


def _rope_ret_f(q, k, cos2, sin2):
    d = cos2.shape[1]
    half = d // 2
    k_scale = d ** -0.5

    def rope(x):
        outs = []
        for h in range(x.shape[1] // d):
            xh = x[:, h * d:(h + 1) * d]
            rot = jnp.concatenate([xh[:, half:], xh[:, :half]], axis=1)
            outs.append(xh * cos2 + rot * sin2)
        return jnp.concatenate(outs, axis=1)

    return (rope(q), rope(k) * k_scale), ()


def _ret_out_f(o, g):
    d = o.shape[1] // RET_HEADS
    outs = []
    for h in range(RET_HEADS):
        oh = o[:, h * d:(h + 1) * d]
        outs.append(oh * lax.rsqrt(jnp.mean(oh * oh, axis=-1, keepdims=True) + EPS))
    y = jnp.concatenate(outs, axis=1)
    return (g * _sigmoid(g) * y,), ()


def _ret_out_bf16_f(o, g):
    (y,), _ = _ret_out_f(o, g)
    return (y.astype(BF16),), ()


def _loss_f(y, t, mask):
    e = (y - t) * mask
    per_row = jnp.sum(e * e, axis=-1, keepdims=True) * (0.5 / y.shape[1])
    total = jnp.sum(per_row, axis=0, keepdims=True)
    return (), (jnp.broadcast_to(total, (1, LANES)),)


def _shift_down(x, s):
    if s == 0:
        return x
    t = x.shape[0]
    row = lax.broadcasted_iota(jnp.int32, x.shape, 0)
    return jnp.where(row >= s, pltpu.roll(x, s, 0), 0.0)


def _shift_up(x, s):
    if s == 0:
        return x
    t = x.shape[0]
    row = lax.broadcasted_iota(jnp.int32, x.shape, 0)
    return jnp.where(row < t - s, pltpu.roll(x, t - s, 0), 0.0)


def _conv_fwd(x, w, b, name):
    bsz, t, c = x.shape
    width = w.shape[0]

    def body(x_ref, w_ref, b_ref, y_ref):
        xv = x_ref[0]
        acc = jnp.broadcast_to(b_ref[...], xv.shape)
        for k in range(width):
            acc = acc + w_ref[k:k + 1, :] * _shift_down(xv, width - 1 - k)
        y_ref[0] = acc

    return pl.pallas_call(
        body,
        name=name,
        grid=(bsz, c // LANES),
        in_specs=[
            pl.BlockSpec((1, t, LANES), lambda i, j: (i, 0, j)),
            pl.BlockSpec((width, LANES), lambda i, j: (0, j)),
            pl.BlockSpec((1, LANES), lambda i, j: (0, j)),
        ],
        out_specs=pl.BlockSpec((1, t, LANES), lambda i, j: (i, 0, j)),
        out_shape=jax.ShapeDtypeStruct(x.shape, F32),
        compiler_params=_params("parallel", "parallel"),
    )(x, w, b)


def _conv_bwd(x, w, dy, name):
    bsz, t, c = x.shape
    width = w.shape[0]

    def body(x_ref, w_ref, dy_ref, dx_ref, dw_ref, db_ref):
        xv, g = x_ref[0], dy_ref[0]
        dx = jnp.zeros_like(xv)
        dws = []
        for k in range(width):
            s = width - 1 - k
            dx = dx + w_ref[k:k + 1, :] * _shift_up(g, s)
            dws.append(jnp.sum(g * _shift_down(xv, s), axis=0, keepdims=True))
        dx_ref[0] = dx
        dw = jnp.concatenate(dws, axis=0)
        db = jnp.sum(g, axis=0, keepdims=True)

        @pl.when(pl.program_id(1) == 0)
        def _():
            dw_ref[...] = dw
            db_ref[...] = db

        @pl.when(pl.program_id(1) != 0)
        def _():
            dw_ref[...] += dw
            db_ref[...] += db

    return pl.pallas_call(
        body,
        name=name,
        grid=(c // LANES, bsz),
        in_specs=[
            pl.BlockSpec((1, t, LANES), lambda j, i: (i, 0, j)),
            pl.BlockSpec((width, LANES), lambda j, i: (0, j)),
            pl.BlockSpec((1, t, LANES), lambda j, i: (i, 0, j)),
        ],
        out_specs=[
            pl.BlockSpec((1, t, LANES), lambda j, i: (i, 0, j)),
            pl.BlockSpec((width, LANES), lambda j, i: (0, j)),
            pl.BlockSpec((1, LANES), lambda j, i: (0, j)),
        ],
        out_shape=[
            jax.ShapeDtypeStruct(x.shape, F32),
            jax.ShapeDtypeStruct(w.shape, F32),
            jax.ShapeDtypeStruct((1, c), F32),
        ],
        compiler_params=_params("parallel", "arbitrary"),
    )(x, w, dy)


def _make_conv(name):
    @jax.custom_vjp
    def op(x, w, b):
        return _conv_fwd(x, w, b, name + "_fwd")

    def fwd(x, w, b):
        return op(x, w, b), (x, w)

    def bwd(res, dy):
        x, w = res
        return tuple(_conv_bwd(x, w, dy, name + "_bwd"))

    op.defvjp(fwd, bwd)
    return op


_SCAN_ROWS = 8


def _scan_fwd(a, b, name):
    bsz, t, c = a.shape
    cw = _pick(c, (4 * LANES, 2 * LANES, LANES))

    def body(a_ref, b_ref, h_ref):
        row = lax.broadcasted_iota(jnp.int32, (_SCAN_ROWS, cw), 0)

        def step(i, carry):
            r0 = pl.multiple_of(i * _SCAN_ROWS, _SCAN_ROWS)
            av, bv = a_ref[0, pl.ds(r0, _SCAN_ROWS), :], b_ref[0, pl.ds(r0, _SCAN_ROWS), :]
            for s in (1, 2, 4):
                a_sh = jnp.where(row >= s, pltpu.roll(av, s, 0), 1.0)
                b_sh = jnp.where(row >= s, pltpu.roll(bv, s, 0), 0.0)
                bv = av * b_sh + bv
                av = av * a_sh
            hv = bv + av * carry
            h_ref[0, pl.ds(r0, _SCAN_ROWS), :] = hv
            return hv[_SCAN_ROWS - 1:, :]

        lax.fori_loop(0, t // _SCAN_ROWS, step, jnp.zeros((1, cw), F32), unroll=2)

    spec = pl.BlockSpec((1, t, cw), lambda i, j: (i, 0, j))
    return pl.pallas_call(
        body,
        name=name,
        grid=(bsz, c // cw),
        in_specs=[spec, spec],
        out_specs=spec,
        out_shape=jax.ShapeDtypeStruct(a.shape, F32),
        compiler_params=_params("parallel", "parallel"),
    )(a, b)


def _scan_bwd(a, h, g, name):
    bsz, t, c = a.shape
    cw = _pick(c, (2 * LANES, LANES))

    def body(a_ref, h_ref, g_ref, da_ref, db_ref):
        rows = _SCAN_ROWS
        row = lax.broadcasted_iota(jnp.int32, (rows, cw), 0)
        n_tiles = t // rows

        def step(n, carry):
            lam_next, a_next = carry
            i = n_tiles - 1 - n
            r0 = pl.multiple_of(i * rows, rows)
            rp = pl.multiple_of(jnp.maximum(i - 1, 0) * rows, rows)
            av, gv, hv = a_ref[0, pl.ds(r0, rows), :], g_ref[0, pl.ds(r0, rows), :], h_ref[0, pl.ds(r0, rows), :]
            h_before = jnp.where(i > 0, h_ref[0, pl.ds(rp, rows), :][rows - 1:, :], 0.0)
            cv = jnp.where(row < rows - 1, pltpu.roll(av, rows - 1, 0), a_next)
            for s in (1, 2, 4):
                c_sh = jnp.where(row < rows - s, pltpu.roll(cv, rows - s, 0), 1.0)
                g_sh = jnp.where(row < rows - s, pltpu.roll(gv, rows - s, 0), 0.0)
                gv = cv * g_sh + gv
                cv = cv * c_sh
            lam = gv + cv * lam_next
            db_ref[0, pl.ds(r0, rows), :] = lam
            da_ref[0, pl.ds(r0, rows), :] = lam * jnp.where(row >= 1, pltpu.roll(hv, 1, 0), h_before)
            return lam[:1, :], av[:1, :]

        zero = jnp.zeros((1, cw), F32)
        lax.fori_loop(0, n_tiles, step, (zero, zero), unroll=2)

    spec = pl.BlockSpec((1, t, cw), lambda i, j: (i, 0, j))
    return pl.pallas_call(
        body,
        name=name,
        grid=(bsz, c // cw),
        in_specs=[spec, spec, spec],
        out_specs=[spec, spec],
        out_shape=[jax.ShapeDtypeStruct(a.shape, F32)] * 2,
        compiler_params=_params("parallel", "parallel"),
    )(a, h, g)


def _make_scan(name):
    @jax.custom_vjp
    def op(a, b):
        return _scan_fwd(a, b, name + "_fwd")

    def fwd(a, b):
        h = op(a, b)
        return h, (a, h)

    def bwd(res, g):
        a, h = res
        da, db = _scan_bwd(a, h, g, name + "_bwd")
        return da, db

    op.defvjp(fwd, bwd)
    return op


def _query_blocks(t):
    blocks, start = [], 0
    while start < t:
        rows = 2 * SEQ_BLOCK if start + 2 * SEQ_BLOCK <= t else SEQ_BLOCK
        blocks.append((start, rows))
        start += rows
    return blocks


def _attn_exp(q, k, start, scale):
    tq, tk = q.shape[0], k.shape[0]
    s = _dot(q, k, "nt") * scale
    qpos = start + lax.broadcasted_iota(jnp.int32, (tq, tk), 0)
    kpos = lax.broadcasted_iota(jnp.int32, (tq, tk), 1)
    s = jnp.where(kpos <= qpos, s, NEG_INF)
    e = jnp.exp(s - jnp.max(s, axis=-1, keepdims=True))
    return e, 1.0 / jnp.sum(e, axis=-1, keepdims=True)


_MLA_SCALE = (MLA_NOPE + MLA_ROPE) ** -0.5


def _attn_specs(t):
    head = pl.BlockSpec((1, t, LANES), lambda b, h: (b, 0, h))
    shared = pl.BlockSpec((1, t, LANES), lambda b, h: (b, 0, 0))
    return head, shared


def _attn_fwd(q, kv, kpe, name):
    bsz, t, hl = q.shape
    head, shared = _attn_specs(t)

    def body(q_ref, kv_ref, kpe_ref, o_ref, k_s, v_s):
        lane = lax.broadcasted_iota(jnp.int32, (t, LANES), 1)
        kvh = kv_ref[0]
        k_s[...] = jnp.where(lane < MLA_NOPE, kvh, kpe_ref[0]).astype(BF16)
        v_s[...] = kvh.astype(BF16)
        for start, rows in _query_blocks(t):
            n = start + rows
            e, inv_l = _attn_exp(q_ref[0, start:n, :], k_s[:n, :], start, _MLA_SCALE)
            o_ref[0, start:n, :] = _dot(e, v_s[:n, :], "nn") * inv_l

    return pl.pallas_call(
        body,
        name=name,
        grid=(bsz, hl // LANES),
        in_specs=[head, head, shared],
        out_specs=head,
        out_shape=jax.ShapeDtypeStruct(q.shape, F32),
        scratch_shapes=[pltpu.VMEM((t, LANES), BF16), pltpu.VMEM((t, LANES), BF16)],
        compiler_params=_params("parallel", "parallel"),
    )(q, kv, kpe)


def _attn_bwd(q, kv, kpe, do, name):
    bsz, t, hl = q.shape
    head, shared = _attn_specs(t)

    def body(q_ref, kv_ref, kpe_ref, do_ref, dq_ref, dkv_ref, dkpe_ref, k_s, v_s, dk_s, dv_s):
        lane = lax.broadcasted_iota(jnp.int32, (t, LANES), 1)
        kvh = kv_ref[0]
        k_s[...] = jnp.where(lane < MLA_NOPE, kvh, kpe_ref[0]).astype(BF16)
        v_s[...] = kvh.astype(BF16)
        for start, rows in reversed(_query_blocks(t)):
            n = start + rows
            qb = q_ref[0, start:n, :]
            dob = jnp.where(lane[:rows] >= MLA_NOPE, do_ref[0, start:n, :], 0.0)
            kk, vv = k_s[:n, :], v_s[:n, :]
            e, inv_l = _attn_exp(qb, kk, start, _MLA_SCALE)
            p = e * inv_l
            dp = _dot(dob, vv, "nt")
            ds = p * (dp - jnp.sum(dp * p, axis=-1, keepdims=True)) * _MLA_SCALE
            dq_ref[0, start:n, :] = _dot(ds, kk, "nn")
            if n == t:
                dk_s[...] = _dot(ds, qb, "tn")
                dv_s[...] = _dot(p, dob, "tn")
            else:
                dk_s[:n, :] += _dot(ds, qb, "tn")
                dv_s[:n, :] += _dot(p, dob, "tn")
        dk = dk_s[...]
        dkv_ref[0] = jnp.where(lane < MLA_NOPE, dk, dv_s[...])
        dkpe = jnp.where(lane >= MLA_NOPE, dk, 0.0)

        @pl.when(pl.program_id(1) == 0)
        def _():
            dkpe_ref[0] = dkpe

        @pl.when(pl.program_id(1) != 0)
        def _():
            dkpe_ref[0] += dkpe

    return pl.pallas_call(
        body,
        name=name,
        grid=(bsz, hl // LANES),
        in_specs=[head, head, shared, head],
        out_specs=[head, head, shared],
        out_shape=[
            jax.ShapeDtypeStruct(q.shape, F32),
            jax.ShapeDtypeStruct(kv.shape, F32),
            jax.ShapeDtypeStruct(kpe.shape, F32),
        ],
        scratch_shapes=[pltpu.VMEM((t, LANES), BF16), pltpu.VMEM((t, LANES), BF16),
                        pltpu.VMEM((t, LANES), F32), pltpu.VMEM((t, LANES), F32)],
        compiler_params=_params("parallel", "arbitrary"),
    )(q, kv, kpe, do)


def _make_attention(name):
    @jax.custom_vjp
    def op(q, kv, kpe):
        return _attn_fwd(q, kv, kpe, name + "_fwd")

    def fwd(q, kv, kpe):
        return op(q, kv, kpe), (q, kv, kpe)

    def bwd(res, do):
        return tuple(_attn_bwd(*res, do, name + "_bwd"))

    op.defvjp(fwd, bwd)
    return op


_ROPE_SHIFT = MLA_ROPE // 2


def _rope_lanes_call(x, c, sm, sp, transpose, name):
    r, width = x.shape
    tm = _row_tile(r, [width, width, LANES, LANES, LANES])

    def body(x_ref, c_ref, sm_ref, sp_ref, y_ref):
        cv, smv, spv = c_ref[...], sm_ref[...], sp_ref[...]
        for b in range(width // LANES):
            xb = x_ref[:, b * LANES:(b + 1) * LANES]
            if transpose:
                yb = xb * cv + pltpu.roll(xb * smv, _ROPE_SHIFT, 1) + pltpu.roll(xb * spv, LANES - _ROPE_SHIFT, 1)
            else:
                yb = xb * cv + pltpu.roll(xb, LANES - _ROPE_SHIFT, 1) * smv + pltpu.roll(xb, _ROPE_SHIFT, 1) * spv
            y_ref[:, b * LANES:(b + 1) * LANES] = yb

    tab = pl.BlockSpec((tm, LANES), lambda i: (i, 0))
    blk = pl.BlockSpec((tm, width), lambda i: (i, 0))
    return pl.pallas_call(
        body,
        name=name,
        grid=(r // tm,),
        in_specs=[blk, tab, tab, tab],
        out_specs=blk,
        out_shape=jax.ShapeDtypeStruct(x.shape, F32),
        compiler_params=_params("parallel"),
    )(x, c, sm, sp)


def _make_rope_lanes(name):
    @jax.custom_vjp
    def op(x, c, sm, sp):
        return _rope_lanes_call(x, c, sm, sp, False, name + "_fwd")

    def fwd(x, c, sm, sp):
        return op(x, c, sm, sp), (c, sm, sp)

    def bwd(res, dy):
        c, sm, sp = res
        return _rope_lanes_call(dy, c, sm, sp, True, name + "_bwd"), jnp.zeros_like(c), jnp.zeros_like(sm), jnp.zeros_like(sp)

    op.defvjp(fwd, bwd)
    return op


def _ret_chunk_rows(t):
    return t // 4 if t % 32 == 0 else SEQ_BLOCK


def _ret_decays(c, log_gamma):
    row = lax.broadcasted_iota(jnp.int32, (c, 1), 0)
    col = lax.broadcasted_iota(jnp.int32, (1, c), 1)
    rowf = row.astype(F32)
    d = jnp.where(row >= col, jnp.exp(log_gamma * rowf) * jnp.exp(-log_gamma * col.astype(F32)), 0.0)
    return d, jnp.exp(log_gamma * (rowf + 1.0)), jnp.exp(log_gamma * (c - 1.0 - rowf)), jnp.exp(log_gamma * c)


def _ret_specs(c, dk, dv, v_block0, n_chunks, reverse):
    pos = (lambda i: n_chunks - 1 - i) if reverse else (lambda i: i)
    return (
        pl.BlockSpec(memory_space=pltpu.SMEM),
        pl.BlockSpec((1, c, dk), lambda b, h, i: (b, pos(i), h)),
        pl.BlockSpec((1, c, dv), lambda b, h, i: (b, pos(i), h + v_block0)),
        pl.BlockSpec((1, c, dv), lambda b, h, i: (b, pos(i), h)),
        pl.BlockSpec((1, 1, dk, dv), lambda b, h, i: (b, h * n_chunks + pos(i), 0, 0)),
    )


def _ret_fwd(lg, q, k, v, name, dv=None, v_block0=0):
    bsz, t, hdk = q.shape
    heads = lg.shape[0]
    dk, dv = hdk // heads, dv or v.shape[2] // heads
    c = _ret_chunk_rows(t)
    n_chunks = t // c
    lg_spec, qk_spec, v_spec, o_spec, s_spec = _ret_specs(c, dk, dv, v_block0, n_chunks, False)

    def body(lg_ref, q_ref, k_ref, v_ref, o_ref, s_ref, state):
        @pl.when(pl.program_id(2) == 0)
        def _():
            state[...] = jnp.zeros((dk, dv), F32)

        d, a, b, g = _ret_decays(c, lg_ref[pl.program_id(1)])
        qb, kb, vb, s_in = q_ref[0], k_ref[0], v_ref[0], state[...]
        s_ref[0, 0] = s_in
        o_ref[0] = _dot(_dot(qb, kb, "nt") * d, vb, "nn") + a * _dot(qb, s_in, "nn")
        state[...] = g * s_in + _dot(kb * b, vb, "tn")

    return pl.pallas_call(
        body,
        name=name,
        grid=(bsz, heads, n_chunks),
        in_specs=[lg_spec, qk_spec, qk_spec, v_spec],
        out_specs=[o_spec, s_spec],
        out_shape=[jax.ShapeDtypeStruct((bsz, t, heads * dv), F32),
                   jax.ShapeDtypeStruct((bsz, heads * n_chunks, dk, dv), F32)],
        scratch_shapes=[pltpu.VMEM((dk, dv), F32)],
        compiler_params=_params("parallel", "parallel", "arbitrary"),
    )(lg, q, k, v)


def _ret_bwd(lg, q, k, v, states, do, name, v_block0=0, dv_dtype=F32):
    bsz, t, hdk = q.shape
    heads = lg.shape[0]
    dk, dv = hdk // heads, do.shape[2] // heads
    c = _ret_chunk_rows(t)
    n_chunks = t // c
    lg_spec, qk_spec, v_spec, o_spec, s_spec = _ret_specs(c, dk, dv, v_block0, n_chunks, True)

    def body(lg_ref, q_ref, k_ref, v_ref, s_ref, do_ref, dq_ref, dk_ref, dv_ref, dstate):
        @pl.when(pl.program_id(2) == 0)
        def _():
            dstate[...] = jnp.zeros((dk, dv), F32)

        d, a, b, g = _ret_decays(c, lg_ref[pl.program_id(1)])
        qb, kb, vb, dob, s_in, ds_out = q_ref[0], k_ref[0], v_ref[0], do_ref[0], s_ref[0, 0], dstate[...]
        scores = _dot(qb, kb, "nt") * d
        dscores = _dot(dob, vb, "nt") * d
        dq_ref[0] = _dot(dscores, kb, "nn") + a * _dot(dob, s_in, "nt")
        dk_ref[0] = _dot(dscores, qb, "tn") + b * _dot(vb, ds_out, "nt")
        dv_ref[0] = (_dot(scores, dob, "tn") + _dot(kb * b, ds_out, "nn")).astype(dv_dtype)
        dstate[...] = g * ds_out + _dot(qb, a * dob, "tn")

    return pl.pallas_call(
        body,
        name=name,
        grid=(bsz, heads, n_chunks),
        in_specs=[lg_spec, qk_spec, qk_spec, v_spec, s_spec, o_spec],
        out_specs=[qk_spec, qk_spec, o_spec],
        out_shape=[
            jax.ShapeDtypeStruct(q.shape, F32),
            jax.ShapeDtypeStruct(k.shape, F32),
            jax.ShapeDtypeStruct(do.shape, dv_dtype),
        ],
        scratch_shapes=[pltpu.VMEM((dk, dv), F32)],
        compiler_params=_params("parallel", "parallel", "arbitrary"),
    )(lg, q, k, v, states, do)


def _adamw(w, g, m, v, name):
    r, c = w.shape
    tr = _pick(r, (256, 128, 64, 32, 16, 8))

    def body(w_ref, g_ref, m_ref, v_ref, d_ref, nm_ref, nv_ref):
        gv = g_ref[...]
        nm = ADAM_B1 * m_ref[...] + (1.0 - ADAM_B1) * gv
        nv = ADAM_B2 * v_ref[...] + (1.0 - ADAM_B2) * (gv * gv)
        m_hat = nm / (1.0 - ADAM_B1 ** ADAM_STEP)
        v_hat = nv / (1.0 - ADAM_B2 ** ADAM_STEP)
        d_ref[...] = -ADAM_LR * (m_hat / (jnp.sqrt(v_hat) + ADAM_EPS) + ADAM_WD * w_ref[...])
        nm_ref[...] = nm
        nv_ref[...] = nv

    spec = pl.BlockSpec((tr, c), lambda i: (i, 0))
    return pl.pallas_call(
        body,
        name=name,
        grid=(r // tr,),
        in_specs=[spec] * 4,
        out_specs=[spec] * 3,
        out_shape=[jax.ShapeDtypeStruct((r, c), F32)] * 3,
        compiler_params=_params("parallel"),
    )(w, g, m, v)


def _rope_tables(t, half, reps):
    inv = ROPE_BASE ** (-jnp.arange(half, dtype=F32) / half)
    ang = jnp.arange(t, dtype=jnp.int32).astype(F32)[:, None] * inv[None, :]
    return jnp.tile(jnp.cos(ang), (1, reps)), jnp.tile(jnp.sin(ang), (1, reps))


def _padded_len(seq):
    return -(-(N_META + seq) // SEQ_BLOCK) * SEQ_BLOCK


def _embed(meta, x):
    bsz, seq, d = x.shape
    t = _padded_len(seq)
    return jnp.concatenate(
        [jnp.broadcast_to(meta[None], (bsz, N_META, d)), x, jnp.zeros((bsz, t - N_META - seq, d), F32)], axis=1
    ).reshape(bsz * t, d)


def _even_mixer(p, conv_w, conv_b, w_rg_a, b_rg_a, w_rg_x, b_rg_x, lru_lambda, q_norm_g, uq_slot, kv_norm_g,
                ukv_slot, gathered, bsz):
    w_uq_pad, w_ukv_full = gathered
    r = p.shape[0]
    t = r // bsz

    def tile_rows(tab):
        return jnp.tile(tab, (bsz, 1))

    lru_w = w_rg_a.shape[2] * w_rg_a.shape[1]
    q_rank, kv_rank = q_norm_g.shape[1], kv_norm_g.shape[1]
    p_gate, p_rec, p_q, p_kv, p_kpe = _split_cols(
        p, (lru_w, 2 * lru_w, 2 * lru_w + q_rank, 2 * lru_w + q_rank + kv_rank))

    xc = _make_conv("conv")(p_rec.reshape(bsz, t, lru_w), conv_w, conv_b).reshape(r, lru_w)
    ga, gx = _make_gate_pair("rg")(xc, w_rg_a[0], w_rg_x[0])
    (a, bb), _ = _make_rowwise(_lru_gates_f, "lru_gates", 3, 0, 3)((ga, gx, xc), (), (b_rg_a, b_rg_x, lru_lambda))
    hh = _make_scan("lru_scan")(a.reshape(bsz, t, lru_w), bb.reshape(bsz, t, lru_w)).reshape(r, lru_w)
    (y_rec,), _ = _make_rowwise(_lru_out_f, "lru_out", 2, 0, 0)((hh, p_gate), (), ())

    (qn,), _ = _make_rowwise(_rmsnorm_f, "q_norm", 1, 0, 1)((p_q,), (), (q_norm_g,))
    (kvn,), _ = _make_rowwise(_rmsnorm_f, "kv_norm", 1, 0, 1)((p_kv,), (), (kv_norm_g,))
    q = _make_slot_linear("ev_uq")(qn, w_uq_pad, uq_slot)
    kv = _make_slot_linear("ev_ukv")(kvn, w_ukv_full, ukv_slot)
    half = MLA_ROPE // 2
    cos, sin = _rope_tables(t, half, 1)
    one, zero = jnp.ones((t, MLA_NOPE), F32), jnp.zeros((t, MLA_NOPE), F32)
    tail = LANES - MLA_NOPE - MLA_ROPE
    c_tab = tile_rows(jnp.concatenate([one, cos, cos, one[:, :tail]], axis=1))
    sm_tab = tile_rows(jnp.concatenate([zero, -sin, zero[:, :half + tail]], axis=1))
    sp_tab = tile_rows(jnp.concatenate([zero, zero[:, :half], sin, zero[:, :tail]], axis=1))
    q = _make_rope_lanes("rope_q")(q, c_tab, sm_tab, sp_tab)
    kpe = _make_rope_lanes("rope_k")(p_kpe, c_tab, sm_tab, sp_tab)
    o = _make_attention("mla")(q.reshape(bsz, t, -1), kv.reshape(bsz, t, -1), kpe.reshape(bsz, t, LANES))
    return jnp.concatenate([y_rec, o.reshape(r, -1)], axis=1)


def _odd_mixer_fwd(p, bsz):
    r, width = p.shape
    t = r // bsz
    qk = width // 6
    dk = qk // RET_HEADS
    cos2, sin2 = _rope_tables(t, dk // 2, 2)
    sin2 = jnp.concatenate([-sin2[:, :dk // 2], sin2[:, dk // 2:]], axis=1)
    rope_args = (_Cols(p, qk, 0), _Cols(p, qk, 1), jnp.tile(cos2, (bsz, 1)), jnp.tile(sin2, (bsz, 1)))
    (rq, rk), _ = _make_rowwise(_rope_ret_f, "rope_ret", 2, 2, 0).fwd_call(*rope_args)
    lg = jnp.log(1.0 - 2.0 ** (-5.0 - jnp.arange(RET_HEADS, dtype=F32)))
    ret_args = (lg, rq.reshape(bsz, t, qk), rk.reshape(bsz, t, qk), p.reshape(bsz, t, width))
    o, states = _ret_fwd(*ret_args, "ret_fwd", dv=2 * dk, v_block0=qk // dk)
    gate_args = (o.reshape(r, 2 * qk), _Cols(p, 2 * qk, 2))
    (y,), _ = _make_rowwise(_ret_out_bf16_f, "ret_out", 2, 0, 0).fwd_call(*gate_args)
    return y, (rope_args, ret_args + (states,), gate_args)


def _odd_mixer_bwd(res, dy):
    rope_args, ret_args, gate_args = res
    bsz, t, qk = ret_args[1].shape
    dk = qk // RET_HEADS
    (do, dg), _ = _make_rowwise(_ret_out_f, "ret_out", 2, 0, 0).bwd_call(
        gate_args, ((dy,), ()), row_dtypes=(F32, BF16))
    drq, drk, dv = _ret_bwd(*ret_args, do.reshape(bsz, t, 2 * qk), "ret_bwd", v_block0=qk // dk, dv_dtype=BF16)
    (dq, dkk), _ = _make_rowwise(_rope_ret_f, "rope_ret", 2, 2, 0).bwd_call(
        rope_args, ((drq.reshape(bsz * t, qk), drk.reshape(bsz * t, qk)), ()), row_dtypes=(BF16, BF16))
    return jnp.concatenate([dq, dkk, dv.reshape(bsz * t, 2 * qk), dg], axis=1)


def _local_loss(h, target):
    bsz, seq, d = target.shape
    t = _padded_len(seq)
    t_real = N_META + seq
    pos = jnp.arange(t, dtype=jnp.int32)
    mask = jnp.tile(((pos >= N_META) & (pos < t_real)).astype(F32)[:, None], (bsz, 1))
    tgt = jnp.concatenate(
        [jnp.zeros((bsz, N_META, d), F32), target, jnp.zeros((bsz, t - t_real, d), F32)], axis=1).reshape(bsz * t, d)
    _, (total,) = _make_rowwise(_loss_f, "loss", 1, 2, 0)((h,), (tgt, mask), ())
    return total[0, 0]


_WEIGHTS = ("meta_tokens", "ev_w_in", "ev_conv_w", "ev_conv_b", "ev_w_rg_a", "ev_b_rg_a", "ev_w_rg_x", "ev_b_rg_x",
            "ev_lru_lambda", "ev_q_norm_g", "ev_w_uq", "ev_kv_norm_g", "ev_w_ukv", "ev_w_out", "od_w_in", "od_w_out",
            "ln_mix_g", "ln_mix_b", "mlp_w1", "mlp_w2", "ln_mlp_g", "ln_mlp_b")


def kernel(x, meta_tokens, ev_w_in, ev_conv_w, ev_conv_b, ev_w_rg_a, ev_b_rg_a, ev_w_rg_x, ev_b_rg_x, ev_lru_lambda, ev_q_norm_g, ev_w_uq, ev_kv_norm_g, ev_w_ukv, ev_w_out, od_w_in, od_w_out, ln_mix_g, ln_mix_b, mlp_w1, mlp_w2, ln_mlp_g, ln_mlp_b, loss_target, m_meta_tokens, m_ev_w_in, m_ev_conv_w, m_ev_conv_b, m_ev_w_rg_a, m_ev_b_rg_a, m_ev_w_rg_x, m_ev_b_rg_x, m_ev_lru_lambda, m_ev_q_norm_g, m_ev_w_uq, m_ev_kv_norm_g, m_ev_w_ukv, m_ev_w_out, m_od_w_in, m_od_w_out, m_ln_mix_g, m_ln_mix_b, m_mlp_w1, m_mlp_w2, m_ln_mlp_g, m_ln_mlp_b, v_meta_tokens, v_ev_w_in, v_ev_conv_w, v_ev_conv_b, v_ev_w_rg_a, v_ev_b_rg_a, v_ev_w_rg_x, v_ev_b_rg_x, v_ev_lru_lambda, v_ev_q_norm_g, v_ev_w_uq, v_ev_kv_norm_g, v_ev_w_ukv, v_ev_w_out, v_od_w_in, v_od_w_out, v_ln_mix_g, v_ln_mix_b, v_mlp_w1, v_mlp_w2, v_ln_mlp_g, v_ln_mlp_b):
    args = locals()
    weights = {n: args[n] for n in _WEIGHTS}
    bsz = x.shape[0]
    my_x, my_y, my_c = _my_place()
    me = 4 * my_x + 2 * my_y + my_c

    big = (("ev_in", ev_w_in[0], True), ("ev_out", ev_w_out[0], False), ("mlp0_w1", mlp_w1[0], True),
           ("mlp0_w2", mlp_w2[0], False), ("od_in", od_w_in[0], True), ("od_out", od_w_out[0], False),
           ("mlp1_w1", mlp_w1[1], True), ("mlp1_w2", mlp_w2[1], False))
    small_sharded = (("meta", meta_tokens, F32), ("conv_w", ev_conv_w[0], F32), ("ev_uq", ev_w_uq[0], BF16),
                     ("ev_ukv", ev_w_ukv[0], BF16))
    to_gather = (tuple((nm, s.astype(dt), True) for nm, s, dt in small_sharded)
                 + tuple((nm, s.astype(BF16), cols) for nm, s, cols in big))
    handles = _gather_start_all([s for _, s, _ in to_gather], "ag_start")
    gathers = {nm: (s, cols, h) for (nm, s, cols), h in zip(to_gather, handles)}
    gather_tokens = (handles[0][4],)

    def full_weight(nm, after):
        shard16, cols, handle = gathers[nm]
        land = _exchange_wait(handle, True, after, "ag_wait_" + nm)
        land = lax.dynamic_update_index_in_dim(land, shard16, me, 0)
        if cols and shard16.shape[1] % LANES == 0:
            return land, True
        return (_unstack_cols(land) if cols else land.reshape(-1, shard16.shape[1])), False

    meta_full, conv_w_full, w_uq_full, w_ukv_full = (
        _unstack_cols(lax.dynamic_update_index_in_dim(
            _exchange_wait(gathers[nm][2], True, gather_tokens[-1], "ag_wait_" + nm), gathers[nm][0], me, 0))
        for nm, _, _ in small_sharded)

    pending = []

    def linear_bwd(nm, x_in, w, dy, cols, unpad=None, **fused):
        w_full, w_stacked = w
        own = None
        if w_stacked:
            stacked = _matmul(x_in, dy, "tn", nm + "_dw", stacked=True)
        else:
            dw = _matmul(x_in, dy, "tn", nm + "_dw")
            dw = dw if unpad is None else unpad(dw)
            n = dw.shape[1] // N_DEV
            if cols:
                stacked = _stack_cols(dw)
                own = lax.dynamic_slice_in_dim(dw, me * n, n, axis=1)
            else:
                stacked = dw.reshape(N_DEV, dw.shape[0] // N_DEV, dw.shape[1])
        handle = _exchange_start(stacked, (_N_PEERS,) + stacked.shape[1:], False, "rs_start_" + nm)
        if own is None:
            own = lax.dynamic_index_in_dim(handle[2], me, 0, keepdims=False)
        pending.append((nm, own, handle))
        return _matmul(dy, w_full, "nt", nm + "_dx", after=(handle[4],), stacked=w_stacked, **fused)

    def linear_fwd(nm, x_in, w, **fused):
        return _matmul(x_in, w[0], "nn", nm + "_fwd", stacked=w[1], **fused)

    def mlp_fwd(h, h16, l):
        w1 = full_weight(f"mlp{l}_w1", h16)
        a16 = linear_fwd(f"mlp{l}_w1", h16, w1, out_relu2=True, out_dtype=BF16)
        w2 = full_weight(f"mlp{l}_w2", a16)
        f = linear_fwd(f"mlp{l}_w2", a16, w2)
        ln_args = (h, f, ln_mlp_g[l:l + 1], ln_mlp_b[l:l + 1])
        return ln_fwd(f"mlp{l}_ln", *ln_args), (h16, w1, a16, w2, ln_args)

    def mlp_bwd(dout, res, l):
        h16, w1, a16, w2, ln_args = res
        dh, df, dg, db = ln_bwd(f"mlp{l}_ln", ln_args, dout)
        du = linear_bwd(f"mlp{l}_w2", a16, w2, df, False, relu2_bwd_of=a16, out_dtype=BF16)
        return (dh, linear_bwd(f"mlp{l}_w1", h16, w1, du, True)), dg, db

    def ln_fwd(nm, h, mix, g, b):
        return _make_rowwise(_ln_res_copy_f, nm, 2, 0, 2).fwd_call(h, mix, g, b)[0]

    def ln_bwd(nm, ln_args, pieces):
        (dh, dmix), (dg, db) = _make_rowwise(_ln_res_f, nm, 2, 0, 2).bwd_call(
            ln_args, ((pieces[0],), ()), more=tuple(pieces[1:]), row_dtypes=(F32, BF16))
        return dh, dmix, dg, db

    h0, vjp_embed = jax.vjp(_embed, meta_full, x)
    n_in = ev_w_in.shape[2] * N_DEV
    kpe0, pad_lo, pad_hi = n_in - MLA_ROPE, MLA_NOPE, LANES - MLA_NOPE - MLA_ROPE
    w_in = full_weight("ev_in", h0)[0]
    zeros_in = jnp.zeros((w_in.shape[0], pad_lo), BF16)
    w_ev_in = (jnp.concatenate([w_in[:, :kpe0], zeros_in, w_in[:, kpe0:], zeros_in[:, :pad_hi]], axis=1), False)

    def unpad_in(dw):
        return jnp.concatenate([dw[:, :kpe0], dw[:, kpe0 + pad_lo:kpe0 + pad_lo + MLA_ROPE]], axis=1)

    p0 = _matmul(h0, w_ev_in[0], "nn", "ev_in_fwd", after=gather_tokens)
    q_rank, d_head = w_uq_full.shape[0], MLA_NOPE + MLA_ROPE
    w_uq_pad = jnp.pad(w_uq_full.reshape(q_rank, MLA_HEADS, d_head), ((0, 0), (0, 0), (0, LANES - d_head)))
    w_uq_pad = w_uq_pad.reshape(q_rank, MLA_HEADS * LANES)
    small = (conv_w_full, ev_conv_b, ev_w_rg_a, ev_b_rg_a, ev_w_rg_x, ev_b_rg_x, ev_lru_lambda, ev_q_norm_g,
             jnp.zeros(w_uq_pad.shape, F32), ev_kv_norm_g, jnp.zeros(w_ukv_full.shape, F32))
    y0, vjp_even = jax.vjp(lambda p, *s: _even_mixer(p, *s, (w_uq_pad, w_ukv_full), bsz), p0, *small)
    w_out = full_weight("ev_out", y0)[0]
    lru_w, d_model = y0.shape[1] - MLA_HEADS * LANES, w_out.shape[1]
    w_att = w_out[lru_w:].reshape(MLA_HEADS, MLA_V, d_model)
    w_att = jnp.concatenate([jnp.zeros((MLA_HEADS, LANES - MLA_V, d_model), BF16), w_att], axis=1)
    w_ev_out = (jnp.concatenate([w_out[:lru_w], w_att.reshape(MLA_HEADS * LANES, d_model)], axis=0), False)

    def unpad_out(dw):
        d_att = dw[lru_w:].reshape(MLA_HEADS, LANES, d_model)[:, LANES - MLA_V:].reshape(MLA_HEADS * MLA_V, d_model)
        return jnp.concatenate([dw[:lru_w], d_att], axis=0)

    mix0 = linear_fwd("ev_out", y0, w_ev_out)
    ln0_args = (h0, mix0, ln_mix_g[0:1], ln_mix_b[0:1])
    h1, h1_16 = ln_fwd("mix0_ln", *ln0_args)
    (h2, h2_16), res_mlp0 = mlp_fwd(h1, h1_16, 0)
    w_od_in = full_weight("od_in", h2_16)
    p1 = linear_fwd("od_in", h2_16, w_od_in)
    y1, res_odd = _odd_mixer_fwd(p1, bsz)
    w_od_out = full_weight("od_out", y1)
    mix1 = linear_fwd("od_out", y1, w_od_out)
    ln1_args = (h2, mix1, ln_mix_g[1:2], ln_mix_b[1:2])
    h3, h3_16 = ln_fwd("mix1_ln", *ln1_args)
    (h4, _), res_mlp1 = mlp_fwd(h3, h3_16, 1)
    loss_local, vjp_loss = jax.vjp(lambda h: _local_loss(h, loss_target), h4)

    dh4 = vjp_loss(jnp.ones((), F32))
    dh3, dg_mlp1, db_mlp1 = mlp_bwd(dh4, res_mlp1, 1)
    dh2, dmix1, dg_mix1, db_mix1 = ln_bwd("mix1_ln", ln1_args, dh3)
    dp1 = _odd_mixer_bwd(res_odd, linear_bwd("od_out", y1, w_od_out, dmix1, False))
    dh2 = (dh2, linear_bwd("od_in", h2_16, w_od_in, dp1, True))
    dh1, dg_mlp0, db_mlp0 = mlp_bwd(dh2, res_mlp0, 0)
    dh0, dmix0, dg_mix0, db_mix0 = ln_bwd("mix0_ln", ln0_args, dh1)
    dp0, *dsmall = vjp_even(linear_bwd("ev_out", y0, w_ev_out, dmix0, False, unpad=unpad_out))
    dh0 = dh0 + linear_bwd("ev_in", h0, w_ev_in, dp0.astype(BF16), True, unpad=unpad_in)
    g_meta_full, grad_x = vjp_embed(dh0)
    (g_conv_w_full, g_conv_b, g_w_rg_a, g_b_rg_a, g_w_rg_x, g_b_rg_x, g_lambda, g_q_norm, g_uq_pad, g_kv_norm,
     g_ukv_full) = dsmall

    for nm, dw in (("ev_uq", g_uq_pad.reshape(q_rank, MLA_HEADS, LANES)[:, :, :d_head].reshape(q_rank, -1)),
                   ("ev_ukv", g_ukv_full)):
        n = dw.shape[1] // N_DEV
        handle = _exchange_start(_stack_cols(dw), (_N_PEERS, dw.shape[0], n), False, "rs_start_" + nm)
        pending.append((nm, lax.dynamic_slice_in_dim(dw, me * n, n, axis=1), handle))

    rep_names = ("ev_conv_b", "ev_w_rg_a", "ev_b_rg_a", "ev_w_rg_x", "ev_b_rg_x", "ev_lru_lambda", "ev_q_norm_g",
                 "ev_kv_norm_g", "ln_mix_g", "ln_mix_b", "ln_mlp_g", "ln_mlp_b")
    rep_local = (g_conv_b, g_w_rg_a, g_b_rg_a, g_w_rg_x, g_b_rg_x, g_lambda, g_q_norm, g_kv_norm,
                 jnp.concatenate([dg_mix0, dg_mix1]), jnp.concatenate([db_mix0, db_mix1]),
                 jnp.concatenate([dg_mlp0, dg_mlp1]), jnp.concatenate([db_mlp0, db_mlp1]), g_conv_w_full, g_meta_full)
    rep_stacked = _pack_rows(rep_local).reshape(N_DEV, -1, LANES)
    rep_rs = _exchange_start(rep_stacked, (_N_PEERS,) + rep_stacked.shape[1:], False, "rep_rs_start", after=(grad_x,))
    rep_own = lax.dynamic_index_in_dim(rep_rs[2], me, 0, keepdims=False)

    after, summed = rep_rs[4], {}
    for nm, own, handle in pending:
        land = _exchange_wait(handle, False, after, "rs_wait_" + nm)
        summed[nm] = after = _sum_own_and_peers(own, land, "rs_sum_" + nm)
    rep_part = _sum_own_and_peers(rep_own, _exchange_wait(rep_rs, False, after, "rep_rs_wait"), "rep_sum")
    rep_ag = _exchange_start(rep_part, (N_DEV,) + rep_part.shape, True, "rep_ag_start")

    grad_w = dict(ev_w_uq=summed["ev_uq"][None], ev_w_ukv=summed["ev_ukv"][None],
                  ev_w_in=summed["ev_in"][None], ev_w_out=summed["ev_out"][None], od_w_in=summed["od_in"][None],
                  od_w_out=summed["od_out"][None], mlp_w1=jnp.stack([summed["mlp0_w1"], summed["mlp1_w1"]]),
                  mlp_w2=jnp.stack([summed["mlp0_w2"], summed["mlp1_w2"]]))

    loss = lax.psum(loss_local, MESH_AXES)
    delta, new_m, new_v = {}, {}, {}

    def adamw(n):
        w, g, m, v = weights[n], grad_w[n], args["m_" + n], args["v_" + n]
        two_d = (-1, w.shape[-1])
        d2, m2, v2 = _adamw(w.reshape(two_d), g.reshape(two_d), m.reshape(two_d), v.reshape(two_d), "adamw_" + n)
        delta[n], new_m[n], new_v[n] = d2.reshape(w.shape), m2.reshape(w.shape), v2.reshape(w.shape)
        return d2

    for n in tuple(grad_w):
        after = adamw(n)
    rep_land = lax.dynamic_update_index_in_dim(_exchange_wait(rep_ag, True, after, "rep_ag_wait"), rep_part, me, 0)
    *rep_total, t_conv_w, t_meta = _unpack_rows(rep_land, rep_local)
    n_conv, n_meta = ev_conv_w.shape[2], meta_tokens.shape[1]
    small_g = dict(zip(rep_names, rep_total))
    small_g.update(meta_tokens=lax.dynamic_slice_in_dim(t_meta, me * n_meta, n_meta, axis=1),
                   ev_conv_w=lax.dynamic_slice_in_dim(t_conv_w, me * n_conv, n_conv, axis=1)[None])
    grad_w.update(small_g)
    for n in small_g:
        adamw(n)
    return (loss, grad_x, *[grad_w[n] for n in _WEIGHTS], *[delta[n] for n in _WEIGHTS],
            *[new_m[n] for n in _WEIGHTS], *[new_v[n] for n in _WEIGHTS])
```

```python
import math

import jax
import jax.numpy as jnp
from jax import lax
from jax.experimental import pallas as pl
from jax.experimental.pallas import tpu as pltpu

F32 = jnp.float32
BF16 = jnp.bfloat16

N_DEV = 8
MESH_AXES = ("x", "y", "c")
LANES = 128
SEQ_BLOCK = 128

N_META = 16
LRU_C = 8.0
MLA_HEADS = 8
MLA_NOPE = 64
MLA_ROPE = 32
MLA_V = 64
RET_HEADS = 4
ROPE_BASE = 10000.0
DEPTH = 2
DN_ALPHA = (2 * DEPTH) ** 0.25
EPS = 1e-5
NEG_INF = -1e30

ADAM_LR = 0.001
ADAM_B1 = 0.9
ADAM_B2 = 0.999
ADAM_EPS = 1e-08
ADAM_WD = 0.01
ADAM_STEP = 10

VMEM_LIMIT = 56 * 1024 * 1024
TN_OPERAND_BYTES = 36 * 1024 * 1024


def _params(*sem):
    return pltpu.CompilerParams(dimension_semantics=sem, vmem_limit_bytes=VMEM_LIMIT)


def _pick(n, cands):
    for c in cands:
        if n % c == 0:
            return c
    return n


ROW_BLOCK_BYTES = 28 * 1024 * 1024


def _row_tile(r, widths):
    lanes = sum(max(w, LANES) for w in widths)
    for c in (1088, 544, 272, 256, 128, 64, 32, 16, 8):
        if r % c == 0 and 2 * 4 * lanes * c <= ROW_BLOCK_BYTES:
            return c
    return r


_DIMS = {"nn": (((1,), (0,)), ((), ())), "nt": (((1,), (1,)), ((), ())), "tn": (((0,), (0,)), ((), ()))}


def _dot(a, b, mode):
    return lax.dot_general(a.astype(BF16), b.astype(BF16), _DIMS[mode], preferred_element_type=F32)


def _matmul(a, b, mode, name, after=(), stacked=False, relu2_bwd_of=None, out_dtype=F32, out_relu2=False):
    if stacked:
        n_blk = b.shape[2] if mode != "tn" else b.shape[1] // N_DEV
    if mode == "nn":
        (m, k), n = a.shape, (N_DEV * n_blk if stacked else b.shape[1])
    elif mode == "nt":
        (m, k), n = a.shape, (b.shape[1] if stacked else b.shape[0])
    else:
        (k, m), n = a.shape, b.shape[1]
    tm = _pick(m, (2176, 1088, 1024, 544, 512, 272, 256, 128, 64, 32, 16, 8))
    tn = _pick(n, (1024, 512, 256, 128))
    tk = _pick(k, (2176, 1088, 1024, 544, 512, 272, 256, 128))
    kb = 2
    if stacked and mode == "nn":
        tn = n_blk
    if stacked and mode == "tn":
        tn = kb * n_blk
    if stacked and mode == "nt":
        tk = kb * n_blk
    if mode == "tn" and 2 * k * (tm * a.dtype.itemsize + tn * b.dtype.itemsize) <= TN_OPERAND_BYTES:
        tk = k
    nk = k // tk
    assert out_dtype == F32 or (nk == 1 and not (stacked and mode == "tn")), "narrow results need a single k step"
    assert not out_relu2 or nk == 1, "relu^2 is applied to a finished tile"

    out_spec = pl.BlockSpec((tm, tn), lambda i, j, kk: (i, j))
    out_shape = jax.ShapeDtypeStruct((m, n), out_dtype)
    if mode == "nn":
        a_spec = pl.BlockSpec((tm, tk), lambda i, j, kk: (i, kk))
        b_spec = pl.BlockSpec((tk, tn), lambda i, j, kk: (kk, j))
        if stacked:
            b_spec = pl.BlockSpec((None, tk, tn), lambda i, j, kk: (j, kk, 0))
    elif mode == "nt":
        a_spec = pl.BlockSpec((tm, tk), lambda i, j, kk: (i, kk))
        b_spec = pl.BlockSpec((tn, tk), lambda i, j, kk: (j, kk))
        if stacked:
            b_spec = pl.BlockSpec((kb, tn, n_blk), lambda i, j, kk: (kk, j, 0))
    else:
        a_spec = pl.BlockSpec((tk, tm), lambda i, j, kk: (kk, i))
        b_spec = pl.BlockSpec((tk, tn), lambda i, j, kk: (kk, j))
        if stacked:
            out_spec = pl.BlockSpec((kb, tm, n_blk), lambda i, j, kk: (j, i, 0))
            out_shape = jax.ShapeDtypeStruct((N_DEV, m, n_blk), F32)
    extra = [] if relu2_bwd_of is None else [relu2_bwd_of]
    extra_specs = [pl.BlockSpec((tm, tn), lambda i, j, kk: (i, j))] * len(extra)

    def body(a_ref, b_ref, *rest):
        def relu2_slope():
            return 2.0 * jnp.sqrt(rest[0][...].astype(F32))

        o_ref = rest[-1]
        kk = pl.program_id(2)
        av = a_ref[...]
        if stacked and mode == "nt":
            part = _dot(av[:, :n_blk], b_ref[0], mode)
            for q in range(1, kb):
                part = part + _dot(av[:, q * n_blk:(q + 1) * n_blk], b_ref[q], mode)
        else:
            part = _dot(av, b_ref[...], mode)
        if stacked and mode == "tn":
            part = jnp.stack([part[:, q * n_blk:(q + 1) * n_blk] for q in range(kb)])
        if nk == 1:
            if out_relu2:
                part = jnp.maximum(part, 0.0)
                part = part * part
            if relu2_bwd_of is not None:
                part = part * relu2_slope()
            o_ref[...] = part.astype(out_dtype)
            return

        @pl.when(kk == 0)
        def _():
            o_ref[...] = part

        @pl.when(kk != 0)
        def _():
            o_ref[...] += part

        if relu2_bwd_of is not None:
            @pl.when(kk == nk - 1)
            def _():
                o_ref[...] *= relu2_slope()

    return pl.pallas_call(
        body,
        name=name,
        grid=(m // tm, n // tn, nk),
        in_specs=[a_spec, b_spec] + extra_specs + [pl.BlockSpec(memory_space=pl.ANY)] * len(after),
        out_specs=out_spec,
        out_shape=out_shape,
        compiler_params=_params("parallel", "parallel", "arbitrary"),
    )(a, b, *extra, *after)


def _make_gate_pair(name):
    def call(body, ins, n_out, reduce_rows, tag):
        x = ins[0]
        m, d = x.shape[0], LANES
        g = x.shape[1] // d
        tm = _pick(m, (1088, 1024, 544, 512, 272, 256, 128, 64, 32, 16, 8))
        rows = pl.BlockSpec((tm, d), lambda h, i: (i, h))
        mats = pl.BlockSpec((1, d, d), lambda h, i: (h, 0, 0))
        return pl.pallas_call(
            body,
            name=name + tag,
            grid=(g, m // tm),
            in_specs=[rows if a.ndim == 2 else mats for a in ins],
            out_specs=[mats if reduce_rows else rows] * n_out,
            out_shape=[jax.ShapeDtypeStruct((g, d, d) if reduce_rows else (m, g * d), F32)] * n_out,
            compiler_params=_params("parallel", "arbitrary" if reduce_rows else "parallel"),
        )(*ins)

    def fwd_body(x_ref, wa_ref, wx_ref, a_ref, b_ref):
        xv = x_ref[...]
        a_ref[...] = _dot(xv, wa_ref[0], "nn")
        b_ref[...] = _dot(xv, wx_ref[0], "nn")

    def dx_body(da_ref, db_ref, wa_ref, wx_ref, dx_ref):
        dx_ref[...] = _dot(da_ref[...], wa_ref[0], "nt") + _dot(db_ref[...], wx_ref[0], "nt")

    def dw_body(x_ref, da_ref, db_ref, dwa_ref, dwx_ref):
        xv = x_ref[...]
        pa, pb = _dot(xv, da_ref[...], "tn"), _dot(xv, db_ref[...], "tn")

        @pl.when(pl.program_id(1) == 0)
        def _():
            dwa_ref[0] = pa
            dwx_ref[0] = pb

        @pl.when(pl.program_id(1) != 0)
        def _():
            dwa_ref[0] += pa
            dwx_ref[0] += pb

    @jax.custom_vjp
    def op(x, wa, wx):
        return tuple(call(fwd_body, (x, wa, wx), 2, False, "_fwd"))

    def fwd(x, wa, wx):
        return op(x, wa, wx), (x, wa, wx)

    def bwd(res, cots):
        x, wa, wx = res
        da, db = cots
        (dx,) = call(dx_body, (da, db, wa, wx), 1, False, "_dx")
        dwa, dwx = call(dw_body, (x, da, db), 2, True, "_dw")
        return dx, dwa, dwx

    op.defvjp(fwd, bwd)
    return op


def _my_place():
    return lax.axis_index("x"), lax.axis_index("y"), lax.axis_index("c")


def _stack_cols(full):
    k, n8 = full.shape
    return full.reshape(k, N_DEV, n8 // N_DEV).transpose(1, 0, 2)


def _unstack_cols(stacked):
    j, k, n = stacked.shape
    return stacked.transpose(1, 0, 2).reshape(k, j * n)


def _split_cols(p, cuts):
    bounds = (0,) + tuple(cuts) + (p.shape[1],)

    @jax.custom_vjp
    def op(z):
        return tuple(z[:, lo:hi] for lo, hi in zip(bounds[:-1], bounds[1:]))

    op.defvjp(lambda z: (op(z), None), lambda _, cots: (jnp.concatenate(cots, axis=1),))
    return op(p)


def _make_slot_linear(name):
    @jax.custom_vjp
    def op(x, w_full, slot):
        return _matmul(x, w_full, "nn", name + "_fwd")

    def fwd(x, w_full, slot):
        return op(x, w_full, slot), (x, w_full)

    def bwd(res, dy):
        x, w = res
        return _matmul(dy, w, "nt", name + "_dx"), jnp.zeros_like(w), _matmul(x, dy, "tn", name + "_dw")

    op.defvjp(fwd, bwd)
    return op


def _pack_rows(gs):
    flat = jnp.concatenate([g.reshape(-1) for g in gs])
    n = flat.shape[0]
    rows = -(-n // (256 * LANES)) * 256
    return jnp.pad(flat, (0, rows * LANES - n)).reshape(rows, LANES)


def _unpack_rows(packed, like):
    flat, out, off = packed.reshape(-1), [], 0
    for g in like:
        out.append(flat[off:off + g.size].reshape(g.shape))
        off += g.size
    return out


_HBM = pl.BlockSpec(memory_space=pltpu.HBM)
_SEM = pl.BlockSpec(memory_space=pltpu.SEMAPHORE)
_SIDE_EFFECT = pltpu.SideEffectType.DATAFLOW_SIDE_EFFECTING
_N_PEERS = N_DEV - 1


def _peer(k):
    x, y, c = _my_place()
    return x ^ ((k >> 2) & 1), y ^ ((k >> 1) & 1), c ^ (k & 1)


def _exchange_start(src, land_shape, gather, name, after=()):
    def body(src_ref, land_ref, *rest):
        send_sems, recv_sems, src_thru, land_thru, token = rest[len(after):]
        x, y, c = _my_place()
        me = 4 * x + 2 * y + c
        for k in range(1, N_DEV):
            px, py, pc = _peer(k)
            pltpu.make_async_remote_copy(
                src_ref=src_ref if gather else src_ref.at[4 * px + 2 * py + pc],
                dst_ref=land_ref.at[me] if gather else land_ref.at[k - 1],
                send_sem=send_sems.at[k - 1],
                recv_sem=recv_sems.at[k - 1],
                device_id=(px, py, pc),
                device_id_type=pl.DeviceIdType.MESH,
            ).start()
        token[...] = jnp.zeros_like(token)

    return pl.pallas_call(
        body,
        name=name,
        out_shape=(
            pltpu.SemaphoreType.DMA((_N_PEERS,)),
            pltpu.SemaphoreType.DMA((_N_PEERS,)),
            pltpu.HBM(src.shape, src.dtype),
            pltpu.HBM(land_shape, src.dtype),
            jax.ShapeDtypeStruct((8, LANES), F32),
        ),
        in_specs=(_HBM, _HBM) + (pl.BlockSpec(memory_space=pl.ANY),) * len(after),
        out_specs=(_SEM, _SEM, _HBM, _HBM, pl.BlockSpec(memory_space=pltpu.VMEM)),
        input_output_aliases={0: 2, 1: 3},
        compiler_params=pltpu.CompilerParams(has_side_effects=_SIDE_EFFECT),
    )(pltpu.with_memory_space_constraint(src, pltpu.HBM),
      pltpu.with_memory_space_constraint(lax.empty(land_shape, src.dtype), pltpu.HBM), *after)


def _gather_start_all(shards, name):
    n = len(shards)

    def body(*refs):
        srcs, lands = refs[:n], refs[n:2 * n]
        outs = refs[2 * n:]
        send_sems, recv_sems, token = outs[:n], outs[n:2 * n], outs[-1]
        x, y, c = _my_place()
        me = 4 * x + 2 * y + c
        for i in range(n):
            for k in range(1, N_DEV):
                pltpu.make_async_remote_copy(
                    src_ref=srcs[i],
                    dst_ref=lands[i].at[me],
                    send_sem=send_sems[i].at[k - 1],
                    recv_sem=recv_sems[i].at[k - 1],
                    device_id=_peer(k),
                    device_id_type=pl.DeviceIdType.MESH,
                ).start()
        token[...] = jnp.zeros_like(token)

    lands = [(N_DEV,) + s.shape for s in shards]
    sems = tuple(pltpu.SemaphoreType.DMA((_N_PEERS,)) for _ in range(2 * n))
    res = pl.pallas_call(
        body,
        name=name,
        out_shape=sems + tuple(pltpu.HBM(s.shape, s.dtype) for s in shards)
        + tuple(pltpu.HBM(ls, s.dtype) for ls, s in zip(lands, shards)) + (jax.ShapeDtypeStruct((8, LANES), F32),),
        in_specs=(_HBM,) * (2 * n),
        out_specs=(_SEM,) * (2 * n) + (_HBM,) * (2 * n) + (pl.BlockSpec(memory_space=pltpu.VMEM),),
        input_output_aliases={i: 2 * n + i for i in range(2 * n)},
        compiler_params=pltpu.CompilerParams(has_side_effects=_SIDE_EFFECT),
    )(*[pltpu.with_memory_space_constraint(s, pltpu.HBM) for s in shards],
      *[pltpu.with_memory_space_constraint(lax.empty(ls, s.dtype), pltpu.HBM) for ls, s in zip(lands, shards)])
    return [(res[i], res[n + i], res[2 * n + i], res[3 * n + i], res[-1]) for i in range(n)]


def _exchange_wait(handle, gather, after, name):
    send_sems, recv_sems, src_thru, land_thru, _ = handle

    def body(src_ref, land_ref, send_sems, recv_sems, after_ref, src_dead, got_ref):
        for k in range(1, N_DEV):
            cp = pltpu.make_async_remote_copy(
                src_ref=src_ref if gather else src_ref.at[k],
                dst_ref=land_ref.at[k - 1],
                send_sem=send_sems.at[k - 1],
                recv_sem=recv_sems.at[k - 1],
                device_id=_peer(k),
                device_id_type=pl.DeviceIdType.MESH,
            )
            cp.wait_send()
            cp.wait_recv()

    return pl.pallas_call(
        body,
        name=name,
        out_shape=(pltpu.HBM(src_thru.shape, src_thru.dtype), pltpu.HBM(land_thru.shape, land_thru.dtype)),
        in_specs=(_HBM, _HBM, _SEM, _SEM, pl.BlockSpec(memory_space=pl.ANY)),
        out_specs=(_HBM, _HBM),
        input_output_aliases={0: 0, 1: 1},
        compiler_params=pltpu.CompilerParams(has_side_effects=_SIDE_EFFECT),
    )(src_thru, land_thru, send_sems, recv_sems, after)[1]


def _sum_own_and_peers(own, land, name):
    r, c = own.shape
    tr = _pick(r, (256, 128, 64, 32, 16, 8))

    def body(o_ref, l_ref, out_ref):
        s = [l_ref[j] for j in range(_N_PEERS)]
        out_ref[...] = ((o_ref[...] + s[0]) + (s[1] + s[2])) + ((s[3] + s[4]) + (s[5] + s[6]))

    return pl.pallas_call(
        body,
        name=name,
        grid=(r // tr,),
        in_specs=[pl.BlockSpec((tr, c), lambda i: (i, 0)), pl.BlockSpec((_N_PEERS, tr, c), lambda i: (0, i, 0))],
        out_specs=pl.BlockSpec((tr, c), lambda i: (i, 0)),
        out_shape=jax.ShapeDtypeStruct((r, c), own.dtype),
        compiler_params=_params("parallel"),
    )(own, land)


class _Cols:
    def __init__(self, array, width, block):
        self.array, self.width, self.block = array, width, block
        self.shape, self.dtype = (array.shape[0], width), array.dtype


def _base(a):
    return a.array if isinstance(a, _Cols) else a


def _col_block(a):
    return a.block if isinstance(a, _Cols) else 0


def _make_rowwise(f, name, n_rows, n_tabs, n_pars):
    n_in = n_rows + n_tabs + n_pars

    def specs(args, tm):
        blocked = [pl.BlockSpec((tm, a.shape[1]), lambda i, blk=_col_block(a): (i, blk)) for a in args[: n_rows + n_tabs]]
        whole = [pl.BlockSpec(a.shape, lambda i: (0, 0)) for a in args[n_rows + n_tabs:]]
        return blocked + whole

    def out_struct(args, tm):
        blk = [jax.ShapeDtypeStruct((tm, a.shape[1]), a.dtype) for a in args[: n_rows + n_tabs]]
        blk += [jax.ShapeDtypeStruct(a.shape, a.dtype) for a in args[n_rows + n_tabs:]]
        return jax.eval_shape(f, *blk)

    def fwd_call(*args):
        r = args[0].shape[0]
        tm = _row_tile(r, [a.shape[1] for a in args[:n_rows + n_tabs]] + [s.shape[1] for s in out_struct(args, 8)[0]])
        ro, so = out_struct(args, tm)

        def body(*refs):
            vals = [x[...] for x in refs[:n_in]]
            outs = refs[n_in:]
            rv, sv = f(*vals)
            for o, v in zip(outs[: len(ro)], rv):
                o[...] = v
            for o, v in zip(outs[len(ro):], sv):
                @pl.when(pl.program_id(0) == 0)
                def _(o=o, v=v):
                    o[...] = v

                @pl.when(pl.program_id(0) != 0)
                def _(o=o, v=v):
                    o[...] += v

        out_shape = [jax.ShapeDtypeStruct((r, s.shape[1]), s.dtype) for s in ro]
        out_shape += [jax.ShapeDtypeStruct(s.shape, s.dtype) for s in so]
        out_specs = [pl.BlockSpec((tm, s.shape[1]), lambda i: (i, 0)) for s in ro]
        out_specs += [pl.BlockSpec(s.shape, lambda i: (0, 0)) for s in so]
        res = pl.pallas_call(
            body,
            name=name + "_fwd",
            grid=(r // tm,),
            in_specs=specs(args, tm),
            out_specs=out_specs,
            out_shape=out_shape,
            compiler_params=_params("arbitrary" if so else "parallel"),
        )(*[_base(a) for a in args])
        return tuple(res[: len(ro)]), tuple(res[len(ro):])

    def bwd_call(args, cots, more=(), row_dtypes=None):
        r = args[0].shape[0]
        crow, csum = cots
        tm = _row_tile(r, [a.shape[1] for a in tuple(args[:n_rows + n_tabs]) + tuple(crow) + tuple(more)]
                       + [a.shape[1] for a in args[:n_rows]])
        ro, so = out_struct(args, tm)
        rows, tabs, pars = args[:n_rows], args[n_rows:n_rows + n_tabs], args[n_rows + n_tabs:]
        n_c = len(crow) + len(csum)

        def body(*refs):
            vals = [x[...] for x in refs[:n_in]]
            cv = [x[...] for x in refs[n_in:n_in + n_c]]
            for x in refs[n_in + n_c:n_in + n_c + len(more)]:
                cv[0] = cv[0] + x[...]
            outs = refs[n_in + n_c + len(more):]
            tv = vals[n_rows:n_rows + n_tabs]

            def g(*dargs):
                return f(*dargs[:n_rows], *tv, *dargs[n_rows:])

            _, vjp = jax.vjp(g, *vals[:n_rows], *vals[n_rows + n_tabs:])
            d = vjp((tuple(cv[: len(crow)]), tuple(cv[len(crow):])))
            for o, v in zip(outs[:n_rows], d[:n_rows]):
                o[...] = v.astype(o.dtype)
            for o, v in zip(outs[n_rows:], d[n_rows:]):
                @pl.when(pl.program_id(0) == 0)
                def _(o=o, v=v):
                    o[...] = v

                @pl.when(pl.program_id(0) != 0)
                def _(o=o, v=v):
                    o[...] += v

        in_specs = specs(args, tm)
        in_specs += [pl.BlockSpec((tm, c.shape[1]), lambda i: (i, 0)) for c in crow]
        in_specs += [pl.BlockSpec(c.shape, lambda i: (0, 0)) for c in csum]
        in_specs += [pl.BlockSpec((tm, c.shape[1]), lambda i: (i, 0)) for c in more]
        out_shape = [jax.ShapeDtypeStruct(a.shape, dt) for a, dt in zip(rows, row_dtypes or [a.dtype for a in rows])]
        out_shape += [jax.ShapeDtypeStruct(a.shape, a.dtype) for a in pars]
        out_specs = [pl.BlockSpec((tm, a.shape[1]), lambda i: (i, 0)) for a in rows]
        out_specs += [pl.BlockSpec(a.shape, lambda i: (0, 0)) for a in pars]
        res = pl.pallas_call(
            body,
            name=name + "_bwd",
            grid=(r // tm,),
            in_specs=in_specs,
            out_specs=out_specs,
            out_shape=out_shape,
            compiler_params=_params("arbitrary" if pars else "parallel"),
        )(*[_base(a) for a in args], *crow, *csum, *more)
        return tuple(res[:n_rows]), tuple(res[n_rows:])

    @jax.custom_vjp
    def op(rows, tabs, pars):
        return fwd_call(*rows, *tabs, *pars)

    op.fwd_call, op.bwd_call = fwd_call, bwd_call

    def fwd(rows, tabs, pars):
        return fwd_call(*rows, *tabs, *pars), (rows, tabs, pars)

    def bwd(res, cots):
        rows, tabs, pars = res
        drows, dpars = bwd_call(tuple(rows) + tuple(tabs) + tuple(pars), cots)
        return drows, tuple(jnp.zeros_like(t) for t in tabs), dpars

    op.defvjp(fwd, bwd)
    return op


def _sigmoid(x):
    return 0.5 * (jnp.tanh(0.5 * x) + 1.0)


@jax.custom_jvp
def _softplus(x):
    e = jnp.exp(-jnp.abs(x))
    u = 1.0 + e
    log1p_e = jnp.where(u == 1.0, e, e * jnp.log(u) / jnp.where(u == 1.0, 1.0, u - 1.0))
    return jnp.maximum(x, 0.0) + log1p_e


@_softplus.defjvp
def _softplus_jvp(primals, tangents):
    (x,), (t,) = primals, tangents
    return _softplus(x), t * _sigmoid(x)


def _gelu(x):
    return 0.5 * x * (1.0 + jnp.tanh(math.sqrt(2.0 / math.pi) * (x + 0.044715 * (x * x * x))))


def _ln_res_f(h, mix, g, b):
    z = DN_ALPHA * h + mix
    mu = jnp.mean(z, axis=-1, keepdims=True)
    zc = z - mu
    var = jnp.mean(zc * zc, axis=-1, keepdims=True)
    return (zc * lax.rsqrt(var + EPS) * g + b,), ()


def _ln_res_copy_f(h, mix, g, b):
    (out,), _ = _ln_res_f(h, mix, g, b)
    return (out, out.astype(BF16)), ()


def _rmsnorm_f(x, g):
    return (x * lax.rsqrt(jnp.mean(x * x, axis=-1, keepdims=True) + EPS) * g,), ()


def _lru_gates_f(ga, gx, xc, b_a, b_x, lam):
    r = _sigmoid(ga + b_a)
    i = _sigmoid(gx + b_x)
    log_a = -LRU_C * r * _softplus(-lam)
    a = jnp.exp(log_a)
    one_minus_a2 = jnp.tanh(-log_a) * (jnp.exp(2.0 * log_a) + 1.0)
    return (a, jnp.sqrt(one_minus_a2) * (i * xc)), ()


def _lru_out_f(hh, p_gate):
    return (hh * _gelu(p_gate),), ()


def _rope_ret_f(q, k, cos2, sin2):
    d = cos2.shape[1]
    half = d // 2
    k_scale = d ** -0.5

    def rope(x):
        outs = []
        for h in range(x.shape[1] // d):
            xh = x[:, h * d:(h + 1) * d]
            rot = jnp.concatenate([xh[:, half:], xh[:, :half]], axis=1)
            outs.append(xh * cos2 + rot * sin2)
        return jnp.concatenate(outs, axis=1)

    return (rope(q), rope(k) * k_scale), ()


def _ret_out_f(o, g):
    d = o.shape[1] // RET_HEADS
    outs = []
    for h in range(RET_HEADS):
        oh = o[:, h * d:(h + 1) * d]
        outs.append(oh * lax.rsqrt(jnp.mean(oh * oh, axis=-1, keepdims=True) + EPS))
    y = jnp.concatenate(outs, axis=1)
    return (g * _sigmoid(g) * y,), ()


def _ret_out_bf16_f(o, g):
    (y,), _ = _ret_out_f(o, g)
    return (y.astype(BF16),), ()


def _loss_f(y, t, mask):
    e = (y - t) * mask
    per_row = jnp.sum(e * e, axis=-1, keepdims=True) * (0.5 / y.shape[1])
    total = jnp.sum(per_row, axis=0, keepdims=True)
    return (), (jnp.broadcast_to(total, (1, LANES)),)


def _shift_down(x, s):
    if s == 0:
        return x
    t = x.shape[0]
    row = lax.broadcasted_iota(jnp.int32, x.shape, 0)
    return jnp.where(row >= s, pltpu.roll(x, s, 0), 0.0)


def _shift_up(x, s):
    if s == 0:
        return x
    t = x.shape[0]
    row = lax.broadcasted_iota(jnp.int32, x.shape, 0)
    return jnp.where(row < t - s, pltpu.roll(x, t - s, 0), 0.0)


def _conv_fwd(x, w, b, name):
    bsz, t, c = x.shape
    width = w.shape[0]

    def body(x_ref, w_ref, b_ref, y_ref):
        xv = x_ref[0]
        acc = jnp.broadcast_to(b_ref[...], xv.shape)
        for k in range(width):
            acc = acc + w_ref[k:k + 1, :] * _shift_down(xv, width - 1 - k)
        y_ref[0] = acc

    return pl.pallas_call(
        body,
        name=name,
        grid=(bsz, c // LANES),
        in_specs=[
            pl.BlockSpec((1, t, LANES), lambda i, j: (i, 0, j)),
            pl.BlockSpec((width, LANES), lambda i, j: (0, j)),
            pl.BlockSpec((1, LANES), lambda i, j: (0, j)),
        ],
        out_specs=pl.BlockSpec((1, t, LANES), lambda i, j: (i, 0, j)),
        out_shape=jax.ShapeDtypeStruct(x.shape, F32),
        compiler_params=_params("parallel", "parallel"),
    )(x, w, b)


def _conv_bwd(x, w, dy, name):
    bsz, t, c = x.shape
    width = w.shape[0]

    def body(x_ref, w_ref, dy_ref, dx_ref, dw_ref, db_ref):
        xv, g = x_ref[0], dy_ref[0]
        dx = jnp.zeros_like(xv)
        dws = []
        for k in range(width):
            s = width - 1 - k
            dx = dx + w_ref[k:k + 1, :] * _shift_up(g, s)
            dws.append(jnp.sum(g * _shift_down(xv, s), axis=0, keepdims=True))
        dx_ref[0] = dx
        dw = jnp.concatenate(dws, axis=0)
        db = jnp.sum(g, axis=0, keepdims=True)

        @pl.when(pl.program_id(1) == 0)
        def _():
            dw_ref[...] = dw
            db_ref[...] = db

        @pl.when(pl.program_id(1) != 0)
        def _():
            dw_ref[...] += dw
            db_ref[...] += db

    return pl.pallas_call(
        body,
        name=name,
        grid=(c // LANES, bsz),
        in_specs=[
            pl.BlockSpec((1, t, LANES), lambda j, i: (i, 0, j)),
            pl.BlockSpec((width, LANES), lambda j, i: (0, j)),
            pl.BlockSpec((1, t, LANES), lambda j, i: (i, 0, j)),
        ],
        out_specs=[
            pl.BlockSpec((1, t, LANES), lambda j, i: (i, 0, j)),
            pl.BlockSpec((width, LANES), lambda j, i: (0, j)),
            pl.BlockSpec((1, LANES), lambda j, i: (0, j)),
        ],
        out_shape=[
            jax.ShapeDtypeStruct(x.shape, F32),
            jax.ShapeDtypeStruct(w.shape, F32),
            jax.ShapeDtypeStruct((1, c), F32),
        ],
        compiler_params=_params("parallel", "arbitrary"),
    )(x, w, dy)


def _make_conv(name):
    @jax.custom_vjp
    def op(x, w, b):
        return _conv_fwd(x, w, b, name + "_fwd")

    def fwd(x, w, b):
        return op(x, w, b), (x, w)

    def bwd(res, dy):
        x, w = res
        return tuple(_conv_bwd(x, w, dy, name + "_bwd"))

    op.defvjp(fwd, bwd)
    return op


_SCAN_ROWS = 8


def _scan_fwd(a, b, name):
    bsz, t, c = a.shape
    cw = _pick(c, (4 * LANES, 2 * LANES, LANES))

    def body(a_ref, b_ref, h_ref):
        row = lax.broadcasted_iota(jnp.int32, (_SCAN_ROWS, cw), 0)

        def step(i, carry):
            r0 = pl.multiple_of(i * _SCAN_ROWS, _SCAN_ROWS)
            av, bv = a_ref[0, pl.ds(r0, _SCAN_ROWS), :], b_ref[0, pl.ds(r0, _SCAN_ROWS), :]
            for s in (1, 2, 4):
                a_sh = jnp.where(row >= s, pltpu.roll(av, s, 0), 1.0)
                b_sh = jnp.where(row >= s, pltpu.roll(bv, s, 0), 0.0)
                bv = av * b_sh + bv
                av = av * a_sh
            hv = bv + av * carry
            h_ref[0, pl.ds(r0, _SCAN_ROWS), :] = hv
            return hv[_SCAN_ROWS - 1:, :]

        lax.fori_loop(0, t // _SCAN_ROWS, step, jnp.zeros((1, cw), F32), unroll=2)

    spec = pl.BlockSpec((1, t, cw), lambda i, j: (i, 0, j))
    return pl.pallas_call(
        body,
        name=name,
        grid=(bsz, c // cw),
        in_specs=[spec, spec],
        out_specs=spec,
        out_shape=jax.ShapeDtypeStruct(a.shape, F32),
        compiler_params=_params("parallel", "parallel"),
    )(a, b)


def _scan_bwd(a, h, g, name):
    bsz, t, c = a.shape
    cw = _pick(c, (2 * LANES, LANES))

    def body(a_ref, h_ref, g_ref, da_ref, db_ref):
        rows = _SCAN_ROWS
        row = lax.broadcasted_iota(jnp.int32, (rows, cw), 0)
        n_tiles = t // rows

        def step(n, carry):
            lam_next, a_next = carry
            i = n_tiles - 1 - n
            r0 = pl.multiple_of(i * rows, rows)
            rp = pl.multiple_of(jnp.maximum(i - 1, 0) * rows, rows)
            av, gv, hv = a_ref[0, pl.ds(r0, rows), :], g_ref[0, pl.ds(r0, rows), :], h_ref[0, pl.ds(r0, rows), :]
            h_before = jnp.where(i > 0, h_ref[0, pl.ds(rp, rows), :][rows - 1:, :], 0.0)
            cv = jnp.where(row < rows - 1, pltpu.roll(av, rows - 1, 0), a_next)
            for s in (1, 2, 4):
                c_sh = jnp.where(row < rows - s, pltpu.roll(cv, rows - s, 0), 1.0)
                g_sh = jnp.where(row < rows - s, pltpu.roll(gv, rows - s, 0), 0.0)
                gv = cv * g_sh + gv
                cv = cv * c_sh
            lam = gv + cv * lam_next
            db_ref[0, pl.ds(r0, rows), :] = lam
            da_ref[0, pl.ds(r0, rows), :] = lam * jnp.where(row >= 1, pltpu.roll(hv, 1, 0), h_before)
            return lam[:1, :], av[:1, :]

        zero = jnp.zeros((1, cw), F32)
        lax.fori_loop(0, n_tiles, step, (zero, zero), unroll=2)

    spec = pl.BlockSpec((1, t, cw), lambda i, j: (i, 0, j))
    return pl.pallas_call(
        body,
        name=name,
        grid=(bsz, c // cw),
        in_specs=[spec, spec, spec],
        out_specs=[spec, spec],
        out_shape=[jax.ShapeDtypeStruct(a.shape, F32)] * 2,
        compiler_params=_params("parallel", "parallel"),
    )(a, h, g)


def _make_scan(name):
    @jax.custom_vjp
    def op(a, b):
        return _scan_fwd(a, b, name + "_fwd")

    def fwd(a, b):
        h = op(a, b)
        return h, (a, h)

    def bwd(res, g):
        a, h = res
        da, db = _scan_bwd(a, h, g, name + "_bwd")
        return da, db

    op.defvjp(fwd, bwd)
    return op


def _query_blocks(t, tall=2):
    blocks, start = [], 0
    while start < t:
        rows = tall * SEQ_BLOCK if start + tall * SEQ_BLOCK <= t else SEQ_BLOCK
        blocks.append((start, rows))
        start += rows
    return blocks


def _attn_exp(q, k, start, scale):
    tq, tk = q.shape[0], k.shape[0]
    s = _dot(q, k, "nt") * scale
    qpos = start + lax.broadcasted_iota(jnp.int32, (tq, tk), 0)
    kpos = lax.broadcasted_iota(jnp.int32, (tq, tk), 1)
    s = jnp.where(kpos <= qpos, s, NEG_INF)
    e = jnp.exp(s - jnp.max(s, axis=-1, keepdims=True))
    return e, 1.0 / jnp.sum(e, axis=-1, keepdims=True)


_MLA_SCALE = (MLA_NOPE + MLA_ROPE) ** -0.5


def _attn_specs(t):
    head = pl.BlockSpec((1, t, LANES), lambda b, h: (b, 0, h))
    shared = pl.BlockSpec((1, t, LANES), lambda b, h: (b, 0, 0))
    return head, shared


def _attn_fwd(q, kv, kpe, name):
    bsz, t, hl = q.shape
    head, shared = _attn_specs(t)

    def body(q_ref, kv_ref, kpe_ref, o_ref, k_s, v_s):
        lane = lax.broadcasted_iota(jnp.int32, (t, LANES), 1)
        kvh = kv_ref[0]
        k_s[...] = jnp.where(lane < MLA_NOPE, kvh, kpe_ref[0]).astype(BF16)
        v_s[...] = kvh.astype(BF16)
        for start, rows in _query_blocks(t):
            n = start + rows
            e, inv_l = _attn_exp(q_ref[0, start:n, :], k_s[:n, :], start, _MLA_SCALE)
            o_ref[0, start:n, :] = _dot(e, v_s[:n, :], "nn") * inv_l

    return pl.pallas_call(
        body,
        name=name,
        grid=(bsz, hl // LANES),
        in_specs=[head, head, shared],
        out_specs=head,
        out_shape=jax.ShapeDtypeStruct(q.shape, F32),
        scratch_shapes=[pltpu.VMEM((t, LANES), BF16), pltpu.VMEM((t, LANES), BF16)],
        compiler_params=_params("parallel", "parallel"),
    )(q, kv, kpe)


def _attn_bwd(q, kv, kpe, do, name):
    bsz, t, hl = q.shape
    head, shared = _attn_specs(t)

    def body(q_ref, kv_ref, kpe_ref, do_ref, dq_ref, dkv_ref, dkpe_ref, k_s, v_s, dk_s, dv_s):
        lane = lax.broadcasted_iota(jnp.int32, (t, LANES), 1)
        kvh = kv_ref[0]
        k_s[...] = jnp.where(lane < MLA_NOPE, kvh, kpe_ref[0]).astype(BF16)
        v_s[...] = kvh.astype(BF16)
        for start, rows in reversed(_query_blocks(t, tall=1)):
            n = start + rows
            qb = q_ref[0, start:n, :]
            dob = jnp.where(lane[:rows] >= MLA_NOPE, do_ref[0, start:n, :], 0.0)
            kk, vv = k_s[:n, :], v_s[:n, :]
            e, inv_l = _attn_exp(qb, kk, start, _MLA_SCALE)
            p = e * inv_l
            dp = _dot(dob, vv, "nt")
            ds = p * (dp - jnp.sum(dp * p, axis=-1, keepdims=True)) * _MLA_SCALE
            dq_ref[0, start:n, :] = _dot(ds, kk, "nn")
            if n == t:
                dk_s[...] = _dot(ds, qb, "tn")
                dv_s[...] = _dot(p, dob, "tn")
            else:
                dk_s[:n, :] += _dot(ds, qb, "tn")
                dv_s[:n, :] += _dot(p, dob, "tn")
        dk = dk_s[...]
        dkv_ref[0] = jnp.where(lane < MLA_NOPE, dk, dv_s[...])
        dkpe = jnp.where(lane >= MLA_NOPE, dk, 0.0)

        @pl.when(pl.program_id(1) == 0)
        def _():
            dkpe_ref[0] = dkpe

        @pl.when(pl.program_id(1) != 0)
        def _():
            dkpe_ref[0] += dkpe

    return pl.pallas_call(
        body,
        name=name,
        grid=(bsz, hl // LANES),
        in_specs=[head, head, shared, head],
        out_specs=[head, head, shared],
        out_shape=[
            jax.ShapeDtypeStruct(q.shape, F32),
            jax.ShapeDtypeStruct(kv.shape, F32),
            jax.ShapeDtypeStruct(kpe.shape, F32),
        ],
        scratch_shapes=[pltpu.VMEM((t, LANES), BF16), pltpu.VMEM((t, LANES), BF16),
                        pltpu.VMEM((t, LANES), F32), pltpu.VMEM((t, LANES), F32)],
        compiler_params=_params("parallel", "arbitrary"),
    )(q, kv, kpe, do)


def _make_attention(name):
    @jax.custom_vjp
    def op(q, kv, kpe):
        return _attn_fwd(q, kv, kpe, name + "_fwd")

    def fwd(q, kv, kpe):
        return op(q, kv, kpe), (q, kv, kpe)

    def bwd(res, do):
        return tuple(_attn_bwd(*res, do, name + "_bwd"))

    op.defvjp(fwd, bwd)
    return op


_ROPE_SHIFT = MLA_ROPE // 2


def _rope_lanes_call(x, c, sm, sp, transpose, name):
    r, width = x.shape
    tm = _row_tile(r, [width, width, LANES, LANES, LANES])

    def body(x_ref, c_ref, sm_ref, sp_ref, y_ref):
        cv, smv, spv = c_ref[...], sm_ref[...], sp_ref[...]
        for b in range(width // LANES):
            xb = x_ref[:, b * LANES:(b + 1) * LANES]
            if transpose:
                yb = xb * cv + pltpu.roll(xb * smv, _ROPE_SHIFT, 1) + pltpu.roll(xb * spv, LANES - _ROPE_SHIFT, 1)
            else:
                yb = xb * cv + pltpu.roll(xb, LANES - _ROPE_SHIFT, 1) * smv + pltpu.roll(xb, _ROPE_SHIFT, 1) * spv
            y_ref[:, b * LANES:(b + 1) * LANES] = yb

    tab = pl.BlockSpec((tm, LANES), lambda i: (i, 0))
    blk = pl.BlockSpec((tm, width), lambda i: (i, 0))
    return pl.pallas_call(
        body,
        name=name,
        grid=(r // tm,),
        in_specs=[blk, tab, tab, tab],
        out_specs=blk,
        out_shape=jax.ShapeDtypeStruct(x.shape, F32),
        compiler_params=_params("parallel"),
    )(x, c, sm, sp)


def _make_rope_lanes(name):
    @jax.custom_vjp
    def op(x, c, sm, sp):
        return _rope_lanes_call(x, c, sm, sp, False, name + "_fwd")

    def fwd(x, c, sm, sp):
        return op(x, c, sm, sp), (c, sm, sp)

    def bwd(res, dy):
        c, sm, sp = res
        return _rope_lanes_call(dy, c, sm, sp, True, name + "_bwd"), jnp.zeros_like(c), jnp.zeros_like(sm), jnp.zeros_like(sp)

    op.defvjp(fwd, bwd)
    return op


def _ret_chunk_rows(t):
    return t // 4 if t % 32 == 0 else SEQ_BLOCK


def _ret_decays(c, log_gamma):
    row = lax.broadcasted_iota(jnp.int32, (c, 1), 0)
    col = lax.broadcasted_iota(jnp.int32, (1, c), 1)
    rowf = row.astype(F32)
    d = jnp.where(row >= col, jnp.exp(log_gamma * rowf) * jnp.exp(-log_gamma * col.astype(F32)), 0.0)
    return d, jnp.exp(log_gamma * (rowf + 1.0)), jnp.exp(log_gamma * (c - 1.0 - rowf)), jnp.exp(log_gamma * c)


def _ret_specs(c, dk, dv, v_block0, n_chunks, reverse):
    pos = (lambda i: n_chunks - 1 - i) if reverse else (lambda i: i)
    return (
        pl.BlockSpec(memory_space=pltpu.SMEM),
        pl.BlockSpec((1, c, dk), lambda b, h, i: (b, pos(i), h)),
        pl.BlockSpec((1, c, dv), lambda b, h, i: (b, pos(i), h + v_block0)),
        pl.BlockSpec((1, c, dv), lambda b, h, i: (b, pos(i), h)),
        pl.BlockSpec((1, 1, dk, dv), lambda b, h, i: (b, h * n_chunks + pos(i), 0, 0)),
    )


def _ret_fwd(lg, q, k, v, name, dv=None, v_block0=0):
    bsz, t, hdk = q.shape
    heads = lg.shape[0]
    dk, dv = hdk // heads, dv or v.shape[2] // heads
    c = _ret_chunk_rows(t)
    n_chunks = t // c
    lg_spec, qk_spec, v_spec, o_spec, s_spec = _ret_specs(c, dk, dv, v_block0, n_chunks, False)

    def body(lg_ref, q_ref, k_ref, v_ref, o_ref, s_ref, state):
        @pl.when(pl.program_id(2) == 0)
        def _():
            state[...] = jnp.zeros((dk, dv), F32)

        d, a, b, g = _ret_decays(c, lg_ref[pl.program_id(1)])
        qb, kb, vb, s_in = q_ref[0], k_ref[0], v_ref[0], state[...]
        s_ref[0, 0] = s_in
        o_ref[0] = _dot(_dot(qb, kb, "nt") * d, vb, "nn") + a * _dot(qb, s_in, "nn")
        state[...] = g * s_in + _dot(kb * b, vb, "tn")

    return pl.pallas_call(
        body,
        name=name,
        grid=(bsz, heads, n_chunks),
        in_specs=[lg_spec, qk_spec, qk_spec, v_spec],
        out_specs=[o_spec, s_spec],
        out_shape=[jax.ShapeDtypeStruct((bsz, t, heads * dv), F32),
                   jax.ShapeDtypeStruct((bsz, heads * n_chunks, dk, dv), F32)],
        scratch_shapes=[pltpu.VMEM((dk, dv), F32)],
        compiler_params=_params("parallel", "parallel", "arbitrary"),
    )(lg, q, k, v)


def _ret_bwd(lg, q, k, v, states, do, name, v_block0=0, dv_dtype=F32):
    bsz, t, hdk = q.shape
    heads = lg.shape[0]
    dk, dv = hdk // heads, do.shape[2] // heads
    c = _ret_chunk_rows(t)
    n_chunks = t // c
    lg_spec, qk_spec, v_spec, o_spec, s_spec = _ret_specs(c, dk, dv, v_block0, n_chunks, True)

    def body(lg_ref, q_ref, k_ref, v_ref, s_ref, do_ref, dq_ref, dk_ref, dv_ref, dstate):
        @pl.when(pl.program_id(2) == 0)
        def _():
            dstate[...] = jnp.zeros((dk, dv), F32)

        d, a, b, g = _ret_decays(c, lg_ref[pl.program_id(1)])
        qb, kb, vb, dob, s_in, ds_out = q_ref[0], k_ref[0], v_ref[0], do_ref[0], s_ref[0, 0], dstate[...]
        scores = _dot(qb, kb, "nt") * d
        dscores = _dot(dob, vb, "nt") * d
        dq_ref[0] = _dot(dscores, kb, "nn") + a * _dot(dob, s_in, "nt")
        dk_ref[0] = _dot(dscores, qb, "tn") + b * _dot(vb, ds_out, "nt")
        dv_ref[0] = (_dot(scores, dob, "tn") + _dot(kb * b, ds_out, "nn")).astype(dv_dtype)
        dstate[...] = g * ds_out + _dot(qb, a * dob, "tn")

    return pl.pallas_call(
        body,
        name=name,
        grid=(bsz, heads, n_chunks),
        in_specs=[lg_spec, qk_spec, qk_spec, v_spec, s_spec, o_spec],
        out_specs=[qk_spec, qk_spec, o_spec],
        out_shape=[
            jax.ShapeDtypeStruct(q.shape, F32),
            jax.ShapeDtypeStruct(k.shape, F32),
            jax.ShapeDtypeStruct(do.shape, dv_dtype),
        ],
        scratch_shapes=[pltpu.VMEM((dk, dv), F32)],
        compiler_params=_params("parallel", "parallel", "arbitrary"),
    )(lg, q, k, v, states, do)


def _adamw(w, g, m, v, name):
    r, c = w.shape
    tr = _pick(r, (256, 128, 64, 32, 16, 8))

    def body(w_ref, g_ref, m_ref, v_ref, d_ref, nm_ref, nv_ref):
        gv = g_ref[...]
        nm = ADAM_B1 * m_ref[...] + (1.0 - ADAM_B1) * gv
        nv = ADAM_B2 * v_ref[...] + (1.0 - ADAM_B2) * (gv * gv)
        m_hat = nm / (1.0 - ADAM_B1 ** ADAM_STEP)
        v_hat = nv / (1.0 - ADAM_B2 ** ADAM_STEP)
        d_ref[...] = -ADAM_LR * (m_hat / (jnp.sqrt(v_hat) + ADAM_EPS) + ADAM_WD * w_ref[...])
        nm_ref[...] = nm
        nv_ref[...] = nv

    spec = pl.BlockSpec((tr, c), lambda i: (i, 0))
    return pl.pallas_call(
        body,
        name=name,
        grid=(r // tr,),
        in_specs=[spec] * 4,
        out_specs=[spec] * 3,
        out_shape=[jax.ShapeDtypeStruct((r, c), F32)] * 3,
        compiler_params=_params("parallel"),
    )(w, g, m, v)


def _rope_tables(t, half, reps):
    inv = ROPE_BASE ** (-jnp.arange(half, dtype=F32) / half)
    ang = jnp.arange(t, dtype=jnp.int32).astype(F32)[:, None] * inv[None, :]
    return jnp.tile(jnp.cos(ang), (1, reps)), jnp.tile(jnp.sin(ang), (1, reps))


def _padded_len(seq):
    return -(-(N_META + seq) // SEQ_BLOCK) * SEQ_BLOCK


def _embed(meta, x):
    bsz, seq, d = x.shape
    t = _padded_len(seq)
    return jnp.concatenate(
        [jnp.broadcast_to(meta[None], (bsz, N_META, d)), x, jnp.zeros((bsz, t - N_META - seq, d), F32)], axis=1
    ).reshape(bsz * t, d)


def _even_mixer(p, conv_w, conv_b, w_rg_a, b_rg_a, w_rg_x, b_rg_x, lru_lambda, q_norm_g, uq_slot, kv_norm_g,
                ukv_slot, gathered, bsz):
    w_uq_pad, w_ukv_full = gathered
    r = p.shape[0]
    t = r // bsz

    def tile_rows(tab):
        return jnp.tile(tab, (bsz, 1))

    lru_w = w_rg_a.shape[2] * w_rg_a.shape[1]
    q_rank, kv_rank = q_norm_g.shape[1], kv_norm_g.shape[1]
    p_gate, p_rec, p_q, p_kv, p_kpe = _split_cols(
        p, (lru_w, 2 * lru_w, 2 * lru_w + q_rank, 2 * lru_w + q_rank + kv_rank))

    xc = _make_conv("conv")(p_rec.reshape(bsz, t, lru_w), conv_w, conv_b).reshape(r, lru_w)
    ga, gx = _make_gate_pair("rg")(xc, w_rg_a[0], w_rg_x[0])
    (a, bb), _ = _make_rowwise(_lru_gates_f, "lru_gates", 3, 0, 3)((ga, gx, xc), (), (b_rg_a, b_rg_x, lru_lambda))
    hh = _make_scan("lru_scan")(a.reshape(bsz, t, lru_w), bb.reshape(bsz, t, lru_w)).reshape(r, lru_w)
    (y_rec,), _ = _make_rowwise(_lru_out_f, "lru_out", 2, 0, 0)((hh, p_gate), (), ())

    (qn,), _ = _make_rowwise(_rmsnorm_f, "q_norm", 1, 0, 1)((p_q,), (), (q_norm_g,))
    (kvn,), _ = _make_rowwise(_rmsnorm_f, "kv_norm", 1, 0, 1)((p_kv,), (), (kv_norm_g,))
    q = _make_slot_linear("ev_uq")(qn, w_uq_pad, uq_slot)
    kv = _make_slot_linear("ev_ukv")(kvn, w_ukv_full, ukv_slot)
    half = MLA_ROPE // 2
    cos, sin = _rope_tables(t, half, 1)
    one, zero = jnp.ones((t, MLA_NOPE), F32), jnp.zeros((t, MLA_NOPE), F32)
    tail = LANES - MLA_NOPE - MLA_ROPE
    c_tab = tile_rows(jnp.concatenate([one, cos, cos, one[:, :tail]], axis=1))
    sm_tab = tile_rows(jnp.concatenate([zero, -sin, zero[:, :half + tail]], axis=1))
    sp_tab = tile_rows(jnp.concatenate([zero, zero[:, :half], sin, zero[:, :tail]], axis=1))
    q = _make_rope_lanes("rope_q")(q, c_tab, sm_tab, sp_tab)
    kpe = _make_rope_lanes("rope_k")(p_kpe, c_tab, sm_tab, sp_tab)
    o = _make_attention("mla")(q.reshape(bsz, t, -1), kv.reshape(bsz, t, -1), kpe.reshape(bsz, t, LANES))
    return jnp.concatenate([y_rec, o.reshape(r, -1)], axis=1)


def _odd_mixer_fwd(p, bsz):
    r, width = p.shape
    t = r // bsz
    qk = width // 6
    dk = qk // RET_HEADS
    cos2, sin2 = _rope_tables(t, dk // 2, 2)
    sin2 = jnp.concatenate([-sin2[:, :dk // 2], sin2[:, dk // 2:]], axis=1)
    rope_args = (_Cols(p, qk, 0), _Cols(p, qk, 1), jnp.tile(cos2, (bsz, 1)), jnp.tile(sin2, (bsz, 1)))
    (rq, rk), _ = _make_rowwise(_rope_ret_f, "rope_ret", 2, 2, 0).fwd_call(*rope_args)
    lg = jnp.log(1.0 - 2.0 ** (-5.0 - jnp.arange(RET_HEADS, dtype=F32)))
    ret_args = (lg, rq.reshape(bsz, t, qk), rk.reshape(bsz, t, qk), p.reshape(bsz, t, width))
    o, states = _ret_fwd(*ret_args, "ret_fwd", dv=2 * dk, v_block0=qk // dk)
    gate_args = (o.reshape(r, 2 * qk), _Cols(p, 2 * qk, 2))
    (y,), _ = _make_rowwise(_ret_out_bf16_f, "ret_out", 2, 0, 0).fwd_call(*gate_args)
    return y, (rope_args, ret_args + (states,), gate_args)


def _odd_mixer_bwd(res, dy):
    rope_args, ret_args, gate_args = res
    bsz, t, qk = ret_args[1].shape
    dk = qk // RET_HEADS
    (do, dg), _ = _make_rowwise(_ret_out_f, "ret_out", 2, 0, 0).bwd_call(
        gate_args, ((dy,), ()), row_dtypes=(F32, BF16))
    drq, drk, dv = _ret_bwd(*ret_args, do.reshape(bsz, t, 2 * qk), "ret_bwd", v_block0=qk // dk, dv_dtype=BF16)
    (dq, dkk), _ = _make_rowwise(_rope_ret_f, "rope_ret", 2, 2, 0).bwd_call(
        rope_args, ((drq.reshape(bsz * t, qk), drk.reshape(bsz * t, qk)), ()), row_dtypes=(BF16, BF16))
    return jnp.concatenate([dq, dkk, dv.reshape(bsz * t, 2 * qk), dg], axis=1)


def _local_loss(h, target):
    bsz, seq, d = target.shape
    t = _padded_len(seq)
    t_real = N_META + seq
    pos = jnp.arange(t, dtype=jnp.int32)
    mask = jnp.tile(((pos >= N_META) & (pos < t_real)).astype(F32)[:, None], (bsz, 1))
    tgt = jnp.concatenate(
        [jnp.zeros((bsz, N_META, d), F32), target, jnp.zeros((bsz, t - t_real, d), F32)], axis=1).reshape(bsz * t, d)
    _, (total,) = _make_rowwise(_loss_f, "loss", 1, 2, 0)((h,), (tgt, mask), ())
    return total[0, 0]


_WEIGHTS = ("meta_tokens", "ev_w_in", "ev_conv_w", "ev_conv_b", "ev_w_rg_a", "ev_b_rg_a", "ev_w_rg_x", "ev_b_rg_x",
            "ev_lru_lambda", "ev_q_norm_g", "ev_w_uq", "ev_kv_norm_g", "ev_w_ukv", "ev_w_out", "od_w_in", "od_w_out",
            "ln_mix_g", "ln_mix_b", "mlp_w1", "mlp_w2", "ln_mlp_g", "ln_mlp_b")


def kernel(x, meta_tokens, ev_w_in, ev_conv_w, ev_conv_b, ev_w_rg_a, ev_b_rg_a, ev_w_rg_x, ev_b_rg_x, ev_lru_lambda, ev_q_norm_g, ev_w_uq, ev_kv_norm_g, ev_w_ukv, ev_w_out, od_w_in, od_w_out, ln_mix_g, ln_mix_b, mlp_w1, mlp_w2, ln_mlp_g, ln_mlp_b, loss_target, m_meta_tokens, m_ev_w_in, m_ev_conv_w, m_ev_conv_b, m_ev_w_rg_a, m_ev_b_rg_a, m_ev_w_rg_x, m_ev_b_rg_x, m_ev_lru_lambda, m_ev_q_norm_g, m_ev_w_uq, m_ev_kv_norm_g, m_ev_w_ukv, m_ev_w_out, m_od_w_in, m_od_w_out, m_ln_mix_g, m_ln_mix_b, m_mlp_w1, m_mlp_w2, m_ln_mlp_g, m_ln_mlp_b, v_meta_tokens, v_ev_w_in, v_ev_conv_w, v_ev_conv_b, v_ev_w_rg_a, v_ev_b_rg_a, v_ev_w_rg_x, v_ev_b_rg_x, v_ev_lru_lambda, v_ev_q_norm_g, v_ev_w_uq, v_ev_kv_norm_g, v_ev_w_ukv, v_ev_w_out, v_od_w_in, v_od_w_out, v_ln_mix_g, v_ln_mix_b, v_mlp_w1, v_mlp_w2, v_ln_mlp_g, v_ln_mlp_b):
    args = locals()
    weights = {n: args[n] for n in _WEIGHTS}
    bsz = x.shape[0]
    my_x, my_y, my_c = _my_place()
    me = 4 * my_x + 2 * my_y + my_c

    big = (("ev_in", ev_w_in[0], True), ("ev_out", ev_w_out[0], False), ("mlp0_w1", mlp_w1[0], True),
           ("mlp0_w2", mlp_w2[0], False), ("od_in", od_w_in[0], True), ("od_out", od_w_out[0], False),
           ("mlp1_w1", mlp_w1[1], True), ("mlp1_w2", mlp_w2[1], False))
    small_sharded = (("meta", meta_tokens, F32), ("conv_w", ev_conv_w[0], F32), ("ev_uq", ev_w_uq[0], BF16),
                     ("ev_ukv", ev_w_ukv[0], BF16))
    to_gather = (tuple((nm, s.astype(dt), True) for nm, s, dt in small_sharded)
                 + tuple((nm, s.astype(BF16), cols) for nm, s, cols in big))
    handles = _gather_start_all([s for _, s, _ in to_gather], "ag_start")
    gathers = {nm: (s, cols, h) for (nm, s, cols), h in zip(to_gather, handles)}
    gather_tokens = (handles[0][4],)

    def full_weight(nm, after):
        shard16, cols, handle = gathers[nm]
        land = _exchange_wait(handle, True, after, "ag_wait_" + nm)
        land = lax.dynamic_update_index_in_dim(land, shard16, me, 0)
        if cols and shard16.shape[1] % LANES == 0:
            return land, True
        return (_unstack_cols(land) if cols else land.reshape(-1, shard16.shape[1])), False

    meta_full, conv_w_full, w_uq_full, w_ukv_full = (
        _unstack_cols(lax.dynamic_update_index_in_dim(
            _exchange_wait(gathers[nm][2], True, gather_tokens[-1], "ag_wait_" + nm), gathers[nm][0], me, 0))
        for nm, _, _ in small_sharded)

    pending = []

    def linear_bwd(nm, x_in, w, dy, cols, unpad=None, **fused):
        w_full, w_stacked = w
        own = None
        if w_stacked:
            stacked = _matmul(x_in, dy, "tn", nm + "_dw", stacked=True)
        else:
            dw = _matmul(x_in, dy, "tn", nm + "_dw")
            dw = dw if unpad is None else unpad(dw)
            n = dw.shape[1] // N_DEV
            if cols:
                stacked = _stack_cols(dw)
                own = lax.dynamic_slice_in_dim(dw, me * n, n, axis=1)
            else:
                stacked = dw.reshape(N_DEV, dw.shape[0] // N_DEV, dw.shape[1])
        handle = _exchange_start(stacked, (_N_PEERS,) + stacked.shape[1:], False, "rs_start_" + nm)
        if own is None:
            own = lax.dynamic_index_in_dim(handle[2], me, 0, keepdims=False)
        pending.append((nm, own, handle))
        return _matmul(dy, w_full, "nt", nm + "_dx", after=(handle[4],), stacked=w_stacked, **fused)

    def linear_fwd(nm, x_in, w, **fused):
        return _matmul(x_in, w[0], "nn", nm + "_fwd", stacked=w[1], **fused)

    def mlp_fwd(h, h16, l):
        w1 = full_weight(f"mlp{l}_w1", h16)
        a16 = linear_fwd(f"mlp{l}_w1", h16, w1, out_relu2=True, out_dtype=BF16)
        w2 = full_weight(f"mlp{l}_w2", a16)
        f = linear_fwd(f"mlp{l}_w2", a16, w2)
        ln_args = (h, f, ln_mlp_g[l:l + 1], ln_mlp_b[l:l + 1])
        return ln_fwd(f"mlp{l}_ln", *ln_args), (h16, w1, a16, w2, ln_args)

    def mlp_bwd(dout, res, l):
        h16, w1, a16, w2, ln_args = res
        dh, df, dg, db = ln_bwd(f"mlp{l}_ln", ln_args, dout)
        du = linear_bwd(f"mlp{l}_w2", a16, w2, df, False, relu2_bwd_of=a16, out_dtype=BF16)
        return (dh, linear_bwd(f"mlp{l}_w1", h16, w1, du, True)), dg, db

    def ln_fwd(nm, h, mix, g, b):
        return _make_rowwise(_ln_res_copy_f, nm, 2, 0, 2).fwd_call(h, mix, g, b)[0]

    def ln_bwd(nm, ln_args, pieces):
        (dh, dmix), (dg, db) = _make_rowwise(_ln_res_f, nm, 2, 0, 2).bwd_call(
            ln_args, ((pieces[0],), ()), more=tuple(pieces[1:]), row_dtypes=(F32, BF16))
        return dh, dmix, dg, db

    h0, vjp_embed = jax.vjp(_embed, meta_full, x)
    n_in = ev_w_in.shape[2] * N_DEV
    kpe0, pad_lo, pad_hi = n_in - MLA_ROPE, MLA_NOPE, LANES - MLA_NOPE - MLA_ROPE
    w_in = full_weight("ev_in", h0)[0]
    zeros_in = jnp.zeros((w_in.shape[0], pad_lo), BF16)
    w_ev_in = (jnp.concatenate([w_in[:, :kpe0], zeros_in, w_in[:, kpe0:], zeros_in[:, :pad_hi]], axis=1), False)

    def unpad_in(dw):
        return jnp.concatenate([dw[:, :kpe0], dw[:, kpe0 + pad_lo:kpe0 + pad_lo + MLA_ROPE]], axis=1)

    p0 = _matmul(h0, w_ev_in[0], "nn", "ev_in_fwd", after=gather_tokens)
    q_rank, d_head = w_uq_full.shape[0], MLA_NOPE + MLA_ROPE
    w_uq_pad = jnp.pad(w_uq_full.reshape(q_rank, MLA_HEADS, d_head), ((0, 0), (0, 0), (0, LANES - d_head)))
    w_uq_pad = w_uq_pad.reshape(q_rank, MLA_HEADS * LANES)
    small = (conv_w_full, ev_conv_b, ev_w_rg_a, ev_b_rg_a, ev_w_rg_x, ev_b_rg_x, ev_lru_lambda, ev_q_norm_g,
             jnp.zeros(w_uq_pad.shape, F32), ev_kv_norm_g, jnp.zeros(w_ukv_full.shape, F32))
    y0, vjp_even = jax.vjp(lambda p, *s: _even_mixer(p, *s, (w_uq_pad, w_ukv_full), bsz), p0, *small)
    w_out = full_weight("ev_out", y0)[0]
    lru_w, d_model = y0.shape[1] - MLA_HEADS * LANES, w_out.shape[1]
    w_att = w_out[lru_w:].reshape(MLA_HEADS, MLA_V, d_model)
    w_att = jnp.concatenate([jnp.zeros((MLA_HEADS, LANES - MLA_V, d_model), BF16), w_att], axis=1)
    w_ev_out = (jnp.concatenate([w_out[:lru_w], w_att.reshape(MLA_HEADS * LANES, d_model)], axis=0), False)

    def unpad_out(dw):
        d_att = dw[lru_w:].reshape(MLA_HEADS, LANES, d_model)[:, LANES - MLA_V:].reshape(MLA_HEADS * MLA_V, d_model)
        return jnp.concatenate([dw[:lru_w], d_att], axis=0)

    mix0 = linear_fwd("ev_out", y0, w_ev_out)
    ln0_args = (h0, mix0, ln_mix_g[0:1], ln_mix_b[0:1])
    h1, h1_16 = ln_fwd("mix0_ln", *ln0_args)
    (h2, h2_16), res_mlp0 = mlp_fwd(h1, h1_16, 0)
    w_od_in = full_weight("od_in", h2_16)
    p1 = linear_fwd("od_in", h2_16, w_od_in)
    y1, res_odd = _odd_mixer_fwd(p1, bsz)
    w_od_out = full_weight("od_out", y1)
    mix1 = linear_fwd("od_out", y1, w_od_out)
    ln1_args = (h2, mix1, ln_mix_g[1:2], ln_mix_b[1:2])
    h3, h3_16 = ln_fwd("mix1_ln", *ln1_args)
    (h4, _), res_mlp1 = mlp_fwd(h3, h3_16, 1)
    loss_local, vjp_loss = jax.vjp(lambda h: _local_loss(h, loss_target), h4)

    dh4 = vjp_loss(jnp.ones((), F32))
    dh3, dg_mlp1, db_mlp1 = mlp_bwd(dh4, res_mlp1, 1)
    dh2, dmix1, dg_mix1, db_mix1 = ln_bwd("mix1_ln", ln1_args, dh3)
    dp1 = _odd_mixer_bwd(res_odd, linear_bwd("od_out", y1, w_od_out, dmix1, False))
    dh2 = (dh2, linear_bwd("od_in", h2_16, w_od_in, dp1, True))
    dh1, dg_mlp0, db_mlp0 = mlp_bwd(dh2, res_mlp0, 0)
    dh0, dmix0, dg_mix0, db_mix0 = ln_bwd("mix0_ln", ln0_args, dh1)
    dp0, *dsmall = vjp_even(linear_bwd("ev_out", y0, w_ev_out, dmix0, False, unpad=unpad_out))
    dh0 = dh0 + linear_bwd("ev_in", h0, w_ev_in, dp0.astype(BF16), True, unpad=unpad_in)
    g_meta_full, grad_x = vjp_embed(dh0)
    (g_conv_w_full, g_conv_b, g_w_rg_a, g_b_rg_a, g_w_rg_x, g_b_rg_x, g_lambda, g_q_norm, g_uq_pad, g_kv_norm,
     g_ukv_full) = dsmall

    for nm, dw in (("ev_uq", g_uq_pad.reshape(q_rank, MLA_HEADS, LANES)[:, :, :d_head].reshape(q_rank, -1)),
                   ("ev_ukv", g_ukv_full)):
        n = dw.shape[1] // N_DEV
        handle = _exchange_start(_stack_cols(dw), (_N_PEERS, dw.shape[0], n), False, "rs_start_" + nm)
        pending.append((nm, lax.dynamic_slice_in_dim(dw, me * n, n, axis=1), handle))

    rep_names = ("ev_conv_b", "ev_w_rg_a", "ev_b_rg_a", "ev_w_rg_x", "ev_b_rg_x", "ev_lru_lambda", "ev_q_norm_g",
                 "ev_kv_norm_g", "ln_mix_g", "ln_mix_b", "ln_mlp_g", "ln_mlp_b")
    rep_local = (g_conv_b, g_w_rg_a, g_b_rg_a, g_w_rg_x, g_b_rg_x, g_lambda, g_q_norm, g_kv_norm,
                 jnp.concatenate([dg_mix0, dg_mix1]), jnp.concatenate([db_mix0, db_mix1]),
                 jnp.concatenate([dg_mlp0, dg_mlp1]), jnp.concatenate([db_mlp0, db_mlp1]), g_conv_w_full, g_meta_full)
    rep_stacked = _pack_rows(rep_local).reshape(N_DEV, -1, LANES)
    rep_rs = _exchange_start(rep_stacked, (_N_PEERS,) + rep_stacked.shape[1:], False, "rep_rs_start", after=(grad_x,))
    rep_own = lax.dynamic_index_in_dim(rep_rs[2], me, 0, keepdims=False)

    after, summed = rep_rs[4], {}
    for nm, own, handle in pending:
        land = _exchange_wait(handle, False, after, "rs_wait_" + nm)
        summed[nm] = after = _sum_own_and_peers(own, land, "rs_sum_" + nm)
    rep_part = _sum_own_and_peers(rep_own, _exchange_wait(rep_rs, False, after, "rep_rs_wait"), "rep_sum")
    rep_ag = _exchange_start(rep_part, (N_DEV,) + rep_part.shape, True, "rep_ag_start")

    grad_w = dict(ev_w_uq=summed["ev_uq"][None], ev_w_ukv=summed["ev_ukv"][None],
                  ev_w_in=summed["ev_in"][None], ev_w_out=summed["ev_out"][None], od_w_in=summed["od_in"][None],
                  od_w_out=summed["od_out"][None], mlp_w1=jnp.stack([summed["mlp0_w1"], summed["mlp1_w1"]]),
                  mlp_w2=jnp.stack([summed["mlp0_w2"], summed["mlp1_w2"]]))

    loss = lax.psum(loss_local, MESH_AXES)
    delta, new_m, new_v = {}, {}, {}

    def adamw(n):
        w, g, m, v = weights[n], grad_w[n], args["m_" + n], args["v_" + n]
        two_d = (-1, w.shape[-1])
        d2, m2, v2 = _adamw(w.reshape(two_d), g.reshape(two_d), m.reshape(two_d), v.reshape(two_d), "adamw_" + n)
        delta[n], new_m[n], new_v[n] = d2.reshape(w.shape), m2.reshape(w.shape), v2.reshape(w.shape)
        return d2

    for n in tuple(grad_w):
        after = adamw(n)
    rep_land = lax.dynamic_update_index_in_dim(_exchange_wait(rep_ag, True, after, "rep_ag_wait"), rep_part, me, 0)
    *rep_total, t_conv_w, t_meta = _unpack_rows(rep_land, rep_local)
    n_conv, n_meta = ev_conv_w.shape[2], meta_tokens.shape[1]
    small_g = dict(zip(rep_names, rep_total))
    small_g.update(meta_tokens=lax.dynamic_slice_in_dim(t_meta, me * n_meta, n_meta, axis=1),
                   ev_conv_w=lax.dynamic_slice_in_dim(t_conv_w, me * n_conv, n_conv, axis=1)[None])
    grad_w.update(small_g)
    for n in small_g:
        adamw(n)
    return (loss, grad_x, *[grad_w[n] for n in _WEIGHTS], *[delta[n] for n in _WEIGHTS],
            *[new_m[n] for n in _WEIGHTS], *[new_v[n] for n in _WEIGHTS])
```
